```python
import jax, jax.numpy as jnp
from jax import lax
import numpy as np

D_MODEL = 2048
BATCH = 8
SEQ = 2048
DEPTH = 1

CHUNK = 64
H_A = 8
DK_A = 128
DV_A = 128
D_A = H_A * DK_A
H_B = 8
DH_B = 128
D_B = H_B * DH_B
N_PAST_CHUNKS = 8
BAND = N_PAST_CHUNKS + 1
REL_FUTURE = CHUNK - 1
REL_PAST = 2 * CHUNK - 1
N_REL = REL_FUTURE + REL_PAST + 1
D_FF = -(-8 * D_MODEL // (3 * 256)) * 256
N_IN = 4 * D_A + 3 * D_B + 2 * D_MODEL
EPS = 1e-6

kernel_name = "hybrid_hgrn2_chunkattn_gated_block"


def rms_norm(x, gain):
    xf = x.astype(jnp.float32)
    y = xf * lax.rsqrt(jnp.mean(xf * xf, axis=-1, keepdims=True) + EPS)
    return (y * gain.astype(jnp.float32)).astype(x.dtype)


def hgrn_lower_bounds(lb_logits):
    p = jax.nn.softmax(lb_logits.astype(jnp.float32), axis=0)
    return jnp.cumsum(p, axis=0)[:DEPTH]


def hgrn2_mixer(q, f_logit, i, g, lb, out_gain):
    B, T, _ = q.shape
    n_chunks = T // CHUNK
    f32 = jnp.float32
    lbf = lb.astype(f32)
    f = lbf + (1.0 - lbf) * jax.nn.sigmoid(f_logit.astype(f32))
    log_f = jnp.log(f)
    k = 1.0 - f
    qs = jax.nn.silu(q.astype(f32))

    def to_chunks(t, dh):
        return t.reshape(B, n_chunks, CHUNK, H_A, dh).transpose(1, 0, 3, 2, 4)

    qc, kc, lfc = to_chunks(qs, DK_A), to_chunks(k, DK_A), to_chunks(log_f, DK_A)
    vc = to_chunks(i.astype(f32), DV_A)
    causal = jnp.tril(jnp.ones((CHUNK, CHUNK), dtype=bool))[:, :, None]

    def step(S, inp):
        qj, kj, lfj, vj = inp
        b = jnp.cumsum(lfj, axis=2)
        o_inter = jnp.einsum('bhtk,bhkv->bhtv', qj * jnp.exp(b), S)
        rel = jnp.where(causal, b[:, :, :, None, :] - b[:, :, None, :, :], -jnp.inf)
        decay = jnp.exp(rel)
        scores = jnp.einsum('bhtk,bhsk,bhtsk->bhts', qj, kj, decay)
        o_intra = jnp.einsum('bhts,bhsv->bhtv', scores, vj)
        b_last = b[:, :, -1:, :]
        S_new = jnp.exp(b_last[:, :, 0, :, None]) * S + jnp.einsum(
            'bhsk,bhsv->bhkv', kj * jnp.exp(b_last - b), vj)
        return S_new, o_inter + o_intra

    S0 = jnp.zeros((B, H_A, DK_A, DV_A), f32)
    _, o = lax.scan(step, S0, (qc, kc, lfc, vc))
    o = o.transpose(1, 0, 3, 2, 4).reshape(B, T, H_A, DV_A)
    o = o * lax.rsqrt(jnp.mean(o * o, axis=-1, keepdims=True) + EPS)
    o = o.reshape(B, T, D_A) * out_gain.astype(f32)
    o = o * jax.nn.silu(g.astype(f32))
    return o.astype(q.dtype)


def head_rms_norm(t, gain):
    tf = t.astype(jnp.float32)
    y = tf * lax.rsqrt(jnp.mean(tf * tf, axis=-1, keepdims=True) + EPS)
    return y * gain.astype(jnp.float32)


def chunked_relpos_attention(q, k, v, q_gain, k_gain, rel_bias):
    B, T, _ = q.shape
    n_chunks = T // CHUNK

    def heads(t):
        return t.reshape(B, n_chunks, CHUNK, H_B, DH_B).transpose(0, 3, 1, 2, 4)

    qh = head_rms_norm(heads(q), q_gain)
    kh = head_rms_norm(heads(k), k_gain)
    vh = heads(v).astype(jnp.float32)

    pad = ((0, 0), (0, 0), (N_PAST_CHUNKS, 0), (0, 0), (0, 0))
    band_idx = jnp.arange(n_chunks)[:, None] + jnp.arange(BAND)[None, :]
    k_band = jnp.pad(kh, pad)[:, :, band_idx].reshape(B, H_B, n_chunks, BAND * CHUNK, DH_B)
    v_band = jnp.pad(vh, pad)[:, :, band_idx].reshape(B, H_B, n_chunks, BAND * CHUNK, DH_B)

    q_pos = jnp.arange(n_chunks)[:, None] * CHUNK + jnp.arange(CHUNK)[None, :]
    k_chunk = band_idx - N_PAST_CHUNKS
    k_pos = (k_chunk[:, :, None] * CHUNK + jnp.arange(CHUNK)[None, None, :]).reshape(
        n_chunks, BAND * CHUNK)
    valid = k_pos >= 0
    dist = q_pos[:, :, None] - k_pos[:, None, :]
    rel_idx = jnp.clip(dist, -REL_FUTURE, REL_PAST) + REL_FUTURE
    bias = rel_bias.astype(jnp.float32)[:, rel_idx]

    scale = DH_B ** -0.5
    scores = jnp.einsum('bhnqd,bhnkd->bhnqk', qh, k_band) * scale + bias[None]
    scores = jnp.where(valid[None, None, :, None, :], scores, -jnp.inf)
    p = jax.nn.softmax(scores, axis=-1)
    o = jnp.einsum('bhnqk,bhnkd->bhnqd', p, v_band)
    return o.transpose(0, 2, 3, 1, 4).reshape(B, T, D_B).astype(q.dtype)


def _fwd_setup_inputs(seed: int = 0) -> dict:
    key = jax.random.key(seed)
    ks = jax.random.split(key, 16)
    f32 = jnp.float32

    def nrm(k, shape, scale):
        return jax.random.normal(k, shape, f32) * scale

    return {
        "x": nrm(ks[0], (BATCH, SEQ, D_MODEL), 1.0),
        "w_in": nrm(ks[1], (DEPTH, D_MODEL, N_IN), D_MODEL ** -0.5),
        "b_gate": nrm(ks[2], (DEPTH, 2 * D_MODEL), 0.02),
        "norm_mix": 1.0 + nrm(ks[3], (DEPTH, D_MODEL), 0.02),
        "norm_ffn": 1.0 + nrm(ks[4], (DEPTH, D_MODEL), 0.02),
        "hgrn_lb_logits": nrm(ks[5], (DEPTH + 1, D_A), 0.5),
        "hgrn_out_gain": 1.0 + nrm(ks[6], (DEPTH, D_A), 0.02),
        "q_gain": 1.0 + nrm(ks[7], (DEPTH, DH_B), 0.02),
        "k_gain": 1.0 + nrm(ks[8], (DEPTH, DH_B), 0.02),
        "rel_bias": nrm(ks[9], (DEPTH, H_B, N_REL), 0.1),
        "w_proj_a": nrm(ks[10], (DEPTH, D_A, D_MODEL), D_A ** -0.5),
        "w_proj_b": nrm(ks[11], (DEPTH, D_B, D_MODEL), D_B ** -0.5),
        "w_out": nrm(ks[12], (DEPTH, D_MODEL, D_MODEL), D_MODEL ** -0.5),
        "w_ffn_in": nrm(ks[13], (DEPTH, D_MODEL, 2 * D_FF), D_MODEL ** -0.5),
        "w_ffn_out": nrm(ks[14], (DEPTH, D_FF, D_MODEL), D_FF ** -0.5),
    }


def _fwd_reference(x, w_in, b_gate, norm_mix, norm_ffn, hgrn_lb_logits, hgrn_out_gain,
              q_gain, k_gain, rel_bias, w_proj_a, w_proj_b, w_out, w_ffn_in, w_ffn_out):
    lower_bounds = hgrn_lower_bounds(hgrn_lb_logits)
    split_pts = [D_A, 2 * D_A, 3 * D_A, 4 * D_A,
                 4 * D_A + D_B, 4 * D_A + 2 * D_B, 4 * D_A + 3 * D_B]
    for l in range(DEPTH):
        h = rms_norm(x, norm_mix[l])
        proj = jnp.einsum('btd,dn->btn', h, w_in[l])
        q_a, f_a, i_a, g_a, q_b, k_b, v_b, gate_logits = jnp.split(proj, split_pts, axis=-1)
        gates = jax.nn.sigmoid((gate_logits + b_gate[l]).astype(jnp.float32)).astype(x.dtype)
        gate_a, gate_b = jnp.split(gates, 2, axis=-1)

        y_a = hgrn2_mixer(q_a, f_a, i_a, g_a, lower_bounds[l], hgrn_out_gain[l])
        y_b = chunked_relpos_attention(q_b, k_b, v_b, q_gain[l], k_gain[l], rel_bias[l])

        merged = (gate_a * jnp.einsum('btc,cd->btd', y_a, w_proj_a[l])
                  + gate_b * jnp.einsum('btc,cd->btd', y_b, w_proj_b[l]))
        x = x + jnp.einsum('btd,de->bte', merged, w_out[l])

        h = rms_norm(x, norm_ffn[l])
        gate_up = jnp.einsum('btd,df->btf', h, w_ffn_in[l])
        ff_gate, ff_up = jnp.split(gate_up, 2, axis=-1)
        x = x + jnp.einsum('btf,fd->btd', jax.nn.silu(ff_gate) * ff_up, w_ffn_out[l])
    return x


import jax as _jax
import jax.numpy as _jnp

TWIN_FORMAT = 'train_step'
FWD_PARAMS = ['x', 'w_in', 'b_gate', 'norm_mix', 'norm_ffn', 'hgrn_lb_logits', 'hgrn_out_gain', 'q_gain', 'k_gain', 'rel_bias', 'w_proj_a', 'w_proj_b', 'w_out', 'w_ffn_in', 'w_ffn_out']
TWIN_WEIGHTS = ['w_in', 'b_gate', 'norm_mix', 'norm_ffn', 'hgrn_lb_logits', 'hgrn_out_gain', 'q_gain', 'k_gain', 'rel_bias', 'w_proj_a', 'w_proj_b', 'w_out', 'w_ffn_in', 'w_ffn_out']
TWIN_DIFF_INPUT = 'x'
TWIN_INPUTS = ['x', 'w_in', 'b_gate', 'norm_mix', 'norm_ffn', 'hgrn_lb_logits', 'hgrn_out_gain', 'q_gain', 'k_gain', 'rel_bias', 'w_proj_a', 'w_proj_b', 'w_out', 'w_ffn_in', 'w_ffn_out', 'loss_target', 'm_w_in', 'm_b_gate', 'm_norm_mix', 'm_norm_ffn', 'm_hgrn_lb_logits', 'm_hgrn_out_gain', 'm_q_gain', 'm_k_gain', 'm_rel_bias', 'm_w_proj_a', 'm_w_proj_b', 'm_w_out', 'm_w_ffn_in', 'm_w_ffn_out', 'v_w_in', 'v_b_gate', 'v_norm_mix', 'v_norm_ffn', 'v_hgrn_lb_logits', 'v_hgrn_out_gain', 'v_q_gain', 'v_k_gain', 'v_rel_bias', 'v_w_proj_a', 'v_w_proj_b', 'v_w_out', 'v_w_ffn_in', 'v_w_ffn_out']
TWIN_OUTPUTS = ['loss', 'grad_x', 'grad_w_in', 'grad_b_gate', 'grad_norm_mix', 'grad_norm_ffn', 'grad_hgrn_lb_logits', 'grad_hgrn_out_gain', 'grad_q_gain', 'grad_k_gain', 'grad_rel_bias', 'grad_w_proj_a', 'grad_w_proj_b', 'grad_w_out', 'grad_w_ffn_in', 'grad_w_ffn_out', 'delta_w_in', 'delta_b_gate', 'delta_norm_mix', 'delta_norm_ffn', 'delta_hgrn_lb_logits', 'delta_hgrn_out_gain', 'delta_q_gain', 'delta_k_gain', 'delta_rel_bias', 'delta_w_proj_a', 'delta_w_proj_b', 'delta_w_out', 'delta_w_ffn_in', 'delta_w_ffn_out', 'new_m_w_in', 'new_m_b_gate', 'new_m_norm_mix', 'new_m_norm_ffn', 'new_m_hgrn_lb_logits', 'new_m_hgrn_out_gain', 'new_m_q_gain', 'new_m_k_gain', 'new_m_rel_bias', 'new_m_w_proj_a', 'new_m_w_proj_b', 'new_m_w_out', 'new_m_w_ffn_in', 'new_m_w_ffn_out', 'new_v_w_in', 'new_v_b_gate', 'new_v_norm_mix', 'new_v_norm_ffn', 'new_v_hgrn_lb_logits', 'new_v_hgrn_out_gain', 'new_v_q_gain', 'new_v_k_gain', 'new_v_rel_bias', 'new_v_w_proj_a', 'new_v_w_proj_b', 'new_v_w_out', 'new_v_w_ffn_in', 'new_v_w_ffn_out']
TWIN_LEAF_KINDS = {'loss': 'loss', 'grad_x': 'grad_x', 'grad_w_in': 'grad_w', 'grad_b_gate': 'grad_w', 'grad_norm_mix': 'grad_w', 'grad_norm_ffn': 'grad_w', 'grad_hgrn_lb_logits': 'grad_w', 'grad_hgrn_out_gain': 'grad_w', 'grad_q_gain': 'grad_w', 'grad_k_gain': 'grad_w', 'grad_rel_bias': 'grad_w', 'grad_w_proj_a': 'grad_w', 'grad_w_proj_b': 'grad_w', 'grad_w_out': 'grad_w', 'grad_w_ffn_in': 'grad_w', 'grad_w_ffn_out': 'grad_w', 'delta_w_in': 'delta_w', 'delta_b_gate': 'delta_w', 'delta_norm_mix': 'delta_w', 'delta_norm_ffn': 'delta_w', 'delta_hgrn_lb_logits': 'delta_w', 'delta_hgrn_out_gain': 'delta_w', 'delta_q_gain': 'delta_w', 'delta_k_gain': 'delta_w', 'delta_rel_bias': 'delta_w', 'delta_w_proj_a': 'delta_w', 'delta_w_proj_b': 'delta_w', 'delta_w_out': 'delta_w', 'delta_w_ffn_in': 'delta_w', 'delta_w_ffn_out': 'delta_w', 'new_m_w_in': 'new_m', 'new_m_b_gate': 'new_m', 'new_m_norm_mix': 'new_m', 'new_m_norm_ffn': 'new_m', 'new_m_hgrn_lb_logits': 'new_m', 'new_m_hgrn_out_gain': 'new_m', 'new_m_q_gain': 'new_m', 'new_m_k_gain': 'new_m', 'new_m_rel_bias': 'new_m', 'new_m_w_proj_a': 'new_m', 'new_m_w_proj_b': 'new_m', 'new_m_w_out': 'new_m', 'new_m_w_ffn_in': 'new_m', 'new_m_w_ffn_out': 'new_m', 'new_v_w_in': 'new_v', 'new_v_b_gate': 'new_v', 'new_v_norm_mix': 'new_v', 'new_v_norm_ffn': 'new_v', 'new_v_hgrn_lb_logits': 'new_v', 'new_v_hgrn_out_gain': 'new_v', 'new_v_q_gain': 'new_v', 'new_v_k_gain': 'new_v', 'new_v_rel_bias': 'new_v', 'new_v_w_proj_a': 'new_v', 'new_v_w_proj_b': 'new_v', 'new_v_w_out': 'new_v', 'new_v_w_ffn_in': 'new_v', 'new_v_w_ffn_out': 'new_v'}


def _forward(args):
    return _fwd_reference(*[args[k] for k in FWD_PARAMS])


def _output_shape():
    out = _jax.eval_shape(lambda: _forward(_fwd_setup_inputs(0)))
    return out.shape, out.dtype

N_MICROBATCH = 1
ADAM_LR = 0.001
ADAM_B1 = 0.9
ADAM_B2 = 0.999
ADAM_EPS = 1e-08
ADAM_WD = 0.01
ADAM_STEP = 10
PER_EXAMPLE_BATCH_AXIS = {'x': 0, 'loss_target': 0}
SHARED_INPUTS = []
_WEIGHT_DTYPES = {'w_in': _jnp.float32, 'b_gate': _jnp.float32, 'norm_mix': _jnp.float32, 'norm_ffn': _jnp.float32, 'hgrn_lb_logits': _jnp.float32, 'hgrn_out_gain': _jnp.float32, 'q_gain': _jnp.float32, 'k_gain': _jnp.float32, 'rel_bias': _jnp.float32, 'w_proj_a': _jnp.float32, 'w_proj_b': _jnp.float32, 'w_out': _jnp.float32, 'w_ffn_in': _jnp.float32, 'w_ffn_out': _jnp.float32}
MOMENT_SCALE = {'w_in': 3.385376e-02, 'b_gate': 2.043180e-01, 'norm_mix': 1.037490e+00, 'norm_ffn': 6.179473e+00, 'hgrn_lb_logits': 6.503919e-03, 'hgrn_out_gain': 1.644574e+00, 'q_gain': 1.684745e-01, 'k_gain': 1.684475e-01, 'rel_bias': 1.223297e-02, 'w_proj_a': 5.155418e-02, 'w_proj_b': 1.276158e-02, 'w_out': 4.991528e-02, 'w_ffn_in': 4.720282e-02, 'w_ffn_out': 6.955555e-02}


def _to_microbatches(a, axis):
    t = _jnp.moveaxis(a, axis, 0)
    t = t.reshape((N_MICROBATCH, t.shape[0] // N_MICROBATCH) + t.shape[1:])
    return _jnp.moveaxis(t, 1, axis + 1)


def setup_inputs(seed: int = 0) -> dict:
    inp = _fwd_setup_inputs(seed)
    key = _jax.random.fold_in(_jax.random.key(seed), 7919)
    shape, _ = _output_shape()
    out = dict(inp)
    out["loss_target"] = _jax.random.normal(_jax.random.fold_in(key, 0), shape, _jnp.float32)
    for i, name in enumerate(TWIN_WEIGHTS):
        w = inp[name].astype(_jnp.float32)
        if MOMENT_SCALE is None:
            s = _jnp.sqrt(_jnp.mean(_jnp.square(w)) + 1e-30)
        else:
            s = MOMENT_SCALE[name]
        km, kv = _jax.random.split(_jax.random.fold_in(key, i + 1))
        out[name] = w
        out["m_" + name] = s * _jax.random.normal(km, w.shape, _jnp.float32)
        out["v_" + name] = (s * s) * _jax.random.uniform(kv, w.shape, _jnp.float32, 0.5, 1.5)
    if N_MICROBATCH > 1:
        for name, axis in PER_EXAMPLE_BATCH_AXIS.items():
            out[name] = _to_microbatches(out[name], axis)
    return {'x': out['x'], 'w_in': out['w_in'], 'b_gate': out['b_gate'], 'norm_mix': out['norm_mix'], 'norm_ffn': out['norm_ffn'], 'hgrn_lb_logits': out['hgrn_lb_logits'], 'hgrn_out_gain': out['hgrn_out_gain'], 'q_gain': out['q_gain'], 'k_gain': out['k_gain'], 'rel_bias': out['rel_bias'], 'w_proj_a': out['w_proj_a'], 'w_proj_b': out['w_proj_b'], 'w_out': out['w_out'], 'w_ffn_in': out['w_ffn_in'], 'w_ffn_out': out['w_ffn_out'], 'loss_target': out['loss_target'], 'm_w_in': out['m_w_in'], 'm_b_gate': out['m_b_gate'], 'm_norm_mix': out['m_norm_mix'], 'm_norm_ffn': out['m_norm_ffn'], 'm_hgrn_lb_logits': out['m_hgrn_lb_logits'], 'm_hgrn_out_gain': out['m_hgrn_out_gain'], 'm_q_gain': out['m_q_gain'], 'm_k_gain': out['m_k_gain'], 'm_rel_bias': out['m_rel_bias'], 'm_w_proj_a': out['m_w_proj_a'], 'm_w_proj_b': out['m_w_proj_b'], 'm_w_out': out['m_w_out'], 'm_w_ffn_in': out['m_w_ffn_in'], 'm_w_ffn_out': out['m_w_ffn_out'], 'v_w_in': out['v_w_in'], 'v_b_gate': out['v_b_gate'], 'v_norm_mix': out['v_norm_mix'], 'v_norm_ffn': out['v_norm_ffn'], 'v_hgrn_lb_logits': out['v_hgrn_lb_logits'], 'v_hgrn_out_gain': out['v_hgrn_out_gain'], 'v_q_gain': out['v_q_gain'], 'v_k_gain': out['v_k_gain'], 'v_rel_bias': out['v_rel_bias'], 'v_w_proj_a': out['v_w_proj_a'], 'v_w_proj_b': out['v_w_proj_b'], 'v_w_out': out['v_w_out'], 'v_w_ffn_in': out['v_w_ffn_in'], 'v_w_ffn_out': out['v_w_ffn_out']}


def _loss(weights, diff, rest, loss_target):
    with _jax.named_scope("forward"):
        args = {**rest, TWIN_DIFF_INPUT: diff, **{k: w.astype(_WEIGHT_DTYPES[k]) for k, w in weights.items()}}
        y = _forward(args)
    with _jax.named_scope("loss_head"):
        err = _jnp.square(y.astype(_jnp.float32) - loss_target)
        return 0.5 * _jnp.sum(_jnp.mean(err, axis=-1)) if err.ndim else 0.5 * err


def _adamw(w, g, m, v):
    m = ADAM_B1 * m + (1.0 - ADAM_B1) * g
    v = ADAM_B2 * v + (1.0 - ADAM_B2) * _jnp.square(g)
    m_hat = m / (1.0 - ADAM_B1 ** ADAM_STEP)
    v_hat = v / (1.0 - ADAM_B2 ** ADAM_STEP)
    delta = -ADAM_LR * (m_hat / (_jnp.sqrt(v_hat) + ADAM_EPS) + ADAM_WD * w)
    return delta, m, v


def reference(x, w_in, b_gate, norm_mix, norm_ffn, hgrn_lb_logits, hgrn_out_gain, q_gain, k_gain, rel_bias, w_proj_a, w_proj_b, w_out, w_ffn_in, w_ffn_out, loss_target, m_w_in, m_b_gate, m_norm_mix, m_norm_ffn, m_hgrn_lb_logits, m_hgrn_out_gain, m_q_gain, m_k_gain, m_rel_bias, m_w_proj_a, m_w_proj_b, m_w_out, m_w_ffn_in, m_w_ffn_out, v_w_in, v_b_gate, v_norm_mix, v_norm_ffn, v_hgrn_lb_logits, v_hgrn_out_gain, v_q_gain, v_k_gain, v_rel_bias, v_w_proj_a, v_w_proj_b, v_w_out, v_w_ffn_in, v_w_ffn_out):
    given = dict(x=x, w_in=w_in, b_gate=b_gate, norm_mix=norm_mix, norm_ffn=norm_ffn, hgrn_lb_logits=hgrn_lb_logits, hgrn_out_gain=hgrn_out_gain, q_gain=q_gain, k_gain=k_gain, rel_bias=rel_bias, w_proj_a=w_proj_a, w_proj_b=w_proj_b, w_out=w_out, w_ffn_in=w_ffn_in, w_ffn_out=w_ffn_out, loss_target=loss_target, m_w_in=m_w_in, m_b_gate=m_b_gate, m_norm_mix=m_norm_mix, m_norm_ffn=m_norm_ffn, m_hgrn_lb_logits=m_hgrn_lb_logits, m_hgrn_out_gain=m_hgrn_out_gain, m_q_gain=m_q_gain, m_k_gain=m_k_gain, m_rel_bias=m_rel_bias, m_w_proj_a=m_w_proj_a, m_w_proj_b=m_w_proj_b, m_w_out=m_w_out, m_w_ffn_in=m_w_ffn_in, m_w_ffn_out=m_w_ffn_out, v_w_in=v_w_in, v_b_gate=v_b_gate, v_norm_mix=v_norm_mix, v_norm_ffn=v_norm_ffn, v_hgrn_lb_logits=v_hgrn_lb_logits, v_hgrn_out_gain=v_hgrn_out_gain, v_q_gain=v_q_gain, v_k_gain=v_k_gain, v_rel_bias=v_rel_bias, v_w_proj_a=v_w_proj_a, v_w_proj_b=v_w_proj_b, v_w_out=v_w_out, v_w_ffn_in=v_w_ffn_in, v_w_ffn_out=v_w_ffn_out)
    weights = {n: given[n] for n in TWIN_WEIGHTS}
    shared = {n: given[n] for n in SHARED_INPUTS}
    per_example = {n: given[n] for n in ['x']}
    grad_fn = _jax.value_and_grad(_loss, argnums=(0, 1))

    def one_microbatch(ex, loss_target):
        ex = dict(ex)
        diff = ex.pop(TWIN_DIFF_INPUT)
        return grad_fn(weights, diff, {**shared, **ex}, loss_target)

    if N_MICROBATCH == 1:
        loss, (grad_w, grad_x) = one_microbatch(per_example, given["loss_target"])
    else:
        def body(carry, xs):
            loss_sum, grad_sum = carry
            l_k, (gw_k, gx_k) = one_microbatch(xs[0], xs[1])
            with _jax.named_scope("update"):
                return (loss_sum + l_k, _jax.tree.map(_jnp.add, grad_sum, gw_k)), gx_k

        init = (_jnp.zeros((), _jnp.float32), _jax.tree.map(_jnp.zeros_like, weights))
        (loss, grad_w), grad_x = _jax.lax.scan(body, init, (per_example, given["loss_target"]))
    with _jax.named_scope("update"):
        delta_w, new_m, new_v = {}, {}, {}
        for n in TWIN_WEIGHTS:
            delta_w[n], new_m[n], new_v[n] = _adamw(weights[n], grad_w[n], given["m_" + n], given["v_" + n])
    return (loss, grad_x, *[grad_w[n] for n in TWIN_WEIGHTS], *[delta_w[n] for n in TWIN_WEIGHTS],
            *[new_m[n] for n in TWIN_WEIGHTS], *[new_v[n] for n in TWIN_WEIGHTS])
```

```python
import functools

import numpy as np
import jax
import jax.numpy as jnp
from jax import lax
from jax.experimental import pallas as pl
from jax.experimental.pallas import tpu as pltpu

f32 = jnp.float32
bf16 = jnp.bfloat16
HI = lax.Precision.HIGHEST
MESH = pl.DeviceIdType.MESH
AXES = ("x", "y", "c")
NDEV = 8

CHUNK = 64
HEAD = 128
SUB = 16
N_PAST = 8
BAND = N_PAST + 1
PAD = N_PAST * CHUNK
REL_FUTURE = CHUNK - 1
REL_PAST = 2 * CHUNK - 1
N_REL = REL_FUTURE + REL_PAST + 1
N_REL_PAD = 256
EPS = 1e-6
NEG = -1e30

ADAM_LR = 0.001
ADAM_B1 = 0.9
ADAM_B2 = 0.999
ADAM_EPS = 1e-08
ADAM_WD = 0.01
ADAM_STEP = 10

VMEM_LIMIT = 56 * 1024 * 1024


def _params(sem=None):
    return pltpu.CompilerParams(dimension_semantics=sem, vmem_limit_bytes=VMEM_LIMIT)


def _tile(n, pref, unit=128):
    if n <= pref:
        return n
    t = (pref // unit) * unit
    while t >= unit:
        if n % t == 0:
            return t
        t -= unit
    return n


_sigmoid = jax.nn.sigmoid


def _mm(a, b, *, mode, name, b_blocked=False, out_blocked=False, out_dtype=f32, tm=1024, tn=1024, tk=2048):
    if mode == "tn":
        K, M = a.shape
    else:
        M, K = a.shape
    if b_blocked:
        nb, mid, cb = b.shape
        if mode == "nn":
            assert mid == K
            N, tn = nb * cb, cb
        else:
            assert mode == "nt" and nb * cb == K
            N, tk = mid, cb
    else:
        N = b.shape[1] if mode in ("nn", "tn") else b.shape[0]
    tm = _tile(M, tm)
    tn = tn if (b_blocked and mode == "nn") or out_blocked else _tile(N, tn)
    tk = tk if b_blocked and mode == "nt" else _tile(K, tk)
    assert M % tm == 0 and N % tn == 0 and K % tk == 0
    nk = K // tk
    grid = (M // tm, N // tn, nk)
    if mode == "tn":
        a_spec = pl.BlockSpec((tk, tm), lambda i, j, k: (k, i))
    else:
        a_spec = pl.BlockSpec((tm, tk), lambda i, j, k: (i, k))
    if mode == "nn":
        b_spec = pl.BlockSpec((None, tk, cb), lambda i, j, k: (j, k, 0)) if b_blocked else pl.BlockSpec((tk, tn), lambda i, j, k: (k, j))
    elif mode == "nt":
        b_spec = pl.BlockSpec((None, tn, cb), lambda i, j, k: (k, j, 0)) if b_blocked else pl.BlockSpec((tn, tk), lambda i, j, k: (j, k))
    else:
        b_spec = pl.BlockSpec((tk, tn), lambda i, j, k: (k, j))
    if out_blocked:
        out_shape = jax.ShapeDtypeStruct((N // tn, M, tn), out_dtype)
        o_spec = pl.BlockSpec((None, tm, tn), lambda i, j, k: (j, i, 0))
    else:
        out_shape = jax.ShapeDtypeStruct((M, N), out_dtype)
        o_spec = pl.BlockSpec((tm, tn), lambda i, j, k: (i, j))
    dims = {"nn": ((1,), (0,)), "nt": ((1,), (1,)), "tn": ((0,), (0,))}[mode]

    def body(a_ref, b_ref, o_ref, *acc):
        p = lax.dot_general(a_ref[...].astype(bf16), b_ref[...].astype(bf16), (dims, ((), ())), preferred_element_type=f32)
        if nk == 1:
            o_ref[...] = p.astype(out_dtype)
        else:
            acc_ref = acc[0]
            k = pl.program_id(2)

            @pl.when(k == 0)
            def _():
                acc_ref[...] = p

            @pl.when(k > 0)
            def _():
                acc_ref[...] += p

            @pl.when(k == nk - 1)
            def _():
                o_ref[...] = acc_ref[...].astype(out_dtype)

    return pl.pallas_call(
        body, name=name, grid=grid, in_specs=[a_spec, b_spec], out_specs=o_spec, out_shape=out_shape,
        scratch_shapes=[pltpu.VMEM((tm, tn), f32)] if nk > 1 else [],
        compiler_params=_params(("parallel", "parallel", "arbitrary")),
    )(a, b)


def _rms_fwd(x, res, gain, name):
    T, D = x.shape
    tr = _tile(T, 256, 8)
    row = pl.BlockSpec((tr, D), lambda i: (i, 0))
    vec = pl.BlockSpec((1, D), lambda i: (0, 0))

    def body(*refs):
        if res is None:
            x_ref, g_ref, h_ref = refs
            xs = x_ref[...]
        else:
            x_ref, r_ref, g_ref, xs_ref, h_ref = refs
            xs = x_ref[...] + r_ref[...]
            xs_ref[...] = xs
        r = lax.rsqrt(jnp.mean(xs * xs, axis=-1, keepdims=True) + EPS)
        h_ref[...] = (xs * r * g_ref[...]).astype(bf16)

    h_shape = jax.ShapeDtypeStruct((T, D), bf16)
    if res is None:
        return pl.pallas_call(body, name=name, grid=(T // tr,), in_specs=[row, vec], out_specs=row, out_shape=h_shape,
                              compiler_params=_params(("parallel",)))(x, gain)
    return pl.pallas_call(body, name=name, grid=(T // tr,), in_specs=[row, row, vec], out_specs=(row, row),
                          out_shape=(jax.ShapeDtypeStruct((T, D), f32), h_shape), compiler_params=_params(("parallel",)))(x, res, gain)


def _rms_bwd(xs, gain, dh, extra, name):
    T, D = xs.shape
    tr = _tile(T, 256, 8)
    row = pl.BlockSpec((tr, D), lambda i: (i, 0))
    vec = pl.BlockSpec((1, D), lambda i: (0, 0))

    def body(x_ref, g_ref, dh_ref, e_ref, dx_ref, dg_ref):
        x = x_ref[...]
        r = lax.rsqrt(jnp.mean(x * x, axis=-1, keepdims=True) + EPS)
        xhat = x * r
        dh_v = dh_ref[...]
        gd = dh_v * g_ref[...]
        dx_ref[...] = e_ref[...] + r * (gd - xhat * jnp.mean(gd * xhat, axis=-1, keepdims=True))
        part = jnp.sum(dh_v * xhat, axis=0, keepdims=True)

        @pl.when(pl.program_id(0) == 0)
        def _():
            dg_ref[...] = part

        @pl.when(pl.program_id(0) > 0)
        def _():
            dg_ref[...] += part

    return pl.pallas_call(body, name=name, grid=(T // tr,), in_specs=[row, vec, row, row], out_specs=(row, vec),
                          out_shape=(jax.ShapeDtypeStruct((T, D), f32), jax.ShapeDtypeStruct((1, D), f32)),
                          compiler_params=_params(("arbitrary",)))(xs, gain, dh, extra)


def _merge_fwd(pa, pb, proj, b_gate, off, name):
    T, D = pa.shape
    tr, tc = _tile(T, 512, 8), _tile(D, 512)
    oa, ob, nb = off // tc, (off + D) // tc, D // tc
    blk = pl.BlockSpec((tr, tc), lambda i, j: (i, j))

    def body(pa_ref, pb_ref, ga_ref, gb_ref, ba_ref, bb_ref, o_ref):
        ga = _sigmoid(ga_ref[...] + ba_ref[...])
        gb = _sigmoid(gb_ref[...] + bb_ref[...])
        o_ref[...] = (ga * pa_ref[...] + gb * pb_ref[...]).astype(bf16)

    return pl.pallas_call(
        body, name=name, grid=(T // tr, nb),
        in_specs=[blk, blk, pl.BlockSpec((tr, tc), lambda i, j: (i, oa + j)), pl.BlockSpec((tr, tc), lambda i, j: (i, ob + j)),
                  pl.BlockSpec((1, tc), lambda i, j: (0, j)), pl.BlockSpec((1, tc), lambda i, j: (0, nb + j))],
        out_specs=blk, out_shape=jax.ShapeDtypeStruct((T, D), bf16), compiler_params=_params(("parallel", "parallel")),
    )(pa, pb, proj, proj, b_gate, b_gate)


def _branch_bwd(dm, p, proj, b_gate, off, boff, name):
    T, D = p.shape
    tr, tc = _tile(T, 512, 8), _tile(D, 512)
    og, obias = off // tc, boff // tc
    blk = pl.BlockSpec((tr, tc), lambda j, i: (i, j))
    vec = pl.BlockSpec((1, tc), lambda j, i: (0, j))

    def body(dm_ref, p_ref, gl_ref, b_ref, dp_ref, dgl_ref, db_ref):
        g = _sigmoid(gl_ref[...] + b_ref[...])
        dm_v = dm_ref[...]
        dp_ref[...] = (dm_v * g).astype(bf16)
        dgl = dm_v * p_ref[...] * g * (1.0 - g)
        dgl_ref[...] = dgl.astype(bf16)
        part = jnp.sum(dgl, axis=0, keepdims=True)

        @pl.when(pl.program_id(1) == 0)
        def _():
            db_ref[...] = part

        @pl.when(pl.program_id(1) > 0)
        def _():
            db_ref[...] += part

    return pl.pallas_call(
        body, name=name, grid=(D // tc, T // tr),
        in_specs=[blk, blk, pl.BlockSpec((tr, tc), lambda j, i: (i, og + j)), pl.BlockSpec((1, tc), lambda j, i: (0, obias + j))],
        out_specs=(blk, blk, vec),
        out_shape=(jax.ShapeDtypeStruct((T, D), bf16), jax.ShapeDtypeStruct((T, D), bf16), jax.ShapeDtypeStruct((1, D), f32)),
        compiler_params=_params(("parallel", "arbitrary")),
    )(dm, p, proj, b_gate)


def _swiglu_fwd(gu, name):
    T, F2 = gu.shape
    F = F2 // 2
    tr, tc = _tile(T, 512, 8), _tile(F, 512)
    nb = F // tc

    def body(g_ref, u_ref, o_ref):
        g = g_ref[...]
        o_ref[...] = (g * _sigmoid(g) * u_ref[...]).astype(bf16)

    return pl.pallas_call(
        body, name=name, grid=(T // tr, nb),
        in_specs=[pl.BlockSpec((tr, tc), lambda i, j: (i, j)), pl.BlockSpec((tr, tc), lambda i, j: (i, nb + j))],
        out_specs=pl.BlockSpec((tr, tc), lambda i, j: (i, j)), out_shape=jax.ShapeDtypeStruct((T, F), bf16),
        compiler_params=_params(("parallel", "parallel")),
    )(gu, gu)


def _swiglu_bwd(gu, dact, name):
    T, F2 = gu.shape
    F = F2 // 2
    tr, tc = _tile(T, 512, 8), _tile(F, 512)
    nb = F // tc
    blk = pl.BlockSpec((tr, tc), lambda i, j: (i, j))

    def body(g_ref, u_ref, d_ref, dg_ref, du_ref):
        g = g_ref[...]
        s = _sigmoid(g)
        d = d_ref[...]
        dg_ref[...] = (d * u_ref[...] * s * (1.0 + g * (1.0 - s))).astype(bf16)
        du_ref[...] = (d * g * s).astype(bf16)

    return pl.pallas_call(
        body, name=name, grid=(T // tr, nb),
        in_specs=[blk, pl.BlockSpec((tr, tc), lambda i, j: (i, nb + j)), blk], out_specs=(blk, blk),
        out_shape=(jax.ShapeDtypeStruct((T, F), bf16), jax.ShapeDtypeStruct((T, F), bf16)),
        compiler_params=_params(("parallel", "parallel")),
    )(gu, gu, dact)


def _loss_head(x1, fo, target, name):
    T, D = x1.shape
    tr = _tile(T, 256, 8)
    row = pl.BlockSpec((tr, D), lambda i: (i, 0))
    acc = pl.BlockSpec((8, 128), lambda i: (0, 0))

    def body(x_ref, f_ref, t_ref, dy_ref, l_ref):
        d = x_ref[...] + f_ref[...] - t_ref[...]
        dy_ref[...] = d * (1.0 / D)
        part = jnp.sum(jnp.sum(d * d, axis=1, keepdims=True), axis=0, keepdims=True)

        @pl.when(pl.program_id(0) == 0)
        def _():
            l_ref[...] = jnp.zeros((8, 128), f32)

        l_ref[...] += part

    return pl.pallas_call(body, name=name, grid=(T // tr,), in_specs=[row, row, row], out_specs=(row, acc),
                          out_shape=(jax.ShapeDtypeStruct((T, D), f32), jax.ShapeDtypeStruct((8, 128), f32)),
                          compiler_params=_params(("arbitrary",)))(x1, fo, target)


def _dotb(a, b, dims):
    return lax.dot_general(a.astype(bf16), b.astype(bf16), (dims, ((), ())), preferred_element_type=f32)


def _hgrn_chunk(q, fl, iv, g, logits, gain, st):
    lb = jax.nn.softmax(logits, axis=0)[0:1]
    f = lb + (1.0 - lb) * _sigmoid(fl)
    lf = jnp.log(f)
    kk = 1.0 - f
    qs = q * _sigmoid(q)
    row = lax.broadcasted_iota(jnp.int32, (CHUNK, CHUNK), 0)
    col = lax.broadcasted_iota(jnp.int32, (CHUNK, CHUNK), 1)
    tril = (col <= row).astype(f32)
    b = lax.dot_general(tril, lf, (((1,), (0,)), ((), ())), precision=HI, preferred_element_type=f32)
    b_last = jnp.sum(lf, axis=0, keepdims=True)
    o = _dotb(qs * jnp.exp(b), st, ((1,), (1,)))
    r3 = lax.broadcasted_iota(jnp.int32, (SUB, SUB, HEAD), 0)
    c3 = lax.broadcasted_iota(jnp.int32, (SUB, SUB, HEAD), 1)
    parts = []
    for i in range(CHUNK // SUB):
        lo, hi = i * SUB, (i + 1) * SUB
        bi = b[lo:hi]
        dec = jnp.exp(jnp.where(c3 <= r3, bi[:, None, :] - bi[None, :, :], -jnp.inf))
        s = jnp.sum(qs[lo:hi][:, None, :] * kk[lo:hi][None, :, :] * dec, axis=-1)
        if i > 0:
            anchor = jnp.max(bi, axis=0, keepdims=True)
            qa = qs[lo:hi] * jnp.exp(bi - anchor)
            kd = kk[:lo] * jnp.exp(anchor - b[:lo])
            s = jnp.concatenate([_dotb(qa, kd, ((1,), (1,))), s], axis=1)
        parts.append(_dotb(s, iv[:hi], ((1,), (0,))))
    o = o + jnp.concatenate(parts, axis=0)
    st_new = st * jnp.exp(b_last) + _dotb(iv, kk * jnp.exp(b_last - b), ((0,), (0,)))
    o = o * lax.rsqrt(jnp.mean(o * o, axis=-1, keepdims=True) + EPS)
    o = o * gain * (g * _sigmoid(g))
    return o, st_new


def _hgrn_fwd(proj, logits, gain, n_heads, name):
    T = proj.shape[0]
    nc = T // CHUNK
    H = n_heads

    def col(k):
        return pl.BlockSpec((CHUNK, HEAD), lambda h, c: (c, k * H + h))

    def body(q_ref, f_ref, i_ref, g_ref, l_ref, ga_ref, y_ref, s_ref, st):
        @pl.when(pl.program_id(1) == 0)
        def _():
            st[...] = jnp.zeros((HEAD, HEAD), f32)

        s_ref[...] = st[...]
        o, st_new = _hgrn_chunk(q_ref[...], f_ref[...], i_ref[...], g_ref[...], l_ref[...], ga_ref[...], st[...])
        y_ref[...] = o.astype(bf16)
        st[...] = st_new

    return pl.pallas_call(
        body, name=name, grid=(H, nc),
        in_specs=[col(0), col(1), col(2), col(3), pl.BlockSpec((2, HEAD), lambda h, c: (0, h)), pl.BlockSpec((1, HEAD), lambda h, c: (0, h))],
        out_specs=(pl.BlockSpec((CHUNK, HEAD), lambda h, c: (c, h)), pl.BlockSpec((None, None, HEAD, HEAD), lambda h, c: (h, c, 0, 0))),
        out_shape=(jax.ShapeDtypeStruct((T, H * HEAD), bf16), jax.ShapeDtypeStruct((H, nc, HEAD, HEAD), f32)),
        scratch_shapes=[pltpu.VMEM((HEAD, HEAD), f32)],
        compiler_params=_params(("parallel", "arbitrary")),
    )(proj, proj, proj, proj, logits, gain)


def _hgrn_bwd(proj, logits, gain, states, dy, n_heads, name):
    T = proj.shape[0]
    nc = T // CHUNK
    H = n_heads

    def col(k):
        return pl.BlockSpec((CHUNK, HEAD), lambda h, c: (nc - 1 - c, k * H + h))

    out_blk = pl.BlockSpec((CHUNK, HEAD), lambda h, c: (nc - 1 - c, h))

    def body(q_ref, f_ref, i_ref, g_ref, l_ref, ga_ref, s_ref, dy_ref, dq_ref, df_ref, di_ref, dg_ref, dl_ref, dga_ref, dst):
        first = pl.program_id(1) == 0

        @pl.when(first)
        def _():
            dst[...] = jnp.zeros((HEAD, HEAD), f32)

        _, vjp = jax.vjp(_hgrn_chunk, q_ref[...], f_ref[...], i_ref[...], g_ref[...], l_ref[...], ga_ref[...], s_ref[...])
        dq, df, di, dg, dl, dga, ds = vjp((dy_ref[...], dst[...]))
        dq_ref[...] = dq.astype(bf16)
        df_ref[...] = df.astype(bf16)
        di_ref[...] = di.astype(bf16)
        dg_ref[...] = dg.astype(bf16)
        dst[...] = ds

        @pl.when(first)
        def _():
            dl_ref[...] = dl
            dga_ref[...] = dga

        @pl.when(jnp.logical_not(first))
        def _():
            dl_ref[...] += dl
            dga_ref[...] += dga

    act = jax.ShapeDtypeStruct((T, H * HEAD), bf16)
    return pl.pallas_call(
        body, name=name, grid=(H, nc),
        in_specs=[col(0), col(1), col(2), col(3), pl.BlockSpec((2, HEAD), lambda h, c: (0, h)), pl.BlockSpec((1, HEAD), lambda h, c: (0, h)),
                  pl.BlockSpec((None, None, HEAD, HEAD), lambda h, c: (h, nc - 1 - c, 0, 0)), out_blk],
        out_specs=(out_blk, out_blk, out_blk, out_blk, pl.BlockSpec((2, HEAD), lambda h, c: (0, h)), pl.BlockSpec((1, HEAD), lambda h, c: (0, h))),
        out_shape=(act, act, act, act, jax.ShapeDtypeStruct((2, H * HEAD), f32), jax.ShapeDtypeStruct((1, H * HEAD), f32)),
        scratch_shapes=[pltpu.VMEM((HEAD, HEAD), f32)],
        compiler_params=_params(("parallel", "arbitrary")),
    )(proj, proj, proj, proj, logits, gain, states, dy)


def _rel_index():
    t = np.arange(CHUNK)[:, None]
    sp = np.arange(BAND * CHUNK)[None, :]
    dist = (N_PAST - sp // CHUNK) * CHUNK + t - sp % CHUNK
    return (np.clip(dist, -REL_FUTURE, REL_PAST) + REL_FUTURE).reshape(1, -1).astype(np.int32)


def _bias_table(rel_bias_pad, idx, name):
    H = rel_bias_pad.shape[0]
    n = idx.shape[1]
    tc = _tile(n, 4096)

    def body(rb_ref, idx_ref, o_ref):
        onehot = (lax.broadcasted_iota(jnp.int32, (N_REL_PAD, tc), 0) == idx_ref[...]).astype(f32)
        o_ref[...] = lax.dot_general(rb_ref[...], onehot, (((1,), (0,)), ((), ())), precision=HI, preferred_element_type=f32)

    return pl.pallas_call(
        body, name=name, grid=(n // tc,),
        in_specs=[pl.BlockSpec((H, N_REL_PAD), lambda j: (0, 0)), pl.BlockSpec((1, tc), lambda j: (0, j))],
        out_specs=pl.BlockSpec((H, tc), lambda j: (0, j)), out_shape=jax.ShapeDtypeStruct((H, n), f32),
        compiler_params=_params(("parallel",)),
    )(rel_bias_pad, idx)


def _bias_table_bwd(dbias, idx, name):
    H, n = dbias.shape
    tc = _tile(n, 4096)

    def body(d_ref, idx_ref, o_ref):
        onehot = (lax.broadcasted_iota(jnp.int32, (N_REL_PAD, tc), 0) == idx_ref[...]).astype(f32)
        part = lax.dot_general(d_ref[...], onehot, (((1,), (1,)), ((), ())), precision=HI, preferred_element_type=f32)

        @pl.when(pl.program_id(0) == 0)
        def _():
            o_ref[...] = part

        @pl.when(pl.program_id(0) > 0)
        def _():
            o_ref[...] += part

    return pl.pallas_call(
        body, name=name, grid=(n // tc,),
        in_specs=[pl.BlockSpec((H, tc), lambda j: (0, j)), pl.BlockSpec((1, tc), lambda j: (0, j))],
        out_specs=pl.BlockSpec((H, N_REL_PAD), lambda j: (0, 0)), out_shape=jax.ShapeDtypeStruct((H, N_REL_PAD), f32),
        compiler_params=_params(("arbitrary",)),
    )(dbias, idx)


def _head_norm(t, gain):
    return t * lax.rsqrt(jnp.mean(t * t, axis=-1, keepdims=True) + EPS) * gain


def _attn_chunk(q, kb, vb, qg, bias, n):
    qh = _head_norm(q, qg)
    s = _dotb(qh, kb, ((1,), (1,))) * (HEAD ** -0.5) + bias
    pos = n * CHUNK - PAD + lax.broadcasted_iota(jnp.int32, (1, BAND * CHUNK), 1)
    s = jnp.where(pos >= 0, s, NEG)
    e = jnp.exp(s - jnp.max(s, axis=-1, keepdims=True))
    p = e / jnp.sum(e, axis=-1, keepdims=True)
    return _dotb(p, vb, ((1,), (0,)))


def _attn_fwd(proj, q_gain, k_gain, bias, off, n_heads, name):
    T = proj.shape[0]
    nc = T // CHUNK
    H = n_heads
    o0 = off // HEAD
    full = lambda k: pl.BlockSpec((T, HEAD), lambda h, c: (0, o0 + k * H + h))
    vec = pl.BlockSpec((1, HEAD), lambda h, c: (0, 0))

    def body(q_ref, k_ref, v_ref, qg_ref, kg_ref, b_ref, y_ref, kp, vp):
        c = pl.program_id(1)

        @pl.when(c == 0)
        def _():
            kp[pl.ds(0, PAD), :] = jnp.zeros((PAD, HEAD), f32)
            vp[pl.ds(0, PAD), :] = jnp.zeros((PAD, HEAD), f32)
            kp[pl.ds(PAD, T), :] = _head_norm(k_ref[...], kg_ref[...])
            vp[pl.ds(PAD, T), :] = v_ref[...]

        band = pl.ds(pl.multiple_of(c * CHUNK, CHUNK), BAND * CHUNK)
        y_ref[...] = _attn_chunk(q_ref[...], kp[band, :], vp[band, :], qg_ref[...], b_ref[...], c).astype(bf16)

    return pl.pallas_call(
        body, name=name, grid=(H, nc),
        in_specs=[pl.BlockSpec((CHUNK, HEAD), lambda h, c: (c, o0 + h)), full(1), full(2), vec, vec,
                  pl.BlockSpec((None, CHUNK, BAND * CHUNK), lambda h, c: (h, 0, 0))],
        out_specs=pl.BlockSpec((CHUNK, HEAD), lambda h, c: (c, h)), out_shape=jax.ShapeDtypeStruct((T, H * HEAD), bf16),
        scratch_shapes=[pltpu.VMEM((T + PAD, HEAD), f32), pltpu.VMEM((T + PAD, HEAD), f32)],
        compiler_params=_params(("parallel", "arbitrary")),
    )(proj, proj, proj, q_gain, k_gain, bias)


def _attn_bwd(proj, q_gain, k_gain, bias, dy, off, n_heads, name):
    T = proj.shape[0]
    nc = T // CHUNK
    H = n_heads
    o0 = off // HEAD
    full = lambda k: pl.BlockSpec((T, HEAD), lambda h, c: (0, o0 + k * H + h))
    full_out = pl.BlockSpec((T, HEAD), lambda h, c: (0, h))
    vec = pl.BlockSpec((1, HEAD), lambda h, c: (0, 0))
    chunk_out = pl.BlockSpec((CHUNK, HEAD), lambda h, c: (c, h))
    bias_blk = pl.BlockSpec((None, CHUNK, BAND * CHUNK), lambda h, c: (h, 0, 0))

    def body(q_ref, k_ref, v_ref, qg_ref, kg_ref, b_ref, dy_ref, dq_ref, dk_ref, dv_ref, db_ref, dqg_ref, dkg_ref, kp, vp, dkp, dvp):
        h = pl.program_id(0)
        c = pl.program_id(1)

        @pl.when(c == 0)
        def _():
            kp[pl.ds(0, PAD), :] = jnp.zeros((PAD, HEAD), f32)
            vp[pl.ds(0, PAD), :] = jnp.zeros((PAD, HEAD), f32)
            kp[pl.ds(PAD, T), :] = _head_norm(k_ref[...], kg_ref[...])
            vp[pl.ds(PAD, T), :] = v_ref[...]
            dkp[...] = jnp.zeros((T + PAD, HEAD), f32)
            dvp[...] = jnp.zeros((T + PAD, HEAD), f32)
            db_ref[...] = jnp.zeros((CHUNK, BAND * CHUNK), f32)

        @pl.when(jnp.logical_and(h == 0, c == 0))
        def _():
            dqg_ref[...] = jnp.zeros((1, HEAD), f32)
            dkg_ref[...] = jnp.zeros((1, HEAD), f32)

        band = pl.ds(pl.multiple_of(c * CHUNK, CHUNK), BAND * CHUNK)
        _, vjp = jax.vjp(functools.partial(_attn_chunk, n=c), q_ref[...], kp[band, :], vp[band, :], qg_ref[...], b_ref[...])
        dq, dkb, dvb, dqg, db = vjp(dy_ref[...])
        dq_ref[...] = dq.astype(bf16)
        dkp[band, :] += dkb
        dvp[band, :] += dvb
        db_ref[...] += db
        dqg_ref[...] += dqg

        @pl.when(c == nc - 1)
        def _():
            _, nvjp = jax.vjp(_head_norm, k_ref[...], kg_ref[...])
            dk, dkg = nvjp(dkp[pl.ds(PAD, T), :])
            dk_ref[...] = dk.astype(bf16)
            dv_ref[...] = dvp[pl.ds(PAD, T), :].astype(bf16)
            dkg_ref[...] += dkg

    act = jax.ShapeDtypeStruct((T, H * HEAD), bf16)
    gvec = jax.ShapeDtypeStruct((1, HEAD), f32)
    pad_buf = pltpu.VMEM((T + PAD, HEAD), f32)
    return pl.pallas_call(
        body, name=name, grid=(H, nc),
        in_specs=[pl.BlockSpec((CHUNK, HEAD), lambda h, c: (c, o0 + h)), full(1), full(2), vec, vec, bias_blk, chunk_out],
        out_specs=(chunk_out, full_out, full_out, bias_blk, vec, vec),
        out_shape=(act, act, act, jax.ShapeDtypeStruct((H, CHUNK, BAND * CHUNK), f32), gvec, gvec),
        scratch_shapes=[pad_buf, pad_buf, pad_buf, pad_buf],
        compiler_params=_params(("arbitrary", "arbitrary")),
    )(proj, proj, proj, q_gain, k_gain, bias, dy)


def _position():
    x, y, c = lax.axis_index("x"), lax.axis_index("y"), lax.axis_index("c")
    return x, y, c, 4 * x + 2 * y + c


def _flip(v, bit):
    return 1 - v if bit else v


def _all_gather(shards, name):
    n = len(shards)

    def body(*refs):
        ins, outs = refs[:n], refs[n:2 * n]
        send, recv, loc = refs[2 * n:]
        x, y, c, me = _position()
        sib = (x, y, 1 - c)
        chips = [(1 - x, y), (x, 1 - y), (1 - x, 1 - y)]

        def copy(w, k, src, blk, to):
            return pltpu.make_async_remote_copy(src_ref=src, dst_ref=outs[w].at[blk], send_sem=send.at[w, k], recv_sem=recv.at[w, k],
                                                device_id=to, device_id_type=MESH)

        mine = [pltpu.make_async_copy(ins[w], outs[w].at[me], loc.at[w]) for w in range(n)]
        for cp in mine:
            cp.start()
        first = []
        for j, (px, py) in enumerate(chips):
            first += [copy(w, 1 + j, ins[w], me, (px, py, c)) for w in range(n)]
        first += [copy(w, 0, ins[w], me, sib) for w in range(n)]
        for cp in first:
            cp.start()
        passed = []
        for j, (px, py) in enumerate(chips):
            blk = 4 * px + 2 * py + c
            for w in range(n):
                copy(w, 1 + j, ins[w], blk, sib).wait_recv()
                fwd = copy(w, 4 + j, outs[w].at[blk], blk, sib)
                fwd.start()
                passed.append(fwd)
        for w in range(n):
            copy(w, 0, ins[w], 4 * x + 2 * y + (1 - c), sib).wait_recv()
        for j, (px, py) in enumerate(chips):
            for w in range(n):
                copy(w, 4 + j, ins[w], 4 * px + 2 * py + (1 - c), sib).wait_recv()
        for cp in first + passed:
            cp.wait_send()
        for cp in mine:
            cp.wait()

    hbm = pl.BlockSpec(memory_space=pl.ANY)
    return pl.pallas_call(
        body, name=name, in_specs=[hbm] * n, out_specs=tuple([hbm] * n),
        out_shape=tuple(jax.ShapeDtypeStruct((NDEV,) + s.shape, s.dtype) for s in shards),
        scratch_shapes=[pltpu.SemaphoreType.DMA((n, 7)), pltpu.SemaphoreType.DMA((n, 7)), pltpu.SemaphoreType.DMA((n,))],
    )(*shards)


def _scatter_parts(grads, name):
    n = len(grads)

    def body(*refs):
        ins, outs = refs[:n], refs[n:2 * n]
        send, recv, loc = refs[2 * n:]
        x, y, c, me = _position()
        mine = [pltpu.make_async_copy(ins[w].at[me], outs[w].at[me], loc.at[w]) for w in range(n)]
        for cp in mine:
            cp.start()
        sends, waits = [], []
        for r in range(1, NDEV):
            px, py, pc = _flip(x, r & 4), _flip(y, r & 2), _flip(c, r & 1)
            peer = 4 * px + 2 * py + pc
            for w in range(n):
                sends.append(pltpu.make_async_remote_copy(src_ref=ins[w].at[peer], dst_ref=outs[w].at[me], send_sem=send.at[w, r - 1],
                                                          recv_sem=recv.at[w, r - 1], device_id=(px, py, pc), device_id_type=MESH))
                waits.append(pltpu.make_async_remote_copy(src_ref=ins[w].at[peer], dst_ref=outs[w].at[peer], send_sem=send.at[w, r - 1],
                                                          recv_sem=recv.at[w, r - 1], device_id=(px, py, pc), device_id_type=MESH))
        for cp in sends:
            cp.start()
        for cp in waits:
            cp.wait_recv()
        for cp in sends:
            cp.wait_send()
        for cp in mine:
            cp.wait()

    hbm = pl.BlockSpec(memory_space=pl.ANY)
    return pl.pallas_call(
        body, name=name, in_specs=[hbm] * n, out_specs=tuple([hbm] * n),
        out_shape=tuple(jax.ShapeDtypeStruct(g.shape, g.dtype) for g in grads),
        scratch_shapes=[pltpu.SemaphoreType.DMA((n, 7)), pltpu.SemaphoreType.DMA((n, 7)), pltpu.SemaphoreType.DMA((n,))],
    )(*grads)


def _small_all_reduce(v, name):
    R, C = v.shape

    def body(v_ref, o_ref, buf, send, recv):
        x, y, c, me = _position()
        buf[me] = v_ref[...]
        sends, waits = [], []
        for r in range(1, NDEV):
            px, py, pc = _flip(x, r & 4), _flip(y, r & 2), _flip(c, r & 1)
            peer = 4 * px + 2 * py + pc
            sends.append(pltpu.make_async_remote_copy(src_ref=v_ref, dst_ref=buf.at[me], send_sem=send.at[r - 1], recv_sem=recv.at[r - 1],
                                                      device_id=(px, py, pc), device_id_type=MESH))
            waits.append(pltpu.make_async_remote_copy(src_ref=v_ref, dst_ref=buf.at[peer], send_sem=send.at[r - 1], recv_sem=recv.at[r - 1],
                                                      device_id=(px, py, pc), device_id_type=MESH))
        for cp in sends:
            cp.start()
        for cp in waits:
            cp.wait_recv()
        for cp in sends:
            cp.wait_send()
        acc = buf[0]
        for i in range(1, NDEV):
            acc = acc + buf[i]
        o_ref[...] = acc

    vm = pl.BlockSpec(memory_space=pltpu.VMEM)
    return pl.pallas_call(
        body, name=name, in_specs=[vm], out_specs=vm, out_shape=jax.ShapeDtypeStruct((R, C), f32),
        scratch_shapes=[pltpu.VMEM((NDEV, R, C), f32), pltpu.SemaphoreType.DMA((7,)), pltpu.SemaphoreType.DMA((7,))],
    )(v)


def _adamw_math(w, g, m, v):
    m = ADAM_B1 * m + (1.0 - ADAM_B1) * g
    v = ADAM_B2 * v + (1.0 - ADAM_B2) * (g * g)
    m_hat = m / (1.0 - ADAM_B1 ** ADAM_STEP)
    v_hat = v / (1.0 - ADAM_B2 ** ADAM_STEP)
    delta = -ADAM_LR * (m_hat / (jnp.sqrt(v_hat) + ADAM_EPS) + ADAM_WD * w)
    return delta, m, v


def _adamw_parts(w, m, v, parts, name):
    R, C = w.shape
    tr = _tile(R, 128, 16)
    blk = pl.BlockSpec((tr, C), lambda i: (i, 0))

    def body(w_ref, m_ref, v_ref, p_ref, g_ref, d_ref, mo_ref, vo_ref):
        g = p_ref[0].astype(f32)
        for i in range(1, NDEV):
            g = g + p_ref[i].astype(f32)
        d, mn, vn = _adamw_math(w_ref[...], g, m_ref[...], v_ref[...])
        g_ref[...] = g
        d_ref[...] = d
        mo_ref[...] = mn
        vo_ref[...] = vn

    shp = jax.ShapeDtypeStruct((R, C), f32)
    return pl.pallas_call(
        body, name=name, grid=(R // tr,), in_specs=[blk, blk, blk, pl.BlockSpec((NDEV, tr, C), lambda i: (0, i, 0))],
        out_specs=(blk, blk, blk, blk), out_shape=(shp, shp, shp, shp), compiler_params=_params(("parallel",)),
    )(w, m, v, parts)


def _adamw_small(w, g, m, v, name):
    def body(w_ref, g_ref, m_ref, v_ref, d_ref, mo_ref, vo_ref):
        d, mn, vn = _adamw_math(w_ref[...], g_ref[...], m_ref[...], v_ref[...])
        d_ref[...] = d
        mo_ref[...] = mn
        vo_ref[...] = vn

    shp = jax.ShapeDtypeStruct(w.shape, f32)
    return pl.pallas_call(body, name=name, out_shape=(shp, shp, shp))(w, g, m, v)


SMALL_COLS = 1024


def _pack(arrs):
    flat = jnp.concatenate([a.reshape(-1) for a in arrs])
    rows = -(-flat.shape[0] // (8 * SMALL_COLS)) * 8
    return jnp.pad(flat, (0, rows * SMALL_COLS - flat.shape[0])).reshape(rows, SMALL_COLS)


def _unpack(packed, like):
    flat = packed.reshape(-1)
    out, pos = [], 0
    for a in like:
        out.append(flat[pos:pos + a.size].reshape(a.shape))
        pos += a.size
    return out


def kernel(x, w_in, b_gate, norm_mix, norm_ffn, hgrn_lb_logits, hgrn_out_gain, q_gain, k_gain, rel_bias, w_proj_a, w_proj_b, w_out, w_ffn_in, w_ffn_out, loss_target, m_w_in, m_b_gate, m_norm_mix, m_norm_ffn, m_hgrn_lb_logits, m_hgrn_out_gain, m_q_gain, m_k_gain, m_rel_bias, m_w_proj_a, m_w_proj_b, m_w_out, m_w_ffn_in, m_w_ffn_out, v_w_in, v_b_gate, v_norm_mix, v_norm_ffn, v_hgrn_lb_logits, v_hgrn_out_gain, v_q_gain, v_k_gain, v_rel_bias, v_w_proj_a, v_w_proj_b, v_w_out, v_w_ffn_in, v_w_ffn_out):
    xs = x[0]
    target = loss_target[0]
    T, D = xs.shape
    d_a = hgrn_out_gain.shape[-1]
    H = d_a // HEAD
    d_b = d_a
    off_b = 4 * d_a
    off_g = off_b + 3 * d_b
    assert rel_bias.shape[1] == H and T % CHUNK == 0 and T // CHUNK > N_PAST

    big_w = [w_in[0], w_proj_a[0], w_proj_b[0], w_out[0], w_ffn_in[0], w_ffn_out[0]]
    big_m = [m_w_in[0], m_w_proj_a[0], m_w_proj_b[0], m_w_out[0], m_w_ffn_in[0], m_w_ffn_out[0]]
    big_v = [v_w_in[0], v_w_proj_a[0], v_w_proj_b[0], v_w_out[0], v_w_ffn_in[0], v_w_ffn_out[0]]

    g_in, g_pa, g_pb, g_out, g_fin, g_fout = _all_gather([w.astype(bf16) for w in big_w], "gather_weights")
    wg_out = g_out.reshape(-1, g_out.shape[-1])
    wg_fout = g_fout.reshape(-1, g_fout.shape[-1])
    F = wg_fout.shape[0]

    h = _rms_fwd(xs, None, norm_mix, "rms_mix")
    proj = _mm(h, g_in, mode="nn", b_blocked=True, name="mm_proj")
    y_a, states = _hgrn_fwd(proj, hgrn_lb_logits, hgrn_out_gain, H, "hgrn_fwd")
    idx = jnp.asarray(_rel_index())
    rb_pad = jnp.pad(rel_bias[0], ((0, 0), (0, N_REL_PAD - N_REL)))
    bias = _bias_table(rb_pad, idx, "bias_table").reshape(H, CHUNK, BAND * CHUNK)
    y_b = _attn_fwd(proj, q_gain, k_gain, bias, off_b, H, "attn_fwd")
    pa = _mm(y_a, g_pa, mode="nn", b_blocked=True, tm=2048, name="mm_proj_a")
    pb = _mm(y_b, g_pb, mode="nn", b_blocked=True, tm=2048, name="mm_proj_b")
    merged = _merge_fwd(pa, pb, proj, b_gate, off_g, "merge_fwd")
    mo = _mm(merged, wg_out, mode="nn", name="mm_out")
    x1, h2 = _rms_fwd(xs, mo, norm_ffn, "rms_ffn")
    gu = _mm(h2, g_fin, mode="nn", b_blocked=True, name="mm_ffn_in")
    act = _swiglu_fwd(gu, "swiglu_fwd")
    fo = _mm(act, wg_fout, mode="nn", tk=2816, name="mm_ffn_out")
    dy, loss_acc = _loss_head(x1, fo, target, "loss_head")
    loss = lax.psum(loss_acc[0, 0] * (0.5 / D), AXES)

    dact = _mm(dy, wg_fout, mode="nt", tn=1408, name="mm_d_act")
    gw_fout = _mm(act, dy, mode="tn", out_dtype=bf16, tm=1408, name="mm_gw_ffn_out")
    dgate, dup = _swiglu_bwd(gu, dact, "swiglu_bwd")
    dgu = jnp.concatenate([dgate, dup], axis=1)
    gw_fin = _mm(h2, dgu, mode="tn", out_blocked=True, out_dtype=bf16, tn=g_fin.shape[-1], name="mm_gw_ffn_in")
    dh2 = _mm(dgu, g_fin, mode="nt", b_blocked=True, name="mm_d_h2")
    dx1, g_norm_ffn = _rms_bwd(x1, norm_ffn, dh2, dy, "rms_ffn_bwd")

    dmerged = _mm(dx1, wg_out, mode="nt", name="mm_d_merged")
    gw_out = _mm(merged, dx1, mode="tn", out_dtype=bf16, name="mm_gw_out")
    dpa, dgl_a, gb_a = _branch_bwd(dmerged, pa, proj, b_gate, off_g, 0, "branch_a_bwd")
    dpb, dgl_b, gb_b = _branch_bwd(dmerged, pb, proj, b_gate, off_g + D, D, "branch_b_bwd")
    dy_a = _mm(dpa, g_pa, mode="nt", b_blocked=True, tm=2048, name="mm_d_ya")
    dy_b = _mm(dpb, g_pb, mode="nt", b_blocked=True, tm=2048, name="mm_d_yb")
    gw_pa = _mm(y_a, dpa, mode="tn", out_blocked=True, out_dtype=bf16, tn=g_pa.shape[-1], name="mm_gw_proj_a")
    gw_pb = _mm(y_b, dpb, mode="tn", out_blocked=True, out_dtype=bf16, tn=g_pb.shape[-1], name="mm_gw_proj_b")

    dq_a, df_a, di_a, dg_a, g_logits, g_gain = _hgrn_bwd(proj, hgrn_lb_logits, hgrn_out_gain, states, dy_a, H, "hgrn_bwd")
    dq_b, dk_b, dv_b, dbias, g_qg, g_kg = _attn_bwd(proj, q_gain, k_gain, bias, dy_b, off_b, H, "attn_bwd")
    g_rel = _bias_table_bwd(dbias.reshape(H, -1), idx, "bias_table_bwd")[:, :N_REL]
    dproj = jnp.concatenate([dq_a, df_a, di_a, dg_a, dq_b, dk_b, dv_b, dgl_a, dgl_b], axis=1)
    gw_in = _mm(h, dproj, mode="tn", out_blocked=True, out_dtype=bf16, tn=g_in.shape[-1], name="mm_gw_in")
    dh = _mm(dproj, g_in, mode="nt", b_blocked=True, name="mm_d_h")
    grad_x, g_norm_mix = _rms_bwd(xs, norm_mix, dh, dx1, "rms_mix_bwd")

    full = [gw_in, gw_pa, gw_pb, gw_out.reshape(NDEV, -1, D), gw_fin, gw_fout.reshape(NDEV, -1, D)]
    parts = _scatter_parts(full, "scatter_grads")
    col_sharded = [True, True, True, False, True, False]
    names = ["w_in", "w_proj_a", "w_proj_b", "w_out", "w_ffn_in", "w_ffn_out"]
    big = {}
    for nm, w, m, v, p in zip(names, big_w, big_m, big_v, parts):
        big[nm] = [o[None] for o in _adamw_parts(w, m, v, p, "adamw_" + nm)]

    small_names = ["b_gate", "norm_mix", "norm_ffn", "hgrn_lb_logits", "hgrn_out_gain", "q_gain", "k_gain", "rel_bias"]
    small_w = [b_gate, norm_mix, norm_ffn, hgrn_lb_logits, hgrn_out_gain, q_gain, k_gain, rel_bias]
    small_m = [m_b_gate, m_norm_mix, m_norm_ffn, m_hgrn_lb_logits, m_hgrn_out_gain, m_q_gain, m_k_gain, m_rel_bias]
    small_v = [v_b_gate, v_norm_mix, v_norm_ffn, v_hgrn_lb_logits, v_hgrn_out_gain, v_q_gain, v_k_gain, v_rel_bias]
    small_g = [jnp.concatenate([gb_a, gb_b], axis=1), g_norm_mix, g_norm_ffn, g_logits, g_gain, g_qg, g_kg, g_rel[None]]
    g_sum = _small_all_reduce(_pack(small_g), "reduce_small")
    d_s, m_s, v_s = _adamw_small(_pack(small_w), g_sum, _pack(small_m), _pack(small_v), "adamw_small")
    small = {}
    for nm, g, d, m, v in zip(small_names, _unpack(g_sum, small_w), _unpack(d_s, small_w), _unpack(m_s, small_w), _unpack(v_s, small_w)):
        small[nm] = [g, d, m, v]

    order = ["w_in", "b_gate", "norm_mix", "norm_ffn", "hgrn_lb_logits", "hgrn_out_gain", "q_gain", "k_gain", "rel_bias",
             "w_proj_a", "w_proj_b", "w_out", "w_ffn_in", "w_ffn_out"]
    res = {**big, **small}
    outs = [loss, grad_x[None]]
    for k in range(4):
        outs += [res[nm][k] for nm in order]
    return tuple(outs)
```

```python
import functools

import numpy as np
import jax
import jax.numpy as jnp
from jax import lax
from jax.experimental import pallas as pl
from jax.experimental.pallas import tpu as pltpu

f32 = jnp.float32
bf16 = jnp.bfloat16
HI = lax.Precision.HIGHEST
MESH = pl.DeviceIdType.MESH
AXES = ("x", "y", "c")
NDEV = 8

CHUNK = 64
HEAD = 128
SUB = 16
N_PAST = 8
BAND = N_PAST + 1
PAD = N_PAST * CHUNK
REL_FUTURE = CHUNK - 1
REL_PAST = 2 * CHUNK - 1
N_REL = REL_FUTURE + REL_PAST + 1
N_REL_PAD = 256
EPS = 1e-6
NEG = -1e30

ADAM_LR = 0.001
ADAM_B1 = 0.9
ADAM_B2 = 0.999
ADAM_EPS = 1e-08
ADAM_WD = 0.01
ADAM_STEP = 10

VMEM_LIMIT = 56 * 1024 * 1024


def _params(sem=None):
    return pltpu.CompilerParams(dimension_semantics=sem, vmem_limit_bytes=VMEM_LIMIT)


def _tile(n, pref, unit=128):
    if n <= pref:
        return n
    t = (pref // unit) * unit
    while t >= unit:
        if n % t == 0:
            return t
        t -= unit
    return n


_sigmoid = jax.nn.sigmoid


def _mm(a, b, *, mode, name, b_blocked=False, out_blocked=False, out_dtype=f32, tm=1024, tn=1024, tk=2048):
    if mode == "tn":
        K, M = a.shape
    else:
        M, K = a.shape
    if b_blocked:
        nb, mid, cb = b.shape
        if mode == "nn":
            assert mid == K
            N, tn = nb * cb, cb
        else:
            assert mode == "nt" and nb * cb == K
            N, tk = mid, cb
    else:
        N = b.shape[1] if mode in ("nn", "tn") else b.shape[0]
    tm = _tile(M, tm)
    tn = tn if (b_blocked and mode == "nn") or out_blocked else _tile(N, tn)
    tk = tk if b_blocked and mode == "nt" else _tile(K, tk)
    assert M % tm == 0 and N % tn == 0 and K % tk == 0
    nk = K // tk
    grid = (M // tm, N // tn, nk)
    if mode == "tn":
        a_spec = pl.BlockSpec((tk, tm), lambda i, j, k: (k, i))
    else:
        a_spec = pl.BlockSpec((tm, tk), lambda i, j, k: (i, k))
    if mode == "nn":
        b_spec = pl.BlockSpec((None, tk, cb), lambda i, j, k: (j, k, 0)) if b_blocked else pl.BlockSpec((tk, tn), lambda i, j, k: (k, j))
    elif mode == "nt":
        b_spec = pl.BlockSpec((None, tn, cb), lambda i, j, k: (k, j, 0)) if b_blocked else pl.BlockSpec((tn, tk), lambda i, j, k: (j, k))
    else:
        b_spec = pl.BlockSpec((tk, tn), lambda i, j, k: (k, j))
    if out_blocked:
        out_shape = jax.ShapeDtypeStruct((N // tn, M, tn), out_dtype)
        o_spec = pl.BlockSpec((None, tm, tn), lambda i, j, k: (j, i, 0))
    else:
        out_shape = jax.ShapeDtypeStruct((M, N), out_dtype)
        o_spec = pl.BlockSpec((tm, tn), lambda i, j, k: (i, j))
    dims = {"nn": ((1,), (0,)), "nt": ((1,), (1,)), "tn": ((0,), (0,))}[mode]

    def body(a_ref, b_ref, o_ref, *acc):
        p = lax.dot_general(a_ref[...].astype(bf16), b_ref[...].astype(bf16), (dims, ((), ())), preferred_element_type=f32)
        if nk == 1:
            o_ref[...] = p.astype(out_dtype)
        else:
            acc_ref = acc[0]
            k = pl.program_id(2)

            @pl.when(k == 0)
            def _():
                acc_ref[...] = p

            @pl.when(k > 0)
            def _():
                acc_ref[...] += p

            @pl.when(k == nk - 1)
            def _():
                o_ref[...] = acc_ref[...].astype(out_dtype)

    return pl.pallas_call(
        body, name=name, grid=grid, in_specs=[a_spec, b_spec], out_specs=o_spec, out_shape=out_shape,
        scratch_shapes=[pltpu.VMEM((tm, tn), f32)] if nk > 1 else [],
        compiler_params=_params(("parallel", "parallel", "arbitrary")),
    )(a, b)


def _rms_fwd(x, res, gain, name, after=()):
    T, D = x.shape
    tr = _tile(T, 256, 8)
    row = pl.BlockSpec((tr, D), lambda i: (i, 0))
    vec = pl.BlockSpec((1, D), lambda i: (0, 0))
    tok = pl.BlockSpec((8, 128), lambda i: (0, 0))

    def body(*refs):
        if res is None:
            x_ref, g_ref, h_ref = refs[0], refs[1], refs[-1]
            xs = x_ref[...]
        else:
            x_ref, r_ref, g_ref, xs_ref, h_ref = refs
            xs = x_ref[...] + r_ref[...]
            xs_ref[...] = xs
        r = lax.rsqrt(jnp.mean(xs * xs, axis=-1, keepdims=True) + EPS)
        h_ref[...] = (xs * r * g_ref[...]).astype(bf16)

    h_shape = jax.ShapeDtypeStruct((T, D), bf16)
    if res is None:
        return pl.pallas_call(body, name=name, grid=(T // tr,), in_specs=[row, vec] + [tok] * len(after), out_specs=row, out_shape=h_shape,
                              compiler_params=_params(("parallel",)))(x, gain, *after)
    return pl.pallas_call(body, name=name, grid=(T // tr,), in_specs=[row, row, vec], out_specs=(row, row),
                          out_shape=(jax.ShapeDtypeStruct((T, D), f32), h_shape), compiler_params=_params(("parallel",)))(x, res, gain)


def _rms_bwd(xs, gain, dh, extra, name):
    T, D = xs.shape
    tr = _tile(T, 256, 8)
    row = pl.BlockSpec((tr, D), lambda i: (i, 0))
    vec = pl.BlockSpec((1, D), lambda i: (0, 0))

    def body(x_ref, g_ref, dh_ref, e_ref, dx_ref, dg_ref):
        x = x_ref[...]
        r = lax.rsqrt(jnp.mean(x * x, axis=-1, keepdims=True) + EPS)
        xhat = x * r
        dh_v = dh_ref[...]
        gd = dh_v * g_ref[...]
        dx_ref[...] = e_ref[...] + r * (gd - xhat * jnp.mean(gd * xhat, axis=-1, keepdims=True))
        part = jnp.sum(dh_v * xhat, axis=0, keepdims=True)

        @pl.when(pl.program_id(0) == 0)
        def _():
            dg_ref[...] = part

        @pl.when(pl.program_id(0) > 0)
        def _():
            dg_ref[...] += part

    return pl.pallas_call(body, name=name, grid=(T // tr,), in_specs=[row, vec, row, row], out_specs=(row, vec),
                          out_shape=(jax.ShapeDtypeStruct((T, D), f32), jax.ShapeDtypeStruct((1, D), f32)),
                          compiler_params=_params(("arbitrary",)))(xs, gain, dh, extra)


def _merge_fwd(pa, pb, proj, b_gate, off, name):
    T, D = pa.shape
    tr, tc = _tile(T, 512, 8), _tile(D, 512)
    oa, ob, nb = off // tc, (off + D) // tc, D // tc
    blk = pl.BlockSpec((tr, tc), lambda i, j: (i, j))

    def body(pa_ref, pb_ref, ga_ref, gb_ref, ba_ref, bb_ref, o_ref):
        ga = _sigmoid(ga_ref[...] + ba_ref[...])
        gb = _sigmoid(gb_ref[...] + bb_ref[...])
        o_ref[...] = (ga * pa_ref[...] + gb * pb_ref[...]).astype(bf16)

    return pl.pallas_call(
        body, name=name, grid=(T // tr, nb),
        in_specs=[blk, blk, pl.BlockSpec((tr, tc), lambda i, j: (i, oa + j)), pl.BlockSpec((tr, tc), lambda i, j: (i, ob + j)),
                  pl.BlockSpec((1, tc), lambda i, j: (0, j)), pl.BlockSpec((1, tc), lambda i, j: (0, nb + j))],
        out_specs=blk, out_shape=jax.ShapeDtypeStruct((T, D), bf16), compiler_params=_params(("parallel", "parallel")),
    )(pa, pb, proj, proj, b_gate, b_gate)


def _branch_bwd(dm, p, proj, b_gate, off, boff, name):
    T, D = p.shape
    tr, tc = _tile(T, 512, 8), _tile(D, 512)
    og, obias = off // tc, boff // tc
    blk = pl.BlockSpec((tr, tc), lambda j, i: (i, j))
    vec = pl.BlockSpec((1, tc), lambda j, i: (0, j))

    def body(dm_ref, p_ref, gl_ref, b_ref, dp_ref, dgl_ref, db_ref):
        g = _sigmoid(gl_ref[...] + b_ref[...])
        dm_v = dm_ref[...]
        dp_ref[...] = (dm_v * g).astype(bf16)
        dgl = dm_v * p_ref[...] * g * (1.0 - g)
        dgl_ref[...] = dgl.astype(bf16)
        part = jnp.sum(dgl, axis=0, keepdims=True)

        @pl.when(pl.program_id(1) == 0)
        def _():
            db_ref[...] = part

        @pl.when(pl.program_id(1) > 0)
        def _():
            db_ref[...] += part

    return pl.pallas_call(
        body, name=name, grid=(D // tc, T // tr),
        in_specs=[blk, blk, pl.BlockSpec((tr, tc), lambda j, i: (i, og + j)), pl.BlockSpec((1, tc), lambda j, i: (0, obias + j))],
        out_specs=(blk, blk, vec),
        out_shape=(jax.ShapeDtypeStruct((T, D), bf16), jax.ShapeDtypeStruct((T, D), bf16), jax.ShapeDtypeStruct((1, D), f32)),
        compiler_params=_params(("parallel", "arbitrary")),
    )(dm, p, proj, b_gate)


def _swiglu_fwd(gu, name):
    T, F2 = gu.shape
    F = F2 // 2
    tr, tc = _tile(T, 512, 8), _tile(F, 512)
    nb = F // tc

    def body(g_ref, u_ref, o_ref):
        g = g_ref[...]
        o_ref[...] = (g * _sigmoid(g) * u_ref[...]).astype(bf16)

    return pl.pallas_call(
        body, name=name, grid=(T // tr, nb),
        in_specs=[pl.BlockSpec((tr, tc), lambda i, j: (i, j)), pl.BlockSpec((tr, tc), lambda i, j: (i, nb + j))],
        out_specs=pl.BlockSpec((tr, tc), lambda i, j: (i, j)), out_shape=jax.ShapeDtypeStruct((T, F), bf16),
        compiler_params=_params(("parallel", "parallel")),
    )(gu, gu)


def _swiglu_bwd(gu, dact, name):
    T, F2 = gu.shape
    F = F2 // 2
    tr, tc = _tile(T, 512, 8), _tile(F, 512)
    nb = F // tc
    blk = pl.BlockSpec((tr, tc), lambda i, j: (i, j))

    def body(g_ref, u_ref, d_ref, dg_ref, du_ref):
        g = g_ref[...]
        s = _sigmoid(g)
        d = d_ref[...]
        dg_ref[...] = (d * u_ref[...] * s * (1.0 + g * (1.0 - s))).astype(bf16)
        du_ref[...] = (d * g * s).astype(bf16)

    return pl.pallas_call(
        body, name=name, grid=(T // tr, nb),
        in_specs=[blk, pl.BlockSpec((tr, tc), lambda i, j: (i, nb + j)), blk], out_specs=(blk, blk),
        out_shape=(jax.ShapeDtypeStruct((T, F), bf16), jax.ShapeDtypeStruct((T, F), bf16)),
        compiler_params=_params(("parallel", "parallel")),
    )(gu, gu, dact)


def _loss_head(x1, fo, target, name):
    T, D = x1.shape
    tr = _tile(T, 256, 8)
    row = pl.BlockSpec((tr, D), lambda i: (i, 0))
    acc = pl.BlockSpec((8, 128), lambda i: (0, 0))

    def body(x_ref, f_ref, t_ref, dy_ref, l_ref):
        d = x_ref[...] + f_ref[...] - t_ref[...]
        dy_ref[...] = d * (1.0 / D)
        part = jnp.sum(jnp.sum(d * d, axis=1, keepdims=True), axis=0, keepdims=True)

        @pl.when(pl.program_id(0) == 0)
        def _():
            l_ref[...] = jnp.zeros((8, 128), f32)

        l_ref[...] += part

    return pl.pallas_call(body, name=name, grid=(T // tr,), in_specs=[row, row, row], out_specs=(row, acc),
                          out_shape=(jax.ShapeDtypeStruct((T, D), f32), jax.ShapeDtypeStruct((8, 128), f32)),
                          compiler_params=_params(("arbitrary",)))(x1, fo, target)


def _dotb(a, b, dims):
    return lax.dot_general(a.astype(bf16), b.astype(bf16), (dims, ((), ())), preferred_element_type=f32)


def _hgrn_chunk(q, fl, iv, g, logits, gain, st):
    lb = jax.nn.softmax(logits, axis=0)[0:1]
    f = lb + (1.0 - lb) * _sigmoid(fl)
    lf = jnp.log(f)
    kk = 1.0 - f
    qs = q * _sigmoid(q)
    row = lax.broadcasted_iota(jnp.int32, (CHUNK, CHUNK), 0)
    col = lax.broadcasted_iota(jnp.int32, (CHUNK, CHUNK), 1)
    tril = (col <= row).astype(f32)
    b = lax.dot_general(tril, lf, (((1,), (0,)), ((), ())), precision=HI, preferred_element_type=f32)
    b_last = jnp.sum(lf, axis=0, keepdims=True)
    o = _dotb(qs * jnp.exp(b), st, ((1,), (1,)))
    r3 = lax.broadcasted_iota(jnp.int32, (SUB, SUB, HEAD), 0)
    c3 = lax.broadcasted_iota(jnp.int32, (SUB, SUB, HEAD), 1)
    parts = []
    for i in range(CHUNK // SUB):
        lo, hi = i * SUB, (i + 1) * SUB
        bi = b[lo:hi]
        dec = jnp.exp(jnp.where(c3 <= r3, bi[:, None, :] - bi[None, :, :], -jnp.inf))
        s = jnp.sum(qs[lo:hi][:, None, :] * kk[lo:hi][None, :, :] * dec, axis=-1)
        if i > 0:
            anchor = jnp.max(bi, axis=0, keepdims=True)
            qa = qs[lo:hi] * jnp.exp(bi - anchor)
            kd = kk[:lo] * jnp.exp(anchor - b[:lo])
            s = jnp.concatenate([_dotb(qa, kd, ((1,), (1,))), s], axis=1)
        parts.append(_dotb(s, iv[:hi], ((1,), (0,))))
    o = o + jnp.concatenate(parts, axis=0)
    st_new = st * jnp.exp(b_last) + _dotb(iv, kk * jnp.exp(b_last - b), ((0,), (0,)))
    o = o * lax.rsqrt(jnp.mean(o * o, axis=-1, keepdims=True) + EPS)
    o = o * gain * (g * _sigmoid(g))
    return o, st_new


def _hgrn_fwd(proj, logits, gain, n_heads, name):
    T = proj.shape[0]
    nc = T // CHUNK
    H = n_heads

    def col(k):
        return pl.BlockSpec((CHUNK, HEAD), lambda h, c: (c, k * H + h))

    def body(q_ref, f_ref, i_ref, g_ref, l_ref, ga_ref, y_ref, s_ref, st):
        @pl.when(pl.program_id(1) == 0)
        def _():
            st[...] = jnp.zeros((HEAD, HEAD), f32)

        s_ref[...] = st[...]
        o, st_new = _hgrn_chunk(q_ref[...], f_ref[...], i_ref[...], g_ref[...], l_ref[...], ga_ref[...], st[...])
        y_ref[...] = o.astype(bf16)
        st[...] = st_new

    return pl.pallas_call(
        body, name=name, grid=(H, nc),
        in_specs=[col(0), col(1), col(2), col(3), pl.BlockSpec((2, HEAD), lambda h, c: (0, h)), pl.BlockSpec((1, HEAD), lambda h, c: (0, h))],
        out_specs=(pl.BlockSpec((CHUNK, HEAD), lambda h, c: (c, h)), pl.BlockSpec((None, None, HEAD, HEAD), lambda h, c: (h, c, 0, 0))),
        out_shape=(jax.ShapeDtypeStruct((T, H * HEAD), bf16), jax.ShapeDtypeStruct((H, nc, HEAD, HEAD), f32)),
        scratch_shapes=[pltpu.VMEM((HEAD, HEAD), f32)],
        compiler_params=_params(("parallel", "arbitrary")),
    )(proj, proj, proj, proj, logits, gain)


def _hgrn_bwd(proj, logits, gain, states, dy, n_heads, name):
    T = proj.shape[0]
    nc = T // CHUNK
    H = n_heads

    def col(k):
        return pl.BlockSpec((CHUNK, HEAD), lambda h, c: (nc - 1 - c, k * H + h))

    out_blk = pl.BlockSpec((CHUNK, HEAD), lambda h, c: (nc - 1 - c, h))

    def body(q_ref, f_ref, i_ref, g_ref, l_ref, ga_ref, s_ref, dy_ref, dq_ref, df_ref, di_ref, dg_ref, dl_ref, dga_ref, dst):
        first = pl.program_id(1) == 0

        @pl.when(first)
        def _():
            dst[...] = jnp.zeros((HEAD, HEAD), f32)

        _, vjp = jax.vjp(_hgrn_chunk, q_ref[...], f_ref[...], i_ref[...], g_ref[...], l_ref[...], ga_ref[...], s_ref[...])
        dq, df, di, dg, dl, dga, ds = vjp((dy_ref[...], dst[...]))
        dq_ref[...] = dq.astype(bf16)
        df_ref[...] = df.astype(bf16)
        di_ref[...] = di.astype(bf16)
        dg_ref[...] = dg.astype(bf16)
        dst[...] = ds

        @pl.when(first)
        def _():
            dl_ref[...] = dl
            dga_ref[...] = dga

        @pl.when(jnp.logical_not(first))
        def _():
            dl_ref[...] += dl
            dga_ref[...] += dga

    act = jax.ShapeDtypeStruct((T, H * HEAD), bf16)
    return pl.pallas_call(
        body, name=name, grid=(H, nc),
        in_specs=[col(0), col(1), col(2), col(3), pl.BlockSpec((2, HEAD), lambda h, c: (0, h)), pl.BlockSpec((1, HEAD), lambda h, c: (0, h)),
                  pl.BlockSpec((None, None, HEAD, HEAD), lambda h, c: (h, nc - 1 - c, 0, 0)), out_blk],
        out_specs=(out_blk, out_blk, out_blk, out_blk, pl.BlockSpec((2, HEAD), lambda h, c: (0, h)), pl.BlockSpec((1, HEAD), lambda h, c: (0, h))),
        out_shape=(act, act, act, act, jax.ShapeDtypeStruct((2, H * HEAD), f32), jax.ShapeDtypeStruct((1, H * HEAD), f32)),
        scratch_shapes=[pltpu.VMEM((HEAD, HEAD), f32)],
        compiler_params=_params(("parallel", "arbitrary")),
    )(proj, proj, proj, proj, logits, gain, states, dy)


def _rel_index():
    t = np.arange(CHUNK)[:, None]
    sp = np.arange(BAND * CHUNK)[None, :]
    dist = (N_PAST - sp // CHUNK) * CHUNK + t - sp % CHUNK
    return (np.clip(dist, -REL_FUTURE, REL_PAST) + REL_FUTURE).reshape(1, -1).astype(np.int32)


def _bias_table(rel_bias_pad, idx, name):
    H = rel_bias_pad.shape[0]
    n = idx.shape[1]
    tc = _tile(n, 4096)

    def body(rb_ref, idx_ref, o_ref):
        onehot = (lax.broadcasted_iota(jnp.int32, (N_REL_PAD, tc), 0) == idx_ref[...]).astype(f32)
        o_ref[...] = lax.dot_general(rb_ref[...], onehot, (((1,), (0,)), ((), ())), precision=HI, preferred_element_type=f32)

    return pl.pallas_call(
        body, name=name, grid=(n // tc,),
        in_specs=[pl.BlockSpec((H, N_REL_PAD), lambda j: (0, 0)), pl.BlockSpec((1, tc), lambda j: (0, j))],
        out_specs=pl.BlockSpec((H, tc), lambda j: (0, j)), out_shape=jax.ShapeDtypeStruct((H, n), f32),
        compiler_params=_params(("parallel",)),
    )(rel_bias_pad, idx)


def _bias_table_bwd(dbias, idx, name):
    H, n = dbias.shape
    tc = _tile(n, 4096)

    def body(d_ref, idx_ref, o_ref):
        onehot = (lax.broadcasted_iota(jnp.int32, (N_REL_PAD, tc), 0) == idx_ref[...]).astype(f32)
        part = lax.dot_general(d_ref[...], onehot, (((1,), (1,)), ((), ())), precision=HI, preferred_element_type=f32)

        @pl.when(pl.program_id(0) == 0)
        def _():
            o_ref[...] = part

        @pl.when(pl.program_id(0) > 0)
        def _():
            o_ref[...] += part

    return pl.pallas_call(
        body, name=name, grid=(n // tc,),
        in_specs=[pl.BlockSpec((H, tc), lambda j: (0, j)), pl.BlockSpec((1, tc), lambda j: (0, j))],
        out_specs=pl.BlockSpec((H, N_REL_PAD), lambda j: (0, 0)), out_shape=jax.ShapeDtypeStruct((H, N_REL_PAD), f32),
        compiler_params=_params(("arbitrary",)),
    )(dbias, idx)


def _head_norm(t, gain):
    return t * lax.rsqrt(jnp.mean(t * t, axis=-1, keepdims=True) + EPS) * gain


def _attn_chunk(q, kb, vb, qg, bias, n):
    qh = _head_norm(q, qg)
    s = _dotb(qh, kb, ((1,), (1,))) * (HEAD ** -0.5) + bias
    pos = n * CHUNK - PAD + lax.broadcasted_iota(jnp.int32, (1, BAND * CHUNK), 1)
    s = jnp.where(pos >= 0, s, NEG)
    e = jnp.exp(s - jnp.max(s, axis=-1, keepdims=True))
    p = e / jnp.sum(e, axis=-1, keepdims=True)
    return _dotb(p, vb, ((1,), (0,)))


def _attn_fwd(proj, q_gain, k_gain, bias, off, n_heads, name):
    T = proj.shape[0]
    nc = T // CHUNK
    H = n_heads
    o0 = off // HEAD
    full = lambda k: pl.BlockSpec((T, HEAD), lambda h, c: (0, o0 + k * H + h))
    vec = pl.BlockSpec((1, HEAD), lambda h, c: (0, 0))

    def body(q_ref, k_ref, v_ref, qg_ref, kg_ref, b_ref, y_ref, kp, vp):
        c = pl.program_id(1)

        @pl.when(c == 0)
        def _():
            kp[pl.ds(0, PAD), :] = jnp.zeros((PAD, HEAD), f32)
            vp[pl.ds(0, PAD), :] = jnp.zeros((PAD, HEAD), f32)
            kp[pl.ds(PAD, T), :] = _head_norm(k_ref[...], kg_ref[...])
            vp[pl.ds(PAD, T), :] = v_ref[...]

        band = pl.ds(pl.multiple_of(c * CHUNK, CHUNK), BAND * CHUNK)
        y_ref[...] = _attn_chunk(q_ref[...], kp[band, :], vp[band, :], qg_ref[...], b_ref[...], c).astype(bf16)

    return pl.pallas_call(
        body, name=name, grid=(H, nc),
        in_specs=[pl.BlockSpec((CHUNK, HEAD), lambda h, c: (c, o0 + h)), full(1), full(2), vec, vec,
                  pl.BlockSpec((None, CHUNK, BAND * CHUNK), lambda h, c: (h, 0, 0))],
        out_specs=pl.BlockSpec((CHUNK, HEAD), lambda h, c: (c, h)), out_shape=jax.ShapeDtypeStruct((T, H * HEAD), bf16),
        scratch_shapes=[pltpu.VMEM((T + PAD, HEAD), f32), pltpu.VMEM((T + PAD, HEAD), f32)],
        compiler_params=_params(("parallel", "arbitrary")),
    )(proj, proj, proj, q_gain, k_gain, bias)


def _attn_bwd(proj, q_gain, k_gain, bias, dy, off, n_heads, name):
    T = proj.shape[0]
    nc = T // CHUNK
    H = n_heads
    o0 = off // HEAD
    full = lambda k: pl.BlockSpec((T, HEAD), lambda h, c: (0, o0 + k * H + h))
    full_out = pl.BlockSpec((T, HEAD), lambda h, c: (0, h))
    vec = pl.BlockSpec((1, HEAD), lambda h, c: (0, 0))
    chunk_out = pl.BlockSpec((CHUNK, HEAD), lambda h, c: (c, h))
    bias_blk = pl.BlockSpec((None, CHUNK, BAND * CHUNK), lambda h, c: (h, 0, 0))

    def body(q_ref, k_ref, v_ref, qg_ref, kg_ref, b_ref, dy_ref, dq_ref, dk_ref, dv_ref, db_ref, dqg_ref, dkg_ref, kp, vp, dkp, dvp):
        h = pl.program_id(0)
        c = pl.program_id(1)

        @pl.when(c == 0)
        def _():
            kp[pl.ds(0, PAD), :] = jnp.zeros((PAD, HEAD), f32)
            vp[pl.ds(0, PAD), :] = jnp.zeros((PAD, HEAD), f32)
            kp[pl.ds(PAD, T), :] = _head_norm(k_ref[...], kg_ref[...])
            vp[pl.ds(PAD, T), :] = v_ref[...]
            dkp[...] = jnp.zeros((T + PAD, HEAD), f32)
            dvp[...] = jnp.zeros((T + PAD, HEAD), f32)
            db_ref[...] = jnp.zeros((CHUNK, BAND * CHUNK), f32)

        @pl.when(jnp.logical_and(h == 0, c == 0))
        def _():
            dqg_ref[...] = jnp.zeros((1, HEAD), f32)
            dkg_ref[...] = jnp.zeros((1, HEAD), f32)

        band = pl.ds(pl.multiple_of(c * CHUNK, CHUNK), BAND * CHUNK)
        _, vjp = jax.vjp(functools.partial(_attn_chunk, n=c), q_ref[...], kp[band, :], vp[band, :], qg_ref[...], b_ref[...])
        dq, dkb, dvb, dqg, db = vjp(dy_ref[...])
        dq_ref[...] = dq.astype(bf16)
        dkp[band, :] += dkb
        dvp[band, :] += dvb
        db_ref[...] += db
        dqg_ref[...] += dqg

        @pl.when(c == nc - 1)
        def _():
            _, nvjp = jax.vjp(_head_norm, k_ref[...], kg_ref[...])
            dk, dkg = nvjp(dkp[pl.ds(PAD, T), :])
            dk_ref[...] = dk.astype(bf16)
            dv_ref[...] = dvp[pl.ds(PAD, T), :].astype(bf16)
            dkg_ref[...] += dkg

    act = jax.ShapeDtypeStruct((T, H * HEAD), bf16)
    gvec = jax.ShapeDtypeStruct((1, HEAD), f32)
    pad_buf = pltpu.VMEM((T + PAD, HEAD), f32)
    return pl.pallas_call(
        body, name=name, grid=(H, nc),
        in_specs=[pl.BlockSpec((CHUNK, HEAD), lambda h, c: (c, o0 + h)), full(1), full(2), vec, vec, bias_blk, chunk_out],
        out_specs=(chunk_out, full_out, full_out, bias_blk, vec, vec),
        out_shape=(act, act, act, jax.ShapeDtypeStruct((H, CHUNK, BAND * CHUNK), f32), gvec, gvec),
        scratch_shapes=[pad_buf, pad_buf, pad_buf, pad_buf],
        compiler_params=_params(("arbitrary", "arbitrary")),
    )(proj, proj, proj, q_gain, k_gain, bias, dy)


def _position():
    x, y, c = lax.axis_index("x"), lax.axis_index("y"), lax.axis_index("c")
    return x, y, c, 4 * x + 2 * y + c


def _flip(v, bit):
    return 1 - v if bit else v


HBM_SPEC = pl.BlockSpec(memory_space=pltpu.HBM)
SEM_SPEC = pl.BlockSpec(memory_space=pltpu.SEMAPHORE)
ANY_SPEC = pl.BlockSpec(memory_space=pl.ANY)
EFFECT = pltpu.SideEffectType.DATAFLOW_SIDE_EFFECTING


def _in_hbm(a):
    return pltpu.with_memory_space_constraint(a, pltpu.HBM)


def _chips(x, y):
    return [(1 - x, y), (x, 1 - y), (1 - x, 1 - y)]


def _gather_start(shards, name):
    n = len(shards)

    def body(*refs):
        ins, lands = refs[:n], refs[n:2 * n]
        s_sib, r_sib, s_ici, r_ici = refs[2 * n:2 * n + 4]
        token, loc = refs[4 * n + 4], refs[4 * n + 5]
        x, y, c, me = _position()
        mine = [pltpu.make_async_copy(ins[w], lands[w].at[me], loc.at[w]) for w in range(n)]
        for cp in mine:
            cp.start()
        for j, (px, py) in enumerate(_chips(x, y)):
            for w in range(n):
                pltpu.make_async_remote_copy(src_ref=ins[w], dst_ref=lands[w].at[me], send_sem=s_ici.at[3 * w + j], recv_sem=r_ici.at[3 * w + j],
                                             device_id=(px, py, c), device_id_type=MESH).start()
        for w in range(n):
            pltpu.make_async_remote_copy(src_ref=ins[w], dst_ref=lands[w].at[me], send_sem=s_sib.at[w], recv_sem=r_sib.at[w],
                                         device_id=(x, y, 1 - c), device_id_type=MESH).start()
        for cp in mine:
            cp.wait()
        token[...] = jnp.zeros_like(token)

    land_shapes = [(NDEV,) + s.shape for s in shards]
    out = pl.pallas_call(
        body, name=name,
        out_shape=(pltpu.SemaphoreType.DMA((n,)), pltpu.SemaphoreType.DMA((n,)), pltpu.SemaphoreType.DMA((3 * n,)), pltpu.SemaphoreType.DMA((3 * n,)),
                   *[pltpu.HBM(s.shape, s.dtype) for s in shards], *[pltpu.HBM(ls, s.dtype) for ls, s in zip(land_shapes, shards)],
                   jax.ShapeDtypeStruct((8, 128), f32)),
        in_specs=[HBM_SPEC] * (2 * n), out_specs=(SEM_SPEC,) * 4 + (HBM_SPEC,) * (2 * n) + (pl.BlockSpec(memory_space=pltpu.VMEM),),
        input_output_aliases={i: 4 + i for i in range(2 * n)},
        scratch_shapes=[pltpu.SemaphoreType.DMA((n,))],
        compiler_params=pltpu.CompilerParams(has_side_effects=EFFECT),
    )(*[_in_hbm(s) for s in shards], *[_in_hbm(lax.empty(ls, s.dtype)) for ls, s in zip(land_shapes, shards)])
    return out[:4], list(out[4:4 + n]), list(out[4 + n:4 + 2 * n]), out[4 + 2 * n]


def _gather_forward(lands, r_ici, after, name):
    n = len(lands)

    def body(*refs):
        lands_in, r_ici_ref = refs[:n], refs[n]
        s_fwd, r_fwd = refs[2 * n + 2], refs[2 * n + 3]
        x, y, c, me = _position()
        for j, (px, py) in enumerate(_chips(x, y)):
            blk = 4 * px + 2 * py + c
            for w in range(n):
                rows = lands_in[w].at[blk]
                pltpu.make_async_remote_copy(src_ref=rows, dst_ref=rows, send_sem=s_fwd.at[3 * w + j], recv_sem=r_ici_ref.at[3 * w + j],
                                             device_id=(px, py, c), device_id_type=MESH).wait_recv()
                pltpu.make_async_remote_copy(src_ref=rows, dst_ref=rows, send_sem=s_fwd.at[3 * w + j], recv_sem=r_fwd.at[3 * w + j],
                                             device_id=(x, y, 1 - c), device_id_type=MESH).start()

    out = pl.pallas_call(
        body, name=name,
        out_shape=(*[pltpu.HBM(l.shape, l.dtype) for l in lands], pltpu.SemaphoreType.DMA((3 * n,)), pltpu.SemaphoreType.DMA((3 * n,))),
        in_specs=[HBM_SPEC] * n + [SEM_SPEC, ANY_SPEC], out_specs=(HBM_SPEC,) * n + (SEM_SPEC, SEM_SPEC),
        input_output_aliases={i: i for i in range(n)},
        compiler_params=pltpu.CompilerParams(has_side_effects=EFFECT),
    )(*lands, r_ici, after)
    return list(out[:n]), out[n], out[n + 1]


def _gather_wait(shards, lands, sems, s_fwd, r_fwd, name):
    n = len(lands)
    s_sib, r_sib, s_ici, _ = sems

    def body(*refs):
        ins, lands_in = refs[:n], refs[n:2 * n]
        s_sib_r, r_sib_r, s_ici_r, s_fwd_r, r_fwd_r = refs[2 * n:2 * n + 5]
        x, y, c, me = _position()
        sib = (x, y, 1 - c)
        for w in range(n):
            cp = pltpu.make_async_remote_copy(src_ref=ins[w], dst_ref=lands_in[w].at[4 * x + 2 * y + (1 - c)], send_sem=s_sib_r.at[w],
                                              recv_sem=r_sib_r.at[w], device_id=sib, device_id_type=MESH)
            cp.wait_recv()
            cp.wait_send()
        for j, (px, py) in enumerate(_chips(x, y)):
            for w in range(n):
                pltpu.make_async_remote_copy(src_ref=ins[w], dst_ref=lands_in[w].at[me], send_sem=s_ici_r.at[3 * w + j], recv_sem=r_fwd_r.at[3 * w + j],
                                             device_id=(px, py, c), device_id_type=MESH).wait_send()
                mine = lands_in[w].at[4 * px + 2 * py + c]
                theirs = lands_in[w].at[4 * px + 2 * py + (1 - c)]
                fwd = pltpu.make_async_remote_copy(src_ref=mine, dst_ref=theirs, send_sem=s_fwd_r.at[3 * w + j], recv_sem=r_fwd_r.at[3 * w + j],
                                                   device_id=sib, device_id_type=MESH)
                fwd.wait_send()
                fwd.wait_recv()

    out = pl.pallas_call(
        body, name=name, out_shape=tuple(pltpu.HBM(l.shape, l.dtype) for l in lands),
        in_specs=[HBM_SPEC] * (2 * n) + [SEM_SPEC] * 5, out_specs=(HBM_SPEC,) * n,
        input_output_aliases={n + i: i for i in range(n)},
        compiler_params=pltpu.CompilerParams(has_side_effects=EFFECT),
    )(*shards, *lands, s_sib, r_sib, s_ici, s_fwd, r_fwd)
    return list(out)


def _scatter_parts(grads, name):
    n = len(grads)

    def body(*refs):
        ins, outs = refs[:n], refs[n:2 * n]
        send, recv, loc = refs[2 * n:]
        x, y, c, me = _position()
        mine = [pltpu.make_async_copy(ins[w].at[me], outs[w].at[me], loc.at[w]) for w in range(n)]
        for cp in mine:
            cp.start()
        sends, waits = [], []
        for r in range(1, NDEV):
            px, py, pc = _flip(x, r & 4), _flip(y, r & 2), _flip(c, r & 1)
            peer = 4 * px + 2 * py + pc
            for w in range(n):
                sends.append(pltpu.make_async_remote_copy(src_ref=ins[w].at[peer], dst_ref=outs[w].at[me], send_sem=send.at[w, r - 1],
                                                          recv_sem=recv.at[w, r - 1], device_id=(px, py, pc), device_id_type=MESH))
                waits.append(pltpu.make_async_remote_copy(src_ref=ins[w].at[peer], dst_ref=outs[w].at[peer], send_sem=send.at[w, r - 1],
                                                          recv_sem=recv.at[w, r - 1], device_id=(px, py, pc), device_id_type=MESH))
        for cp in sends:
            cp.start()
        for cp in waits:
            cp.wait_recv()
        for cp in sends:
            cp.wait_send()
        for cp in mine:
            cp.wait()

    hbm = pl.BlockSpec(memory_space=pl.ANY)
    return pl.pallas_call(
        body, name=name, in_specs=[hbm] * n, out_specs=tuple([hbm] * n),
        out_shape=tuple(jax.ShapeDtypeStruct(g.shape, g.dtype) for g in grads),
        scratch_shapes=[pltpu.SemaphoreType.DMA((n, 7)), pltpu.SemaphoreType.DMA((n, 7)), pltpu.SemaphoreType.DMA((n,))],
    )(*grads)


def _small_all_reduce(v, name):
    R, C = v.shape

    def body(v_ref, o_ref, buf, send, recv):
        x, y, c, me = _position()
        buf[me] = v_ref[...]
        sends, waits = [], []
        for r in range(1, NDEV):
            px, py, pc = _flip(x, r & 4), _flip(y, r & 2), _flip(c, r & 1)
            peer = 4 * px + 2 * py + pc
            sends.append(pltpu.make_async_remote_copy(src_ref=v_ref, dst_ref=buf.at[me], send_sem=send.at[r - 1], recv_sem=recv.at[r - 1],
                                                      device_id=(px, py, pc), device_id_type=MESH))
            waits.append(pltpu.make_async_remote_copy(src_ref=v_ref, dst_ref=buf.at[peer], send_sem=send.at[r - 1], recv_sem=recv.at[r - 1],
                                                      device_id=(px, py, pc), device_id_type=MESH))
        for cp in sends:
            cp.start()
        for cp in waits:
            cp.wait_recv()
        for cp in sends:
            cp.wait_send()
        acc = buf[0]
        for i in range(1, NDEV):
            acc = acc + buf[i]
        o_ref[...] = acc

    vm = pl.BlockSpec(memory_space=pltpu.VMEM)
    return pl.pallas_call(
        body, name=name, in_specs=[vm], out_specs=vm, out_shape=jax.ShapeDtypeStruct((R, C), f32),
        scratch_shapes=[pltpu.VMEM((NDEV, R, C), f32), pltpu.SemaphoreType.DMA((7,)), pltpu.SemaphoreType.DMA((7,))],
    )(v)


def _adamw_math(w, g, m, v):
    m = ADAM_B1 * m + (1.0 - ADAM_B1) * g
    v = ADAM_B2 * v + (1.0 - ADAM_B2) * (g * g)
    m_hat = m / (1.0 - ADAM_B1 ** ADAM_STEP)
    v_hat = v / (1.0 - ADAM_B2 ** ADAM_STEP)
    delta = -ADAM_LR * (m_hat / (jnp.sqrt(v_hat) + ADAM_EPS) + ADAM_WD * w)
    return delta, m, v


def _adamw_parts(w, m, v, parts, name):
    R, C = w.shape
    tr = _tile(R, 128, 16)
    blk = pl.BlockSpec((tr, C), lambda i: (i, 0))

    def body(w_ref, m_ref, v_ref, p_ref, g_ref, d_ref, mo_ref, vo_ref):
        g = p_ref[0].astype(f32)
        for i in range(1, NDEV):
            g = g + p_ref[i].astype(f32)
        d, mn, vn = _adamw_math(w_ref[...], g, m_ref[...], v_ref[...])
        g_ref[...] = g
        d_ref[...] = d
        mo_ref[...] = mn
        vo_ref[...] = vn

    shp = jax.ShapeDtypeStruct((R, C), f32)
    return pl.pallas_call(
        body, name=name, grid=(R // tr,), in_specs=[blk, blk, blk, pl.BlockSpec((NDEV, tr, C), lambda i: (0, i, 0))],
        out_specs=(blk, blk, blk, blk), out_shape=(shp, shp, shp, shp), compiler_params=_params(("parallel",)),
    )(w, m, v, parts)


def _adamw_small(w, g, m, v, name):
    def body(w_ref, g_ref, m_ref, v_ref, d_ref, mo_ref, vo_ref):
        d, mn, vn = _adamw_math(w_ref[...], g_ref[...], m_ref[...], v_ref[...])
        d_ref[...] = d
        mo_ref[...] = mn
        vo_ref[...] = vn

    shp = jax.ShapeDtypeStruct(w.shape, f32)
    return pl.pallas_call(body, name=name, out_shape=(shp, shp, shp))(w, g, m, v)


SMALL_COLS = 1024


def _pack(arrs):
    flat = jnp.concatenate([a.reshape(-1) for a in arrs])
    rows = -(-flat.shape[0] // (8 * SMALL_COLS)) * 8
    return jnp.pad(flat, (0, rows * SMALL_COLS - flat.shape[0])).reshape(rows, SMALL_COLS)


def _unpack(packed, like):
    flat = packed.reshape(-1)
    out, pos = [], 0
    for a in like:
        out.append(flat[pos:pos + a.size].reshape(a.shape))
        pos += a.size
    return out


def kernel(x, w_in, b_gate, norm_mix, norm_ffn, hgrn_lb_logits, hgrn_out_gain, q_gain, k_gain, rel_bias, w_proj_a, w_proj_b, w_out, w_ffn_in, w_ffn_out, loss_target, m_w_in, m_b_gate, m_norm_mix, m_norm_ffn, m_hgrn_lb_logits, m_hgrn_out_gain, m_q_gain, m_k_gain, m_rel_bias, m_w_proj_a, m_w_proj_b, m_w_out, m_w_ffn_in, m_w_ffn_out, v_w_in, v_b_gate, v_norm_mix, v_norm_ffn, v_hgrn_lb_logits, v_hgrn_out_gain, v_q_gain, v_k_gain, v_rel_bias, v_w_proj_a, v_w_proj_b, v_w_out, v_w_ffn_in, v_w_ffn_out):
    xs = x[0]
    target = loss_target[0]
    T, D = xs.shape
    d_a = hgrn_out_gain.shape[-1]
    H = d_a // HEAD
    d_b = d_a
    off_b = 4 * d_a
    off_g = off_b + 3 * d_b
    assert rel_bias.shape[1] == H and T % CHUNK == 0 and T // CHUNK > N_PAST

    big_w = [w_in[0], w_proj_a[0], w_proj_b[0], w_out[0], w_ffn_in[0], w_ffn_out[0]]
    big_m = [m_w_in[0], m_w_proj_a[0], m_w_proj_b[0], m_w_out[0], m_w_ffn_in[0], m_w_ffn_out[0]]
    big_v = [v_w_in[0], v_w_proj_a[0], v_w_proj_b[0], v_w_out[0], v_w_ffn_in[0], v_w_ffn_out[0]]

    sh = [w.astype(bf16) for w in big_w]
    sem_a, sh_a, land_a, tok_a = _gather_start(sh[0:1], "gather_start_a")
    sem_b, sh_b, land_b, tok_b = _gather_start(sh[1:4], "gather_start_b")
    sem_c, sh_c, land_c, tok_c = _gather_start(sh[4:6], "gather_start_c")

    h = _rms_fwd(xs, None, norm_mix, "rms_mix", after=(tok_a, tok_b, tok_c))
    land_a, sf_a, rf_a = _gather_forward(land_a, sem_a[3], h, "gather_forward_a")
    (g_in,) = _gather_wait(sh_a, land_a, sem_a, sf_a, rf_a, "gather_wait_a")
    proj = _mm(h, g_in, mode="nn", b_blocked=True, name="mm_proj")
    land_b, sf_b, rf_b = _gather_forward(land_b, sem_b[3], proj, "gather_forward_b")
    g_pa, g_pb, g_out = _gather_wait(sh_b, land_b, sem_b, sf_b, rf_b, "gather_wait_b")
    y_a, states = _hgrn_fwd(proj, hgrn_lb_logits, hgrn_out_gain, H, "hgrn_fwd")
    idx = jnp.asarray(_rel_index())
    rb_pad = jnp.pad(rel_bias[0], ((0, 0), (0, N_REL_PAD - N_REL)))
    bias = _bias_table(rb_pad, idx, "bias_table").reshape(H, CHUNK, BAND * CHUNK)
    y_b = _attn_fwd(proj, q_gain, k_gain, bias, off_b, H, "attn_fwd")
    land_c, sf_c, rf_c = _gather_forward(land_c, sem_c[3], y_a, "gather_forward_c")
    g_fin, g_fout = _gather_wait(sh_c, land_c, sem_c, sf_c, rf_c, "gather_wait_c")
    wg_out = g_out.reshape(-1, g_out.shape[-1])
    wg_fout = g_fout.reshape(-1, g_fout.shape[-1])
    pa = _mm(y_a, g_pa, mode="nn", b_blocked=True, tm=2048, name="mm_proj_a")
    pb = _mm(y_b, g_pb, mode="nn", b_blocked=True, tm=2048, name="mm_proj_b")
    merged = _merge_fwd(pa, pb, proj, b_gate, off_g, "merge_fwd")
    mo = _mm(merged, wg_out, mode="nn", name="mm_out")
    x1, h2 = _rms_fwd(xs, mo, norm_ffn, "rms_ffn")
    gu = _mm(h2, g_fin, mode="nn", b_blocked=True, name="mm_ffn_in")
    act = _swiglu_fwd(gu, "swiglu_fwd")
    fo = _mm(act, wg_fout, mode="nn", tk=2816, name="mm_ffn_out")
    dy, loss_acc = _loss_head(x1, fo, target, "loss_head")
    loss = lax.psum(loss_acc[0, 0] * (0.5 / D), AXES)

    dact = _mm(dy, wg_fout, mode="nt", tn=1408, name="mm_d_act")
    gw_fout = _mm(act, dy, mode="tn", out_dtype=bf16, tm=1408, name="mm_gw_ffn_out")
    dgate, dup = _swiglu_bwd(gu, dact, "swiglu_bwd")
    dgu = jnp.concatenate([dgate, dup], axis=1)
    gw_fin = _mm(h2, dgu, mode="tn", out_blocked=True, out_dtype=bf16, tn=g_fin.shape[-1], name="mm_gw_ffn_in")
    dh2 = _mm(dgu, g_fin, mode="nt", b_blocked=True, name="mm_d_h2")
    dx1, g_norm_ffn = _rms_bwd(x1, norm_ffn, dh2, dy, "rms_ffn_bwd")

    dmerged = _mm(dx1, wg_out, mode="nt", name="mm_d_merged")
    gw_out = _mm(merged, dx1, mode="tn", out_dtype=bf16, name="mm_gw_out")
    dpa, dgl_a, gb_a = _branch_bwd(dmerged, pa, proj, b_gate, off_g, 0, "branch_a_bwd")
    dpb, dgl_b, gb_b = _branch_bwd(dmerged, pb, proj, b_gate, off_g + D, D, "branch_b_bwd")
    dy_a = _mm(dpa, g_pa, mode="nt", b_blocked=True, tm=2048, name="mm_d_ya")
    dy_b = _mm(dpb, g_pb, mode="nt", b_blocked=True, tm=2048, name="mm_d_yb")
    gw_pa = _mm(y_a, dpa, mode="tn", out_blocked=True, out_dtype=bf16, tn=g_pa.shape[-1], name="mm_gw_proj_a")
    gw_pb = _mm(y_b, dpb, mode="tn", out_blocked=True, out_dtype=bf16, tn=g_pb.shape[-1], name="mm_gw_proj_b")

    dq_a, df_a, di_a, dg_a, g_logits, g_gain = _hgrn_bwd(proj, hgrn_lb_logits, hgrn_out_gain, states, dy_a, H, "hgrn_bwd")
    dq_b, dk_b, dv_b, dbias, g_qg, g_kg = _attn_bwd(proj, q_gain, k_gain, bias, dy_b, off_b, H, "attn_bwd")
    g_rel = _bias_table_bwd(dbias.reshape(H, -1), idx, "bias_table_bwd")[:, :N_REL]
    dproj = jnp.concatenate([dq_a, df_a, di_a, dg_a, dq_b, dk_b, dv_b, dgl_a, dgl_b], axis=1)
    gw_in = _mm(h, dproj, mode="tn", out_blocked=True, out_dtype=bf16, tn=g_in.shape[-1], name="mm_gw_in")
    dh = _mm(dproj, g_in, mode="nt", b_blocked=True, name="mm_d_h")
    grad_x, g_norm_mix = _rms_bwd(xs, norm_mix, dh, dx1, "rms_mix_bwd")

    full = [gw_in, gw_pa, gw_pb, gw_out.reshape(NDEV, -1, D), gw_fin, gw_fout.reshape(NDEV, -1, D)]
    parts = _scatter_parts(full, "scatter_grads")
    col_sharded = [True, True, True, False, True, False]
    names = ["w_in", "w_proj_a", "w_proj_b", "w_out", "w_ffn_in", "w_ffn_out"]
    big = {}
    for nm, w, m, v, p in zip(names, big_w, big_m, big_v, parts):
        big[nm] = [o[None] for o in _adamw_parts(w, m, v, p, "adamw_" + nm)]

    small_names = ["b_gate", "norm_mix", "norm_ffn", "hgrn_lb_logits", "hgrn_out_gain", "q_gain", "k_gain", "rel_bias"]
    small_w = [b_gate, norm_mix, norm_ffn, hgrn_lb_logits, hgrn_out_gain, q_gain, k_gain, rel_bias]
    small_m = [m_b_gate, m_norm_mix, m_norm_ffn, m_hgrn_lb_logits, m_hgrn_out_gain, m_q_gain, m_k_gain, m_rel_bias]
    small_v = [v_b_gate, v_norm_mix, v_norm_ffn, v_hgrn_lb_logits, v_hgrn_out_gain, v_q_gain, v_k_gain, v_rel_bias]
    small_g = [jnp.concatenate([gb_a, gb_b], axis=1), g_norm_mix, g_norm_ffn, g_logits, g_gain, g_qg, g_kg, g_rel[None]]
    g_sum = _small_all_reduce(_pack(small_g), "reduce_small")
    d_s, m_s, v_s = _adamw_small(_pack(small_w), g_sum, _pack(small_m), _pack(small_v), "adamw_small")
    small = {}
    for nm, g, d, m, v in zip(small_names, _unpack(g_sum, small_w), _unpack(d_s, small_w), _unpack(m_s, small_w), _unpack(v_s, small_w)):
        small[nm] = [g, d, m, v]

    order = ["w_in", "b_gate", "norm_mix", "norm_ffn", "hgrn_lb_logits", "hgrn_out_gain", "q_gain", "k_gain", "rel_bias",
             "w_proj_a", "w_proj_b", "w_out", "w_ffn_in", "w_ffn_out"]
    res = {**big, **small}
    outs = [loss, grad_x[None]]
    for k in range(4):
        outs += [res[nm][k] for nm in order]
    return tuple(outs)
```

```python
import functools

import numpy as np
import jax
import jax.numpy as jnp
from jax import lax
from jax.experimental import pallas as pl
from jax.experimental.pallas import tpu as pltpu
from jax.experimental.pallas import tpu_sc as plsc

f32 = jnp.float32
bf16 = jnp.bfloat16
HI = lax.Precision.HIGHEST
MESH = pl.DeviceIdType.MESH
AXES = ("x", "y", "c")
NDEV = 8

CHUNK = 64
HEAD = 128
SUB = 16
N_PAST = 8
BAND = N_PAST + 1
PAD = N_PAST * CHUNK
REL_FUTURE = CHUNK - 1
REL_PAST = 2 * CHUNK - 1
N_REL = REL_FUTURE + REL_PAST + 1
N_REL_PAD = 256
EPS = 1e-6
NEG = -1e30

ADAM_LR = 0.001
ADAM_B1 = 0.9
ADAM_B2 = 0.999
ADAM_EPS = 1e-08
ADAM_WD = 0.01
ADAM_STEP = 10

VMEM_LIMIT = 56 * 1024 * 1024


def _params(sem=None):
    return pltpu.CompilerParams(dimension_semantics=sem, vmem_limit_bytes=VMEM_LIMIT)


def _tile(n, pref, unit=128):
    if n <= pref:
        return n
    t = (pref // unit) * unit
    while t >= unit:
        if n % t == 0:
            return t
        t -= unit
    return n


_sigmoid = jax.nn.sigmoid


def _mm(a, b, *, mode, name, b_blocked=False, out_blocked=False, out_dtype=f32, tm=1024, tn=1024, tk=2048):
    if mode == "tn":
        K, M = a.shape
    else:
        M, K = a.shape
    if b_blocked:
        nb, mid, cb = b.shape
        if mode == "nn":
            assert mid == K
            N, tn = nb * cb, cb
        else:
            assert mode == "nt" and nb * cb == K
            N, tk = mid, cb
    else:
        N = b.shape[1] if mode in ("nn", "tn") else b.shape[0]
    tm = _tile(M, tm)
    tn = tn if (b_blocked and mode == "nn") or out_blocked else _tile(N, tn)
    tk = tk if b_blocked and mode == "nt" else _tile(K, tk)
    assert M % tm == 0 and N % tn == 0 and K % tk == 0
    nk = K // tk
    grid = (M // tm, N // tn, nk)
    if mode == "tn":
        a_spec = pl.BlockSpec((tk, tm), lambda i, j, k: (k, i))
    else:
        a_spec = pl.BlockSpec((tm, tk), lambda i, j, k: (i, k))
    if mode == "nn":
        b_spec = pl.BlockSpec((None, tk, cb), lambda i, j, k: (j, k, 0)) if b_blocked else pl.BlockSpec((tk, tn), lambda i, j, k: (k, j))
    elif mode == "nt":
        b_spec = pl.BlockSpec((None, tn, cb), lambda i, j, k: (k, j, 0)) if b_blocked else pl.BlockSpec((tn, tk), lambda i, j, k: (j, k))
    else:
        b_spec = pl.BlockSpec((tk, tn), lambda i, j, k: (k, j))
    if out_blocked:
        out_shape = jax.ShapeDtypeStruct((N // tn, M, tn), out_dtype)
        o_spec = pl.BlockSpec((None, tm, tn), lambda i, j, k: (j, i, 0))
    else:
        out_shape = jax.ShapeDtypeStruct((M, N), out_dtype)
        o_spec = pl.BlockSpec((tm, tn), lambda i, j, k: (i, j))
    dims = {"nn": ((1,), (0,)), "nt": ((1,), (1,)), "tn": ((0,), (0,))}[mode]

    def body(a_ref, b_ref, o_ref, *acc):
        p = lax.dot_general(a_ref[...].astype(bf16), b_ref[...].astype(bf16), (dims, ((), ())), preferred_element_type=f32)
        if nk == 1:
            o_ref[...] = p.astype(out_dtype)
        else:
            acc_ref = acc[0]
            k = pl.program_id(2)

            @pl.when(k == 0)
            def _():
                acc_ref[...] = p

            @pl.when(k > 0)
            def _():
                acc_ref[...] += p

            @pl.when(k == nk - 1)
            def _():
                o_ref[...] = acc_ref[...].astype(out_dtype)

    return pl.pallas_call(
        body, name=name, grid=grid, in_specs=[a_spec, b_spec], out_specs=o_spec, out_shape=out_shape,
        scratch_shapes=[pltpu.VMEM((tm, tn), f32)] if nk > 1 else [],
        compiler_params=_params(("parallel", "parallel", "arbitrary")),
    )(a, b)


def _rms_fwd(x, res, gain, name):
    T, D = x.shape
    tr = _tile(T, 256, 8)
    row = pl.BlockSpec((tr, D), lambda i: (i, 0))
    vec = pl.BlockSpec((1, D), lambda i: (0, 0))

    def body(*refs):
        if res is None:
            x_ref, g_ref, h_ref = refs
            xs = x_ref[...]
        else:
            x_ref, r_ref, g_ref, xs_ref, h_ref = refs
            xs = x_ref[...] + r_ref[...]
            xs_ref[...] = xs
        r = lax.rsqrt(jnp.mean(xs * xs, axis=-1, keepdims=True) + EPS)
        h_ref[...] = (xs * r * g_ref[...]).astype(bf16)

    h_shape = jax.ShapeDtypeStruct((T, D), bf16)
    if res is None:
        return pl.pallas_call(body, name=name, grid=(T // tr,), in_specs=[row, vec], out_specs=row, out_shape=h_shape,
                              compiler_params=_params(("parallel",)))(x, gain)
    return pl.pallas_call(body, name=name, grid=(T // tr,), in_specs=[row, row, vec], out_specs=(row, row),
                          out_shape=(jax.ShapeDtypeStruct((T, D), f32), h_shape), compiler_params=_params(("parallel",)))(x, res, gain)


def _rms_bwd(xs, gain, dh, extra, name):
    T, D = xs.shape
    tr = _tile(T, 256, 8)
    row = pl.BlockSpec((tr, D), lambda i: (i, 0))
    vec = pl.BlockSpec((1, D), lambda i: (0, 0))

    def body(x_ref, g_ref, dh_ref, e_ref, dx_ref, dg_ref):
        x = x_ref[...]
        r = lax.rsqrt(jnp.mean(x * x, axis=-1, keepdims=True) + EPS)
        xhat = x * r
        dh_v = dh_ref[...]
        gd = dh_v * g_ref[...]
        dx_ref[...] = e_ref[...] + r * (gd - xhat * jnp.mean(gd * xhat, axis=-1, keepdims=True))
        part = jnp.sum(dh_v * xhat, axis=0, keepdims=True)

        @pl.when(pl.program_id(0) == 0)
        def _():
            dg_ref[...] = part

        @pl.when(pl.program_id(0) > 0)
        def _():
            dg_ref[...] += part

    return pl.pallas_call(body, name=name, grid=(T // tr,), in_specs=[row, vec, row, row], out_specs=(row, vec),
                          out_shape=(jax.ShapeDtypeStruct((T, D), f32), jax.ShapeDtypeStruct((1, D), f32)),
                          compiler_params=_params(("arbitrary",)))(xs, gain, dh, extra)


def _merge_fwd(pa, pb, proj, b_gate, off, name):
    T, D = pa.shape
    tr, tc = _tile(T, 512, 8), _tile(D, 512)
    oa, ob, nb = off // tc, (off + D) // tc, D // tc
    blk = pl.BlockSpec((tr, tc), lambda i, j: (i, j))

    def body(pa_ref, pb_ref, ga_ref, gb_ref, ba_ref, bb_ref, o_ref):
        ga = _sigmoid(ga_ref[...] + ba_ref[...])
        gb = _sigmoid(gb_ref[...] + bb_ref[...])
        o_ref[...] = (ga * pa_ref[...] + gb * pb_ref[...]).astype(bf16)

    return pl.pallas_call(
        body, name=name, grid=(T // tr, nb),
        in_specs=[blk, blk, pl.BlockSpec((tr, tc), lambda i, j: (i, oa + j)), pl.BlockSpec((tr, tc), lambda i, j: (i, ob + j)),
                  pl.BlockSpec((1, tc), lambda i, j: (0, j)), pl.BlockSpec((1, tc), lambda i, j: (0, nb + j))],
        out_specs=blk, out_shape=jax.ShapeDtypeStruct((T, D), bf16), compiler_params=_params(("parallel", "parallel")),
    )(pa, pb, proj, proj, b_gate, b_gate)


def _branch_bwd(dm, p, proj, b_gate, off, boff, name):
    T, D = p.shape
    tr, tc = _tile(T, 512, 8), _tile(D, 512)
    og, obias = off // tc, boff // tc
    blk = pl.BlockSpec((tr, tc), lambda j, i: (i, j))
    vec = pl.BlockSpec((1, tc), lambda j, i: (0, j))

    def body(dm_ref, p_ref, gl_ref, b_ref, dp_ref, dgl_ref, db_ref):
        g = _sigmoid(gl_ref[...] + b_ref[...])
        dm_v = dm_ref[...]
        dp_ref[...] = (dm_v * g).astype(bf16)
        dgl = dm_v * p_ref[...] * g * (1.0 - g)
        dgl_ref[...] = dgl.astype(bf16)
        part = jnp.sum(dgl, axis=0, keepdims=True)

        @pl.when(pl.program_id(1) == 0)
        def _():
            db_ref[...] = part

        @pl.when(pl.program_id(1) > 0)
        def _():
            db_ref[...] += part

    return pl.pallas_call(
        body, name=name, grid=(D // tc, T // tr),
        in_specs=[blk, blk, pl.BlockSpec((tr, tc), lambda j, i: (i, og + j)), pl.BlockSpec((1, tc), lambda j, i: (0, obias + j))],
        out_specs=(blk, blk, vec),
        out_shape=(jax.ShapeDtypeStruct((T, D), bf16), jax.ShapeDtypeStruct((T, D), bf16), jax.ShapeDtypeStruct((1, D), f32)),
        compiler_params=_params(("parallel", "arbitrary")),
    )(dm, p, proj, b_gate)


def _swiglu_fwd(gu, name):
    T, F2 = gu.shape
    F = F2 // 2
    tr, tc = _tile(T, 512, 8), _tile(F, 512)
    nb = F // tc

    def body(g_ref, u_ref, o_ref):
        g = g_ref[...]
        o_ref[...] = (g * _sigmoid(g) * u_ref[...]).astype(bf16)

    return pl.pallas_call(
        body, name=name, grid=(T // tr, nb),
        in_specs=[pl.BlockSpec((tr, tc), lambda i, j: (i, j)), pl.BlockSpec((tr, tc), lambda i, j: (i, nb + j))],
        out_specs=pl.BlockSpec((tr, tc), lambda i, j: (i, j)), out_shape=jax.ShapeDtypeStruct((T, F), bf16),
        compiler_params=_params(("parallel", "parallel")),
    )(gu, gu)


def _swiglu_bwd(gu, dact, name):
    T, F2 = gu.shape
    F = F2 // 2
    tr, tc = _tile(T, 512, 8), _tile(F, 512)
    nb = F // tc
    blk = pl.BlockSpec((tr, tc), lambda i, j: (i, j))

    def body(g_ref, u_ref, d_ref, dg_ref, du_ref):
        g = g_ref[...]
        s = _sigmoid(g)
        d = d_ref[...]
        dg_ref[...] = (d * u_ref[...] * s * (1.0 + g * (1.0 - s))).astype(bf16)
        du_ref[...] = (d * g * s).astype(bf16)

    return pl.pallas_call(
        body, name=name, grid=(T // tr, nb),
        in_specs=[blk, pl.BlockSpec((tr, tc), lambda i, j: (i, nb + j)), blk], out_specs=(blk, blk),
        out_shape=(jax.ShapeDtypeStruct((T, F), bf16), jax.ShapeDtypeStruct((T, F), bf16)),
        compiler_params=_params(("parallel", "parallel")),
    )(gu, gu, dact)


def _loss_head(x1, fo, target, name):
    T, D = x1.shape
    tr = _tile(T, 256, 8)
    row = pl.BlockSpec((tr, D), lambda i: (i, 0))
    acc = pl.BlockSpec((8, 128), lambda i: (0, 0))

    def body(x_ref, f_ref, t_ref, dy_ref, l_ref):
        d = x_ref[...] + f_ref[...] - t_ref[...]
        dy_ref[...] = d * (1.0 / D)
        part = jnp.sum(jnp.sum(d * d, axis=1, keepdims=True), axis=0, keepdims=True)

        @pl.when(pl.program_id(0) == 0)
        def _():
            l_ref[...] = jnp.zeros((8, 128), f32)

        l_ref[...] += part

    return pl.pallas_call(body, name=name, grid=(T // tr,), in_specs=[row, row, row], out_specs=(row, acc),
                          out_shape=(jax.ShapeDtypeStruct((T, D), f32), jax.ShapeDtypeStruct((8, 128), f32)),
                          compiler_params=_params(("arbitrary",)))(x1, fo, target)


def _dotb(a, b, dims):
    return lax.dot_general(a.astype(bf16), b.astype(bf16), (dims, ((), ())), preferred_element_type=f32)


def _hgrn_chunk(q, fl, iv, g, logits, gain, st):
    lb = jax.nn.softmax(logits, axis=0)[0:1]
    f = lb + (1.0 - lb) * _sigmoid(fl)
    lf = jnp.log(f)
    kk = 1.0 - f
    qs = q * _sigmoid(q)
    row = lax.broadcasted_iota(jnp.int32, (CHUNK, CHUNK), 0)
    col = lax.broadcasted_iota(jnp.int32, (CHUNK, CHUNK), 1)
    tril = (col <= row).astype(f32)
    b = lax.dot_general(tril, lf, (((1,), (0,)), ((), ())), precision=HI, preferred_element_type=f32)
    b_last = jnp.sum(lf, axis=0, keepdims=True)
    o = _dotb(qs * jnp.exp(b), st, ((1,), (1,)))
    r3 = lax.broadcasted_iota(jnp.int32, (SUB, SUB, HEAD), 0)
    c3 = lax.broadcasted_iota(jnp.int32, (SUB, SUB, HEAD), 1)
    parts = []
    for i in range(CHUNK // SUB):
        lo, hi = i * SUB, (i + 1) * SUB
        bi = b[lo:hi]
        dec = jnp.exp(jnp.where(c3 <= r3, bi[:, None, :] - bi[None, :, :], -jnp.inf))
        s = jnp.sum(qs[lo:hi][:, None, :] * kk[lo:hi][None, :, :] * dec, axis=-1)
        if i > 0:
            anchor = jnp.max(bi, axis=0, keepdims=True)
            qa = qs[lo:hi] * jnp.exp(bi - anchor)
            kd = kk[:lo] * jnp.exp(anchor - b[:lo])
            s = jnp.concatenate([_dotb(qa, kd, ((1,), (1,))), s], axis=1)
        parts.append(_dotb(s, iv[:hi], ((1,), (0,))))
    o = o + jnp.concatenate(parts, axis=0)
    st_new = st * jnp.exp(b_last) + _dotb(iv, kk * jnp.exp(b_last - b), ((0,), (0,)))
    o = o * lax.rsqrt(jnp.mean(o * o, axis=-1, keepdims=True) + EPS)
    o = o * gain * (g * _sigmoid(g))
    return o, st_new


def _hgrn_fwd(proj, logits, gain, n_heads, name):
    T = proj.shape[0]
    nc = T // CHUNK
    H = n_heads

    def col(k):
        return pl.BlockSpec((CHUNK, HEAD), lambda h, c: (c, k * H + h))

    def body(q_ref, f_ref, i_ref, g_ref, l_ref, ga_ref, y_ref, s_ref, st):
        @pl.when(pl.program_id(1) == 0)
        def _():
            st[...] = jnp.zeros((HEAD, HEAD), f32)

        s_ref[...] = st[...]
        o, st_new = _hgrn_chunk(q_ref[...], f_ref[...], i_ref[...], g_ref[...], l_ref[...], ga_ref[...], st[...])
        y_ref[...] = o.astype(bf16)
        st[...] = st_new

    return pl.pallas_call(
        body, name=name, grid=(H, nc),
        in_specs=[col(0), col(1), col(2), col(3), pl.BlockSpec((2, HEAD), lambda h, c: (0, h)), pl.BlockSpec((1, HEAD), lambda h, c: (0, h))],
        out_specs=(pl.BlockSpec((CHUNK, HEAD), lambda h, c: (c, h)), pl.BlockSpec((None, None, HEAD, HEAD), lambda h, c: (h, c, 0, 0))),
        out_shape=(jax.ShapeDtypeStruct((T, H * HEAD), bf16), jax.ShapeDtypeStruct((H, nc, HEAD, HEAD), f32)),
        scratch_shapes=[pltpu.VMEM((HEAD, HEAD), f32)],
        compiler_params=_params(("parallel", "arbitrary")),
    )(proj, proj, proj, proj, logits, gain)


def _hgrn_bwd(proj, logits, gain, states, dy, n_heads, name):
    T = proj.shape[0]
    nc = T // CHUNK
    H = n_heads

    def col(k):
        return pl.BlockSpec((CHUNK, HEAD), lambda h, c: (nc - 1 - c, k * H + h))

    out_blk = pl.BlockSpec((CHUNK, HEAD), lambda h, c: (nc - 1 - c, h))

    def body(q_ref, f_ref, i_ref, g_ref, l_ref, ga_ref, s_ref, dy_ref, dq_ref, df_ref, di_ref, dg_ref, dl_ref, dga_ref, dst):
        first = pl.program_id(1) == 0

        @pl.when(first)
        def _():
            dst[...] = jnp.zeros((HEAD, HEAD), f32)

        _, vjp = jax.vjp(_hgrn_chunk, q_ref[...], f_ref[...], i_ref[...], g_ref[...], l_ref[...], ga_ref[...], s_ref[...])
        dq, df, di, dg, dl, dga, ds = vjp((dy_ref[...], dst[...]))
        dq_ref[...] = dq.astype(bf16)
        df_ref[...] = df.astype(bf16)
        di_ref[...] = di.astype(bf16)
        dg_ref[...] = dg.astype(bf16)
        dst[...] = ds

        @pl.when(first)
        def _():
            dl_ref[...] = dl
            dga_ref[...] = dga

        @pl.when(jnp.logical_not(first))
        def _():
            dl_ref[...] += dl
            dga_ref[...] += dga

    act = jax.ShapeDtypeStruct((T, H * HEAD), bf16)
    return pl.pallas_call(
        body, name=name, grid=(H, nc),
        in_specs=[col(0), col(1), col(2), col(3), pl.BlockSpec((2, HEAD), lambda h, c: (0, h)), pl.BlockSpec((1, HEAD), lambda h, c: (0, h)),
                  pl.BlockSpec((None, None, HEAD, HEAD), lambda h, c: (h, nc - 1 - c, 0, 0)), out_blk],
        out_specs=(out_blk, out_blk, out_blk, out_blk, pl.BlockSpec((2, HEAD), lambda h, c: (0, h)), pl.BlockSpec((1, HEAD), lambda h, c: (0, h))),
        out_shape=(act, act, act, act, jax.ShapeDtypeStruct((2, H * HEAD), f32), jax.ShapeDtypeStruct((1, H * HEAD), f32)),
        scratch_shapes=[pltpu.VMEM((HEAD, HEAD), f32)],
        compiler_params=_params(("parallel", "arbitrary")),
    )(proj, proj, proj, proj, logits, gain, states, dy)


def _rel_index():
    t = np.arange(CHUNK)[:, None]
    sp = np.arange(BAND * CHUNK)[None, :]
    dist = (N_PAST - sp // CHUNK) * CHUNK + t - sp % CHUNK
    return (np.clip(dist, -REL_FUTURE, REL_PAST) + REL_FUTURE).reshape(1, -1).astype(np.int32)


def _bias_table(rel_bias_pad, idx, name):
    H = rel_bias_pad.shape[0]
    n = idx.shape[1]
    tc = _tile(n, 4096)

    def body(rb_ref, idx_ref, o_ref):
        onehot = (lax.broadcasted_iota(jnp.int32, (N_REL_PAD, tc), 0) == idx_ref[...]).astype(f32)
        o_ref[...] = lax.dot_general(rb_ref[...], onehot, (((1,), (0,)), ((), ())), precision=HI, preferred_element_type=f32)

    return pl.pallas_call(
        body, name=name, grid=(n // tc,),
        in_specs=[pl.BlockSpec((H, N_REL_PAD), lambda j: (0, 0)), pl.BlockSpec((1, tc), lambda j: (0, j))],
        out_specs=pl.BlockSpec((H, tc), lambda j: (0, j)), out_shape=jax.ShapeDtypeStruct((H, n), f32),
        compiler_params=_params(("parallel",)),
    )(rel_bias_pad, idx)


def _bias_table_bwd(dbias, idx, name):
    H, n = dbias.shape
    tc = _tile(n, 4096)

    def body(d_ref, idx_ref, o_ref):
        onehot = (lax.broadcasted_iota(jnp.int32, (N_REL_PAD, tc), 0) == idx_ref[...]).astype(f32)
        part = lax.dot_general(d_ref[...], onehot, (((1,), (1,)), ((), ())), precision=HI, preferred_element_type=f32)

        @pl.when(pl.program_id(0) == 0)
        def _():
            o_ref[...] = part

        @pl.when(pl.program_id(0) > 0)
        def _():
            o_ref[...] += part

    return pl.pallas_call(
        body, name=name, grid=(n // tc,),
        in_specs=[pl.BlockSpec((H, tc), lambda j: (0, j)), pl.BlockSpec((1, tc), lambda j: (0, j))],
        out_specs=pl.BlockSpec((H, N_REL_PAD), lambda j: (0, 0)), out_shape=jax.ShapeDtypeStruct((H, N_REL_PAD), f32),
        compiler_params=_params(("arbitrary",)),
    )(dbias, idx)


def _head_norm(t, gain):
    return t * lax.rsqrt(jnp.mean(t * t, axis=-1, keepdims=True) + EPS) * gain


def _attn_chunk(q, kb, vb, qg, bias, n):
    qh = _head_norm(q, qg)
    s = _dotb(qh, kb, ((1,), (1,))) * (HEAD ** -0.5) + bias
    pos = n * CHUNK - PAD + lax.broadcasted_iota(jnp.int32, (1, BAND * CHUNK), 1)
    s = jnp.where(pos >= 0, s, NEG)
    e = jnp.exp(s - jnp.max(s, axis=-1, keepdims=True))
    p = e / jnp.sum(e, axis=-1, keepdims=True)
    return _dotb(p, vb, ((1,), (0,)))


def _attn_fwd(proj, q_gain, k_gain, bias, off, n_heads, name):
    T = proj.shape[0]
    nc = T // CHUNK
    H = n_heads
    o0 = off // HEAD
    full = lambda k: pl.BlockSpec((T, HEAD), lambda h, c: (0, o0 + k * H + h))
    vec = pl.BlockSpec((1, HEAD), lambda h, c: (0, 0))

    def body(q_ref, k_ref, v_ref, qg_ref, kg_ref, b_ref, y_ref, kp, vp):
        c = pl.program_id(1)

        @pl.when(c == 0)
        def _():
            kp[pl.ds(0, PAD), :] = jnp.zeros((PAD, HEAD), f32)
            vp[pl.ds(0, PAD), :] = jnp.zeros((PAD, HEAD), f32)
            kp[pl.ds(PAD, T), :] = _head_norm(k_ref[...], kg_ref[...])
            vp[pl.ds(PAD, T), :] = v_ref[...]

        band = pl.ds(pl.multiple_of(c * CHUNK, CHUNK), BAND * CHUNK)
        y_ref[...] = _attn_chunk(q_ref[...], kp[band, :], vp[band, :], qg_ref[...], b_ref[...], c).astype(bf16)

    return pl.pallas_call(
        body, name=name, grid=(H, nc),
        in_specs=[pl.BlockSpec((CHUNK, HEAD), lambda h, c: (c, o0 + h)), full(1), full(2), vec, vec,
                  pl.BlockSpec((None, CHUNK, BAND * CHUNK), lambda h, c: (h, 0, 0))],
        out_specs=pl.BlockSpec((CHUNK, HEAD), lambda h, c: (c, h)), out_shape=jax.ShapeDtypeStruct((T, H * HEAD), bf16),
        scratch_shapes=[pltpu.VMEM((T + PAD, HEAD), f32), pltpu.VMEM((T + PAD, HEAD), f32)],
        compiler_params=_params(("parallel", "arbitrary")),
    )(proj, proj, proj, q_gain, k_gain, bias)


def _attn_bwd(proj, q_gain, k_gain, bias, dy, off, n_heads, name):
    T = proj.shape[0]
    nc = T // CHUNK
    H = n_heads
    o0 = off // HEAD
    full = lambda k: pl.BlockSpec((T, HEAD), lambda h, c: (0, o0 + k * H + h))
    full_out = pl.BlockSpec((T, HEAD), lambda h, c: (0, h))
    vec = pl.BlockSpec((1, HEAD), lambda h, c: (0, 0))
    chunk_out = pl.BlockSpec((CHUNK, HEAD), lambda h, c: (c, h))
    bias_blk = pl.BlockSpec((None, CHUNK, BAND * CHUNK), lambda h, c: (h, 0, 0))

    def body(q_ref, k_ref, v_ref, qg_ref, kg_ref, b_ref, dy_ref, dq_ref, dk_ref, dv_ref, db_ref, dqg_ref, dkg_ref, kp, vp, dkp, dvp):
        h = pl.program_id(0)
        c = pl.program_id(1)

        @pl.when(c == 0)
        def _():
            kp[pl.ds(0, PAD), :] = jnp.zeros((PAD, HEAD), f32)
            vp[pl.ds(0, PAD), :] = jnp.zeros((PAD, HEAD), f32)
            kp[pl.ds(PAD, T), :] = _head_norm(k_ref[...], kg_ref[...])
            vp[pl.ds(PAD, T), :] = v_ref[...]
            dkp[...] = jnp.zeros((T + PAD, HEAD), f32)
            dvp[...] = jnp.zeros((T + PAD, HEAD), f32)
            db_ref[...] = jnp.zeros((CHUNK, BAND * CHUNK), f32)

        @pl.when(jnp.logical_and(h == 0, c == 0))
        def _():
            dqg_ref[...] = jnp.zeros((1, HEAD), f32)
            dkg_ref[...] = jnp.zeros((1, HEAD), f32)

        band = pl.ds(pl.multiple_of(c * CHUNK, CHUNK), BAND * CHUNK)
        _, vjp = jax.vjp(functools.partial(_attn_chunk, n=c), q_ref[...], kp[band, :], vp[band, :], qg_ref[...], b_ref[...])
        dq, dkb, dvb, dqg, db = vjp(dy_ref[...])
        dq_ref[...] = dq.astype(bf16)
        dkp[band, :] += dkb
        dvp[band, :] += dvb
        db_ref[...] += db
        dqg_ref[...] += dqg

        @pl.when(c == nc - 1)
        def _():
            _, nvjp = jax.vjp(_head_norm, k_ref[...], kg_ref[...])
            dk, dkg = nvjp(dkp[pl.ds(PAD, T), :])
            dk_ref[...] = dk.astype(bf16)
            dv_ref[...] = dvp[pl.ds(PAD, T), :].astype(bf16)
            dkg_ref[...] += dkg

    act = jax.ShapeDtypeStruct((T, H * HEAD), bf16)
    gvec = jax.ShapeDtypeStruct((1, HEAD), f32)
    pad_buf = pltpu.VMEM((T + PAD, HEAD), f32)
    return pl.pallas_call(
        body, name=name, grid=(H, nc),
        in_specs=[pl.BlockSpec((CHUNK, HEAD), lambda h, c: (c, o0 + h)), full(1), full(2), vec, vec, bias_blk, chunk_out],
        out_specs=(chunk_out, full_out, full_out, bias_blk, vec, vec),
        out_shape=(act, act, act, jax.ShapeDtypeStruct((H, CHUNK, BAND * CHUNK), f32), gvec, gvec),
        scratch_shapes=[pad_buf, pad_buf, pad_buf, pad_buf],
        compiler_params=_params(("arbitrary", "arbitrary")),
    )(proj, proj, proj, q_gain, k_gain, bias, dy)


def _position():
    x, y, c = lax.axis_index("x"), lax.axis_index("y"), lax.axis_index("c")
    return x, y, c, 4 * x + 2 * y + c


def _flip(v, bit):
    return 1 - v if bit else v


def _chips(x, y):
    return [(1 - x, y), (x, 1 - y), (1 - x, 1 - y)]


def _seq_gather(shards, name, collective_id):
    n = len(shards)

    def body(*refs):
        ins, outs = refs[:n], refs[n:2 * n]
        send, recv, loc = refs[2 * n:]
        x, y, c, me = _position()
        sib = (x, y, 1 - c)
        chips = _chips(x, y)
        barrier = pltpu.get_barrier_semaphore()
        for peer in [sib] + [(px, py, c) for px, py in chips]:
            pl.semaphore_signal(barrier, inc=1, device_id=peer, device_id_type=MESH)
        pl.semaphore_wait(barrier, 4)

        def copy(w, k, src, blk, to):
            return pltpu.make_async_remote_copy(src_ref=src, dst_ref=outs[w].at[blk], send_sem=send.at[7 * w + k], recv_sem=recv.at[7 * w + k],
                                                device_id=to, device_id_type=MESH)

        mine = [pltpu.make_async_copy(ins[w], outs[w].at[me], loc.at[w]) for w in range(n)]
        for cp in mine:
            cp.start()
        first = []
        for j, (px, py) in enumerate(chips):
            first += [copy(w, 1 + j, ins[w], me, (px, py, c)) for w in range(n)]
        first += [copy(w, 0, ins[w], me, sib) for w in range(n)]
        for cp in first:
            cp.start()
        passed = []
        for j, (px, py) in enumerate(chips):
            blk = 4 * px + 2 * py + c
            for w in range(n):
                copy(w, 1 + j, ins[w], blk, sib).wait_recv()
                fwd = copy(w, 4 + j, outs[w].at[blk], blk, sib)
                fwd.start()
                passed.append(fwd)
        for w in range(n):
            copy(w, 0, ins[w], 4 * x + 2 * y + (1 - c), sib).wait_recv()
        for j, (px, py) in enumerate(chips):
            for w in range(n):
                copy(w, 4 + j, ins[w], 4 * px + 2 * py + (1 - c), sib).wait_recv()
        for cp in first + passed:
            cp.wait_send()
        for cp in mine:
            cp.wait()

    return pl.kernel(
        body, out_type=tuple(jax.ShapeDtypeStruct((NDEV,) + s.shape, s.dtype) for s in shards),
        mesh=plsc.ScalarSubcoreMesh(axis_name="sequencer", num_cores=1), name=name,
        scratch_types=(pltpu.SemaphoreType.DMA((7 * n,)), pltpu.SemaphoreType.DMA((7 * n,)), pltpu.SemaphoreType.DMA((n,))),
        compiler_params=pltpu.CompilerParams(collective_id=collective_id),
    )(*shards)


def _seq_scatter(grads, name, collective_id):
    n = len(grads)

    def body(*refs):
        ins, outs = refs[:n], refs[n:2 * n]
        send, recv, loc = refs[2 * n:]
        x, y, c, me = _position()
        peers = [(_flip(x, r & 4), _flip(y, r & 2), _flip(c, r & 1)) for r in range(1, NDEV)]
        barrier = pltpu.get_barrier_semaphore()
        for peer in peers:
            pl.semaphore_signal(barrier, inc=1, device_id=peer, device_id_type=MESH)
        pl.semaphore_wait(barrier, NDEV - 1)
        mine = [pltpu.make_async_copy(ins[w].at[me], outs[w].at[me], loc.at[w]) for w in range(n)]
        for cp in mine:
            cp.start()
        sends, waits = [], []
        for k, (px, py, pc) in enumerate(peers):
            peer = 4 * px + 2 * py + pc
            for w in range(n):
                sends.append(pltpu.make_async_remote_copy(src_ref=ins[w].at[peer], dst_ref=outs[w].at[me], send_sem=send.at[7 * w + k],
                                                          recv_sem=recv.at[7 * w + k], device_id=(px, py, pc), device_id_type=MESH))
                waits.append(pltpu.make_async_remote_copy(src_ref=ins[w].at[peer], dst_ref=outs[w].at[peer], send_sem=send.at[7 * w + k],
                                                          recv_sem=recv.at[7 * w + k], device_id=(px, py, pc), device_id_type=MESH))
        for cp in sends:
            cp.start()
        for cp in waits:
            cp.wait_recv()
        for cp in sends:
            cp.wait_send()
        for cp in mine:
            cp.wait()

    return pl.kernel(
        body, out_type=tuple(jax.ShapeDtypeStruct(g.shape, g.dtype) for g in grads),
        mesh=plsc.ScalarSubcoreMesh(axis_name="sequencer", num_cores=1), name=name,
        scratch_types=(pltpu.SemaphoreType.DMA((7 * n,)), pltpu.SemaphoreType.DMA((7 * n,)), pltpu.SemaphoreType.DMA((n,))),
        compiler_params=pltpu.CompilerParams(collective_id=collective_id),
    )(*grads)


def _small_all_reduce(v, name):
    R, C = v.shape

    def body(v_ref, o_ref, buf, send, recv):
        x, y, c, me = _position()
        buf[me] = v_ref[...]
        sends, waits = [], []
        for r in range(1, NDEV):
            px, py, pc = _flip(x, r & 4), _flip(y, r & 2), _flip(c, r & 1)
            peer = 4 * px + 2 * py + pc
            sends.append(pltpu.make_async_remote_copy(src_ref=v_ref, dst_ref=buf.at[me], send_sem=send.at[r - 1], recv_sem=recv.at[r - 1],
                                                      device_id=(px, py, pc), device_id_type=MESH))
            waits.append(pltpu.make_async_remote_copy(src_ref=v_ref, dst_ref=buf.at[peer], send_sem=send.at[r - 1], recv_sem=recv.at[r - 1],
                                                      device_id=(px, py, pc), device_id_type=MESH))
        for cp in sends:
            cp.start()
        for cp in waits:
            cp.wait_recv()
        for cp in sends:
            cp.wait_send()
        acc = buf[0]
        for i in range(1, NDEV):
            acc = acc + buf[i]
        o_ref[...] = acc

    vm = pl.BlockSpec(memory_space=pltpu.VMEM)
    return pl.pallas_call(
        body, name=name, in_specs=[vm], out_specs=vm, out_shape=jax.ShapeDtypeStruct((R, C), f32),
        scratch_shapes=[pltpu.VMEM((NDEV, R, C), f32), pltpu.SemaphoreType.DMA((7,)), pltpu.SemaphoreType.DMA((7,))],
    )(v)


def _adamw_math(w, g, m, v):
    m = ADAM_B1 * m + (1.0 - ADAM_B1) * g
    v = ADAM_B2 * v + (1.0 - ADAM_B2) * (g * g)
    m_hat = m / (1.0 - ADAM_B1 ** ADAM_STEP)
    v_hat = v / (1.0 - ADAM_B2 ** ADAM_STEP)
    delta = -ADAM_LR * (m_hat / (jnp.sqrt(v_hat) + ADAM_EPS) + ADAM_WD * w)
    return delta, m, v


def _adamw_parts(w, m, v, parts, name):
    R, C = w.shape
    tr = _tile(R, 128, 16)
    blk = pl.BlockSpec((tr, C), lambda i: (i, 0))

    def body(w_ref, m_ref, v_ref, p_ref, g_ref, d_ref, mo_ref, vo_ref):
        g = p_ref[0].astype(f32)
        for i in range(1, NDEV):
            g = g + p_ref[i].astype(f32)
        d, mn, vn = _adamw_math(w_ref[...], g, m_ref[...], v_ref[...])
        g_ref[...] = g
        d_ref[...] = d
        mo_ref[...] = mn
        vo_ref[...] = vn

    shp = jax.ShapeDtypeStruct((R, C), f32)
    return pl.pallas_call(
        body, name=name, grid=(R // tr,), in_specs=[blk, blk, blk, pl.BlockSpec((NDEV, tr, C), lambda i: (0, i, 0))],
        out_specs=(blk, blk, blk, blk), out_shape=(shp, shp, shp, shp), compiler_params=_params(("parallel",)),
    )(w, m, v, parts)


def _adamw_small(w, g, m, v, name):
    def body(w_ref, g_ref, m_ref, v_ref, d_ref, mo_ref, vo_ref):
        d, mn, vn = _adamw_math(w_ref[...], g_ref[...], m_ref[...], v_ref[...])
        d_ref[...] = d
        mo_ref[...] = mn
        vo_ref[...] = vn

    shp = jax.ShapeDtypeStruct(w.shape, f32)
    return pl.pallas_call(body, name=name, out_shape=(shp, shp, shp))(w, g, m, v)


SMALL_COLS = 1024


def _pack(arrs):
    flat = jnp.concatenate([a.reshape(-1) for a in arrs])
    rows = -(-flat.shape[0] // (8 * SMALL_COLS)) * 8
    return jnp.pad(flat, (0, rows * SMALL_COLS - flat.shape[0])).reshape(rows, SMALL_COLS)


def _unpack(packed, like):
    flat = packed.reshape(-1)
    out, pos = [], 0
    for a in like:
        out.append(flat[pos:pos + a.size].reshape(a.shape))
        pos += a.size
    return out


def kernel(x, w_in, b_gate, norm_mix, norm_ffn, hgrn_lb_logits, hgrn_out_gain, q_gain, k_gain, rel_bias, w_proj_a, w_proj_b, w_out, w_ffn_in, w_ffn_out, loss_target, m_w_in, m_b_gate, m_norm_mix, m_norm_ffn, m_hgrn_lb_logits, m_hgrn_out_gain, m_q_gain, m_k_gain, m_rel_bias, m_w_proj_a, m_w_proj_b, m_w_out, m_w_ffn_in, m_w_ffn_out, v_w_in, v_b_gate, v_norm_mix, v_norm_ffn, v_hgrn_lb_logits, v_hgrn_out_gain, v_q_gain, v_k_gain, v_rel_bias, v_w_proj_a, v_w_proj_b, v_w_out, v_w_ffn_in, v_w_ffn_out):
    xs = x[0]
    target = loss_target[0]
    T, D = xs.shape
    d_a = hgrn_out_gain.shape[-1]
    H = d_a // HEAD
    d_b = d_a
    off_b = 4 * d_a
    off_g = off_b + 3 * d_b
    assert rel_bias.shape[1] == H and T % CHUNK == 0 and T // CHUNK > N_PAST

    big_w = [w_in[0], w_proj_a[0], w_proj_b[0], w_out[0], w_ffn_in[0], w_ffn_out[0]]
    big_m = [m_w_in[0], m_w_proj_a[0], m_w_proj_b[0], m_w_out[0], m_w_ffn_in[0], m_w_ffn_out[0]]
    big_v = [v_w_in[0], v_w_proj_a[0], v_w_proj_b[0], v_w_out[0], v_w_ffn_in[0], v_w_ffn_out[0]]

    sh = [w.astype(bf16) for w in big_w]
    (g_in,) = _seq_gather(sh[0:1], "gather_a", 1)
    g_pa, g_pb, g_out = _seq_gather(sh[1:4], "gather_b", 2)
    g_fin, g_fout = _seq_gather(sh[4:6], "gather_c", 3)

    h = _rms_fwd(xs, None, norm_mix, "rms_mix")
    proj = _mm(h, g_in, mode="nn", b_blocked=True, name="mm_proj")
    y_a, states = _hgrn_fwd(proj, hgrn_lb_logits, hgrn_out_gain, H, "hgrn_fwd")
    idx = jnp.asarray(_rel_index())
    rb_pad = jnp.pad(rel_bias[0], ((0, 0), (0, N_REL_PAD - N_REL)))
    bias = _bias_table(rb_pad, idx, "bias_table").reshape(H, CHUNK, BAND * CHUNK)
    y_b = _attn_fwd(proj, q_gain, k_gain, bias, off_b, H, "attn_fwd")
    wg_out = g_out.reshape(-1, g_out.shape[-1])
    wg_fout = g_fout.reshape(-1, g_fout.shape[-1])
    pa = _mm(y_a, g_pa, mode="nn", b_blocked=True, tm=2048, name="mm_proj_a")
    pb = _mm(y_b, g_pb, mode="nn", b_blocked=True, tm=2048, name="mm_proj_b")
    merged = _merge_fwd(pa, pb, proj, b_gate, off_g, "merge_fwd")
    mo = _mm(merged, wg_out, mode="nn", name="mm_out")
    x1, h2 = _rms_fwd(xs, mo, norm_ffn, "rms_ffn")
    gu = _mm(h2, g_fin, mode="nn", b_blocked=True, name="mm_ffn_in")
    act = _swiglu_fwd(gu, "swiglu_fwd")
    fo = _mm(act, wg_fout, mode="nn", tk=2816, name="mm_ffn_out")
    dy, loss_acc = _loss_head(x1, fo, target, "loss_head")
    loss_part = loss_acc[0:1, 0:1] * (0.5 / D)

    dact = _mm(dy, wg_fout, mode="nt", tn=1408, name="mm_d_act")
    gw_fout = _mm(act, dy, mode="tn", out_dtype=bf16, tm=1408, name="mm_gw_ffn_out")
    dgate, dup = _swiglu_bwd(gu, dact, "swiglu_bwd")
    dgu = jnp.concatenate([dgate, dup], axis=1)
    gw_fin = _mm(h2, dgu, mode="tn", out_blocked=True, out_dtype=bf16, tn=g_fin.shape[-1], name="mm_gw_ffn_in")
    dh2 = _mm(dgu, g_fin, mode="nt", b_blocked=True, name="mm_d_h2")
    dx1, g_norm_ffn = _rms_bwd(x1, norm_ffn, dh2, dy, "rms_ffn_bwd")

    dmerged = _mm(dx1, wg_out, mode="nt", name="mm_d_merged")
    gw_out = _mm(merged, dx1, mode="tn", out_dtype=bf16, name="mm_gw_out")
    dpa, dgl_a, gb_a = _branch_bwd(dmerged, pa, proj, b_gate, off_g, 0, "branch_a_bwd")
    dpb, dgl_b, gb_b = _branch_bwd(dmerged, pb, proj, b_gate, off_g + D, D, "branch_b_bwd")
    dy_a = _mm(dpa, g_pa, mode="nt", b_blocked=True, tm=2048, name="mm_d_ya")
    dy_b = _mm(dpb, g_pb, mode="nt", b_blocked=True, tm=2048, name="mm_d_yb")
    gw_pa = _mm(y_a, dpa, mode="tn", out_blocked=True, out_dtype=bf16, tn=g_pa.shape[-1], name="mm_gw_proj_a")
    gw_pb = _mm(y_b, dpb, mode="tn", out_blocked=True, out_dtype=bf16, tn=g_pb.shape[-1], name="mm_gw_proj_b")

    dq_a, df_a, di_a, dg_a, g_logits, g_gain = _hgrn_bwd(proj, hgrn_lb_logits, hgrn_out_gain, states, dy_a, H, "hgrn_bwd")
    dq_b, dk_b, dv_b, dbias, g_qg, g_kg = _attn_bwd(proj, q_gain, k_gain, bias, dy_b, off_b, H, "attn_bwd")
    g_rel = _bias_table_bwd(dbias.reshape(H, -1), idx, "bias_table_bwd")[:, :N_REL]
    dproj = jnp.concatenate([dq_a, df_a, di_a, dg_a, dq_b, dk_b, dv_b, dgl_a, dgl_b], axis=1)
    gw_in = _mm(h, dproj, mode="tn", out_blocked=True, out_dtype=bf16, tn=g_in.shape[-1], name="mm_gw_in")
    dh = _mm(dproj, g_in, mode="nt", b_blocked=True, name="mm_d_h")
    grad_x, g_norm_mix = _rms_bwd(xs, norm_mix, dh, dx1, "rms_mix_bwd")

    (p_fout,) = _seq_scatter([gw_fout.reshape(NDEV, -1, D)], "scatter_a", 4)
    (p_fin,) = _seq_scatter([gw_fin], "scatter_b", 5)
    p_out, p_pa, p_pb = _seq_scatter([gw_out.reshape(NDEV, -1, D), gw_pa, gw_pb], "scatter_c", 6)
    (p_in,) = _seq_scatter([gw_in], "scatter_d", 7)
    parts = [p_in, p_pa, p_pb, p_out, p_fin, p_fout]
    names = ["w_in", "w_proj_a", "w_proj_b", "w_out", "w_ffn_in", "w_ffn_out"]
    big = {}
    for nm, w, m, v, p in zip(names, big_w, big_m, big_v, parts):
        big[nm] = [o[None] for o in _adamw_parts(w, m, v, p, "adamw_" + nm)]

    small_names = ["b_gate", "norm_mix", "norm_ffn", "hgrn_lb_logits", "hgrn_out_gain", "q_gain", "k_gain", "rel_bias"]
    small_w = [b_gate, norm_mix, norm_ffn, hgrn_lb_logits, hgrn_out_gain, q_gain, k_gain, rel_bias]
    small_m = [m_b_gate, m_norm_mix, m_norm_ffn, m_hgrn_lb_logits, m_hgrn_out_gain, m_q_gain, m_k_gain, m_rel_bias]
    small_v = [v_b_gate, v_norm_mix, v_norm_ffn, v_hgrn_lb_logits, v_hgrn_out_gain, v_q_gain, v_k_gain, v_rel_bias]
    small_g = [jnp.concatenate([gb_a, gb_b], axis=1), g_norm_mix, g_norm_ffn, g_logits, g_gain, g_qg, g_kg, g_rel[None], loss_part]
    g_sum = _small_all_reduce(_pack(small_g), "reduce_small")
    loss = _unpack(g_sum, small_g)[-1].reshape(())
    d_s, m_s, v_s = _adamw_small(_pack(small_w), g_sum, _pack(small_m), _pack(small_v), "adamw_small")
    small = {}
    for nm, g, d, m, v in zip(small_names, _unpack(g_sum, small_w), _unpack(d_s, small_w), _unpack(m_s, small_w), _unpack(v_s, small_w)):
        small[nm] = [g, d, m, v]

    order = ["w_in", "b_gate", "norm_mix", "norm_ffn", "hgrn_lb_logits", "hgrn_out_gain", "q_gain", "k_gain", "rel_bias",
             "w_proj_a", "w_proj_b", "w_out", "w_ffn_in", "w_ffn_out"]
    res = {**big, **small}
    outs = [loss, grad_x[None]]
    for k in range(4):
        outs += [res[nm][k] for nm in order]
    return tuple(outs)
```

```python
import functools

import numpy as np
import jax
import jax.numpy as jnp
from jax import lax
from jax.experimental import pallas as pl
from jax.experimental.pallas import tpu as pltpu
from jax.experimental.pallas import tpu_sc as plsc

f32 = jnp.float32
bf16 = jnp.bfloat16
HI = lax.Precision.HIGHEST
MESH = pl.DeviceIdType.MESH
AXES = ("x", "y", "c")
NDEV = 8

CHUNK = 64
HEAD = 128
SUB = 16
N_PAST = 8
BAND = N_PAST + 1
PAD = N_PAST * CHUNK
REL_FUTURE = CHUNK - 1
REL_PAST = 2 * CHUNK - 1
N_REL = REL_FUTURE + REL_PAST + 1
N_REL_PAD = 256
EPS = 1e-6
NEG = -1e30

ADAM_LR = 0.001
ADAM_B1 = 0.9
ADAM_B2 = 0.999
ADAM_EPS = 1e-08
ADAM_WD = 0.01
ADAM_STEP = 10

VMEM_LIMIT = 56 * 1024 * 1024


def _params(sem=None):
    return pltpu.CompilerParams(dimension_semantics=sem, vmem_limit_bytes=VMEM_LIMIT)


def _tile(n, pref, unit=128):
    if n <= pref:
        return n
    t = (pref // unit) * unit
    while t >= unit:
        if n % t == 0:
            return t
        t -= unit
    return n


_sigmoid = jax.nn.sigmoid


def _mm(a, b, *, mode, name, b_blocked=False, out_blocked=False, out_dtype=f32, tm=1024, tn=1024, tk=2048):
    if mode == "tn":
        K, M = a.shape
    else:
        M, K = a.shape
    if b_blocked:
        nb, mid, cb = b.shape
        if mode == "nn":
            assert mid == K
            N, tn = nb * cb, cb
        else:
            assert mode == "nt" and nb * cb == K
            N, tk = mid, cb
    else:
        N = b.shape[1] if mode in ("nn", "tn") else b.shape[0]
    tm = _tile(M, tm)
    tn = tn if (b_blocked and mode == "nn") or out_blocked else _tile(N, tn)
    tk = tk if b_blocked and mode == "nt" else _tile(K, tk)
    assert M % tm == 0 and N % tn == 0 and K % tk == 0
    nk = K // tk
    grid = (M // tm, N // tn, nk)
    if mode == "tn":
        a_spec = pl.BlockSpec((tk, tm), lambda i, j, k: (k, i))
    else:
        a_spec = pl.BlockSpec((tm, tk), lambda i, j, k: (i, k))
    if mode == "nn":
        b_spec = pl.BlockSpec((None, tk, cb), lambda i, j, k: (j, k, 0)) if b_blocked else pl.BlockSpec((tk, tn), lambda i, j, k: (k, j))
    elif mode == "nt":
        b_spec = pl.BlockSpec((None, tn, cb), lambda i, j, k: (k, j, 0)) if b_blocked else pl.BlockSpec((tn, tk), lambda i, j, k: (j, k))
    else:
        b_spec = pl.BlockSpec((tk, tn), lambda i, j, k: (k, j))
    if out_blocked:
        out_shape = jax.ShapeDtypeStruct((N // tn, M, tn), out_dtype)
        o_spec = pl.BlockSpec((None, tm, tn), lambda i, j, k: (j, i, 0))
    else:
        out_shape = jax.ShapeDtypeStruct((M, N), out_dtype)
        o_spec = pl.BlockSpec((tm, tn), lambda i, j, k: (i, j))
    dims = {"nn": ((1,), (0,)), "nt": ((1,), (1,)), "tn": ((0,), (0,))}[mode]

    def body(a_ref, b_ref, o_ref, *acc):
        p = lax.dot_general(a_ref[...].astype(bf16), b_ref[...].astype(bf16), (dims, ((), ())), preferred_element_type=f32)
        if nk == 1:
            o_ref[...] = p.astype(out_dtype)
        else:
            acc_ref = acc[0]
            k = pl.program_id(2)

            @pl.when(k == 0)
            def _():
                acc_ref[...] = p

            @pl.when(k > 0)
            def _():
                acc_ref[...] += p

            @pl.when(k == nk - 1)
            def _():
                o_ref[...] = acc_ref[...].astype(out_dtype)

    return pl.pallas_call(
        body, name=name, grid=grid, in_specs=[a_spec, b_spec], out_specs=o_spec, out_shape=out_shape,
        scratch_shapes=[pltpu.VMEM((tm, tn), f32)] if nk > 1 else [],
        compiler_params=_params(("parallel", "parallel", "arbitrary")),
    )(a, b)


def _rms_fwd(x, res, gain, name):
    T, D = x.shape
    tr = _tile(T, 256, 8)
    row = pl.BlockSpec((tr, D), lambda i: (i, 0))
    vec = pl.BlockSpec((1, D), lambda i: (0, 0))

    def body(*refs):
        if res is None:
            x_ref, g_ref, h_ref = refs
            xs = x_ref[...]
        else:
            x_ref, r_ref, g_ref, xs_ref, h_ref = refs
            xs = x_ref[...] + r_ref[...]
            xs_ref[...] = xs
        r = lax.rsqrt(jnp.mean(xs * xs, axis=-1, keepdims=True) + EPS)
        h_ref[...] = (xs * r * g_ref[...]).astype(bf16)

    h_shape = jax.ShapeDtypeStruct((T, D), bf16)
    if res is None:
        return pl.pallas_call(body, name=name, grid=(T // tr,), in_specs=[row, vec], out_specs=row, out_shape=h_shape,
                              compiler_params=_params(("parallel",)))(x, gain)
    return pl.pallas_call(body, name=name, grid=(T // tr,), in_specs=[row, row, vec], out_specs=(row, row),
                          out_shape=(jax.ShapeDtypeStruct((T, D), f32), h_shape), compiler_params=_params(("parallel",)))(x, res, gain)


def _rms_bwd(xs, gain, dh, extra, name):
    T, D = xs.shape
    tr = _tile(T, 256, 8)
    row = pl.BlockSpec((tr, D), lambda i: (i, 0))
    vec = pl.BlockSpec((1, D), lambda i: (0, 0))

    def body(x_ref, g_ref, dh_ref, e_ref, dx_ref, dg_ref):
        x = x_ref[...]
        r = lax.rsqrt(jnp.mean(x * x, axis=-1, keepdims=True) + EPS)
        xhat = x * r
        dh_v = dh_ref[...]
        gd = dh_v * g_ref[...]
        dx_ref[...] = e_ref[...] + r * (gd - xhat * jnp.mean(gd * xhat, axis=-1, keepdims=True))
        part = jnp.sum(dh_v * xhat, axis=0, keepdims=True)

        @pl.when(pl.program_id(0) == 0)
        def _():
            dg_ref[...] = part

        @pl.when(pl.program_id(0) > 0)
        def _():
            dg_ref[...] += part

    return pl.pallas_call(body, name=name, grid=(T // tr,), in_specs=[row, vec, row, row], out_specs=(row, vec),
                          out_shape=(jax.ShapeDtypeStruct((T, D), f32), jax.ShapeDtypeStruct((1, D), f32)),
                          compiler_params=_params(("arbitrary",)))(xs, gain, dh, extra)


def _merge_fwd(pa, pb, proj, b_gate, off, name):
    T, D = pa.shape
    tr, tc = _tile(T, 512, 8), _tile(D, 512)
    oa, ob, nb = off // tc, (off + D) // tc, D // tc
    blk = pl.BlockSpec((tr, tc), lambda i, j: (i, j))

    def body(pa_ref, pb_ref, ga_ref, gb_ref, ba_ref, bb_ref, o_ref):
        ga = _sigmoid(ga_ref[...] + ba_ref[...])
        gb = _sigmoid(gb_ref[...] + bb_ref[...])
        o_ref[...] = (ga * pa_ref[...] + gb * pb_ref[...]).astype(bf16)

    return pl.pallas_call(
        body, name=name, grid=(T // tr, nb),
        in_specs=[blk, blk, pl.BlockSpec((tr, tc), lambda i, j: (i, oa + j)), pl.BlockSpec((tr, tc), lambda i, j: (i, ob + j)),
                  pl.BlockSpec((1, tc), lambda i, j: (0, j)), pl.BlockSpec((1, tc), lambda i, j: (0, nb + j))],
        out_specs=blk, out_shape=jax.ShapeDtypeStruct((T, D), bf16), compiler_params=_params(("parallel", "parallel")),
    )(pa, pb, proj, proj, b_gate, b_gate)


def _branch_bwd(dm, p, proj, b_gate, off, boff, name):
    T, D = p.shape
    tr, tc = _tile(T, 512, 8), _tile(D, 512)
    og, obias = off // tc, boff // tc
    blk = pl.BlockSpec((tr, tc), lambda j, i: (i, j))
    vec = pl.BlockSpec((1, tc), lambda j, i: (0, j))

    def body(dm_ref, p_ref, gl_ref, b_ref, dp_ref, dgl_ref, db_ref):
        g = _sigmoid(gl_ref[...] + b_ref[...])
        dm_v = dm_ref[...]
        dp_ref[...] = (dm_v * g).astype(bf16)
        dgl = dm_v * p_ref[...] * g * (1.0 - g)
        dgl_ref[...] = dgl.astype(bf16)
        part = jnp.sum(dgl, axis=0, keepdims=True)

        @pl.when(pl.program_id(1) == 0)
        def _():
            db_ref[...] = part

        @pl.when(pl.program_id(1) > 0)
        def _():
            db_ref[...] += part

    return pl.pallas_call(
        body, name=name, grid=(D // tc, T // tr),
        in_specs=[blk, blk, pl.BlockSpec((tr, tc), lambda j, i: (i, og + j)), pl.BlockSpec((1, tc), lambda j, i: (0, obias + j))],
        out_specs=(blk, blk, vec),
        out_shape=(jax.ShapeDtypeStruct((T, D), bf16), jax.ShapeDtypeStruct((T, D), bf16), jax.ShapeDtypeStruct((1, D), f32)),
        compiler_params=_params(("parallel", "arbitrary")),
    )(dm, p, proj, b_gate)


def _swiglu_fwd(gu, name):
    T, F2 = gu.shape
    F = F2 // 2
    tr, tc = _tile(T, 512, 8), _tile(F, 512)
    nb = F // tc

    def body(g_ref, u_ref, o_ref):
        g = g_ref[...]
        o_ref[...] = (g * _sigmoid(g) * u_ref[...]).astype(bf16)

    return pl.pallas_call(
        body, name=name, grid=(T // tr, nb),
        in_specs=[pl.BlockSpec((tr, tc), lambda i, j: (i, j)), pl.BlockSpec((tr, tc), lambda i, j: (i, nb + j))],
        out_specs=pl.BlockSpec((tr, tc), lambda i, j: (i, j)), out_shape=jax.ShapeDtypeStruct((T, F), bf16),
        compiler_params=_params(("parallel", "parallel")),
    )(gu, gu)


def _swiglu_bwd(gu, dact, name):
    T, F2 = gu.shape
    F = F2 // 2
    tr, tc = _tile(T, 512, 8), _tile(F, 512)
    nb = F // tc
    blk = pl.BlockSpec((tr, tc), lambda i, j: (i, j))

    def body(g_ref, u_ref, d_ref, dg_ref, du_ref):
        g = g_ref[...]
        s = _sigmoid(g)
        d = d_ref[...]
        dg_ref[...] = (d * u_ref[...] * s * (1.0 + g * (1.0 - s))).astype(bf16)
        du_ref[...] = (d * g * s).astype(bf16)

    return pl.pallas_call(
        body, name=name, grid=(T // tr, nb),
        in_specs=[blk, pl.BlockSpec((tr, tc), lambda i, j: (i, nb + j)), blk], out_specs=(blk, blk),
        out_shape=(jax.ShapeDtypeStruct((T, F), bf16), jax.ShapeDtypeStruct((T, F), bf16)),
        compiler_params=_params(("parallel", "parallel")),
    )(gu, gu, dact)


def _loss_head(x1, fo, target, name):
    T, D = x1.shape
    tr = _tile(T, 256, 8)
    row = pl.BlockSpec((tr, D), lambda i: (i, 0))
    acc = pl.BlockSpec((8, 128), lambda i: (0, 0))

    def body(x_ref, f_ref, t_ref, dy_ref, l_ref):
        d = x_ref[...] + f_ref[...] - t_ref[...]
        dy_ref[...] = d * (1.0 / D)
        part = jnp.sum(jnp.sum(d * d, axis=1, keepdims=True), axis=0, keepdims=True)

        @pl.when(pl.program_id(0) == 0)
        def _():
            l_ref[...] = jnp.zeros((8, 128), f32)

        l_ref[...] += part

    return pl.pallas_call(body, name=name, grid=(T // tr,), in_specs=[row, row, row], out_specs=(row, acc),
                          out_shape=(jax.ShapeDtypeStruct((T, D), f32), jax.ShapeDtypeStruct((8, 128), f32)),
                          compiler_params=_params(("arbitrary",)))(x1, fo, target)


def _dotb(a, b, dims):
    return lax.dot_general(a.astype(bf16), b.astype(bf16), (dims, ((), ())), preferred_element_type=f32)


def _hgrn_chunk(q, fl, iv, g, logits, gain, st):
    lb = jax.nn.softmax(logits, axis=0)[0:1]
    f = lb + (1.0 - lb) * _sigmoid(fl)
    lf = jnp.log(f)
    kk = 1.0 - f
    qs = q * _sigmoid(q)
    row = lax.broadcasted_iota(jnp.int32, (CHUNK, CHUNK), 0)
    col = lax.broadcasted_iota(jnp.int32, (CHUNK, CHUNK), 1)
    tril = (col <= row).astype(f32)
    b = lax.dot_general(tril, lf, (((1,), (0,)), ((), ())), precision=HI, preferred_element_type=f32)
    b_last = jnp.sum(lf, axis=0, keepdims=True)
    o = _dotb(qs * jnp.exp(b), st, ((1,), (1,)))
    r3 = lax.broadcasted_iota(jnp.int32, (SUB, SUB, HEAD), 0)
    c3 = lax.broadcasted_iota(jnp.int32, (SUB, SUB, HEAD), 1)
    parts = []
    for i in range(CHUNK // SUB):
        lo, hi = i * SUB, (i + 1) * SUB
        bi = b[lo:hi]
        dec = jnp.exp(jnp.where(c3 <= r3, bi[:, None, :] - bi[None, :, :], -jnp.inf))
        s = jnp.sum(qs[lo:hi][:, None, :] * kk[lo:hi][None, :, :] * dec, axis=-1)
        if i > 0:
            anchor = jnp.max(bi, axis=0, keepdims=True)
            qa = qs[lo:hi] * jnp.exp(bi - anchor)
            kd = kk[:lo] * jnp.exp(anchor - b[:lo])
            s = jnp.concatenate([_dotb(qa, kd, ((1,), (1,))), s], axis=1)
        parts.append(_dotb(s, iv[:hi], ((1,), (0,))))
    o = o + jnp.concatenate(parts, axis=0)
    st_new = st * jnp.exp(b_last) + _dotb(iv, kk * jnp.exp(b_last - b), ((0,), (0,)))
    o = o * lax.rsqrt(jnp.mean(o * o, axis=-1, keepdims=True) + EPS)
    o = o * gain * (g * _sigmoid(g))
    return o, st_new


def _hgrn_fwd(proj, logits, gain, n_heads, name):
    T = proj.shape[0]
    nc = T // CHUNK
    H = n_heads

    def col(k):
        return pl.BlockSpec((CHUNK, HEAD), lambda h, c: (c, k * H + h))

    def body(q_ref, f_ref, i_ref, g_ref, l_ref, ga_ref, y_ref, s_ref, st):
        @pl.when(pl.program_id(1) == 0)
        def _():
            st[...] = jnp.zeros((HEAD, HEAD), f32)

        s_ref[...] = st[...]
        o, st_new = _hgrn_chunk(q_ref[...], f_ref[...], i_ref[...], g_ref[...], l_ref[...], ga_ref[...], st[...])
        y_ref[...] = o.astype(bf16)
        st[...] = st_new

    return pl.pallas_call(
        body, name=name, grid=(H, nc),
        in_specs=[col(0), col(1), col(2), col(3), pl.BlockSpec((2, HEAD), lambda h, c: (0, h)), pl.BlockSpec((1, HEAD), lambda h, c: (0, h))],
        out_specs=(pl.BlockSpec((CHUNK, HEAD), lambda h, c: (c, h)), pl.BlockSpec((None, None, HEAD, HEAD), lambda h, c: (h, c, 0, 0))),
        out_shape=(jax.ShapeDtypeStruct((T, H * HEAD), bf16), jax.ShapeDtypeStruct((H, nc, HEAD, HEAD), f32)),
        scratch_shapes=[pltpu.VMEM((HEAD, HEAD), f32)],
        compiler_params=_params(("parallel", "arbitrary")),
    )(proj, proj, proj, proj, logits, gain)


def _hgrn_bwd(proj, logits, gain, states, dy, n_heads, name):
    T = proj.shape[0]
    nc = T // CHUNK
    H = n_heads

    def col(k):
        return pl.BlockSpec((CHUNK, HEAD), lambda h, c: (nc - 1 - c, k * H + h))

    out_blk = pl.BlockSpec((CHUNK, HEAD), lambda h, c: (nc - 1 - c, h))

    def body(q_ref, f_ref, i_ref, g_ref, l_ref, ga_ref, s_ref, dy_ref, dq_ref, df_ref, di_ref, dg_ref, dl_ref, dga_ref, dst):
        first = pl.program_id(1) == 0

        @pl.when(first)
        def _():
            dst[...] = jnp.zeros((HEAD, HEAD), f32)

        _, vjp = jax.vjp(_hgrn_chunk, q_ref[...], f_ref[...], i_ref[...], g_ref[...], l_ref[...], ga_ref[...], s_ref[...])
        dq, df, di, dg, dl, dga, ds = vjp((dy_ref[...], dst[...]))
        dq_ref[...] = dq.astype(bf16)
        df_ref[...] = df.astype(bf16)
        di_ref[...] = di.astype(bf16)
        dg_ref[...] = dg.astype(bf16)
        dst[...] = ds

        @pl.when(first)
        def _():
            dl_ref[...] = dl
            dga_ref[...] = dga

        @pl.when(jnp.logical_not(first))
        def _():
            dl_ref[...] += dl
            dga_ref[...] += dga

    act = jax.ShapeDtypeStruct((T, H * HEAD), bf16)
    return pl.pallas_call(
        body, name=name, grid=(H, nc),
        in_specs=[col(0), col(1), col(2), col(3), pl.BlockSpec((2, HEAD), lambda h, c: (0, h)), pl.BlockSpec((1, HEAD), lambda h, c: (0, h)),
                  pl.BlockSpec((None, None, HEAD, HEAD), lambda h, c: (h, nc - 1 - c, 0, 0)), out_blk],
        out_specs=(out_blk, out_blk, out_blk, out_blk, pl.BlockSpec((2, HEAD), lambda h, c: (0, h)), pl.BlockSpec((1, HEAD), lambda h, c: (0, h))),
        out_shape=(act, act, act, act, jax.ShapeDtypeStruct((2, H * HEAD), f32), jax.ShapeDtypeStruct((1, H * HEAD), f32)),
        scratch_shapes=[pltpu.VMEM((HEAD, HEAD), f32)],
        compiler_params=_params(("parallel", "arbitrary")),
    )(proj, proj, proj, proj, logits, gain, states, dy)


def _rel_index():
    t = np.arange(CHUNK)[:, None]
    sp = np.arange(BAND * CHUNK)[None, :]
    dist = (N_PAST - sp // CHUNK) * CHUNK + t - sp % CHUNK
    return (np.clip(dist, -REL_FUTURE, REL_PAST) + REL_FUTURE).reshape(1, -1).astype(np.int32)


def _bias_table(rel_bias_pad, idx, name):
    H = rel_bias_pad.shape[0]
    n = idx.shape[1]
    tc = _tile(n, 4096)

    def body(rb_ref, idx_ref, o_ref):
        onehot = (lax.broadcasted_iota(jnp.int32, (N_REL_PAD, tc), 0) == idx_ref[...]).astype(f32)
        o_ref[...] = lax.dot_general(rb_ref[...], onehot, (((1,), (0,)), ((), ())), precision=HI, preferred_element_type=f32)

    return pl.pallas_call(
        body, name=name, grid=(n // tc,),
        in_specs=[pl.BlockSpec((H, N_REL_PAD), lambda j: (0, 0)), pl.BlockSpec((1, tc), lambda j: (0, j))],
        out_specs=pl.BlockSpec((H, tc), lambda j: (0, j)), out_shape=jax.ShapeDtypeStruct((H, n), f32),
        compiler_params=_params(("parallel",)),
    )(rel_bias_pad, idx)


def _bias_table_bwd(dbias, idx, name):
    H, n = dbias.shape
    tc = _tile(n, 4096)

    def body(d_ref, idx_ref, o_ref):
        onehot = (lax.broadcasted_iota(jnp.int32, (N_REL_PAD, tc), 0) == idx_ref[...]).astype(f32)
        part = lax.dot_general(d_ref[...], onehot, (((1,), (1,)), ((), ())), precision=HI, preferred_element_type=f32)

        @pl.when(pl.program_id(0) == 0)
        def _():
            o_ref[...] = part

        @pl.when(pl.program_id(0) > 0)
        def _():
            o_ref[...] += part

    return pl.pallas_call(
        body, name=name, grid=(n // tc,),
        in_specs=[pl.BlockSpec((H, tc), lambda j: (0, j)), pl.BlockSpec((1, tc), lambda j: (0, j))],
        out_specs=pl.BlockSpec((H, N_REL_PAD), lambda j: (0, 0)), out_shape=jax.ShapeDtypeStruct((H, N_REL_PAD), f32),
        compiler_params=_params(("arbitrary",)),
    )(dbias, idx)


def _head_norm(t, gain):
    return t * lax.rsqrt(jnp.mean(t * t, axis=-1, keepdims=True) + EPS) * gain


def _attn_chunk(q, kb, vb, qg, bias, n):
    qh = _head_norm(q, qg)
    s = _dotb(qh, kb, ((1,), (1,))) * (HEAD ** -0.5) + bias
    pos = n * CHUNK - PAD + lax.broadcasted_iota(jnp.int32, (1, BAND * CHUNK), 1)
    s = jnp.where(pos >= 0, s, NEG)
    e = jnp.exp(s - jnp.max(s, axis=-1, keepdims=True))
    p = e / jnp.sum(e, axis=-1, keepdims=True)
    return _dotb(p, vb, ((1,), (0,)))


def _attn_fwd(proj, q_gain, k_gain, bias, off, n_heads, name):
    T = proj.shape[0]
    nc = T // CHUNK
    H = n_heads
    o0 = off // HEAD
    full = lambda k: pl.BlockSpec((T, HEAD), lambda h, c: (0, o0 + k * H + h))
    vec = pl.BlockSpec((1, HEAD), lambda h, c: (0, 0))

    def body(q_ref, k_ref, v_ref, qg_ref, kg_ref, b_ref, y_ref, kp, vp):
        c = pl.program_id(1)

        @pl.when(c == 0)
        def _():
            kp[pl.ds(0, PAD), :] = jnp.zeros((PAD, HEAD), f32)
            vp[pl.ds(0, PAD), :] = jnp.zeros((PAD, HEAD), f32)
            kp[pl.ds(PAD, T), :] = _head_norm(k_ref[...], kg_ref[...])
            vp[pl.ds(PAD, T), :] = v_ref[...]

        band = pl.ds(pl.multiple_of(c * CHUNK, CHUNK), BAND * CHUNK)
        y_ref[...] = _attn_chunk(q_ref[...], kp[band, :], vp[band, :], qg_ref[...], b_ref[...], c).astype(bf16)

    return pl.pallas_call(
        body, name=name, grid=(H, nc),
        in_specs=[pl.BlockSpec((CHUNK, HEAD), lambda h, c: (c, o0 + h)), full(1), full(2), vec, vec,
                  pl.BlockSpec((None, CHUNK, BAND * CHUNK), lambda h, c: (h, 0, 0))],
        out_specs=pl.BlockSpec((CHUNK, HEAD), lambda h, c: (c, h)), out_shape=jax.ShapeDtypeStruct((T, H * HEAD), bf16),
        scratch_shapes=[pltpu.VMEM((T + PAD, HEAD), f32), pltpu.VMEM((T + PAD, HEAD), f32)],
        compiler_params=_params(("parallel", "arbitrary")),
    )(proj, proj, proj, q_gain, k_gain, bias)


def _attn_bwd(proj, q_gain, k_gain, bias, dy, off, n_heads, name):
    T = proj.shape[0]
    nc = T // CHUNK
    H = n_heads
    o0 = off // HEAD
    full = lambda k: pl.BlockSpec((T, HEAD), lambda h, c: (0, o0 + k * H + h))
    full_out = pl.BlockSpec((T, HEAD), lambda h, c: (0, h))
    vec = pl.BlockSpec((1, HEAD), lambda h, c: (0, 0))
    chunk_out = pl.BlockSpec((CHUNK, HEAD), lambda h, c: (c, h))
    bias_blk = pl.BlockSpec((None, CHUNK, BAND * CHUNK), lambda h, c: (h, 0, 0))

    def body(q_ref, k_ref, v_ref, qg_ref, kg_ref, b_ref, dy_ref, dq_ref, dk_ref, dv_ref, db_ref, dqg_ref, dkg_ref, kp, vp, dkp, dvp):
        h = pl.program_id(0)
        c = pl.program_id(1)

        @pl.when(c == 0)
        def _():
            kp[pl.ds(0, PAD), :] = jnp.zeros((PAD, HEAD), f32)
            vp[pl.ds(0, PAD), :] = jnp.zeros((PAD, HEAD), f32)
            kp[pl.ds(PAD, T), :] = _head_norm(k_ref[...], kg_ref[...])
            vp[pl.ds(PAD, T), :] = v_ref[...]
            dkp[...] = jnp.zeros((T + PAD, HEAD), f32)
            dvp[...] = jnp.zeros((T + PAD, HEAD), f32)
            db_ref[...] = jnp.zeros((CHUNK, BAND * CHUNK), f32)

        @pl.when(jnp.logical_and(h == 0, c == 0))
        def _():
            dqg_ref[...] = jnp.zeros((1, HEAD), f32)
            dkg_ref[...] = jnp.zeros((1, HEAD), f32)

        band = pl.ds(pl.multiple_of(c * CHUNK, CHUNK), BAND * CHUNK)
        _, vjp = jax.vjp(functools.partial(_attn_chunk, n=c), q_ref[...], kp[band, :], vp[band, :], qg_ref[...], b_ref[...])
        dq, dkb, dvb, dqg, db = vjp(dy_ref[...])
        dq_ref[...] = dq.astype(bf16)
        dkp[band, :] += dkb
        dvp[band, :] += dvb
        db_ref[...] += db
        dqg_ref[...] += dqg

        @pl.when(c == nc - 1)
        def _():
            _, nvjp = jax.vjp(_head_norm, k_ref[...], kg_ref[...])
            dk, dkg = nvjp(dkp[pl.ds(PAD, T), :])
            dk_ref[...] = dk.astype(bf16)
            dv_ref[...] = dvp[pl.ds(PAD, T), :].astype(bf16)
            dkg_ref[...] += dkg

    act = jax.ShapeDtypeStruct((T, H * HEAD), bf16)
    gvec = jax.ShapeDtypeStruct((1, HEAD), f32)
    pad_buf = pltpu.VMEM((T + PAD, HEAD), f32)
    return pl.pallas_call(
        body, name=name, grid=(H, nc),
        in_specs=[pl.BlockSpec((CHUNK, HEAD), lambda h, c: (c, o0 + h)), full(1), full(2), vec, vec, bias_blk, chunk_out],
        out_specs=(chunk_out, full_out, full_out, bias_blk, vec, vec),
        out_shape=(act, act, act, jax.ShapeDtypeStruct((H, CHUNK, BAND * CHUNK), f32), gvec, gvec),
        scratch_shapes=[pad_buf, pad_buf, pad_buf, pad_buf],
        compiler_params=_params(("arbitrary", "arbitrary")),
    )(proj, proj, proj, q_gain, k_gain, bias, dy)


def _position():
    x, y, c = lax.axis_index("x"), lax.axis_index("y"), lax.axis_index("c")
    return x, y, c, 4 * x + 2 * y + c


def _flip(v, bit):
    return 1 - v if bit else v


def _chips(x, y):
    return [(1 - x, y), (x, 1 - y), (1 - x, 1 - y)]


def _seq_gather(shards, name, collective_id):
    n = len(shards)

    def body(*refs):
        ins, outs = refs[:n], refs[n:2 * n]
        send, recv, loc = refs[2 * n:]
        x, y, c, me = _position()
        sib = (x, y, 1 - c)
        chips = _chips(x, y)
        barrier = pltpu.get_barrier_semaphore()
        for peer in [sib] + [(px, py, c) for px, py in chips]:
            pl.semaphore_signal(barrier, inc=1, device_id=peer, device_id_type=MESH)
        pl.semaphore_wait(barrier, 4)

        def copy(w, k, src, blk, to):
            return pltpu.make_async_remote_copy(src_ref=src, dst_ref=outs[w].at[blk], send_sem=send.at[7 * w + k], recv_sem=recv.at[7 * w + k],
                                                device_id=to, device_id_type=MESH)

        mine = [pltpu.make_async_copy(ins[w], outs[w].at[me], loc.at[w]) for w in range(n)]
        for cp in mine:
            cp.start()
        first = []
        for j, (px, py) in enumerate(chips):
            first += [copy(w, 1 + j, ins[w], me, (px, py, c)) for w in range(n)]
        first += [copy(w, 0, ins[w], me, sib) for w in range(n)]
        for cp in first:
            cp.start()
        passed = []
        for j, (px, py) in enumerate(chips):
            blk = 4 * px + 2 * py + c
            for w in range(n):
                copy(w, 1 + j, ins[w], blk, sib).wait_recv()
                fwd = copy(w, 4 + j, outs[w].at[blk], blk, sib)
                fwd.start()
                passed.append(fwd)
        for w in range(n):
            copy(w, 0, ins[w], 4 * x + 2 * y + (1 - c), sib).wait_recv()
        for j, (px, py) in enumerate(chips):
            for w in range(n):
                copy(w, 4 + j, ins[w], 4 * px + 2 * py + (1 - c), sib).wait_recv()
        for cp in first + passed:
            cp.wait_send()
        for cp in mine:
            cp.wait()

    return pl.kernel(
        body, out_type=tuple(jax.ShapeDtypeStruct((NDEV,) + s.shape, s.dtype) for s in shards),
        mesh=plsc.ScalarSubcoreMesh(axis_name="sequencer", num_cores=1), name=name,
        scratch_types=(pltpu.SemaphoreType.DMA((7 * n,)), pltpu.SemaphoreType.DMA((7 * n,)), pltpu.SemaphoreType.DMA((n,))),
        compiler_params=pltpu.CompilerParams(collective_id=collective_id),
    )(*shards)


def _seq_scatter(grads, name, collective_id, after=()):
    n, na = len(grads), len(after)

    def body(*refs):
        ins, outs = refs[:n], refs[n + na:2 * n + na]
        send, recv, loc = refs[2 * n + na:]
        x, y, c, me = _position()
        peers = [(_flip(x, r & 4), _flip(y, r & 2), _flip(c, r & 1)) for r in range(1, NDEV)]
        barrier = pltpu.get_barrier_semaphore()
        for peer in peers:
            pl.semaphore_signal(barrier, inc=1, device_id=peer, device_id_type=MESH)
        pl.semaphore_wait(barrier, NDEV - 1)
        mine = [pltpu.make_async_copy(ins[w].at[me], outs[w].at[me], loc.at[w]) for w in range(n)]
        for cp in mine:
            cp.start()
        sends, waits = [], []
        for k, (px, py, pc) in enumerate(peers):
            peer = 4 * px + 2 * py + pc
            for w in range(n):
                sends.append(pltpu.make_async_remote_copy(src_ref=ins[w].at[peer], dst_ref=outs[w].at[me], send_sem=send.at[7 * w + k],
                                                          recv_sem=recv.at[7 * w + k], device_id=(px, py, pc), device_id_type=MESH))
                waits.append(pltpu.make_async_remote_copy(src_ref=ins[w].at[peer], dst_ref=outs[w].at[peer], send_sem=send.at[7 * w + k],
                                                          recv_sem=recv.at[7 * w + k], device_id=(px, py, pc), device_id_type=MESH))
        for cp in sends:
            cp.start()
        for cp in waits:
            cp.wait_recv()
        for cp in sends:
            cp.wait_send()
        for cp in mine:
            cp.wait()

    return pl.kernel(
        body, out_type=tuple(jax.ShapeDtypeStruct(g.shape, g.dtype) for g in grads),
        mesh=plsc.ScalarSubcoreMesh(axis_name="sequencer", num_cores=1), name=name,
        scratch_types=(pltpu.SemaphoreType.DMA((7 * n,)), pltpu.SemaphoreType.DMA((7 * n,)), pltpu.SemaphoreType.DMA((n,))),
        compiler_params=pltpu.CompilerParams(collective_id=collective_id),
    )(*grads, *after)


def _small_all_reduce(v, name):
    R, C = v.shape

    def body(v_ref, o_ref, buf, send, recv):
        x, y, c, me = _position()
        buf[me] = v_ref[...]
        sends, waits = [], []
        for r in range(1, NDEV):
            px, py, pc = _flip(x, r & 4), _flip(y, r & 2), _flip(c, r & 1)
            peer = 4 * px + 2 * py + pc
            sends.append(pltpu.make_async_remote_copy(src_ref=v_ref, dst_ref=buf.at[me], send_sem=send.at[r - 1], recv_sem=recv.at[r - 1],
                                                      device_id=(px, py, pc), device_id_type=MESH))
            waits.append(pltpu.make_async_remote_copy(src_ref=v_ref, dst_ref=buf.at[peer], send_sem=send.at[r - 1], recv_sem=recv.at[r - 1],
                                                      device_id=(px, py, pc), device_id_type=MESH))
        for cp in sends:
            cp.start()
        for cp in waits:
            cp.wait_recv()
        for cp in sends:
            cp.wait_send()
        acc = buf[0]
        for i in range(1, NDEV):
            acc = acc + buf[i]
        o_ref[...] = acc

    vm = pl.BlockSpec(memory_space=pltpu.VMEM)
    return pl.pallas_call(
        body, name=name, in_specs=[vm], out_specs=vm, out_shape=jax.ShapeDtypeStruct((R, C), f32),
        scratch_shapes=[pltpu.VMEM((NDEV, R, C), f32), pltpu.SemaphoreType.DMA((7,)), pltpu.SemaphoreType.DMA((7,))],
    )(v)


def _adamw_math(w, g, m, v):
    m = ADAM_B1 * m + (1.0 - ADAM_B1) * g
    v = ADAM_B2 * v + (1.0 - ADAM_B2) * (g * g)
    m_hat = m / (1.0 - ADAM_B1 ** ADAM_STEP)
    v_hat = v / (1.0 - ADAM_B2 ** ADAM_STEP)
    delta = -ADAM_LR * (m_hat / (jnp.sqrt(v_hat) + ADAM_EPS) + ADAM_WD * w)
    return delta, m, v


def _adamw_parts(w, m, v, parts, name):
    R, C = w.shape
    tr = _tile(R, 128, 16)
    blk = pl.BlockSpec((tr, C), lambda i: (i, 0))

    def body(w_ref, m_ref, v_ref, p_ref, g_ref, d_ref, mo_ref, vo_ref):
        g = p_ref[0].astype(f32)
        for i in range(1, NDEV):
            g = g + p_ref[i].astype(f32)
        d, mn, vn = _adamw_math(w_ref[...], g, m_ref[...], v_ref[...])
        g_ref[...] = g
        d_ref[...] = d
        mo_ref[...] = mn
        vo_ref[...] = vn

    shp = jax.ShapeDtypeStruct((R, C), f32)
    return pl.pallas_call(
        body, name=name, grid=(R // tr,), in_specs=[blk, blk, blk, pl.BlockSpec((NDEV, tr, C), lambda i: (0, i, 0))],
        out_specs=(blk, blk, blk, blk), out_shape=(shp, shp, shp, shp), compiler_params=_params(("parallel",)),
    )(w, m, v, parts)


def _adamw_small(w, g, m, v, name):
    def body(w_ref, g_ref, m_ref, v_ref, d_ref, mo_ref, vo_ref):
        d, mn, vn = _adamw_math(w_ref[...], g_ref[...], m_ref[...], v_ref[...])
        d_ref[...] = d
        mo_ref[...] = mn
        vo_ref[...] = vn

    shp = jax.ShapeDtypeStruct(w.shape, f32)
    return pl.pallas_call(body, name=name, out_shape=(shp, shp, shp))(w, g, m, v)


SMALL_COLS = 1024


def _pack(arrs):
    flat = jnp.concatenate([a.reshape(-1) for a in arrs])
    rows = -(-flat.shape[0] // (8 * SMALL_COLS)) * 8
    return jnp.pad(flat, (0, rows * SMALL_COLS - flat.shape[0])).reshape(rows, SMALL_COLS)


def _unpack(packed, like):
    flat = packed.reshape(-1)
    out, pos = [], 0
    for a in like:
        out.append(flat[pos:pos + a.size].reshape(a.shape))
        pos += a.size
    return out


def kernel(x, w_in, b_gate, norm_mix, norm_ffn, hgrn_lb_logits, hgrn_out_gain, q_gain, k_gain, rel_bias, w_proj_a, w_proj_b, w_out, w_ffn_in, w_ffn_out, loss_target, m_w_in, m_b_gate, m_norm_mix, m_norm_ffn, m_hgrn_lb_logits, m_hgrn_out_gain, m_q_gain, m_k_gain, m_rel_bias, m_w_proj_a, m_w_proj_b, m_w_out, m_w_ffn_in, m_w_ffn_out, v_w_in, v_b_gate, v_norm_mix, v_norm_ffn, v_hgrn_lb_logits, v_hgrn_out_gain, v_q_gain, v_k_gain, v_rel_bias, v_w_proj_a, v_w_proj_b, v_w_out, v_w_ffn_in, v_w_ffn_out):
    xs = x[0]
    target = loss_target[0]
    T, D = xs.shape
    d_a = hgrn_out_gain.shape[-1]
    H = d_a // HEAD
    d_b = d_a
    off_b = 4 * d_a
    off_g = off_b + 3 * d_b
    assert rel_bias.shape[1] == H and T % CHUNK == 0 and T // CHUNK > N_PAST

    big_w = [w_in[0], w_proj_a[0], w_proj_b[0], w_out[0], w_ffn_in[0], w_ffn_out[0]]
    big_m = [m_w_in[0], m_w_proj_a[0], m_w_proj_b[0], m_w_out[0], m_w_ffn_in[0], m_w_ffn_out[0]]
    big_v = [v_w_in[0], v_w_proj_a[0], v_w_proj_b[0], v_w_out[0], v_w_ffn_in[0], v_w_ffn_out[0]]

    sh = [w.astype(bf16) for w in big_w]
    (g_in,) = _seq_gather(sh[0:1], "gather_a", 1)
    g_pa, g_pb, g_out = _seq_gather(sh[1:4], "gather_b", 2)
    g_fin, g_fout = _seq_gather(sh[4:6], "gather_c", 3)

    h = _rms_fwd(xs, None, norm_mix, "rms_mix")
    proj = _mm(h, g_in, mode="nn", b_blocked=True, name="mm_proj")
    y_a, states = _hgrn_fwd(proj, hgrn_lb_logits, hgrn_out_gain, H, "hgrn_fwd")
    idx = jnp.asarray(_rel_index())
    rb_pad = jnp.pad(rel_bias[0], ((0, 0), (0, N_REL_PAD - N_REL)))
    bias = _bias_table(rb_pad, idx, "bias_table").reshape(H, CHUNK, BAND * CHUNK)
    y_b = _attn_fwd(proj, q_gain, k_gain, bias, off_b, H, "attn_fwd")
    wg_out = g_out.reshape(-1, g_out.shape[-1])
    wg_fout = g_fout.reshape(-1, g_fout.shape[-1])
    pa = _mm(y_a, g_pa, mode="nn", b_blocked=True, tm=2048, name="mm_proj_a")
    pb = _mm(y_b, g_pb, mode="nn", b_blocked=True, tm=2048, name="mm_proj_b")
    merged = _merge_fwd(pa, pb, proj, b_gate, off_g, "merge_fwd")
    mo = _mm(merged, wg_out, mode="nn", name="mm_out")
    x1, h2 = _rms_fwd(xs, mo, norm_ffn, "rms_ffn")
    gu = _mm(h2, g_fin, mode="nn", b_blocked=True, name="mm_ffn_in")
    act = _swiglu_fwd(gu, "swiglu_fwd")
    fo = _mm(act, wg_fout, mode="nn", tk=2816, name="mm_ffn_out")
    dy, loss_acc = _loss_head(x1, fo, target, "loss_head")
    loss_part = loss_acc[0:1, 0:1] * (0.5 / D)

    dact = _mm(dy, wg_fout, mode="nt", tn=1408, name="mm_d_act")
    gw_fout = _mm(act, dy, mode="tn", out_dtype=bf16, tm=1408, name="mm_gw_ffn_out")
    dgate, dup = _swiglu_bwd(gu, dact, "swiglu_bwd")
    dgu = jnp.concatenate([dgate, dup], axis=1)
    gw_fin = _mm(h2, dgu, mode="tn", out_blocked=True, out_dtype=bf16, tn=g_fin.shape[-1], name="mm_gw_ffn_in")
    dh2 = _mm(dgu, g_fin, mode="nt", b_blocked=True, name="mm_d_h2")
    dx1, g_norm_ffn = _rms_bwd(x1, norm_ffn, dh2, dy, "rms_ffn_bwd")

    dmerged = _mm(dx1, wg_out, mode="nt", name="mm_d_merged")
    gw_out = _mm(merged, dx1, mode="tn", out_dtype=bf16, name="mm_gw_out")
    dpa, dgl_a, gb_a = _branch_bwd(dmerged, pa, proj, b_gate, off_g, 0, "branch_a_bwd")
    dpb, dgl_b, gb_b = _branch_bwd(dmerged, pb, proj, b_gate, off_g + D, D, "branch_b_bwd")
    dy_a = _mm(dpa, g_pa, mode="nt", b_blocked=True, tm=2048, name="mm_d_ya")
    dy_b = _mm(dpb, g_pb, mode="nt", b_blocked=True, tm=2048, name="mm_d_yb")
    gw_pa = _mm(y_a, dpa, mode="tn", out_blocked=True, out_dtype=bf16, tn=g_pa.shape[-1], name="mm_gw_proj_a")
    gw_pb = _mm(y_b, dpb, mode="tn", out_blocked=True, out_dtype=bf16, tn=g_pb.shape[-1], name="mm_gw_proj_b")

    dq_a, df_a, di_a, dg_a, g_logits, g_gain = _hgrn_bwd(proj, hgrn_lb_logits, hgrn_out_gain, states, dy_a, H, "hgrn_bwd")
    dq_b, dk_b, dv_b, dbias, g_qg, g_kg = _attn_bwd(proj, q_gain, k_gain, bias, dy_b, off_b, H, "attn_bwd")
    g_rel = _bias_table_bwd(dbias.reshape(H, -1), idx, "bias_table_bwd")[:, :N_REL]
    dproj = jnp.concatenate([dq_a, df_a, di_a, dg_a, dq_b, dk_b, dv_b, dgl_a, dgl_b], axis=1)
    gw_in = _mm(h, dproj, mode="tn", out_blocked=True, out_dtype=bf16, tn=g_in.shape[-1], name="mm_gw_in")
    dh = _mm(dproj, g_in, mode="nt", b_blocked=True, name="mm_d_h")
    grad_x, g_norm_mix = _rms_bwd(xs, norm_mix, dh, dx1, "rms_mix_bwd")

    (p_fout,) = _seq_scatter([gw_fout.reshape(NDEV, -1, D)], "scatter_a", 4)
    (p_fin,) = _seq_scatter([gw_fin], "scatter_b", 5, after=(g_norm_ffn,))
    p_out, p_pa, p_pb = _seq_scatter([gw_out.reshape(NDEV, -1, D), gw_pa, gw_pb], "scatter_c", 6, after=(g_gain,))
    (p_in,) = _seq_scatter([gw_in], "scatter_d", 7)
    parts = [p_in, p_pa, p_pb, p_out, p_fin, p_fout]
    names = ["w_in", "w_proj_a", "w_proj_b", "w_out", "w_ffn_in", "w_ffn_out"]
    big = {}
    for nm, w, m, v, p in zip(names, big_w, big_m, big_v, parts):
        big[nm] = [o[None] for o in _adamw_parts(w, m, v, p, "adamw_" + nm)]

    small_names = ["b_gate", "norm_mix", "norm_ffn", "hgrn_lb_logits", "hgrn_out_gain", "q_gain", "k_gain", "rel_bias"]
    small_w = [b_gate, norm_mix, norm_ffn, hgrn_lb_logits, hgrn_out_gain, q_gain, k_gain, rel_bias]
    small_m = [m_b_gate, m_norm_mix, m_norm_ffn, m_hgrn_lb_logits, m_hgrn_out_gain, m_q_gain, m_k_gain, m_rel_bias]
    small_v = [v_b_gate, v_norm_mix, v_norm_ffn, v_hgrn_lb_logits, v_hgrn_out_gain, v_q_gain, v_k_gain, v_rel_bias]
    small_g = [jnp.concatenate([gb_a, gb_b], axis=1), g_norm_mix, g_norm_ffn, g_logits, g_gain, g_qg, g_kg, g_rel[None], loss_part]
    g_sum = _small_all_reduce(_pack(small_g), "reduce_small")
    loss = _unpack(g_sum, small_g)[-1].reshape(())
    d_s, m_s, v_s = _adamw_small(_pack(small_w), g_sum, _pack(small_m), _pack(small_v), "adamw_small")
    small = {}
    for nm, g, d, m, v in zip(small_names, _unpack(g_sum, small_w), _unpack(d_s, small_w), _unpack(m_s, small_w), _unpack(v_s, small_w)):
        small[nm] = [g, d, m, v]

    order = ["w_in", "b_gate", "norm_mix", "norm_ffn", "hgrn_lb_logits", "hgrn_out_gain", "q_gain", "k_gain", "rel_bias",
             "w_proj_a", "w_proj_b", "w_out", "w_ffn_in", "w_ffn_out"]
    res = {**big, **small}
    outs = [loss, grad_x[None]]
    for k in range(4):
        outs += [res[nm][k] for nm in order]
    return tuple(outs)
```

```python
import functools

import numpy as np
import jax
import jax.numpy as jnp
from jax import lax
from jax.experimental import pallas as pl
from jax.experimental.pallas import tpu as pltpu
from jax.experimental.pallas import tpu_sc as plsc

f32 = jnp.float32
bf16 = jnp.bfloat16
HI = lax.Precision.HIGHEST
MESH = pl.DeviceIdType.MESH
AXES = ("x", "y", "c")
NDEV = 8

CHUNK = 64
HEAD = 128
SUB = 16
N_PAST = 8
BAND = N_PAST + 1
PAD = N_PAST * CHUNK
REL_FUTURE = CHUNK - 1
REL_PAST = 2 * CHUNK - 1
N_REL = REL_FUTURE + REL_PAST + 1
N_REL_PAD = 256
EPS = 1e-6
NEG = -1e30

ADAM_LR = 0.001
ADAM_B1 = 0.9
ADAM_B2 = 0.999
ADAM_EPS = 1e-08
ADAM_WD = 0.01
ADAM_STEP = 10

VMEM_LIMIT = 56 * 1024 * 1024


def _params(sem=None):
    return pltpu.CompilerParams(dimension_semantics=sem, vmem_limit_bytes=VMEM_LIMIT)


def _tile(n, pref, unit=128):
    if n <= pref:
        return n
    t = (pref // unit) * unit
    while t >= unit:
        if n % t == 0:
            return t
        t -= unit
    return n


_sigmoid = jax.nn.sigmoid


def _mm(a, b, *, mode, name, b_blocked=False, out_blocked=False, out_dtype=f32, tm=1024, tn=1024, tk=2048):
    if mode == "tn":
        K, M = a.shape
    else:
        M, K = a.shape
    if b_blocked:
        nb, mid, cb = b.shape
        if mode == "nn":
            assert mid == K
            N, tn = nb * cb, cb
        else:
            assert mode == "nt" and nb * cb == K
            N, tk = mid, cb
    else:
        N = b.shape[1] if mode in ("nn", "tn") else b.shape[0]
    tm = _tile(M, tm)
    tn = tn if (b_blocked and mode == "nn") or out_blocked else _tile(N, tn)
    tk = tk if b_blocked and mode == "nt" else _tile(K, tk)
    assert M % tm == 0 and N % tn == 0 and K % tk == 0
    nk = K // tk
    grid = (M // tm, N // tn, nk)
    if mode == "tn":
        a_spec = pl.BlockSpec((tk, tm), lambda i, j, k: (k, i))
    else:
        a_spec = pl.BlockSpec((tm, tk), lambda i, j, k: (i, k))
    if mode == "nn":
        b_spec = pl.BlockSpec((None, tk, cb), lambda i, j, k: (j, k, 0)) if b_blocked else pl.BlockSpec((tk, tn), lambda i, j, k: (k, j))
    elif mode == "nt":
        b_spec = pl.BlockSpec((None, tn, cb), lambda i, j, k: (k, j, 0)) if b_blocked else pl.BlockSpec((tn, tk), lambda i, j, k: (j, k))
    else:
        b_spec = pl.BlockSpec((tk, tn), lambda i, j, k: (k, j))
    if out_blocked:
        out_shape = jax.ShapeDtypeStruct((N // tn, M, tn), out_dtype)
        o_spec = pl.BlockSpec((None, tm, tn), lambda i, j, k: (j, i, 0))
    else:
        out_shape = jax.ShapeDtypeStruct((M, N), out_dtype)
        o_spec = pl.BlockSpec((tm, tn), lambda i, j, k: (i, j))
    dims = {"nn": ((1,), (0,)), "nt": ((1,), (1,)), "tn": ((0,), (0,))}[mode]

    def body(a_ref, b_ref, o_ref, *acc):
        p = lax.dot_general(a_ref[...].astype(bf16), b_ref[...].astype(bf16), (dims, ((), ())), preferred_element_type=f32)
        if nk == 1:
            o_ref[...] = p.astype(out_dtype)
        else:
            acc_ref = acc[0]
            k = pl.program_id(2)

            @pl.when(k == 0)
            def _():
                acc_ref[...] = p

            @pl.when(k > 0)
            def _():
                acc_ref[...] += p

            @pl.when(k == nk - 1)
            def _():
                o_ref[...] = acc_ref[...].astype(out_dtype)

    return pl.pallas_call(
        body, name=name, grid=grid, in_specs=[a_spec, b_spec], out_specs=o_spec, out_shape=out_shape,
        scratch_shapes=[pltpu.VMEM((tm, tn), f32)] if nk > 1 else [],
        compiler_params=_params(("parallel", "parallel", "arbitrary")),
    )(a, b)


def _rms_fwd(x, res, gain, name):
    T, D = x.shape
    tr = _tile(T, 256, 8)
    row = pl.BlockSpec((tr, D), lambda i: (i, 0))
    vec = pl.BlockSpec((1, D), lambda i: (0, 0))

    def body(*refs):
        if res is None:
            x_ref, g_ref, h_ref = refs
            xs = x_ref[...]
        else:
            x_ref, r_ref, g_ref, xs_ref, h_ref = refs
            xs = x_ref[...] + r_ref[...]
            xs_ref[...] = xs
        r = lax.rsqrt(jnp.mean(xs * xs, axis=-1, keepdims=True) + EPS)
        h_ref[...] = (xs * r * g_ref[...]).astype(bf16)

    h_shape = jax.ShapeDtypeStruct((T, D), bf16)
    if res is None:
        return pl.pallas_call(body, name=name, grid=(T // tr,), in_specs=[row, vec], out_specs=row, out_shape=h_shape,
                              compiler_params=_params(("parallel",)))(x, gain)
    return pl.pallas_call(body, name=name, grid=(T // tr,), in_specs=[row, row, vec], out_specs=(row, row),
                          out_shape=(jax.ShapeDtypeStruct((T, D), f32), h_shape), compiler_params=_params(("parallel",)))(x, res, gain)


def _rms_bwd(xs, gain, dh, extra, name):
    T, D = xs.shape
    tr = _tile(T, 256, 8)
    row = pl.BlockSpec((tr, D), lambda i: (i, 0))
    vec = pl.BlockSpec((1, D), lambda i: (0, 0))

    def body(x_ref, g_ref, dh_ref, e_ref, dx_ref, dg_ref):
        x = x_ref[...]
        r = lax.rsqrt(jnp.mean(x * x, axis=-1, keepdims=True) + EPS)
        xhat = x * r
        dh_v = dh_ref[...]
        gd = dh_v * g_ref[...]
        dx_ref[...] = e_ref[...] + r * (gd - xhat * jnp.mean(gd * xhat, axis=-1, keepdims=True))
        part = jnp.sum(dh_v * xhat, axis=0, keepdims=True)

        @pl.when(pl.program_id(0) == 0)
        def _():
            dg_ref[...] = part

        @pl.when(pl.program_id(0) > 0)
        def _():
            dg_ref[...] += part

    return pl.pallas_call(body, name=name, grid=(T // tr,), in_specs=[row, vec, row, row], out_specs=(row, vec),
                          out_shape=(jax.ShapeDtypeStruct((T, D), f32), jax.ShapeDtypeStruct((1, D), f32)),
                          compiler_params=_params(("arbitrary",)))(xs, gain, dh, extra)


def _merge_fwd(pa, pb, proj, b_gate, off, name):
    T, D = pa.shape
    tr, tc = _tile(T, 512, 8), _tile(D, 512)
    oa, ob, nb = off // tc, (off + D) // tc, D // tc
    blk = pl.BlockSpec((tr, tc), lambda i, j: (i, j))

    def body(pa_ref, pb_ref, ga_ref, gb_ref, ba_ref, bb_ref, o_ref):
        ga = _sigmoid(ga_ref[...] + ba_ref[...])
        gb = _sigmoid(gb_ref[...] + bb_ref[...])
        o_ref[...] = (ga * pa_ref[...] + gb * pb_ref[...]).astype(bf16)

    return pl.pallas_call(
        body, name=name, grid=(T // tr, nb),
        in_specs=[blk, blk, pl.BlockSpec((tr, tc), lambda i, j: (i, oa + j)), pl.BlockSpec((tr, tc), lambda i, j: (i, ob + j)),
                  pl.BlockSpec((1, tc), lambda i, j: (0, j)), pl.BlockSpec((1, tc), lambda i, j: (0, nb + j))],
        out_specs=blk, out_shape=jax.ShapeDtypeStruct((T, D), bf16), compiler_params=_params(("parallel", "parallel")),
    )(pa, pb, proj, proj, b_gate, b_gate)


def _branch_bwd(dm, p, proj, b_gate, off, boff, name):
    T, D = p.shape
    tr, tc = _tile(T, 512, 8), _tile(D, 512)
    og, obias = off // tc, boff // tc
    blk = pl.BlockSpec((tr, tc), lambda j, i: (i, j))
    vec = pl.BlockSpec((1, tc), lambda j, i: (0, j))

    def body(dm_ref, p_ref, gl_ref, b_ref, dp_ref, dgl_ref, db_ref):
        g = _sigmoid(gl_ref[...] + b_ref[...])
        dm_v = dm_ref[...]
        dp_ref[...] = (dm_v * g).astype(bf16)
        dgl = dm_v * p_ref[...] * g * (1.0 - g)
        dgl_ref[...] = dgl.astype(bf16)
        part = jnp.sum(dgl, axis=0, keepdims=True)

        @pl.when(pl.program_id(1) == 0)
        def _():
            db_ref[...] = part

        @pl.when(pl.program_id(1) > 0)
        def _():
            db_ref[...] += part

    return pl.pallas_call(
        body, name=name, grid=(D // tc, T // tr),
        in_specs=[blk, blk, pl.BlockSpec((tr, tc), lambda j, i: (i, og + j)), pl.BlockSpec((1, tc), lambda j, i: (0, obias + j))],
        out_specs=(blk, blk, vec),
        out_shape=(jax.ShapeDtypeStruct((T, D), bf16), jax.ShapeDtypeStruct((T, D), bf16), jax.ShapeDtypeStruct((1, D), f32)),
        compiler_params=_params(("parallel", "arbitrary")),
    )(dm, p, proj, b_gate)


def _swiglu_fwd(gu, name):
    T, F2 = gu.shape
    F = F2 // 2
    tr, tc = _tile(T, 512, 8), _tile(F, 512)
    nb = F // tc

    def body(g_ref, u_ref, o_ref):
        g = g_ref[...]
        o_ref[...] = (g * _sigmoid(g) * u_ref[...]).astype(bf16)

    return pl.pallas_call(
        body, name=name, grid=(T // tr, nb),
        in_specs=[pl.BlockSpec((tr, tc), lambda i, j: (i, j)), pl.BlockSpec((tr, tc), lambda i, j: (i, nb + j))],
        out_specs=pl.BlockSpec((tr, tc), lambda i, j: (i, j)), out_shape=jax.ShapeDtypeStruct((T, F), bf16),
        compiler_params=_params(("parallel", "parallel")),
    )(gu, gu)


def _swiglu_bwd(gu, dact, name):
    T, F2 = gu.shape
    F = F2 // 2
    tr, tc = _tile(T, 512, 8), _tile(F, 512)
    nb = F // tc
    blk = pl.BlockSpec((tr, tc), lambda i, j: (i, j))

    def body(g_ref, u_ref, d_ref, dg_ref, du_ref):
        g = g_ref[...]
        s = _sigmoid(g)
        d = d_ref[...]
        dg_ref[...] = (d * u_ref[...] * s * (1.0 + g * (1.0 - s))).astype(bf16)
        du_ref[...] = (d * g * s).astype(bf16)

    return pl.pallas_call(
        body, name=name, grid=(T // tr, nb),
        in_specs=[blk, pl.BlockSpec((tr, tc), lambda i, j: (i, nb + j)), blk], out_specs=(blk, blk),
        out_shape=(jax.ShapeDtypeStruct((T, F), bf16), jax.ShapeDtypeStruct((T, F), bf16)),
        compiler_params=_params(("parallel", "parallel")),
    )(gu, gu, dact)


def _loss_head(x1, fo, target, name):
    T, D = x1.shape
    tr = _tile(T, 256, 8)
    row = pl.BlockSpec((tr, D), lambda i: (i, 0))
    acc = pl.BlockSpec((8, 128), lambda i: (0, 0))

    def body(x_ref, f_ref, t_ref, dy_ref, l_ref):
        d = x_ref[...] + f_ref[...] - t_ref[...]
        dy_ref[...] = d * (1.0 / D)
        part = jnp.sum(jnp.sum(d * d, axis=1, keepdims=True), axis=0, keepdims=True)

        @pl.when(pl.program_id(0) == 0)
        def _():
            l_ref[...] = jnp.zeros((8, 128), f32)

        l_ref[...] += part

    return pl.pallas_call(body, name=name, grid=(T // tr,), in_specs=[row, row, row], out_specs=(row, acc),
                          out_shape=(jax.ShapeDtypeStruct((T, D), f32), jax.ShapeDtypeStruct((8, 128), f32)),
                          compiler_params=_params(("arbitrary",)))(x1, fo, target)


def _dotb(a, b, dims):
    return lax.dot_general(a.astype(bf16), b.astype(bf16), (dims, ((), ())), preferred_element_type=f32)


def _hgrn_chunk(q, fl, iv, g, logits, gain, st):
    lb = jax.nn.softmax(logits, axis=0)[0:1]
    f = lb + (1.0 - lb) * _sigmoid(fl)
    lf = jnp.log(f)
    kk = 1.0 - f
    qs = q * _sigmoid(q)
    row = lax.broadcasted_iota(jnp.int32, (CHUNK, CHUNK), 0)
    col = lax.broadcasted_iota(jnp.int32, (CHUNK, CHUNK), 1)
    tril = (col <= row).astype(f32)
    b = lax.dot_general(tril, lf, (((1,), (0,)), ((), ())), precision=HI, preferred_element_type=f32)
    b_last = jnp.sum(lf, axis=0, keepdims=True)
    o = _dotb(qs * jnp.exp(b), st, ((1,), (1,)))
    r3 = lax.broadcasted_iota(jnp.int32, (SUB, SUB, HEAD), 0)
    c3 = lax.broadcasted_iota(jnp.int32, (SUB, SUB, HEAD), 1)
    parts = []
    for i in range(CHUNK // SUB):
        lo, hi = i * SUB, (i + 1) * SUB
        bi = b[lo:hi]
        dec = jnp.exp(jnp.where(c3 <= r3, bi[:, None, :] - bi[None, :, :], -jnp.inf))
        s = jnp.sum(qs[lo:hi][:, None, :] * kk[lo:hi][None, :, :] * dec, axis=-1)
        if i > 0:
            anchor = jnp.max(bi, axis=0, keepdims=True)
            qa = qs[lo:hi] * jnp.exp(bi - anchor)
            kd = kk[:lo] * jnp.exp(anchor - b[:lo])
            s = jnp.concatenate([_dotb(qa, kd, ((1,), (1,))), s], axis=1)
        parts.append(_dotb(s, iv[:hi], ((1,), (0,))))
    o = o + jnp.concatenate(parts, axis=0)
    st_new = st * jnp.exp(b_last) + _dotb(iv, kk * jnp.exp(b_last - b), ((0,), (0,)))
    o = o * lax.rsqrt(jnp.mean(o * o, axis=-1, keepdims=True) + EPS)
    o = o * gain * (g * _sigmoid(g))
    return o, st_new


def _hgrn_fwd(proj, logits, gain, n_heads, name):
    T = proj.shape[0]
    nc = T // CHUNK
    H = n_heads

    def col(k):
        return pl.BlockSpec((CHUNK, HEAD), lambda h, c: (c, k * H + h))

    def body(q_ref, f_ref, i_ref, g_ref, l_ref, ga_ref, y_ref, s_ref, st):
        @pl.when(pl.program_id(1) == 0)
        def _():
            st[...] = jnp.zeros((HEAD, HEAD), f32)

        s_ref[...] = st[...]
        o, st_new = _hgrn_chunk(q_ref[...], f_ref[...], i_ref[...], g_ref[...], l_ref[...], ga_ref[...], st[...])
        y_ref[...] = o.astype(bf16)
        st[...] = st_new

    return pl.pallas_call(
        body, name=name, grid=(H, nc),
        in_specs=[col(0), col(1), col(2), col(3), pl.BlockSpec((2, HEAD), lambda h, c: (0, h)), pl.BlockSpec((1, HEAD), lambda h, c: (0, h))],
        out_specs=(pl.BlockSpec((CHUNK, HEAD), lambda h, c: (c, h)), pl.BlockSpec((None, None, HEAD, HEAD), lambda h, c: (h, c, 0, 0))),
        out_shape=(jax.ShapeDtypeStruct((T, H * HEAD), bf16), jax.ShapeDtypeStruct((H, nc, HEAD, HEAD), f32)),
        scratch_shapes=[pltpu.VMEM((HEAD, HEAD), f32)],
        compiler_params=_params(("parallel", "arbitrary")),
    )(proj, proj, proj, proj, logits, gain)


def _hgrn_bwd(proj, logits, gain, states, dy, n_heads, name):
    T = proj.shape[0]
    nc = T // CHUNK
    H = n_heads

    def col(k):
        return pl.BlockSpec((CHUNK, HEAD), lambda h, c: (nc - 1 - c, k * H + h))

    out_blk = pl.BlockSpec((CHUNK, HEAD), lambda h, c: (nc - 1 - c, h))

    def body(q_ref, f_ref, i_ref, g_ref, l_ref, ga_ref, s_ref, dy_ref, dq_ref, df_ref, di_ref, dg_ref, dl_ref, dga_ref, dst):
        first = pl.program_id(1) == 0

        @pl.when(first)
        def _():
            dst[...] = jnp.zeros((HEAD, HEAD), f32)

        _, vjp = jax.vjp(_hgrn_chunk, q_ref[...], f_ref[...], i_ref[...], g_ref[...], l_ref[...], ga_ref[...], s_ref[...])
        dq, df, di, dg, dl, dga, ds = vjp((dy_ref[...], dst[...]))
        dq_ref[...] = dq.astype(bf16)
        df_ref[...] = df.astype(bf16)
        di_ref[...] = di.astype(bf16)
        dg_ref[...] = dg.astype(bf16)
        dst[...] = ds

        @pl.when(first)
        def _():
            dl_ref[...] = dl
            dga_ref[...] = dga

        @pl.when(jnp.logical_not(first))
        def _():
            dl_ref[...] += dl
            dga_ref[...] += dga

    act = jax.ShapeDtypeStruct((T, H * HEAD), bf16)
    return pl.pallas_call(
        body, name=name, grid=(H, nc),
        in_specs=[col(0), col(1), col(2), col(3), pl.BlockSpec((2, HEAD), lambda h, c: (0, h)), pl.BlockSpec((1, HEAD), lambda h, c: (0, h)),
                  pl.BlockSpec((None, None, HEAD, HEAD), lambda h, c: (h, nc - 1 - c, 0, 0)), out_blk],
        out_specs=(out_blk, out_blk, out_blk, out_blk, pl.BlockSpec((2, HEAD), lambda h, c: (0, h)), pl.BlockSpec((1, HEAD), lambda h, c: (0, h))),
        out_shape=(act, act, act, act, jax.ShapeDtypeStruct((2, H * HEAD), f32), jax.ShapeDtypeStruct((1, H * HEAD), f32)),
        scratch_shapes=[pltpu.VMEM((HEAD, HEAD), f32)],
        compiler_params=_params(("parallel", "arbitrary")),
    )(proj, proj, proj, proj, logits, gain, states, dy)


def _rel_index():
    t = np.arange(CHUNK)[:, None]
    sp = np.arange(BAND * CHUNK)[None, :]
    dist = (N_PAST - sp // CHUNK) * CHUNK + t - sp % CHUNK
    return (np.clip(dist, -REL_FUTURE, REL_PAST) + REL_FUTURE).reshape(1, -1).astype(np.int32)


def _bias_table(rel_bias_pad, idx, name):
    H = rel_bias_pad.shape[0]
    n = idx.shape[1]
    tc = _tile(n, 4096)

    def body(rb_ref, idx_ref, o_ref):
        onehot = (lax.broadcasted_iota(jnp.int32, (N_REL_PAD, tc), 0) == idx_ref[...]).astype(f32)
        o_ref[...] = lax.dot_general(rb_ref[...], onehot, (((1,), (0,)), ((), ())), precision=HI, preferred_element_type=f32)

    return pl.pallas_call(
        body, name=name, grid=(n // tc,),
        in_specs=[pl.BlockSpec((H, N_REL_PAD), lambda j: (0, 0)), pl.BlockSpec((1, tc), lambda j: (0, j))],
        out_specs=pl.BlockSpec((H, tc), lambda j: (0, j)), out_shape=jax.ShapeDtypeStruct((H, n), f32),
        compiler_params=_params(("parallel",)),
    )(rel_bias_pad, idx)


def _bias_table_bwd(dbias, idx, name):
    H, n = dbias.shape
    tc = _tile(n, 4096)

    def body(d_ref, idx_ref, o_ref):
        onehot = (lax.broadcasted_iota(jnp.int32, (N_REL_PAD, tc), 0) == idx_ref[...]).astype(f32)
        part = lax.dot_general(d_ref[...], onehot, (((1,), (1,)), ((), ())), precision=HI, preferred_element_type=f32)

        @pl.when(pl.program_id(0) == 0)
        def _():
            o_ref[...] = part

        @pl.when(pl.program_id(0) > 0)
        def _():
            o_ref[...] += part

    return pl.pallas_call(
        body, name=name, grid=(n // tc,),
        in_specs=[pl.BlockSpec((H, tc), lambda j: (0, j)), pl.BlockSpec((1, tc), lambda j: (0, j))],
        out_specs=pl.BlockSpec((H, N_REL_PAD), lambda j: (0, 0)), out_shape=jax.ShapeDtypeStruct((H, N_REL_PAD), f32),
        compiler_params=_params(("arbitrary",)),
    )(dbias, idx)


def _head_norm(t, gain):
    return t * lax.rsqrt(jnp.mean(t * t, axis=-1, keepdims=True) + EPS) * gain


def _attn_chunk(q, kb, vb, qg, bias, n):
    qh = _head_norm(q, qg)
    s = _dotb(qh, kb, ((1,), (1,))) * (HEAD ** -0.5) + bias
    pos = n * CHUNK - PAD + lax.broadcasted_iota(jnp.int32, (1, BAND * CHUNK), 1)
    s = jnp.where(pos >= 0, s, NEG)
    e = jnp.exp(s - jnp.max(s, axis=-1, keepdims=True))
    p = e / jnp.sum(e, axis=-1, keepdims=True)
    return _dotb(p, vb, ((1,), (0,)))


def _attn_fwd(proj, q_gain, k_gain, bias, off, n_heads, name):
    T = proj.shape[0]
    nc = T // CHUNK
    H = n_heads
    o0 = off // HEAD
    full = lambda k: pl.BlockSpec((T, HEAD), lambda h, c: (0, o0 + k * H + h))
    vec = pl.BlockSpec((1, HEAD), lambda h, c: (0, 0))

    def body(q_ref, k_ref, v_ref, qg_ref, kg_ref, b_ref, y_ref, kp, vp):
        c = pl.program_id(1)

        @pl.when(c == 0)
        def _():
            kp[pl.ds(0, PAD), :] = jnp.zeros((PAD, HEAD), f32)
            vp[pl.ds(0, PAD), :] = jnp.zeros((PAD, HEAD), f32)
            kp[pl.ds(PAD, T), :] = _head_norm(k_ref[...], kg_ref[...])
            vp[pl.ds(PAD, T), :] = v_ref[...]

        band = pl.ds(pl.multiple_of(c * CHUNK, CHUNK), BAND * CHUNK)
        y_ref[...] = _attn_chunk(q_ref[...], kp[band, :], vp[band, :], qg_ref[...], b_ref[...], c).astype(bf16)

    return pl.pallas_call(
        body, name=name, grid=(H, nc),
        in_specs=[pl.BlockSpec((CHUNK, HEAD), lambda h, c: (c, o0 + h)), full(1), full(2), vec, vec,
                  pl.BlockSpec((None, CHUNK, BAND * CHUNK), lambda h, c: (h, 0, 0))],
        out_specs=pl.BlockSpec((CHUNK, HEAD), lambda h, c: (c, h)), out_shape=jax.ShapeDtypeStruct((T, H * HEAD), bf16),
        scratch_shapes=[pltpu.VMEM((T + PAD, HEAD), f32), pltpu.VMEM((T + PAD, HEAD), f32)],
        compiler_params=_params(("parallel", "arbitrary")),
    )(proj, proj, proj, q_gain, k_gain, bias)


def _attn_bwd(proj, q_gain, k_gain, bias, dy, off, n_heads, name):
    T = proj.shape[0]
    nc = T // CHUNK
    H = n_heads
    o0 = off // HEAD
    full = lambda k: pl.BlockSpec((T, HEAD), lambda h, c: (0, o0 + k * H + h))
    full_out = pl.BlockSpec((T, HEAD), lambda h, c: (0, h))
    vec = pl.BlockSpec((1, HEAD), lambda h, c: (0, 0))
    chunk_out = pl.BlockSpec((CHUNK, HEAD), lambda h, c: (c, h))
    bias_blk = pl.BlockSpec((None, CHUNK, BAND * CHUNK), lambda h, c: (h, 0, 0))

    def body(q_ref, k_ref, v_ref, qg_ref, kg_ref, b_ref, dy_ref, dq_ref, dk_ref, dv_ref, db_ref, dqg_ref, dkg_ref, kp, vp, dkp, dvp):
        h = pl.program_id(0)
        c = pl.program_id(1)

        @pl.when(c == 0)
        def _():
            kp[pl.ds(0, PAD), :] = jnp.zeros((PAD, HEAD), f32)
            vp[pl.ds(0, PAD), :] = jnp.zeros((PAD, HEAD), f32)
            kp[pl.ds(PAD, T), :] = _head_norm(k_ref[...], kg_ref[...])
            vp[pl.ds(PAD, T), :] = v_ref[...]
            dkp[...] = jnp.zeros((T + PAD, HEAD), f32)
            dvp[...] = jnp.zeros((T + PAD, HEAD), f32)
            db_ref[...] = jnp.zeros((CHUNK, BAND * CHUNK), f32)

        @pl.when(jnp.logical_and(h == 0, c == 0))
        def _():
            dqg_ref[...] = jnp.zeros((1, HEAD), f32)
            dkg_ref[...] = jnp.zeros((1, HEAD), f32)

        band = pl.ds(pl.multiple_of(c * CHUNK, CHUNK), BAND * CHUNK)
        _, vjp = jax.vjp(functools.partial(_attn_chunk, n=c), q_ref[...], kp[band, :], vp[band, :], qg_ref[...], b_ref[...])
        dq, dkb, dvb, dqg, db = vjp(dy_ref[...])
        dq_ref[...] = dq.astype(bf16)
        dkp[band, :] += dkb
        dvp[band, :] += dvb
        db_ref[...] += db
        dqg_ref[...] += dqg

        @pl.when(c == nc - 1)
        def _():
            _, nvjp = jax.vjp(_head_norm, k_ref[...], kg_ref[...])
            dk, dkg = nvjp(dkp[pl.ds(PAD, T), :])
            dk_ref[...] = dk.astype(bf16)
            dv_ref[...] = dvp[pl.ds(PAD, T), :].astype(bf16)
            dkg_ref[...] += dkg

    act = jax.ShapeDtypeStruct((T, H * HEAD), bf16)
    gvec = jax.ShapeDtypeStruct((1, HEAD), f32)
    pad_buf = pltpu.VMEM((T + PAD, HEAD), f32)
    return pl.pallas_call(
        body, name=name, grid=(H, nc),
        in_specs=[pl.BlockSpec((CHUNK, HEAD), lambda h, c: (c, o0 + h)), full(1), full(2), vec, vec, bias_blk, chunk_out],
        out_specs=(chunk_out, full_out, full_out, bias_blk, vec, vec),
        out_shape=(act, act, act, jax.ShapeDtypeStruct((H, CHUNK, BAND * CHUNK), f32), gvec, gvec),
        scratch_shapes=[pad_buf, pad_buf, pad_buf, pad_buf],
        compiler_params=_params(("arbitrary", "arbitrary")),
    )(proj, proj, proj, q_gain, k_gain, bias, dy)


def _position():
    x, y, c = lax.axis_index("x"), lax.axis_index("y"), lax.axis_index("c")
    return x, y, c, 4 * x + 2 * y + c


def _flip(v, bit):
    return 1 - v if bit else v


def _chips(x, y):
    return [(1 - x, y), (x, 1 - y), (1 - x, 1 - y)]


def _seq_gather(shards, name, collective_id):
    n = len(shards)

    def body(*refs):
        ins, outs = refs[:n], refs[n:2 * n]
        send, recv, loc = refs[2 * n:]
        x, y, c, me = _position()
        sib = (x, y, 1 - c)
        chips = _chips(x, y)
        barrier = pltpu.get_barrier_semaphore()
        for peer in [sib] + [(px, py, c) for px, py in chips]:
            pl.semaphore_signal(barrier, inc=1, device_id=peer, device_id_type=MESH)
        pl.semaphore_wait(barrier, 4)

        def copy(w, k, src, blk, to):
            return pltpu.make_async_remote_copy(src_ref=src, dst_ref=outs[w].at[blk], send_sem=send.at[7 * w + k], recv_sem=recv.at[7 * w + k],
                                                device_id=to, device_id_type=MESH)

        mine = [pltpu.make_async_copy(ins[w], outs[w].at[me], loc.at[w]) for w in range(n)]
        for cp in mine:
            cp.start()
        first = []
        for j, (px, py) in enumerate(chips):
            first += [copy(w, 1 + j, ins[w], me, (px, py, c)) for w in range(n)]
        first += [copy(w, 0, ins[w], me, sib) for w in range(n)]
        for cp in first:
            cp.start()
        passed = []
        for j, (px, py) in enumerate(chips):
            blk = 4 * px + 2 * py + c
            for w in range(n):
                copy(w, 1 + j, ins[w], blk, sib).wait_recv()
                fwd = copy(w, 4 + j, outs[w].at[blk], blk, sib)
                fwd.start()
                passed.append(fwd)
        for w in range(n):
            copy(w, 0, ins[w], 4 * x + 2 * y + (1 - c), sib).wait_recv()
        for j, (px, py) in enumerate(chips):
            for w in range(n):
                copy(w, 4 + j, ins[w], 4 * px + 2 * py + (1 - c), sib).wait_recv()
        for cp in first + passed:
            cp.wait_send()
        for cp in mine:
            cp.wait()

    return pl.kernel(
        body, out_type=tuple(jax.ShapeDtypeStruct((NDEV,) + s.shape, s.dtype) for s in shards),
        mesh=plsc.ScalarSubcoreMesh(axis_name="sequencer", num_cores=1), name=name,
        scratch_types=(pltpu.SemaphoreType.DMA((7 * n,)), pltpu.SemaphoreType.DMA((7 * n,)), pltpu.SemaphoreType.DMA((n,))),
        compiler_params=pltpu.CompilerParams(collective_id=collective_id),
    )(*shards)


def _seq_scatter(grads, name, collective_id, after=()):
    n, na = len(grads), len(after)

    def body(*refs):
        ins, outs = refs[:n], refs[n + na:2 * n + na]
        send, recv, loc = refs[2 * n + na:]
        x, y, c, me = _position()
        peers = [(_flip(x, r & 4), _flip(y, r & 2), _flip(c, r & 1)) for r in range(1, NDEV)]
        barrier = pltpu.get_barrier_semaphore()
        for peer in peers:
            pl.semaphore_signal(barrier, inc=1, device_id=peer, device_id_type=MESH)
        pl.semaphore_wait(barrier, NDEV - 1)
        mine = [pltpu.make_async_copy(ins[w].at[me], outs[w].at[me], loc.at[w]) for w in range(n)]
        for cp in mine:
            cp.start()
        sends, waits = [], []
        for k, (px, py, pc) in enumerate(peers):
            peer = 4 * px + 2 * py + pc
            for w in range(n):
                sends.append(pltpu.make_async_remote_copy(src_ref=ins[w].at[peer], dst_ref=outs[w].at[me], send_sem=send.at[7 * w + k],
                                                          recv_sem=recv.at[7 * w + k], device_id=(px, py, pc), device_id_type=MESH))
                waits.append(pltpu.make_async_remote_copy(src_ref=ins[w].at[peer], dst_ref=outs[w].at[peer], send_sem=send.at[7 * w + k],
                                                          recv_sem=recv.at[7 * w + k], device_id=(px, py, pc), device_id_type=MESH))
        for cp in sends:
            cp.start()
        for cp in waits:
            cp.wait_recv()
        for cp in sends:
            cp.wait_send()
        for cp in mine:
            cp.wait()

    return pl.kernel(
        body, out_type=tuple(jax.ShapeDtypeStruct(g.shape, g.dtype) for g in grads),
        mesh=plsc.ScalarSubcoreMesh(axis_name="sequencer", num_cores=1), name=name,
        scratch_types=(pltpu.SemaphoreType.DMA((7 * n,)), pltpu.SemaphoreType.DMA((7 * n,)), pltpu.SemaphoreType.DMA((n,))),
        compiler_params=pltpu.CompilerParams(collective_id=collective_id),
    )(*grads, *after)


def _small_all_reduce(v, name):
    R, C = v.shape

    def body(v_ref, o_ref, buf, send, recv):
        x, y, c, me = _position()
        buf[me] = v_ref[...]
        sends, waits = [], []
        for r in range(1, NDEV):
            px, py, pc = _flip(x, r & 4), _flip(y, r & 2), _flip(c, r & 1)
            peer = 4 * px + 2 * py + pc
            sends.append(pltpu.make_async_remote_copy(src_ref=v_ref, dst_ref=buf.at[me], send_sem=send.at[r - 1], recv_sem=recv.at[r - 1],
                                                      device_id=(px, py, pc), device_id_type=MESH))
            waits.append(pltpu.make_async_remote_copy(src_ref=v_ref, dst_ref=buf.at[peer], send_sem=send.at[r - 1], recv_sem=recv.at[r - 1],
                                                      device_id=(px, py, pc), device_id_type=MESH))
        for cp in sends:
            cp.start()
        for cp in waits:
            cp.wait_recv()
        for cp in sends:
            cp.wait_send()
        acc = buf[0]
        for i in range(1, NDEV):
            acc = acc + buf[i]
        o_ref[...] = acc

    vm = pl.BlockSpec(memory_space=pltpu.VMEM)
    return pl.pallas_call(
        body, name=name, in_specs=[vm], out_specs=vm, out_shape=jax.ShapeDtypeStruct((R, C), f32),
        scratch_shapes=[pltpu.VMEM((NDEV, R, C), f32), pltpu.SemaphoreType.DMA((7,)), pltpu.SemaphoreType.DMA((7,))],
    )(v)


def _adamw_math(w, g, m, v):
    m = ADAM_B1 * m + (1.0 - ADAM_B1) * g
    v = ADAM_B2 * v + (1.0 - ADAM_B2) * (g * g)
    m_hat = m / (1.0 - ADAM_B1 ** ADAM_STEP)
    v_hat = v / (1.0 - ADAM_B2 ** ADAM_STEP)
    delta = -ADAM_LR * (m_hat / (jnp.sqrt(v_hat) + ADAM_EPS) + ADAM_WD * w)
    return delta, m, v


def _adamw_parts(w, m, v, parts, name, after=()):
    R, C = w.shape
    tr = _tile(R, 128, 16)
    blk = pl.BlockSpec((tr, C), lambda i: (i, 0))

    def body(w_ref, m_ref, v_ref, p_ref, *rest):
        g_ref, d_ref, mo_ref, vo_ref = rest[len(after):]
        g = p_ref[0].astype(f32)
        for i in range(1, NDEV):
            g = g + p_ref[i].astype(f32)
        d, mn, vn = _adamw_math(w_ref[...], g, m_ref[...], v_ref[...])
        g_ref[...] = g
        d_ref[...] = d
        mo_ref[...] = mn
        vo_ref[...] = vn

    shp = jax.ShapeDtypeStruct((R, C), f32)
    return pl.pallas_call(
        body, name=name, grid=(R // tr,),
        in_specs=[blk, blk, blk, pl.BlockSpec((NDEV, tr, C), lambda i: (0, i, 0))] + [pl.BlockSpec(a.shape, lambda i: (0, 0)) for a in after],
        out_specs=(blk, blk, blk, blk), out_shape=(shp, shp, shp, shp), compiler_params=_params(("parallel",)),
    )(w, m, v, parts, *after)


def _adamw_small(w, g, m, v, name):
    def body(w_ref, g_ref, m_ref, v_ref, d_ref, mo_ref, vo_ref):
        d, mn, vn = _adamw_math(w_ref[...], g_ref[...], m_ref[...], v_ref[...])
        d_ref[...] = d
        mo_ref[...] = mn
        vo_ref[...] = vn

    shp = jax.ShapeDtypeStruct(w.shape, f32)
    return pl.pallas_call(body, name=name, out_shape=(shp, shp, shp))(w, g, m, v)


SMALL_COLS = 1024


def _pack(arrs):
    flat = jnp.concatenate([a.reshape(-1) for a in arrs])
    rows = -(-flat.shape[0] // (8 * SMALL_COLS)) * 8
    return jnp.pad(flat, (0, rows * SMALL_COLS - flat.shape[0])).reshape(rows, SMALL_COLS)


def _unpack(packed, like):
    flat = packed.reshape(-1)
    out, pos = [], 0
    for a in like:
        out.append(flat[pos:pos + a.size].reshape(a.shape))
        pos += a.size
    return out


def kernel(x, w_in, b_gate, norm_mix, norm_ffn, hgrn_lb_logits, hgrn_out_gain, q_gain, k_gain, rel_bias, w_proj_a, w_proj_b, w_out, w_ffn_in, w_ffn_out, loss_target, m_w_in, m_b_gate, m_norm_mix, m_norm_ffn, m_hgrn_lb_logits, m_hgrn_out_gain, m_q_gain, m_k_gain, m_rel_bias, m_w_proj_a, m_w_proj_b, m_w_out, m_w_ffn_in, m_w_ffn_out, v_w_in, v_b_gate, v_norm_mix, v_norm_ffn, v_hgrn_lb_logits, v_hgrn_out_gain, v_q_gain, v_k_gain, v_rel_bias, v_w_proj_a, v_w_proj_b, v_w_out, v_w_ffn_in, v_w_ffn_out):
    xs = x[0]
    target = loss_target[0]
    T, D = xs.shape
    d_a = hgrn_out_gain.shape[-1]
    H = d_a // HEAD
    d_b = d_a
    off_b = 4 * d_a
    off_g = off_b + 3 * d_b
    assert rel_bias.shape[1] == H and T % CHUNK == 0 and T // CHUNK > N_PAST

    big_w = [w_in[0], w_proj_a[0], w_proj_b[0], w_out[0], w_ffn_in[0], w_ffn_out[0]]
    big_m = [m_w_in[0], m_w_proj_a[0], m_w_proj_b[0], m_w_out[0], m_w_ffn_in[0], m_w_ffn_out[0]]
    big_v = [v_w_in[0], v_w_proj_a[0], v_w_proj_b[0], v_w_out[0], v_w_ffn_in[0], v_w_ffn_out[0]]

    sh = [w.astype(bf16) for w in big_w]
    (g_in,) = _seq_gather(sh[0:1], "gather_a", 1)
    g_pa, g_pb, g_out = _seq_gather(sh[1:4], "gather_b", 2)
    g_fin, g_fout = _seq_gather(sh[4:6], "gather_c", 3)

    h = _rms_fwd(xs, None, norm_mix, "rms_mix")
    proj = _mm(h, g_in, mode="nn", b_blocked=True, name="mm_proj")
    y_a, states = _hgrn_fwd(proj, hgrn_lb_logits, hgrn_out_gain, H, "hgrn_fwd")
    idx = jnp.asarray(_rel_index())
    rb_pad = jnp.pad(rel_bias[0], ((0, 0), (0, N_REL_PAD - N_REL)))
    bias = _bias_table(rb_pad, idx, "bias_table").reshape(H, CHUNK, BAND * CHUNK)
    y_b = _attn_fwd(proj, q_gain, k_gain, bias, off_b, H, "attn_fwd")
    wg_out = g_out.reshape(-1, g_out.shape[-1])
    wg_fout = g_fout.reshape(-1, g_fout.shape[-1])
    pa = _mm(y_a, g_pa, mode="nn", b_blocked=True, tm=2048, name="mm_proj_a")
    pb = _mm(y_b, g_pb, mode="nn", b_blocked=True, tm=2048, name="mm_proj_b")
    merged = _merge_fwd(pa, pb, proj, b_gate, off_g, "merge_fwd")
    mo = _mm(merged, wg_out, mode="nn", name="mm_out")
    x1, h2 = _rms_fwd(xs, mo, norm_ffn, "rms_ffn")
    gu = _mm(h2, g_fin, mode="nn", b_blocked=True, name="mm_ffn_in")
    act = _swiglu_fwd(gu, "swiglu_fwd")
    fo = _mm(act, wg_fout, mode="nn", tk=2816, name="mm_ffn_out")
    dy, loss_acc = _loss_head(x1, fo, target, "loss_head")
    loss_part = loss_acc[0:1, 0:1] * (0.5 / D)

    dact = _mm(dy, wg_fout, mode="nt", tn=1408, name="mm_d_act")
    gw_fout = _mm(act, dy, mode="tn", out_dtype=bf16, tm=1408, name="mm_gw_ffn_out")
    dgate, dup = _swiglu_bwd(gu, dact, "swiglu_bwd")
    dgu = jnp.concatenate([dgate, dup], axis=1)
    gw_fin = _mm(h2, dgu, mode="tn", out_blocked=True, out_dtype=bf16, tn=g_fin.shape[-1], name="mm_gw_ffn_in")
    dh2 = _mm(dgu, g_fin, mode="nt", b_blocked=True, name="mm_d_h2")
    dx1, g_norm_ffn = _rms_bwd(x1, norm_ffn, dh2, dy, "rms_ffn_bwd")

    dmerged = _mm(dx1, wg_out, mode="nt", name="mm_d_merged")
    gw_out = _mm(merged, dx1, mode="tn", out_dtype=bf16, name="mm_gw_out")
    dpa, dgl_a, gb_a = _branch_bwd(dmerged, pa, proj, b_gate, off_g, 0, "branch_a_bwd")
    dpb, dgl_b, gb_b = _branch_bwd(dmerged, pb, proj, b_gate, off_g + D, D, "branch_b_bwd")
    dy_a = _mm(dpa, g_pa, mode="nt", b_blocked=True, tm=2048, name="mm_d_ya")
    dy_b = _mm(dpb, g_pb, mode="nt", b_blocked=True, tm=2048, name="mm_d_yb")
    gw_pa = _mm(y_a, dpa, mode="tn", out_blocked=True, out_dtype=bf16, tn=g_pa.shape[-1], name="mm_gw_proj_a")
    gw_pb = _mm(y_b, dpb, mode="tn", out_blocked=True, out_dtype=bf16, tn=g_pb.shape[-1], name="mm_gw_proj_b")

    dq_a, df_a, di_a, dg_a, g_logits, g_gain = _hgrn_bwd(proj, hgrn_lb_logits, hgrn_out_gain, states, dy_a, H, "hgrn_bwd")
    dq_b, dk_b, dv_b, dbias, g_qg, g_kg = _attn_bwd(proj, q_gain, k_gain, bias, dy_b, off_b, H, "attn_bwd")
    g_rel = _bias_table_bwd(dbias.reshape(H, -1), idx, "bias_table_bwd")[:, :N_REL]
    dproj = jnp.concatenate([dq_a, df_a, di_a, dg_a, dq_b, dk_b, dv_b, dgl_a, dgl_b], axis=1)
    gw_in = _mm(h, dproj, mode="tn", out_blocked=True, out_dtype=bf16, tn=g_in.shape[-1], name="mm_gw_in")
    dh = _mm(dproj, g_in, mode="nt", b_blocked=True, name="mm_d_h")
    grad_x, g_norm_mix = _rms_bwd(xs, norm_mix, dh, dx1, "rms_mix_bwd")

    p_fout, p_fin = _seq_scatter([gw_fout.reshape(NDEV, -1, D), gw_fin], "scatter_a", 4, after=(g_norm_ffn,))
    p_out, p_pa, p_pb = _seq_scatter([gw_out.reshape(NDEV, -1, D), gw_pa, gw_pb], "scatter_b", 5, after=(g_gain,))
    (p_in,) = _seq_scatter([gw_in], "scatter_c", 6)
    parts = [p_in, p_pa, p_pb, p_out, p_fin, p_fout]
    names = ["w_in", "w_proj_a", "w_proj_b", "w_out", "w_ffn_in", "w_ffn_out"]
    big = {}
    for nm, w, m, v, p in zip(names, big_w, big_m, big_v, parts):
        big[nm] = [o[None] for o in _adamw_parts(w, m, v, p, "adamw_" + nm, after=() if nm == "w_in" else (g_norm_mix,))]

    small_names = ["b_gate", "norm_mix", "norm_ffn", "hgrn_lb_logits", "hgrn_out_gain", "q_gain", "k_gain", "rel_bias"]
    small_w = [b_gate, norm_mix, norm_ffn, hgrn_lb_logits, hgrn_out_gain, q_gain, k_gain, rel_bias]
    small_m = [m_b_gate, m_norm_mix, m_norm_ffn, m_hgrn_lb_logits, m_hgrn_out_gain, m_q_gain, m_k_gain, m_rel_bias]
    small_v = [v_b_gate, v_norm_mix, v_norm_ffn, v_hgrn_lb_logits, v_hgrn_out_gain, v_q_gain, v_k_gain, v_rel_bias]
    small_g = [jnp.concatenate([gb_a, gb_b], axis=1), g_norm_mix, g_norm_ffn, g_logits, g_gain, g_qg, g_kg, g_rel[None], loss_part]
    g_sum = _small_all_reduce(_pack(small_g), "reduce_small")
    loss = _unpack(g_sum, small_g)[-1].reshape(())
    d_s, m_s, v_s = _adamw_small(_pack(small_w), g_sum, _pack(small_m), _pack(small_v), "adamw_small")
    small = {}
    for nm, g, d, m, v in zip(small_names, _unpack(g_sum, small_w), _unpack(d_s, small_w), _unpack(m_s, small_w), _unpack(v_s, small_w)):
        small[nm] = [g, d, m, v]

    order = ["w_in", "b_gate", "norm_mix", "norm_ffn", "hgrn_lb_logits", "hgrn_out_gain", "q_gain", "k_gain", "rel_bias",
             "w_proj_a", "w_proj_b", "w_out", "w_ffn_in", "w_ffn_out"]
    res = {**big, **small}
    outs = [loss, grad_x[None]]
    for k in range(4):
        outs += [res[nm][k] for nm in order]
    return tuple(outs)
```

```python
import functools

import numpy as np
import jax
import jax.numpy as jnp
from jax import lax
from jax.experimental import pallas as pl
from jax.experimental.pallas import tpu as pltpu
from jax.experimental.pallas import tpu_sc as plsc

f32 = jnp.float32
bf16 = jnp.bfloat16
HI = lax.Precision.HIGHEST
MESH = pl.DeviceIdType.MESH
AXES = ("x", "y", "c")
NDEV = 8

CHUNK = 64
HEAD = 128
SUB = 16
HGRN_BWD_HEADS = 4
ATTN_CHUNKS = 4
N_PAST = 8
BAND = N_PAST + 1
PAD = N_PAST * CHUNK
REL_FUTURE = CHUNK - 1
REL_PAST = 2 * CHUNK - 1
N_REL = REL_FUTURE + REL_PAST + 1
N_REL_PAD = 256
EPS = 1e-6
NEG = -1e30

ADAM_LR = 0.001
ADAM_B1 = 0.9
ADAM_B2 = 0.999
ADAM_EPS = 1e-08
ADAM_WD = 0.01
ADAM_STEP = 10

VMEM_LIMIT = 56 * 1024 * 1024


def _params(sem=None):
    return pltpu.CompilerParams(dimension_semantics=sem, vmem_limit_bytes=VMEM_LIMIT)


def _tile(n, pref, unit=128):
    if n <= pref:
        return n
    t = (pref // unit) * unit
    while t >= unit:
        if n % t == 0:
            return t
        t -= unit
    return n


_sigmoid = jax.nn.sigmoid


def _mm(a, b, *, mode, name, b_blocked=False, out_blocked=False, out_dtype=f32, tm=1024, tn=1024, tk=2048):
    if mode == "tn":
        K, M = a.shape
    else:
        M, K = a.shape
    if b_blocked:
        nb, mid, cb = b.shape
        if mode == "nn":
            assert mid == K
            N, tn = nb * cb, cb
        else:
            assert mode == "nt" and nb * cb == K
            N, tk = mid, cb
    else:
        N = b.shape[1] if mode in ("nn", "tn") else b.shape[0]
    tm = _tile(M, tm)
    tn = tn if (b_blocked and mode == "nn") or out_blocked else _tile(N, tn)
    tk = tk if b_blocked and mode == "nt" else _tile(K, tk)
    assert M % tm == 0 and N % tn == 0 and K % tk == 0
    nk = K // tk
    grid = (M // tm, N // tn, nk)
    if mode == "tn":
        a_spec = pl.BlockSpec((tk, tm), lambda i, j, k: (k, i))
    else:
        a_spec = pl.BlockSpec((tm, tk), lambda i, j, k: (i, k))
    if mode == "nn":
        b_spec = pl.BlockSpec((None, tk, cb), lambda i, j, k: (j, k, 0)) if b_blocked else pl.BlockSpec((tk, tn), lambda i, j, k: (k, j))
    elif mode == "nt":
        b_spec = pl.BlockSpec((None, tn, cb), lambda i, j, k: (k, j, 0)) if b_blocked else pl.BlockSpec((tn, tk), lambda i, j, k: (j, k))
    else:
        b_spec = pl.BlockSpec((tk, tn), lambda i, j, k: (k, j))
    if out_blocked:
        out_shape = jax.ShapeDtypeStruct((N // tn, M, tn), out_dtype)
        o_spec = pl.BlockSpec((None, tm, tn), lambda i, j, k: (j, i, 0))
    else:
        out_shape = jax.ShapeDtypeStruct((M, N), out_dtype)
        o_spec = pl.BlockSpec((tm, tn), lambda i, j, k: (i, j))
    dims = {"nn": ((1,), (0,)), "nt": ((1,), (1,)), "tn": ((0,), (0,))}[mode]

    def body(a_ref, b_ref, o_ref, *acc):
        p = lax.dot_general(a_ref[...].astype(bf16), b_ref[...].astype(bf16), (dims, ((), ())), preferred_element_type=f32)
        if nk == 1:
            o_ref[...] = p.astype(out_dtype)
        else:
            acc_ref = acc[0]
            k = pl.program_id(2)

            @pl.when(k == 0)
            def _():
                acc_ref[...] = p

            @pl.when(k > 0)
            def _():
                acc_ref[...] += p

            @pl.when(k == nk - 1)
            def _():
                o_ref[...] = acc_ref[...].astype(out_dtype)

    return pl.pallas_call(
        body, name=name, grid=grid, in_specs=[a_spec, b_spec], out_specs=o_spec, out_shape=out_shape,
        scratch_shapes=[pltpu.VMEM((tm, tn), f32)] if nk > 1 else [],
        compiler_params=_params(("parallel", "parallel", "arbitrary")),
    )(a, b)


def _rms_fwd(x, res, gain, name):
    T, D = x.shape
    tr = _tile(T, 256, 8)
    row = pl.BlockSpec((tr, D), lambda i: (i, 0))
    vec = pl.BlockSpec((1, D), lambda i: (0, 0))

    def body(*refs):
        if res is None:
            x_ref, g_ref, h_ref = refs
            xs = x_ref[...]
        else:
            x_ref, r_ref, g_ref, xs_ref, h_ref = refs
            xs = x_ref[...] + r_ref[...]
            xs_ref[...] = xs
        r = lax.rsqrt(jnp.mean(xs * xs, axis=-1, keepdims=True) + EPS)
        h_ref[...] = (xs * r * g_ref[...]).astype(bf16)

    h_shape = jax.ShapeDtypeStruct((T, D), bf16)
    if res is None:
        return pl.pallas_call(body, name=name, grid=(T // tr,), in_specs=[row, vec], out_specs=row, out_shape=h_shape,
                              compiler_params=_params(("parallel",)))(x, gain)
    return pl.pallas_call(body, name=name, grid=(T // tr,), in_specs=[row, row, vec], out_specs=(row, row),
                          out_shape=(jax.ShapeDtypeStruct((T, D), f32), h_shape), compiler_params=_params(("parallel",)))(x, res, gain)


def _rms_bwd(xs, gain, dh, extra, name):
    T, D = xs.shape
    tr = _tile(T, 256, 8)
    row = pl.BlockSpec((tr, D), lambda i: (i, 0))
    vec = pl.BlockSpec((1, D), lambda i: (0, 0))

    def body(x_ref, g_ref, dh_ref, e_ref, dx_ref, dg_ref):
        x = x_ref[...]
        r = lax.rsqrt(jnp.mean(x * x, axis=-1, keepdims=True) + EPS)
        xhat = x * r
        dh_v = dh_ref[...]
        gd = dh_v * g_ref[...]
        dx_ref[...] = e_ref[...] + r * (gd - xhat * jnp.mean(gd * xhat, axis=-1, keepdims=True))
        part = jnp.sum(dh_v * xhat, axis=0, keepdims=True)

        @pl.when(pl.program_id(0) == 0)
        def _():
            dg_ref[...] = part

        @pl.when(pl.program_id(0) > 0)
        def _():
            dg_ref[...] += part

    return pl.pallas_call(body, name=name, grid=(T // tr,), in_specs=[row, vec, row, row], out_specs=(row, vec),
                          out_shape=(jax.ShapeDtypeStruct((T, D), f32), jax.ShapeDtypeStruct((1, D), f32)),
                          compiler_params=_params(("arbitrary",)))(xs, gain, dh, extra)


def _merge_fwd(pa, pb, proj, b_gate, off, name):
    T, D = pa.shape
    tr, tc = _tile(T, 512, 8), _tile(D, 512)
    oa, ob, nb = off // tc, (off + D) // tc, D // tc
    blk = pl.BlockSpec((tr, tc), lambda i, j: (i, j))

    def body(pa_ref, pb_ref, ga_ref, gb_ref, ba_ref, bb_ref, o_ref):
        ga = _sigmoid(ga_ref[...] + ba_ref[...])
        gb = _sigmoid(gb_ref[...] + bb_ref[...])
        o_ref[...] = (ga * pa_ref[...] + gb * pb_ref[...]).astype(bf16)

    return pl.pallas_call(
        body, name=name, grid=(T // tr, nb),
        in_specs=[blk, blk, pl.BlockSpec((tr, tc), lambda i, j: (i, oa + j)), pl.BlockSpec((tr, tc), lambda i, j: (i, ob + j)),
                  pl.BlockSpec((1, tc), lambda i, j: (0, j)), pl.BlockSpec((1, tc), lambda i, j: (0, nb + j))],
        out_specs=blk, out_shape=jax.ShapeDtypeStruct((T, D), bf16), compiler_params=_params(("parallel", "parallel")),
    )(pa, pb, proj, proj, b_gate, b_gate)


def _branch_bwd(dm, p, proj, b_gate, off, boff, name):
    T, D = p.shape
    tr, tc = _tile(T, 512, 8), _tile(D, 512)
    og, obias = off // tc, boff // tc
    blk = pl.BlockSpec((tr, tc), lambda j, i: (i, j))
    vec = pl.BlockSpec((1, tc), lambda j, i: (0, j))

    def body(dm_ref, p_ref, gl_ref, b_ref, dp_ref, dgl_ref, db_ref):
        g = _sigmoid(gl_ref[...] + b_ref[...])
        dm_v = dm_ref[...]
        dp_ref[...] = (dm_v * g).astype(bf16)
        dgl = dm_v * p_ref[...] * g * (1.0 - g)
        dgl_ref[...] = dgl.astype(bf16)
        part = jnp.sum(dgl, axis=0, keepdims=True)

        @pl.when(pl.program_id(1) == 0)
        def _():
            db_ref[...] = part

        @pl.when(pl.program_id(1) > 0)
        def _():
            db_ref[...] += part

    return pl.pallas_call(
        body, name=name, grid=(D // tc, T // tr),
        in_specs=[blk, blk, pl.BlockSpec((tr, tc), lambda j, i: (i, og + j)), pl.BlockSpec((1, tc), lambda j, i: (0, obias + j))],
        out_specs=(blk, blk, vec),
        out_shape=(jax.ShapeDtypeStruct((T, D), bf16), jax.ShapeDtypeStruct((T, D), bf16), jax.ShapeDtypeStruct((1, D), f32)),
        compiler_params=_params(("parallel", "arbitrary")),
    )(dm, p, proj, b_gate)


def _swiglu_fwd(gu, name):
    T, F2 = gu.shape
    F = F2 // 2
    tr, tc = _tile(T, 512, 8), _tile(F, 512)
    nb = F // tc

    def body(g_ref, u_ref, o_ref):
        g = g_ref[...]
        o_ref[...] = (g * _sigmoid(g) * u_ref[...]).astype(bf16)

    return pl.pallas_call(
        body, name=name, grid=(T // tr, nb),
        in_specs=[pl.BlockSpec((tr, tc), lambda i, j: (i, j)), pl.BlockSpec((tr, tc), lambda i, j: (i, nb + j))],
        out_specs=pl.BlockSpec((tr, tc), lambda i, j: (i, j)), out_shape=jax.ShapeDtypeStruct((T, F), bf16),
        compiler_params=_params(("parallel", "parallel")),
    )(gu, gu)


def _swiglu_bwd(gu, dact, name):
    T, F2 = gu.shape
    F = F2 // 2
    tr, tc = _tile(T, 512, 8), _tile(F, 512)
    nb = F // tc
    blk = pl.BlockSpec((tr, tc), lambda i, j: (i, j))

    def body(g_ref, u_ref, d_ref, dg_ref, du_ref):
        g = g_ref[...]
        s = _sigmoid(g)
        d = d_ref[...]
        dg_ref[...] = (d * u_ref[...] * s * (1.0 + g * (1.0 - s))).astype(bf16)
        du_ref[...] = (d * g * s).astype(bf16)

    return pl.pallas_call(
        body, name=name, grid=(T // tr, nb),
        in_specs=[blk, pl.BlockSpec((tr, tc), lambda i, j: (i, nb + j)), blk], out_specs=(blk, blk),
        out_shape=(jax.ShapeDtypeStruct((T, F), bf16), jax.ShapeDtypeStruct((T, F), bf16)),
        compiler_params=_params(("parallel", "parallel")),
    )(gu, gu, dact)


def _loss_head(x1, fo, target, name):
    T, D = x1.shape
    tr = _tile(T, 256, 8)
    row = pl.BlockSpec((tr, D), lambda i: (i, 0))
    acc = pl.BlockSpec((8, 128), lambda i: (0, 0))

    def body(x_ref, f_ref, t_ref, dy_ref, l_ref):
        d = x_ref[...] + f_ref[...] - t_ref[...]
        dy_ref[...] = d * (1.0 / D)
        part = jnp.sum(jnp.sum(d * d, axis=1, keepdims=True), axis=0, keepdims=True)

        @pl.when(pl.program_id(0) == 0)
        def _():
            l_ref[...] = jnp.zeros((8, 128), f32)

        l_ref[...] += part

    return pl.pallas_call(body, name=name, grid=(T // tr,), in_specs=[row, row, row], out_specs=(row, acc),
                          out_shape=(jax.ShapeDtypeStruct((T, D), f32), jax.ShapeDtypeStruct((8, 128), f32)),
                          compiler_params=_params(("arbitrary",)))(x1, fo, target)


_DIMS = {"nn": ((1,), (0,)), "nt": ((1,), (1,)), "tn": ((0,), (0,))}
_MODE = {v: k for k, v in _DIMS.items()}


def _dot_bf16(a, b, mode):
    return lax.dot_general(a.astype(bf16), b.astype(bf16), (_DIMS[mode], ((), ())), preferred_element_type=f32)


@functools.partial(jax.custom_vjp, nondiff_argnums=(2,))
def _dotm(a, b, mode):
    return _dot_bf16(a, b, mode)


def _dotm_fwd(a, b, mode):
    return _dot_bf16(a, b, mode), (a, b)


def _dotm_bwd(mode, res, g):
    a, b = res
    if mode == "nn":
        return _dot_bf16(g, b, "nt"), _dot_bf16(a, g, "tn")
    if mode == "nt":
        return _dot_bf16(g, b, "nn"), _dot_bf16(g, a, "tn")
    return _dot_bf16(b, g, "nt"), _dot_bf16(a, g, "nn")


_dotm.defvjp(_dotm_fwd, _dotm_bwd)


def _dotb(a, b, dims):
    return _dotm(a, b, _MODE[dims])


def _triangle_sum(v, lower):
    row = lax.broadcasted_iota(jnp.int32, (CHUNK, CHUNK), 0)
    col = lax.broadcasted_iota(jnp.int32, (CHUNK, CHUNK), 1)
    tri = ((col <= row) if lower else (col >= row)).astype(bf16)

    def top(t):
        return lax.bitcast_convert_type(lax.bitcast_convert_type(t, jnp.uint32) & jnp.uint32(0xFFFF0000), f32)

    hi = top(v)
    mid = top(v - hi)
    low = (v - hi) - mid
    hi, mid, low = hi.astype(bf16), mid.astype(bf16), low.astype(bf16)
    dn = (((1,), (0,)), ((), ()))
    return (lax.dot_general(tri, hi, dn, preferred_element_type=f32) + lax.dot_general(tri, mid, dn, preferred_element_type=f32)
            + lax.dot_general(tri, low, dn, preferred_element_type=f32))


@jax.custom_vjp
def _cumsum_rows(v):
    return _triangle_sum(v, True)


_cumsum_rows.defvjp(lambda v: (_triangle_sum(v, True), None), lambda _, g: (_triangle_sum(g, False),))


def _hgrn_chunk(q, fl, iv, g, logits, gain, st):
    lb = jax.nn.softmax(logits, axis=0)[0:1]
    f = lb + (1.0 - lb) * _sigmoid(fl)
    lf = jnp.log(f)
    kk = 1.0 - f
    qs = q * _sigmoid(q)
    b = _cumsum_rows(lf)
    b_last = jnp.sum(lf, axis=0, keepdims=True)
    o = _dotb(qs * jnp.exp(b), st, ((1,), (1,)))
    r3 = lax.broadcasted_iota(jnp.int32, (SUB, SUB, HEAD), 0)
    c3 = lax.broadcasted_iota(jnp.int32, (SUB, SUB, HEAD), 1)
    parts = []
    for i in range(CHUNK // SUB):
        lo, hi = i * SUB, (i + 1) * SUB
        bi = b[lo:hi]
        dec = jnp.exp(jnp.where(c3 <= r3, bi[:, None, :] - bi[None, :, :], -jnp.inf))
        s = jnp.sum(qs[lo:hi][:, None, :] * kk[lo:hi][None, :, :] * dec, axis=-1)
        if i > 0:
            anchor = jnp.max(bi, axis=0, keepdims=True)
            qa = qs[lo:hi] * jnp.exp(bi - anchor)
            kd = kk[:lo] * jnp.exp(anchor - b[:lo])
            s = jnp.concatenate([_dotb(qa, kd, ((1,), (1,))), s], axis=1)
        parts.append(_dotb(s, iv[:hi], ((1,), (0,))))
    o = o + jnp.concatenate(parts, axis=0)
    st_new = st * jnp.exp(b_last) + _dotb(iv, kk * jnp.exp(b_last - b), ((0,), (0,)))
    o = o * lax.rsqrt(jnp.mean(o * o, axis=-1, keepdims=True) + EPS)
    o = o * gain * (g * _sigmoid(g))
    return o, st_new


def _group(n, pref):
    while n % pref:
        pref //= 2
    return pref


def _hgrn_fwd(proj, logits, gain, n_heads, name):
    T = proj.shape[0]
    nc = T // CHUNK
    H = n_heads
    HB = _group(H, 4)
    W = HB * HEAD

    def col(k):
        return pl.BlockSpec((CHUNK, W), lambda h, c: (c, k * (H // HB) + h))

    def body(q_ref, f_ref, i_ref, g_ref, l_ref, ga_ref, y_ref, s_ref, st):
        @pl.when(pl.program_id(1) == 0)
        def _():
            st[...] = jnp.zeros((HB, HEAD, HEAD), f32)

        for j in range(HB):
            cols = slice(j * HEAD, (j + 1) * HEAD)
            s_ref[j] = st[j]
            o, st_new = _hgrn_chunk(q_ref[:, cols], f_ref[:, cols], i_ref[:, cols], g_ref[:, cols], l_ref[:, cols], ga_ref[:, cols], st[j])
            y_ref[:, cols] = o.astype(bf16)
            st[j] = st_new

    return pl.pallas_call(
        body, name=name, grid=(H // HB, nc),
        in_specs=[col(0), col(1), col(2), col(3), pl.BlockSpec((2, W), lambda h, c: (0, h)), pl.BlockSpec((1, W), lambda h, c: (0, h))],
        out_specs=(pl.BlockSpec((CHUNK, W), lambda h, c: (c, h)), pl.BlockSpec((HB, None, HEAD, HEAD), lambda h, c: (h, c, 0, 0))),
        out_shape=(jax.ShapeDtypeStruct((T, H * HEAD), bf16), jax.ShapeDtypeStruct((H, nc, HEAD, HEAD), f32)),
        scratch_shapes=[pltpu.VMEM((HB, HEAD, HEAD), f32)],
        compiler_params=_params(("parallel", "arbitrary")),
    )(proj, proj, proj, proj, logits, gain)


def _hgrn_bwd(proj, logits, gain, states, dy, n_heads, name):
    T = proj.shape[0]
    nc = T // CHUNK
    H = n_heads
    HB = _group(H, HGRN_BWD_HEADS)
    W = HB * HEAD

    def col(k):
        return pl.BlockSpec((CHUNK, W), lambda h, c: (nc - 1 - c, k * (H // HB) + h))

    out_blk = pl.BlockSpec((CHUNK, W), lambda h, c: (nc - 1 - c, h))

    def body(q_ref, f_ref, i_ref, g_ref, l_ref, ga_ref, s_ref, dy_ref, dq_ref, df_ref, di_ref, dg_ref, dl_ref, dga_ref, dst):
        first = pl.program_id(1) == 0

        @pl.when(first)
        def _():
            dst[...] = jnp.zeros((HB, HEAD, HEAD), f32)
            dl_ref[...] = jnp.zeros((2, W), f32)
            dga_ref[...] = jnp.zeros((1, W), f32)

        for j in range(HB):
            cols = slice(j * HEAD, (j + 1) * HEAD)
            _, vjp = jax.vjp(_hgrn_chunk, q_ref[:, cols], f_ref[:, cols], i_ref[:, cols], g_ref[:, cols], l_ref[:, cols], ga_ref[:, cols], s_ref[j])
            dq, df, di, dg, dl, dga, ds = vjp((dy_ref[:, cols], dst[j]))
            dq_ref[:, cols] = dq.astype(bf16)
            df_ref[:, cols] = df.astype(bf16)
            di_ref[:, cols] = di.astype(bf16)
            dg_ref[:, cols] = dg.astype(bf16)
            dst[j] = ds
            dl_ref[:, cols] += dl
            dga_ref[:, cols] += dga

    act = jax.ShapeDtypeStruct((T, H * HEAD), bf16)
    return pl.pallas_call(
        body, name=name, grid=(H // HB, nc),
        in_specs=[col(0), col(1), col(2), col(3), pl.BlockSpec((2, W), lambda h, c: (0, h)), pl.BlockSpec((1, W), lambda h, c: (0, h)),
                  pl.BlockSpec((HB, None, HEAD, HEAD), lambda h, c: (h, nc - 1 - c, 0, 0)), out_blk],
        out_specs=(out_blk, out_blk, out_blk, out_blk, pl.BlockSpec((2, W), lambda h, c: (0, h)), pl.BlockSpec((1, W), lambda h, c: (0, h))),
        out_shape=(act, act, act, act, jax.ShapeDtypeStruct((2, H * HEAD), f32), jax.ShapeDtypeStruct((1, H * HEAD), f32)),
        scratch_shapes=[pltpu.VMEM((HB, HEAD, HEAD), f32)],
        compiler_params=_params(("parallel", "arbitrary")),
    )(proj, proj, proj, proj, logits, gain, states, dy)


def _rel_index():
    t = np.arange(CHUNK)[:, None]
    sp = np.arange(BAND * CHUNK)[None, :]
    dist = (N_PAST - sp // CHUNK) * CHUNK + t - sp % CHUNK
    return (np.clip(dist, -REL_FUTURE, REL_PAST) + REL_FUTURE).reshape(1, -1).astype(np.int32)


def _bias_table(rel_bias_pad, idx, name):
    H = rel_bias_pad.shape[0]
    n = idx.shape[1]
    tc = _tile(n, 4096)

    def body(rb_ref, idx_ref, o_ref):
        onehot = (lax.broadcasted_iota(jnp.int32, (N_REL_PAD, tc), 0) == idx_ref[...]).astype(f32)
        o_ref[...] = lax.dot_general(rb_ref[...], onehot, (((1,), (0,)), ((), ())), precision=HI, preferred_element_type=f32)

    return pl.pallas_call(
        body, name=name, grid=(n // tc,),
        in_specs=[pl.BlockSpec((H, N_REL_PAD), lambda j: (0, 0)), pl.BlockSpec((1, tc), lambda j: (0, j))],
        out_specs=pl.BlockSpec((H, tc), lambda j: (0, j)), out_shape=jax.ShapeDtypeStruct((H, n), f32),
        compiler_params=_params(("parallel",)),
    )(rel_bias_pad, idx)


def _bias_table_bwd(dbias, idx, name):
    H, n = dbias.shape
    tc = _tile(n, 4096)

    def body(d_ref, idx_ref, o_ref):
        onehot = (lax.broadcasted_iota(jnp.int32, (N_REL_PAD, tc), 0) == idx_ref[...]).astype(f32)
        part = lax.dot_general(d_ref[...], onehot, (((1,), (1,)), ((), ())), precision=HI, preferred_element_type=f32)

        @pl.when(pl.program_id(0) == 0)
        def _():
            o_ref[...] = part

        @pl.when(pl.program_id(0) > 0)
        def _():
            o_ref[...] += part

    return pl.pallas_call(
        body, name=name, grid=(n // tc,),
        in_specs=[pl.BlockSpec((H, tc), lambda j: (0, j)), pl.BlockSpec((1, tc), lambda j: (0, j))],
        out_specs=pl.BlockSpec((H, N_REL_PAD), lambda j: (0, 0)), out_shape=jax.ShapeDtypeStruct((H, N_REL_PAD), f32),
        compiler_params=_params(("arbitrary",)),
    )(dbias, idx)


def _head_norm(t, gain):
    return t * lax.rsqrt(jnp.mean(t * t, axis=-1, keepdims=True) + EPS) * gain


def _attn_chunk(q, kb, vb, qg, bias, n):
    qh = _head_norm(q, qg)
    s = _dotb(qh, kb, ((1,), (1,))) * (HEAD ** -0.5) + bias
    pos = n * CHUNK - PAD + lax.broadcasted_iota(jnp.int32, (1, BAND * CHUNK), 1)
    s = jnp.where(pos >= 0, s, NEG)
    e = jnp.exp(s - jnp.max(s, axis=-1, keepdims=True))
    p = e / jnp.sum(e, axis=-1, keepdims=True)
    return _dotb(p, vb, ((1,), (0,)))


def _attn_fwd(proj, q_gain, k_gain, bias, off, n_heads, name):
    T = proj.shape[0]
    nc = T // CHUNK
    H = n_heads
    CB = _group(nc, ATTN_CHUNKS)
    o0 = off // HEAD
    full = lambda k: pl.BlockSpec((T, HEAD), lambda h, c: (0, o0 + k * H + h))
    vec = pl.BlockSpec((1, HEAD), lambda h, c: (0, 0))

    def body(q_ref, k_ref, v_ref, qg_ref, kg_ref, b_ref, y_ref, kp, vp):
        c = pl.program_id(1)

        @pl.when(c == 0)
        def _():
            kp[pl.ds(0, PAD), :] = jnp.zeros((PAD, HEAD), f32)
            vp[pl.ds(0, PAD), :] = jnp.zeros((PAD, HEAD), f32)
            kp[pl.ds(PAD, T), :] = _head_norm(k_ref[...], kg_ref[...])
            vp[pl.ds(PAD, T), :] = v_ref[...]

        for j in range(CB):
            n = c * CB + j
            rows = pl.ds(j * CHUNK, CHUNK)
            band = pl.ds(pl.multiple_of(n * CHUNK, CHUNK), BAND * CHUNK)
            y_ref[rows, :] = _attn_chunk(q_ref[rows, :], kp[band, :], vp[band, :], qg_ref[...], b_ref[...], n).astype(bf16)

    return pl.pallas_call(
        body, name=name, grid=(H, nc // CB),
        in_specs=[pl.BlockSpec((CB * CHUNK, HEAD), lambda h, c: (c, o0 + h)), full(1), full(2), vec, vec,
                  pl.BlockSpec((None, CHUNK, BAND * CHUNK), lambda h, c: (h, 0, 0))],
        out_specs=pl.BlockSpec((CB * CHUNK, HEAD), lambda h, c: (c, h)), out_shape=jax.ShapeDtypeStruct((T, H * HEAD), bf16),
        scratch_shapes=[pltpu.VMEM((T + PAD, HEAD), f32), pltpu.VMEM((T + PAD, HEAD), f32)],
        compiler_params=_params(("parallel", "arbitrary")),
    )(proj, proj, proj, q_gain, k_gain, bias)


def _attn_bwd(proj, q_gain, k_gain, bias, dy, off, n_heads, name):
    T = proj.shape[0]
    nc = T // CHUNK
    H = n_heads
    CB = _group(nc, ATTN_CHUNKS)
    o0 = off // HEAD
    full = lambda k: pl.BlockSpec((T, HEAD), lambda h, c: (0, o0 + k * H + h))
    full_out = pl.BlockSpec((T, HEAD), lambda h, c: (0, h))
    vec = pl.BlockSpec((1, HEAD), lambda h, c: (0, 0))
    chunk_out = pl.BlockSpec((CB * CHUNK, HEAD), lambda h, c: (c, h))
    bias_blk = pl.BlockSpec((None, CHUNK, BAND * CHUNK), lambda h, c: (h, 0, 0))

    def body(q_ref, k_ref, v_ref, qg_ref, kg_ref, b_ref, dy_ref, dq_ref, dk_ref, dv_ref, db_ref, dqg_ref, dkg_ref, kp, vp, dkp, dvp):
        h = pl.program_id(0)
        c = pl.program_id(1)

        @pl.when(c == 0)
        def _():
            kp[pl.ds(0, PAD), :] = jnp.zeros((PAD, HEAD), f32)
            vp[pl.ds(0, PAD), :] = jnp.zeros((PAD, HEAD), f32)
            kp[pl.ds(PAD, T), :] = _head_norm(k_ref[...], kg_ref[...])
            vp[pl.ds(PAD, T), :] = v_ref[...]
            dkp[...] = jnp.zeros((T + PAD, HEAD), f32)
            dvp[...] = jnp.zeros((T + PAD, HEAD), f32)
            db_ref[...] = jnp.zeros((CHUNK, BAND * CHUNK), f32)

        @pl.when(jnp.logical_and(h == 0, c == 0))
        def _():
            dqg_ref[...] = jnp.zeros((1, HEAD), f32)
            dkg_ref[...] = jnp.zeros((1, HEAD), f32)

        grads = []
        for j in range(CB):
            n = c * CB + j
            rows = pl.ds(j * CHUNK, CHUNK)
            band = pl.ds(pl.multiple_of(n * CHUNK, CHUNK), BAND * CHUNK)
            _, vjp = jax.vjp(functools.partial(_attn_chunk, n=n), q_ref[rows, :], kp[band, :], vp[band, :], qg_ref[...], b_ref[...])
            grads.append((rows, band) + vjp(dy_ref[rows, :]))
        for rows, band, dq, dkb, dvb, dqg, db in grads:
            dq_ref[rows, :] = dq.astype(bf16)
            dkp[band, :] += dkb
            dvp[band, :] += dvb
            db_ref[...] += db
            dqg_ref[...] += dqg

        @pl.when(c == nc // CB - 1)
        def _():
            _, nvjp = jax.vjp(_head_norm, k_ref[...], kg_ref[...])
            dk, dkg = nvjp(dkp[pl.ds(PAD, T), :])
            dk_ref[...] = dk.astype(bf16)
            dv_ref[...] = dvp[pl.ds(PAD, T), :].astype(bf16)
            dkg_ref[...] += dkg

    act = jax.ShapeDtypeStruct((T, H * HEAD), bf16)
    gvec = jax.ShapeDtypeStruct((1, HEAD), f32)
    pad_buf = pltpu.VMEM((T + PAD, HEAD), f32)
    return pl.pallas_call(
        body, name=name, grid=(H, nc // CB),
        in_specs=[pl.BlockSpec((CB * CHUNK, HEAD), lambda h, c: (c, o0 + h)), full(1), full(2), vec, vec, bias_blk, chunk_out],
        out_specs=(chunk_out, full_out, full_out, bias_blk, vec, vec),
        out_shape=(act, act, act, jax.ShapeDtypeStruct((H, CHUNK, BAND * CHUNK), f32), gvec, gvec),
        scratch_shapes=[pad_buf, pad_buf, pad_buf, pad_buf],
        compiler_params=_params(("arbitrary", "arbitrary")),
    )(proj, proj, proj, q_gain, k_gain, bias, dy)


def _position():
    x, y, c = lax.axis_index("x"), lax.axis_index("y"), lax.axis_index("c")
    return x, y, c, 4 * x + 2 * y + c


def _flip(v, bit):
    return 1 - v if bit else v


def _chips(x, y):
    return [(1 - x, y), (x, 1 - y), (1 - x, 1 - y)]


def _seq_gather(shards, name, collective_id):
    n = len(shards)

    def body(*refs):
        ins, outs = refs[:n], refs[n:2 * n]
        send, recv, loc = refs[2 * n:]
        x, y, c, me = _position()
        sib = (x, y, 1 - c)
        chips = _chips(x, y)
        barrier = pltpu.get_barrier_semaphore()
        for peer in [sib] + [(px, py, c) for px, py in chips]:
            pl.semaphore_signal(barrier, inc=1, device_id=peer, device_id_type=MESH)
        pl.semaphore_wait(barrier, 4)

        def copy(w, k, src, blk, to):
            return pltpu.make_async_remote_copy(src_ref=src, dst_ref=outs[w].at[blk], send_sem=send.at[7 * w + k], recv_sem=recv.at[7 * w + k],
                                                device_id=to, device_id_type=MESH)

        mine = [pltpu.make_async_copy(ins[w], outs[w].at[me], loc.at[w]) for w in range(n)]
        for cp in mine:
            cp.start()
        first = []
        for j, (px, py) in enumerate(chips):
            first += [copy(w, 1 + j, ins[w], me, (px, py, c)) for w in range(n)]
        first += [copy(w, 0, ins[w], me, sib) for w in range(n)]
        for cp in first:
            cp.start()
        passed = []
        for j, (px, py) in enumerate(chips):
            blk = 4 * px + 2 * py + c
            for w in range(n):
                copy(w, 1 + j, ins[w], blk, sib).wait_recv()
                fwd = copy(w, 4 + j, outs[w].at[blk], blk, sib)
                fwd.start()
                passed.append(fwd)
        for w in range(n):
            copy(w, 0, ins[w], 4 * x + 2 * y + (1 - c), sib).wait_recv()
        for j, (px, py) in enumerate(chips):
            for w in range(n):
                copy(w, 4 + j, ins[w], 4 * px + 2 * py + (1 - c), sib).wait_recv()
        for cp in first + passed:
            cp.wait_send()
        for cp in mine:
            cp.wait()

    return pl.kernel(
        body, out_type=tuple(jax.ShapeDtypeStruct((NDEV,) + s.shape, s.dtype) for s in shards),
        mesh=plsc.ScalarSubcoreMesh(axis_name="sequencer", num_cores=1), name=name,
        scratch_types=(pltpu.SemaphoreType.DMA((7 * n,)), pltpu.SemaphoreType.DMA((7 * n,)), pltpu.SemaphoreType.DMA((n,))),
        compiler_params=pltpu.CompilerParams(collective_id=collective_id),
    )(*shards)


def _seq_scatter(grads, name, collective_id, after=()):
    n, na = len(grads), len(after)

    def body(*refs):
        ins, outs = refs[:n], refs[n + na:2 * n + na]
        send, recv, loc = refs[2 * n + na:]
        x, y, c, me = _position()
        peers = [(_flip(x, r & 4), _flip(y, r & 2), _flip(c, r & 1)) for r in range(1, NDEV)]
        barrier = pltpu.get_barrier_semaphore()
        for peer in peers:
            pl.semaphore_signal(barrier, inc=1, device_id=peer, device_id_type=MESH)
        pl.semaphore_wait(barrier, NDEV - 1)
        mine = [pltpu.make_async_copy(ins[w].at[me], outs[w].at[me], loc.at[w]) for w in range(n)]
        for cp in mine:
            cp.start()
        sends, waits = [], []
        for k, (px, py, pc) in enumerate(peers):
            peer = 4 * px + 2 * py + pc
            for w in range(n):
                sends.append(pltpu.make_async_remote_copy(src_ref=ins[w].at[peer], dst_ref=outs[w].at[me], send_sem=send.at[7 * w + k],
                                                          recv_sem=recv.at[7 * w + k], device_id=(px, py, pc), device_id_type=MESH))
                waits.append(pltpu.make_async_remote_copy(src_ref=ins[w].at[peer], dst_ref=outs[w].at[peer], send_sem=send.at[7 * w + k],
                                                          recv_sem=recv.at[7 * w + k], device_id=(px, py, pc), device_id_type=MESH))
        for cp in sends:
            cp.start()
        for cp in waits:
            cp.wait_recv()
        for cp in sends:
            cp.wait_send()
        for cp in mine:
            cp.wait()

    return pl.kernel(
        body, out_type=tuple(jax.ShapeDtypeStruct(g.shape, g.dtype) for g in grads),
        mesh=plsc.ScalarSubcoreMesh(axis_name="sequencer", num_cores=1), name=name,
        scratch_types=(pltpu.SemaphoreType.DMA((7 * n,)), pltpu.SemaphoreType.DMA((7 * n,)), pltpu.SemaphoreType.DMA((n,))),
        compiler_params=pltpu.CompilerParams(collective_id=collective_id),
    )(*grads, *after)


def _small_all_reduce(v, name):
    R, C = v.shape

    def body(v_ref, o_ref, buf, send, recv):
        x, y, c, me = _position()
        buf[me] = v_ref[...]
        sends, waits = [], []
        for r in range(1, NDEV):
            px, py, pc = _flip(x, r & 4), _flip(y, r & 2), _flip(c, r & 1)
            peer = 4 * px + 2 * py + pc
            sends.append(pltpu.make_async_remote_copy(src_ref=v_ref, dst_ref=buf.at[me], send_sem=send.at[r - 1], recv_sem=recv.at[r - 1],
                                                      device_id=(px, py, pc), device_id_type=MESH))
            waits.append(pltpu.make_async_remote_copy(src_ref=v_ref, dst_ref=buf.at[peer], send_sem=send.at[r - 1], recv_sem=recv.at[r - 1],
                                                      device_id=(px, py, pc), device_id_type=MESH))
        for cp in sends:
            cp.start()
        for cp in waits:
            cp.wait_recv()
        for cp in sends:
            cp.wait_send()
        acc = buf[0]
        for i in range(1, NDEV):
            acc = acc + buf[i]
        o_ref[...] = acc

    vm = pl.BlockSpec(memory_space=pltpu.VMEM)
    return pl.pallas_call(
        body, name=name, in_specs=[vm], out_specs=vm, out_shape=jax.ShapeDtypeStruct((R, C), f32),
        scratch_shapes=[pltpu.VMEM((NDEV, R, C), f32), pltpu.SemaphoreType.DMA((7,)), pltpu.SemaphoreType.DMA((7,))],
    )(v)


def _adamw_math(w, g, m, v):
    m = ADAM_B1 * m + (1.0 - ADAM_B1) * g
    v = ADAM_B2 * v + (1.0 - ADAM_B2) * (g * g)
    m_hat = m / (1.0 - ADAM_B1 ** ADAM_STEP)
    v_hat = v / (1.0 - ADAM_B2 ** ADAM_STEP)
    delta = -ADAM_LR * (m_hat / (jnp.sqrt(v_hat) + ADAM_EPS) + ADAM_WD * w)
    return delta, m, v


def _adamw_parts(w, m, v, parts, name, after=()):
    R, C = w.shape
    tr = _tile(R, 128, 16)
    blk = pl.BlockSpec((tr, C), lambda i: (i, 0))

    def body(w_ref, m_ref, v_ref, p_ref, *rest):
        g_ref, d_ref, mo_ref, vo_ref = rest[len(after):]
        g = p_ref[0].astype(f32)
        for i in range(1, NDEV):
            g = g + p_ref[i].astype(f32)
        d, mn, vn = _adamw_math(w_ref[...], g, m_ref[...], v_ref[...])
        g_ref[...] = g
        d_ref[...] = d
        mo_ref[...] = mn
        vo_ref[...] = vn

    shp = jax.ShapeDtypeStruct((R, C), f32)
    return pl.pallas_call(
        body, name=name, grid=(R // tr,),
        in_specs=[blk, blk, blk, pl.BlockSpec((NDEV, tr, C), lambda i: (0, i, 0))] + [pl.BlockSpec(a.shape, lambda i: (0, 0)) for a in after],
        out_specs=(blk, blk, blk, blk), out_shape=(shp, shp, shp, shp), compiler_params=_params(("parallel",)),
    )(w, m, v, parts, *after)


def _adamw_small(w, g, m, v, name):
    def body(w_ref, g_ref, m_ref, v_ref, d_ref, mo_ref, vo_ref):
        d, mn, vn = _adamw_math(w_ref[...], g_ref[...], m_ref[...], v_ref[...])
        d_ref[...] = d
        mo_ref[...] = mn
        vo_ref[...] = vn

    shp = jax.ShapeDtypeStruct(w.shape, f32)
    return pl.pallas_call(body, name=name, out_shape=(shp, shp, shp))(w, g, m, v)


SMALL_COLS = 1024


def _pack(arrs):
    flat = jnp.concatenate([a.reshape(-1) for a in arrs])
    rows = -(-flat.shape[0] // (8 * SMALL_COLS)) * 8
    return jnp.pad(flat, (0, rows * SMALL_COLS - flat.shape[0])).reshape(rows, SMALL_COLS)


def _unpack(packed, like):
    flat = packed.reshape(-1)
    out, pos = [], 0
    for a in like:
        out.append(flat[pos:pos + a.size].reshape(a.shape))
        pos += a.size
    return out


def kernel(x, w_in, b_gate, norm_mix, norm_ffn, hgrn_lb_logits, hgrn_out_gain, q_gain, k_gain, rel_bias, w_proj_a, w_proj_b, w_out, w_ffn_in, w_ffn_out, loss_target, m_w_in, m_b_gate, m_norm_mix, m_norm_ffn, m_hgrn_lb_logits, m_hgrn_out_gain, m_q_gain, m_k_gain, m_rel_bias, m_w_proj_a, m_w_proj_b, m_w_out, m_w_ffn_in, m_w_ffn_out, v_w_in, v_b_gate, v_norm_mix, v_norm_ffn, v_hgrn_lb_logits, v_hgrn_out_gain, v_q_gain, v_k_gain, v_rel_bias, v_w_proj_a, v_w_proj_b, v_w_out, v_w_ffn_in, v_w_ffn_out):
    xs = x[0]
    target = loss_target[0]
    T, D = xs.shape
    d_a = hgrn_out_gain.shape[-1]
    H = d_a // HEAD
    d_b = d_a
    off_b = 4 * d_a
    off_g = off_b + 3 * d_b
    assert rel_bias.shape[1] == H and T % CHUNK == 0 and T // CHUNK > N_PAST

    big_w = [w_in[0], w_proj_a[0], w_proj_b[0], w_out[0], w_ffn_in[0], w_ffn_out[0]]
    big_m = [m_w_in[0], m_w_proj_a[0], m_w_proj_b[0], m_w_out[0], m_w_ffn_in[0], m_w_ffn_out[0]]
    big_v = [v_w_in[0], v_w_proj_a[0], v_w_proj_b[0], v_w_out[0], v_w_ffn_in[0], v_w_ffn_out[0]]

    sh = [w.astype(bf16) for w in big_w]
    (g_in,) = _seq_gather(sh[0:1], "gather_a", 1)
    g_pa, g_pb, g_out = _seq_gather(sh[1:4], "gather_b", 2)
    g_fin, g_fout = _seq_gather(sh[4:6], "gather_c", 3)

    h = _rms_fwd(xs, None, norm_mix, "rms_mix")
    proj = _mm(h, g_in, mode="nn", b_blocked=True, name="mm_proj")
    y_a, states = _hgrn_fwd(proj, hgrn_lb_logits, hgrn_out_gain, H, "hgrn_fwd")
    idx = jnp.asarray(_rel_index())
    rb_pad = jnp.pad(rel_bias[0], ((0, 0), (0, N_REL_PAD - N_REL)))
    bias = _bias_table(rb_pad, idx, "bias_table").reshape(H, CHUNK, BAND * CHUNK)
    y_b = _attn_fwd(proj, q_gain, k_gain, bias, off_b, H, "attn_fwd")
    wg_out = g_out.reshape(-1, g_out.shape[-1])
    wg_fout = g_fout.reshape(-1, g_fout.shape[-1])
    pa = _mm(y_a, g_pa, mode="nn", b_blocked=True, tm=2048, name="mm_proj_a")
    pb = _mm(y_b, g_pb, mode="nn", b_blocked=True, tm=2048, name="mm_proj_b")
    merged = _merge_fwd(pa, pb, proj, b_gate, off_g, "merge_fwd")
    mo = _mm(merged, wg_out, mode="nn", name="mm_out")
    x1, h2 = _rms_fwd(xs, mo, norm_ffn, "rms_ffn")
    gu = _mm(h2, g_fin, mode="nn", b_blocked=True, name="mm_ffn_in")
    act = _swiglu_fwd(gu, "swiglu_fwd")
    fo = _mm(act, wg_fout, mode="nn", tk=2816, name="mm_ffn_out")
    dy, loss_acc = _loss_head(x1, fo, target, "loss_head")
    loss_part = loss_acc[0:1, 0:1] * (0.5 / D)

    dact = _mm(dy, wg_fout, mode="nt", tn=1408, name="mm_d_act")
    gw_fout = _mm(act, dy, mode="tn", out_dtype=bf16, tm=1408, name="mm_gw_ffn_out")
    dgate, dup = _swiglu_bwd(gu, dact, "swiglu_bwd")
    dgu = jnp.concatenate([dgate, dup], axis=1)
    gw_fin = _mm(h2, dgu, mode="tn", out_blocked=True, out_dtype=bf16, tn=g_fin.shape[-1], name="mm_gw_ffn_in")
    dh2 = _mm(dgu, g_fin, mode="nt", b_blocked=True, name="mm_d_h2")
    dx1, g_norm_ffn = _rms_bwd(x1, norm_ffn, dh2, dy, "rms_ffn_bwd")

    dmerged = _mm(dx1, wg_out, mode="nt", name="mm_d_merged")
    gw_out = _mm(merged, dx1, mode="tn", out_dtype=bf16, name="mm_gw_out")
    dpa, dgl_a, gb_a = _branch_bwd(dmerged, pa, proj, b_gate, off_g, 0, "branch_a_bwd")
    dpb, dgl_b, gb_b = _branch_bwd(dmerged, pb, proj, b_gate, off_g + D, D, "branch_b_bwd")
    dy_a = _mm(dpa, g_pa, mode="nt", b_blocked=True, tm=2048, name="mm_d_ya")
    dy_b = _mm(dpb, g_pb, mode="nt", b_blocked=True, tm=2048, name="mm_d_yb")
    gw_pa = _mm(y_a, dpa, mode="tn", out_blocked=True, out_dtype=bf16, tn=g_pa.shape[-1], name="mm_gw_proj_a")
    gw_pb = _mm(y_b, dpb, mode="tn", out_blocked=True, out_dtype=bf16, tn=g_pb.shape[-1], name="mm_gw_proj_b")

    dq_a, df_a, di_a, dg_a, g_logits, g_gain = _hgrn_bwd(proj, hgrn_lb_logits, hgrn_out_gain, states, dy_a, H, "hgrn_bwd")
    dq_b, dk_b, dv_b, dbias, g_qg, g_kg = _attn_bwd(proj, q_gain, k_gain, bias, dy_b, off_b, H, "attn_bwd")
    g_rel = _bias_table_bwd(dbias.reshape(H, -1), idx, "bias_table_bwd")[:, :N_REL]
    dproj = jnp.concatenate([dq_a, df_a, di_a, dg_a, dq_b, dk_b, dv_b, dgl_a, dgl_b], axis=1)
    gw_in = _mm(h, dproj, mode="tn", out_blocked=True, out_dtype=bf16, tn=g_in.shape[-1], name="mm_gw_in")
    dh = _mm(dproj, g_in, mode="nt", b_blocked=True, name="mm_d_h")
    grad_x, g_norm_mix = _rms_bwd(xs, norm_mix, dh, dx1, "rms_mix_bwd")

    p_fout, p_fin = _seq_scatter([gw_fout.reshape(NDEV, -1, D), gw_fin], "scatter_a", 4, after=(g_norm_ffn,))
    p_out, p_pa, p_pb = _seq_scatter([gw_out.reshape(NDEV, -1, D), gw_pa, gw_pb], "scatter_b", 5, after=(g_gain,))
    (p_in,) = _seq_scatter([gw_in], "scatter_c", 6)
    parts = [p_in, p_pa, p_pb, p_out, p_fin, p_fout]
    names = ["w_in", "w_proj_a", "w_proj_b", "w_out", "w_ffn_in", "w_ffn_out"]
    big = {}
    for nm, w, m, v, p in zip(names, big_w, big_m, big_v, parts):
        big[nm] = [o[None] for o in _adamw_parts(w, m, v, p, "adamw_" + nm, after=() if nm == "w_in" else (g_norm_mix,))]

    small_names = ["b_gate", "norm_mix", "norm_ffn", "hgrn_lb_logits", "hgrn_out_gain", "q_gain", "k_gain", "rel_bias"]
    small_w = [b_gate, norm_mix, norm_ffn, hgrn_lb_logits, hgrn_out_gain, q_gain, k_gain, rel_bias]
    small_m = [m_b_gate, m_norm_mix, m_norm_ffn, m_hgrn_lb_logits, m_hgrn_out_gain, m_q_gain, m_k_gain, m_rel_bias]
    small_v = [v_b_gate, v_norm_mix, v_norm_ffn, v_hgrn_lb_logits, v_hgrn_out_gain, v_q_gain, v_k_gain, v_rel_bias]
    small_g = [jnp.concatenate([gb_a, gb_b], axis=1), g_norm_mix, g_norm_ffn, g_logits, g_gain, g_qg, g_kg, g_rel[None], loss_part]
    g_sum = _small_all_reduce(_pack(small_g), "reduce_small")
    loss = _unpack(g_sum, small_g)[-1].reshape(())
    d_s, m_s, v_s = _adamw_small(_pack(small_w), g_sum, _pack(small_m), _pack(small_v), "adamw_small")
    small = {}
    for nm, g, d, m, v in zip(small_names, _unpack(g_sum, small_w), _unpack(d_s, small_w), _unpack(m_s, small_w), _unpack(v_s, small_w)):
        small[nm] = [g, d, m, v]

    order = ["w_in", "b_gate", "norm_mix", "norm_ffn", "hgrn_lb_logits", "hgrn_out_gain", "q_gain", "k_gain", "rel_bias",
             "w_proj_a", "w_proj_b", "w_out", "w_ffn_in", "w_ffn_out"]
    res = {**big, **small}
    outs = [loss, grad_x[None]]
    for k in range(4):
        outs += [res[nm][k] for nm in order]
    return tuple(outs)
```

```python
import functools

import numpy as np
import jax
import jax.numpy as jnp
from jax import lax
from jax.experimental import pallas as pl
from jax.experimental.pallas import tpu as pltpu
from jax.experimental.pallas import tpu_sc as plsc

f32 = jnp.float32
bf16 = jnp.bfloat16
HI = lax.Precision.HIGHEST
MESH = pl.DeviceIdType.MESH
AXES = ("x", "y", "c")
NDEV = 8

CHUNK = 64
HEAD = 128
SUB = 8
HGRN_BWD_HEADS = 8
ATTN_CHUNKS = 8
N_PAST = 8
BAND = N_PAST + 1
PAD = N_PAST * CHUNK
REL_FUTURE = CHUNK - 1
REL_PAST = 2 * CHUNK - 1
N_REL = REL_FUTURE + REL_PAST + 1
N_REL_PAD = 256
EPS = 1e-6
NEG = -1e30

ADAM_LR = 0.001
ADAM_B1 = 0.9
ADAM_B2 = 0.999
ADAM_EPS = 1e-08
ADAM_WD = 0.01
ADAM_STEP = 10

VMEM_LIMIT = 56 * 1024 * 1024


def _params(sem=None):
    return pltpu.CompilerParams(dimension_semantics=sem, vmem_limit_bytes=VMEM_LIMIT)


def _tile(n, pref, unit=128):
    if n <= pref:
        return n
    t = (pref // unit) * unit
    while t >= unit:
        if n % t == 0:
            return t
        t -= unit
    return n


_sigmoid = jax.nn.sigmoid


def _mm(a, b, *, mode, name, b_blocked=False, out_blocked=False, out_dtype=f32, tm=1024, tn=1024, tk=2048):
    if mode == "tn":
        K, M = a.shape
    else:
        M, K = a.shape
    if b_blocked:
        nb, mid, cb = b.shape
        if mode == "nn":
            assert mid == K
            N, tn = nb * cb, cb
        else:
            assert mode == "nt" and nb * cb == K
            N, tk = mid, cb
    else:
        N = b.shape[1] if mode in ("nn", "tn") else b.shape[0]
    tm = _tile(M, tm)
    tn = tn if (b_blocked and mode == "nn") or out_blocked else _tile(N, tn)
    tk = tk if b_blocked and mode == "nt" else _tile(K, tk)
    assert M % tm == 0 and N % tn == 0 and K % tk == 0
    nk = K // tk
    grid = (M // tm, N // tn, nk)
    if mode == "tn":
        a_spec = pl.BlockSpec((tk, tm), lambda i, j, k: (k, i))
    else:
        a_spec = pl.BlockSpec((tm, tk), lambda i, j, k: (i, k))
    if mode == "nn":
        b_spec = pl.BlockSpec((None, tk, cb), lambda i, j, k: (j, k, 0)) if b_blocked else pl.BlockSpec((tk, tn), lambda i, j, k: (k, j))
    elif mode == "nt":
        b_spec = pl.BlockSpec((None, tn, cb), lambda i, j, k: (k, j, 0)) if b_blocked else pl.BlockSpec((tn, tk), lambda i, j, k: (j, k))
    else:
        b_spec = pl.BlockSpec((tk, tn), lambda i, j, k: (k, j))
    if out_blocked:
        out_shape = jax.ShapeDtypeStruct((N // tn, M, tn), out_dtype)
        o_spec = pl.BlockSpec((None, tm, tn), lambda i, j, k: (j, i, 0))
    else:
        out_shape = jax.ShapeDtypeStruct((M, N), out_dtype)
        o_spec = pl.BlockSpec((tm, tn), lambda i, j, k: (i, j))
    dims = {"nn": ((1,), (0,)), "nt": ((1,), (1,)), "tn": ((0,), (0,))}[mode]

    def body(a_ref, b_ref, o_ref, *acc):
        p = lax.dot_general(a_ref[...].astype(bf16), b_ref[...].astype(bf16), (dims, ((), ())), preferred_element_type=f32)
        if nk == 1:
            o_ref[...] = p.astype(out_dtype)
        else:
            acc_ref = acc[0]
            k = pl.program_id(2)

            @pl.when(k == 0)
            def _():
                acc_ref[...] = p

            @pl.when(k > 0)
            def _():
                acc_ref[...] += p

            @pl.when(k == nk - 1)
            def _():
                o_ref[...] = acc_ref[...].astype(out_dtype)

    return pl.pallas_call(
        body, name=name, grid=grid, in_specs=[a_spec, b_spec], out_specs=o_spec, out_shape=out_shape,
        scratch_shapes=[pltpu.VMEM((tm, tn), f32)] if nk > 1 else [],
        compiler_params=_params(("parallel", "parallel", "arbitrary")),
    )(a, b)


def _rms_fwd(x, res, gain, name):
    T, D = x.shape
    tr = _tile(T, 256, 8)
    row = pl.BlockSpec((tr, D), lambda i: (i, 0))
    vec = pl.BlockSpec((1, D), lambda i: (0, 0))

    def body(*refs):
        if res is None:
            x_ref, g_ref, h_ref = refs
            xs = x_ref[...]
        else:
            x_ref, r_ref, g_ref, xs_ref, h_ref = refs
            xs = x_ref[...] + r_ref[...]
            xs_ref[...] = xs
        r = lax.rsqrt(jnp.mean(xs * xs, axis=-1, keepdims=True) + EPS)
        h_ref[...] = (xs * r * g_ref[...]).astype(bf16)

    h_shape = jax.ShapeDtypeStruct((T, D), bf16)
    if res is None:
        return pl.pallas_call(body, name=name, grid=(T // tr,), in_specs=[row, vec], out_specs=row, out_shape=h_shape,
                              compiler_params=_params(("parallel",)))(x, gain)
    return pl.pallas_call(body, name=name, grid=(T // tr,), in_specs=[row, row, vec], out_specs=(row, row),
                          out_shape=(jax.ShapeDtypeStruct((T, D), f32), h_shape), compiler_params=_params(("parallel",)))(x, res, gain)


def _rms_bwd(xs, gain, dh, extra, name):
    T, D = xs.shape
    tr = _tile(T, 256, 8)
    row = pl.BlockSpec((tr, D), lambda i: (i, 0))
    vec = pl.BlockSpec((1, D), lambda i: (0, 0))

    def body(x_ref, g_ref, dh_ref, e_ref, dx_ref, dg_ref):
        x = x_ref[...]
        r = lax.rsqrt(jnp.mean(x * x, axis=-1, keepdims=True) + EPS)
        xhat = x * r
        dh_v = dh_ref[...]
        gd = dh_v * g_ref[...]
        dx_ref[...] = e_ref[...] + r * (gd - xhat * jnp.mean(gd * xhat, axis=-1, keepdims=True))
        part = jnp.sum(dh_v * xhat, axis=0, keepdims=True)

        @pl.when(pl.program_id(0) == 0)
        def _():
            dg_ref[...] = part

        @pl.when(pl.program_id(0) > 0)
        def _():
            dg_ref[...] += part

    return pl.pallas_call(body, name=name, grid=(T // tr,), in_specs=[row, vec, row, row], out_specs=(row, vec),
                          out_shape=(jax.ShapeDtypeStruct((T, D), f32), jax.ShapeDtypeStruct((1, D), f32)),
                          compiler_params=_params(("arbitrary",)))(xs, gain, dh, extra)


def _merge_fwd(pa, pb, proj, b_gate, off, name):
    T, D = pa.shape
    tr, tc = _tile(T, 512, 8), _tile(D, 512)
    oa, ob, nb = off // tc, (off + D) // tc, D // tc
    blk = pl.BlockSpec((tr, tc), lambda i, j: (i, j))

    def body(pa_ref, pb_ref, ga_ref, gb_ref, ba_ref, bb_ref, o_ref):
        ga = _sigmoid(ga_ref[...] + ba_ref[...])
        gb = _sigmoid(gb_ref[...] + bb_ref[...])
        o_ref[...] = (ga * pa_ref[...] + gb * pb_ref[...]).astype(bf16)

    return pl.pallas_call(
        body, name=name, grid=(T // tr, nb),
        in_specs=[blk, blk, pl.BlockSpec((tr, tc), lambda i, j: (i, oa + j)), pl.BlockSpec((tr, tc), lambda i, j: (i, ob + j)),
                  pl.BlockSpec((1, tc), lambda i, j: (0, j)), pl.BlockSpec((1, tc), lambda i, j: (0, nb + j))],
        out_specs=blk, out_shape=jax.ShapeDtypeStruct((T, D), bf16), compiler_params=_params(("parallel", "parallel")),
    )(pa, pb, proj, proj, b_gate, b_gate)


def _branch_bwd(dm, p, proj, b_gate, off, boff, name):
    T, D = p.shape
    tr, tc = _tile(T, 512, 8), _tile(D, 512)
    og, obias = off // tc, boff // tc
    blk = pl.BlockSpec((tr, tc), lambda j, i: (i, j))
    vec = pl.BlockSpec((1, tc), lambda j, i: (0, j))

    def body(dm_ref, p_ref, gl_ref, b_ref, dp_ref, dgl_ref, db_ref):
        g = _sigmoid(gl_ref[...] + b_ref[...])
        dm_v = dm_ref[...]
        dp_ref[...] = (dm_v * g).astype(bf16)
        dgl = dm_v * p_ref[...] * g * (1.0 - g)
        dgl_ref[...] = dgl.astype(bf16)
        part = jnp.sum(dgl, axis=0, keepdims=True)

        @pl.when(pl.program_id(1) == 0)
        def _():
            db_ref[...] = part

        @pl.when(pl.program_id(1) > 0)
        def _():
            db_ref[...] += part

    return pl.pallas_call(
        body, name=name, grid=(D // tc, T // tr),
        in_specs=[blk, blk, pl.BlockSpec((tr, tc), lambda j, i: (i, og + j)), pl.BlockSpec((1, tc), lambda j, i: (0, obias + j))],
        out_specs=(blk, blk, vec),
        out_shape=(jax.ShapeDtypeStruct((T, D), bf16), jax.ShapeDtypeStruct((T, D), bf16), jax.ShapeDtypeStruct((1, D), f32)),
        compiler_params=_params(("parallel", "arbitrary")),
    )(dm, p, proj, b_gate)


def _swiglu_fwd(gu, name):
    T, F2 = gu.shape
    F = F2 // 2
    tr, tc = _tile(T, 512, 8), _tile(F, 512)
    nb = F // tc

    def body(g_ref, u_ref, o_ref):
        g = g_ref[...]
        o_ref[...] = (g * _sigmoid(g) * u_ref[...]).astype(bf16)

    return pl.pallas_call(
        body, name=name, grid=(T // tr, nb),
        in_specs=[pl.BlockSpec((tr, tc), lambda i, j: (i, j)), pl.BlockSpec((tr, tc), lambda i, j: (i, nb + j))],
        out_specs=pl.BlockSpec((tr, tc), lambda i, j: (i, j)), out_shape=jax.ShapeDtypeStruct((T, F), bf16),
        compiler_params=_params(("parallel", "parallel")),
    )(gu, gu)


def _swiglu_bwd(gu, dact, name):
    T, F2 = gu.shape
    F = F2 // 2
    tr, tc = _tile(T, 512, 8), _tile(F, 512)
    nb = F // tc
    blk = pl.BlockSpec((tr, tc), lambda i, j: (i, j))

    def body(g_ref, u_ref, d_ref, dg_ref, du_ref):
        g = g_ref[...]
        s = _sigmoid(g)
        d = d_ref[...]
        dg_ref[...] = (d * u_ref[...] * s * (1.0 + g * (1.0 - s))).astype(bf16)
        du_ref[...] = (d * g * s).astype(bf16)

    return pl.pallas_call(
        body, name=name, grid=(T // tr, nb),
        in_specs=[blk, pl.BlockSpec((tr, tc), lambda i, j: (i, nb + j)), blk], out_specs=(blk, blk),
        out_shape=(jax.ShapeDtypeStruct((T, F), bf16), jax.ShapeDtypeStruct((T, F), bf16)),
        compiler_params=_params(("parallel", "parallel")),
    )(gu, gu, dact)


def _loss_head(x1, fo, target, name):
    T, D = x1.shape
    tr = _tile(T, 256, 8)
    row = pl.BlockSpec((tr, D), lambda i: (i, 0))
    acc = pl.BlockSpec((8, 128), lambda i: (0, 0))

    def body(x_ref, f_ref, t_ref, dy_ref, l_ref):
        d = x_ref[...] + f_ref[...] - t_ref[...]
        dy_ref[...] = d * (1.0 / D)
        part = jnp.sum(jnp.sum(d * d, axis=1, keepdims=True), axis=0, keepdims=True)

        @pl.when(pl.program_id(0) == 0)
        def _():
            l_ref[...] = jnp.zeros((8, 128), f32)

        l_ref[...] += part

    return pl.pallas_call(body, name=name, grid=(T // tr,), in_specs=[row, row, row], out_specs=(row, acc),
                          out_shape=(jax.ShapeDtypeStruct((T, D), f32), jax.ShapeDtypeStruct((8, 128), f32)),
                          compiler_params=_params(("arbitrary",)))(x1, fo, target)


_DIMS = {"nn": ((1,), (0,)), "nt": ((1,), (1,)), "tn": ((0,), (0,))}
_MODE = {v: k for k, v in _DIMS.items()}


def _dot_bf16(a, b, mode):
    return lax.dot_general(a.astype(bf16), b.astype(bf16), (_DIMS[mode], ((), ())), preferred_element_type=f32)


@functools.partial(jax.custom_vjp, nondiff_argnums=(2,))
def _dotm(a, b, mode):
    return _dot_bf16(a, b, mode)


def _dotm_fwd(a, b, mode):
    return _dot_bf16(a, b, mode), (a, b)


def _dotm_bwd(mode, res, g):
    a, b = res
    if mode == "nn":
        return _dot_bf16(g, b, "nt"), _dot_bf16(a, g, "tn")
    if mode == "nt":
        return _dot_bf16(g, b, "nn"), _dot_bf16(g, a, "tn")
    return _dot_bf16(b, g, "nt"), _dot_bf16(a, g, "nn")


_dotm.defvjp(_dotm_fwd, _dotm_bwd)


def _dotb(a, b, dims):
    return _dotm(a, b, _MODE[dims])


def _triangle_sum(v, lower):
    row = lax.broadcasted_iota(jnp.int32, (CHUNK, CHUNK), 0)
    col = lax.broadcasted_iota(jnp.int32, (CHUNK, CHUNK), 1)
    tri = ((col <= row) if lower else (col >= row)).astype(bf16)

    def top(t):
        return lax.bitcast_convert_type(lax.bitcast_convert_type(t, jnp.uint32) & jnp.uint32(0xFFFF0000), f32)

    hi = top(v)
    mid = top(v - hi)
    low = (v - hi) - mid
    hi, mid, low = hi.astype(bf16), mid.astype(bf16), low.astype(bf16)
    dn = (((1,), (0,)), ((), ()))
    return (lax.dot_general(tri, hi, dn, preferred_element_type=f32) + lax.dot_general(tri, mid, dn, preferred_element_type=f32)
            + lax.dot_general(tri, low, dn, preferred_element_type=f32))


@jax.custom_vjp
def _cumsum_rows(v):
    return _triangle_sum(v, True)


_cumsum_rows.defvjp(lambda v: (_triangle_sum(v, True), None), lambda _, g: (_triangle_sum(g, False),))


def _hgrn_heads(q, fl, iv, g, logits, gain, st):
    r = range(len(q))
    lb = [jax.nn.softmax(logits[j], axis=0)[0:1] for j in r]
    f = [lb[j] + (1.0 - lb[j]) * _sigmoid(fl[j]) for j in r]
    lf = [jnp.log(f[j]) for j in r]
    kk = [1.0 - f[j] for j in r]
    qs = [q[j] * _sigmoid(q[j]) for j in r]
    b = [_cumsum_rows(lf[j]) for j in r]
    b_last = [jnp.sum(lf[j], axis=0, keepdims=True) for j in r]
    o = [_dotb(qs[j] * jnp.exp(b[j]), st[j], ((1,), (1,))) for j in r]
    r3 = lax.broadcasted_iota(jnp.int32, (SUB, SUB, HEAD), 0)
    c3 = lax.broadcasted_iota(jnp.int32, (SUB, SUB, HEAD), 1)
    parts = [[] for _ in r]
    for i in range(CHUNK // SUB):
        lo, hi = i * SUB, (i + 1) * SUB
        bi = [b[j][lo:hi] for j in r]
        dec = [jnp.exp(jnp.where(c3 <= r3, bi[j][:, None, :] - bi[j][None, :, :], -jnp.inf)) for j in r]
        s = [jnp.sum(qs[j][lo:hi][:, None, :] * kk[j][lo:hi][None, :, :] * dec[j], axis=-1) for j in r]
        if i > 0:
            anchor = [jnp.max(bi[j], axis=0, keepdims=True) for j in r]
            qa = [qs[j][lo:hi] * jnp.exp(bi[j] - anchor[j]) for j in r]
            kd = [kk[j][:lo] * jnp.exp(anchor[j] - b[j][:lo]) for j in r]
            s = [jnp.concatenate([_dotb(qa[j], kd[j], ((1,), (1,))), s[j]], axis=1) for j in r]
        for j in r:
            parts[j].append(_dotb(s[j], iv[j][:hi], ((1,), (0,))))
    o = [o[j] + jnp.concatenate(parts[j], axis=0) for j in r]
    st_new = [st[j] * jnp.exp(b_last[j]) + _dotb(iv[j], kk[j] * jnp.exp(b_last[j] - b[j]), ((0,), (0,))) for j in r]
    o = [o[j] * lax.rsqrt(jnp.mean(o[j] * o[j], axis=-1, keepdims=True) + EPS) for j in r]
    o = [o[j] * gain[j] * (g[j] * _sigmoid(g[j])) for j in r]
    return o, st_new


def _group(n, pref):
    while n % pref:
        pref //= 2
    return pref


def _hgrn_fwd(proj, logits, gain, n_heads, name):
    T = proj.shape[0]
    nc = T // CHUNK
    H = n_heads
    HB = _group(H, 8)
    W = HB * HEAD

    def col(k):
        return pl.BlockSpec((CHUNK, W), lambda h, c: (c, k * (H // HB) + h))

    def body(q_ref, f_ref, i_ref, g_ref, l_ref, ga_ref, y_ref, s_ref, st):
        @pl.when(pl.program_id(1) == 0)
        def _():
            st[...] = jnp.zeros((HB, HEAD, HEAD), f32)

        cols = [slice(j * HEAD, (j + 1) * HEAD) for j in range(HB)]
        heads = lambda ref: [ref[:, cs] for cs in cols]
        s_ref[...] = st[...]
        o, st_new = _hgrn_heads(heads(q_ref), heads(f_ref), heads(i_ref), heads(g_ref), heads(l_ref), heads(ga_ref), [st[j] for j in range(HB)])
        for j, cs in enumerate(cols):
            y_ref[:, cs] = o[j].astype(bf16)
            st[j] = st_new[j]

    return pl.pallas_call(
        body, name=name, grid=(H // HB, nc),
        in_specs=[col(0), col(1), col(2), col(3), pl.BlockSpec((2, W), lambda h, c: (0, h)), pl.BlockSpec((1, W), lambda h, c: (0, h))],
        out_specs=(pl.BlockSpec((CHUNK, W), lambda h, c: (c, h)), pl.BlockSpec((HB, None, HEAD, HEAD), lambda h, c: (h, c, 0, 0))),
        out_shape=(jax.ShapeDtypeStruct((T, H * HEAD), bf16), jax.ShapeDtypeStruct((H, nc, HEAD, HEAD), f32)),
        scratch_shapes=[pltpu.VMEM((HB, HEAD, HEAD), f32)],
        compiler_params=_params(("parallel", "arbitrary")),
    )(proj, proj, proj, proj, logits, gain)


def _hgrn_bwd(proj, logits, gain, states, dy, n_heads, name):
    T = proj.shape[0]
    nc = T // CHUNK
    H = n_heads
    HB = _group(H, HGRN_BWD_HEADS)
    W = HB * HEAD

    def col(k):
        return pl.BlockSpec((CHUNK, W), lambda h, c: (nc - 1 - c, k * (H // HB) + h))

    out_blk = pl.BlockSpec((CHUNK, W), lambda h, c: (nc - 1 - c, h))

    def body(q_ref, f_ref, i_ref, g_ref, l_ref, ga_ref, s_ref, dy_ref, dq_ref, df_ref, di_ref, dg_ref, dl_ref, dga_ref, dst):
        first = pl.program_id(1) == 0

        @pl.when(first)
        def _():
            dst[...] = jnp.zeros((HB, HEAD, HEAD), f32)
            dl_ref[...] = jnp.zeros((2, W), f32)
            dga_ref[...] = jnp.zeros((1, W), f32)

        cols = [slice(j * HEAD, (j + 1) * HEAD) for j in range(HB)]
        heads = lambda ref: [ref[:, cs] for cs in cols]
        _, vjp = jax.vjp(_hgrn_heads, heads(q_ref), heads(f_ref), heads(i_ref), heads(g_ref), heads(l_ref), heads(ga_ref),
                         [s_ref[j] for j in range(HB)])
        dq, df, di, dg, dl, dga, ds = vjp((heads(dy_ref), [dst[j] for j in range(HB)]))
        for j, cs in enumerate(cols):
            dq_ref[:, cs] = dq[j].astype(bf16)
            df_ref[:, cs] = df[j].astype(bf16)
            di_ref[:, cs] = di[j].astype(bf16)
            dg_ref[:, cs] = dg[j].astype(bf16)
            dst[j] = ds[j]
            dl_ref[:, cs] += dl[j]
            dga_ref[:, cs] += dga[j]

    act = jax.ShapeDtypeStruct((T, H * HEAD), bf16)
    return pl.pallas_call(
        body, name=name, grid=(H // HB, nc),
        in_specs=[col(0), col(1), col(2), col(3), pl.BlockSpec((2, W), lambda h, c: (0, h)), pl.BlockSpec((1, W), lambda h, c: (0, h)),
                  pl.BlockSpec((HB, None, HEAD, HEAD), lambda h, c: (h, nc - 1 - c, 0, 0)), out_blk],
        out_specs=(out_blk, out_blk, out_blk, out_blk, pl.BlockSpec((2, W), lambda h, c: (0, h)), pl.BlockSpec((1, W), lambda h, c: (0, h))),
        out_shape=(act, act, act, act, jax.ShapeDtypeStruct((2, H * HEAD), f32), jax.ShapeDtypeStruct((1, H * HEAD), f32)),
        scratch_shapes=[pltpu.VMEM((HB, HEAD, HEAD), f32)],
        compiler_params=_params(("parallel", "arbitrary")),
    )(proj, proj, proj, proj, logits, gain, states, dy)


def _rel_index():
    t = np.arange(CHUNK)[:, None]
    sp = np.arange(BAND * CHUNK)[None, :]
    dist = (N_PAST - sp // CHUNK) * CHUNK + t - sp % CHUNK
    return (np.clip(dist, -REL_FUTURE, REL_PAST) + REL_FUTURE).reshape(1, -1).astype(np.int32)


def _bias_table(rel_bias_pad, idx, name):
    H = rel_bias_pad.shape[0]
    n = idx.shape[1]
    tc = _tile(n, 4096)

    def body(rb_ref, idx_ref, o_ref):
        onehot = (lax.broadcasted_iota(jnp.int32, (N_REL_PAD, tc), 0) == idx_ref[...]).astype(f32)
        o_ref[...] = lax.dot_general(rb_ref[...], onehot, (((1,), (0,)), ((), ())), precision=HI, preferred_element_type=f32)

    return pl.pallas_call(
        body, name=name, grid=(n // tc,),
        in_specs=[pl.BlockSpec((H, N_REL_PAD), lambda j: (0, 0)), pl.BlockSpec((1, tc), lambda j: (0, j))],
        out_specs=pl.BlockSpec((H, tc), lambda j: (0, j)), out_shape=jax.ShapeDtypeStruct((H, n), f32),
        compiler_params=_params(("parallel",)),
    )(rel_bias_pad, idx)


def _bias_table_bwd(dbias, idx, name):
    H, n = dbias.shape
    tc = _tile(n, 4096)

    def body(d_ref, idx_ref, o_ref):
        onehot = (lax.broadcasted_iota(jnp.int32, (N_REL_PAD, tc), 0) == idx_ref[...]).astype(f32)
        part = lax.dot_general(d_ref[...], onehot, (((1,), (1,)), ((), ())), precision=HI, preferred_element_type=f32)

        @pl.when(pl.program_id(0) == 0)
        def _():
            o_ref[...] = part

        @pl.when(pl.program_id(0) > 0)
        def _():
            o_ref[...] += part

    return pl.pallas_call(
        body, name=name, grid=(n // tc,),
        in_specs=[pl.BlockSpec((H, tc), lambda j: (0, j)), pl.BlockSpec((1, tc), lambda j: (0, j))],
        out_specs=pl.BlockSpec((H, N_REL_PAD), lambda j: (0, 0)), out_shape=jax.ShapeDtypeStruct((H, N_REL_PAD), f32),
        compiler_params=_params(("arbitrary",)),
    )(dbias, idx)


def _head_norm(t, gain):
    return t * lax.rsqrt(jnp.mean(t * t, axis=-1, keepdims=True) + EPS) * gain


def _attn_chunks(qs, kbs, vbs, qg, bias, ns):
    r = range(len(qs))
    qh = [_head_norm(qs[j], qg) for j in r]
    s = [_dotb(qh[j], kbs[j], ((1,), (1,))) * (HEAD ** -0.5) + bias for j in r]
    col = lax.broadcasted_iota(jnp.int32, (1, BAND * CHUNK), 1)
    s = [jnp.where(ns[j] * CHUNK - PAD + col >= 0, s[j], NEG) for j in r]
    e = [jnp.exp(s[j] - jnp.max(s[j], axis=-1, keepdims=True)) for j in r]
    p = [e[j] / jnp.sum(e[j], axis=-1, keepdims=True) for j in r]
    return [_dotb(p[j], vbs[j], ((1,), (0,))) for j in r]


def _attn_fwd(proj, q_gain, k_gain, bias, off, n_heads, name):
    T = proj.shape[0]
    nc = T // CHUNK
    H = n_heads
    CB = _group(nc, ATTN_CHUNKS)
    o0 = off // HEAD
    full = lambda k: pl.BlockSpec((T, HEAD), lambda h, c: (0, o0 + k * H + h))
    vec = pl.BlockSpec((1, HEAD), lambda h, c: (0, 0))

    def body(q_ref, k_ref, v_ref, qg_ref, kg_ref, b_ref, y_ref, kp, vp):
        c = pl.program_id(1)

        @pl.when(c == 0)
        def _():
            kp[pl.ds(0, PAD), :] = jnp.zeros((PAD, HEAD), f32)
            vp[pl.ds(0, PAD), :] = jnp.zeros((PAD, HEAD), f32)
            kp[pl.ds(PAD, T), :] = _head_norm(k_ref[...], kg_ref[...])
            vp[pl.ds(PAD, T), :] = v_ref[...]

        ns = [c * CB + j for j in range(CB)]
        rows = [pl.ds(j * CHUNK, CHUNK) for j in range(CB)]
        bands = [pl.ds(pl.multiple_of(n * CHUNK, CHUNK), BAND * CHUNK) for n in ns]
        outs = _attn_chunks([q_ref[r, :] for r in rows], [kp[b, :] for b in bands], [vp[b, :] for b in bands], qg_ref[...], b_ref[...], ns)
        for r, o in zip(rows, outs):
            y_ref[r, :] = o.astype(bf16)

    return pl.pallas_call(
        body, name=name, grid=(H, nc // CB),
        in_specs=[pl.BlockSpec((CB * CHUNK, HEAD), lambda h, c: (c, o0 + h)), full(1), full(2), vec, vec,
                  pl.BlockSpec((None, CHUNK, BAND * CHUNK), lambda h, c: (h, 0, 0))],
        out_specs=pl.BlockSpec((CB * CHUNK, HEAD), lambda h, c: (c, h)), out_shape=jax.ShapeDtypeStruct((T, H * HEAD), bf16),
        scratch_shapes=[pltpu.VMEM((T + PAD, HEAD), f32), pltpu.VMEM((T + PAD, HEAD), f32)],
        compiler_params=_params(("parallel", "arbitrary")),
    )(proj, proj, proj, q_gain, k_gain, bias)


def _attn_bwd(proj, q_gain, k_gain, bias, dy, off, n_heads, name):
    T = proj.shape[0]
    nc = T // CHUNK
    H = n_heads
    CB = _group(nc, ATTN_CHUNKS)
    o0 = off // HEAD
    full = lambda k: pl.BlockSpec((T, HEAD), lambda h, c: (0, o0 + k * H + h))
    full_out = pl.BlockSpec((T, HEAD), lambda h, c: (0, h))
    vec = pl.BlockSpec((1, HEAD), lambda h, c: (0, 0))
    chunk_out = pl.BlockSpec((CB * CHUNK, HEAD), lambda h, c: (c, h))
    bias_blk = pl.BlockSpec((None, CHUNK, BAND * CHUNK), lambda h, c: (h, 0, 0))

    def body(q_ref, k_ref, v_ref, qg_ref, kg_ref, b_ref, dy_ref, dq_ref, dk_ref, dv_ref, db_ref, dqg_ref, dkg_ref, kp, vp, dkp, dvp):
        h = pl.program_id(0)
        c = pl.program_id(1)

        @pl.when(c == 0)
        def _():
            kp[pl.ds(0, PAD), :] = jnp.zeros((PAD, HEAD), f32)
            vp[pl.ds(0, PAD), :] = jnp.zeros((PAD, HEAD), f32)
            kp[pl.ds(PAD, T), :] = _head_norm(k_ref[...], kg_ref[...])
            vp[pl.ds(PAD, T), :] = v_ref[...]
            dkp[...] = jnp.zeros((T + PAD, HEAD), f32)
            dvp[...] = jnp.zeros((T + PAD, HEAD), f32)
            db_ref[...] = jnp.zeros((CHUNK, BAND * CHUNK), f32)

        @pl.when(jnp.logical_and(h == 0, c == 0))
        def _():
            dqg_ref[...] = jnp.zeros((1, HEAD), f32)
            dkg_ref[...] = jnp.zeros((1, HEAD), f32)

        ns = [c * CB + j for j in range(CB)]
        rows = [pl.ds(j * CHUNK, CHUNK) for j in range(CB)]
        bands = [pl.ds(pl.multiple_of(n * CHUNK, CHUNK), BAND * CHUNK) for n in ns]
        _, vjp = jax.vjp(functools.partial(_attn_chunks, ns=ns), [q_ref[r, :] for r in rows], [kp[b, :] for b in bands],
                         [vp[b, :] for b in bands], qg_ref[...], b_ref[...])
        dqs, dkbs, dvbs, dqg, db = vjp([dy_ref[r, :] for r in rows])
        db_ref[...] += db
        dqg_ref[...] += dqg
        for r, b, dq, dkb, dvb in zip(rows, bands, dqs, dkbs, dvbs):
            dq_ref[r, :] = dq.astype(bf16)
            dkp[b, :] += dkb
            dvp[b, :] += dvb

        @pl.when(c == nc // CB - 1)
        def _():
            _, nvjp = jax.vjp(_head_norm, k_ref[...], kg_ref[...])
            dk, dkg = nvjp(dkp[pl.ds(PAD, T), :])
            dk_ref[...] = dk.astype(bf16)
            dv_ref[...] = dvp[pl.ds(PAD, T), :].astype(bf16)
            dkg_ref[...] += dkg

    act = jax.ShapeDtypeStruct((T, H * HEAD), bf16)
    gvec = jax.ShapeDtypeStruct((1, HEAD), f32)
    pad_buf = pltpu.VMEM((T + PAD, HEAD), f32)
    return pl.pallas_call(
        body, name=name, grid=(H, nc // CB),
        in_specs=[pl.BlockSpec((CB * CHUNK, HEAD), lambda h, c: (c, o0 + h)), full(1), full(2), vec, vec, bias_blk, chunk_out],
        out_specs=(chunk_out, full_out, full_out, bias_blk, vec, vec),
        out_shape=(act, act, act, jax.ShapeDtypeStruct((H, CHUNK, BAND * CHUNK), f32), gvec, gvec),
        scratch_shapes=[pad_buf, pad_buf, pad_buf, pad_buf],
        compiler_params=_params(("arbitrary", "arbitrary")),
    )(proj, proj, proj, q_gain, k_gain, bias, dy)


def _position():
    x, y, c = lax.axis_index("x"), lax.axis_index("y"), lax.axis_index("c")
    return x, y, c, 4 * x + 2 * y + c


def _flip(v, bit):
    return 1 - v if bit else v


def _chips(x, y):
    return [(1 - x, y), (x, 1 - y), (1 - x, 1 - y)]


def _seq_gather(shards, name, collective_id):
    n = len(shards)

    def body(*refs):
        ins, outs = refs[:n], refs[n:2 * n]
        send, recv, loc = refs[2 * n:]
        x, y, c, me = _position()
        sib = (x, y, 1 - c)
        chips = _chips(x, y)
        barrier = pltpu.get_barrier_semaphore()
        for peer in [sib] + [(px, py, c) for px, py in chips]:
            pl.semaphore_signal(barrier, inc=1, device_id=peer, device_id_type=MESH)
        pl.semaphore_wait(barrier, 4)

        def copy(w, k, src, blk, to):
            return pltpu.make_async_remote_copy(src_ref=src, dst_ref=outs[w].at[blk], send_sem=send.at[7 * w + k], recv_sem=recv.at[7 * w + k],
                                                device_id=to, device_id_type=MESH)

        mine = [pltpu.make_async_copy(ins[w], outs[w].at[me], loc.at[w]) for w in range(n)]
        for cp in mine:
            cp.start()
        first = []
        for j, (px, py) in enumerate(chips):
            first += [copy(w, 1 + j, ins[w], me, (px, py, c)) for w in range(n)]
        first += [copy(w, 0, ins[w], me, sib) for w in range(n)]
        for cp in first:
            cp.start()
        passed = []
        for j, (px, py) in enumerate(chips):
            blk = 4 * px + 2 * py + c
            for w in range(n):
                copy(w, 1 + j, ins[w], blk, sib).wait_recv()
                fwd = copy(w, 4 + j, outs[w].at[blk], blk, sib)
                fwd.start()
                passed.append(fwd)
        for w in range(n):
            copy(w, 0, ins[w], 4 * x + 2 * y + (1 - c), sib).wait_recv()
        for j, (px, py) in enumerate(chips):
            for w in range(n):
                copy(w, 4 + j, ins[w], 4 * px + 2 * py + (1 - c), sib).wait_recv()
        for cp in first + passed:
            cp.wait_send()
        for cp in mine:
            cp.wait()

    return pl.kernel(
        body, out_type=tuple(jax.ShapeDtypeStruct((NDEV,) + s.shape, s.dtype) for s in shards),
        mesh=plsc.ScalarSubcoreMesh(axis_name="sequencer", num_cores=1), name=name,
        scratch_types=(pltpu.SemaphoreType.DMA((7 * n,)), pltpu.SemaphoreType.DMA((7 * n,)), pltpu.SemaphoreType.DMA((n,))),
        compiler_params=pltpu.CompilerParams(collective_id=collective_id),
    )(*shards)


def _seq_scatter(grads, name, collective_id, after=()):
    n, na = len(grads), len(after)

    def body(*refs):
        ins, outs = refs[:n], refs[n + na:2 * n + na]
        send, recv, loc = refs[2 * n + na:]
        x, y, c, me = _position()
        peers = [(_flip(x, r & 4), _flip(y, r & 2), _flip(c, r & 1)) for r in range(1, NDEV)]
        barrier = pltpu.get_barrier_semaphore()
        for peer in peers:
            pl.semaphore_signal(barrier, inc=1, device_id=peer, device_id_type=MESH)
        pl.semaphore_wait(barrier, NDEV - 1)
        mine = [pltpu.make_async_copy(ins[w].at[me], outs[w].at[me], loc.at[w]) for w in range(n)]
        for cp in mine:
            cp.start()
        sends, waits = [], []
        for k, (px, py, pc) in enumerate(peers):
            peer = 4 * px + 2 * py + pc
            for w in range(n):
                sends.append(pltpu.make_async_remote_copy(src_ref=ins[w].at[peer], dst_ref=outs[w].at[me], send_sem=send.at[7 * w + k],
                                                          recv_sem=recv.at[7 * w + k], device_id=(px, py, pc), device_id_type=MESH))
                waits.append(pltpu.make_async_remote_copy(src_ref=ins[w].at[peer], dst_ref=outs[w].at[peer], send_sem=send.at[7 * w + k],
                                                          recv_sem=recv.at[7 * w + k], device_id=(px, py, pc), device_id_type=MESH))
        for cp in sends:
            cp.start()
        for cp in waits:
            cp.wait_recv()
        for cp in sends:
            cp.wait_send()
        for cp in mine:
            cp.wait()

    return pl.kernel(
        body, out_type=tuple(jax.ShapeDtypeStruct(g.shape, g.dtype) for g in grads),
        mesh=plsc.ScalarSubcoreMesh(axis_name="sequencer", num_cores=1), name=name,
        scratch_types=(pltpu.SemaphoreType.DMA((7 * n,)), pltpu.SemaphoreType.DMA((7 * n,)), pltpu.SemaphoreType.DMA((n,))),
        compiler_params=pltpu.CompilerParams(collective_id=collective_id),
    )(*grads, *after)


def _small_all_reduce(v, name):
    R, C = v.shape

    def body(v_ref, o_ref, buf, send, recv):
        x, y, c, me = _position()
        buf[me] = v_ref[...]
        sends, waits = [], []
        for r in range(1, NDEV):
            px, py, pc = _flip(x, r & 4), _flip(y, r & 2), _flip(c, r & 1)
            peer = 4 * px + 2 * py + pc
            sends.append(pltpu.make_async_remote_copy(src_ref=v_ref, dst_ref=buf.at[me], send_sem=send.at[r - 1], recv_sem=recv.at[r - 1],
                                                      device_id=(px, py, pc), device_id_type=MESH))
            waits.append(pltpu.make_async_remote_copy(src_ref=v_ref, dst_ref=buf.at[peer], send_sem=send.at[r - 1], recv_sem=recv.at[r - 1],
                                                      device_id=(px, py, pc), device_id_type=MESH))
        for cp in sends:
            cp.start()
        for cp in waits:
            cp.wait_recv()
        for cp in sends:
            cp.wait_send()
        acc = buf[0]
        for i in range(1, NDEV):
            acc = acc + buf[i]
        o_ref[...] = acc

    vm = pl.BlockSpec(memory_space=pltpu.VMEM)
    return pl.pallas_call(
        body, name=name, in_specs=[vm], out_specs=vm, out_shape=jax.ShapeDtypeStruct((R, C), f32),
        scratch_shapes=[pltpu.VMEM((NDEV, R, C), f32), pltpu.SemaphoreType.DMA((7,)), pltpu.SemaphoreType.DMA((7,))],
    )(v)


def _adamw_math(w, g, m, v):
    m = ADAM_B1 * m + (1.0 - ADAM_B1) * g
    v = ADAM_B2 * v + (1.0 - ADAM_B2) * (g * g)
    m_hat = m / (1.0 - ADAM_B1 ** ADAM_STEP)
    v_hat = v / (1.0 - ADAM_B2 ** ADAM_STEP)
    delta = -ADAM_LR * (m_hat / (jnp.sqrt(v_hat) + ADAM_EPS) + ADAM_WD * w)
    return delta, m, v


def _adamw_parts(w, m, v, parts, name, after=()):
    R, C = w.shape
    tr = _tile(R, 128, 16)
    blk = pl.BlockSpec((tr, C), lambda i: (i, 0))

    def body(w_ref, m_ref, v_ref, p_ref, *rest):
        g_ref, d_ref, mo_ref, vo_ref = rest[len(after):]
        g = p_ref[0].astype(f32)
        for i in range(1, NDEV):
            g = g + p_ref[i].astype(f32)
        d, mn, vn = _adamw_math(w_ref[...], g, m_ref[...], v_ref[...])
        g_ref[...] = g
        d_ref[...] = d
        mo_ref[...] = mn
        vo_ref[...] = vn

    shp = jax.ShapeDtypeStruct((R, C), f32)
    return pl.pallas_call(
        body, name=name, grid=(R // tr,),
        in_specs=[blk, blk, blk, pl.BlockSpec((NDEV, tr, C), lambda i: (0, i, 0))] + [pl.BlockSpec(a.shape, lambda i: (0, 0)) for a in after],
        out_specs=(blk, blk, blk, blk), out_shape=(shp, shp, shp, shp), compiler_params=_params(("parallel",)),
    )(w, m, v, parts, *after)


def _adamw_small(w, g, m, v, name):
    def body(w_ref, g_ref, m_ref, v_ref, d_ref, mo_ref, vo_ref):
        d, mn, vn = _adamw_math(w_ref[...], g_ref[...], m_ref[...], v_ref[...])
        d_ref[...] = d
        mo_ref[...] = mn
        vo_ref[...] = vn

    shp = jax.ShapeDtypeStruct(w.shape, f32)
    return pl.pallas_call(body, name=name, out_shape=(shp, shp, shp))(w, g, m, v)


SMALL_COLS = 1024


def _pack(arrs):
    flat = jnp.concatenate([a.reshape(-1) for a in arrs])
    rows = -(-flat.shape[0] // (8 * SMALL_COLS)) * 8
    return jnp.pad(flat, (0, rows * SMALL_COLS - flat.shape[0])).reshape(rows, SMALL_COLS)


def _unpack(packed, like):
    flat = packed.reshape(-1)
    out, pos = [], 0
    for a in like:
        out.append(flat[pos:pos + a.size].reshape(a.shape))
        pos += a.size
    return out


def kernel(x, w_in, b_gate, norm_mix, norm_ffn, hgrn_lb_logits, hgrn_out_gain, q_gain, k_gain, rel_bias, w_proj_a, w_proj_b, w_out, w_ffn_in, w_ffn_out, loss_target, m_w_in, m_b_gate, m_norm_mix, m_norm_ffn, m_hgrn_lb_logits, m_hgrn_out_gain, m_q_gain, m_k_gain, m_rel_bias, m_w_proj_a, m_w_proj_b, m_w_out, m_w_ffn_in, m_w_ffn_out, v_w_in, v_b_gate, v_norm_mix, v_norm_ffn, v_hgrn_lb_logits, v_hgrn_out_gain, v_q_gain, v_k_gain, v_rel_bias, v_w_proj_a, v_w_proj_b, v_w_out, v_w_ffn_in, v_w_ffn_out):
    xs = x[0]
    target = loss_target[0]
    T, D = xs.shape
    d_a = hgrn_out_gain.shape[-1]
    H = d_a // HEAD
    d_b = d_a
    off_b = 4 * d_a
    off_g = off_b + 3 * d_b
    assert rel_bias.shape[1] == H and T % CHUNK == 0 and T // CHUNK > N_PAST

    big_w = [w_in[0], w_proj_a[0], w_proj_b[0], w_out[0], w_ffn_in[0], w_ffn_out[0]]
    big_m = [m_w_in[0], m_w_proj_a[0], m_w_proj_b[0], m_w_out[0], m_w_ffn_in[0], m_w_ffn_out[0]]
    big_v = [v_w_in[0], v_w_proj_a[0], v_w_proj_b[0], v_w_out[0], v_w_ffn_in[0], v_w_ffn_out[0]]

    sh = [w.astype(bf16) for w in big_w]
    (g_in,) = _seq_gather(sh[0:1], "gather_a", 1)
    g_pa, g_pb, g_out = _seq_gather(sh[1:4], "gather_b", 2)
    g_fin, g_fout = _seq_gather(sh[4:6], "gather_c", 3)

    h = _rms_fwd(xs, None, norm_mix, "rms_mix")
    proj = _mm(h, g_in, mode="nn", b_blocked=True, name="mm_proj")
    y_a, states = _hgrn_fwd(proj, hgrn_lb_logits, hgrn_out_gain, H, "hgrn_fwd")
    idx = jnp.asarray(_rel_index())
    rb_pad = jnp.pad(rel_bias[0], ((0, 0), (0, N_REL_PAD - N_REL)))
    bias = _bias_table(rb_pad, idx, "bias_table").reshape(H, CHUNK, BAND * CHUNK)
    y_b = _attn_fwd(proj, q_gain, k_gain, bias, off_b, H, "attn_fwd")
    wg_out = g_out.reshape(-1, g_out.shape[-1])
    wg_fout = g_fout.reshape(-1, g_fout.shape[-1])
    pa = _mm(y_a, g_pa, mode="nn", b_blocked=True, tm=2048, name="mm_proj_a")
    pb = _mm(y_b, g_pb, mode="nn", b_blocked=True, tm=2048, name="mm_proj_b")
    merged = _merge_fwd(pa, pb, proj, b_gate, off_g, "merge_fwd")
    mo = _mm(merged, wg_out, mode="nn", name="mm_out")
    x1, h2 = _rms_fwd(xs, mo, norm_ffn, "rms_ffn")
    gu = _mm(h2, g_fin, mode="nn", b_blocked=True, name="mm_ffn_in")
    act = _swiglu_fwd(gu, "swiglu_fwd")
    fo = _mm(act, wg_fout, mode="nn", tk=2816, name="mm_ffn_out")
    dy, loss_acc = _loss_head(x1, fo, target, "loss_head")
    loss_part = loss_acc[0:1, 0:1] * (0.5 / D)

    dact = _mm(dy, wg_fout, mode="nt", tn=1408, name="mm_d_act")
    gw_fout = _mm(act, dy, mode="tn", out_dtype=bf16, tm=1408, name="mm_gw_ffn_out")
    dgate, dup = _swiglu_bwd(gu, dact, "swiglu_bwd")
    dgu = jnp.concatenate([dgate, dup], axis=1)
    gw_fin = _mm(h2, dgu, mode="tn", out_blocked=True, out_dtype=bf16, tn=g_fin.shape[-1], name="mm_gw_ffn_in")
    dh2 = _mm(dgu, g_fin, mode="nt", b_blocked=True, name="mm_d_h2")
    dx1, g_norm_ffn = _rms_bwd(x1, norm_ffn, dh2, dy, "rms_ffn_bwd")

    dmerged = _mm(dx1, wg_out, mode="nt", name="mm_d_merged")
    gw_out = _mm(merged, dx1, mode="tn", out_dtype=bf16, name="mm_gw_out")
    dpa, dgl_a, gb_a = _branch_bwd(dmerged, pa, proj, b_gate, off_g, 0, "branch_a_bwd")
    dpb, dgl_b, gb_b = _branch_bwd(dmerged, pb, proj, b_gate, off_g + D, D, "branch_b_bwd")
    dy_a = _mm(dpa, g_pa, mode="nt", b_blocked=True, tm=2048, name="mm_d_ya")
    dy_b = _mm(dpb, g_pb, mode="nt", b_blocked=True, tm=2048, name="mm_d_yb")
    gw_pa = _mm(y_a, dpa, mode="tn", out_blocked=True, out_dtype=bf16, tn=g_pa.shape[-1], name="mm_gw_proj_a")
    gw_pb = _mm(y_b, dpb, mode="tn", out_blocked=True, out_dtype=bf16, tn=g_pb.shape[-1], name="mm_gw_proj_b")

    dq_a, df_a, di_a, dg_a, g_logits, g_gain = _hgrn_bwd(proj, hgrn_lb_logits, hgrn_out_gain, states, dy_a, H, "hgrn_bwd")
    dq_b, dk_b, dv_b, dbias, g_qg, g_kg = _attn_bwd(proj, q_gain, k_gain, bias, dy_b, off_b, H, "attn_bwd")
    g_rel = _bias_table_bwd(dbias.reshape(H, -1), idx, "bias_table_bwd")[:, :N_REL]
    dproj = jnp.concatenate([dq_a, df_a, di_a, dg_a, dq_b, dk_b, dv_b, dgl_a, dgl_b], axis=1)
    gw_in = _mm(h, dproj, mode="tn", out_blocked=True, out_dtype=bf16, tn=g_in.shape[-1], name="mm_gw_in")
    dh = _mm(dproj, g_in, mode="nt", b_blocked=True, name="mm_d_h")
    grad_x, g_norm_mix = _rms_bwd(xs, norm_mix, dh, dx1, "rms_mix_bwd")

    p_fout, p_fin = _seq_scatter([gw_fout.reshape(NDEV, -1, D), gw_fin], "scatter_a", 4, after=(g_norm_ffn,))
    p_out, p_pa, p_pb = _seq_scatter([gw_out.reshape(NDEV, -1, D), gw_pa, gw_pb], "scatter_b", 5, after=(g_gain,))
    (p_in,) = _seq_scatter([gw_in], "scatter_c", 6)
    parts = [p_in, p_pa, p_pb, p_out, p_fin, p_fout]
    names = ["w_in", "w_proj_a", "w_proj_b", "w_out", "w_ffn_in", "w_ffn_out"]
    big = {}
    for nm, w, m, v, p in zip(names, big_w, big_m, big_v, parts):
        big[nm] = [o[None] for o in _adamw_parts(w, m, v, p, "adamw_" + nm, after=() if nm == "w_in" else (g_norm_mix,))]

    small_names = ["b_gate", "norm_mix", "norm_ffn", "hgrn_lb_logits", "hgrn_out_gain", "q_gain", "k_gain", "rel_bias"]
    small_w = [b_gate, norm_mix, norm_ffn, hgrn_lb_logits, hgrn_out_gain, q_gain, k_gain, rel_bias]
    small_m = [m_b_gate, m_norm_mix, m_norm_ffn, m_hgrn_lb_logits, m_hgrn_out_gain, m_q_gain, m_k_gain, m_rel_bias]
    small_v = [v_b_gate, v_norm_mix, v_norm_ffn, v_hgrn_lb_logits, v_hgrn_out_gain, v_q_gain, v_k_gain, v_rel_bias]
    small_g = [jnp.concatenate([gb_a, gb_b], axis=1), g_norm_mix, g_norm_ffn, g_logits, g_gain, g_qg, g_kg, g_rel[None], loss_part]
    g_sum = _small_all_reduce(_pack(small_g), "reduce_small")
    loss = _unpack(g_sum, small_g)[-1].reshape(())
    d_s, m_s, v_s = _adamw_small(_pack(small_w), g_sum, _pack(small_m), _pack(small_v), "adamw_small")
    small = {}
    for nm, g, d, m, v in zip(small_names, _unpack(g_sum, small_w), _unpack(d_s, small_w), _unpack(m_s, small_w), _unpack(v_s, small_w)):
        small[nm] = [g, d, m, v]

    order = ["w_in", "b_gate", "norm_mix", "norm_ffn", "hgrn_lb_logits", "hgrn_out_gain", "q_gain", "k_gain", "rel_bias",
             "w_proj_a", "w_proj_b", "w_out", "w_ffn_in", "w_ffn_out"]
    res = {**big, **small}
    outs = [loss, grad_x[None]]
    for k in range(4):
        outs += [res[nm][k] for nm in order]
    return tuple(outs)
```

```python
import functools

import numpy as np
import jax
import jax.numpy as jnp
from jax import lax
from jax.experimental import pallas as pl
from jax.experimental.pallas import tpu as pltpu
from jax.experimental.pallas import tpu_sc as plsc

f32 = jnp.float32
bf16 = jnp.bfloat16
HI = lax.Precision.HIGHEST
MESH = pl.DeviceIdType.MESH
AXES = ("x", "y", "c")
NDEV = 8

CHUNK = 64
HEAD = 128
SUB = 8
HGRN_BWD_HEADS = 8
ATTN_CHUNKS = 8
N_PAST = 8
BAND = N_PAST + 1
PAD = N_PAST * CHUNK
REL_FUTURE = CHUNK - 1
REL_PAST = 2 * CHUNK - 1
N_REL = REL_FUTURE + REL_PAST + 1
N_REL_PAD = 256
EPS = 1e-6
NEG = -1e30

ADAM_LR = 0.001
ADAM_B1 = 0.9
ADAM_B2 = 0.999
ADAM_EPS = 1e-08
ADAM_WD = 0.01
ADAM_STEP = 10

VMEM_LIMIT = 56 * 1024 * 1024


def _params(sem=None):
    return pltpu.CompilerParams(dimension_semantics=sem, vmem_limit_bytes=VMEM_LIMIT)


def _tile(n, pref, unit=128):
    if n <= pref:
        return n
    t = (pref // unit) * unit
    while t >= unit:
        if n % t == 0:
            return t
        t -= unit
    return n


_sigmoid = jax.nn.sigmoid


def _mm(a, b, *, mode, name, b_blocked=False, out_blocked=False, out_dtype=f32, tm=1024, tn=1024, tk=2048):
    if mode == "tn":
        K, M = a.shape
    else:
        M, K = a.shape
    if b_blocked:
        nb, mid, cb = b.shape
        if mode == "nn":
            assert mid == K
            N, tn = nb * cb, cb
        else:
            assert mode == "nt" and nb * cb == K
            N, tk = mid, cb
    else:
        N = b.shape[1] if mode in ("nn", "tn") else b.shape[0]
    tm = _tile(M, tm)
    tn = tn if (b_blocked and mode == "nn") or out_blocked else _tile(N, tn)
    tk = tk if b_blocked and mode == "nt" else _tile(K, tk)
    assert M % tm == 0 and N % tn == 0 and K % tk == 0
    nk = K // tk
    grid = (M // tm, N // tn, nk)
    if mode == "tn":
        a_spec = pl.BlockSpec((tk, tm), lambda i, j, k: (k, i))
    else:
        a_spec = pl.BlockSpec((tm, tk), lambda i, j, k: (i, k))
    if mode == "nn":
        b_spec = pl.BlockSpec((None, tk, cb), lambda i, j, k: (j, k, 0)) if b_blocked else pl.BlockSpec((tk, tn), lambda i, j, k: (k, j))
    elif mode == "nt":
        b_spec = pl.BlockSpec((None, tn, cb), lambda i, j, k: (k, j, 0)) if b_blocked else pl.BlockSpec((tn, tk), lambda i, j, k: (j, k))
    else:
        b_spec = pl.BlockSpec((tk, tn), lambda i, j, k: (k, j))
    if out_blocked:
        out_shape = jax.ShapeDtypeStruct((N // tn, M, tn), out_dtype)
        o_spec = pl.BlockSpec((None, tm, tn), lambda i, j, k: (j, i, 0))
    else:
        out_shape = jax.ShapeDtypeStruct((M, N), out_dtype)
        o_spec = pl.BlockSpec((tm, tn), lambda i, j, k: (i, j))
    dims = {"nn": ((1,), (0,)), "nt": ((1,), (1,)), "tn": ((0,), (0,))}[mode]

    def body(a_ref, b_ref, o_ref, *acc):
        p = lax.dot_general(a_ref[...].astype(bf16), b_ref[...].astype(bf16), (dims, ((), ())), preferred_element_type=f32)
        if nk == 1:
            o_ref[...] = p.astype(out_dtype)
        else:
            acc_ref = acc[0]
            k = pl.program_id(2)

            @pl.when(k == 0)
            def _():
                acc_ref[...] = p

            @pl.when(k > 0)
            def _():
                acc_ref[...] += p

            @pl.when(k == nk - 1)
            def _():
                o_ref[...] = acc_ref[...].astype(out_dtype)

    return pl.pallas_call(
        body, name=name, grid=grid, in_specs=[a_spec, b_spec], out_specs=o_spec, out_shape=out_shape,
        scratch_shapes=[pltpu.VMEM((tm, tn), f32)] if nk > 1 else [],
        compiler_params=_params(("parallel", "parallel", "arbitrary")),
    )(a, b)


def _rms_fwd(x, res, gain, name):
    T, D = x.shape
    tr = _tile(T, 256, 8)
    row = pl.BlockSpec((tr, D), lambda i: (i, 0))
    vec = pl.BlockSpec((1, D), lambda i: (0, 0))

    def body(*refs):
        if res is None:
            x_ref, g_ref, h_ref = refs
            xs = x_ref[...]
        else:
            x_ref, r_ref, g_ref, xs_ref, h_ref = refs
            xs = x_ref[...] + r_ref[...]
            xs_ref[...] = xs
        r = lax.rsqrt(jnp.mean(xs * xs, axis=-1, keepdims=True) + EPS)
        h_ref[...] = (xs * r * g_ref[...]).astype(bf16)

    h_shape = jax.ShapeDtypeStruct((T, D), bf16)
    if res is None:
        return pl.pallas_call(body, name=name, grid=(T // tr,), in_specs=[row, vec], out_specs=row, out_shape=h_shape,
                              compiler_params=_params(("parallel",)))(x, gain)
    return pl.pallas_call(body, name=name, grid=(T // tr,), in_specs=[row, row, vec], out_specs=(row, row),
                          out_shape=(jax.ShapeDtypeStruct((T, D), f32), h_shape), compiler_params=_params(("parallel",)))(x, res, gain)


def _rms_bwd(xs, gain, dh, extra, name):
    T, D = xs.shape
    tr = _tile(T, 256, 8)
    row = pl.BlockSpec((tr, D), lambda i: (i, 0))
    vec = pl.BlockSpec((1, D), lambda i: (0, 0))

    def body(x_ref, g_ref, dh_ref, e_ref, dx_ref, dg_ref):
        x = x_ref[...]
        r = lax.rsqrt(jnp.mean(x * x, axis=-1, keepdims=True) + EPS)
        xhat = x * r
        dh_v = dh_ref[...]
        gd = dh_v * g_ref[...]
        dx_ref[...] = e_ref[...] + r * (gd - xhat * jnp.mean(gd * xhat, axis=-1, keepdims=True))
        part = jnp.sum(dh_v * xhat, axis=0, keepdims=True)

        @pl.when(pl.program_id(0) == 0)
        def _():
            dg_ref[...] = part

        @pl.when(pl.program_id(0) > 0)
        def _():
            dg_ref[...] += part

    return pl.pallas_call(body, name=name, grid=(T // tr,), in_specs=[row, vec, row, row], out_specs=(row, vec),
                          out_shape=(jax.ShapeDtypeStruct((T, D), f32), jax.ShapeDtypeStruct((1, D), f32)),
                          compiler_params=_params(("arbitrary",)))(xs, gain, dh, extra)


def _merge_fwd(pa, pb, proj, b_gate, off, name):
    T, D = pa.shape
    tr, tc = _tile(T, 512, 8), _tile(D, 512)
    oa, ob, nb = off // tc, (off + D) // tc, D // tc
    blk = pl.BlockSpec((tr, tc), lambda i, j: (i, j))

    def body(pa_ref, pb_ref, ga_ref, gb_ref, ba_ref, bb_ref, o_ref):
        ga = _sigmoid(ga_ref[...] + ba_ref[...])
        gb = _sigmoid(gb_ref[...] + bb_ref[...])
        o_ref[...] = (ga * pa_ref[...] + gb * pb_ref[...]).astype(bf16)

    return pl.pallas_call(
        body, name=name, grid=(T // tr, nb),
        in_specs=[blk, blk, pl.BlockSpec((tr, tc), lambda i, j: (i, oa + j)), pl.BlockSpec((tr, tc), lambda i, j: (i, ob + j)),
                  pl.BlockSpec((1, tc), lambda i, j: (0, j)), pl.BlockSpec((1, tc), lambda i, j: (0, nb + j))],
        out_specs=blk, out_shape=jax.ShapeDtypeStruct((T, D), bf16), compiler_params=_params(("parallel", "parallel")),
    )(pa, pb, proj, proj, b_gate, b_gate)


def _branch_bwd(dm, p, proj, b_gate, off, boff, name):
    T, D = p.shape
    tr, tc = _tile(T, 512, 8), _tile(D, 512)
    og, obias = off // tc, boff // tc
    blk = pl.BlockSpec((tr, tc), lambda j, i: (i, j))
    vec = pl.BlockSpec((1, tc), lambda j, i: (0, j))

    def body(dm_ref, p_ref, gl_ref, b_ref, dp_ref, dgl_ref, db_ref):
        g = _sigmoid(gl_ref[...] + b_ref[...])
        dm_v = dm_ref[...]
        dp_ref[...] = (dm_v * g).astype(bf16)
        dgl = dm_v * p_ref[...] * g * (1.0 - g)
        dgl_ref[...] = dgl.astype(bf16)
        part = jnp.sum(dgl, axis=0, keepdims=True)

        @pl.when(pl.program_id(1) == 0)
        def _():
            db_ref[...] = part

        @pl.when(pl.program_id(1) > 0)
        def _():
            db_ref[...] += part

    return pl.pallas_call(
        body, name=name, grid=(D // tc, T // tr),
        in_specs=[blk, blk, pl.BlockSpec((tr, tc), lambda j, i: (i, og + j)), pl.BlockSpec((1, tc), lambda j, i: (0, obias + j))],
        out_specs=(blk, blk, vec),
        out_shape=(jax.ShapeDtypeStruct((T, D), bf16), jax.ShapeDtypeStruct((T, D), bf16), jax.ShapeDtypeStruct((1, D), f32)),
        compiler_params=_params(("parallel", "arbitrary")),
    )(dm, p, proj, b_gate)


def _swiglu_fwd(gu, name):
    T, F2 = gu.shape
    F = F2 // 2
    tr, tc = _tile(T, 512, 8), _tile(F, 512)
    nb = F // tc

    def body(g_ref, u_ref, o_ref):
        g = g_ref[...]
        o_ref[...] = (g * _sigmoid(g) * u_ref[...]).astype(bf16)

    return pl.pallas_call(
        body, name=name, grid=(T // tr, nb),
        in_specs=[pl.BlockSpec((tr, tc), lambda i, j: (i, j)), pl.BlockSpec((tr, tc), lambda i, j: (i, nb + j))],
        out_specs=pl.BlockSpec((tr, tc), lambda i, j: (i, j)), out_shape=jax.ShapeDtypeStruct((T, F), bf16),
        compiler_params=_params(("parallel", "parallel")),
    )(gu, gu)


def _swiglu_bwd(gu, dact, name):
    T, F2 = gu.shape
    F = F2 // 2
    tr, tc = _tile(T, 512, 8), _tile(F, 512)
    nb = F // tc
    blk = pl.BlockSpec((tr, tc), lambda i, j: (i, j))

    def body(g_ref, u_ref, d_ref, dg_ref, du_ref):
        g = g_ref[...]
        s = _sigmoid(g)
        d = d_ref[...]
        dg_ref[...] = (d * u_ref[...] * s * (1.0 + g * (1.0 - s))).astype(bf16)
        du_ref[...] = (d * g * s).astype(bf16)

    return pl.pallas_call(
        body, name=name, grid=(T // tr, nb),
        in_specs=[blk, pl.BlockSpec((tr, tc), lambda i, j: (i, nb + j)), blk], out_specs=(blk, blk),
        out_shape=(jax.ShapeDtypeStruct((T, F), bf16), jax.ShapeDtypeStruct((T, F), bf16)),
        compiler_params=_params(("parallel", "parallel")),
    )(gu, gu, dact)


def _loss_head(x1, fo, target, name):
    T, D = x1.shape
    tr = _tile(T, 256, 8)
    row = pl.BlockSpec((tr, D), lambda i: (i, 0))
    acc = pl.BlockSpec((8, 128), lambda i: (0, 0))

    def body(x_ref, f_ref, t_ref, dy_ref, l_ref):
        d = x_ref[...] + f_ref[...] - t_ref[...]
        dy_ref[...] = d * (1.0 / D)
        part = jnp.sum(jnp.sum(d * d, axis=1, keepdims=True), axis=0, keepdims=True)

        @pl.when(pl.program_id(0) == 0)
        def _():
            l_ref[...] = jnp.zeros((8, 128), f32)

        l_ref[...] += part

    return pl.pallas_call(body, name=name, grid=(T // tr,), in_specs=[row, row, row], out_specs=(row, acc),
                          out_shape=(jax.ShapeDtypeStruct((T, D), f32), jax.ShapeDtypeStruct((8, 128), f32)),
                          compiler_params=_params(("arbitrary",)))(x1, fo, target)


_DIMS = {"nn": ((1,), (0,)), "nt": ((1,), (1,)), "tn": ((0,), (0,))}
_MODE = {v: k for k, v in _DIMS.items()}


def _dot_bf16(a, b, mode):
    return lax.dot_general(a.astype(bf16), b.astype(bf16), (_DIMS[mode], ((), ())), preferred_element_type=f32)


@functools.partial(jax.custom_vjp, nondiff_argnums=(2,))
def _dotm(a, b, mode):
    return _dot_bf16(a, b, mode)


def _dotm_fwd(a, b, mode):
    return _dot_bf16(a, b, mode), (a, b)


def _dotm_bwd(mode, res, g):
    a, b = res
    if mode == "nn":
        return _dot_bf16(g, b, "nt"), _dot_bf16(a, g, "tn")
    if mode == "nt":
        return _dot_bf16(g, b, "nn"), _dot_bf16(g, a, "tn")
    return _dot_bf16(b, g, "nt"), _dot_bf16(a, g, "nn")


_dotm.defvjp(_dotm_fwd, _dotm_bwd)


def _dotb(a, b, dims):
    return _dotm(a, b, _MODE[dims])


def _triangle_sum(v, lower):
    row = lax.broadcasted_iota(jnp.int32, (CHUNK, CHUNK), 0)
    col = lax.broadcasted_iota(jnp.int32, (CHUNK, CHUNK), 1)
    tri = ((col <= row) if lower else (col >= row)).astype(bf16)

    def top(t):
        return lax.bitcast_convert_type(lax.bitcast_convert_type(t, jnp.uint32) & jnp.uint32(0xFFFF0000), f32)

    hi = top(v)
    mid = top(v - hi)
    low = (v - hi) - mid
    hi, mid, low = hi.astype(bf16), mid.astype(bf16), low.astype(bf16)
    dn = (((1,), (0,)), ((), ()))
    return (lax.dot_general(tri, hi, dn, preferred_element_type=f32) + lax.dot_general(tri, mid, dn, preferred_element_type=f32)
            + lax.dot_general(tri, low, dn, preferred_element_type=f32))


@jax.custom_vjp
def _cumsum_rows(v):
    return _triangle_sum(v, True)


_cumsum_rows.defvjp(lambda v: (_triangle_sum(v, True), None), lambda _, g: (_triangle_sum(g, False),))


def _hgrn_heads(q, fl, iv, g, logits, gain, st):
    r = range(len(q))
    lb = [jax.nn.softmax(logits[j], axis=0)[0:1] for j in r]
    f = [lb[j] + (1.0 - lb[j]) * _sigmoid(fl[j]) for j in r]
    lf = [jnp.log(f[j]) for j in r]
    kk = [1.0 - f[j] for j in r]
    qs = [q[j] * _sigmoid(q[j]) for j in r]
    b = [_cumsum_rows(lf[j]) for j in r]
    b_last = [jnp.sum(lf[j], axis=0, keepdims=True) for j in r]
    o = [_dotb(qs[j] * jnp.exp(b[j]), st[j], ((1,), (1,))) for j in r]
    r3 = lax.broadcasted_iota(jnp.int32, (SUB, SUB, HEAD), 0)
    c3 = lax.broadcasted_iota(jnp.int32, (SUB, SUB, HEAD), 1)
    parts = [[] for _ in r]
    for i in range(CHUNK // SUB):
        lo, hi = i * SUB, (i + 1) * SUB
        bi = [b[j][lo:hi] for j in r]
        dec = [jnp.exp(jnp.where(c3 <= r3, bi[j][:, None, :] - bi[j][None, :, :], -jnp.inf)) for j in r]
        s = [jnp.sum(qs[j][lo:hi][:, None, :] * kk[j][lo:hi][None, :, :] * dec[j], axis=-1) for j in r]
        if i > 0:
            anchor = [jnp.max(bi[j], axis=0, keepdims=True) for j in r]
            qa = [qs[j][lo:hi] * jnp.exp(bi[j] - anchor[j]) for j in r]
            kd = [kk[j][:lo] * jnp.exp(anchor[j] - b[j][:lo]) for j in r]
            s = [jnp.concatenate([_dotb(qa[j], kd[j], ((1,), (1,))), s[j]], axis=1) for j in r]
        for j in r:
            parts[j].append(_dotb(s[j], iv[j][:hi], ((1,), (0,))))
    o = [o[j] + jnp.concatenate(parts[j], axis=0) for j in r]
    st_new = [st[j] * jnp.exp(b_last[j]) + _dotb(iv[j], kk[j] * jnp.exp(b_last[j] - b[j]), ((0,), (0,))) for j in r]
    o = [o[j] * lax.rsqrt(jnp.mean(o[j] * o[j], axis=-1, keepdims=True) + EPS) for j in r]
    o = [o[j] * gain[j] * (g[j] * _sigmoid(g[j])) for j in r]
    return o, st_new


def _group(n, pref):
    while n % pref:
        pref //= 2
    return pref


def _hgrn_fwd(proj, logits, gain, n_heads, name):
    T = proj.shape[0]
    nc = T // CHUNK
    H = n_heads
    HB = _group(H, 8)
    W = HB * HEAD

    def col(k):
        return pl.BlockSpec((CHUNK, W), lambda h, c: (c, k * (H // HB) + h))

    def body(q_ref, f_ref, i_ref, g_ref, l_ref, ga_ref, y_ref, s_ref, st):
        @pl.when(pl.program_id(1) == 0)
        def _():
            st[...] = jnp.zeros((HB, HEAD, HEAD), f32)

        cols = [slice(j * HEAD, (j + 1) * HEAD) for j in range(HB)]
        heads = lambda ref: [ref[:, cs] for cs in cols]
        s_ref[...] = st[...]
        o, st_new = _hgrn_heads(heads(q_ref), heads(f_ref), heads(i_ref), heads(g_ref), heads(l_ref), heads(ga_ref), [st[j] for j in range(HB)])
        for j, cs in enumerate(cols):
            y_ref[:, cs] = o[j].astype(bf16)
            st[j] = st_new[j]

    return pl.pallas_call(
        body, name=name, grid=(H // HB, nc),
        in_specs=[col(0), col(1), col(2), col(3), pl.BlockSpec((2, W), lambda h, c: (0, h)), pl.BlockSpec((1, W), lambda h, c: (0, h))],
        out_specs=(pl.BlockSpec((CHUNK, W), lambda h, c: (c, h)), pl.BlockSpec((HB, None, HEAD, HEAD), lambda h, c: (h, c, 0, 0))),
        out_shape=(jax.ShapeDtypeStruct((T, H * HEAD), bf16), jax.ShapeDtypeStruct((H, nc, HEAD, HEAD), f32)),
        scratch_shapes=[pltpu.VMEM((HB, HEAD, HEAD), f32)],
        compiler_params=_params(("parallel", "arbitrary")),
    )(proj, proj, proj, proj, logits, gain)


def _hgrn_bwd(proj, logits, gain, states, dy, n_heads, name):
    T = proj.shape[0]
    nc = T // CHUNK
    H = n_heads
    HB = _group(H, HGRN_BWD_HEADS)
    W = HB * HEAD

    def col(k):
        return pl.BlockSpec((CHUNK, W), lambda h, c: (nc - 1 - c, k * (H // HB) + h))

    out_blk = pl.BlockSpec((CHUNK, W), lambda h, c: (nc - 1 - c, h))

    def body(q_ref, f_ref, i_ref, g_ref, l_ref, ga_ref, s_ref, dy_ref, dq_ref, df_ref, di_ref, dg_ref, dl_ref, dga_ref, dst):
        first = pl.program_id(1) == 0

        @pl.when(first)
        def _():
            dst[...] = jnp.zeros((HB, HEAD, HEAD), f32)
            dl_ref[...] = jnp.zeros((2, W), f32)
            dga_ref[...] = jnp.zeros((1, W), f32)

        cols = [slice(j * HEAD, (j + 1) * HEAD) for j in range(HB)]
        heads = lambda ref: [ref[:, cs] for cs in cols]
        _, vjp = jax.vjp(_hgrn_heads, heads(q_ref), heads(f_ref), heads(i_ref), heads(g_ref), heads(l_ref), heads(ga_ref),
                         [s_ref[j] for j in range(HB)])
        dq, df, di, dg, dl, dga, ds = vjp((heads(dy_ref), [dst[j] for j in range(HB)]))
        for j, cs in enumerate(cols):
            dq_ref[:, cs] = dq[j].astype(bf16)
            df_ref[:, cs] = df[j].astype(bf16)
            di_ref[:, cs] = di[j].astype(bf16)
            dg_ref[:, cs] = dg[j].astype(bf16)
            dst[j] = ds[j]
            dl_ref[:, cs] += dl[j]
            dga_ref[:, cs] += dga[j]

    act = jax.ShapeDtypeStruct((T, H * HEAD), bf16)
    return pl.pallas_call(
        body, name=name, grid=(H // HB, nc),
        in_specs=[col(0), col(1), col(2), col(3), pl.BlockSpec((2, W), lambda h, c: (0, h)), pl.BlockSpec((1, W), lambda h, c: (0, h)),
                  pl.BlockSpec((HB, None, HEAD, HEAD), lambda h, c: (h, nc - 1 - c, 0, 0)), out_blk],
        out_specs=(out_blk, out_blk, out_blk, out_blk, pl.BlockSpec((2, W), lambda h, c: (0, h)), pl.BlockSpec((1, W), lambda h, c: (0, h))),
        out_shape=(act, act, act, act, jax.ShapeDtypeStruct((2, H * HEAD), f32), jax.ShapeDtypeStruct((1, H * HEAD), f32)),
        scratch_shapes=[pltpu.VMEM((HB, HEAD, HEAD), f32)],
        compiler_params=_params(("parallel", "arbitrary")),
    )(proj, proj, proj, proj, logits, gain, states, dy)


def _rel_index():
    t = np.arange(CHUNK)[:, None]
    sp = np.arange(BAND * CHUNK)[None, :]
    dist = (N_PAST - sp // CHUNK) * CHUNK + t - sp % CHUNK
    return (np.clip(dist, -REL_FUTURE, REL_PAST) + REL_FUTURE).reshape(1, -1).astype(np.int32)


def _bias_table(rel_bias_pad, idx, name):
    H = rel_bias_pad.shape[0]
    n = idx.shape[1]
    tc = _tile(n, 4096)

    def body(rb_ref, idx_ref, o_ref):
        onehot = (lax.broadcasted_iota(jnp.int32, (N_REL_PAD, tc), 0) == idx_ref[...]).astype(f32)
        o_ref[...] = lax.dot_general(rb_ref[...], onehot, (((1,), (0,)), ((), ())), precision=HI, preferred_element_type=f32)

    return pl.pallas_call(
        body, name=name, grid=(n // tc,),
        in_specs=[pl.BlockSpec((H, N_REL_PAD), lambda j: (0, 0)), pl.BlockSpec((1, tc), lambda j: (0, j))],
        out_specs=pl.BlockSpec((H, tc), lambda j: (0, j)), out_shape=jax.ShapeDtypeStruct((H, n), f32),
        compiler_params=_params(("parallel",)),
    )(rel_bias_pad, idx)


def _bias_table_bwd(dbias, idx, name):
    H, n = dbias.shape
    tc = _tile(n, 4096)

    def body(d_ref, idx_ref, o_ref):
        onehot = (lax.broadcasted_iota(jnp.int32, (N_REL_PAD, tc), 0) == idx_ref[...]).astype(f32)
        part = lax.dot_general(d_ref[...], onehot, (((1,), (1,)), ((), ())), precision=HI, preferred_element_type=f32)

        @pl.when(pl.program_id(0) == 0)
        def _():
            o_ref[...] = part

        @pl.when(pl.program_id(0) > 0)
        def _():
            o_ref[...] += part

    return pl.pallas_call(
        body, name=name, grid=(n // tc,),
        in_specs=[pl.BlockSpec((H, tc), lambda j: (0, j)), pl.BlockSpec((1, tc), lambda j: (0, j))],
        out_specs=pl.BlockSpec((H, N_REL_PAD), lambda j: (0, 0)), out_shape=jax.ShapeDtypeStruct((H, N_REL_PAD), f32),
        compiler_params=_params(("arbitrary",)),
    )(dbias, idx)


def _head_norm(t, gain):
    return t * lax.rsqrt(jnp.mean(t * t, axis=-1, keepdims=True) + EPS) * gain


def _attn_chunks(qs, kbs, vbs, qg, bias, ns):
    r = range(len(qs))
    qh = [_head_norm(qs[j], qg) for j in r]
    s = [_dotb(qh[j], kbs[j], ((1,), (1,))) * (HEAD ** -0.5) + bias for j in r]
    col = lax.broadcasted_iota(jnp.int32, (1, BAND * CHUNK), 1)
    s = [jnp.where(ns[j] * CHUNK - PAD + col >= 0, s[j], NEG) for j in r]
    e = [jnp.exp(s[j] - jnp.max(s[j], axis=-1, keepdims=True)) for j in r]
    p = [e[j] / jnp.sum(e[j], axis=-1, keepdims=True) for j in r]
    return [_dotb(p[j], vbs[j], ((1,), (0,))) for j in r]


def _attn_fwd(proj, q_gain, k_gain, bias, off, n_heads, name):
    T = proj.shape[0]
    nc = T // CHUNK
    H = n_heads
    CB = _group(nc, ATTN_CHUNKS)
    o0 = off // HEAD
    full = lambda k: pl.BlockSpec((T, HEAD), lambda h, c: (0, o0 + k * H + h))
    vec = pl.BlockSpec((1, HEAD), lambda h, c: (0, 0))

    def body(q_ref, k_ref, v_ref, qg_ref, kg_ref, b_ref, y_ref, kp, vp):
        c = pl.program_id(1)

        @pl.when(c == 0)
        def _():
            kp[pl.ds(0, PAD), :] = jnp.zeros((PAD, HEAD), f32)
            vp[pl.ds(0, PAD), :] = jnp.zeros((PAD, HEAD), f32)
            kp[pl.ds(PAD, T), :] = _head_norm(k_ref[...], kg_ref[...])
            vp[pl.ds(PAD, T), :] = v_ref[...]

        ns = [c * CB + j for j in range(CB)]
        rows = [pl.ds(j * CHUNK, CHUNK) for j in range(CB)]
        bands = [pl.ds(pl.multiple_of(n * CHUNK, CHUNK), BAND * CHUNK) for n in ns]
        outs = _attn_chunks([q_ref[r, :] for r in rows], [kp[b, :] for b in bands], [vp[b, :] for b in bands], qg_ref[...], b_ref[...], ns)
        for r, o in zip(rows, outs):
            y_ref[r, :] = o.astype(bf16)

    return pl.pallas_call(
        body, name=name, grid=(H, nc // CB),
        in_specs=[pl.BlockSpec((CB * CHUNK, HEAD), lambda h, c: (c, o0 + h)), full(1), full(2), vec, vec,
                  pl.BlockSpec((None, CHUNK, BAND * CHUNK), lambda h, c: (h, 0, 0))],
        out_specs=pl.BlockSpec((CB * CHUNK, HEAD), lambda h, c: (c, h)), out_shape=jax.ShapeDtypeStruct((T, H * HEAD), bf16),
        scratch_shapes=[pltpu.VMEM((T + PAD, HEAD), f32), pltpu.VMEM((T + PAD, HEAD), f32)],
        compiler_params=_params(("parallel", "arbitrary")),
    )(proj, proj, proj, q_gain, k_gain, bias)


def _attn_bwd(proj, q_gain, k_gain, bias, dy, off, n_heads, name):
    T = proj.shape[0]
    nc = T // CHUNK
    H = n_heads
    CB = _group(nc, ATTN_CHUNKS)
    o0 = off // HEAD
    full = lambda k: pl.BlockSpec((T, HEAD), lambda h, c: (0, o0 + k * H + h))
    full_out = pl.BlockSpec((T, HEAD), lambda h, c: (0, h))
    vec = pl.BlockSpec((1, HEAD), lambda h, c: (0, 0))
    chunk_out = pl.BlockSpec((CB * CHUNK, HEAD), lambda h, c: (c, h))
    bias_blk = pl.BlockSpec((None, CHUNK, BAND * CHUNK), lambda h, c: (h, 0, 0))

    def body(q_ref, k_ref, v_ref, qg_ref, kg_ref, b_ref, dy_ref, dq_ref, dk_ref, dv_ref, db_ref, dqg_ref, dkg_ref, kp, vp, dkp, dvp):
        h = pl.program_id(0)
        c = pl.program_id(1)

        @pl.when(c == 0)
        def _():
            kp[pl.ds(0, PAD), :] = jnp.zeros((PAD, HEAD), f32)
            vp[pl.ds(0, PAD), :] = jnp.zeros((PAD, HEAD), f32)
            kp[pl.ds(PAD, T), :] = _head_norm(k_ref[...], kg_ref[...])
            vp[pl.ds(PAD, T), :] = v_ref[...]
            dkp[...] = jnp.zeros((T + PAD, HEAD), f32)
            dvp[...] = jnp.zeros((T + PAD, HEAD), f32)
            db_ref[...] = jnp.zeros((CHUNK, BAND * CHUNK), f32)

        @pl.when(jnp.logical_and(h == 0, c == 0))
        def _():
            dqg_ref[...] = jnp.zeros((1, HEAD), f32)
            dkg_ref[...] = jnp.zeros((1, HEAD), f32)

        ns = [c * CB + j for j in range(CB)]
        rows = [pl.ds(j * CHUNK, CHUNK) for j in range(CB)]
        bands = [pl.ds(pl.multiple_of(n * CHUNK, CHUNK), BAND * CHUNK) for n in ns]
        _, vjp = jax.vjp(functools.partial(_attn_chunks, ns=ns), [q_ref[r, :] for r in rows], [kp[b, :] for b in bands],
                         [vp[b, :] for b in bands], qg_ref[...], b_ref[...])
        dqs, dkbs, dvbs, dqg, db = vjp([dy_ref[r, :] for r in rows])
        db_ref[...] += db
        dqg_ref[...] += dqg
        for r, b, dq, dkb, dvb in zip(rows, bands, dqs, dkbs, dvbs):
            dq_ref[r, :] = dq.astype(bf16)
            dkp[b, :] += dkb
            dvp[b, :] += dvb

        @pl.when(c == nc // CB - 1)
        def _():
            _, nvjp = jax.vjp(_head_norm, k_ref[...], kg_ref[...])
            dk, dkg = nvjp(dkp[pl.ds(PAD, T), :])
            dk_ref[...] = dk.astype(bf16)
            dv_ref[...] = dvp[pl.ds(PAD, T), :].astype(bf16)
            dkg_ref[...] += dkg

    act = jax.ShapeDtypeStruct((T, H * HEAD), bf16)
    gvec = jax.ShapeDtypeStruct((1, HEAD), f32)
    pad_buf = pltpu.VMEM((T + PAD, HEAD), f32)
    return pl.pallas_call(
        body, name=name, grid=(H, nc // CB),
        in_specs=[pl.BlockSpec((CB * CHUNK, HEAD), lambda h, c: (c, o0 + h)), full(1), full(2), vec, vec, bias_blk, chunk_out],
        out_specs=(chunk_out, full_out, full_out, bias_blk, vec, vec),
        out_shape=(act, act, act, jax.ShapeDtypeStruct((H, CHUNK, BAND * CHUNK), f32), gvec, gvec),
        scratch_shapes=[pad_buf, pad_buf, pad_buf, pad_buf],
        compiler_params=_params(("arbitrary", "arbitrary")),
    )(proj, proj, proj, q_gain, k_gain, bias, dy)


def _position():
    x, y, c = lax.axis_index("x"), lax.axis_index("y"), lax.axis_index("c")
    return x, y, c, 4 * x + 2 * y + c


def _flip(v, bit):
    return 1 - v if bit else v


def _chips(x, y):
    return [(1 - x, y), (x, 1 - y), (1 - x, 1 - y)]


def _seq_gather(shards, name, collective_id):
    n = len(shards)

    def body(*refs):
        ins, outs = refs[:n], refs[n:2 * n]
        send, recv, loc = refs[2 * n:]
        x, y, c, me = _position()
        sib = (x, y, 1 - c)
        chips = _chips(x, y)
        barrier = pltpu.get_barrier_semaphore()
        for peer in [sib] + [(px, py, c) for px, py in chips]:
            pl.semaphore_signal(barrier, inc=1, device_id=peer, device_id_type=MESH)
        pl.semaphore_wait(barrier, 4)

        def copy(w, k, src, blk, to):
            return pltpu.make_async_remote_copy(src_ref=src, dst_ref=outs[w].at[blk], send_sem=send.at[7 * w + k], recv_sem=recv.at[7 * w + k],
                                                device_id=to, device_id_type=MESH)

        mine = [pltpu.make_async_copy(ins[w], outs[w].at[me], loc.at[w]) for w in range(n)]
        for cp in mine:
            cp.start()
        first = []
        for j, (px, py) in enumerate(chips):
            first += [copy(w, 1 + j, ins[w], me, (px, py, c)) for w in range(n)]
        first += [copy(w, 0, ins[w], me, sib) for w in range(n)]
        for cp in first:
            cp.start()
        passed = []
        for j, (px, py) in enumerate(chips):
            blk = 4 * px + 2 * py + c
            for w in range(n):
                copy(w, 1 + j, ins[w], blk, sib).wait_recv()
                fwd = copy(w, 4 + j, outs[w].at[blk], blk, sib)
                fwd.start()
                passed.append(fwd)
        for w in range(n):
            copy(w, 0, ins[w], 4 * x + 2 * y + (1 - c), sib).wait_recv()
        for j, (px, py) in enumerate(chips):
            for w in range(n):
                copy(w, 4 + j, ins[w], 4 * px + 2 * py + (1 - c), sib).wait_recv()
        for cp in first + passed:
            cp.wait_send()
        for cp in mine:
            cp.wait()

    return pl.kernel(
        body, out_type=tuple(jax.ShapeDtypeStruct((NDEV,) + s.shape, s.dtype) for s in shards),
        mesh=plsc.ScalarSubcoreMesh(axis_name="sequencer", num_cores=1), name=name,
        scratch_types=(pltpu.SemaphoreType.DMA((7 * n,)), pltpu.SemaphoreType.DMA((7 * n,)), pltpu.SemaphoreType.DMA((n,))),
        compiler_params=pltpu.CompilerParams(collective_id=collective_id),
    )(*shards)


NCHIP = 4


def _seq_pair_exchange(grads, name, collective_id, after=()):
    n, na = len(grads), len(after)

    def body(*refs):
        ins, outs = refs[:n], refs[n + na:2 * n + na]
        send, recv = refs[2 * n + na:]
        x, y, c, me = _position()
        sib = (x, y, 1 - c)
        barrier = pltpu.get_barrier_semaphore()
        pl.semaphore_signal(barrier, inc=1, device_id=sib, device_id_type=MESH)
        pl.semaphore_wait(barrier, 1)
        copies = [pltpu.make_async_remote_copy(src_ref=ins[w].at[2 * k + (1 - c)], dst_ref=outs[w].at[k], send_sem=send.at[NCHIP * w + k],
                                               recv_sem=recv.at[NCHIP * w + k], device_id=sib, device_id_type=MESH)
                  for w in range(n) for k in range(NCHIP)]
        for cp in copies:
            cp.start()
        for cp in copies:
            cp.wait_recv()
        for cp in copies:
            cp.wait_send()

    return pl.kernel(
        body, out_type=tuple(jax.ShapeDtypeStruct((NCHIP,) + g.shape[1:], g.dtype) for g in grads),
        mesh=plsc.ScalarSubcoreMesh(axis_name="sequencer", num_cores=1), name=name,
        scratch_types=(pltpu.SemaphoreType.DMA((NCHIP * n,)), pltpu.SemaphoreType.DMA((NCHIP * n,))),
        compiler_params=pltpu.CompilerParams(collective_id=collective_id),
    )(*grads, *after)


def _pair_add(grad, sib_part, name):
    _, R, C = grad.shape
    tr = _tile(R, 256, 16)
    core = jnp.reshape(lax.axis_index("c"), (1,)).astype(jnp.int32)

    def body(c_ref, g_ref, s_ref, o_ref):
        o_ref[...] = (g_ref[...].astype(f32) + s_ref[...].astype(f32)).astype(bf16)

    blk = pl.BlockSpec((None, tr, C), lambda k, i, c_ref: (k, i, 0))
    return pl.pallas_call(
        body, name=name,
        grid_spec=pltpu.PrefetchScalarGridSpec(
            num_scalar_prefetch=1, grid=(NCHIP, R // tr),
            in_specs=[pl.BlockSpec((None, tr, C), lambda k, i, c_ref: (2 * k + c_ref[0], i, 0)), blk], out_specs=blk),
        out_shape=jax.ShapeDtypeStruct((NCHIP, R, C), bf16), compiler_params=_params(("parallel", "parallel")),
    )(core, grad, sib_part)


def _seq_chip_exchange(sums, name, collective_id, after=()):
    n, na = len(sums), len(after)

    def body(*refs):
        ins, outs = refs[:n], refs[n + na:2 * n + na]
        send, recv, loc = refs[2 * n + na:]
        x, y, c, me = _position()
        chips = _chips(x, y)
        mine = 2 * x + y
        barrier = pltpu.get_barrier_semaphore()
        for px, py in chips:
            pl.semaphore_signal(barrier, inc=1, device_id=(px, py, c), device_id_type=MESH)
        pl.semaphore_wait(barrier, 3)
        local = [pltpu.make_async_copy(ins[w].at[mine], outs[w].at[mine], loc.at[w]) for w in range(n)]
        for cp in local:
            cp.start()
        sends, waits = [], []
        for j, (px, py) in enumerate(chips):
            for w in range(n):
                sems = dict(send_sem=send.at[3 * w + j], recv_sem=recv.at[3 * w + j], device_id=(px, py, c), device_id_type=MESH)
                sends.append(pltpu.make_async_remote_copy(src_ref=ins[w].at[2 * px + py], dst_ref=outs[w].at[mine], **sems))
                waits.append(pltpu.make_async_remote_copy(src_ref=ins[w].at[2 * px + py], dst_ref=outs[w].at[2 * px + py], **sems))
        for cp in sends:
            cp.start()
        for cp in waits:
            cp.wait_recv()
        for cp in sends:
            cp.wait_send()
        for cp in local:
            cp.wait()

    return pl.kernel(
        body, out_type=tuple(jax.ShapeDtypeStruct(s.shape, s.dtype) for s in sums),
        mesh=plsc.ScalarSubcoreMesh(axis_name="sequencer", num_cores=1), name=name,
        scratch_types=(pltpu.SemaphoreType.DMA((3 * n,)), pltpu.SemaphoreType.DMA((3 * n,)), pltpu.SemaphoreType.DMA((n,))),
        compiler_params=pltpu.CompilerParams(collective_id=collective_id),
    )(*sums, *after)


def _reduce_scatter(grads, tag, ids, after=()):
    sib_parts = _seq_pair_exchange(grads, "pair_exchange_" + tag, ids[0], after=after)
    sums = [_pair_add(g, s, "pair_add_%s%d" % (tag, i)) for i, (g, s) in enumerate(zip(grads, sib_parts))]
    return _seq_chip_exchange(sums, "chip_exchange_" + tag, ids[1])


def _small_all_reduce(v, name):
    R, C = v.shape

    def body(v_ref, o_ref, buf, send, recv):
        x, y, c, me = _position()
        buf[me] = v_ref[...]
        sends, waits = [], []
        for r in range(1, NDEV):
            px, py, pc = _flip(x, r & 4), _flip(y, r & 2), _flip(c, r & 1)
            peer = 4 * px + 2 * py + pc
            sends.append(pltpu.make_async_remote_copy(src_ref=v_ref, dst_ref=buf.at[me], send_sem=send.at[r - 1], recv_sem=recv.at[r - 1],
                                                      device_id=(px, py, pc), device_id_type=MESH))
            waits.append(pltpu.make_async_remote_copy(src_ref=v_ref, dst_ref=buf.at[peer], send_sem=send.at[r - 1], recv_sem=recv.at[r - 1],
                                                      device_id=(px, py, pc), device_id_type=MESH))
        for cp in sends:
            cp.start()
        for cp in waits:
            cp.wait_recv()
        for cp in sends:
            cp.wait_send()
        acc = buf[0]
        for i in range(1, NDEV):
            acc = acc + buf[i]
        o_ref[...] = acc

    vm = pl.BlockSpec(memory_space=pltpu.VMEM)
    return pl.pallas_call(
        body, name=name, in_specs=[vm], out_specs=vm, out_shape=jax.ShapeDtypeStruct((R, C), f32),
        scratch_shapes=[pltpu.VMEM((NDEV, R, C), f32), pltpu.SemaphoreType.DMA((7,)), pltpu.SemaphoreType.DMA((7,))],
    )(v)


def _adamw_math(w, g, m, v):
    m = ADAM_B1 * m + (1.0 - ADAM_B1) * g
    v = ADAM_B2 * v + (1.0 - ADAM_B2) * (g * g)
    m_hat = m / (1.0 - ADAM_B1 ** ADAM_STEP)
    v_hat = v / (1.0 - ADAM_B2 ** ADAM_STEP)
    delta = -ADAM_LR * (m_hat / (jnp.sqrt(v_hat) + ADAM_EPS) + ADAM_WD * w)
    return delta, m, v


def _adamw_parts(w, m, v, parts, name, after=()):
    R, C = w.shape
    tr = _tile(R, 128, 16)
    blk = pl.BlockSpec((tr, C), lambda i: (i, 0))

    def body(w_ref, m_ref, v_ref, p_ref, *rest):
        g_ref, d_ref, mo_ref, vo_ref = rest[len(after):]
        g = p_ref[0].astype(f32)
        for i in range(1, NCHIP):
            g = g + p_ref[i].astype(f32)
        d, mn, vn = _adamw_math(w_ref[...], g, m_ref[...], v_ref[...])
        g_ref[...] = g
        d_ref[...] = d
        mo_ref[...] = mn
        vo_ref[...] = vn

    shp = jax.ShapeDtypeStruct((R, C), f32)
    return pl.pallas_call(
        body, name=name, grid=(R // tr,),
        in_specs=[blk, blk, blk, pl.BlockSpec((NCHIP, tr, C), lambda i: (0, i, 0))] + [pl.BlockSpec(a.shape, lambda i: (0, 0)) for a in after],
        out_specs=(blk, blk, blk, blk), out_shape=(shp, shp, shp, shp), compiler_params=_params(("parallel",)),
    )(w, m, v, parts, *after)


def _adamw_small(w, g, m, v, name):
    def body(w_ref, g_ref, m_ref, v_ref, d_ref, mo_ref, vo_ref):
        d, mn, vn = _adamw_math(w_ref[...], g_ref[...], m_ref[...], v_ref[...])
        d_ref[...] = d
        mo_ref[...] = mn
        vo_ref[...] = vn

    shp = jax.ShapeDtypeStruct(w.shape, f32)
    return pl.pallas_call(body, name=name, out_shape=(shp, shp, shp))(w, g, m, v)


SMALL_COLS = 1024


def _pack(arrs):
    flat = jnp.concatenate([a.reshape(-1) for a in arrs])
    rows = -(-flat.shape[0] // (8 * SMALL_COLS)) * 8
    return jnp.pad(flat, (0, rows * SMALL_COLS - flat.shape[0])).reshape(rows, SMALL_COLS)


def _unpack(packed, like):
    flat = packed.reshape(-1)
    out, pos = [], 0
    for a in like:
        out.append(flat[pos:pos + a.size].reshape(a.shape))
        pos += a.size
    return out


def kernel(x, w_in, b_gate, norm_mix, norm_ffn, hgrn_lb_logits, hgrn_out_gain, q_gain, k_gain, rel_bias, w_proj_a, w_proj_b, w_out, w_ffn_in, w_ffn_out, loss_target, m_w_in, m_b_gate, m_norm_mix, m_norm_ffn, m_hgrn_lb_logits, m_hgrn_out_gain, m_q_gain, m_k_gain, m_rel_bias, m_w_proj_a, m_w_proj_b, m_w_out, m_w_ffn_in, m_w_ffn_out, v_w_in, v_b_gate, v_norm_mix, v_norm_ffn, v_hgrn_lb_logits, v_hgrn_out_gain, v_q_gain, v_k_gain, v_rel_bias, v_w_proj_a, v_w_proj_b, v_w_out, v_w_ffn_in, v_w_ffn_out):
    xs = x[0]
    target = loss_target[0]
    T, D = xs.shape
    d_a = hgrn_out_gain.shape[-1]
    H = d_a // HEAD
    d_b = d_a
    off_b = 4 * d_a
    off_g = off_b + 3 * d_b
    assert rel_bias.shape[1] == H and T % CHUNK == 0 and T // CHUNK > N_PAST

    big_w = [w_in[0], w_proj_a[0], w_proj_b[0], w_out[0], w_ffn_in[0], w_ffn_out[0]]
    big_m = [m_w_in[0], m_w_proj_a[0], m_w_proj_b[0], m_w_out[0], m_w_ffn_in[0], m_w_ffn_out[0]]
    big_v = [v_w_in[0], v_w_proj_a[0], v_w_proj_b[0], v_w_out[0], v_w_ffn_in[0], v_w_ffn_out[0]]

    sh = [w.astype(bf16) for w in big_w]
    (g_in,) = _seq_gather(sh[0:1], "gather_a", 1)
    g_pa, g_pb, g_out = _seq_gather(sh[1:4], "gather_b", 2)
    (g_fin,) = _seq_gather(sh[4:5], "gather_c", 3)
    (g_fout,) = _seq_gather(sh[5:6], "gather_d", 4)

    h = _rms_fwd(xs, None, norm_mix, "rms_mix")
    proj = _mm(h, g_in, mode="nn", b_blocked=True, name="mm_proj")
    y_a, states = _hgrn_fwd(proj, hgrn_lb_logits, hgrn_out_gain, H, "hgrn_fwd")
    idx = jnp.asarray(_rel_index())
    rb_pad = jnp.pad(rel_bias[0], ((0, 0), (0, N_REL_PAD - N_REL)))
    bias = _bias_table(rb_pad, idx, "bias_table").reshape(H, CHUNK, BAND * CHUNK)
    y_b = _attn_fwd(proj, q_gain, k_gain, bias, off_b, H, "attn_fwd")
    wg_out = g_out.reshape(-1, g_out.shape[-1])
    wg_fout = g_fout.reshape(-1, g_fout.shape[-1])
    pa = _mm(y_a, g_pa, mode="nn", b_blocked=True, tm=2048, name="mm_proj_a")
    pb = _mm(y_b, g_pb, mode="nn", b_blocked=True, tm=2048, name="mm_proj_b")
    merged = _merge_fwd(pa, pb, proj, b_gate, off_g, "merge_fwd")
    mo = _mm(merged, wg_out, mode="nn", name="mm_out")
    x1, h2 = _rms_fwd(xs, mo, norm_ffn, "rms_ffn")
    gu = _mm(h2, g_fin, mode="nn", b_blocked=True, name="mm_ffn_in")
    act = _swiglu_fwd(gu, "swiglu_fwd")
    fo = _mm(act, wg_fout, mode="nn", tk=2816, name="mm_ffn_out")
    dy, loss_acc = _loss_head(x1, fo, target, "loss_head")
    loss_part = loss_acc[0:1, 0:1] * (0.5 / D)

    dact = _mm(dy, wg_fout, mode="nt", tn=1408, name="mm_d_act")
    gw_fout = _mm(act, dy, mode="tn", out_dtype=bf16, tm=1408, name="mm_gw_ffn_out")
    dgate, dup = _swiglu_bwd(gu, dact, "swiglu_bwd")
    dgu = jnp.concatenate([dgate, dup], axis=1)
    gw_fin = _mm(h2, dgu, mode="tn", out_blocked=True, out_dtype=bf16, tn=g_fin.shape[-1], name="mm_gw_ffn_in")
    dh2 = _mm(dgu, g_fin, mode="nt", b_blocked=True, name="mm_d_h2")
    dx1, g_norm_ffn = _rms_bwd(x1, norm_ffn, dh2, dy, "rms_ffn_bwd")

    dmerged = _mm(dx1, wg_out, mode="nt", name="mm_d_merged")
    gw_out = _mm(merged, dx1, mode="tn", out_dtype=bf16, name="mm_gw_out")
    dpa, dgl_a, gb_a = _branch_bwd(dmerged, pa, proj, b_gate, off_g, 0, "branch_a_bwd")
    dpb, dgl_b, gb_b = _branch_bwd(dmerged, pb, proj, b_gate, off_g + D, D, "branch_b_bwd")
    dy_a = _mm(dpa, g_pa, mode="nt", b_blocked=True, tm=2048, name="mm_d_ya")
    dy_b = _mm(dpb, g_pb, mode="nt", b_blocked=True, tm=2048, name="mm_d_yb")
    gw_pa = _mm(y_a, dpa, mode="tn", out_blocked=True, out_dtype=bf16, tn=g_pa.shape[-1], name="mm_gw_proj_a")
    gw_pb = _mm(y_b, dpb, mode="tn", out_blocked=True, out_dtype=bf16, tn=g_pb.shape[-1], name="mm_gw_proj_b")

    dq_a, df_a, di_a, dg_a, g_logits, g_gain = _hgrn_bwd(proj, hgrn_lb_logits, hgrn_out_gain, states, dy_a, H, "hgrn_bwd")
    dq_b, dk_b, dv_b, dbias, g_qg, g_kg = _attn_bwd(proj, q_gain, k_gain, bias, dy_b, off_b, H, "attn_bwd")
    g_rel = _bias_table_bwd(dbias.reshape(H, -1), idx, "bias_table_bwd")[:, :N_REL]
    dproj = jnp.concatenate([dq_a, df_a, di_a, dg_a, dq_b, dk_b, dv_b, dgl_a, dgl_b], axis=1)
    gw_in = _mm(h, dproj, mode="tn", out_blocked=True, out_dtype=bf16, tn=g_in.shape[-1], name="mm_gw_in")
    dh = _mm(dproj, g_in, mode="nt", b_blocked=True, name="mm_d_h")
    grad_x, g_norm_mix = _rms_bwd(xs, norm_mix, dh, dx1, "rms_mix_bwd")

    p_fout, p_fin = _reduce_scatter([gw_fout.reshape(NDEV, -1, D), gw_fin], "a", (5, 6), after=(g_norm_ffn,))
    p_out, p_pa, p_pb = _reduce_scatter([gw_out.reshape(NDEV, -1, D), gw_pa, gw_pb], "b", (7, 8), after=(g_gain,))
    (p_in,) = _reduce_scatter([gw_in], "c", (9, 10))
    parts = [p_in, p_pa, p_pb, p_out, p_fin, p_fout]
    names = ["w_in", "w_proj_a", "w_proj_b", "w_out", "w_ffn_in", "w_ffn_out"]
    big = {}
    for nm, w, m, v, p in zip(names, big_w, big_m, big_v, parts):
        big[nm] = [o[None] for o in _adamw_parts(w, m, v, p, "adamw_" + nm, after=() if nm == "w_in" else (g_norm_mix,))]

    small_names = ["b_gate", "norm_mix", "norm_ffn", "hgrn_lb_logits", "hgrn_out_gain", "q_gain", "k_gain", "rel_bias"]
    small_w = [b_gate, norm_mix, norm_ffn, hgrn_lb_logits, hgrn_out_gain, q_gain, k_gain, rel_bias]
    small_m = [m_b_gate, m_norm_mix, m_norm_ffn, m_hgrn_lb_logits, m_hgrn_out_gain, m_q_gain, m_k_gain, m_rel_bias]
    small_v = [v_b_gate, v_norm_mix, v_norm_ffn, v_hgrn_lb_logits, v_hgrn_out_gain, v_q_gain, v_k_gain, v_rel_bias]
    small_g = [jnp.concatenate([gb_a, gb_b], axis=1), g_norm_mix, g_norm_ffn, g_logits, g_gain, g_qg, g_kg, g_rel[None], loss_part]
    g_sum = _small_all_reduce(_pack(small_g), "reduce_small")
    loss = _unpack(g_sum, small_g)[-1].reshape(())
    d_s, m_s, v_s = _adamw_small(_pack(small_w), g_sum, _pack(small_m), _pack(small_v), "adamw_small")
    small = {}
    for nm, g, d, m, v in zip(small_names, _unpack(g_sum, small_w), _unpack(d_s, small_w), _unpack(m_s, small_w), _unpack(v_s, small_w)):
        small[nm] = [g, d, m, v]

    order = ["w_in", "b_gate", "norm_mix", "norm_ffn", "hgrn_lb_logits", "hgrn_out_gain", "q_gain", "k_gain", "rel_bias",
             "w_proj_a", "w_proj_b", "w_out", "w_ffn_in", "w_ffn_out"]
    res = {**big, **small}
    outs = [loss, grad_x[None]]
    for k in range(4):
        outs += [res[nm][k] for nm in order]
    return tuple(outs)
```

```python
import functools

import numpy as np
import jax
import jax.numpy as jnp
from jax import lax
from jax.experimental import pallas as pl
from jax.experimental.pallas import tpu as pltpu
from jax.experimental.pallas import tpu_sc as plsc

f32 = jnp.float32
bf16 = jnp.bfloat16
HI = lax.Precision.HIGHEST
MESH = pl.DeviceIdType.MESH
AXES = ("x", "y", "c")
NDEV = 8

CHUNK = 64
HEAD = 128
SUB = 8
HGRN_BWD_HEADS = 8
ATTN_CHUNKS = 8
N_PAST = 8
BAND = N_PAST + 1
PAD = N_PAST * CHUNK
REL_FUTURE = CHUNK - 1
REL_PAST = 2 * CHUNK - 1
N_REL = REL_FUTURE + REL_PAST + 1
N_REL_PAD = 256
EPS = 1e-6
NEG = -1e30

ADAM_LR = 0.001
ADAM_B1 = 0.9
ADAM_B2 = 0.999
ADAM_EPS = 1e-08
ADAM_WD = 0.01
ADAM_STEP = 10

VMEM_LIMIT = 56 * 1024 * 1024


def _params(sem=None):
    return pltpu.CompilerParams(dimension_semantics=sem, vmem_limit_bytes=VMEM_LIMIT)


def _tile(n, pref, unit=128):
    if n <= pref:
        return n
    t = (pref // unit) * unit
    while t >= unit:
        if n % t == 0:
            return t
        t -= unit
    return n


_sigmoid = jax.nn.sigmoid


def _mm(a, b, *, mode, name, b_blocked=False, out_blocked=False, out_dtype=f32, tm=1024, tn=1024, tk=2048, after=()):
    if mode == "tn":
        K, M = a.shape
    else:
        M, K = a.shape
    if b_blocked:
        nb, mid, cb = b.shape
        if mode == "nn":
            assert mid == K
            N, tn = nb * cb, cb
        else:
            assert mode == "nt" and nb * cb == K
            N, tk = mid, cb
    else:
        N = b.shape[1] if mode in ("nn", "tn") else b.shape[0]
    tm = _tile(M, tm)
    tn = tn if (b_blocked and mode == "nn") or out_blocked else _tile(N, tn)
    tk = tk if b_blocked and mode == "nt" else _tile(K, tk)
    assert M % tm == 0 and N % tn == 0 and K % tk == 0
    nk = K // tk
    grid = (M // tm, N // tn, nk)
    if mode == "tn":
        a_spec = pl.BlockSpec((tk, tm), lambda i, j, k: (k, i))
    else:
        a_spec = pl.BlockSpec((tm, tk), lambda i, j, k: (i, k))
    if mode == "nn":
        b_spec = pl.BlockSpec((None, tk, cb), lambda i, j, k: (j, k, 0)) if b_blocked else pl.BlockSpec((tk, tn), lambda i, j, k: (k, j))
    elif mode == "nt":
        b_spec = pl.BlockSpec((None, tn, cb), lambda i, j, k: (k, j, 0)) if b_blocked else pl.BlockSpec((tn, tk), lambda i, j, k: (j, k))
    else:
        b_spec = pl.BlockSpec((tk, tn), lambda i, j, k: (k, j))
    if out_blocked:
        out_shape = jax.ShapeDtypeStruct((N // tn, M, tn), out_dtype)
        o_spec = pl.BlockSpec((None, tm, tn), lambda i, j, k: (j, i, 0))
    else:
        out_shape = jax.ShapeDtypeStruct((M, N), out_dtype)
        o_spec = pl.BlockSpec((tm, tn), lambda i, j, k: (i, j))
    dims = {"nn": ((1,), (0,)), "nt": ((1,), (1,)), "tn": ((0,), (0,))}[mode]

    def body(a_ref, b_ref, *rest):
        o_ref, acc = rest[len(after)], rest[len(after) + 1:]
        p = lax.dot_general(a_ref[...].astype(bf16), b_ref[...].astype(bf16), (dims, ((), ())), preferred_element_type=f32)
        if nk == 1:
            o_ref[...] = p.astype(out_dtype)
        else:
            acc_ref = acc[0]
            k = pl.program_id(2)

            @pl.when(k == 0)
            def _():
                acc_ref[...] = p

            @pl.when(k > 0)
            def _():
                acc_ref[...] += p

            @pl.when(k == nk - 1)
            def _():
                o_ref[...] = acc_ref[...].astype(out_dtype)

    return pl.pallas_call(
        body, name=name, grid=grid, in_specs=[a_spec, b_spec] + [pl.BlockSpec(memory_space=pl.ANY)] * len(after), out_specs=o_spec,
        out_shape=out_shape, scratch_shapes=[pltpu.VMEM((tm, tn), f32)] if nk > 1 else [],
        compiler_params=_params(("parallel", "parallel", "arbitrary")),
    )(a, b, *after)


def _rms_fwd(x, res, gain, name):
    T, D = x.shape
    tr = _tile(T, 256, 8)
    row = pl.BlockSpec((tr, D), lambda i: (i, 0))
    vec = pl.BlockSpec((1, D), lambda i: (0, 0))

    def body(*refs):
        if res is None:
            x_ref, g_ref, h_ref = refs
            xs = x_ref[...]
        else:
            x_ref, r_ref, g_ref, xs_ref, h_ref = refs
            xs = x_ref[...] + r_ref[...]
            xs_ref[...] = xs
        r = lax.rsqrt(jnp.mean(xs * xs, axis=-1, keepdims=True) + EPS)
        h_ref[...] = (xs * r * g_ref[...]).astype(bf16)

    h_shape = jax.ShapeDtypeStruct((T, D), bf16)
    if res is None:
        return pl.pallas_call(body, name=name, grid=(T // tr,), in_specs=[row, vec], out_specs=row, out_shape=h_shape,
                              compiler_params=_params(("parallel",)))(x, gain)
    return pl.pallas_call(body, name=name, grid=(T // tr,), in_specs=[row, row, vec], out_specs=(row, row),
                          out_shape=(jax.ShapeDtypeStruct((T, D), f32), h_shape), compiler_params=_params(("parallel",)))(x, res, gain)


def _rms_bwd(xs, gain, dh, extra, name, after=()):
    T, D = xs.shape
    tr = _tile(T, 256, 8)
    row = pl.BlockSpec((tr, D), lambda i: (i, 0))
    vec = pl.BlockSpec((1, D), lambda i: (0, 0))

    def body(x_ref, g_ref, dh_ref, e_ref, *rest):
        dx_ref, dg_ref = rest[len(after):]
        x = x_ref[...]
        r = lax.rsqrt(jnp.mean(x * x, axis=-1, keepdims=True) + EPS)
        xhat = x * r
        dh_v = dh_ref[...]
        gd = dh_v * g_ref[...]
        dx_ref[...] = e_ref[...] + r * (gd - xhat * jnp.mean(gd * xhat, axis=-1, keepdims=True))
        part = jnp.sum(dh_v * xhat, axis=0, keepdims=True)

        @pl.when(pl.program_id(0) == 0)
        def _():
            dg_ref[...] = part

        @pl.when(pl.program_id(0) > 0)
        def _():
            dg_ref[...] += part

    return pl.pallas_call(body, name=name, grid=(T // tr,),
                          in_specs=[row, vec, row, row] + [pl.BlockSpec(memory_space=pl.ANY)] * len(after), out_specs=(row, vec),
                          out_shape=(jax.ShapeDtypeStruct((T, D), f32), jax.ShapeDtypeStruct((1, D), f32)),
                          compiler_params=_params(("arbitrary",)))(xs, gain, dh, extra, *after)


def _merge_fwd(pa, pb, proj, b_gate, off, name):
    T, D = pa.shape
    tr, tc = _tile(T, 512, 8), _tile(D, 512)
    oa, ob, nb = off // tc, (off + D) // tc, D // tc
    blk = pl.BlockSpec((tr, tc), lambda i, j: (i, j))

    def body(pa_ref, pb_ref, ga_ref, gb_ref, ba_ref, bb_ref, o_ref):
        ga = _sigmoid(ga_ref[...] + ba_ref[...])
        gb = _sigmoid(gb_ref[...] + bb_ref[...])
        o_ref[...] = (ga * pa_ref[...] + gb * pb_ref[...]).astype(bf16)

    return pl.pallas_call(
        body, name=name, grid=(T // tr, nb),
        in_specs=[blk, blk, pl.BlockSpec((tr, tc), lambda i, j: (i, oa + j)), pl.BlockSpec((tr, tc), lambda i, j: (i, ob + j)),
                  pl.BlockSpec((1, tc), lambda i, j: (0, j)), pl.BlockSpec((1, tc), lambda i, j: (0, nb + j))],
        out_specs=blk, out_shape=jax.ShapeDtypeStruct((T, D), bf16), compiler_params=_params(("parallel", "parallel")),
    )(pa, pb, proj, proj, b_gate, b_gate)


def _branch_bwd(dm, p, proj, b_gate, off, boff, name):
    T, D = p.shape
    tr, tc = _tile(T, 512, 8), _tile(D, 512)
    og, obias = off // tc, boff // tc
    blk = pl.BlockSpec((tr, tc), lambda j, i: (i, j))
    vec = pl.BlockSpec((1, tc), lambda j, i: (0, j))

    def body(dm_ref, p_ref, gl_ref, b_ref, dp_ref, dgl_ref, db_ref):
        g = _sigmoid(gl_ref[...] + b_ref[...])
        dm_v = dm_ref[...]
        dp_ref[...] = (dm_v * g).astype(bf16)
        dgl = dm_v * p_ref[...] * g * (1.0 - g)
        dgl_ref[...] = dgl.astype(bf16)
        part = jnp.sum(dgl, axis=0, keepdims=True)

        @pl.when(pl.program_id(1) == 0)
        def _():
            db_ref[...] = part

        @pl.when(pl.program_id(1) > 0)
        def _():
            db_ref[...] += part

    return pl.pallas_call(
        body, name=name, grid=(D // tc, T // tr),
        in_specs=[blk, blk, pl.BlockSpec((tr, tc), lambda j, i: (i, og + j)), pl.BlockSpec((1, tc), lambda j, i: (0, obias + j))],
        out_specs=(blk, blk, vec),
        out_shape=(jax.ShapeDtypeStruct((T, D), bf16), jax.ShapeDtypeStruct((T, D), bf16), jax.ShapeDtypeStruct((1, D), f32)),
        compiler_params=_params(("parallel", "arbitrary")),
    )(dm, p, proj, b_gate)


def _swiglu_fwd(gu, name):
    T, F2 = gu.shape
    F = F2 // 2
    tr, tc = _tile(T, 512, 8), _tile(F, 512)
    nb = F // tc

    def body(g_ref, u_ref, o_ref):
        g = g_ref[...]
        o_ref[...] = (g * _sigmoid(g) * u_ref[...]).astype(bf16)

    return pl.pallas_call(
        body, name=name, grid=(T // tr, nb),
        in_specs=[pl.BlockSpec((tr, tc), lambda i, j: (i, j)), pl.BlockSpec((tr, tc), lambda i, j: (i, nb + j))],
        out_specs=pl.BlockSpec((tr, tc), lambda i, j: (i, j)), out_shape=jax.ShapeDtypeStruct((T, F), bf16),
        compiler_params=_params(("parallel", "parallel")),
    )(gu, gu)


def _swiglu_bwd(gu, dact, name):
    T, F2 = gu.shape
    F = F2 // 2
    tr, tc = _tile(T, 512, 8), _tile(F, 512)
    nb = F // tc
    blk = pl.BlockSpec((tr, tc), lambda i, j: (i, j))

    def body(g_ref, u_ref, d_ref, dg_ref, du_ref):
        g = g_ref[...]
        s = _sigmoid(g)
        d = d_ref[...]
        dg_ref[...] = (d * u_ref[...] * s * (1.0 + g * (1.0 - s))).astype(bf16)
        du_ref[...] = (d * g * s).astype(bf16)

    return pl.pallas_call(
        body, name=name, grid=(T // tr, nb),
        in_specs=[blk, pl.BlockSpec((tr, tc), lambda i, j: (i, nb + j)), blk], out_specs=(blk, blk),
        out_shape=(jax.ShapeDtypeStruct((T, F), bf16), jax.ShapeDtypeStruct((T, F), bf16)),
        compiler_params=_params(("parallel", "parallel")),
    )(gu, gu, dact)


def _loss_head(x1, fo, target, name):
    T, D = x1.shape
    tr = _tile(T, 256, 8)
    row = pl.BlockSpec((tr, D), lambda i: (i, 0))
    acc = pl.BlockSpec((8, 128), lambda i: (0, 0))

    def body(x_ref, f_ref, t_ref, dy_ref, l_ref):
        d = x_ref[...] + f_ref[...] - t_ref[...]
        dy_ref[...] = d * (1.0 / D)
        part = jnp.sum(jnp.sum(d * d, axis=1, keepdims=True), axis=0, keepdims=True)

        @pl.when(pl.program_id(0) == 0)
        def _():
            l_ref[...] = jnp.zeros((8, 128), f32)

        l_ref[...] += part

    return pl.pallas_call(body, name=name, grid=(T // tr,), in_specs=[row, row, row], out_specs=(row, acc),
                          out_shape=(jax.ShapeDtypeStruct((T, D), f32), jax.ShapeDtypeStruct((8, 128), f32)),
                          compiler_params=_params(("arbitrary",)))(x1, fo, target)


_DIMS = {"nn": ((1,), (0,)), "nt": ((1,), (1,)), "tn": ((0,), (0,))}
_MODE = {v: k for k, v in _DIMS.items()}


def _dot_bf16(a, b, mode):
    return lax.dot_general(a.astype(bf16), b.astype(bf16), (_DIMS[mode], ((), ())), preferred_element_type=f32)


@functools.partial(jax.custom_vjp, nondiff_argnums=(2,))
def _dotm(a, b, mode):
    return _dot_bf16(a, b, mode)


def _dotm_fwd(a, b, mode):
    return _dot_bf16(a, b, mode), (a, b)


def _dotm_bwd(mode, res, g):
    a, b = res
    if mode == "nn":
        return _dot_bf16(g, b, "nt"), _dot_bf16(a, g, "tn")
    if mode == "nt":
        return _dot_bf16(g, b, "nn"), _dot_bf16(g, a, "tn")
    return _dot_bf16(b, g, "nt"), _dot_bf16(a, g, "nn")


_dotm.defvjp(_dotm_fwd, _dotm_bwd)


def _dotb(a, b, dims):
    return _dotm(a, b, _MODE[dims])


def _triangle_sum(v, lower):
    row = lax.broadcasted_iota(jnp.int32, (CHUNK, CHUNK), 0)
    col = lax.broadcasted_iota(jnp.int32, (CHUNK, CHUNK), 1)
    tri = ((col <= row) if lower else (col >= row)).astype(bf16)

    def top(t):
        return lax.bitcast_convert_type(lax.bitcast_convert_type(t, jnp.uint32) & jnp.uint32(0xFFFF0000), f32)

    hi = top(v)
    mid = top(v - hi)
    low = (v - hi) - mid
    hi, mid, low = hi.astype(bf16), mid.astype(bf16), low.astype(bf16)
    dn = (((1,), (0,)), ((), ()))
    return (lax.dot_general(tri, hi, dn, preferred_element_type=f32) + lax.dot_general(tri, mid, dn, preferred_element_type=f32)
            + lax.dot_general(tri, low, dn, preferred_element_type=f32))


@jax.custom_vjp
def _cumsum_rows(v):
    return _triangle_sum(v, True)


_cumsum_rows.defvjp(lambda v: (_triangle_sum(v, True), None), lambda _, g: (_triangle_sum(g, False),))


def _hgrn_heads(q, fl, iv, g, logits, gain, st):
    r = range(len(q))
    lb = [jax.nn.softmax(logits[j], axis=0)[0:1] for j in r]
    f = [lb[j] + (1.0 - lb[j]) * _sigmoid(fl[j]) for j in r]
    lf = [jnp.log(f[j]) for j in r]
    kk = [1.0 - f[j] for j in r]
    qs = [q[j] * _sigmoid(q[j]) for j in r]
    b = [_cumsum_rows(lf[j]) for j in r]
    b_last = [jnp.sum(lf[j], axis=0, keepdims=True) for j in r]
    o = [_dotb(qs[j] * jnp.exp(b[j]), st[j], ((1,), (1,))) for j in r]
    r3 = lax.broadcasted_iota(jnp.int32, (SUB, SUB, HEAD), 0)
    c3 = lax.broadcasted_iota(jnp.int32, (SUB, SUB, HEAD), 1)
    parts = [[] for _ in r]
    for i in range(CHUNK // SUB):
        lo, hi = i * SUB, (i + 1) * SUB
        bi = [b[j][lo:hi] for j in r]
        dec = [jnp.exp(jnp.where(c3 <= r3, bi[j][:, None, :] - bi[j][None, :, :], -jnp.inf)) for j in r]
        s = [jnp.sum(qs[j][lo:hi][:, None, :] * kk[j][lo:hi][None, :, :] * dec[j], axis=-1) for j in r]
        if i > 0:
            anchor = [jnp.max(bi[j], axis=0, keepdims=True) for j in r]
            qa = [qs[j][lo:hi] * jnp.exp(bi[j] - anchor[j]) for j in r]
            kd = [kk[j][:lo] * jnp.exp(anchor[j] - b[j][:lo]) for j in r]
            s = [jnp.concatenate([_dotb(qa[j], kd[j], ((1,), (1,))), s[j]], axis=1) for j in r]
        for j in r:
            parts[j].append(_dotb(s[j], iv[j][:hi], ((1,), (0,))))
    o = [o[j] + jnp.concatenate(parts[j], axis=0) for j in r]
    st_new = [st[j] * jnp.exp(b_last[j]) + _dotb(iv[j], kk[j] * jnp.exp(b_last[j] - b[j]), ((0,), (0,))) for j in r]
    o = [o[j] * lax.rsqrt(jnp.mean(o[j] * o[j], axis=-1, keepdims=True) + EPS) for j in r]
    o = [o[j] * gain[j] * (g[j] * _sigmoid(g[j])) for j in r]
    return o, st_new


def _group(n, pref):
    while n % pref:
        pref //= 2
    return pref


def _hgrn_fwd(proj, logits, gain, n_heads, name):
    T = proj.shape[0]
    nc = T // CHUNK
    H = n_heads
    HB = _group(H, 8)
    W = HB * HEAD

    def col(k):
        return pl.BlockSpec((CHUNK, W), lambda h, c: (c, k * (H // HB) + h))

    def body(q_ref, f_ref, i_ref, g_ref, l_ref, ga_ref, y_ref, s_ref, st):
        @pl.when(pl.program_id(1) == 0)
        def _():
            st[...] = jnp.zeros((HB, HEAD, HEAD), f32)

        cols = [slice(j * HEAD, (j + 1) * HEAD) for j in range(HB)]
        heads = lambda ref: [ref[:, cs] for cs in cols]
        s_ref[...] = st[...]
        o, st_new = _hgrn_heads(heads(q_ref), heads(f_ref), heads(i_ref), heads(g_ref), heads(l_ref), heads(ga_ref), [st[j] for j in range(HB)])
        for j, cs in enumerate(cols):
            y_ref[:, cs] = o[j].astype(bf16)
            st[j] = st_new[j]

    return pl.pallas_call(
        body, name=name, grid=(H // HB, nc),
        in_specs=[col(0), col(1), col(2), col(3), pl.BlockSpec((2, W), lambda h, c: (0, h)), pl.BlockSpec((1, W), lambda h, c: (0, h))],
        out_specs=(pl.BlockSpec((CHUNK, W), lambda h, c: (c, h)), pl.BlockSpec((HB, None, HEAD, HEAD), lambda h, c: (h, c, 0, 0))),
        out_shape=(jax.ShapeDtypeStruct((T, H * HEAD), bf16), jax.ShapeDtypeStruct((H, nc, HEAD, HEAD), f32)),
        scratch_shapes=[pltpu.VMEM((HB, HEAD, HEAD), f32)],
        compiler_params=_params(("parallel", "arbitrary")),
    )(proj, proj, proj, proj, logits, gain)


def _hgrn_bwd(proj, logits, gain, states, dy, n_heads, name):
    T = proj.shape[0]
    nc = T // CHUNK
    H = n_heads
    HB = _group(H, HGRN_BWD_HEADS)
    W = HB * HEAD

    def col(k):
        return pl.BlockSpec((CHUNK, W), lambda h, c: (nc - 1 - c, k * (H // HB) + h))

    out_blk = pl.BlockSpec((CHUNK, W), lambda h, c: (nc - 1 - c, h))

    def body(q_ref, f_ref, i_ref, g_ref, l_ref, ga_ref, s_ref, dy_ref, dq_ref, df_ref, di_ref, dg_ref, dl_ref, dga_ref, dst):
        first = pl.program_id(1) == 0

        @pl.when(first)
        def _():
            dst[...] = jnp.zeros((HB, HEAD, HEAD), f32)
            dl_ref[...] = jnp.zeros((2, W), f32)
            dga_ref[...] = jnp.zeros((1, W), f32)

        cols = [slice(j * HEAD, (j + 1) * HEAD) for j in range(HB)]
        heads = lambda ref: [ref[:, cs] for cs in cols]
        _, vjp = jax.vjp(_hgrn_heads, heads(q_ref), heads(f_ref), heads(i_ref), heads(g_ref), heads(l_ref), heads(ga_ref),
                         [s_ref[j] for j in range(HB)])
        dq, df, di, dg, dl, dga, ds = vjp((heads(dy_ref), [dst[j] for j in range(HB)]))
        for j, cs in enumerate(cols):
            dq_ref[:, cs] = dq[j].astype(bf16)
            df_ref[:, cs] = df[j].astype(bf16)
            di_ref[:, cs] = di[j].astype(bf16)
            dg_ref[:, cs] = dg[j].astype(bf16)
            dst[j] = ds[j]
            dl_ref[:, cs] += dl[j]
            dga_ref[:, cs] += dga[j]

    act = jax.ShapeDtypeStruct((T, H * HEAD), bf16)
    return pl.pallas_call(
        body, name=name, grid=(H // HB, nc),
        in_specs=[col(0), col(1), col(2), col(3), pl.BlockSpec((2, W), lambda h, c: (0, h)), pl.BlockSpec((1, W), lambda h, c: (0, h)),
                  pl.BlockSpec((HB, None, HEAD, HEAD), lambda h, c: (h, nc - 1 - c, 0, 0)), out_blk],
        out_specs=(out_blk, out_blk, out_blk, out_blk, pl.BlockSpec((2, W), lambda h, c: (0, h)), pl.BlockSpec((1, W), lambda h, c: (0, h))),
        out_shape=(act, act, act, act, jax.ShapeDtypeStruct((2, H * HEAD), f32), jax.ShapeDtypeStruct((1, H * HEAD), f32)),
        scratch_shapes=[pltpu.VMEM((HB, HEAD, HEAD), f32)],
        compiler_params=_params(("parallel", "arbitrary")),
    )(proj, proj, proj, proj, logits, gain, states, dy)


def _rel_index():
    t = np.arange(CHUNK)[:, None]
    sp = np.arange(BAND * CHUNK)[None, :]
    dist = (N_PAST - sp // CHUNK) * CHUNK + t - sp % CHUNK
    return (np.clip(dist, -REL_FUTURE, REL_PAST) + REL_FUTURE).reshape(1, -1).astype(np.int32)


def _bias_table(rel_bias_pad, idx, name):
    H = rel_bias_pad.shape[0]
    n = idx.shape[1]
    tc = _tile(n, 4096)

    def body(rb_ref, idx_ref, o_ref):
        onehot = (lax.broadcasted_iota(jnp.int32, (N_REL_PAD, tc), 0) == idx_ref[...]).astype(f32)
        o_ref[...] = lax.dot_general(rb_ref[...], onehot, (((1,), (0,)), ((), ())), precision=HI, preferred_element_type=f32)

    return pl.pallas_call(
        body, name=name, grid=(n // tc,),
        in_specs=[pl.BlockSpec((H, N_REL_PAD), lambda j: (0, 0)), pl.BlockSpec((1, tc), lambda j: (0, j))],
        out_specs=pl.BlockSpec((H, tc), lambda j: (0, j)), out_shape=jax.ShapeDtypeStruct((H, n), f32),
        compiler_params=_params(("parallel",)),
    )(rel_bias_pad, idx)


def _bias_table_bwd(dbias, idx, name):
    H, n = dbias.shape
    tc = _tile(n, 4096)

    def body(d_ref, idx_ref, o_ref):
        onehot = (lax.broadcasted_iota(jnp.int32, (N_REL_PAD, tc), 0) == idx_ref[...]).astype(f32)
        part = lax.dot_general(d_ref[...], onehot, (((1,), (1,)), ((), ())), precision=HI, preferred_element_type=f32)

        @pl.when(pl.program_id(0) == 0)
        def _():
            o_ref[...] = part

        @pl.when(pl.program_id(0) > 0)
        def _():
            o_ref[...] += part

    return pl.pallas_call(
        body, name=name, grid=(n // tc,),
        in_specs=[pl.BlockSpec((H, tc), lambda j: (0, j)), pl.BlockSpec((1, tc), lambda j: (0, j))],
        out_specs=pl.BlockSpec((H, N_REL_PAD), lambda j: (0, 0)), out_shape=jax.ShapeDtypeStruct((H, N_REL_PAD), f32),
        compiler_params=_params(("arbitrary",)),
    )(dbias, idx)


def _head_norm(t, gain):
    return t * lax.rsqrt(jnp.mean(t * t, axis=-1, keepdims=True) + EPS) * gain


def _attn_chunks(qs, kbs, vbs, qg, bias, ns):
    r = range(len(qs))
    qh = [_head_norm(qs[j], qg) for j in r]
    s = [_dotb(qh[j], kbs[j], ((1,), (1,))) * (HEAD ** -0.5) + bias for j in r]
    col = lax.broadcasted_iota(jnp.int32, (1, BAND * CHUNK), 1)
    s = [jnp.where(ns[j] * CHUNK - PAD + col >= 0, s[j], NEG) for j in r]
    e = [jnp.exp(s[j] - jnp.max(s[j], axis=-1, keepdims=True)) for j in r]
    p = [e[j] / jnp.sum(e[j], axis=-1, keepdims=True) for j in r]
    return [_dotb(p[j], vbs[j], ((1,), (0,))) for j in r]


def _attn_fwd(proj, q_gain, k_gain, bias, off, n_heads, name):
    T = proj.shape[0]
    nc = T // CHUNK
    H = n_heads
    CB = _group(nc, ATTN_CHUNKS)
    o0 = off // HEAD
    full = lambda k: pl.BlockSpec((T, HEAD), lambda h, c: (0, o0 + k * H + h))
    vec = pl.BlockSpec((1, HEAD), lambda h, c: (0, 0))

    def body(q_ref, k_ref, v_ref, qg_ref, kg_ref, b_ref, y_ref, kp, vp):
        c = pl.program_id(1)

        @pl.when(c == 0)
        def _():
            kp[pl.ds(0, PAD), :] = jnp.zeros((PAD, HEAD), f32)
            vp[pl.ds(0, PAD), :] = jnp.zeros((PAD, HEAD), f32)
            kp[pl.ds(PAD, T), :] = _head_norm(k_ref[...], kg_ref[...])
            vp[pl.ds(PAD, T), :] = v_ref[...]

        ns = [c * CB + j for j in range(CB)]
        rows = [pl.ds(j * CHUNK, CHUNK) for j in range(CB)]
        bands = [pl.ds(pl.multiple_of(n * CHUNK, CHUNK), BAND * CHUNK) for n in ns]
        outs = _attn_chunks([q_ref[r, :] for r in rows], [kp[b, :] for b in bands], [vp[b, :] for b in bands], qg_ref[...], b_ref[...], ns)
        for r, o in zip(rows, outs):
            y_ref[r, :] = o.astype(bf16)

    return pl.pallas_call(
        body, name=name, grid=(H, nc // CB),
        in_specs=[pl.BlockSpec((CB * CHUNK, HEAD), lambda h, c: (c, o0 + h)), full(1), full(2), vec, vec,
                  pl.BlockSpec((None, CHUNK, BAND * CHUNK), lambda h, c: (h, 0, 0))],
        out_specs=pl.BlockSpec((CB * CHUNK, HEAD), lambda h, c: (c, h)), out_shape=jax.ShapeDtypeStruct((T, H * HEAD), bf16),
        scratch_shapes=[pltpu.VMEM((T + PAD, HEAD), f32), pltpu.VMEM((T + PAD, HEAD), f32)],
        compiler_params=_params(("parallel", "arbitrary")),
    )(proj, proj, proj, q_gain, k_gain, bias)


def _attn_bwd(proj, q_gain, k_gain, bias, dy, off, n_heads, name):
    T = proj.shape[0]
    nc = T // CHUNK
    H = n_heads
    CB = _group(nc, ATTN_CHUNKS)
    o0 = off // HEAD
    full = lambda k: pl.BlockSpec((T, HEAD), lambda h, c: (0, o0 + k * H + h))
    full_out = pl.BlockSpec((T, HEAD), lambda h, c: (0, h))
    vec = pl.BlockSpec((1, HEAD), lambda h, c: (0, 0))
    chunk_out = pl.BlockSpec((CB * CHUNK, HEAD), lambda h, c: (c, h))
    bias_blk = pl.BlockSpec((None, CHUNK, BAND * CHUNK), lambda h, c: (h, 0, 0))

    def body(q_ref, k_ref, v_ref, qg_ref, kg_ref, b_ref, dy_ref, dq_ref, dk_ref, dv_ref, db_ref, dqg_ref, dkg_ref, kp, vp, dkp, dvp):
        h = pl.program_id(0)
        c = pl.program_id(1)

        @pl.when(c == 0)
        def _():
            kp[pl.ds(0, PAD), :] = jnp.zeros((PAD, HEAD), f32)
            vp[pl.ds(0, PAD), :] = jnp.zeros((PAD, HEAD), f32)
            kp[pl.ds(PAD, T), :] = _head_norm(k_ref[...], kg_ref[...])
            vp[pl.ds(PAD, T), :] = v_ref[...]
            dkp[...] = jnp.zeros((T + PAD, HEAD), f32)
            dvp[...] = jnp.zeros((T + PAD, HEAD), f32)
            db_ref[...] = jnp.zeros((CHUNK, BAND * CHUNK), f32)

        @pl.when(jnp.logical_and(h == 0, c == 0))
        def _():
            dqg_ref[...] = jnp.zeros((1, HEAD), f32)
            dkg_ref[...] = jnp.zeros((1, HEAD), f32)

        ns = [c * CB + j for j in range(CB)]
        rows = [pl.ds(j * CHUNK, CHUNK) for j in range(CB)]
        bands = [pl.ds(pl.multiple_of(n * CHUNK, CHUNK), BAND * CHUNK) for n in ns]
        _, vjp = jax.vjp(functools.partial(_attn_chunks, ns=ns), [q_ref[r, :] for r in rows], [kp[b, :] for b in bands],
                         [vp[b, :] for b in bands], qg_ref[...], b_ref[...])
        dqs, dkbs, dvbs, dqg, db = vjp([dy_ref[r, :] for r in rows])
        db_ref[...] += db
        dqg_ref[...] += dqg
        for r, b, dq, dkb, dvb in zip(rows, bands, dqs, dkbs, dvbs):
            dq_ref[r, :] = dq.astype(bf16)
            dkp[b, :] += dkb
            dvp[b, :] += dvb

        @pl.when(c == nc // CB - 1)
        def _():
            _, nvjp = jax.vjp(_head_norm, k_ref[...], kg_ref[...])
            dk, dkg = nvjp(dkp[pl.ds(PAD, T), :])
            dk_ref[...] = dk.astype(bf16)
            dv_ref[...] = dvp[pl.ds(PAD, T), :].astype(bf16)
            dkg_ref[...] += dkg

    act = jax.ShapeDtypeStruct((T, H * HEAD), bf16)
    gvec = jax.ShapeDtypeStruct((1, HEAD), f32)
    pad_buf = pltpu.VMEM((T + PAD, HEAD), f32)
    return pl.pallas_call(
        body, name=name, grid=(H, nc // CB),
        in_specs=[pl.BlockSpec((CB * CHUNK, HEAD), lambda h, c: (c, o0 + h)), full(1), full(2), vec, vec, bias_blk, chunk_out],
        out_specs=(chunk_out, full_out, full_out, bias_blk, vec, vec),
        out_shape=(act, act, act, jax.ShapeDtypeStruct((H, CHUNK, BAND * CHUNK), f32), gvec, gvec),
        scratch_shapes=[pad_buf, pad_buf, pad_buf, pad_buf],
        compiler_params=_params(("arbitrary", "arbitrary")),
    )(proj, proj, proj, q_gain, k_gain, bias, dy)


def _position():
    x, y, c = lax.axis_index("x"), lax.axis_index("y"), lax.axis_index("c")
    return x, y, c, 4 * x + 2 * y + c


def _flip(v, bit):
    return 1 - v if bit else v


def _chips(x, y):
    return [(1 - x, y), (x, 1 - y), (1 - x, 1 - y)]


def _seq_gather(shards, name, collective_id):
    n = len(shards)

    def body(*refs):
        ins, outs = refs[:n], refs[n:2 * n]
        send, recv, loc = refs[2 * n:]
        x, y, c, me = _position()
        sib = (x, y, 1 - c)
        chips = _chips(x, y)
        barrier = pltpu.get_barrier_semaphore()
        for peer in [sib] + [(px, py, c) for px, py in chips]:
            pl.semaphore_signal(barrier, inc=1, device_id=peer, device_id_type=MESH)
        pl.semaphore_wait(barrier, 4)

        def copy(w, k, src, blk, to):
            return pltpu.make_async_remote_copy(src_ref=src, dst_ref=outs[w].at[blk], send_sem=send.at[7 * w + k], recv_sem=recv.at[7 * w + k],
                                                device_id=to, device_id_type=MESH)

        mine = [pltpu.make_async_copy(ins[w], outs[w].at[me], loc.at[w]) for w in range(n)]
        for cp in mine:
            cp.start()
        first = []
        for j, (px, py) in enumerate(chips):
            first += [copy(w, 1 + j, ins[w], me, (px, py, c)) for w in range(n)]
        first += [copy(w, 0, ins[w], me, sib) for w in range(n)]
        for cp in first:
            cp.start()
        passed = []
        for j, (px, py) in enumerate(chips):
            blk = 4 * px + 2 * py + c
            for w in range(n):
                copy(w, 1 + j, ins[w], blk, sib).wait_recv()
                fwd = copy(w, 4 + j, outs[w].at[blk], blk, sib)
                fwd.start()
                passed.append(fwd)
        for w in range(n):
            copy(w, 0, ins[w], 4 * x + 2 * y + (1 - c), sib).wait_recv()
        for j, (px, py) in enumerate(chips):
            for w in range(n):
                copy(w, 4 + j, ins[w], 4 * px + 2 * py + (1 - c), sib).wait_recv()
        for cp in first + passed:
            cp.wait_send()
        for cp in mine:
            cp.wait()

    return pl.kernel(
        body, out_type=tuple(jax.ShapeDtypeStruct((NDEV,) + s.shape, s.dtype) for s in shards),
        mesh=plsc.ScalarSubcoreMesh(axis_name="sequencer", num_cores=1), name=name,
        scratch_types=(pltpu.SemaphoreType.DMA((7 * n,)), pltpu.SemaphoreType.DMA((7 * n,)), pltpu.SemaphoreType.DMA((n,))),
        compiler_params=pltpu.CompilerParams(collective_id=collective_id),
    )(*shards)


NCHIP = 4


def _seq_pair_exchange(grads, name, collective_id, after=()):
    n, na = len(grads), len(after)

    def body(*refs):
        ins, outs = refs[:n], refs[n + na:2 * n + na]
        send, recv = refs[2 * n + na:]
        x, y, c, me = _position()
        sib = (x, y, 1 - c)
        barrier = pltpu.get_barrier_semaphore()
        pl.semaphore_signal(barrier, inc=1, device_id=sib, device_id_type=MESH)
        pl.semaphore_wait(barrier, 1)
        copies = [pltpu.make_async_remote_copy(src_ref=ins[w].at[2 * k + (1 - c)], dst_ref=outs[w].at[k], send_sem=send.at[NCHIP * w + k],
                                               recv_sem=recv.at[NCHIP * w + k], device_id=sib, device_id_type=MESH)
                  for w in range(n) for k in range(NCHIP)]
        for cp in copies:
            cp.start()
        for cp in copies:
            cp.wait_recv()
        for cp in copies:
            cp.wait_send()

    return pl.kernel(
        body, out_type=tuple(jax.ShapeDtypeStruct((NCHIP,) + g.shape[1:], g.dtype) for g in grads),
        mesh=plsc.ScalarSubcoreMesh(axis_name="sequencer", num_cores=1), name=name,
        scratch_types=(pltpu.SemaphoreType.DMA((NCHIP * n,)), pltpu.SemaphoreType.DMA((NCHIP * n,))),
        compiler_params=pltpu.CompilerParams(collective_id=collective_id),
    )(*grads, *after)


def _pair_add(grad, sib_part, name, after=()):
    _, R, C = grad.shape
    tr = _tile(R, 1024, 16)
    core = jnp.reshape(lax.axis_index("c"), (1,)).astype(jnp.int32)

    def body(c_ref, g_ref, s_ref, *rest):
        rest[-1][...] = (g_ref[...].astype(f32) + s_ref[...].astype(f32)).astype(bf16)

    blk = pl.BlockSpec((None, tr, C), lambda k, i, c_ref: (k, i, 0))
    return pl.pallas_call(
        body, name=name,
        grid_spec=pltpu.PrefetchScalarGridSpec(
            num_scalar_prefetch=1, grid=(NCHIP, R // tr),
            in_specs=[pl.BlockSpec((None, tr, C), lambda k, i, c_ref: (2 * k + c_ref[0], i, 0)), blk]
            + [pl.BlockSpec(memory_space=pl.ANY)] * len(after), out_specs=blk),
        out_shape=jax.ShapeDtypeStruct((NCHIP, R, C), bf16), compiler_params=_params(("parallel", "parallel")),
    )(core, grad, sib_part, *after)


def _seq_chip_exchange(sums, name, collective_id, after=()):
    n, na = len(sums), len(after)

    def body(*refs):
        ins, outs = refs[:n], refs[n + na:2 * n + na]
        send, recv, loc = refs[2 * n + na:]
        x, y, c, me = _position()
        chips = _chips(x, y)
        mine = 2 * x + y
        barrier = pltpu.get_barrier_semaphore()
        for px, py in chips:
            pl.semaphore_signal(barrier, inc=1, device_id=(px, py, c), device_id_type=MESH)
        pl.semaphore_wait(barrier, 3)
        local = [pltpu.make_async_copy(ins[w].at[mine], outs[w].at[mine], loc.at[w]) for w in range(n)]
        for cp in local:
            cp.start()
        sends, waits = [], []
        for j, (px, py) in enumerate(chips):
            for w in range(n):
                sems = dict(send_sem=send.at[3 * w + j], recv_sem=recv.at[3 * w + j], device_id=(px, py, c), device_id_type=MESH)
                sends.append(pltpu.make_async_remote_copy(src_ref=ins[w].at[2 * px + py], dst_ref=outs[w].at[mine], **sems))
                waits.append(pltpu.make_async_remote_copy(src_ref=ins[w].at[2 * px + py], dst_ref=outs[w].at[2 * px + py], **sems))
        for cp in sends:
            cp.start()
        for cp in waits:
            cp.wait_recv()
        for cp in sends:
            cp.wait_send()
        for cp in local:
            cp.wait()

    return pl.kernel(
        body, out_type=tuple(jax.ShapeDtypeStruct(s.shape, s.dtype) for s in sums),
        mesh=plsc.ScalarSubcoreMesh(axis_name="sequencer", num_cores=1), name=name,
        scratch_types=(pltpu.SemaphoreType.DMA((3 * n,)), pltpu.SemaphoreType.DMA((3 * n,)), pltpu.SemaphoreType.DMA((n,))),
        compiler_params=pltpu.CompilerParams(collective_id=collective_id),
    )(*sums, *after)


def _reduce_scatter(grads, tag, ids, after=(), add_after=()):
    sib_parts = _seq_pair_exchange(grads, "pair_exchange_" + tag, ids[0], after=after)
    sums = [_pair_add(g, s, "pair_add_%s%d" % (tag, i), after=add_after) for i, (g, s) in enumerate(zip(grads, sib_parts))]
    return _seq_chip_exchange(sums, "chip_exchange_" + tag, ids[1]), sums


def _small_all_reduce(v, name):
    R, C = v.shape

    def body(v_ref, o_ref, buf, send, recv):
        x, y, c, me = _position()
        buf[me] = v_ref[...]
        sends, waits = [], []
        for r in range(1, NDEV):
            px, py, pc = _flip(x, r & 4), _flip(y, r & 2), _flip(c, r & 1)
            peer = 4 * px + 2 * py + pc
            sends.append(pltpu.make_async_remote_copy(src_ref=v_ref, dst_ref=buf.at[me], send_sem=send.at[r - 1], recv_sem=recv.at[r - 1],
                                                      device_id=(px, py, pc), device_id_type=MESH))
            waits.append(pltpu.make_async_remote_copy(src_ref=v_ref, dst_ref=buf.at[peer], send_sem=send.at[r - 1], recv_sem=recv.at[r - 1],
                                                      device_id=(px, py, pc), device_id_type=MESH))
        for cp in sends:
            cp.start()
        for cp in waits:
            cp.wait_recv()
        for cp in sends:
            cp.wait_send()
        acc = buf[0]
        for i in range(1, NDEV):
            acc = acc + buf[i]
        o_ref[...] = acc

    vm = pl.BlockSpec(memory_space=pltpu.VMEM)
    return pl.pallas_call(
        body, name=name, in_specs=[vm], out_specs=vm, out_shape=jax.ShapeDtypeStruct((R, C), f32),
        scratch_shapes=[pltpu.VMEM((NDEV, R, C), f32), pltpu.SemaphoreType.DMA((7,)), pltpu.SemaphoreType.DMA((7,))],
    )(v)


def _adamw_math(w, g, m, v):
    m = ADAM_B1 * m + (1.0 - ADAM_B1) * g
    v = ADAM_B2 * v + (1.0 - ADAM_B2) * (g * g)
    m_hat = m / (1.0 - ADAM_B1 ** ADAM_STEP)
    v_hat = v / (1.0 - ADAM_B2 ** ADAM_STEP)
    delta = -ADAM_LR * (m_hat / (jnp.sqrt(v_hat) + ADAM_EPS) + ADAM_WD * w)
    return delta, m, v


def _adamw_parts(w, m, v, parts, name, after=()):
    R, C = w.shape
    tr = _tile(R, 128, 16)
    blk = pl.BlockSpec((tr, C), lambda i: (i, 0))

    def body(w_ref, m_ref, v_ref, p_ref, *rest):
        g_ref, d_ref, mo_ref, vo_ref = rest[len(after):]
        g = p_ref[0].astype(f32)
        for i in range(1, NCHIP):
            g = g + p_ref[i].astype(f32)
        d, mn, vn = _adamw_math(w_ref[...], g, m_ref[...], v_ref[...])
        g_ref[...] = g
        d_ref[...] = d
        mo_ref[...] = mn
        vo_ref[...] = vn

    shp = jax.ShapeDtypeStruct((R, C), f32)
    return pl.pallas_call(
        body, name=name, grid=(R // tr,),
        in_specs=[blk, blk, blk, pl.BlockSpec((NCHIP, tr, C), lambda i: (0, i, 0))] + [pl.BlockSpec(a.shape, lambda i: (0, 0)) for a in after],
        out_specs=(blk, blk, blk, blk), out_shape=(shp, shp, shp, shp), compiler_params=_params(("parallel",)),
    )(w, m, v, parts, *after)


def _adamw_small(w, g, m, v, name):
    def body(w_ref, g_ref, m_ref, v_ref, d_ref, mo_ref, vo_ref):
        d, mn, vn = _adamw_math(w_ref[...], g_ref[...], m_ref[...], v_ref[...])
        d_ref[...] = d
        mo_ref[...] = mn
        vo_ref[...] = vn

    shp = jax.ShapeDtypeStruct(w.shape, f32)
    return pl.pallas_call(body, name=name, out_shape=(shp, shp, shp))(w, g, m, v)


SMALL_COLS = 1024


def _pack(arrs):
    flat = jnp.concatenate([a.reshape(-1) for a in arrs])
    rows = -(-flat.shape[0] // (8 * SMALL_COLS)) * 8
    return jnp.pad(flat, (0, rows * SMALL_COLS - flat.shape[0])).reshape(rows, SMALL_COLS)


def _unpack(packed, like):
    flat = packed.reshape(-1)
    out, pos = [], 0
    for a in like:
        out.append(flat[pos:pos + a.size].reshape(a.shape))
        pos += a.size
    return out


def kernel(x, w_in, b_gate, norm_mix, norm_ffn, hgrn_lb_logits, hgrn_out_gain, q_gain, k_gain, rel_bias, w_proj_a, w_proj_b, w_out, w_ffn_in, w_ffn_out, loss_target, m_w_in, m_b_gate, m_norm_mix, m_norm_ffn, m_hgrn_lb_logits, m_hgrn_out_gain, m_q_gain, m_k_gain, m_rel_bias, m_w_proj_a, m_w_proj_b, m_w_out, m_w_ffn_in, m_w_ffn_out, v_w_in, v_b_gate, v_norm_mix, v_norm_ffn, v_hgrn_lb_logits, v_hgrn_out_gain, v_q_gain, v_k_gain, v_rel_bias, v_w_proj_a, v_w_proj_b, v_w_out, v_w_ffn_in, v_w_ffn_out):
    xs = x[0]
    target = loss_target[0]
    T, D = xs.shape
    d_a = hgrn_out_gain.shape[-1]
    H = d_a // HEAD
    d_b = d_a
    off_b = 4 * d_a
    off_g = off_b + 3 * d_b
    assert rel_bias.shape[1] == H and T % CHUNK == 0 and T // CHUNK > N_PAST

    big_w = [w_in[0], w_proj_a[0], w_proj_b[0], w_out[0], w_ffn_in[0], w_ffn_out[0]]
    big_m = [m_w_in[0], m_w_proj_a[0], m_w_proj_b[0], m_w_out[0], m_w_ffn_in[0], m_w_ffn_out[0]]
    big_v = [v_w_in[0], v_w_proj_a[0], v_w_proj_b[0], v_w_out[0], v_w_ffn_in[0], v_w_ffn_out[0]]

    sh = [w.astype(bf16) for w in big_w]
    (g_in,) = _seq_gather(sh[0:1], "gather_a", 1)
    g_pa, g_pb, g_out = _seq_gather(sh[1:4], "gather_b", 2)
    (g_fin,) = _seq_gather(sh[4:5], "gather_c", 3)
    (g_fout,) = _seq_gather(sh[5:6], "gather_d", 4)

    h = _rms_fwd(xs, None, norm_mix, "rms_mix")
    proj = _mm(h, g_in, mode="nn", b_blocked=True, name="mm_proj")
    y_a, states = _hgrn_fwd(proj, hgrn_lb_logits, hgrn_out_gain, H, "hgrn_fwd")
    idx = jnp.asarray(_rel_index())
    rb_pad = jnp.pad(rel_bias[0], ((0, 0), (0, N_REL_PAD - N_REL)))
    bias = _bias_table(rb_pad, idx, "bias_table").reshape(H, CHUNK, BAND * CHUNK)
    y_b = _attn_fwd(proj, q_gain, k_gain, bias, off_b, H, "attn_fwd")
    wg_out = g_out.reshape(-1, g_out.shape[-1])
    wg_fout = g_fout.reshape(-1, g_fout.shape[-1])
    pa = _mm(y_a, g_pa, mode="nn", b_blocked=True, tm=2048, name="mm_proj_a")
    pb = _mm(y_b, g_pb, mode="nn", b_blocked=True, tm=2048, name="mm_proj_b")
    merged = _merge_fwd(pa, pb, proj, b_gate, off_g, "merge_fwd")
    mo = _mm(merged, wg_out, mode="nn", name="mm_out")
    x1, h2 = _rms_fwd(xs, mo, norm_ffn, "rms_ffn")
    gu = _mm(h2, g_fin, mode="nn", b_blocked=True, name="mm_ffn_in")
    act = _swiglu_fwd(gu, "swiglu_fwd")
    fo = _mm(act, wg_fout, mode="nn", tk=2816, name="mm_ffn_out")
    dy, loss_acc = _loss_head(x1, fo, target, "loss_head")
    loss_part = loss_acc[0:1, 0:1] * (0.5 / D)

    dact = _mm(dy, wg_fout, mode="nt", tn=1408, name="mm_d_act")
    gw_fout = _mm(act, dy, mode="tn", out_dtype=bf16, tm=1408, name="mm_gw_ffn_out")
    dgate, dup = _swiglu_bwd(gu, dact, "swiglu_bwd")
    dgu = jnp.concatenate([dgate, dup], axis=1)
    gw_fin = _mm(h2, dgu, mode="tn", out_blocked=True, out_dtype=bf16, tn=g_fin.shape[-1], name="mm_gw_ffn_in")
    dh2 = _mm(dgu, g_fin, mode="nt", b_blocked=True, name="mm_d_h2")
    (p_fout, p_fin), sums_a = _reduce_scatter([gw_fout.reshape(NDEV, -1, D), gw_fin], "a", (5, 6), add_after=(dh2,))
    dx1, g_norm_ffn = _rms_bwd(x1, norm_ffn, dh2, dy, "rms_ffn_bwd", after=sums_a)

    dmerged = _mm(dx1, wg_out, mode="nt", name="mm_d_merged")
    gw_out = _mm(merged, dx1, mode="tn", out_dtype=bf16, name="mm_gw_out")
    dpa, dgl_a, gb_a = _branch_bwd(dmerged, pa, proj, b_gate, off_g, 0, "branch_a_bwd")
    dpb, dgl_b, gb_b = _branch_bwd(dmerged, pb, proj, b_gate, off_g + D, D, "branch_b_bwd")
    dy_a = _mm(dpa, g_pa, mode="nt", b_blocked=True, tm=2048, name="mm_d_ya")
    dy_b = _mm(dpb, g_pb, mode="nt", b_blocked=True, tm=2048, name="mm_d_yb")
    gw_pa = _mm(y_a, dpa, mode="tn", out_blocked=True, out_dtype=bf16, tn=g_pa.shape[-1], name="mm_gw_proj_a")
    gw_pb = _mm(y_b, dpb, mode="tn", out_blocked=True, out_dtype=bf16, tn=g_pb.shape[-1], name="mm_gw_proj_b")

    dq_a, df_a, di_a, dg_a, g_logits, g_gain = _hgrn_bwd(proj, hgrn_lb_logits, hgrn_out_gain, states, dy_a, H, "hgrn_bwd")
    dq_b, dk_b, dv_b, dbias, g_qg, g_kg = _attn_bwd(proj, q_gain, k_gain, bias, dy_b, off_b, H, "attn_bwd")
    g_rel_pad = _bias_table_bwd(dbias.reshape(H, -1), idx, "bias_table_bwd")
    g_rel = g_rel_pad[:, :N_REL]
    dproj = jnp.concatenate([dq_a, df_a, di_a, dg_a, dq_b, dk_b, dv_b, dgl_a, dgl_b], axis=1)
    (p_out, p_pa, p_pb), sums_b = _reduce_scatter([gw_out.reshape(NDEV, -1, D), gw_pa, gw_pb], "b", (7, 8), after=(g_gain, p_fout, p_fin), add_after=(dy_b,))
    gw_in = _mm(h, dproj, mode="tn", out_blocked=True, out_dtype=bf16, tn=g_in.shape[-1], name="mm_gw_in", after=sums_b)
    (p_in,), sums_c = _reduce_scatter([gw_in], "c", (9, 10), after=(p_out, p_pa, p_pb), add_after=(g_rel_pad,))
    dh = _mm(dproj, g_in, mode="nt", b_blocked=True, name="mm_d_h", after=sums_c)
    grad_x, g_norm_mix = _rms_bwd(xs, norm_mix, dh, dx1, "rms_mix_bwd")

    parts = [p_in, p_pa, p_pb, p_out, p_fin, p_fout]
    names = ["w_in", "w_proj_a", "w_proj_b", "w_out", "w_ffn_in", "w_ffn_out"]
    big = {}
    for nm, w, m, v, p in zip(names, big_w, big_m, big_v, parts):
        big[nm] = [o[None] for o in _adamw_parts(w, m, v, p, "adamw_" + nm, after=() if nm == "w_in" else (g_norm_mix,))]

    small_names = ["b_gate", "norm_mix", "norm_ffn", "hgrn_lb_logits", "hgrn_out_gain", "q_gain", "k_gain", "rel_bias"]
    small_w = [b_gate, norm_mix, norm_ffn, hgrn_lb_logits, hgrn_out_gain, q_gain, k_gain, rel_bias]
    small_m = [m_b_gate, m_norm_mix, m_norm_ffn, m_hgrn_lb_logits, m_hgrn_out_gain, m_q_gain, m_k_gain, m_rel_bias]
    small_v = [v_b_gate, v_norm_mix, v_norm_ffn, v_hgrn_lb_logits, v_hgrn_out_gain, v_q_gain, v_k_gain, v_rel_bias]
    small_g = [jnp.concatenate([gb_a, gb_b], axis=1), g_norm_mix, g_norm_ffn, g_logits, g_gain, g_qg, g_kg, g_rel[None], loss_part]
    g_sum = _small_all_reduce(_pack(small_g), "reduce_small")
    loss = _unpack(g_sum, small_g)[-1].reshape(())
    d_s, m_s, v_s = _adamw_small(_pack(small_w), g_sum, _pack(small_m), _pack(small_v), "adamw_small")
    small = {}
    for nm, g, d, m, v in zip(small_names, _unpack(g_sum, small_w), _unpack(d_s, small_w), _unpack(m_s, small_w), _unpack(v_s, small_w)):
        small[nm] = [g, d, m, v]

    order = ["w_in", "b_gate", "norm_mix", "norm_ffn", "hgrn_lb_logits", "hgrn_out_gain", "q_gain", "k_gain", "rel_bias",
             "w_proj_a", "w_proj_b", "w_out", "w_ffn_in", "w_ffn_out"]
    res = {**big, **small}
    outs = [loss, grad_x[None]]
    for k in range(4):
        outs += [res[nm][k] for nm in order]
    return tuple(outs)
```

```python
import functools

import numpy as np
import jax
import jax.numpy as jnp
from jax import lax
from jax.experimental import pallas as pl
from jax.experimental.pallas import tpu as pltpu
from jax.experimental.pallas import tpu_sc as plsc

f32 = jnp.float32
bf16 = jnp.bfloat16
HI = lax.Precision.HIGHEST
MESH = pl.DeviceIdType.MESH
AXES = ("x", "y", "c")
NDEV = 8

CHUNK = 64
HEAD = 128
SUB = 8
HGRN_BWD_HEADS = 8
ATTN_CHUNKS = 8
N_PAST = 8
BAND = N_PAST + 1
PAD = N_PAST * CHUNK
REL_FUTURE = CHUNK - 1
REL_PAST = 2 * CHUNK - 1
N_REL = REL_FUTURE + REL_PAST + 1
N_REL_PAD = 256
EPS = 1e-6
NEG = -1e30

ADAM_LR = 0.001
ADAM_B1 = 0.9
ADAM_B2 = 0.999
ADAM_EPS = 1e-08
ADAM_WD = 0.01
ADAM_STEP = 10

VMEM_LIMIT = 56 * 1024 * 1024


def _params(sem=None):
    return pltpu.CompilerParams(dimension_semantics=sem, vmem_limit_bytes=VMEM_LIMIT)


def _tile(n, pref, unit=128):
    if n <= pref:
        return n
    t = (pref // unit) * unit
    while t >= unit:
        if n % t == 0:
            return t
        t -= unit
    return n


_sigmoid = jax.nn.sigmoid


def _mm(a, b, *, mode, name, b_blocked=False, out_blocked=False, out_dtype=f32, tm=1024, tn=1024, tk=2048, after=(),
        a_stacked=False, b_stacked=False):
    if a_stacked:
        assert mode == "nt"
        M, K = a.shape[1], 2 * a.shape[2]
    elif mode == "tn":
        K, M = a.shape
    else:
        M, K = a.shape
    if b_blocked:
        nb, mid, cb = b.shape
        if mode == "nn":
            assert mid == K
            N, tn = nb * cb, cb
        else:
            assert mode == "nt" and nb * cb == K
            N, tk = mid, cb
    elif b_stacked:
        assert mode == "tn"
        N = 2 * b.shape[2]
    else:
        N = b.shape[1] if mode in ("nn", "tn") else b.shape[0]
    tm = _tile(M, tm)
    tn = tn if (b_blocked and mode == "nn") or out_blocked else _tile(N, tn)
    tk = tk if b_blocked and mode == "nt" else _tile(K, tk)
    assert M % tm == 0 and N % tn == 0 and K % tk == 0
    nk = K // tk
    grid = (M // tm, N // tn, nk)
    if a_stacked:
        ka = K // 2 // tk
        a_spec = pl.BlockSpec((None, tm, tk), lambda i, j, k: (k // ka, i, k % ka))
    elif mode == "tn":
        a_spec = pl.BlockSpec((tk, tm), lambda i, j, k: (k, i))
    else:
        a_spec = pl.BlockSpec((tm, tk), lambda i, j, k: (i, k))
    if mode == "nn":
        b_spec = pl.BlockSpec((None, tk, cb), lambda i, j, k: (j, k, 0)) if b_blocked else pl.BlockSpec((tk, tn), lambda i, j, k: (k, j))
    elif mode == "nt":
        b_spec = pl.BlockSpec((None, tn, cb), lambda i, j, k: (k, j, 0)) if b_blocked else pl.BlockSpec((tn, tk), lambda i, j, k: (j, k))
    elif b_stacked:
        nh = N // 2 // tn
        b_spec = pl.BlockSpec((None, tk, tn), lambda i, j, k: (j // nh, k, j % nh))
    else:
        b_spec = pl.BlockSpec((tk, tn), lambda i, j, k: (k, j))
    if out_blocked:
        out_shape = jax.ShapeDtypeStruct((N // tn, M, tn), out_dtype)
        o_spec = pl.BlockSpec((None, tm, tn), lambda i, j, k: (j, i, 0))
    else:
        out_shape = jax.ShapeDtypeStruct((M, N), out_dtype)
        o_spec = pl.BlockSpec((tm, tn), lambda i, j, k: (i, j))
    dims = {"nn": ((1,), (0,)), "nt": ((1,), (1,)), "tn": ((0,), (0,))}[mode]

    def body(a_ref, b_ref, *rest):
        o_ref, acc = rest[len(after)], rest[len(after) + 1:]
        p = lax.dot_general(a_ref[...].astype(bf16), b_ref[...].astype(bf16), (dims, ((), ())), preferred_element_type=f32)
        if nk == 1:
            o_ref[...] = p.astype(out_dtype)
        else:
            acc_ref = acc[0]
            k = pl.program_id(2)

            @pl.when(k == 0)
            def _():
                acc_ref[...] = p

            @pl.when(k > 0)
            def _():
                acc_ref[...] += p

            @pl.when(k == nk - 1)
            def _():
                o_ref[...] = acc_ref[...].astype(out_dtype)

    return pl.pallas_call(
        body, name=name, grid=grid, in_specs=[a_spec, b_spec] + [pl.BlockSpec(memory_space=pl.ANY)] * len(after), out_specs=o_spec,
        out_shape=out_shape, scratch_shapes=[pltpu.VMEM((tm, tn), f32)] if nk > 1 else [],
        compiler_params=_params(("parallel", "parallel", "arbitrary")),
    )(a, b, *after)


def _rms_fwd(x, res, gain, name):
    T, D = x.shape
    tr = _tile(T, 256, 8)
    row = pl.BlockSpec((tr, D), lambda i: (i, 0))
    vec = pl.BlockSpec((1, D), lambda i: (0, 0))

    def body(*refs):
        if res is None:
            x_ref, g_ref, h_ref = refs
            xs = x_ref[...]
        else:
            x_ref, r_ref, g_ref, xs_ref, h_ref = refs
            xs = x_ref[...] + r_ref[...]
            xs_ref[...] = xs
        r = lax.rsqrt(jnp.mean(xs * xs, axis=-1, keepdims=True) + EPS)
        h_ref[...] = (xs * r * g_ref[...]).astype(bf16)

    h_shape = jax.ShapeDtypeStruct((T, D), bf16)
    if res is None:
        return pl.pallas_call(body, name=name, grid=(T // tr,), in_specs=[row, vec], out_specs=row, out_shape=h_shape,
                              compiler_params=_params(("parallel",)))(x, gain)
    return pl.pallas_call(body, name=name, grid=(T // tr,), in_specs=[row, row, vec], out_specs=(row, row),
                          out_shape=(jax.ShapeDtypeStruct((T, D), f32), h_shape), compiler_params=_params(("parallel",)))(x, res, gain)


def _rms_bwd(xs, gain, dh, extra, name, after=()):
    T, D = xs.shape
    tr = _tile(T, 256, 8)
    row = pl.BlockSpec((tr, D), lambda i: (i, 0))
    vec = pl.BlockSpec((1, D), lambda i: (0, 0))

    def body(x_ref, g_ref, dh_ref, e_ref, *rest):
        dx_ref, dg_ref = rest[len(after):]
        x = x_ref[...]
        r = lax.rsqrt(jnp.mean(x * x, axis=-1, keepdims=True) + EPS)
        xhat = x * r
        dh_v = dh_ref[...]
        gd = dh_v * g_ref[...]
        dx_ref[...] = e_ref[...] + r * (gd - xhat * jnp.mean(gd * xhat, axis=-1, keepdims=True))
        part = jnp.sum(dh_v * xhat, axis=0, keepdims=True)

        @pl.when(pl.program_id(0) == 0)
        def _():
            dg_ref[...] = part

        @pl.when(pl.program_id(0) > 0)
        def _():
            dg_ref[...] += part

    return pl.pallas_call(body, name=name, grid=(T // tr,),
                          in_specs=[row, vec, row, row] + [pl.BlockSpec(memory_space=pl.ANY)] * len(after), out_specs=(row, vec),
                          out_shape=(jax.ShapeDtypeStruct((T, D), f32), jax.ShapeDtypeStruct((1, D), f32)),
                          compiler_params=_params(("arbitrary",)))(xs, gain, dh, extra, *after)


def _merge_fwd(pa, pb, proj, b_gate, off, name):
    T, D = pa.shape
    tr, tc = _tile(T, 512, 8), _tile(D, 512)
    oa, ob, nb = off // tc, (off + D) // tc, D // tc
    blk = pl.BlockSpec((tr, tc), lambda i, j: (i, j))

    def body(pa_ref, pb_ref, ga_ref, gb_ref, ba_ref, bb_ref, o_ref):
        ga = _sigmoid(ga_ref[...] + ba_ref[...])
        gb = _sigmoid(gb_ref[...] + bb_ref[...])
        o_ref[...] = (ga * pa_ref[...] + gb * pb_ref[...]).astype(bf16)

    return pl.pallas_call(
        body, name=name, grid=(T // tr, nb),
        in_specs=[blk, blk, pl.BlockSpec((tr, tc), lambda i, j: (i, oa + j)), pl.BlockSpec((tr, tc), lambda i, j: (i, ob + j)),
                  pl.BlockSpec((1, tc), lambda i, j: (0, j)), pl.BlockSpec((1, tc), lambda i, j: (0, nb + j))],
        out_specs=blk, out_shape=jax.ShapeDtypeStruct((T, D), bf16), compiler_params=_params(("parallel", "parallel")),
    )(pa, pb, proj, proj, b_gate, b_gate)


def _branch_bwd(dm, p, proj, b_gate, off, boff, name):
    T, D = p.shape
    tr, tc = _tile(T, 512, 8), _tile(D, 512)
    og, obias = off // tc, boff // tc
    blk = pl.BlockSpec((tr, tc), lambda j, i: (i, j))
    vec = pl.BlockSpec((1, tc), lambda j, i: (0, j))

    def body(dm_ref, p_ref, gl_ref, b_ref, dp_ref, dgl_ref, db_ref):
        g = _sigmoid(gl_ref[...] + b_ref[...])
        dm_v = dm_ref[...]
        dp_ref[...] = (dm_v * g).astype(bf16)
        dgl = dm_v * p_ref[...] * g * (1.0 - g)
        dgl_ref[...] = dgl.astype(bf16)
        part = jnp.sum(dgl, axis=0, keepdims=True)

        @pl.when(pl.program_id(1) == 0)
        def _():
            db_ref[...] = part

        @pl.when(pl.program_id(1) > 0)
        def _():
            db_ref[...] += part

    return pl.pallas_call(
        body, name=name, grid=(D // tc, T // tr),
        in_specs=[blk, blk, pl.BlockSpec((tr, tc), lambda j, i: (i, og + j)), pl.BlockSpec((1, tc), lambda j, i: (0, obias + j))],
        out_specs=(blk, blk, vec),
        out_shape=(jax.ShapeDtypeStruct((T, D), bf16), jax.ShapeDtypeStruct((T, D), bf16), jax.ShapeDtypeStruct((1, D), f32)),
        compiler_params=_params(("parallel", "arbitrary")),
    )(dm, p, proj, b_gate)


def _ffn_in_swiglu(h, w, name):
    T, K = h.shape
    nb, _, cb = w.shape
    half = nb // 2
    F = half * cb
    tm = _tile(T, 512, 8)

    def body(h_ref, wg_ref, wu_ref, gu_ref, act_ref):
        dn = (((1,), (0,)), ((), ()))
        hv = h_ref[...]
        g = lax.dot_general(hv, wg_ref[...], dn, preferred_element_type=f32)
        u = lax.dot_general(hv, wu_ref[...], dn, preferred_element_type=f32)
        gu_ref[0] = g
        gu_ref[1] = u
        act_ref[...] = (g * _sigmoid(g) * u).astype(bf16)

    return pl.pallas_call(
        body, name=name, grid=(half, T // tm),
        in_specs=[pl.BlockSpec((tm, K), lambda j, i: (i, 0)), pl.BlockSpec((None, K, cb), lambda j, i: (j, 0, 0)),
                  pl.BlockSpec((None, K, cb), lambda j, i: (j + half, 0, 0))],
        out_specs=(pl.BlockSpec((2, tm, cb), lambda j, i: (0, i, j)), pl.BlockSpec((tm, cb), lambda j, i: (i, j))),
        out_shape=(jax.ShapeDtypeStruct((2, T, F), f32), jax.ShapeDtypeStruct((T, F), bf16)),
        compiler_params=_params(("parallel", "parallel")),
    )(h, w, w)


def _d_act_swiglu(dy, w_out, gu, name):
    T, D = dy.shape
    F = w_out.shape[0]
    tm, tn = _tile(T, 512, 8), _tile(F, 1408)

    def body(dy_ref, w_ref, gu_ref, o_ref):
        d = lax.dot_general(dy_ref[...].astype(bf16), w_ref[...], (((1,), (1,)), ((), ())), preferred_element_type=f32)
        g = gu_ref[0]
        s = _sigmoid(g)
        o_ref[0] = (d * gu_ref[1] * s * (1.0 + g * (1.0 - s))).astype(bf16)
        o_ref[1] = (d * g * s).astype(bf16)

    blk = pl.BlockSpec((2, tm, tn), lambda j, i: (0, i, j))
    return pl.pallas_call(
        body, name=name, grid=(F // tn, T // tm),
        in_specs=[pl.BlockSpec((tm, D), lambda j, i: (i, 0)), pl.BlockSpec((tn, D), lambda j, i: (j, 0)), blk],
        out_specs=blk, out_shape=jax.ShapeDtypeStruct((2, T, F), bf16), compiler_params=_params(("parallel", "parallel")),
    )(dy, w_out, gu)


def _loss_head(x1, fo, target, name):
    T, D = x1.shape
    tr = _tile(T, 256, 8)
    row = pl.BlockSpec((tr, D), lambda i: (i, 0))
    acc = pl.BlockSpec((8, 128), lambda i: (0, 0))

    def body(x_ref, f_ref, t_ref, dy_ref, l_ref):
        d = x_ref[...] + f_ref[...] - t_ref[...]
        dy_ref[...] = d * (1.0 / D)
        part = jnp.sum(jnp.sum(d * d, axis=1, keepdims=True), axis=0, keepdims=True)

        @pl.when(pl.program_id(0) == 0)
        def _():
            l_ref[...] = jnp.zeros((8, 128), f32)

        l_ref[...] += part

    return pl.pallas_call(body, name=name, grid=(T // tr,), in_specs=[row, row, row], out_specs=(row, acc),
                          out_shape=(jax.ShapeDtypeStruct((T, D), f32), jax.ShapeDtypeStruct((8, 128), f32)),
                          compiler_params=_params(("arbitrary",)))(x1, fo, target)


_DIMS = {"nn": ((1,), (0,)), "nt": ((1,), (1,)), "tn": ((0,), (0,))}
_MODE = {v: k for k, v in _DIMS.items()}


def _dot_bf16(a, b, mode):
    return lax.dot_general(a.astype(bf16), b.astype(bf16), (_DIMS[mode], ((), ())), preferred_element_type=f32)


@functools.partial(jax.custom_vjp, nondiff_argnums=(2,))
def _dotm(a, b, mode):
    return _dot_bf16(a, b, mode)


def _dotm_fwd(a, b, mode):
    return _dot_bf16(a, b, mode), (a, b)


def _dotm_bwd(mode, res, g):
    a, b = res
    if mode == "nn":
        return _dot_bf16(g, b, "nt"), _dot_bf16(a, g, "tn")
    if mode == "nt":
        return _dot_bf16(g, b, "nn"), _dot_bf16(g, a, "tn")
    return _dot_bf16(b, g, "nt"), _dot_bf16(a, g, "nn")


_dotm.defvjp(_dotm_fwd, _dotm_bwd)


def _dotb(a, b, dims):
    return _dotm(a, b, _MODE[dims])


def _triangle_sum(v, lower):
    row = lax.broadcasted_iota(jnp.int32, (CHUNK, CHUNK), 0)
    col = lax.broadcasted_iota(jnp.int32, (CHUNK, CHUNK), 1)
    tri = ((col <= row) if lower else (col >= row)).astype(bf16)

    def top(t):
        return lax.bitcast_convert_type(lax.bitcast_convert_type(t, jnp.uint32) & jnp.uint32(0xFFFF0000), f32)

    hi = top(v)
    mid = top(v - hi)
    low = (v - hi) - mid
    hi, mid, low = hi.astype(bf16), mid.astype(bf16), low.astype(bf16)
    dn = (((1,), (0,)), ((), ()))
    return (lax.dot_general(tri, hi, dn, preferred_element_type=f32) + lax.dot_general(tri, mid, dn, preferred_element_type=f32)
            + lax.dot_general(tri, low, dn, preferred_element_type=f32))


@jax.custom_vjp
def _cumsum_rows(v):
    return _triangle_sum(v, True)


_cumsum_rows.defvjp(lambda v: (_triangle_sum(v, True), None), lambda _, g: (_triangle_sum(g, False),))


def _hgrn_heads(q, fl, iv, g, logits, gain, st):
    r = range(len(q))
    lb = [jax.nn.softmax(logits[j], axis=0)[0:1] for j in r]
    f = [lb[j] + (1.0 - lb[j]) * _sigmoid(fl[j]) for j in r]
    lf = [jnp.log(f[j]) for j in r]
    kk = [1.0 - f[j] for j in r]
    qs = [q[j] * _sigmoid(q[j]) for j in r]
    b = [_cumsum_rows(lf[j]) for j in r]
    b_last = [jnp.sum(lf[j], axis=0, keepdims=True) for j in r]
    o = [_dotb(qs[j] * jnp.exp(b[j]), st[j], ((1,), (1,))) for j in r]
    r3 = lax.broadcasted_iota(jnp.int32, (SUB, SUB, HEAD), 0)
    c3 = lax.broadcasted_iota(jnp.int32, (SUB, SUB, HEAD), 1)
    parts = [[] for _ in r]
    for i in range(CHUNK // SUB):
        lo, hi = i * SUB, (i + 1) * SUB
        bi = [b[j][lo:hi] for j in r]
        dec = [jnp.exp(jnp.where(c3 <= r3, bi[j][:, None, :] - bi[j][None, :, :], -jnp.inf)) for j in r]
        s = [jnp.sum(qs[j][lo:hi][:, None, :] * kk[j][lo:hi][None, :, :] * dec[j], axis=-1) for j in r]
        if i > 0:
            anchor = [jnp.max(bi[j], axis=0, keepdims=True) for j in r]
            qa = [qs[j][lo:hi] * jnp.exp(bi[j] - anchor[j]) for j in r]
            kd = [kk[j][:lo] * jnp.exp(anchor[j] - b[j][:lo]) for j in r]
            s = [jnp.concatenate([_dotb(qa[j], kd[j], ((1,), (1,))), s[j]], axis=1) for j in r]
        for j in r:
            parts[j].append(_dotb(s[j], iv[j][:hi], ((1,), (0,))))
    o = [o[j] + jnp.concatenate(parts[j], axis=0) for j in r]
    st_new = [st[j] * jnp.exp(b_last[j]) + _dotb(iv[j], kk[j] * jnp.exp(b_last[j] - b[j]), ((0,), (0,))) for j in r]
    o = [o[j] * lax.rsqrt(jnp.mean(o[j] * o[j], axis=-1, keepdims=True) + EPS) for j in r]
    o = [o[j] * gain[j] * (g[j] * _sigmoid(g[j])) for j in r]
    return o, st_new


def _group(n, pref):
    while n % pref:
        pref //= 2
    return pref


def _hgrn_fwd(proj, logits, gain, n_heads, name):
    T = proj.shape[0]
    nc = T // CHUNK
    H = n_heads
    HB = _group(H, 8)
    W = HB * HEAD

    def col(k):
        return pl.BlockSpec((CHUNK, W), lambda h, c: (c, k * (H // HB) + h))

    def body(q_ref, f_ref, i_ref, g_ref, l_ref, ga_ref, y_ref, s_ref, st):
        @pl.when(pl.program_id(1) == 0)
        def _():
            st[...] = jnp.zeros((HB, HEAD, HEAD), f32)

        cols = [slice(j * HEAD, (j + 1) * HEAD) for j in range(HB)]
        heads = lambda ref: [ref[:, cs] for cs in cols]
        s_ref[...] = st[...]
        o, st_new = _hgrn_heads(heads(q_ref), heads(f_ref), heads(i_ref), heads(g_ref), heads(l_ref), heads(ga_ref), [st[j] for j in range(HB)])
        for j, cs in enumerate(cols):
            y_ref[:, cs] = o[j].astype(bf16)
            st[j] = st_new[j]

    return pl.pallas_call(
        body, name=name, grid=(H // HB, nc),
        in_specs=[col(0), col(1), col(2), col(3), pl.BlockSpec((2, W), lambda h, c: (0, h)), pl.BlockSpec((1, W), lambda h, c: (0, h))],
        out_specs=(pl.BlockSpec((CHUNK, W), lambda h, c: (c, h)), pl.BlockSpec((HB, None, HEAD, HEAD), lambda h, c: (h, c, 0, 0))),
        out_shape=(jax.ShapeDtypeStruct((T, H * HEAD), bf16), jax.ShapeDtypeStruct((H, nc, HEAD, HEAD), f32)),
        scratch_shapes=[pltpu.VMEM((HB, HEAD, HEAD), f32)],
        compiler_params=_params(("parallel", "arbitrary")),
    )(proj, proj, proj, proj, logits, gain)


def _hgrn_bwd(proj, logits, gain, states, dy, n_heads, name):
    T = proj.shape[0]
    nc = T // CHUNK
    H = n_heads
    HB = _group(H, HGRN_BWD_HEADS)
    W = HB * HEAD

    def col(k):
        return pl.BlockSpec((CHUNK, W), lambda h, c: (nc - 1 - c, k * (H // HB) + h))

    out_blk = pl.BlockSpec((CHUNK, W), lambda h, c: (nc - 1 - c, h))

    def body(q_ref, f_ref, i_ref, g_ref, l_ref, ga_ref, s_ref, dy_ref, dq_ref, df_ref, di_ref, dg_ref, dl_ref, dga_ref, dst):
        first = pl.program_id(1) == 0

        @pl.when(first)
        def _():
            dst[...] = jnp.zeros((HB, HEAD, HEAD), f32)
            dl_ref[...] = jnp.zeros((2, W), f32)
            dga_ref[...] = jnp.zeros((1, W), f32)

        cols = [slice(j * HEAD, (j + 1) * HEAD) for j in range(HB)]
        heads = lambda ref: [ref[:, cs] for cs in cols]
        _, vjp = jax.vjp(_hgrn_heads, heads(q_ref), heads(f_ref), heads(i_ref), heads(g_ref), heads(l_ref), heads(ga_ref),
                         [s_ref[j] for j in range(HB)])
        dq, df, di, dg, dl, dga, ds = vjp((heads(dy_ref), [dst[j] for j in range(HB)]))
        for j, cs in enumerate(cols):
            dq_ref[:, cs] = dq[j].astype(bf16)
            df_ref[:, cs] = df[j].astype(bf16)
            di_ref[:, cs] = di[j].astype(bf16)
            dg_ref[:, cs] = dg[j].astype(bf16)
            dst[j] = ds[j]
            dl_ref[:, cs] += dl[j]
            dga_ref[:, cs] += dga[j]

    act = jax.ShapeDtypeStruct((T, H * HEAD), bf16)
    return pl.pallas_call(
        body, name=name, grid=(H // HB, nc),
        in_specs=[col(0), col(1), col(2), col(3), pl.BlockSpec((2, W), lambda h, c: (0, h)), pl.BlockSpec((1, W), lambda h, c: (0, h)),
                  pl.BlockSpec((HB, None, HEAD, HEAD), lambda h, c: (h, nc - 1 - c, 0, 0)), out_blk],
        out_specs=(out_blk, out_blk, out_blk, out_blk, pl.BlockSpec((2, W), lambda h, c: (0, h)), pl.BlockSpec((1, W), lambda h, c: (0, h))),
        out_shape=(act, act, act, act, jax.ShapeDtypeStruct((2, H * HEAD), f32), jax.ShapeDtypeStruct((1, H * HEAD), f32)),
        scratch_shapes=[pltpu.VMEM((HB, HEAD, HEAD), f32)],
        compiler_params=_params(("parallel", "arbitrary")),
    )(proj, proj, proj, proj, logits, gain, states, dy)


def _rel_index():
    t = np.arange(CHUNK)[:, None]
    sp = np.arange(BAND * CHUNK)[None, :]
    dist = (N_PAST - sp // CHUNK) * CHUNK + t - sp % CHUNK
    return (np.clip(dist, -REL_FUTURE, REL_PAST) + REL_FUTURE).reshape(1, -1).astype(np.int32)


def _bias_table(rel_bias_pad, idx, name):
    H = rel_bias_pad.shape[0]
    n = idx.shape[1]
    tc = _tile(n, 4096)

    def body(rb_ref, idx_ref, o_ref):
        onehot = (lax.broadcasted_iota(jnp.int32, (N_REL_PAD, tc), 0) == idx_ref[...]).astype(f32)
        o_ref[...] = lax.dot_general(rb_ref[...], onehot, (((1,), (0,)), ((), ())), precision=HI, preferred_element_type=f32)

    return pl.pallas_call(
        body, name=name, grid=(n // tc,),
        in_specs=[pl.BlockSpec((H, N_REL_PAD), lambda j: (0, 0)), pl.BlockSpec((1, tc), lambda j: (0, j))],
        out_specs=pl.BlockSpec((H, tc), lambda j: (0, j)), out_shape=jax.ShapeDtypeStruct((H, n), f32),
        compiler_params=_params(("parallel",)),
    )(rel_bias_pad, idx)


def _bias_table_bwd(dbias, idx, name):
    H, n = dbias.shape
    tc = _tile(n, 4096)

    def body(d_ref, idx_ref, o_ref):
        onehot = (lax.broadcasted_iota(jnp.int32, (N_REL_PAD, tc), 0) == idx_ref[...]).astype(f32)
        part = lax.dot_general(d_ref[...], onehot, (((1,), (1,)), ((), ())), precision=HI, preferred_element_type=f32)

        @pl.when(pl.program_id(0) == 0)
        def _():
            o_ref[...] = part

        @pl.when(pl.program_id(0) > 0)
        def _():
            o_ref[...] += part

    return pl.pallas_call(
        body, name=name, grid=(n // tc,),
        in_specs=[pl.BlockSpec((H, tc), lambda j: (0, j)), pl.BlockSpec((1, tc), lambda j: (0, j))],
        out_specs=pl.BlockSpec((H, N_REL_PAD), lambda j: (0, 0)), out_shape=jax.ShapeDtypeStruct((H, N_REL_PAD), f32),
        compiler_params=_params(("arbitrary",)),
    )(dbias, idx)


def _head_norm(t, gain):
    return t * lax.rsqrt(jnp.mean(t * t, axis=-1, keepdims=True) + EPS) * gain


def _attn_chunks(qs, kbs, vbs, qg, bias, ns):
    r = range(len(qs))
    qh = [_head_norm(qs[j], qg) for j in r]
    s = [_dotb(qh[j], kbs[j], ((1,), (1,))) * (HEAD ** -0.5) + bias for j in r]
    col = lax.broadcasted_iota(jnp.int32, (1, BAND * CHUNK), 1)
    s = [jnp.where(ns[j] * CHUNK - PAD + col >= 0, s[j], NEG) for j in r]
    e = [jnp.exp(s[j] - jnp.max(s[j], axis=-1, keepdims=True)) for j in r]
    p = [e[j] / jnp.sum(e[j], axis=-1, keepdims=True) for j in r]
    return [_dotb(p[j], vbs[j], ((1,), (0,))) for j in r]


def _attn_fwd(proj, q_gain, k_gain, bias, off, n_heads, name):
    T = proj.shape[0]
    nc = T // CHUNK
    H = n_heads
    CB = _group(nc, ATTN_CHUNKS)
    o0 = off // HEAD
    full = lambda k: pl.BlockSpec((T, HEAD), lambda h, c: (0, o0 + k * H + h))
    vec = pl.BlockSpec((1, HEAD), lambda h, c: (0, 0))

    def body(q_ref, k_ref, v_ref, qg_ref, kg_ref, b_ref, y_ref, kp, vp):
        c = pl.program_id(1)

        @pl.when(c == 0)
        def _():
            kp[pl.ds(0, PAD), :] = jnp.zeros((PAD, HEAD), f32)
            vp[pl.ds(0, PAD), :] = jnp.zeros((PAD, HEAD), f32)
            kp[pl.ds(PAD, T), :] = _head_norm(k_ref[...], kg_ref[...])
            vp[pl.ds(PAD, T), :] = v_ref[...]

        ns = [c * CB + j for j in range(CB)]
        rows = [pl.ds(j * CHUNK, CHUNK) for j in range(CB)]
        bands = [pl.ds(pl.multiple_of(n * CHUNK, CHUNK), BAND * CHUNK) for n in ns]
        outs = _attn_chunks([q_ref[r, :] for r in rows], [kp[b, :] for b in bands], [vp[b, :] for b in bands], qg_ref[...], b_ref[...], ns)
        for r, o in zip(rows, outs):
            y_ref[r, :] = o.astype(bf16)

    return pl.pallas_call(
        body, name=name, grid=(H, nc // CB),
        in_specs=[pl.BlockSpec((CB * CHUNK, HEAD), lambda h, c: (c, o0 + h)), full(1), full(2), vec, vec,
                  pl.BlockSpec((None, CHUNK, BAND * CHUNK), lambda h, c: (h, 0, 0))],
        out_specs=pl.BlockSpec((CB * CHUNK, HEAD), lambda h, c: (c, h)), out_shape=jax.ShapeDtypeStruct((T, H * HEAD), bf16),
        scratch_shapes=[pltpu.VMEM((T + PAD, HEAD), f32), pltpu.VMEM((T + PAD, HEAD), f32)],
        compiler_params=_params(("parallel", "arbitrary")),
    )(proj, proj, proj, q_gain, k_gain, bias)


def _attn_bwd(proj, q_gain, k_gain, bias, dy, off, n_heads, name):
    T = proj.shape[0]
    nc = T // CHUNK
    H = n_heads
    CB = _group(nc, ATTN_CHUNKS)
    o0 = off // HEAD
    full = lambda k: pl.BlockSpec((T, HEAD), lambda h, c: (0, o0 + k * H + h))
    full_out = pl.BlockSpec((T, HEAD), lambda h, c: (0, h))
    vec = pl.BlockSpec((1, HEAD), lambda h, c: (0, 0))
    chunk_out = pl.BlockSpec((CB * CHUNK, HEAD), lambda h, c: (c, h))
    bias_blk = pl.BlockSpec((None, CHUNK, BAND * CHUNK), lambda h, c: (h, 0, 0))

    def body(q_ref, k_ref, v_ref, qg_ref, kg_ref, b_ref, dy_ref, dq_ref, dk_ref, dv_ref, db_ref, dqg_ref, dkg_ref, kp, vp, dkp, dvp):
        h = pl.program_id(0)
        c = pl.program_id(1)

        @pl.when(c == 0)
        def _():
            kp[pl.ds(0, PAD), :] = jnp.zeros((PAD, HEAD), f32)
            vp[pl.ds(0, PAD), :] = jnp.zeros((PAD, HEAD), f32)
            kp[pl.ds(PAD, T), :] = _head_norm(k_ref[...], kg_ref[...])
            vp[pl.ds(PAD, T), :] = v_ref[...]
            dkp[...] = jnp.zeros((T + PAD, HEAD), f32)
            dvp[...] = jnp.zeros((T + PAD, HEAD), f32)
            db_ref[...] = jnp.zeros((CHUNK, BAND * CHUNK), f32)

        @pl.when(jnp.logical_and(h == 0, c == 0))
        def _():
            dqg_ref[...] = jnp.zeros((1, HEAD), f32)
            dkg_ref[...] = jnp.zeros((1, HEAD), f32)

        ns = [c * CB + j for j in range(CB)]
        rows = [pl.ds(j * CHUNK, CHUNK) for j in range(CB)]
        bands = [pl.ds(pl.multiple_of(n * CHUNK, CHUNK), BAND * CHUNK) for n in ns]
        _, vjp = jax.vjp(functools.partial(_attn_chunks, ns=ns), [q_ref[r, :] for r in rows], [kp[b, :] for b in bands],
                         [vp[b, :] for b in bands], qg_ref[...], b_ref[...])
        dqs, dkbs, dvbs, dqg, db = vjp([dy_ref[r, :] for r in rows])
        db_ref[...] += db
        dqg_ref[...] += dqg
        for r, b, dq, dkb, dvb in zip(rows, bands, dqs, dkbs, dvbs):
            dq_ref[r, :] = dq.astype(bf16)
            dkp[b, :] += dkb
            dvp[b, :] += dvb

        @pl.when(c == nc // CB - 1)
        def _():
            _, nvjp = jax.vjp(_head_norm, k_ref[...], kg_ref[...])
            dk, dkg = nvjp(dkp[pl.ds(PAD, T), :])
            dk_ref[...] = dk.astype(bf16)
            dv_ref[...] = dvp[pl.ds(PAD, T), :].astype(bf16)
            dkg_ref[...] += dkg

    act = jax.ShapeDtypeStruct((T, H * HEAD), bf16)
    gvec = jax.ShapeDtypeStruct((1, HEAD), f32)
    pad_buf = pltpu.VMEM((T + PAD, HEAD), f32)
    return pl.pallas_call(
        body, name=name, grid=(H, nc // CB),
        in_specs=[pl.BlockSpec((CB * CHUNK, HEAD), lambda h, c: (c, o0 + h)), full(1), full(2), vec, vec, bias_blk, chunk_out],
        out_specs=(chunk_out, full_out, full_out, bias_blk, vec, vec),
        out_shape=(act, act, act, jax.ShapeDtypeStruct((H, CHUNK, BAND * CHUNK), f32), gvec, gvec),
        scratch_shapes=[pad_buf, pad_buf, pad_buf, pad_buf],
        compiler_params=_params(("arbitrary", "arbitrary")),
    )(proj, proj, proj, q_gain, k_gain, bias, dy)


def _position():
    x, y, c = lax.axis_index("x"), lax.axis_index("y"), lax.axis_index("c")
    return x, y, c, 4 * x + 2 * y + c


def _flip(v, bit):
    return 1 - v if bit else v


def _chips(x, y):
    return [(1 - x, y), (x, 1 - y), (1 - x, 1 - y)]


def _seq_gather(shards, name, collective_id):
    n = len(shards)

    def body(*refs):
        ins, outs = refs[:n], refs[n:2 * n]
        send, recv, loc = refs[2 * n:]
        x, y, c, me = _position()
        sib = (x, y, 1 - c)
        chips = _chips(x, y)
        barrier = pltpu.get_barrier_semaphore()
        for peer in [sib] + [(px, py, c) for px, py in chips]:
            pl.semaphore_signal(barrier, inc=1, device_id=peer, device_id_type=MESH)
        pl.semaphore_wait(barrier, 4)

        def copy(w, k, src, blk, to):
            return pltpu.make_async_remote_copy(src_ref=src, dst_ref=outs[w].at[blk], send_sem=send.at[7 * w + k], recv_sem=recv.at[7 * w + k],
                                                device_id=to, device_id_type=MESH)

        mine = [pltpu.make_async_copy(ins[w], outs[w].at[me], loc.at[w]) for w in range(n)]
        for cp in mine:
            cp.start()
        first = []
        for j, (px, py) in enumerate(chips):
            first += [copy(w, 1 + j, ins[w], me, (px, py, c)) for w in range(n)]
        first += [copy(w, 0, ins[w], me, sib) for w in range(n)]
        for cp in first:
            cp.start()
        passed = []
        for j, (px, py) in enumerate(chips):
            blk = 4 * px + 2 * py + c
            for w in range(n):
                copy(w, 1 + j, ins[w], blk, sib).wait_recv()
                fwd = copy(w, 4 + j, outs[w].at[blk], blk, sib)
                fwd.start()
                passed.append(fwd)
        for w in range(n):
            copy(w, 0, ins[w], 4 * x + 2 * y + (1 - c), sib).wait_recv()
        for j, (px, py) in enumerate(chips):
            for w in range(n):
                copy(w, 4 + j, ins[w], 4 * px + 2 * py + (1 - c), sib).wait_recv()
        for cp in first + passed:
            cp.wait_send()
        for cp in mine:
            cp.wait()

    return pl.kernel(
        body, out_type=tuple(jax.ShapeDtypeStruct((NDEV,) + s.shape, s.dtype) for s in shards),
        mesh=plsc.ScalarSubcoreMesh(axis_name="sequencer", num_cores=1), name=name,
        scratch_types=(pltpu.SemaphoreType.DMA((7 * n,)), pltpu.SemaphoreType.DMA((7 * n,)), pltpu.SemaphoreType.DMA((n,))),
        compiler_params=pltpu.CompilerParams(collective_id=collective_id),
    )(*shards)


NCHIP = 4


def _seq_pair_exchange(grads, name, collective_id, after=()):
    n, na = len(grads), len(after)

    def body(*refs):
        ins, outs = refs[:n], refs[n + na:2 * n + na]
        send, recv = refs[2 * n + na:]
        x, y, c, me = _position()
        sib = (x, y, 1 - c)
        barrier = pltpu.get_barrier_semaphore()
        pl.semaphore_signal(barrier, inc=1, device_id=sib, device_id_type=MESH)
        pl.semaphore_wait(barrier, 1)
        copies = [pltpu.make_async_remote_copy(src_ref=ins[w].at[2 * k + (1 - c)], dst_ref=outs[w].at[k], send_sem=send.at[NCHIP * w + k],
                                               recv_sem=recv.at[NCHIP * w + k], device_id=sib, device_id_type=MESH)
                  for w in range(n) for k in range(NCHIP)]
        for cp in copies:
            cp.start()
        for cp in copies:
            cp.wait_recv()
        for cp in copies:
            cp.wait_send()

    return pl.kernel(
        body, out_type=tuple(jax.ShapeDtypeStruct((NCHIP,) + g.shape[1:], g.dtype) for g in grads),
        mesh=plsc.ScalarSubcoreMesh(axis_name="sequencer", num_cores=1), name=name,
        scratch_types=(pltpu.SemaphoreType.DMA((NCHIP * n,)), pltpu.SemaphoreType.DMA((NCHIP * n,))),
        compiler_params=pltpu.CompilerParams(collective_id=collective_id),
    )(*grads, *after)


def _pair_add(grad, sib_part, name, after=()):
    _, R, C = grad.shape
    tr = _tile(R, 1024, 16)
    core = jnp.reshape(lax.axis_index("c"), (1,)).astype(jnp.int32)

    def body(c_ref, g_ref, s_ref, *rest):
        rest[-1][...] = (g_ref[...].astype(f32) + s_ref[...].astype(f32)).astype(bf16)

    blk = pl.BlockSpec((None, tr, C), lambda k, i, c_ref: (k, i, 0))
    return pl.pallas_call(
        body, name=name,
        grid_spec=pltpu.PrefetchScalarGridSpec(
            num_scalar_prefetch=1, grid=(NCHIP, R // tr),
            in_specs=[pl.BlockSpec((None, tr, C), lambda k, i, c_ref: (2 * k + c_ref[0], i, 0)), blk]
            + [pl.BlockSpec(memory_space=pl.ANY)] * len(after), out_specs=blk),
        out_shape=jax.ShapeDtypeStruct((NCHIP, R, C), bf16), compiler_params=_params(("parallel", "parallel")),
    )(core, grad, sib_part, *after)


def _seq_chip_exchange(sums, name, collective_id, after=()):
    n, na = len(sums), len(after)

    def body(*refs):
        ins, outs = refs[:n], refs[n + na:2 * n + na]
        send, recv, loc = refs[2 * n + na:]
        x, y, c, me = _position()
        chips = _chips(x, y)
        mine = 2 * x + y
        barrier = pltpu.get_barrier_semaphore()
        for px, py in chips:
            pl.semaphore_signal(barrier, inc=1, device_id=(px, py, c), device_id_type=MESH)
        pl.semaphore_wait(barrier, 3)
        local = [pltpu.make_async_copy(ins[w].at[mine], outs[w].at[mine], loc.at[w]) for w in range(n)]
        for cp in local:
            cp.start()
        sends, waits = [], []
        for j, (px, py) in enumerate(chips):
            for w in range(n):
                sems = dict(send_sem=send.at[3 * w + j], recv_sem=recv.at[3 * w + j], device_id=(px, py, c), device_id_type=MESH)
                sends.append(pltpu.make_async_remote_copy(src_ref=ins[w].at[2 * px + py], dst_ref=outs[w].at[mine], **sems))
                waits.append(pltpu.make_async_remote_copy(src_ref=ins[w].at[2 * px + py], dst_ref=outs[w].at[2 * px + py], **sems))
        for cp in sends:
            cp.start()
        for cp in waits:
            cp.wait_recv()
        for cp in sends:
            cp.wait_send()
        for cp in local:
            cp.wait()

    return pl.kernel(
        body, out_type=tuple(jax.ShapeDtypeStruct(s.shape, s.dtype) for s in sums),
        mesh=plsc.ScalarSubcoreMesh(axis_name="sequencer", num_cores=1), name=name,
        scratch_types=(pltpu.SemaphoreType.DMA((3 * n,)), pltpu.SemaphoreType.DMA((3 * n,)), pltpu.SemaphoreType.DMA((n,))),
        compiler_params=pltpu.CompilerParams(collective_id=collective_id),
    )(*sums, *after)


def _reduce_scatter(grads, tag, ids, after=(), add_after=()):
    sib_parts = _seq_pair_exchange(grads, "pair_exchange_" + tag, ids[0], after=after)
    sums = [_pair_add(g, s, "pair_add_%s%d" % (tag, i), after=add_after) for i, (g, s) in enumerate(zip(grads, sib_parts))]
    return _seq_chip_exchange(sums, "chip_exchange_" + tag, ids[1]), sums


def _small_all_reduce(v, name):
    R, C = v.shape

    def body(v_ref, o_ref, buf, send, recv):
        x, y, c, me = _position()
        buf[me] = v_ref[...]
        sends, waits = [], []
        for r in range(1, NDEV):
            px, py, pc = _flip(x, r & 4), _flip(y, r & 2), _flip(c, r & 1)
            peer = 4 * px + 2 * py + pc
            sends.append(pltpu.make_async_remote_copy(src_ref=v_ref, dst_ref=buf.at[me], send_sem=send.at[r - 1], recv_sem=recv.at[r - 1],
                                                      device_id=(px, py, pc), device_id_type=MESH))
            waits.append(pltpu.make_async_remote_copy(src_ref=v_ref, dst_ref=buf.at[peer], send_sem=send.at[r - 1], recv_sem=recv.at[r - 1],
                                                      device_id=(px, py, pc), device_id_type=MESH))
        for cp in sends:
            cp.start()
        for cp in waits:
            cp.wait_recv()
        for cp in sends:
            cp.wait_send()
        acc = buf[0]
        for i in range(1, NDEV):
            acc = acc + buf[i]
        o_ref[...] = acc

    vm = pl.BlockSpec(memory_space=pltpu.VMEM)
    return pl.pallas_call(
        body, name=name, in_specs=[vm], out_specs=vm, out_shape=jax.ShapeDtypeStruct((R, C), f32),
        scratch_shapes=[pltpu.VMEM((NDEV, R, C), f32), pltpu.SemaphoreType.DMA((7,)), pltpu.SemaphoreType.DMA((7,))],
    )(v)


def _adamw_math(w, g, m, v):
    m = ADAM_B1 * m + (1.0 - ADAM_B1) * g
    v = ADAM_B2 * v + (1.0 - ADAM_B2) * (g * g)
    m_hat = m / (1.0 - ADAM_B1 ** ADAM_STEP)
    v_hat = v / (1.0 - ADAM_B2 ** ADAM_STEP)
    delta = -ADAM_LR * (m_hat / (jnp.sqrt(v_hat) + ADAM_EPS) + ADAM_WD * w)
    return delta, m, v


def _adamw_parts(w, m, v, parts, name, after=()):
    R, C = w.shape
    tr = _tile(R, 128, 16)
    blk = pl.BlockSpec((tr, C), lambda i: (i, 0))

    def body(w_ref, m_ref, v_ref, p_ref, *rest):
        g_ref, d_ref, mo_ref, vo_ref = rest[len(after):]
        g = p_ref[0].astype(f32)
        for i in range(1, NCHIP):
            g = g + p_ref[i].astype(f32)
        d, mn, vn = _adamw_math(w_ref[...], g, m_ref[...], v_ref[...])
        g_ref[...] = g
        d_ref[...] = d
        mo_ref[...] = mn
        vo_ref[...] = vn

    shp = jax.ShapeDtypeStruct((R, C), f32)
    return pl.pallas_call(
        body, name=name, grid=(R // tr,),
        in_specs=[blk, blk, blk, pl.BlockSpec((NCHIP, tr, C), lambda i: (0, i, 0))] + [pl.BlockSpec(a.shape, lambda i: (0, 0)) for a in after],
        out_specs=(blk, blk, blk, blk), out_shape=(shp, shp, shp, shp), compiler_params=_params(("parallel",)),
    )(w, m, v, parts, *after)


def _adamw_small(w, g, m, v, name):
    def body(w_ref, g_ref, m_ref, v_ref, d_ref, mo_ref, vo_ref):
        d, mn, vn = _adamw_math(w_ref[...], g_ref[...], m_ref[...], v_ref[...])
        d_ref[...] = d
        mo_ref[...] = mn
        vo_ref[...] = vn

    shp = jax.ShapeDtypeStruct(w.shape, f32)
    return pl.pallas_call(body, name=name, out_shape=(shp, shp, shp))(w, g, m, v)


SMALL_COLS = 1024


def _pack(arrs):
    flat = jnp.concatenate([a.reshape(-1) for a in arrs])
    rows = -(-flat.shape[0] // (8 * SMALL_COLS)) * 8
    return jnp.pad(flat, (0, rows * SMALL_COLS - flat.shape[0])).reshape(rows, SMALL_COLS)


def _unpack(packed, like):
    flat = packed.reshape(-1)
    out, pos = [], 0
    for a in like:
        out.append(flat[pos:pos + a.size].reshape(a.shape))
        pos += a.size
    return out


def kernel(x, w_in, b_gate, norm_mix, norm_ffn, hgrn_lb_logits, hgrn_out_gain, q_gain, k_gain, rel_bias, w_proj_a, w_proj_b, w_out, w_ffn_in, w_ffn_out, loss_target, m_w_in, m_b_gate, m_norm_mix, m_norm_ffn, m_hgrn_lb_logits, m_hgrn_out_gain, m_q_gain, m_k_gain, m_rel_bias, m_w_proj_a, m_w_proj_b, m_w_out, m_w_ffn_in, m_w_ffn_out, v_w_in, v_b_gate, v_norm_mix, v_norm_ffn, v_hgrn_lb_logits, v_hgrn_out_gain, v_q_gain, v_k_gain, v_rel_bias, v_w_proj_a, v_w_proj_b, v_w_out, v_w_ffn_in, v_w_ffn_out):
    xs = x[0]
    target = loss_target[0]
    T, D = xs.shape
    d_a = hgrn_out_gain.shape[-1]
    H = d_a // HEAD
    d_b = d_a
    off_b = 4 * d_a
    off_g = off_b + 3 * d_b
    assert rel_bias.shape[1] == H and T % CHUNK == 0 and T // CHUNK > N_PAST

    big_w = [w_in[0], w_proj_a[0], w_proj_b[0], w_out[0], w_ffn_in[0], w_ffn_out[0]]
    big_m = [m_w_in[0], m_w_proj_a[0], m_w_proj_b[0], m_w_out[0], m_w_ffn_in[0], m_w_ffn_out[0]]
    big_v = [v_w_in[0], v_w_proj_a[0], v_w_proj_b[0], v_w_out[0], v_w_ffn_in[0], v_w_ffn_out[0]]

    sh = [w.astype(bf16) for w in big_w]
    (g_in,) = _seq_gather(sh[0:1], "gather_a", 1)
    g_pa, g_pb, g_out = _seq_gather(sh[1:4], "gather_b", 2)
    (g_fin,) = _seq_gather(sh[4:5], "gather_c", 3)
    (g_fout,) = _seq_gather(sh[5:6], "gather_d", 4)

    h = _rms_fwd(xs, None, norm_mix, "rms_mix")
    proj = _mm(h, g_in, mode="nn", b_blocked=True, name="mm_proj")
    y_a, states = _hgrn_fwd(proj, hgrn_lb_logits, hgrn_out_gain, H, "hgrn_fwd")
    idx = jnp.asarray(_rel_index())
    rb_pad = jnp.pad(rel_bias[0], ((0, 0), (0, N_REL_PAD - N_REL)))
    bias = _bias_table(rb_pad, idx, "bias_table").reshape(H, CHUNK, BAND * CHUNK)
    y_b = _attn_fwd(proj, q_gain, k_gain, bias, off_b, H, "attn_fwd")
    wg_out = g_out.reshape(-1, g_out.shape[-1])
    wg_fout = g_fout.reshape(-1, g_fout.shape[-1])
    pa = _mm(y_a, g_pa, mode="nn", b_blocked=True, tm=2048, name="mm_proj_a")
    pb = _mm(y_b, g_pb, mode="nn", b_blocked=True, tm=2048, name="mm_proj_b")
    merged = _merge_fwd(pa, pb, proj, b_gate, off_g, "merge_fwd")
    mo = _mm(merged, wg_out, mode="nn", name="mm_out")
    x1, h2 = _rms_fwd(xs, mo, norm_ffn, "rms_ffn")
    gu, act = _ffn_in_swiglu(h2, g_fin, "mm_ffn_in")
    fo = _mm(act, wg_fout, mode="nn", tk=2816, name="mm_ffn_out")
    dy, loss_acc = _loss_head(x1, fo, target, "loss_head")
    loss_part = loss_acc[0:1, 0:1] * (0.5 / D)

    gw_fout = _mm(act, dy, mode="tn", out_dtype=bf16, tm=1408, name="mm_gw_ffn_out")
    dgu = _d_act_swiglu(dy, wg_fout, gu, "mm_d_act")
    gw_fin = _mm(h2, dgu, mode="tn", b_stacked=True, out_blocked=True, out_dtype=bf16, tn=g_fin.shape[-1], name="mm_gw_ffn_in")
    dh2 = _mm(dgu, g_fin, mode="nt", a_stacked=True, b_blocked=True, name="mm_d_h2")
    (p_fout, p_fin), sums_a = _reduce_scatter([gw_fout.reshape(NDEV, -1, D), gw_fin], "a", (5, 6), add_after=(dh2,))
    dx1, g_norm_ffn = _rms_bwd(x1, norm_ffn, dh2, dy, "rms_ffn_bwd", after=sums_a)

    dmerged = _mm(dx1, wg_out, mode="nt", name="mm_d_merged")
    gw_out = _mm(merged, dx1, mode="tn", out_dtype=bf16, name="mm_gw_out")
    dpa, dgl_a, gb_a = _branch_bwd(dmerged, pa, proj, b_gate, off_g, 0, "branch_a_bwd")
    dpb, dgl_b, gb_b = _branch_bwd(dmerged, pb, proj, b_gate, off_g + D, D, "branch_b_bwd")
    dy_a = _mm(dpa, g_pa, mode="nt", b_blocked=True, tm=2048, name="mm_d_ya")
    dy_b = _mm(dpb, g_pb, mode="nt", b_blocked=True, tm=2048, name="mm_d_yb")
    gw_pa = _mm(y_a, dpa, mode="tn", out_blocked=True, out_dtype=bf16, tn=g_pa.shape[-1], name="mm_gw_proj_a")
    gw_pb = _mm(y_b, dpb, mode="tn", out_blocked=True, out_dtype=bf16, tn=g_pb.shape[-1], name="mm_gw_proj_b")

    dq_a, df_a, di_a, dg_a, g_logits, g_gain = _hgrn_bwd(proj, hgrn_lb_logits, hgrn_out_gain, states, dy_a, H, "hgrn_bwd")
    dq_b, dk_b, dv_b, dbias, g_qg, g_kg = _attn_bwd(proj, q_gain, k_gain, bias, dy_b, off_b, H, "attn_bwd")
    g_rel_pad = _bias_table_bwd(dbias.reshape(H, -1), idx, "bias_table_bwd")
    g_rel = g_rel_pad[:, :N_REL]
    dproj = jnp.concatenate([dq_a, df_a, di_a, dg_a, dq_b, dk_b, dv_b, dgl_a, dgl_b], axis=1)
    (p_out, p_pa, p_pb), sums_b = _reduce_scatter([gw_out.reshape(NDEV, -1, D), gw_pa, gw_pb], "b", (7, 8), after=(g_gain, p_fout, p_fin), add_after=(dy_b,))
    gw_in = _mm(h, dproj, mode="tn", out_blocked=True, out_dtype=bf16, tn=g_in.shape[-1], name="mm_gw_in", after=sums_b)
    (p_in,), sums_c = _reduce_scatter([gw_in], "c", (9, 10), after=(p_out, p_pa, p_pb), add_after=(g_rel_pad,))
    dh = _mm(dproj, g_in, mode="nt", b_blocked=True, name="mm_d_h", after=sums_c)
    grad_x, g_norm_mix = _rms_bwd(xs, norm_mix, dh, dx1, "rms_mix_bwd")

    parts = [p_in, p_pa, p_pb, p_out, p_fin, p_fout]
    names = ["w_in", "w_proj_a", "w_proj_b", "w_out", "w_ffn_in", "w_ffn_out"]
    big = {}
    for nm, w, m, v, p in zip(names, big_w, big_m, big_v, parts):
        big[nm] = [o[None] for o in _adamw_parts(w, m, v, p, "adamw_" + nm, after=() if nm == "w_in" else (g_norm_mix,))]

    small_names = ["b_gate", "norm_mix", "norm_ffn", "hgrn_lb_logits", "hgrn_out_gain", "q_gain", "k_gain", "rel_bias"]
    small_w = [b_gate, norm_mix, norm_ffn, hgrn_lb_logits, hgrn_out_gain, q_gain, k_gain, rel_bias]
    small_m = [m_b_gate, m_norm_mix, m_norm_ffn, m_hgrn_lb_logits, m_hgrn_out_gain, m_q_gain, m_k_gain, m_rel_bias]
    small_v = [v_b_gate, v_norm_mix, v_norm_ffn, v_hgrn_lb_logits, v_hgrn_out_gain, v_q_gain, v_k_gain, v_rel_bias]
    small_g = [jnp.concatenate([gb_a, gb_b], axis=1), g_norm_mix, g_norm_ffn, g_logits, g_gain, g_qg, g_kg, g_rel[None], loss_part]
    g_sum = _small_all_reduce(_pack(small_g), "reduce_small")
    loss = _unpack(g_sum, small_g)[-1].reshape(())
    d_s, m_s, v_s = _adamw_small(_pack(small_w), g_sum, _pack(small_m), _pack(small_v), "adamw_small")
    small = {}
    for nm, g, d, m, v in zip(small_names, _unpack(g_sum, small_w), _unpack(d_s, small_w), _unpack(m_s, small_w), _unpack(v_s, small_w)):
        small[nm] = [g, d, m, v]

    order = ["w_in", "b_gate", "norm_mix", "norm_ffn", "hgrn_lb_logits", "hgrn_out_gain", "q_gain", "k_gain", "rel_bias",
             "w_proj_a", "w_proj_b", "w_out", "w_ffn_in", "w_ffn_out"]
    res = {**big, **small}
    outs = [loss, grad_x[None]]
    for k in range(4):
        outs += [res[nm][k] for nm in order]
    return tuple(outs)
```

```python
import functools

import numpy as np
import jax
import jax.numpy as jnp
from jax import lax
from jax.experimental import pallas as pl
from jax.experimental.pallas import tpu as pltpu
from jax.experimental.pallas import tpu_sc as plsc

f32 = jnp.float32
bf16 = jnp.bfloat16
HI = lax.Precision.HIGHEST
MESH = pl.DeviceIdType.MESH
AXES = ("x", "y", "c")
NDEV = 8

CHUNK = 64
HEAD = 128
SUB = 8
HGRN_BWD_HEADS = 8
ATTN_CHUNKS = 8
N_PAST = 8
BAND = N_PAST + 1
PAD = N_PAST * CHUNK
REL_FUTURE = CHUNK - 1
REL_PAST = 2 * CHUNK - 1
N_REL = REL_FUTURE + REL_PAST + 1
N_REL_PAD = 256
EPS = 1e-6
NEG = -1e30

ADAM_LR = 0.001
ADAM_B1 = 0.9
ADAM_B2 = 0.999
ADAM_EPS = 1e-08
ADAM_WD = 0.01
ADAM_STEP = 10

VMEM_LIMIT = 56 * 1024 * 1024


def _params(sem=None):
    return pltpu.CompilerParams(dimension_semantics=sem, vmem_limit_bytes=VMEM_LIMIT)


def _tile(n, pref, unit=128):
    if n <= pref:
        return n
    t = (pref // unit) * unit
    while t >= unit:
        if n % t == 0:
            return t
        t -= unit
    return n


_sigmoid = jax.nn.sigmoid


def _mm(a, b, *, mode, name, b_blocked=False, out_blocked=False, out_dtype=f32, tm=1024, tn=1024, tk=2048, after=(),
        a_stacked=False, b_stacked=False):
    if a_stacked:
        assert mode == "nt"
        M, K = a.shape[1], 2 * a.shape[2]
    elif mode == "tn":
        K, M = a.shape
    else:
        M, K = a.shape
    if b_blocked:
        nb, mid, cb = b.shape
        if mode == "nn":
            assert mid == K
            N, tn = nb * cb, cb
        else:
            assert mode == "nt" and nb * cb == K
            N, tk = mid, cb
    elif b_stacked:
        assert mode == "tn"
        N = 2 * b.shape[2]
    else:
        N = b.shape[1] if mode in ("nn", "tn") else b.shape[0]
    tm = _tile(M, tm)
    tn = tn if (b_blocked and mode == "nn") or out_blocked else _tile(N, tn)
    tk = tk if b_blocked and mode == "nt" else _tile(K, tk)
    assert M % tm == 0 and N % tn == 0 and K % tk == 0
    nk = K // tk
    grid = (M // tm, N // tn, nk)
    if a_stacked:
        ka = K // 2 // tk
        a_spec = pl.BlockSpec((None, tm, tk), lambda i, j, k: (k // ka, i, k % ka))
    elif mode == "tn":
        a_spec = pl.BlockSpec((tk, tm), lambda i, j, k: (k, i))
    else:
        a_spec = pl.BlockSpec((tm, tk), lambda i, j, k: (i, k))
    if mode == "nn":
        b_spec = pl.BlockSpec((None, tk, cb), lambda i, j, k: (j, k, 0)) if b_blocked else pl.BlockSpec((tk, tn), lambda i, j, k: (k, j))
    elif mode == "nt":
        b_spec = pl.BlockSpec((None, tn, cb), lambda i, j, k: (k, j, 0)) if b_blocked else pl.BlockSpec((tn, tk), lambda i, j, k: (j, k))
    elif b_stacked:
        nh = N // 2 // tn
        b_spec = pl.BlockSpec((None, tk, tn), lambda i, j, k: (j // nh, k, j % nh))
    else:
        b_spec = pl.BlockSpec((tk, tn), lambda i, j, k: (k, j))
    if out_blocked:
        out_shape = jax.ShapeDtypeStruct((N // tn, M, tn), out_dtype)
        o_spec = pl.BlockSpec((None, tm, tn), lambda i, j, k: (j, i, 0))
    else:
        out_shape = jax.ShapeDtypeStruct((M, N), out_dtype)
        o_spec = pl.BlockSpec((tm, tn), lambda i, j, k: (i, j))
    dims = {"nn": ((1,), (0,)), "nt": ((1,), (1,)), "tn": ((0,), (0,))}[mode]

    def body(a_ref, b_ref, *rest):
        o_ref, acc = rest[len(after)], rest[len(after) + 1:]
        p = lax.dot_general(a_ref[...].astype(bf16), b_ref[...].astype(bf16), (dims, ((), ())), preferred_element_type=f32)
        if nk == 1:
            o_ref[...] = p.astype(out_dtype)
        else:
            acc_ref = acc[0]
            k = pl.program_id(2)

            @pl.when(k == 0)
            def _():
                acc_ref[...] = p

            @pl.when(k > 0)
            def _():
                acc_ref[...] += p

            @pl.when(k == nk - 1)
            def _():
                o_ref[...] = acc_ref[...].astype(out_dtype)

    return pl.pallas_call(
        body, name=name, grid=grid, in_specs=[a_spec, b_spec] + [pl.BlockSpec(memory_space=pl.ANY)] * len(after), out_specs=o_spec,
        out_shape=out_shape, scratch_shapes=[pltpu.VMEM((tm, tn), f32)] if nk > 1 else [],
        compiler_params=_params(("parallel", "parallel", "arbitrary")),
    )(a, b, *after)


def _rms_fwd(x, res, gain, name):
    T, D = x.shape
    tr = _tile(T, 256, 8)
    row = pl.BlockSpec((tr, D), lambda i: (i, 0))
    vec = pl.BlockSpec((1, D), lambda i: (0, 0))

    def body(*refs):
        if res is None:
            x_ref, g_ref, h_ref = refs
            xs = x_ref[...]
        else:
            x_ref, r_ref, g_ref, xs_ref, h_ref = refs
            xs = x_ref[...] + r_ref[...]
            xs_ref[...] = xs
        r = lax.rsqrt(jnp.mean(xs * xs, axis=-1, keepdims=True) + EPS)
        h_ref[...] = (xs * r * g_ref[...]).astype(bf16)

    h_shape = jax.ShapeDtypeStruct((T, D), bf16)
    if res is None:
        return pl.pallas_call(body, name=name, grid=(T // tr,), in_specs=[row, vec], out_specs=row, out_shape=h_shape,
                              compiler_params=_params(("parallel",)))(x, gain)
    return pl.pallas_call(body, name=name, grid=(T // tr,), in_specs=[row, row, vec], out_specs=(row, row),
                          out_shape=(jax.ShapeDtypeStruct((T, D), f32), h_shape), compiler_params=_params(("parallel",)))(x, res, gain)


def _rms_bwd(xs, gain, dh, extra, name, after=()):
    T, D = xs.shape
    tr = _tile(T, 256, 8)
    row = pl.BlockSpec((tr, D), lambda i: (i, 0))
    vec = pl.BlockSpec((1, D), lambda i: (0, 0))

    def body(x_ref, g_ref, dh_ref, e_ref, *rest):
        dx_ref, dg_ref = rest[len(after):]
        x = x_ref[...]
        r = lax.rsqrt(jnp.mean(x * x, axis=-1, keepdims=True) + EPS)
        xhat = x * r
        dh_v = dh_ref[...]
        gd = dh_v * g_ref[...]
        dx_ref[...] = e_ref[...] + r * (gd - xhat * jnp.mean(gd * xhat, axis=-1, keepdims=True))
        part = jnp.sum(dh_v * xhat, axis=0, keepdims=True)

        @pl.when(pl.program_id(0) == 0)
        def _():
            dg_ref[...] = part

        @pl.when(pl.program_id(0) > 0)
        def _():
            dg_ref[...] += part

    return pl.pallas_call(body, name=name, grid=(T // tr,),
                          in_specs=[row, vec, row, row] + [pl.BlockSpec(memory_space=pl.ANY)] * len(after), out_specs=(row, vec),
                          out_shape=(jax.ShapeDtypeStruct((T, D), f32), jax.ShapeDtypeStruct((1, D), f32)),
                          compiler_params=_params(("arbitrary",)))(xs, gain, dh, extra, *after)


def _merge_fwd(pa, pb, proj, b_gate, off, name):
    T, D = pa.shape
    tr, tc = _tile(T, 512, 8), _tile(D, 512)
    oa, ob, nb = off // tc, (off + D) // tc, D // tc
    blk = pl.BlockSpec((tr, tc), lambda i, j: (i, j))

    def body(pa_ref, pb_ref, ga_ref, gb_ref, ba_ref, bb_ref, o_ref):
        ga = _sigmoid(ga_ref[...] + ba_ref[...])
        gb = _sigmoid(gb_ref[...] + bb_ref[...])
        o_ref[...] = (ga * pa_ref[...] + gb * pb_ref[...]).astype(bf16)

    return pl.pallas_call(
        body, name=name, grid=(T // tr, nb),
        in_specs=[blk, blk, pl.BlockSpec((tr, tc), lambda i, j: (i, oa + j)), pl.BlockSpec((tr, tc), lambda i, j: (i, ob + j)),
                  pl.BlockSpec((1, tc), lambda i, j: (0, j)), pl.BlockSpec((1, tc), lambda i, j: (0, nb + j))],
        out_specs=blk, out_shape=jax.ShapeDtypeStruct((T, D), bf16), compiler_params=_params(("parallel", "parallel")),
    )(pa, pb, proj, proj, b_gate, b_gate)


def _branch_bwd(dm, p, proj, b_gate, off, boff, name):
    T, D = p.shape
    tr, tc = _tile(T, 512, 8), _tile(D, 512)
    og, obias = off // tc, boff // tc
    blk = pl.BlockSpec((tr, tc), lambda j, i: (i, j))
    vec = pl.BlockSpec((1, tc), lambda j, i: (0, j))

    def body(dm_ref, p_ref, gl_ref, b_ref, dp_ref, dgl_ref, db_ref):
        g = _sigmoid(gl_ref[...] + b_ref[...])
        dm_v = dm_ref[...]
        dp_ref[...] = (dm_v * g).astype(bf16)
        dgl = dm_v * p_ref[...] * g * (1.0 - g)
        dgl_ref[...] = dgl.astype(bf16)
        part = jnp.sum(dgl, axis=0, keepdims=True)

        @pl.when(pl.program_id(1) == 0)
        def _():
            db_ref[...] = part

        @pl.when(pl.program_id(1) > 0)
        def _():
            db_ref[...] += part

    return pl.pallas_call(
        body, name=name, grid=(D // tc, T // tr),
        in_specs=[blk, blk, pl.BlockSpec((tr, tc), lambda j, i: (i, og + j)), pl.BlockSpec((1, tc), lambda j, i: (0, obias + j))],
        out_specs=(blk, blk, vec),
        out_shape=(jax.ShapeDtypeStruct((T, D), bf16), jax.ShapeDtypeStruct((T, D), bf16), jax.ShapeDtypeStruct((1, D), f32)),
        compiler_params=_params(("parallel", "arbitrary")),
    )(dm, p, proj, b_gate)


def _ffn_in_swiglu(h, w, name):
    T, K = h.shape
    nb, _, cb = w.shape
    half = nb // 2
    F = half * cb
    tm = _tile(T, 512, 8)

    def body(h_ref, wg_ref, wu_ref, gu_ref, act_ref):
        dn = (((1,), (0,)), ((), ()))
        hv = h_ref[...]
        g = lax.dot_general(hv, wg_ref[...], dn, preferred_element_type=f32)
        u = lax.dot_general(hv, wu_ref[...], dn, preferred_element_type=f32)
        gu_ref[0] = g
        gu_ref[1] = u
        act_ref[...] = (g * _sigmoid(g) * u).astype(bf16)

    return pl.pallas_call(
        body, name=name, grid=(half, T // tm),
        in_specs=[pl.BlockSpec((tm, K), lambda j, i: (i, 0)), pl.BlockSpec((None, K, cb), lambda j, i: (j, 0, 0)),
                  pl.BlockSpec((None, K, cb), lambda j, i: (j + half, 0, 0))],
        out_specs=(pl.BlockSpec((2, tm, cb), lambda j, i: (0, i, j)), pl.BlockSpec((tm, cb), lambda j, i: (i, j))),
        out_shape=(jax.ShapeDtypeStruct((2, T, F), f32), jax.ShapeDtypeStruct((T, F), bf16)),
        compiler_params=_params(("parallel", "parallel")),
    )(h, w, w)


def _d_act_swiglu(dy, w_out, gu, name):
    T, D = dy.shape
    F = w_out.shape[0]
    tm, tn = _tile(T, 512, 8), _tile(F, 1408)

    def body(dy_ref, w_ref, gu_ref, o_ref):
        d = lax.dot_general(dy_ref[...].astype(bf16), w_ref[...], (((1,), (1,)), ((), ())), preferred_element_type=f32)
        g = gu_ref[0]
        s = _sigmoid(g)
        o_ref[0] = (d * gu_ref[1] * s * (1.0 + g * (1.0 - s))).astype(bf16)
        o_ref[1] = (d * g * s).astype(bf16)

    blk = pl.BlockSpec((2, tm, tn), lambda j, i: (0, i, j))
    return pl.pallas_call(
        body, name=name, grid=(F // tn, T // tm),
        in_specs=[pl.BlockSpec((tm, D), lambda j, i: (i, 0)), pl.BlockSpec((tn, D), lambda j, i: (j, 0)), blk],
        out_specs=blk, out_shape=jax.ShapeDtypeStruct((2, T, F), bf16), compiler_params=_params(("parallel", "parallel")),
    )(dy, w_out, gu)


def _loss_head(x1, fo, target, name):
    T, D = x1.shape
    tr = _tile(T, 256, 8)
    row = pl.BlockSpec((tr, D), lambda i: (i, 0))
    acc = pl.BlockSpec((8, 128), lambda i: (0, 0))

    def body(x_ref, f_ref, t_ref, dy_ref, l_ref):
        d = x_ref[...] + f_ref[...] - t_ref[...]
        dy_ref[...] = d * (1.0 / D)
        part = jnp.sum(jnp.sum(d * d, axis=1, keepdims=True), axis=0, keepdims=True)

        @pl.when(pl.program_id(0) == 0)
        def _():
            l_ref[...] = jnp.zeros((8, 128), f32)

        l_ref[...] += part

    return pl.pallas_call(body, name=name, grid=(T // tr,), in_specs=[row, row, row], out_specs=(row, acc),
                          out_shape=(jax.ShapeDtypeStruct((T, D), f32), jax.ShapeDtypeStruct((8, 128), f32)),
                          compiler_params=_params(("arbitrary",)))(x1, fo, target)


_DIMS = {"nn": ((1,), (0,)), "nt": ((1,), (1,)), "tn": ((0,), (0,))}
_MODE = {v: k for k, v in _DIMS.items()}


def _dot_bf16(a, b, mode):
    return lax.dot_general(a.astype(bf16), b.astype(bf16), (_DIMS[mode], ((), ())), preferred_element_type=f32)


@functools.partial(jax.custom_vjp, nondiff_argnums=(2,))
def _dotm(a, b, mode):
    return _dot_bf16(a, b, mode)


def _dotm_fwd(a, b, mode):
    return _dot_bf16(a, b, mode), (a, b)


def _dotm_bwd(mode, res, g):
    a, b = res
    if mode == "nn":
        return _dot_bf16(g, b, "nt"), _dot_bf16(a, g, "tn")
    if mode == "nt":
        return _dot_bf16(g, b, "nn"), _dot_bf16(g, a, "tn")
    return _dot_bf16(b, g, "nt"), _dot_bf16(a, g, "nn")


_dotm.defvjp(_dotm_fwd, _dotm_bwd)


def _dotb(a, b, dims):
    return _dotm(a, b, _MODE[dims])


def _triangle_sum(v, lower):
    row = lax.broadcasted_iota(jnp.int32, (CHUNK, CHUNK), 0)
    col = lax.broadcasted_iota(jnp.int32, (CHUNK, CHUNK), 1)
    tri = ((col <= row) if lower else (col >= row)).astype(bf16)

    def top(t):
        return lax.bitcast_convert_type(lax.bitcast_convert_type(t, jnp.uint32) & jnp.uint32(0xFFFF0000), f32)

    hi = top(v)
    mid = top(v - hi)
    low = (v - hi) - mid
    hi, mid, low = hi.astype(bf16), mid.astype(bf16), low.astype(bf16)
    dn = (((1,), (0,)), ((), ()))
    return (lax.dot_general(tri, hi, dn, preferred_element_type=f32) + lax.dot_general(tri, mid, dn, preferred_element_type=f32)
            + lax.dot_general(tri, low, dn, preferred_element_type=f32))


@jax.custom_vjp
def _cumsum_rows(v):
    return _triangle_sum(v, True)


_cumsum_rows.defvjp(lambda v: (_triangle_sum(v, True), None), lambda _, g: (_triangle_sum(g, False),))


def _hgrn_heads(q, fl, iv, g, logits, gain, st):
    r = range(len(q))
    lb = [jax.nn.softmax(logits[j], axis=0)[0:1] for j in r]
    f = [lb[j] + (1.0 - lb[j]) * _sigmoid(fl[j]) for j in r]
    lf = [jnp.log(f[j]) for j in r]
    kk = [1.0 - f[j] for j in r]
    qs = [q[j] * _sigmoid(q[j]) for j in r]
    b = [_cumsum_rows(lf[j]) for j in r]
    b_last = [jnp.sum(lf[j], axis=0, keepdims=True) for j in r]
    o = [_dotb(qs[j] * jnp.exp(b[j]), st[j], ((1,), (1,))) for j in r]
    r3 = lax.broadcasted_iota(jnp.int32, (SUB, SUB, HEAD), 0)
    c3 = lax.broadcasted_iota(jnp.int32, (SUB, SUB, HEAD), 1)
    parts = [[] for _ in r]
    for i in range(CHUNK // SUB):
        lo, hi = i * SUB, (i + 1) * SUB
        bi = [b[j][lo:hi] for j in r]
        dec = [jnp.exp(jnp.where(c3 <= r3, bi[j][:, None, :] - bi[j][None, :, :], -jnp.inf)) for j in r]
        s = [jnp.sum(qs[j][lo:hi][:, None, :] * kk[j][lo:hi][None, :, :] * dec[j], axis=-1) for j in r]
        if i > 0:
            anchor = [jnp.max(bi[j], axis=0, keepdims=True) for j in r]
            qa = [qs[j][lo:hi] * jnp.exp(bi[j] - anchor[j]) for j in r]
            kd = [kk[j][:lo] * jnp.exp(anchor[j] - b[j][:lo]) for j in r]
            s = [jnp.concatenate([_dotb(qa[j], kd[j], ((1,), (1,))), s[j]], axis=1) for j in r]
        for j in r:
            parts[j].append(_dotb(s[j], iv[j][:hi], ((1,), (0,))))
    o = [o[j] + jnp.concatenate(parts[j], axis=0) for j in r]
    st_new = [st[j] * jnp.exp(b_last[j]) + _dotb(iv[j], kk[j] * jnp.exp(b_last[j] - b[j]), ((0,), (0,))) for j in r]
    o = [o[j] * lax.rsqrt(jnp.mean(o[j] * o[j], axis=-1, keepdims=True) + EPS) for j in r]
    o = [o[j] * gain[j] * (g[j] * _sigmoid(g[j])) for j in r]
    return o, st_new


def _group(n, pref):
    while n % pref:
        pref //= 2
    return pref


def _hgrn_fwd(proj, logits, gain, n_heads, name):
    T = proj.shape[0]
    nc = T // CHUNK
    H = n_heads
    HB = _group(H, 8)
    W = HB * HEAD

    def col(k):
        return pl.BlockSpec((CHUNK, W), lambda h, c: (c, k * (H // HB) + h))

    def body(q_ref, f_ref, i_ref, g_ref, l_ref, ga_ref, y_ref, s_ref, st):
        @pl.when(pl.program_id(1) == 0)
        def _():
            st[...] = jnp.zeros((HB, HEAD, HEAD), f32)

        cols = [slice(j * HEAD, (j + 1) * HEAD) for j in range(HB)]
        heads = lambda ref: [ref[:, cs] for cs in cols]
        s_ref[...] = st[...]
        o, st_new = _hgrn_heads(heads(q_ref), heads(f_ref), heads(i_ref), heads(g_ref), heads(l_ref), heads(ga_ref), [st[j] for j in range(HB)])
        for j, cs in enumerate(cols):
            y_ref[:, cs] = o[j].astype(bf16)
            st[j] = st_new[j]

    return pl.pallas_call(
        body, name=name, grid=(H // HB, nc),
        in_specs=[col(0), col(1), col(2), col(3), pl.BlockSpec((2, W), lambda h, c: (0, h)), pl.BlockSpec((1, W), lambda h, c: (0, h))],
        out_specs=(pl.BlockSpec((CHUNK, W), lambda h, c: (c, h)), pl.BlockSpec((HB, None, HEAD, HEAD), lambda h, c: (h, c, 0, 0))),
        out_shape=(jax.ShapeDtypeStruct((T, H * HEAD), bf16), jax.ShapeDtypeStruct((H, nc, HEAD, HEAD), f32)),
        scratch_shapes=[pltpu.VMEM((HB, HEAD, HEAD), f32)],
        compiler_params=_params(("parallel", "arbitrary")),
    )(proj, proj, proj, proj, logits, gain)


def _hgrn_bwd(proj, logits, gain, states, dy, n_heads, name):
    T = proj.shape[0]
    nc = T // CHUNK
    H = n_heads
    HB = _group(H, HGRN_BWD_HEADS)
    W = HB * HEAD

    def col(k):
        return pl.BlockSpec((CHUNK, W), lambda h, c: (nc - 1 - c, k * (H // HB) + h))

    out_blk = pl.BlockSpec((CHUNK, W), lambda h, c: (nc - 1 - c, h))

    def body(q_ref, f_ref, i_ref, g_ref, l_ref, ga_ref, s_ref, dy_ref, dq_ref, df_ref, di_ref, dg_ref, dl_ref, dga_ref, dst):
        first = pl.program_id(1) == 0

        @pl.when(first)
        def _():
            dst[...] = jnp.zeros((HB, HEAD, HEAD), f32)
            dl_ref[...] = jnp.zeros((2, W), f32)
            dga_ref[...] = jnp.zeros((1, W), f32)

        cols = [slice(j * HEAD, (j + 1) * HEAD) for j in range(HB)]
        heads = lambda ref: [ref[:, cs] for cs in cols]
        _, vjp = jax.vjp(_hgrn_heads, heads(q_ref), heads(f_ref), heads(i_ref), heads(g_ref), heads(l_ref), heads(ga_ref),
                         [s_ref[j] for j in range(HB)])
        dq, df, di, dg, dl, dga, ds = vjp((heads(dy_ref), [dst[j] for j in range(HB)]))
        for j, cs in enumerate(cols):
            dq_ref[:, cs] = dq[j].astype(bf16)
            df_ref[:, cs] = df[j].astype(bf16)
            di_ref[:, cs] = di[j].astype(bf16)
            dg_ref[:, cs] = dg[j].astype(bf16)
            dst[j] = ds[j]
            dl_ref[:, cs] += dl[j]
            dga_ref[:, cs] += dga[j]

    act = jax.ShapeDtypeStruct((T, H * HEAD), bf16)
    return pl.pallas_call(
        body, name=name, grid=(H // HB, nc),
        in_specs=[col(0), col(1), col(2), col(3), pl.BlockSpec((2, W), lambda h, c: (0, h)), pl.BlockSpec((1, W), lambda h, c: (0, h)),
                  pl.BlockSpec((HB, None, HEAD, HEAD), lambda h, c: (h, nc - 1 - c, 0, 0)), out_blk],
        out_specs=(out_blk, out_blk, out_blk, out_blk, pl.BlockSpec((2, W), lambda h, c: (0, h)), pl.BlockSpec((1, W), lambda h, c: (0, h))),
        out_shape=(act, act, act, act, jax.ShapeDtypeStruct((2, H * HEAD), f32), jax.ShapeDtypeStruct((1, H * HEAD), f32)),
        scratch_shapes=[pltpu.VMEM((HB, HEAD, HEAD), f32)],
        compiler_params=_params(("parallel", "arbitrary")),
    )(proj, proj, proj, proj, logits, gain, states, dy)


def _rel_index():
    t = np.arange(CHUNK)[:, None]
    sp = np.arange(BAND * CHUNK)[None, :]
    dist = (N_PAST - sp // CHUNK) * CHUNK + t - sp % CHUNK
    return (np.clip(dist, -REL_FUTURE, REL_PAST) + REL_FUTURE).reshape(1, -1).astype(np.int32)


def _bias_table(rel_bias_pad, idx, name):
    H = rel_bias_pad.shape[0]
    n = idx.shape[1]
    tc = _tile(n, 4096)

    def body(rb_ref, idx_ref, o_ref):
        onehot = (lax.broadcasted_iota(jnp.int32, (N_REL_PAD, tc), 0) == idx_ref[...]).astype(f32)
        o_ref[...] = lax.dot_general(rb_ref[...], onehot, (((1,), (0,)), ((), ())), precision=HI, preferred_element_type=f32)

    return pl.pallas_call(
        body, name=name, grid=(n // tc,),
        in_specs=[pl.BlockSpec((H, N_REL_PAD), lambda j: (0, 0)), pl.BlockSpec((1, tc), lambda j: (0, j))],
        out_specs=pl.BlockSpec((H, tc), lambda j: (0, j)), out_shape=jax.ShapeDtypeStruct((H, n), f32),
        compiler_params=_params(("parallel",)),
    )(rel_bias_pad, idx)


def _bias_table_bwd(dbias, idx, name):
    H, n = dbias.shape
    tc = _tile(n, 4096)

    def body(d_ref, idx_ref, o_ref):
        onehot = (lax.broadcasted_iota(jnp.int32, (N_REL_PAD, tc), 0) == idx_ref[...]).astype(f32)
        part = lax.dot_general(d_ref[...], onehot, (((1,), (1,)), ((), ())), precision=HI, preferred_element_type=f32)

        @pl.when(pl.program_id(0) == 0)
        def _():
            o_ref[...] = part

        @pl.when(pl.program_id(0) > 0)
        def _():
            o_ref[...] += part

    return pl.pallas_call(
        body, name=name, grid=(n // tc,),
        in_specs=[pl.BlockSpec((H, tc), lambda j: (0, j)), pl.BlockSpec((1, tc), lambda j: (0, j))],
        out_specs=pl.BlockSpec((H, N_REL_PAD), lambda j: (0, 0)), out_shape=jax.ShapeDtypeStruct((H, N_REL_PAD), f32),
        compiler_params=_params(("arbitrary",)),
    )(dbias, idx)


def _head_norm(t, gain):
    return t * lax.rsqrt(jnp.mean(t * t, axis=-1, keepdims=True) + EPS) * gain


def _attn_chunks(qs, kbs, vbs, qg, bias, ns):
    r = range(len(qs))
    qh = [_head_norm(qs[j], qg) for j in r]
    s = [_dotb(qh[j], kbs[j], ((1,), (1,))) * (HEAD ** -0.5) + bias for j in r]
    col = lax.broadcasted_iota(jnp.int32, (1, BAND * CHUNK), 1)
    s = [jnp.where(ns[j] * CHUNK - PAD + col >= 0, s[j], NEG) for j in r]
    e = [jnp.exp(s[j] - jnp.max(s[j], axis=-1, keepdims=True)) for j in r]
    p = [e[j] / jnp.sum(e[j], axis=-1, keepdims=True) for j in r]
    return [_dotb(p[j], vbs[j], ((1,), (0,))) for j in r]


def _attn_fwd(proj, q_gain, k_gain, bias, off, n_heads, name):
    T = proj.shape[0]
    nc = T // CHUNK
    H = n_heads
    CB = _group(nc, ATTN_CHUNKS)
    o0 = off // HEAD
    full = lambda k: pl.BlockSpec((T, HEAD), lambda h, c: (0, o0 + k * H + h))
    vec = pl.BlockSpec((1, HEAD), lambda h, c: (0, 0))

    def body(q_ref, k_ref, v_ref, qg_ref, kg_ref, b_ref, y_ref, kp, vp):
        c = pl.program_id(1)

        @pl.when(c == 0)
        def _():
            kp[pl.ds(0, PAD), :] = jnp.zeros((PAD, HEAD), f32)
            vp[pl.ds(0, PAD), :] = jnp.zeros((PAD, HEAD), f32)
            kp[pl.ds(PAD, T), :] = _head_norm(k_ref[...], kg_ref[...])
            vp[pl.ds(PAD, T), :] = v_ref[...]

        ns = [c * CB + j for j in range(CB)]
        rows = [pl.ds(j * CHUNK, CHUNK) for j in range(CB)]
        bands = [pl.ds(pl.multiple_of(n * CHUNK, CHUNK), BAND * CHUNK) for n in ns]
        outs = _attn_chunks([q_ref[r, :] for r in rows], [kp[b, :] for b in bands], [vp[b, :] for b in bands], qg_ref[...], b_ref[...], ns)
        for r, o in zip(rows, outs):
            y_ref[r, :] = o.astype(bf16)

    return pl.pallas_call(
        body, name=name, grid=(H, nc // CB),
        in_specs=[pl.BlockSpec((CB * CHUNK, HEAD), lambda h, c: (c, o0 + h)), full(1), full(2), vec, vec,
                  pl.BlockSpec((None, CHUNK, BAND * CHUNK), lambda h, c: (h, 0, 0))],
        out_specs=pl.BlockSpec((CB * CHUNK, HEAD), lambda h, c: (c, h)), out_shape=jax.ShapeDtypeStruct((T, H * HEAD), bf16),
        scratch_shapes=[pltpu.VMEM((T + PAD, HEAD), f32), pltpu.VMEM((T + PAD, HEAD), f32)],
        compiler_params=_params(("parallel", "arbitrary")),
    )(proj, proj, proj, q_gain, k_gain, bias)


def _attn_bwd(proj, q_gain, k_gain, bias, dy, off, n_heads, name):
    T = proj.shape[0]
    nc = T // CHUNK
    H = n_heads
    CB = _group(nc, ATTN_CHUNKS)
    o0 = off // HEAD
    full = lambda k: pl.BlockSpec((T, HEAD), lambda h, c: (0, o0 + k * H + h))
    full_out = pl.BlockSpec((T, HEAD), lambda h, c: (0, h))
    vec = pl.BlockSpec((1, HEAD), lambda h, c: (0, 0))
    chunk_out = pl.BlockSpec((CB * CHUNK, HEAD), lambda h, c: (c, h))
    bias_blk = pl.BlockSpec((None, CHUNK, BAND * CHUNK), lambda h, c: (h, 0, 0))

    def body(q_ref, k_ref, v_ref, qg_ref, kg_ref, b_ref, dy_ref, dq_ref, dk_ref, dv_ref, db_ref, dqg_ref, dkg_ref, kp, vp, dkp, dvp):
        h = pl.program_id(0)
        c = pl.program_id(1)

        @pl.when(c == 0)
        def _():
            kp[pl.ds(0, PAD), :] = jnp.zeros((PAD, HEAD), f32)
            vp[pl.ds(0, PAD), :] = jnp.zeros((PAD, HEAD), f32)
            kp[pl.ds(PAD, T), :] = _head_norm(k_ref[...], kg_ref[...])
            vp[pl.ds(PAD, T), :] = v_ref[...]
            dkp[...] = jnp.zeros((T + PAD, HEAD), f32)
            dvp[...] = jnp.zeros((T + PAD, HEAD), f32)
            db_ref[...] = jnp.zeros((CHUNK, BAND * CHUNK), f32)

        @pl.when(jnp.logical_and(h == 0, c == 0))
        def _():
            dqg_ref[...] = jnp.zeros((1, HEAD), f32)
            dkg_ref[...] = jnp.zeros((1, HEAD), f32)

        ns = [c * CB + j for j in range(CB)]
        rows = [pl.ds(j * CHUNK, CHUNK) for j in range(CB)]
        bands = [pl.ds(pl.multiple_of(n * CHUNK, CHUNK), BAND * CHUNK) for n in ns]
        _, vjp = jax.vjp(functools.partial(_attn_chunks, ns=ns), [q_ref[r, :] for r in rows], [kp[b, :] for b in bands],
                         [vp[b, :] for b in bands], qg_ref[...], b_ref[...])
        dqs, dkbs, dvbs, dqg, db = vjp([dy_ref[r, :] for r in rows])
        db_ref[...] += db
        dqg_ref[...] += dqg
        for r, b, dq, dkb, dvb in zip(rows, bands, dqs, dkbs, dvbs):
            dq_ref[r, :] = dq.astype(bf16)
            dkp[b, :] += dkb
            dvp[b, :] += dvb

        @pl.when(c == nc // CB - 1)
        def _():
            _, nvjp = jax.vjp(_head_norm, k_ref[...], kg_ref[...])
            dk, dkg = nvjp(dkp[pl.ds(PAD, T), :])
            dk_ref[...] = dk.astype(bf16)
            dv_ref[...] = dvp[pl.ds(PAD, T), :].astype(bf16)
            dkg_ref[...] += dkg

    act = jax.ShapeDtypeStruct((T, H * HEAD), bf16)
    gvec = jax.ShapeDtypeStruct((1, HEAD), f32)
    pad_buf = pltpu.VMEM((T + PAD, HEAD), f32)
    return pl.pallas_call(
        body, name=name, grid=(H, nc // CB),
        in_specs=[pl.BlockSpec((CB * CHUNK, HEAD), lambda h, c: (c, o0 + h)), full(1), full(2), vec, vec, bias_blk, chunk_out],
        out_specs=(chunk_out, full_out, full_out, bias_blk, vec, vec),
        out_shape=(act, act, act, jax.ShapeDtypeStruct((H, CHUNK, BAND * CHUNK), f32), gvec, gvec),
        scratch_shapes=[pad_buf, pad_buf, pad_buf, pad_buf],
        compiler_params=_params(("arbitrary", "arbitrary")),
    )(proj, proj, proj, q_gain, k_gain, bias, dy)


def _position():
    x, y, c = lax.axis_index("x"), lax.axis_index("y"), lax.axis_index("c")
    return x, y, c, 4 * x + 2 * y + c


def _flip(v, bit):
    return 1 - v if bit else v


def _chips(x, y):
    return [(1 - x, y), (x, 1 - y), (1 - x, 1 - y)]


def _seq_gather(shards, name, collective_id):
    n = len(shards)

    def body(*refs):
        ins, outs = refs[:n], refs[n:2 * n]
        send, recv, loc = refs[2 * n:]
        x, y, c, me = _position()
        sib = (x, y, 1 - c)
        sel = lambda a, b: c * a + (1 - c) * b
        n1 = (sel(1 - x, x), sel(y, 1 - y))
        n2 = (sel(x, 1 - x), sel(1 - y, y))
        far = (1 - x, 1 - y)
        idx = lambda chip, core: 4 * chip[0] + 2 * chip[1] + core
        barrier = pltpu.get_barrier_semaphore()
        for peer in [sib, (*n1, c), (*n2, c)]:
            pl.semaphore_signal(barrier, inc=1, device_id=peer, device_id_type=MESH)
        pl.semaphore_wait(barrier, 3)

        def copy(w, k, src, blk, to):
            return pltpu.make_async_remote_copy(src_ref=src, dst_ref=outs[w].at[blk], send_sem=send.at[7 * w + k], recv_sem=recv.at[7 * w + k],
                                                device_id=to, device_id_type=MESH)

        mine = [pltpu.make_async_copy(ins[w], outs[w].at[me], loc.at[w]) for w in range(n)]
        for cp in mine:
            cp.start()
        sent = [copy(w, 1, ins[w], me, (*n1, c)) for w in range(n)] + [copy(w, 2, ins[w], me, (*n2, c)) for w in range(n)]
        sent += [copy(w, 0, ins[w], me, sib) for w in range(n)]
        for cp in sent:
            cp.start()
        for k, chip in ((1, n1), (2, n2), (3, far)):
            blk = idx(chip, c)
            for w in range(n):
                copy(w, k, ins[w], blk, sib).wait_recv()
                if k == 1:
                    sent.append(copy(w, 3, outs[w].at[blk], blk, (*n2, c)))
                    sent[-1].start()
                sent.append(copy(w, 3 + k, outs[w].at[blk], blk, sib))
                sent[-1].start()
        for w in range(n):
            copy(w, 0, ins[w], idx((x, y), 1 - c), sib).wait_recv()
        for k, chip in ((4, n2), (5, n1), (6, far)):
            for w in range(n):
                copy(w, k, ins[w], idx(chip, 1 - c), sib).wait_recv()
        for cp in sent:
            cp.wait_send()
        for cp in mine:
            cp.wait()

    return pl.kernel(
        body, out_type=tuple(jax.ShapeDtypeStruct((NDEV,) + s.shape, s.dtype) for s in shards),
        mesh=plsc.ScalarSubcoreMesh(axis_name="sequencer", num_cores=1), name=name,
        scratch_types=(pltpu.SemaphoreType.DMA((7 * n,)), pltpu.SemaphoreType.DMA((7 * n,)), pltpu.SemaphoreType.DMA((n,))),
        compiler_params=pltpu.CompilerParams(collective_id=collective_id),
    )(*shards)


NCHIP = 4


def _seq_pair_exchange(grads, name, collective_id, after=()):
    n, na = len(grads), len(after)

    def body(*refs):
        ins, outs = refs[:n], refs[n + na:2 * n + na]
        send, recv = refs[2 * n + na:]
        x, y, c, me = _position()
        sib = (x, y, 1 - c)
        barrier = pltpu.get_barrier_semaphore()
        pl.semaphore_signal(barrier, inc=1, device_id=sib, device_id_type=MESH)
        pl.semaphore_wait(barrier, 1)
        copies = [pltpu.make_async_remote_copy(src_ref=ins[w].at[2 * k + (1 - c)], dst_ref=outs[w].at[k], send_sem=send.at[NCHIP * w + k],
                                               recv_sem=recv.at[NCHIP * w + k], device_id=sib, device_id_type=MESH)
                  for w in range(n) for k in range(NCHIP)]
        for cp in copies:
            cp.start()
        for cp in copies:
            cp.wait_recv()
        for cp in copies:
            cp.wait_send()

    return pl.kernel(
        body, out_type=tuple(jax.ShapeDtypeStruct((NCHIP,) + g.shape[1:], g.dtype) for g in grads),
        mesh=plsc.ScalarSubcoreMesh(axis_name="sequencer", num_cores=1), name=name,
        scratch_types=(pltpu.SemaphoreType.DMA((NCHIP * n,)), pltpu.SemaphoreType.DMA((NCHIP * n,))),
        compiler_params=pltpu.CompilerParams(collective_id=collective_id),
    )(*grads, *after)


def _pair_add(grad, sib_part, name, after=()):
    _, R, C = grad.shape
    tr = _tile(R, 1024, 16)
    core = jnp.reshape(lax.axis_index("c"), (1,)).astype(jnp.int32)

    def body(c_ref, g_ref, s_ref, *rest):
        rest[-1][...] = (g_ref[...].astype(f32) + s_ref[...].astype(f32)).astype(bf16)

    blk = pl.BlockSpec((None, tr, C), lambda k, i, c_ref: (k, i, 0))
    return pl.pallas_call(
        body, name=name,
        grid_spec=pltpu.PrefetchScalarGridSpec(
            num_scalar_prefetch=1, grid=(NCHIP, R // tr),
            in_specs=[pl.BlockSpec((None, tr, C), lambda k, i, c_ref: (2 * k + c_ref[0], i, 0)), blk]
            + [pl.BlockSpec(memory_space=pl.ANY)] * len(after), out_specs=blk),
        out_shape=jax.ShapeDtypeStruct((NCHIP, R, C), bf16), compiler_params=_params(("parallel", "parallel")),
    )(core, grad, sib_part, *after)


def _seq_chip_exchange(sums, name, collective_id, after=()):
    n, na = len(sums), len(after)

    def body(*refs):
        ins, outs = refs[:n], refs[n + na:2 * n + na]
        send, recv, loc = refs[2 * n + na:]
        x, y, c, me = _position()
        chips = _chips(x, y)
        mine = 2 * x + y
        barrier = pltpu.get_barrier_semaphore()
        for px, py in chips:
            pl.semaphore_signal(barrier, inc=1, device_id=(px, py, c), device_id_type=MESH)
        pl.semaphore_wait(barrier, 3)
        local = [pltpu.make_async_copy(ins[w].at[mine], outs[w].at[mine], loc.at[w]) for w in range(n)]
        for cp in local:
            cp.start()
        sends, waits = [], []
        for j, (px, py) in enumerate(chips):
            for w in range(n):
                sems = dict(send_sem=send.at[3 * w + j], recv_sem=recv.at[3 * w + j], device_id=(px, py, c), device_id_type=MESH)
                sends.append(pltpu.make_async_remote_copy(src_ref=ins[w].at[2 * px + py], dst_ref=outs[w].at[mine], **sems))
                waits.append(pltpu.make_async_remote_copy(src_ref=ins[w].at[2 * px + py], dst_ref=outs[w].at[2 * px + py], **sems))
        for cp in sends:
            cp.start()
        for cp in waits:
            cp.wait_recv()
        for cp in sends:
            cp.wait_send()
        for cp in local:
            cp.wait()

    return pl.kernel(
        body, out_type=tuple(jax.ShapeDtypeStruct(s.shape, s.dtype) for s in sums),
        mesh=plsc.ScalarSubcoreMesh(axis_name="sequencer", num_cores=1), name=name,
        scratch_types=(pltpu.SemaphoreType.DMA((3 * n,)), pltpu.SemaphoreType.DMA((3 * n,)), pltpu.SemaphoreType.DMA((n,))),
        compiler_params=pltpu.CompilerParams(collective_id=collective_id),
    )(*sums, *after)


def _reduce_scatter(grads, tag, ids, after=(), add_after=()):
    sib_parts = _seq_pair_exchange(grads, "pair_exchange_" + tag, ids[0], after=after)
    sums = [_pair_add(g, s, "pair_add_%s%d" % (tag, i), after=add_after) for i, (g, s) in enumerate(zip(grads, sib_parts))]
    return _seq_chip_exchange(sums, "chip_exchange_" + tag, ids[1]), sums


def _small_all_reduce(v, name):
    R, C = v.shape

    def body(v_ref, o_ref, buf, send, recv):
        x, y, c, me = _position()
        buf[me] = v_ref[...]
        sends, waits = [], []
        for r in range(1, NDEV):
            px, py, pc = _flip(x, r & 4), _flip(y, r & 2), _flip(c, r & 1)
            peer = 4 * px + 2 * py + pc
            sends.append(pltpu.make_async_remote_copy(src_ref=v_ref, dst_ref=buf.at[me], send_sem=send.at[r - 1], recv_sem=recv.at[r - 1],
                                                      device_id=(px, py, pc), device_id_type=MESH))
            waits.append(pltpu.make_async_remote_copy(src_ref=v_ref, dst_ref=buf.at[peer], send_sem=send.at[r - 1], recv_sem=recv.at[r - 1],
                                                      device_id=(px, py, pc), device_id_type=MESH))
        for cp in sends:
            cp.start()
        for cp in waits:
            cp.wait_recv()
        for cp in sends:
            cp.wait_send()
        acc = buf[0]
        for i in range(1, NDEV):
            acc = acc + buf[i]
        o_ref[...] = acc

    vm = pl.BlockSpec(memory_space=pltpu.VMEM)
    return pl.pallas_call(
        body, name=name, in_specs=[vm], out_specs=vm, out_shape=jax.ShapeDtypeStruct((R, C), f32),
        scratch_shapes=[pltpu.VMEM((NDEV, R, C), f32), pltpu.SemaphoreType.DMA((7,)), pltpu.SemaphoreType.DMA((7,))],
    )(v)


def _adamw_math(w, g, m, v):
    m = ADAM_B1 * m + (1.0 - ADAM_B1) * g
    v = ADAM_B2 * v + (1.0 - ADAM_B2) * (g * g)
    m_hat = m / (1.0 - ADAM_B1 ** ADAM_STEP)
    v_hat = v / (1.0 - ADAM_B2 ** ADAM_STEP)
    delta = -ADAM_LR * (m_hat / (jnp.sqrt(v_hat) + ADAM_EPS) + ADAM_WD * w)
    return delta, m, v


def _adamw_parts(w, m, v, parts, name, after=()):
    R, C = w.shape
    tr = _tile(R, 128, 16)
    blk = pl.BlockSpec((tr, C), lambda i: (i, 0))

    def body(w_ref, m_ref, v_ref, p_ref, *rest):
        g_ref, d_ref, mo_ref, vo_ref = rest[len(after):]
        g = p_ref[0].astype(f32)
        for i in range(1, NCHIP):
            g = g + p_ref[i].astype(f32)
        d, mn, vn = _adamw_math(w_ref[...], g, m_ref[...], v_ref[...])
        g_ref[...] = g
        d_ref[...] = d
        mo_ref[...] = mn
        vo_ref[...] = vn

    shp = jax.ShapeDtypeStruct((R, C), f32)
    return pl.pallas_call(
        body, name=name, grid=(R // tr,),
        in_specs=[blk, blk, blk, pl.BlockSpec((NCHIP, tr, C), lambda i: (0, i, 0))] + [pl.BlockSpec(a.shape, lambda i: (0, 0)) for a in after],
        out_specs=(blk, blk, blk, blk), out_shape=(shp, shp, shp, shp), compiler_params=_params(("parallel",)),
    )(w, m, v, parts, *after)


def _adamw_small(w, g, m, v, name):
    def body(w_ref, g_ref, m_ref, v_ref, d_ref, mo_ref, vo_ref):
        d, mn, vn = _adamw_math(w_ref[...], g_ref[...], m_ref[...], v_ref[...])
        d_ref[...] = d
        mo_ref[...] = mn
        vo_ref[...] = vn

    shp = jax.ShapeDtypeStruct(w.shape, f32)
    return pl.pallas_call(body, name=name, out_shape=(shp, shp, shp))(w, g, m, v)


SMALL_COLS = 1024


def _pack(arrs):
    flat = jnp.concatenate([a.reshape(-1) for a in arrs])
    rows = -(-flat.shape[0] // (8 * SMALL_COLS)) * 8
    return jnp.pad(flat, (0, rows * SMALL_COLS - flat.shape[0])).reshape(rows, SMALL_COLS)


def _unpack(packed, like):
    flat = packed.reshape(-1)
    out, pos = [], 0
    for a in like:
        out.append(flat[pos:pos + a.size].reshape(a.shape))
        pos += a.size
    return out


def kernel(x, w_in, b_gate, norm_mix, norm_ffn, hgrn_lb_logits, hgrn_out_gain, q_gain, k_gain, rel_bias, w_proj_a, w_proj_b, w_out, w_ffn_in, w_ffn_out, loss_target, m_w_in, m_b_gate, m_norm_mix, m_norm_ffn, m_hgrn_lb_logits, m_hgrn_out_gain, m_q_gain, m_k_gain, m_rel_bias, m_w_proj_a, m_w_proj_b, m_w_out, m_w_ffn_in, m_w_ffn_out, v_w_in, v_b_gate, v_norm_mix, v_norm_ffn, v_hgrn_lb_logits, v_hgrn_out_gain, v_q_gain, v_k_gain, v_rel_bias, v_w_proj_a, v_w_proj_b, v_w_out, v_w_ffn_in, v_w_ffn_out):
    xs = x[0]
    target = loss_target[0]
    T, D = xs.shape
    d_a = hgrn_out_gain.shape[-1]
    H = d_a // HEAD
    d_b = d_a
    off_b = 4 * d_a
    off_g = off_b + 3 * d_b
    assert rel_bias.shape[1] == H and T % CHUNK == 0 and T // CHUNK > N_PAST

    big_w = [w_in[0], w_proj_a[0], w_proj_b[0], w_out[0], w_ffn_in[0], w_ffn_out[0]]
    big_m = [m_w_in[0], m_w_proj_a[0], m_w_proj_b[0], m_w_out[0], m_w_ffn_in[0], m_w_ffn_out[0]]
    big_v = [v_w_in[0], v_w_proj_a[0], v_w_proj_b[0], v_w_out[0], v_w_ffn_in[0], v_w_ffn_out[0]]

    sh = [w.astype(bf16) for w in big_w]
    (g_in,) = _seq_gather(sh[0:1], "gather_a", 1)
    g_pa, g_pb, g_out = _seq_gather(sh[1:4], "gather_b", 2)
    (g_fin,) = _seq_gather(sh[4:5], "gather_c", 3)
    (g_fout,) = _seq_gather(sh[5:6], "gather_d", 4)

    h = _rms_fwd(xs, None, norm_mix, "rms_mix")
    proj = _mm(h, g_in, mode="nn", b_blocked=True, name="mm_proj")
    y_a, states = _hgrn_fwd(proj, hgrn_lb_logits, hgrn_out_gain, H, "hgrn_fwd")
    idx = jnp.asarray(_rel_index())
    rb_pad = jnp.pad(rel_bias[0], ((0, 0), (0, N_REL_PAD - N_REL)))
    bias = _bias_table(rb_pad, idx, "bias_table").reshape(H, CHUNK, BAND * CHUNK)
    y_b = _attn_fwd(proj, q_gain, k_gain, bias, off_b, H, "attn_fwd")
    wg_out = g_out.reshape(-1, g_out.shape[-1])
    wg_fout = g_fout.reshape(-1, g_fout.shape[-1])
    pa = _mm(y_a, g_pa, mode="nn", b_blocked=True, tm=2048, name="mm_proj_a")
    pb = _mm(y_b, g_pb, mode="nn", b_blocked=True, tm=2048, name="mm_proj_b")
    merged = _merge_fwd(pa, pb, proj, b_gate, off_g, "merge_fwd")
    mo = _mm(merged, wg_out, mode="nn", name="mm_out")
    x1, h2 = _rms_fwd(xs, mo, norm_ffn, "rms_ffn")
    gu, act = _ffn_in_swiglu(h2, g_fin, "mm_ffn_in")
    fo = _mm(act, wg_fout, mode="nn", tk=2816, name="mm_ffn_out")
    dy, loss_acc = _loss_head(x1, fo, target, "loss_head")
    loss_part = loss_acc[0:1, 0:1] * (0.5 / D)

    gw_fout = _mm(act, dy, mode="tn", out_dtype=bf16, tm=1408, name="mm_gw_ffn_out")
    dgu = _d_act_swiglu(dy, wg_fout, gu, "mm_d_act")
    gw_fin = _mm(h2, dgu, mode="tn", b_stacked=True, out_blocked=True, out_dtype=bf16, tn=g_fin.shape[-1], name="mm_gw_ffn_in")
    dh2 = _mm(dgu, g_fin, mode="nt", a_stacked=True, b_blocked=True, name="mm_d_h2")
    (p_fout, p_fin), sums_a = _reduce_scatter([gw_fout.reshape(NDEV, -1, D), gw_fin], "a", (5, 6), add_after=(dh2,))
    dx1, g_norm_ffn = _rms_bwd(x1, norm_ffn, dh2, dy, "rms_ffn_bwd", after=sums_a)

    dmerged = _mm(dx1, wg_out, mode="nt", name="mm_d_merged")
    gw_out = _mm(merged, dx1, mode="tn", out_dtype=bf16, name="mm_gw_out")
    dpa, dgl_a, gb_a = _branch_bwd(dmerged, pa, proj, b_gate, off_g, 0, "branch_a_bwd")
    dpb, dgl_b, gb_b = _branch_bwd(dmerged, pb, proj, b_gate, off_g + D, D, "branch_b_bwd")
    dy_a = _mm(dpa, g_pa, mode="nt", b_blocked=True, tm=2048, name="mm_d_ya")
    dy_b = _mm(dpb, g_pb, mode="nt", b_blocked=True, tm=2048, name="mm_d_yb")
    gw_pa = _mm(y_a, dpa, mode="tn", out_blocked=True, out_dtype=bf16, tn=g_pa.shape[-1], name="mm_gw_proj_a")
    gw_pb = _mm(y_b, dpb, mode="tn", out_blocked=True, out_dtype=bf16, tn=g_pb.shape[-1], name="mm_gw_proj_b")

    dq_a, df_a, di_a, dg_a, g_logits, g_gain = _hgrn_bwd(proj, hgrn_lb_logits, hgrn_out_gain, states, dy_a, H, "hgrn_bwd")
    dq_b, dk_b, dv_b, dbias, g_qg, g_kg = _attn_bwd(proj, q_gain, k_gain, bias, dy_b, off_b, H, "attn_bwd")
    g_rel_pad = _bias_table_bwd(dbias.reshape(H, -1), idx, "bias_table_bwd")
    g_rel = g_rel_pad[:, :N_REL]
    dproj = jnp.concatenate([dq_a, df_a, di_a, dg_a, dq_b, dk_b, dv_b, dgl_a, dgl_b], axis=1)
    (p_out, p_pa, p_pb), sums_b = _reduce_scatter([gw_out.reshape(NDEV, -1, D), gw_pa, gw_pb], "b", (7, 8), after=(g_gain, p_fout, p_fin), add_after=(dy_b,))
    gw_in = _mm(h, dproj, mode="tn", out_blocked=True, out_dtype=bf16, tn=g_in.shape[-1], name="mm_gw_in", after=sums_b)
    (p_in,), sums_c = _reduce_scatter([gw_in], "c", (9, 10), after=(p_out, p_pa, p_pb), add_after=(g_rel_pad,))
    dh = _mm(dproj, g_in, mode="nt", b_blocked=True, name="mm_d_h", after=sums_c)
    grad_x, g_norm_mix = _rms_bwd(xs, norm_mix, dh, dx1, "rms_mix_bwd")

    parts = [p_in, p_pa, p_pb, p_out, p_fin, p_fout]
    names = ["w_in", "w_proj_a", "w_proj_b", "w_out", "w_ffn_in", "w_ffn_out"]
    big = {}
    for nm, w, m, v, p in zip(names, big_w, big_m, big_v, parts):
        big[nm] = [o[None] for o in _adamw_parts(w, m, v, p, "adamw_" + nm, after=() if nm == "w_in" else (g_norm_mix,))]

    small_names = ["b_gate", "norm_mix", "norm_ffn", "hgrn_lb_logits", "hgrn_out_gain", "q_gain", "k_gain", "rel_bias"]
    small_w = [b_gate, norm_mix, norm_ffn, hgrn_lb_logits, hgrn_out_gain, q_gain, k_gain, rel_bias]
    small_m = [m_b_gate, m_norm_mix, m_norm_ffn, m_hgrn_lb_logits, m_hgrn_out_gain, m_q_gain, m_k_gain, m_rel_bias]
    small_v = [v_b_gate, v_norm_mix, v_norm_ffn, v_hgrn_lb_logits, v_hgrn_out_gain, v_q_gain, v_k_gain, v_rel_bias]
    small_g = [jnp.concatenate([gb_a, gb_b], axis=1), g_norm_mix, g_norm_ffn, g_logits, g_gain, g_qg, g_kg, g_rel[None], loss_part]
    g_sum = _small_all_reduce(_pack(small_g), "reduce_small")
    loss = _unpack(g_sum, small_g)[-1].reshape(())
    d_s, m_s, v_s = _adamw_small(_pack(small_w), g_sum, _pack(small_m), _pack(small_v), "adamw_small")
    small = {}
    for nm, g, d, m, v in zip(small_names, _unpack(g_sum, small_w), _unpack(d_s, small_w), _unpack(m_s, small_w), _unpack(v_s, small_w)):
        small[nm] = [g, d, m, v]

    order = ["w_in", "b_gate", "norm_mix", "norm_ffn", "hgrn_lb_logits", "hgrn_out_gain", "q_gain", "k_gain", "rel_bias",
             "w_proj_a", "w_proj_b", "w_out", "w_ffn_in", "w_ffn_out"]
    res = {**big, **small}
    outs = [loss, grad_x[None]]
    for k in range(4):
        outs += [res[nm][k] for nm in order]
    return tuple(outs)
```

```python
import functools

import numpy as np
import jax
import jax.numpy as jnp
from jax import lax
from jax.experimental import pallas as pl
from jax.experimental.pallas import tpu as pltpu
from jax.experimental.pallas import tpu_sc as plsc

f32 = jnp.float32
bf16 = jnp.bfloat16
HI = lax.Precision.HIGHEST
MESH = pl.DeviceIdType.MESH
AXES = ("x", "y", "c")
NDEV = 8

CHUNK = 64
HEAD = 128
SUB = 8
HGRN_BWD_HEADS = 8
ATTN_CHUNKS = 8
N_PAST = 8
BAND = N_PAST + 1
PAD = N_PAST * CHUNK
REL_FUTURE = CHUNK - 1
REL_PAST = 2 * CHUNK - 1
N_REL = REL_FUTURE + REL_PAST + 1
N_REL_PAD = 256
EPS = 1e-6
NEG = -1e30

ADAM_LR = 0.001
ADAM_B1 = 0.9
ADAM_B2 = 0.999
ADAM_EPS = 1e-08
ADAM_WD = 0.01
ADAM_STEP = 10

VMEM_LIMIT = 56 * 1024 * 1024


def _params(sem=None):
    return pltpu.CompilerParams(dimension_semantics=sem, vmem_limit_bytes=VMEM_LIMIT)


def _tile(n, pref, unit=128):
    if n <= pref:
        return n
    t = (pref // unit) * unit
    while t >= unit:
        if n % t == 0:
            return t
        t -= unit
    return n


_sigmoid = jax.nn.sigmoid


def _mm(a, b, *, mode, name, b_blocked=False, out_blocked=False, out_dtype=f32, tm=1024, tn=1024, tk=2048, after=(),
        a_stacked=False, b_stacked=False):
    if a_stacked:
        assert mode == "nt"
        M, K = a.shape[1], 2 * a.shape[2]
    elif mode == "tn":
        K, M = a.shape
    else:
        M, K = a.shape
    if b_blocked:
        nb, mid, cb = b.shape
        if mode == "nn":
            assert mid == K
            N, tn = nb * cb, cb
        else:
            assert mode == "nt" and nb * cb == K
            N, tk = mid, cb
    elif b_stacked:
        assert mode == "tn"
        N = 2 * b.shape[2]
    else:
        N = b.shape[1] if mode in ("nn", "tn") else b.shape[0]
    tm = _tile(M, tm)
    tn = tn if (b_blocked and mode == "nn") or out_blocked else _tile(N, tn)
    tk = tk if b_blocked and mode == "nt" else _tile(K, tk)
    assert M % tm == 0 and N % tn == 0 and K % tk == 0
    nk = K // tk
    grid = (M // tm, N // tn, nk)
    if a_stacked:
        ka = K // 2 // tk
        a_spec = pl.BlockSpec((None, tm, tk), lambda i, j, k: (k // ka, i, k % ka))
    elif mode == "tn":
        a_spec = pl.BlockSpec((tk, tm), lambda i, j, k: (k, i))
    else:
        a_spec = pl.BlockSpec((tm, tk), lambda i, j, k: (i, k))
    if mode == "nn":
        b_spec = pl.BlockSpec((None, tk, cb), lambda i, j, k: (j, k, 0)) if b_blocked else pl.BlockSpec((tk, tn), lambda i, j, k: (k, j))
    elif mode == "nt":
        b_spec = pl.BlockSpec((None, tn, cb), lambda i, j, k: (k, j, 0)) if b_blocked else pl.BlockSpec((tn, tk), lambda i, j, k: (j, k))
    elif b_stacked:
        nh = N // 2 // tn
        b_spec = pl.BlockSpec((None, tk, tn), lambda i, j, k: (j // nh, k, j % nh))
    else:
        b_spec = pl.BlockSpec((tk, tn), lambda i, j, k: (k, j))
    if out_blocked:
        out_shape = jax.ShapeDtypeStruct((N // tn, M, tn), out_dtype)
        o_spec = pl.BlockSpec((None, tm, tn), lambda i, j, k: (j, i, 0))
    else:
        out_shape = jax.ShapeDtypeStruct((M, N), out_dtype)
        o_spec = pl.BlockSpec((tm, tn), lambda i, j, k: (i, j))
    dims = {"nn": ((1,), (0,)), "nt": ((1,), (1,)), "tn": ((0,), (0,))}[mode]

    def body(a_ref, b_ref, *rest):
        o_ref, acc = rest[len(after)], rest[len(after) + 1:]
        p = lax.dot_general(a_ref[...].astype(bf16), b_ref[...].astype(bf16), (dims, ((), ())), preferred_element_type=f32)
        if nk == 1:
            o_ref[...] = p.astype(out_dtype)
        else:
            acc_ref = acc[0]
            k = pl.program_id(2)

            @pl.when(k == 0)
            def _():
                acc_ref[...] = p

            @pl.when(k > 0)
            def _():
                acc_ref[...] += p

            @pl.when(k == nk - 1)
            def _():
                o_ref[...] = acc_ref[...].astype(out_dtype)

    return pl.pallas_call(
        body, name=name, grid=grid, in_specs=[a_spec, b_spec] + [pl.BlockSpec(memory_space=pl.ANY)] * len(after), out_specs=o_spec,
        out_shape=out_shape, scratch_shapes=[pltpu.VMEM((tm, tn), f32)] if nk > 1 else [],
        compiler_params=_params(("parallel", "parallel", "arbitrary")),
    )(a, b, *after)


def _rms_fwd(x, res, gain, name):
    T, D = x.shape
    tr = _tile(T, 256, 8)
    row = pl.BlockSpec((tr, D), lambda i: (i, 0))
    vec = pl.BlockSpec((1, D), lambda i: (0, 0))

    def body(*refs):
        if res is None:
            x_ref, g_ref, h_ref = refs
            xs = x_ref[...]
        else:
            x_ref, r_ref, g_ref, xs_ref, h_ref = refs
            xs = x_ref[...] + r_ref[...]
            xs_ref[...] = xs
        r = lax.rsqrt(jnp.mean(xs * xs, axis=-1, keepdims=True) + EPS)
        h_ref[...] = (xs * r * g_ref[...]).astype(bf16)

    h_shape = jax.ShapeDtypeStruct((T, D), bf16)
    if res is None:
        return pl.pallas_call(body, name=name, grid=(T // tr,), in_specs=[row, vec], out_specs=row, out_shape=h_shape,
                              compiler_params=_params(("parallel",)))(x, gain)
    return pl.pallas_call(body, name=name, grid=(T // tr,), in_specs=[row, row, vec], out_specs=(row, row),
                          out_shape=(jax.ShapeDtypeStruct((T, D), f32), h_shape), compiler_params=_params(("parallel",)))(x, res, gain)


def _rms_bwd(xs, gain, dh, extra, name, after=()):
    T, D = xs.shape
    tr = _tile(T, 256, 8)
    row = pl.BlockSpec((tr, D), lambda i: (i, 0))
    vec = pl.BlockSpec((1, D), lambda i: (0, 0))

    def body(x_ref, g_ref, dh_ref, e_ref, *rest):
        dx_ref, dg_ref = rest[len(after):]
        x = x_ref[...]
        r = lax.rsqrt(jnp.mean(x * x, axis=-1, keepdims=True) + EPS)
        xhat = x * r
        dh_v = dh_ref[...]
        gd = dh_v * g_ref[...]
        dx_ref[...] = e_ref[...] + r * (gd - xhat * jnp.mean(gd * xhat, axis=-1, keepdims=True))
        part = jnp.sum(dh_v * xhat, axis=0, keepdims=True)

        @pl.when(pl.program_id(0) == 0)
        def _():
            dg_ref[...] = part

        @pl.when(pl.program_id(0) > 0)
        def _():
            dg_ref[...] += part

    return pl.pallas_call(body, name=name, grid=(T // tr,),
                          in_specs=[row, vec, row, row] + [pl.BlockSpec(memory_space=pl.ANY)] * len(after), out_specs=(row, vec),
                          out_shape=(jax.ShapeDtypeStruct((T, D), f32), jax.ShapeDtypeStruct((1, D), f32)),
                          compiler_params=_params(("arbitrary",)))(xs, gain, dh, extra, *after)


def _merge_fwd(pa, pb, proj, b_gate, off, name):
    T, D = pa.shape
    tr, tc = _tile(T, 512, 8), _tile(D, 512)
    oa, ob, nb = off // tc, (off + D) // tc, D // tc
    blk = pl.BlockSpec((tr, tc), lambda i, j: (i, j))

    def body(pa_ref, pb_ref, ga_ref, gb_ref, ba_ref, bb_ref, o_ref):
        ga = _sigmoid(ga_ref[...] + ba_ref[...])
        gb = _sigmoid(gb_ref[...] + bb_ref[...])
        o_ref[...] = (ga * pa_ref[...] + gb * pb_ref[...]).astype(bf16)

    return pl.pallas_call(
        body, name=name, grid=(T // tr, nb),
        in_specs=[blk, blk, pl.BlockSpec((tr, tc), lambda i, j: (i, oa + j)), pl.BlockSpec((tr, tc), lambda i, j: (i, ob + j)),
                  pl.BlockSpec((1, tc), lambda i, j: (0, j)), pl.BlockSpec((1, tc), lambda i, j: (0, nb + j))],
        out_specs=blk, out_shape=jax.ShapeDtypeStruct((T, D), bf16), compiler_params=_params(("parallel", "parallel")),
    )(pa, pb, proj, proj, b_gate, b_gate)


def _d_merged_branches(dx, w_out, pa, pb, proj, b_gate, off, name):
    T, D = pa.shape
    tm, tn = _tile(T, 512, 8), _tile(D, 1024)
    oa, ob, nb = off // tn, (off + D) // tn, D // tn
    blk = pl.BlockSpec((tm, tn), lambda j, i: (i, j))

    def body(dx_ref, w_ref, pa_ref, pb_ref, ga_ref, gb_ref, ba_ref, bb_ref, dpa_ref, dpb_ref, dgl_ref, db_ref):
        dm = lax.dot_general(dx_ref[...].astype(bf16), w_ref[...], (((1,), (1,)), ((), ())), preferred_element_type=f32)
        sums = []
        for p_ref, gl_ref, b_ref, dp_ref, k in ((pa_ref, ga_ref, ba_ref, dpa_ref, 0), (pb_ref, gb_ref, bb_ref, dpb_ref, 1)):
            g = _sigmoid(gl_ref[...] + b_ref[...])
            dp_ref[...] = (dm * g).astype(bf16)
            dgl = dm * p_ref[...] * g * (1.0 - g)
            dgl_ref[k] = dgl.astype(bf16)
            sums.append(jnp.sum(dgl, axis=0, keepdims=True))

        @pl.when(pl.program_id(1) == 0)
        def _():
            db_ref[...] = jnp.zeros((2, 1, tn), f32)

        db_ref[0] += sums[0]
        db_ref[1] += sums[1]

    return pl.pallas_call(
        body, name=name, grid=(nb, T // tm),
        in_specs=[pl.BlockSpec((tm, D), lambda j, i: (i, 0)), pl.BlockSpec((tn, D), lambda j, i: (j, 0)), blk, blk,
                  pl.BlockSpec((tm, tn), lambda j, i: (i, oa + j)), pl.BlockSpec((tm, tn), lambda j, i: (i, ob + j)),
                  pl.BlockSpec((1, tn), lambda j, i: (0, j)), pl.BlockSpec((1, tn), lambda j, i: (0, nb + j))],
        out_specs=(blk, blk, pl.BlockSpec((2, tm, tn), lambda j, i: (0, i, j)), pl.BlockSpec((2, 1, tn), lambda j, i: (0, 0, j))),
        out_shape=(jax.ShapeDtypeStruct((T, D), bf16), jax.ShapeDtypeStruct((T, D), bf16), jax.ShapeDtypeStruct((2, T, D), bf16),
                   jax.ShapeDtypeStruct((2, 1, D), f32)),
        compiler_params=_params(("parallel", "arbitrary")),
    )(dx, w_out, pa, pb, proj, proj, b_gate, b_gate)


def _ffn_in_swiglu(h, w, name):
    T, K = h.shape
    nb, _, cb = w.shape
    half = nb // 2
    F = half * cb
    tm = _tile(T, 512, 8)

    def body(h_ref, wg_ref, wu_ref, gu_ref, act_ref):
        dn = (((1,), (0,)), ((), ()))
        hv = h_ref[...]
        g = lax.dot_general(hv, wg_ref[...], dn, preferred_element_type=f32)
        u = lax.dot_general(hv, wu_ref[...], dn, preferred_element_type=f32)
        gu_ref[0] = g
        gu_ref[1] = u
        act_ref[...] = (g * _sigmoid(g) * u).astype(bf16)

    return pl.pallas_call(
        body, name=name, grid=(half, T // tm),
        in_specs=[pl.BlockSpec((tm, K), lambda j, i: (i, 0)), pl.BlockSpec((None, K, cb), lambda j, i: (j, 0, 0)),
                  pl.BlockSpec((None, K, cb), lambda j, i: (j + half, 0, 0))],
        out_specs=(pl.BlockSpec((2, tm, cb), lambda j, i: (0, i, j)), pl.BlockSpec((tm, cb), lambda j, i: (i, j))),
        out_shape=(jax.ShapeDtypeStruct((2, T, F), f32), jax.ShapeDtypeStruct((T, F), bf16)),
        compiler_params=_params(("parallel", "parallel")),
    )(h, w, w)


def _d_act_swiglu(dy, w_out, gu, name):
    T, D = dy.shape
    F = w_out.shape[0]
    tm, tn = _tile(T, 512, 8), _tile(F, 1408)

    def body(dy_ref, w_ref, gu_ref, o_ref):
        halves = [pl.ds(r * (tm // 2), tm // 2) for r in range(2)]
        wv = w_ref[...]
        d = [lax.dot_general(dy_ref[r, :].astype(bf16), wv, (((1,), (1,)), ((), ())), preferred_element_type=f32) for r in halves]
        for r, dr in zip(halves, d):
            g = gu_ref[0, r, :]
            s = _sigmoid(g)
            o_ref[0, r, :] = (dr * gu_ref[1, r, :] * s * (1.0 + g * (1.0 - s))).astype(bf16)
            o_ref[1, r, :] = (dr * g * s).astype(bf16)

    blk = pl.BlockSpec((2, tm, tn), lambda j, i: (0, i, j))
    return pl.pallas_call(
        body, name=name, grid=(F // tn, T // tm),
        in_specs=[pl.BlockSpec((tm, D), lambda j, i: (i, 0)), pl.BlockSpec((tn, D), lambda j, i: (j, 0)), blk],
        out_specs=blk, out_shape=jax.ShapeDtypeStruct((2, T, F), bf16), compiler_params=_params(("parallel", "parallel")),
    )(dy, w_out, gu)


def _loss_head(x1, fo, target, name):
    T, D = x1.shape
    tr = _tile(T, 256, 8)
    row = pl.BlockSpec((tr, D), lambda i: (i, 0))
    acc = pl.BlockSpec((8, 128), lambda i: (0, 0))

    def body(x_ref, f_ref, t_ref, dy_ref, l_ref):
        d = x_ref[...] + f_ref[...] - t_ref[...]
        dy_ref[...] = d * (1.0 / D)
        part = jnp.sum(jnp.sum(d * d, axis=1, keepdims=True), axis=0, keepdims=True)

        @pl.when(pl.program_id(0) == 0)
        def _():
            l_ref[...] = jnp.zeros((8, 128), f32)

        l_ref[...] += part

    return pl.pallas_call(body, name=name, grid=(T // tr,), in_specs=[row, row, row], out_specs=(row, acc),
                          out_shape=(jax.ShapeDtypeStruct((T, D), f32), jax.ShapeDtypeStruct((8, 128), f32)),
                          compiler_params=_params(("arbitrary",)))(x1, fo, target)


_DIMS = {"nn": ((1,), (0,)), "nt": ((1,), (1,)), "tn": ((0,), (0,))}
_MODE = {v: k for k, v in _DIMS.items()}


def _dot_bf16(a, b, mode):
    return lax.dot_general(a.astype(bf16), b.astype(bf16), (_DIMS[mode], ((), ())), preferred_element_type=f32)


@functools.partial(jax.custom_vjp, nondiff_argnums=(2,))
def _dotm(a, b, mode):
    return _dot_bf16(a, b, mode)


def _dotm_fwd(a, b, mode):
    return _dot_bf16(a, b, mode), (a, b)


def _dotm_bwd(mode, res, g):
    a, b = res
    if mode == "nn":
        return _dot_bf16(g, b, "nt"), _dot_bf16(a, g, "tn")
    if mode == "nt":
        return _dot_bf16(g, b, "nn"), _dot_bf16(g, a, "tn")
    return _dot_bf16(b, g, "nt"), _dot_bf16(a, g, "nn")


_dotm.defvjp(_dotm_fwd, _dotm_bwd)


def _dotb(a, b, dims):
    return _dotm(a, b, _MODE[dims])


def _split3(v):
    def top(t):
        return lax.bitcast_convert_type(lax.bitcast_convert_type(t, jnp.uint32) & jnp.uint32(0xFFFF0000), f32)

    hi = top(v)
    mid = top(v - hi)
    low = (v - hi) - mid
    return hi.astype(bf16), mid.astype(bf16), low.astype(bf16)


def _dot3(v, m, dims, v_first):
    m = m.astype(bf16)
    dn = (dims, ((), ()))
    parts = [lax.dot_general(p, m, dn, preferred_element_type=f32) if v_first else lax.dot_general(m, p, dn, preferred_element_type=f32)
             for p in _split3(v)]
    return parts[0] + parts[1] + parts[2]


def _triangle_sum(v, lower):
    row = lax.broadcasted_iota(jnp.int32, (CHUNK, CHUNK), 0)
    col = lax.broadcasted_iota(jnp.int32, (CHUNK, CHUNK), 1)
    return _dot3(v, (col <= row) if lower else (col >= row), ((1,), (0,)), False)


@jax.custom_vjp
def _cumsum_rows(v):
    return _triangle_sum(v, True)


_cumsum_rows.defvjp(lambda v: (_triangle_sum(v, True), None), lambda _, g: (_triangle_sum(g, False),))


def _hgrn_heads(q, fl, iv, g, logits, gain, st):
    r = range(len(q))
    lb = [jax.nn.softmax(logits[j], axis=0)[0:1] for j in r]
    f = [lb[j] + (1.0 - lb[j]) * _sigmoid(fl[j]) for j in r]
    lf = [jnp.log(f[j]) for j in r]
    kk = [1.0 - f[j] for j in r]
    qs = [q[j] * _sigmoid(q[j]) for j in r]
    b = [_cumsum_rows(lf[j]) for j in r]
    b_last = [jnp.sum(lf[j], axis=0, keepdims=True) for j in r]
    o = [_dotb(qs[j] * jnp.exp(b[j]), st[j], ((1,), (1,))) for j in r]
    r3 = lax.broadcasted_iota(jnp.int32, (SUB, SUB, HEAD), 0)
    c3 = lax.broadcasted_iota(jnp.int32, (SUB, SUB, HEAD), 1)
    parts = [[] for _ in r]
    for i in range(CHUNK // SUB):
        lo, hi = i * SUB, (i + 1) * SUB
        bi = [b[j][lo:hi] for j in r]
        dec = [jnp.exp(jnp.where(c3 <= r3, bi[j][:, None, :] - bi[j][None, :, :], -jnp.inf)) for j in r]
        s = [jnp.sum(qs[j][lo:hi][:, None, :] * kk[j][lo:hi][None, :, :] * dec[j], axis=-1) for j in r]
        if i > 0:
            anchor = [jnp.max(bi[j], axis=0, keepdims=True) for j in r]
            qa = [qs[j][lo:hi] * jnp.exp(bi[j] - anchor[j]) for j in r]
            kd = [kk[j][:lo] * jnp.exp(anchor[j] - b[j][:lo]) for j in r]
            s = [jnp.concatenate([_dotb(qa[j], kd[j], ((1,), (1,))), s[j]], axis=1) for j in r]
        for j in r:
            parts[j].append(_dotb(s[j], iv[j][:hi], ((1,), (0,))))
    o = [o[j] + jnp.concatenate(parts[j], axis=0) for j in r]
    st_new = [st[j] * jnp.exp(b_last[j]) + _dotb(iv[j], kk[j] * jnp.exp(b_last[j] - b[j]), ((0,), (0,))) for j in r]
    o = [o[j] * lax.rsqrt(jnp.mean(o[j] * o[j], axis=-1, keepdims=True) + EPS) for j in r]
    o = [o[j] * gain[j] * (g[j] * _sigmoid(g[j])) for j in r]
    return o, st_new


def _group(n, pref):
    while n % pref:
        pref //= 2
    return pref


def _hgrn_fwd(proj, logits, gain, n_heads, name):
    T = proj.shape[0]
    nc = T // CHUNK
    H = n_heads
    HB = _group(H, 8)
    W = HB * HEAD

    def col(k):
        return pl.BlockSpec((CHUNK, W), lambda h, c: (c, k * (H // HB) + h))

    def body(q_ref, f_ref, i_ref, g_ref, l_ref, ga_ref, y_ref, s_ref, st):
        @pl.when(pl.program_id(1) == 0)
        def _():
            st[...] = jnp.zeros((HB, HEAD, HEAD), f32)

        cols = [slice(j * HEAD, (j + 1) * HEAD) for j in range(HB)]
        heads = lambda ref: [ref[:, cs] for cs in cols]
        s_ref[...] = st[...]
        o, st_new = _hgrn_heads(heads(q_ref), heads(f_ref), heads(i_ref), heads(g_ref), heads(l_ref), heads(ga_ref), [st[j] for j in range(HB)])
        for j, cs in enumerate(cols):
            y_ref[:, cs] = o[j].astype(bf16)
            st[j] = st_new[j]

    return pl.pallas_call(
        body, name=name, grid=(H // HB, nc),
        in_specs=[col(0), col(1), col(2), col(3), pl.BlockSpec((2, W), lambda h, c: (0, h)), pl.BlockSpec((1, W), lambda h, c: (0, h))],
        out_specs=(pl.BlockSpec((CHUNK, W), lambda h, c: (c, h)), pl.BlockSpec((HB, None, HEAD, HEAD), lambda h, c: (h, c, 0, 0))),
        out_shape=(jax.ShapeDtypeStruct((T, H * HEAD), bf16), jax.ShapeDtypeStruct((H, nc, HEAD, HEAD), f32)),
        scratch_shapes=[pltpu.VMEM((HB, HEAD, HEAD), f32)],
        compiler_params=_params(("parallel", "arbitrary")),
    )(proj, proj, proj, proj, logits, gain)


def _hgrn_bwd(proj, logits, gain, states, dy, n_heads, name):
    T = proj.shape[0]
    nc = T // CHUNK
    H = n_heads
    HB = _group(H, HGRN_BWD_HEADS)
    W = HB * HEAD

    def col(k):
        return pl.BlockSpec((CHUNK, W), lambda h, c: (nc - 1 - c, k * (H // HB) + h))

    out_blk = pl.BlockSpec((CHUNK, W), lambda h, c: (nc - 1 - c, h))

    def body(q_ref, f_ref, i_ref, g_ref, l_ref, ga_ref, s_ref, dy_ref, dq_ref, df_ref, di_ref, dg_ref, dl_ref, dga_ref, dst):
        first = pl.program_id(1) == 0

        @pl.when(first)
        def _():
            dst[...] = jnp.zeros((HB, HEAD, HEAD), f32)
            dl_ref[...] = jnp.zeros((2, W), f32)
            dga_ref[...] = jnp.zeros((1, W), f32)

        cols = [slice(j * HEAD, (j + 1) * HEAD) for j in range(HB)]
        heads = lambda ref: [ref[:, cs] for cs in cols]
        _, vjp = jax.vjp(_hgrn_heads, heads(q_ref), heads(f_ref), heads(i_ref), heads(g_ref), heads(l_ref), heads(ga_ref),
                         [s_ref[j] for j in range(HB)])
        dq, df, di, dg, dl, dga, ds = vjp((heads(dy_ref), [dst[j] for j in range(HB)]))
        for j, cs in enumerate(cols):
            dq_ref[:, cs] = dq[j].astype(bf16)
            df_ref[:, cs] = df[j].astype(bf16)
            di_ref[:, cs] = di[j].astype(bf16)
            dg_ref[:, cs] = dg[j].astype(bf16)
            dst[j] = ds[j]
            dl_ref[:, cs] += dl[j]
            dga_ref[:, cs] += dga[j]

    act = jax.ShapeDtypeStruct((T, H * HEAD), bf16)
    return pl.pallas_call(
        body, name=name, grid=(H // HB, nc),
        in_specs=[col(0), col(1), col(2), col(3), pl.BlockSpec((2, W), lambda h, c: (0, h)), pl.BlockSpec((1, W), lambda h, c: (0, h)),
                  pl.BlockSpec((HB, None, HEAD, HEAD), lambda h, c: (h, nc - 1 - c, 0, 0)), out_blk],
        out_specs=(out_blk, out_blk, out_blk, out_blk, pl.BlockSpec((2, W), lambda h, c: (0, h)), pl.BlockSpec((1, W), lambda h, c: (0, h))),
        out_shape=(act, act, act, act, jax.ShapeDtypeStruct((2, H * HEAD), f32), jax.ShapeDtypeStruct((1, H * HEAD), f32)),
        scratch_shapes=[pltpu.VMEM((HB, HEAD, HEAD), f32)],
        compiler_params=_params(("parallel", "arbitrary")),
    )(proj, proj, proj, proj, logits, gain, states, dy)


def _rel_index():
    t = np.arange(CHUNK)[:, None]
    sp = np.arange(BAND * CHUNK)[None, :]
    dist = (N_PAST - sp // CHUNK) * CHUNK + t - sp % CHUNK
    return (np.clip(dist, -REL_FUTURE, REL_PAST) + REL_FUTURE).reshape(1, -1).astype(np.int32)


def _bias_table(rel_bias_pad, idx, name):
    H = rel_bias_pad.shape[0]
    n = idx.shape[1]
    tc = _tile(n, 4096)

    def body(rb_ref, idx_ref, o_ref):
        onehot = lax.broadcasted_iota(jnp.int32, (N_REL_PAD, tc), 0) == idx_ref[...]
        o_ref[...] = _dot3(rb_ref[...], onehot, ((1,), (0,)), True)

    return pl.pallas_call(
        body, name=name, grid=(n // tc,),
        in_specs=[pl.BlockSpec((H, N_REL_PAD), lambda j: (0, 0)), pl.BlockSpec((1, tc), lambda j: (0, j))],
        out_specs=pl.BlockSpec((H, tc), lambda j: (0, j)), out_shape=jax.ShapeDtypeStruct((H, n), f32),
        compiler_params=_params(("parallel",)),
    )(rel_bias_pad, idx)


def _bias_table_bwd(dbias, idx, name):
    H, n = dbias.shape
    tc = _tile(n, 4096)

    def body(d_ref, idx_ref, o_ref):
        onehot = lax.broadcasted_iota(jnp.int32, (N_REL_PAD, tc), 0) == idx_ref[...]
        part = _dot3(d_ref[...], onehot, ((1,), (1,)), True)

        @pl.when(pl.program_id(0) == 0)
        def _():
            o_ref[...] = part

        @pl.when(pl.program_id(0) > 0)
        def _():
            o_ref[...] += part

    return pl.pallas_call(
        body, name=name, grid=(n // tc,),
        in_specs=[pl.BlockSpec((H, tc), lambda j: (0, j)), pl.BlockSpec((1, tc), lambda j: (0, j))],
        out_specs=pl.BlockSpec((H, N_REL_PAD), lambda j: (0, 0)), out_shape=jax.ShapeDtypeStruct((H, N_REL_PAD), f32),
        compiler_params=_params(("arbitrary",)),
    )(dbias, idx)


def _head_norm(t, gain):
    return t * lax.rsqrt(jnp.mean(t * t, axis=-1, keepdims=True) + EPS) * gain


def _attn_chunks(qs, kbs, vbs, qg, bias, ns):
    r = range(len(qs))
    qh = [_head_norm(qs[j], qg) for j in r]
    s = [_dotb(qh[j], kbs[j], ((1,), (1,))) * (HEAD ** -0.5) + bias for j in r]
    col = lax.broadcasted_iota(jnp.int32, (1, BAND * CHUNK), 1)
    s = [jnp.where(ns[j] * CHUNK - PAD + col >= 0, s[j], NEG) for j in r]
    e = [jnp.exp(s[j] - jnp.max(s[j], axis=-1, keepdims=True)) for j in r]
    p = [e[j] / jnp.sum(e[j], axis=-1, keepdims=True) for j in r]
    return [_dotb(p[j], vbs[j], ((1,), (0,))) for j in r]


def _attn_fwd(proj, q_gain, k_gain, bias, off, n_heads, name):
    T = proj.shape[0]
    nc = T // CHUNK
    H = n_heads
    CB = _group(nc, ATTN_CHUNKS)
    o0 = off // HEAD
    full = lambda k: pl.BlockSpec((T, HEAD), lambda h, c: (0, o0 + k * H + h))
    vec = pl.BlockSpec((1, HEAD), lambda h, c: (0, 0))

    def body(q_ref, k_ref, v_ref, qg_ref, kg_ref, b_ref, y_ref, kp, vp):
        c = pl.program_id(1)

        @pl.when(c == 0)
        def _():
            kp[pl.ds(0, PAD), :] = jnp.zeros((PAD, HEAD), f32)
            vp[pl.ds(0, PAD), :] = jnp.zeros((PAD, HEAD), f32)
            kp[pl.ds(PAD, T), :] = _head_norm(k_ref[...], kg_ref[...])
            vp[pl.ds(PAD, T), :] = v_ref[...]

        ns = [c * CB + j for j in range(CB)]
        rows = [pl.ds(j * CHUNK, CHUNK) for j in range(CB)]
        bands = [pl.ds(pl.multiple_of(n * CHUNK, CHUNK), BAND * CHUNK) for n in ns]
        outs = _attn_chunks([q_ref[r, :] for r in rows], [kp[b, :] for b in bands], [vp[b, :] for b in bands], qg_ref[...], b_ref[...], ns)
        for r, o in zip(rows, outs):
            y_ref[r, :] = o.astype(bf16)

    return pl.pallas_call(
        body, name=name, grid=(H, nc // CB),
        in_specs=[pl.BlockSpec((CB * CHUNK, HEAD), lambda h, c: (c, o0 + h)), full(1), full(2), vec, vec,
                  pl.BlockSpec((None, CHUNK, BAND * CHUNK), lambda h, c: (h, 0, 0))],
        out_specs=pl.BlockSpec((CB * CHUNK, HEAD), lambda h, c: (c, h)), out_shape=jax.ShapeDtypeStruct((T, H * HEAD), bf16),
        scratch_shapes=[pltpu.VMEM((T + PAD, HEAD), f32), pltpu.VMEM((T + PAD, HEAD), f32)],
        compiler_params=_params(("parallel", "arbitrary")),
    )(proj, proj, proj, q_gain, k_gain, bias)


def _attn_bwd(proj, q_gain, k_gain, bias, dy, off, n_heads, name):
    T = proj.shape[0]
    nc = T // CHUNK
    H = n_heads
    CB = _group(nc, ATTN_CHUNKS)
    o0 = off // HEAD
    full = lambda k: pl.BlockSpec((T, HEAD), lambda h, c: (0, o0 + k * H + h))
    full_out = pl.BlockSpec((T, HEAD), lambda h, c: (0, h))
    vec = pl.BlockSpec((1, HEAD), lambda h, c: (0, 0))
    chunk_out = pl.BlockSpec((CB * CHUNK, HEAD), lambda h, c: (c, h))
    bias_blk = pl.BlockSpec((None, CHUNK, BAND * CHUNK), lambda h, c: (h, 0, 0))

    def body(q_ref, k_ref, v_ref, qg_ref, kg_ref, b_ref, dy_ref, dq_ref, dk_ref, dv_ref, db_ref, dqg_ref, dkg_ref, kp, vp, dkp, dvp):
        h = pl.program_id(0)
        c = pl.program_id(1)

        @pl.when(c == 0)
        def _():
            kp[pl.ds(0, PAD), :] = jnp.zeros((PAD, HEAD), f32)
            vp[pl.ds(0, PAD), :] = jnp.zeros((PAD, HEAD), f32)
            kp[pl.ds(PAD, T), :] = _head_norm(k_ref[...], kg_ref[...])
            vp[pl.ds(PAD, T), :] = v_ref[...]
            dkp[...] = jnp.zeros((T + PAD, HEAD), f32)
            dvp[...] = jnp.zeros((T + PAD, HEAD), f32)
            db_ref[...] = jnp.zeros((CHUNK, BAND * CHUNK), f32)

        @pl.when(jnp.logical_and(h == 0, c == 0))
        def _():
            dqg_ref[...] = jnp.zeros((1, HEAD), f32)
            dkg_ref[...] = jnp.zeros((1, HEAD), f32)

        ns = [c * CB + j for j in range(CB)]
        rows = [pl.ds(j * CHUNK, CHUNK) for j in range(CB)]
        bands = [pl.ds(pl.multiple_of(n * CHUNK, CHUNK), BAND * CHUNK) for n in ns]
        _, vjp = jax.vjp(functools.partial(_attn_chunks, ns=ns), [q_ref[r, :] for r in rows], [kp[b, :] for b in bands],
                         [vp[b, :] for b in bands], qg_ref[...], b_ref[...])
        dqs, dkbs, dvbs, dqg, db = vjp([dy_ref[r, :] for r in rows])
        db_ref[...] += db
        dqg_ref[...] += dqg
        for r, b, dq, dkb, dvb in zip(rows, bands, dqs, dkbs, dvbs):
            dq_ref[r, :] = dq.astype(bf16)
            dkp[b, :] += dkb
            dvp[b, :] += dvb

        @pl.when(c == nc // CB - 1)
        def _():
            _, nvjp = jax.vjp(_head_norm, k_ref[...], kg_ref[...])
            dk, dkg = nvjp(dkp[pl.ds(PAD, T), :])
            dk_ref[...] = dk.astype(bf16)
            dv_ref[...] = dvp[pl.ds(PAD, T), :].astype(bf16)
            dkg_ref[...] += dkg

    act = jax.ShapeDtypeStruct((T, H * HEAD), bf16)
    gvec = jax.ShapeDtypeStruct((1, HEAD), f32)
    pad_buf = pltpu.VMEM((T + PAD, HEAD), f32)
    return pl.pallas_call(
        body, name=name, grid=(H, nc // CB),
        in_specs=[pl.BlockSpec((CB * CHUNK, HEAD), lambda h, c: (c, o0 + h)), full(1), full(2), vec, vec, bias_blk, chunk_out],
        out_specs=(chunk_out, full_out, full_out, bias_blk, vec, vec),
        out_shape=(act, act, act, jax.ShapeDtypeStruct((H, CHUNK, BAND * CHUNK), f32), gvec, gvec),
        scratch_shapes=[pad_buf, pad_buf, pad_buf, pad_buf],
        compiler_params=_params(("arbitrary", "arbitrary")),
    )(proj, proj, proj, q_gain, k_gain, bias, dy)


def _position():
    x, y, c = lax.axis_index("x"), lax.axis_index("y"), lax.axis_index("c")
    return x, y, c, 4 * x + 2 * y + c


def _flip(v, bit):
    return 1 - v if bit else v


def _chips(x, y):
    return [(1 - x, y), (x, 1 - y), (1 - x, 1 - y)]


def _seq_gather(shards, name, collective_id):
    n = len(shards)

    def body(*refs):
        ins, outs = refs[:n], refs[n:2 * n]
        send, recv, loc = refs[2 * n:]
        x, y, c, me = _position()
        sib = (x, y, 1 - c)
        sel = lambda a, b: c * a + (1 - c) * b
        n1 = (sel(1 - x, x), sel(y, 1 - y))
        n2 = (sel(x, 1 - x), sel(1 - y, y))
        far = (1 - x, 1 - y)
        idx = lambda chip, core: 4 * chip[0] + 2 * chip[1] + core
        barrier = pltpu.get_barrier_semaphore()
        for peer in [sib, (*n1, c), (*n2, c)]:
            pl.semaphore_signal(barrier, inc=1, device_id=peer, device_id_type=MESH)
        pl.semaphore_wait(barrier, 3)

        def copy(w, k, src, blk, to):
            return pltpu.make_async_remote_copy(src_ref=src, dst_ref=outs[w].at[blk], send_sem=send.at[7 * w + k], recv_sem=recv.at[7 * w + k],
                                                device_id=to, device_id_type=MESH)

        mine = [pltpu.make_async_copy(ins[w], outs[w].at[me], loc.at[w]) for w in range(n)]
        for cp in mine:
            cp.start()
        sent = [copy(w, 1, ins[w], me, (*n1, c)) for w in range(n)] + [copy(w, 2, ins[w], me, (*n2, c)) for w in range(n)]
        sent += [copy(w, 0, ins[w], me, sib) for w in range(n)]
        for cp in sent:
            cp.start()
        for k, chip in ((1, n1), (2, n2), (3, far)):
            blk = idx(chip, c)
            for w in range(n):
                copy(w, k, ins[w], blk, sib).wait_recv()
                if k == 1:
                    sent.append(copy(w, 3, outs[w].at[blk], blk, (*n2, c)))
                    sent[-1].start()
                sent.append(copy(w, 3 + k, outs[w].at[blk], blk, sib))
                sent[-1].start()
        for w in range(n):
            copy(w, 0, ins[w], idx((x, y), 1 - c), sib).wait_recv()
        for k, chip in ((4, n2), (5, n1), (6, far)):
            for w in range(n):
                copy(w, k, ins[w], idx(chip, 1 - c), sib).wait_recv()
        for cp in sent:
            cp.wait_send()
        for cp in mine:
            cp.wait()

    return pl.kernel(
        body, out_type=tuple(jax.ShapeDtypeStruct((NDEV,) + s.shape, s.dtype) for s in shards),
        mesh=plsc.ScalarSubcoreMesh(axis_name="sequencer", num_cores=1), name=name,
        scratch_types=(pltpu.SemaphoreType.DMA((7 * n,)), pltpu.SemaphoreType.DMA((7 * n,)), pltpu.SemaphoreType.DMA((n,))),
        compiler_params=pltpu.CompilerParams(collective_id=collective_id),
    )(*shards)


NCHIP = 4


def _seq_pair_exchange(grads, name, collective_id, after=()):
    n, na = len(grads), len(after)

    def body(*refs):
        ins, outs = refs[:n], refs[n + na:2 * n + na]
        send, recv = refs[2 * n + na:]
        x, y, c, me = _position()
        sib = (x, y, 1 - c)
        barrier = pltpu.get_barrier_semaphore()
        pl.semaphore_signal(barrier, inc=1, device_id=sib, device_id_type=MESH)
        pl.semaphore_wait(barrier, 1)
        copies = [pltpu.make_async_remote_copy(src_ref=ins[w].at[2 * k + (1 - c)], dst_ref=outs[w].at[k], send_sem=send.at[NCHIP * w + k],
                                               recv_sem=recv.at[NCHIP * w + k], device_id=sib, device_id_type=MESH)
                  for w in range(n) for k in range(NCHIP)]
        for cp in copies:
            cp.start()
        for cp in copies:
            cp.wait_recv()
        for cp in copies:
            cp.wait_send()

    return pl.kernel(
        body, out_type=tuple(jax.ShapeDtypeStruct((NCHIP,) + g.shape[1:], g.dtype) for g in grads),
        mesh=plsc.ScalarSubcoreMesh(axis_name="sequencer", num_cores=1), name=name,
        scratch_types=(pltpu.SemaphoreType.DMA((NCHIP * n,)), pltpu.SemaphoreType.DMA((NCHIP * n,))),
        compiler_params=pltpu.CompilerParams(collective_id=collective_id),
    )(*grads, *after)


def _pair_add(grad, sib_part, name, after=()):
    _, R, C = grad.shape
    tr = _tile(R, 1024, 16)
    core = jnp.reshape(lax.axis_index("c"), (1,)).astype(jnp.int32)

    def body(c_ref, g_ref, s_ref, *rest):
        rest[-1][...] = (g_ref[...].astype(f32) + s_ref[...].astype(f32)).astype(bf16)

    blk = pl.BlockSpec((None, tr, C), lambda k, i, c_ref: (k, i, 0))
    return pl.pallas_call(
        body, name=name,
        grid_spec=pltpu.PrefetchScalarGridSpec(
            num_scalar_prefetch=1, grid=(NCHIP, R // tr),
            in_specs=[pl.BlockSpec((None, tr, C), lambda k, i, c_ref: (2 * k + c_ref[0], i, 0)), blk]
            + [pl.BlockSpec(memory_space=pl.ANY)] * len(after), out_specs=blk),
        out_shape=jax.ShapeDtypeStruct((NCHIP, R, C), bf16), compiler_params=_params(("parallel", "parallel")),
    )(core, grad, sib_part, *after)


def _seq_chip_exchange(sums, name, collective_id, after=()):
    n, na = len(sums), len(after)

    def body(*refs):
        ins, outs = refs[:n], refs[n + na:2 * n + na]
        send, recv, loc = refs[2 * n + na:]
        x, y, c, me = _position()
        chips = _chips(x, y)
        mine = 2 * x + y
        barrier = pltpu.get_barrier_semaphore()
        for px, py in chips:
            pl.semaphore_signal(barrier, inc=1, device_id=(px, py, c), device_id_type=MESH)
        pl.semaphore_wait(barrier, 3)
        local = [pltpu.make_async_copy(ins[w].at[mine], outs[w].at[mine], loc.at[w]) for w in range(n)]
        for cp in local:
            cp.start()
        sends, waits = [], []
        for j, (px, py) in enumerate(chips):
            for w in range(n):
                sems = dict(send_sem=send.at[3 * w + j], recv_sem=recv.at[3 * w + j], device_id=(px, py, c), device_id_type=MESH)
                sends.append(pltpu.make_async_remote_copy(src_ref=ins[w].at[2 * px + py], dst_ref=outs[w].at[mine], **sems))
                waits.append(pltpu.make_async_remote_copy(src_ref=ins[w].at[2 * px + py], dst_ref=outs[w].at[2 * px + py], **sems))
        for cp in sends:
            cp.start()
        for cp in waits:
            cp.wait_recv()
        for cp in sends:
            cp.wait_send()
        for cp in local:
            cp.wait()

    return pl.kernel(
        body, out_type=tuple(jax.ShapeDtypeStruct(s.shape, s.dtype) for s in sums),
        mesh=plsc.ScalarSubcoreMesh(axis_name="sequencer", num_cores=1), name=name,
        scratch_types=(pltpu.SemaphoreType.DMA((3 * n,)), pltpu.SemaphoreType.DMA((3 * n,)), pltpu.SemaphoreType.DMA((n,))),
        compiler_params=pltpu.CompilerParams(collective_id=collective_id),
    )(*sums, *after)


def _reduce_scatter(grads, tag, ids, after=(), add_after=()):
    sib_parts = _seq_pair_exchange(grads, "pair_exchange_" + tag, ids[0], after=after)
    sums = [_pair_add(g, s, "pair_add_%s%d" % (tag, i), after=add_after) for i, (g, s) in enumerate(zip(grads, sib_parts))]
    return _seq_chip_exchange(sums, "chip_exchange_" + tag, ids[1]), sums


def _small_all_reduce(v, name):
    R, C = v.shape

    def body(v_ref, o_ref, buf, send, recv):
        x, y, c, me = _position()
        buf[me] = v_ref[...]
        sends, waits = [], []
        for r in range(1, NDEV):
            px, py, pc = _flip(x, r & 4), _flip(y, r & 2), _flip(c, r & 1)
            peer = 4 * px + 2 * py + pc
            sends.append(pltpu.make_async_remote_copy(src_ref=v_ref, dst_ref=buf.at[me], send_sem=send.at[r - 1], recv_sem=recv.at[r - 1],
                                                      device_id=(px, py, pc), device_id_type=MESH))
            waits.append(pltpu.make_async_remote_copy(src_ref=v_ref, dst_ref=buf.at[peer], send_sem=send.at[r - 1], recv_sem=recv.at[r - 1],
                                                      device_id=(px, py, pc), device_id_type=MESH))
        for cp in sends:
            cp.start()
        for cp in waits:
            cp.wait_recv()
        for cp in sends:
            cp.wait_send()
        acc = buf[0]
        for i in range(1, NDEV):
            acc = acc + buf[i]
        o_ref[...] = acc

    vm = pl.BlockSpec(memory_space=pltpu.VMEM)
    return pl.pallas_call(
        body, name=name, in_specs=[vm], out_specs=vm, out_shape=jax.ShapeDtypeStruct((R, C), f32),
        scratch_shapes=[pltpu.VMEM((NDEV, R, C), f32), pltpu.SemaphoreType.DMA((7,)), pltpu.SemaphoreType.DMA((7,))],
    )(v)


def _adamw_math(w, g, m, v):
    m = ADAM_B1 * m + (1.0 - ADAM_B1) * g
    v = ADAM_B2 * v + (1.0 - ADAM_B2) * (g * g)
    m_hat = m / (1.0 - ADAM_B1 ** ADAM_STEP)
    v_hat = v / (1.0 - ADAM_B2 ** ADAM_STEP)
    delta = -ADAM_LR * (m_hat / (jnp.sqrt(v_hat) + ADAM_EPS) + ADAM_WD * w)
    return delta, m, v


def _adamw_parts(w, m, v, parts, name, after=()):
    R, C = w.shape
    tr = _tile(R, 128, 16)
    blk = pl.BlockSpec((tr, C), lambda i: (i, 0))

    def body(w_ref, m_ref, v_ref, p_ref, *rest):
        g_ref, d_ref, mo_ref, vo_ref = rest[len(after):]
        g = p_ref[0].astype(f32)
        for i in range(1, NCHIP):
            g = g + p_ref[i].astype(f32)
        d, mn, vn = _adamw_math(w_ref[...], g, m_ref[...], v_ref[...])
        g_ref[...] = g
        d_ref[...] = d
        mo_ref[...] = mn
        vo_ref[...] = vn

    shp = jax.ShapeDtypeStruct((R, C), f32)
    return pl.pallas_call(
        body, name=name, grid=(R // tr,),
        in_specs=[blk, blk, blk, pl.BlockSpec((NCHIP, tr, C), lambda i: (0, i, 0))] + [pl.BlockSpec(a.shape, lambda i: (0, 0)) for a in after],
        out_specs=(blk, blk, blk, blk), out_shape=(shp, shp, shp, shp), compiler_params=_params(("parallel",)),
    )(w, m, v, parts, *after)


def _adamw_small(w, g, m, v, name):
    def body(w_ref, g_ref, m_ref, v_ref, d_ref, mo_ref, vo_ref):
        d, mn, vn = _adamw_math(w_ref[...], g_ref[...], m_ref[...], v_ref[...])
        d_ref[...] = d
        mo_ref[...] = mn
        vo_ref[...] = vn

    shp = jax.ShapeDtypeStruct(w.shape, f32)
    return pl.pallas_call(body, name=name, out_shape=(shp, shp, shp))(w, g, m, v)


SMALL_COLS = 1024


def _pack(arrs):
    flat = jnp.concatenate([a.reshape(-1) for a in arrs])
    rows = -(-flat.shape[0] // (8 * SMALL_COLS)) * 8
    return jnp.pad(flat, (0, rows * SMALL_COLS - flat.shape[0])).reshape(rows, SMALL_COLS)


def _unpack(packed, like):
    flat = packed.reshape(-1)
    out, pos = [], 0
    for a in like:
        out.append(flat[pos:pos + a.size].reshape(a.shape))
        pos += a.size
    return out


def kernel(x, w_in, b_gate, norm_mix, norm_ffn, hgrn_lb_logits, hgrn_out_gain, q_gain, k_gain, rel_bias, w_proj_a, w_proj_b, w_out, w_ffn_in, w_ffn_out, loss_target, m_w_in, m_b_gate, m_norm_mix, m_norm_ffn, m_hgrn_lb_logits, m_hgrn_out_gain, m_q_gain, m_k_gain, m_rel_bias, m_w_proj_a, m_w_proj_b, m_w_out, m_w_ffn_in, m_w_ffn_out, v_w_in, v_b_gate, v_norm_mix, v_norm_ffn, v_hgrn_lb_logits, v_hgrn_out_gain, v_q_gain, v_k_gain, v_rel_bias, v_w_proj_a, v_w_proj_b, v_w_out, v_w_ffn_in, v_w_ffn_out):
    xs = x[0]
    target = loss_target[0]
    T, D = xs.shape
    d_a = hgrn_out_gain.shape[-1]
    H = d_a // HEAD
    d_b = d_a
    off_b = 4 * d_a
    off_g = off_b + 3 * d_b
    assert rel_bias.shape[1] == H and T % CHUNK == 0 and T // CHUNK > N_PAST

    big_w = [w_in[0], w_proj_a[0], w_proj_b[0], w_out[0], w_ffn_in[0], w_ffn_out[0]]
    big_m = [m_w_in[0], m_w_proj_a[0], m_w_proj_b[0], m_w_out[0], m_w_ffn_in[0], m_w_ffn_out[0]]
    big_v = [v_w_in[0], v_w_proj_a[0], v_w_proj_b[0], v_w_out[0], v_w_ffn_in[0], v_w_ffn_out[0]]

    sh = [w.astype(bf16) for w in big_w]
    (g_in,) = _seq_gather(sh[0:1], "gather_a", 1)
    g_pa, g_pb, g_out = _seq_gather(sh[1:4], "gather_b", 2)
    (g_fin,) = _seq_gather(sh[4:5], "gather_c", 3)
    (g_fout,) = _seq_gather(sh[5:6], "gather_d", 4)

    h = _rms_fwd(xs, None, norm_mix, "rms_mix")
    proj = _mm(h, g_in, mode="nn", b_blocked=True, name="mm_proj")
    y_a, states = _hgrn_fwd(proj, hgrn_lb_logits, hgrn_out_gain, H, "hgrn_fwd")
    idx = jnp.asarray(_rel_index())
    rb_pad = jnp.pad(rel_bias[0], ((0, 0), (0, N_REL_PAD - N_REL)))
    bias = _bias_table(rb_pad, idx, "bias_table").reshape(H, CHUNK, BAND * CHUNK)
    y_b = _attn_fwd(proj, q_gain, k_gain, bias, off_b, H, "attn_fwd")
    wg_out = g_out.reshape(-1, g_out.shape[-1])
    wg_fout = g_fout.reshape(-1, g_fout.shape[-1])
    pa = _mm(y_a, g_pa, mode="nn", b_blocked=True, tm=2048, name="mm_proj_a")
    pb = _mm(y_b, g_pb, mode="nn", b_blocked=True, tm=2048, name="mm_proj_b")
    merged = _merge_fwd(pa, pb, proj, b_gate, off_g, "merge_fwd")
    mo = _mm(merged, wg_out, mode="nn", name="mm_out")
    x1, h2 = _rms_fwd(xs, mo, norm_ffn, "rms_ffn")
    gu, act = _ffn_in_swiglu(h2, g_fin, "mm_ffn_in")
    fo = _mm(act, wg_fout, mode="nn", tk=2816, name="mm_ffn_out")
    dy, loss_acc = _loss_head(x1, fo, target, "loss_head")
    loss_part = loss_acc[0:1, 0:1] * (0.5 / D)

    gw_fout = _mm(act, dy, mode="tn", out_dtype=bf16, tm=1408, name="mm_gw_ffn_out")
    dgu = _d_act_swiglu(dy, wg_fout, gu, "mm_d_act")
    gw_fin = _mm(h2, dgu, mode="tn", b_stacked=True, out_blocked=True, out_dtype=bf16, tn=g_fin.shape[-1], name="mm_gw_ffn_in")
    dh2 = _mm(dgu, g_fin, mode="nt", a_stacked=True, b_blocked=True, name="mm_d_h2")
    (p_fout, p_fin), sums_a = _reduce_scatter([gw_fout.reshape(NDEV, -1, D), gw_fin], "a", (5, 6), add_after=(dh2,))
    dx1, g_norm_ffn = _rms_bwd(x1, norm_ffn, dh2, dy, "rms_ffn_bwd", after=sums_a)

    gw_out = _mm(merged, dx1, mode="tn", out_dtype=bf16, name="mm_gw_out")
    dpa, dpb, dgl, g_b_gate = _d_merged_branches(dx1, wg_out, pa, pb, proj, b_gate, off_g, "mm_d_merged")
    dy_a = _mm(dpa, g_pa, mode="nt", b_blocked=True, tm=2048, name="mm_d_ya")
    dy_b = _mm(dpb, g_pb, mode="nt", b_blocked=True, tm=2048, name="mm_d_yb")
    gw_pa = _mm(y_a, dpa, mode="tn", out_blocked=True, out_dtype=bf16, tn=g_pa.shape[-1], name="mm_gw_proj_a")
    gw_pb = _mm(y_b, dpb, mode="tn", out_blocked=True, out_dtype=bf16, tn=g_pb.shape[-1], name="mm_gw_proj_b")

    dq_a, df_a, di_a, dg_a, g_logits, g_gain = _hgrn_bwd(proj, hgrn_lb_logits, hgrn_out_gain, states, dy_a, H, "hgrn_bwd")
    dq_b, dk_b, dv_b, dbias, g_qg, g_kg = _attn_bwd(proj, q_gain, k_gain, bias, dy_b, off_b, H, "attn_bwd")
    g_rel_pad = _bias_table_bwd(dbias.reshape(H, -1), idx, "bias_table_bwd")
    g_rel = g_rel_pad[:, :N_REL]
    dproj = jnp.concatenate([dq_a, df_a, di_a, dg_a, dq_b, dk_b, dv_b, dgl[0], dgl[1]], axis=1)
    (p_out, p_pa, p_pb), sums_b = _reduce_scatter([gw_out.reshape(NDEV, -1, D), gw_pa, gw_pb], "b", (7, 8), after=(g_gain, p_fout, p_fin), add_after=(dy_b,))
    gw_in = _mm(h, dproj, mode="tn", out_blocked=True, out_dtype=bf16, tn=g_in.shape[-1], name="mm_gw_in", after=sums_b)
    (p_in,), sums_c = _reduce_scatter([gw_in], "c", (9, 10), after=(p_out, p_pa, p_pb), add_after=(g_rel_pad,))
    dh = _mm(dproj, g_in, mode="nt", b_blocked=True, name="mm_d_h", after=sums_c)
    grad_x, g_norm_mix = _rms_bwd(xs, norm_mix, dh, dx1, "rms_mix_bwd")

    parts = [p_in, p_pa, p_pb, p_out, p_fin, p_fout]
    names = ["w_in", "w_proj_a", "w_proj_b", "w_out", "w_ffn_in", "w_ffn_out"]
    big = {}
    for nm, w, m, v, p in zip(names, big_w, big_m, big_v, parts):
        big[nm] = [o[None] for o in _adamw_parts(w, m, v, p, "adamw_" + nm, after=() if nm == "w_in" else (g_norm_mix,))]

    small_names = ["b_gate", "norm_mix", "norm_ffn", "hgrn_lb_logits", "hgrn_out_gain", "q_gain", "k_gain", "rel_bias"]
    small_w = [b_gate, norm_mix, norm_ffn, hgrn_lb_logits, hgrn_out_gain, q_gain, k_gain, rel_bias]
    small_m = [m_b_gate, m_norm_mix, m_norm_ffn, m_hgrn_lb_logits, m_hgrn_out_gain, m_q_gain, m_k_gain, m_rel_bias]
    small_v = [v_b_gate, v_norm_mix, v_norm_ffn, v_hgrn_lb_logits, v_hgrn_out_gain, v_q_gain, v_k_gain, v_rel_bias]
    small_g = [g_b_gate.reshape(1, -1), g_norm_mix, g_norm_ffn, g_logits, g_gain, g_qg, g_kg, g_rel[None], loss_part]
    g_sum = _small_all_reduce(_pack(small_g), "reduce_small")
    loss = _unpack(g_sum, small_g)[-1].reshape(())
    d_s, m_s, v_s = _adamw_small(_pack(small_w), g_sum, _pack(small_m), _pack(small_v), "adamw_small")
    small = {}
    for nm, g, d, m, v in zip(small_names, _unpack(g_sum, small_w), _unpack(d_s, small_w), _unpack(m_s, small_w), _unpack(v_s, small_w)):
        small[nm] = [g, d, m, v]

    order = ["w_in", "b_gate", "norm_mix", "norm_ffn", "hgrn_lb_logits", "hgrn_out_gain", "q_gain", "k_gain", "rel_bias",
             "w_proj_a", "w_proj_b", "w_out", "w_ffn_in", "w_ffn_out"]
    res = {**big, **small}
    outs = [loss, grad_x[None]]
    for k in range(4):
        outs += [res[nm][k] for nm in order]
    return tuple(outs)
```

```python
import functools

import numpy as np
import jax
import jax.numpy as jnp
from jax import lax
from jax.experimental import pallas as pl
from jax.experimental.pallas import tpu as pltpu
from jax.experimental.pallas import tpu_sc as plsc

f32 = jnp.float32
bf16 = jnp.bfloat16
MESH = pl.DeviceIdType.MESH
AXES = ("x", "y", "c")
NDEV = 8

CHUNK = 64
HEAD = 128
SUB = 8
HGRN_BWD_HEADS = 8
ATTN_CHUNKS = 8
N_PAST = 8
BAND = N_PAST + 1
PAD = N_PAST * CHUNK
REL_FUTURE = CHUNK - 1
REL_PAST = 2 * CHUNK - 1
N_REL = REL_FUTURE + REL_PAST + 1
N_REL_PAD = 256
EPS = 1e-6
NEG = -1e30

ADAM_LR = 0.001
ADAM_B1 = 0.9
ADAM_B2 = 0.999
ADAM_EPS = 1e-08
ADAM_WD = 0.01
ADAM_STEP = 10

VMEM_LIMIT = 56 * 1024 * 1024


def _params(sem=None):
    return pltpu.CompilerParams(dimension_semantics=sem, vmem_limit_bytes=VMEM_LIMIT)


def _tile(n, pref, unit=128):
    if n <= pref:
        return n
    t = (pref // unit) * unit
    while t >= unit:
        if n % t == 0:
            return t
        t -= unit
    return n


_sigmoid = jax.nn.sigmoid


def _mm(a, b, *, mode, name, b_blocked=False, out_blocked=False, out_dtype=f32, tm=1024, tn=1024, tk=2048, after=(),
        b_stacked=False):
    if mode == "tn":
        K, M = a.shape
    else:
        M, K = a.shape
    if b_blocked:
        nb, mid, cb = b.shape
        if mode == "nn":
            assert mid == K
            N, tn = nb * cb, cb
        else:
            assert mode == "nt" and nb * cb == K
            N, tk = mid, cb
    elif b_stacked:
        assert mode == "tn"
        N = 2 * b.shape[2]
    else:
        N = b.shape[1] if mode in ("nn", "tn") else b.shape[0]
    tm = _tile(M, tm)
    tn = tn if (b_blocked and mode == "nn") or out_blocked else _tile(N, tn)
    tk = tk if b_blocked and mode == "nt" else _tile(K, tk)
    assert M % tm == 0 and N % tn == 0 and K % tk == 0
    nk = K // tk
    grid = (M // tm, N // tn, nk)
    if mode == "tn":
        a_spec = pl.BlockSpec((tk, tm), lambda i, j, k: (k, i))
    else:
        a_spec = pl.BlockSpec((tm, tk), lambda i, j, k: (i, k))
    if mode == "nn":
        b_spec = pl.BlockSpec((None, tk, cb), lambda i, j, k: (j, k, 0)) if b_blocked else pl.BlockSpec((tk, tn), lambda i, j, k: (k, j))
    elif mode == "nt":
        b_spec = pl.BlockSpec((None, tn, cb), lambda i, j, k: (k, j, 0)) if b_blocked else pl.BlockSpec((tn, tk), lambda i, j, k: (j, k))
    elif b_stacked:
        nh = N // 2 // tn
        b_spec = pl.BlockSpec((None, tk, tn), lambda i, j, k: (j // nh, k, j % nh))
    else:
        b_spec = pl.BlockSpec((tk, tn), lambda i, j, k: (k, j))
    if out_blocked:
        out_shape = jax.ShapeDtypeStruct((N // tn, M, tn), out_dtype)
        o_spec = pl.BlockSpec((None, tm, tn), lambda i, j, k: (j, i, 0))
    else:
        out_shape = jax.ShapeDtypeStruct((M, N), out_dtype)
        o_spec = pl.BlockSpec((tm, tn), lambda i, j, k: (i, j))
    dims = {"nn": ((1,), (0,)), "nt": ((1,), (1,)), "tn": ((0,), (0,))}[mode]

    def body(a_ref, b_ref, *rest):
        o_ref, acc = rest[len(after)], rest[len(after) + 1:]
        p = lax.dot_general(a_ref[...].astype(bf16), b_ref[...].astype(bf16), (dims, ((), ())), preferred_element_type=f32)
        if nk == 1:
            o_ref[...] = p.astype(out_dtype)
        else:
            acc_ref = acc[0]
            k = pl.program_id(2)

            @pl.when(k == 0)
            def _():
                acc_ref[...] = p

            @pl.when(k > 0)
            def _():
                acc_ref[...] += p

            @pl.when(k == nk - 1)
            def _():
                o_ref[...] = acc_ref[...].astype(out_dtype)

    return pl.pallas_call(
        body, name=name, grid=grid, in_specs=[a_spec, b_spec] + [pl.BlockSpec(memory_space=pl.ANY)] * len(after), out_specs=o_spec,
        out_shape=out_shape, scratch_shapes=[pltpu.VMEM((tm, tn), f32)] if nk > 1 else [],
        compiler_params=_params(("parallel", "parallel", "arbitrary")),
    )(a, b, *after)


def _mm_nt_blocked(a, b, name, *, a_stacked=False, after=(), tm=256, tn=512):
    nb, N, cb = b.shape
    M = a.shape[1] if a_stacked else a.shape[0]
    tm, tn = _tile(M, tm, 8), _tile(N, tn)
    half = nb // 2

    def body(a_ref, b_ref, *rest):
        dn = (((1,), (1,)), ((), ()))
        p = None
        for k in range(nb):
            a_k = a_ref[k // half, :, (k % half) * cb:(k % half + 1) * cb] if a_stacked else a_ref[:, k * cb:(k + 1) * cb]
            d = lax.dot_general(a_k, b_ref[k], dn, preferred_element_type=f32)
            p = d if p is None else p + d
        rest[len(after)][...] = p

    a_spec = pl.BlockSpec((2, tm, half * cb), lambda j, i: (0, i, 0)) if a_stacked else pl.BlockSpec((tm, nb * cb), lambda j, i: (i, 0))
    return pl.pallas_call(
        body, name=name, grid=(N // tn, M // tm),
        in_specs=[a_spec, pl.BlockSpec((nb, tn, cb), lambda j, i: (0, j, 0))] + [pl.BlockSpec(memory_space=pl.ANY)] * len(after),
        out_specs=pl.BlockSpec((tm, tn), lambda j, i: (i, j)), out_shape=jax.ShapeDtypeStruct((M, N), f32),
        compiler_params=_params(("parallel", "parallel")),
    )(a, b, *after)


def _rms_fwd(x, gain, name):
    T, D = x.shape
    tr = _tile(T, 256, 8)
    row = pl.BlockSpec((tr, D), lambda i: (i, 0))

    def body(x_ref, g_ref, h_ref):
        xs = x_ref[...]
        r = lax.rsqrt(jnp.mean(xs * xs, axis=-1, keepdims=True) + EPS)
        h_ref[...] = (xs * r * g_ref[...]).astype(bf16)

    return pl.pallas_call(body, name=name, grid=(T // tr,), in_specs=[row, pl.BlockSpec((1, D), lambda i: (0, 0))], out_specs=row,
                          out_shape=jax.ShapeDtypeStruct((T, D), bf16), compiler_params=_params(("parallel",)))(x, gain)


def _rms_bwd(xs, gain, dh, extra, name, after=()):
    T, D = xs.shape
    tr = _tile(T, 256, 8)
    row = pl.BlockSpec((tr, D), lambda i: (i, 0))
    vec = pl.BlockSpec((1, D), lambda i: (0, 0))

    def body(x_ref, g_ref, dh_ref, e_ref, *rest):
        dx_ref, dg_ref = rest[len(after):]
        x = x_ref[...]
        r = lax.rsqrt(jnp.mean(x * x, axis=-1, keepdims=True) + EPS)
        xhat = x * r
        dh_v = dh_ref[...]
        gd = dh_v * g_ref[...]
        dx_ref[...] = e_ref[...] + r * (gd - xhat * jnp.mean(gd * xhat, axis=-1, keepdims=True))
        part = jnp.sum(dh_v * xhat, axis=0, keepdims=True)

        @pl.when(pl.program_id(0) == 0)
        def _():
            dg_ref[...] = part

        @pl.when(pl.program_id(0) > 0)
        def _():
            dg_ref[...] += part

    return pl.pallas_call(body, name=name, grid=(T // tr,),
                          in_specs=[row, vec, row, row] + [pl.BlockSpec(memory_space=pl.ANY)] * len(after), out_specs=(row, vec),
                          out_shape=(jax.ShapeDtypeStruct((T, D), f32), jax.ShapeDtypeStruct((1, D), f32)),
                          compiler_params=_params(("arbitrary",)))(xs, gain, dh, extra, *after)


def _proj_merge(y_a, y_b, w_a, w_b, proj, b_gate, off, name):
    T, K = y_a.shape
    nb, _, cb = w_a.shape
    D = nb * cb
    tm = _tile(T, 1024, 8)
    oa, ob = off // cb, (off + D) // cb
    blk = pl.BlockSpec((tm, cb), lambda j, i: (i, j))
    row = pl.BlockSpec((tm, K), lambda j, i: (i, 0))
    wsp = pl.BlockSpec((None, K, cb), lambda j, i: (j, 0, 0))

    def body(ya_ref, yb_ref, wa_ref, wb_ref, ga_ref, gb_ref, ba_ref, bb_ref, pa_ref, pb_ref, m_ref):
        dn = (((1,), (0,)), ((), ()))
        pa = lax.dot_general(ya_ref[...], wa_ref[...], dn, preferred_element_type=f32)
        pb = lax.dot_general(yb_ref[...], wb_ref[...], dn, preferred_element_type=f32)
        pa_ref[...] = pa
        pb_ref[...] = pb
        m_ref[...] = (_sigmoid(ga_ref[...] + ba_ref[...]) * pa + _sigmoid(gb_ref[...] + bb_ref[...]) * pb).astype(bf16)

    return pl.pallas_call(
        body, name=name, grid=(nb, T // tm),
        in_specs=[row, row, wsp, wsp, pl.BlockSpec((tm, cb), lambda j, i: (i, oa + j)), pl.BlockSpec((tm, cb), lambda j, i: (i, ob + j)),
                  pl.BlockSpec((1, cb), lambda j, i: (0, j)), pl.BlockSpec((1, cb), lambda j, i: (0, nb + j))],
        out_specs=(blk, blk, blk),
        out_shape=(jax.ShapeDtypeStruct((T, D), f32), jax.ShapeDtypeStruct((T, D), f32), jax.ShapeDtypeStruct((T, D), bf16)),
        compiler_params=_params(("parallel", "parallel")),
    )(y_a, y_b, w_a, w_b, proj, proj, b_gate, b_gate)


def _out_rms(merged, w_out, x, gain, name):
    T, K = merged.shape
    D = w_out.shape[1]
    tm = _tile(T, 256, 8)
    row = pl.BlockSpec((tm, D), lambda i: (i, 0))

    def body(m_ref, w_ref, x_ref, g_ref, x1_ref, h_ref):
        x1 = x_ref[...] + lax.dot_general(m_ref[...], w_ref[...], (((1,), (0,)), ((), ())), preferred_element_type=f32)
        x1_ref[...] = x1
        r = lax.rsqrt(jnp.mean(x1 * x1, axis=-1, keepdims=True) + EPS)
        h_ref[...] = (x1 * r * g_ref[...]).astype(bf16)

    return pl.pallas_call(
        body, name=name, grid=(T // tm,),
        in_specs=[pl.BlockSpec((tm, K), lambda i: (i, 0)), pl.BlockSpec((K, D), lambda i: (0, 0)), row, pl.BlockSpec((1, D), lambda i: (0, 0))],
        out_specs=(row, row), out_shape=(jax.ShapeDtypeStruct((T, D), f32), jax.ShapeDtypeStruct((T, D), bf16)),
        compiler_params=_params(("parallel",)),
    )(merged, w_out, x, gain)


def _d_merged_branches(dx, w_out, pa, pb, proj, b_gate, off, name):
    T, D = pa.shape
    tm, tn = _tile(T, 512, 8), _tile(D, 1024)
    oa, ob, nb = off // tn, (off + D) // tn, D // tn
    blk = pl.BlockSpec((tm, tn), lambda j, i: (i, j))

    def body(dx_ref, w_ref, pa_ref, pb_ref, ga_ref, gb_ref, ba_ref, bb_ref, dpa_ref, dpb_ref, dgl_ref, db_ref):
        dm = lax.dot_general(dx_ref[...].astype(bf16), w_ref[...], (((1,), (1,)), ((), ())), preferred_element_type=f32)
        sums = []
        for p_ref, gl_ref, b_ref, dp_ref, k in ((pa_ref, ga_ref, ba_ref, dpa_ref, 0), (pb_ref, gb_ref, bb_ref, dpb_ref, 1)):
            g = _sigmoid(gl_ref[...] + b_ref[...])
            dp_ref[...] = (dm * g).astype(bf16)
            dgl = dm * p_ref[...] * g * (1.0 - g)
            dgl_ref[k] = dgl.astype(bf16)
            sums.append(jnp.sum(dgl, axis=0, keepdims=True))

        @pl.when(pl.program_id(1) == 0)
        def _():
            db_ref[...] = jnp.zeros((2, 1, tn), f32)

        db_ref[0] += sums[0]
        db_ref[1] += sums[1]

    return pl.pallas_call(
        body, name=name, grid=(nb, T // tm),
        in_specs=[pl.BlockSpec((tm, D), lambda j, i: (i, 0)), pl.BlockSpec((tn, D), lambda j, i: (j, 0)), blk, blk,
                  pl.BlockSpec((tm, tn), lambda j, i: (i, oa + j)), pl.BlockSpec((tm, tn), lambda j, i: (i, ob + j)),
                  pl.BlockSpec((1, tn), lambda j, i: (0, j)), pl.BlockSpec((1, tn), lambda j, i: (0, nb + j))],
        out_specs=(blk, blk, pl.BlockSpec((2, tm, tn), lambda j, i: (0, i, j)), pl.BlockSpec((2, 1, tn), lambda j, i: (0, 0, j))),
        out_shape=(jax.ShapeDtypeStruct((T, D), bf16), jax.ShapeDtypeStruct((T, D), bf16), jax.ShapeDtypeStruct((2, T, D), bf16),
                   jax.ShapeDtypeStruct((2, 1, D), f32)),
        compiler_params=_params(("parallel", "arbitrary")),
    )(dx, w_out, pa, pb, proj, proj, b_gate, b_gate)


def _ffn_in_swiglu(h, w, name):
    T, K = h.shape
    nb, _, cb = w.shape
    half = nb // 2
    F = half * cb
    tm = _tile(T, 512, 8)

    def body(h_ref, wg_ref, wu_ref, gu_ref, act_ref):
        dn = (((1,), (0,)), ((), ()))
        hv = h_ref[...]
        g = lax.dot_general(hv, wg_ref[...], dn, preferred_element_type=f32)
        u = lax.dot_general(hv, wu_ref[...], dn, preferred_element_type=f32)
        gu_ref[0] = g
        gu_ref[1] = u
        act_ref[...] = (g * _sigmoid(g) * u).astype(bf16)

    return pl.pallas_call(
        body, name=name, grid=(half, T // tm),
        in_specs=[pl.BlockSpec((tm, K), lambda j, i: (i, 0)), pl.BlockSpec((None, K, cb), lambda j, i: (j, 0, 0)),
                  pl.BlockSpec((None, K, cb), lambda j, i: (j + half, 0, 0))],
        out_specs=(pl.BlockSpec((2, tm, cb), lambda j, i: (0, i, j)), pl.BlockSpec((tm, cb), lambda j, i: (i, j))),
        out_shape=(jax.ShapeDtypeStruct((2, T, F), f32), jax.ShapeDtypeStruct((T, F), bf16)),
        compiler_params=_params(("parallel", "parallel")),
    )(h, w, w)


def _d_act_swiglu(dy, w_out, gu, name):
    T, D = dy.shape
    F = w_out.shape[0]
    tm, tn = _tile(T, 512, 8), _tile(F, 1408)

    def body(dy_ref, w_ref, gu_ref, o_ref):
        halves = [pl.ds(r * (tm // 2), tm // 2) for r in range(2)]
        wv = w_ref[...]
        d = [lax.dot_general(dy_ref[r, :].astype(bf16), wv, (((1,), (1,)), ((), ())), preferred_element_type=f32) for r in halves]
        for r, dr in zip(halves, d):
            g = gu_ref[0, r, :]
            s = _sigmoid(g)
            o_ref[0, r, :] = (dr * gu_ref[1, r, :] * s * (1.0 + g * (1.0 - s))).astype(bf16)
            o_ref[1, r, :] = (dr * g * s).astype(bf16)

    blk = pl.BlockSpec((2, tm, tn), lambda j, i: (0, i, j))
    return pl.pallas_call(
        body, name=name, grid=(F // tn, T // tm),
        in_specs=[pl.BlockSpec((tm, D), lambda j, i: (i, 0)), pl.BlockSpec((tn, D), lambda j, i: (j, 0)), blk],
        out_specs=blk, out_shape=jax.ShapeDtypeStruct((2, T, F), bf16), compiler_params=_params(("parallel", "parallel")),
    )(dy, w_out, gu)


def _ffn_out_loss(act, w, x1, target, name):
    T, F = act.shape
    D = w.shape[1]
    tm, tn, tk = _tile(T, 512, 8), _tile(D, 1024), _tile(F, 2816)
    nk = F // tk
    blk = pl.BlockSpec((tm, tn), lambda i, j, k: (i, j))

    def body(a_ref, w_ref, x_ref, t_ref, dy_ref, l_ref, acc_ref):
        i, j, k = pl.program_id(0), pl.program_id(1), pl.program_id(2)
        p = lax.dot_general(a_ref[...], w_ref[...], (((1,), (0,)), ((), ())), preferred_element_type=f32)

        @pl.when(jnp.logical_and(jnp.logical_and(i == 0, j == 0), k == 0))
        def _():
            l_ref[...] = jnp.zeros((8, 128), f32)

        @pl.when(k == 0)
        def _():
            acc_ref[...] = p

        @pl.when(k > 0)
        def _():
            acc_ref[...] += p

        @pl.when(k == nk - 1)
        def _():
            d = acc_ref[...] + x_ref[...] - t_ref[...]
            dy_ref[...] = d * (1.0 / D)
            l_ref[...] += jnp.sum(jnp.sum(d * d, axis=1, keepdims=True), axis=0, keepdims=True)

    return pl.pallas_call(
        body, name=name, grid=(T // tm, D // tn, nk),
        in_specs=[pl.BlockSpec((tm, tk), lambda i, j, k: (i, k)), pl.BlockSpec((tk, tn), lambda i, j, k: (k, j)), blk, blk],
        out_specs=(blk, pl.BlockSpec((8, 128), lambda i, j, k: (0, 0))),
        out_shape=(jax.ShapeDtypeStruct((T, D), f32), jax.ShapeDtypeStruct((8, 128), f32)),
        scratch_shapes=[pltpu.VMEM((tm, tn), f32)],
        compiler_params=_params(("arbitrary", "arbitrary", "arbitrary")),
    )(act, w, x1, target)


_DIMS = {"nn": ((1,), (0,)), "nt": ((1,), (1,)), "tn": ((0,), (0,))}
_MODE = {v: k for k, v in _DIMS.items()}


def _dot_bf16(a, b, mode):
    return lax.dot_general(a.astype(bf16), b.astype(bf16), (_DIMS[mode], ((), ())), preferred_element_type=f32)


@functools.partial(jax.custom_vjp, nondiff_argnums=(2,))
def _dotm(a, b, mode):
    return _dot_bf16(a, b, mode)


def _dotm_fwd(a, b, mode):
    return _dot_bf16(a, b, mode), (a, b)


def _dotm_bwd(mode, res, g):
    a, b = res
    if mode == "nn":
        return _dot_bf16(g, b, "nt"), _dot_bf16(a, g, "tn")
    if mode == "nt":
        return _dot_bf16(g, b, "nn"), _dot_bf16(g, a, "tn")
    return _dot_bf16(b, g, "nt"), _dot_bf16(a, g, "nn")


_dotm.defvjp(_dotm_fwd, _dotm_bwd)


def _dotb(a, b, dims):
    return _dotm(a, b, _MODE[dims])


def _split3(v):
    def top(t):
        return lax.bitcast_convert_type(lax.bitcast_convert_type(t, jnp.uint32) & jnp.uint32(0xFFFF0000), f32)

    hi = top(v)
    mid = top(v - hi)
    low = (v - hi) - mid
    return hi.astype(bf16), mid.astype(bf16), low.astype(bf16)


def _dot3(v, m, dims, v_first):
    m = m.astype(bf16)
    dn = (dims, ((), ()))
    parts = [lax.dot_general(p, m, dn, preferred_element_type=f32) if v_first else lax.dot_general(m, p, dn, preferred_element_type=f32)
             for p in _split3(v)]
    return parts[0] + parts[1] + parts[2]


def _triangle_sum(v, lower):
    row = lax.broadcasted_iota(jnp.int32, (CHUNK, CHUNK), 0)
    col = lax.broadcasted_iota(jnp.int32, (CHUNK, CHUNK), 1)
    return _dot3(v, (col <= row) if lower else (col >= row), ((1,), (0,)), False)


@jax.custom_vjp
def _cumsum_rows(v):
    return _triangle_sum(v, True)


_cumsum_rows.defvjp(lambda v: (_triangle_sum(v, True), None), lambda _, g: (_triangle_sum(g, False),))


def _hgrn_heads(q, fl, iv, g, logits, gain, st):
    r = range(len(q))
    lb = [jax.nn.softmax(logits[j], axis=0)[0:1] for j in r]
    f = [lb[j] + (1.0 - lb[j]) * _sigmoid(fl[j]) for j in r]
    lf = [jnp.log(f[j]) for j in r]
    kk = [1.0 - f[j] for j in r]
    qs = [q[j] * _sigmoid(q[j]) for j in r]
    b = [_cumsum_rows(lf[j]) for j in r]
    b_last = [jnp.sum(lf[j], axis=0, keepdims=True) for j in r]
    o = [_dotb(qs[j] * jnp.exp(b[j]), st[j], ((1,), (1,))) for j in r]
    r3 = lax.broadcasted_iota(jnp.int32, (SUB, SUB, HEAD), 0)
    c3 = lax.broadcasted_iota(jnp.int32, (SUB, SUB, HEAD), 1)
    parts = [[] for _ in r]
    for i in range(CHUNK // SUB):
        lo, hi = i * SUB, (i + 1) * SUB
        bi = [b[j][lo:hi] for j in r]
        dec = [jnp.exp(jnp.where(c3 <= r3, bi[j][:, None, :] - bi[j][None, :, :], -jnp.inf)) for j in r]
        s = [jnp.sum(qs[j][lo:hi][:, None, :] * kk[j][lo:hi][None, :, :] * dec[j], axis=-1) for j in r]
        if i > 0:
            anchor = [jnp.max(bi[j], axis=0, keepdims=True) for j in r]
            qa = [qs[j][lo:hi] * jnp.exp(bi[j] - anchor[j]) for j in r]
            kd = [kk[j][:lo] * jnp.exp(anchor[j] - b[j][:lo]) for j in r]
            s = [jnp.concatenate([_dotb(qa[j], kd[j], ((1,), (1,))), s[j]], axis=1) for j in r]
        for j in r:
            parts[j].append(_dotb(s[j], iv[j][:hi], ((1,), (0,))))
    o = [o[j] + jnp.concatenate(parts[j], axis=0) for j in r]
    st_new = [st[j] * jnp.exp(b_last[j]) + _dotb(iv[j], kk[j] * jnp.exp(b_last[j] - b[j]), ((0,), (0,))) for j in r]
    o = [o[j] * lax.rsqrt(jnp.mean(o[j] * o[j], axis=-1, keepdims=True) + EPS) for j in r]
    o = [o[j] * gain[j] * (g[j] * _sigmoid(g[j])) for j in r]
    return o, st_new


def _group(n, pref):
    while n % pref:
        pref //= 2
    return pref


def _hgrn_fwd(proj, logits, gain, n_heads, name):
    T = proj.shape[0]
    nc = T // CHUNK
    H = n_heads
    HB = _group(H, 8)
    W = HB * HEAD

    def col(k):
        return pl.BlockSpec((CHUNK, W), lambda h, c: (c, k * (H // HB) + h))

    def body(q_ref, f_ref, i_ref, g_ref, l_ref, ga_ref, y_ref, s_ref, st):
        @pl.when(pl.program_id(1) == 0)
        def _():
            st[...] = jnp.zeros((HB, HEAD, HEAD), f32)

        cols = [slice(j * HEAD, (j + 1) * HEAD) for j in range(HB)]
        heads = lambda ref: [ref[:, cs] for cs in cols]
        s_ref[...] = st[...]
        o, st_new = _hgrn_heads(heads(q_ref), heads(f_ref), heads(i_ref), heads(g_ref), heads(l_ref), heads(ga_ref), [st[j] for j in range(HB)])
        for j, cs in enumerate(cols):
            y_ref[:, cs] = o[j].astype(bf16)
            st[j] = st_new[j]

    return pl.pallas_call(
        body, name=name, grid=(H // HB, nc),
        in_specs=[col(0), col(1), col(2), col(3), pl.BlockSpec((2, W), lambda h, c: (0, h)), pl.BlockSpec((1, W), lambda h, c: (0, h))],
        out_specs=(pl.BlockSpec((CHUNK, W), lambda h, c: (c, h)), pl.BlockSpec((HB, None, HEAD, HEAD), lambda h, c: (h, c, 0, 0))),
        out_shape=(jax.ShapeDtypeStruct((T, H * HEAD), bf16), jax.ShapeDtypeStruct((H, nc, HEAD, HEAD), f32)),
        scratch_shapes=[pltpu.VMEM((HB, HEAD, HEAD), f32)],
        compiler_params=_params(("parallel", "arbitrary")),
    )(proj, proj, proj, proj, logits, gain)


def _hgrn_bwd(proj, logits, gain, states, dy, n_heads, name):
    T = proj.shape[0]
    nc = T // CHUNK
    H = n_heads
    HB = _group(H, HGRN_BWD_HEADS)
    W = HB * HEAD

    def col(k):
        return pl.BlockSpec((CHUNK, W), lambda h, c: (nc - 1 - c, k * (H // HB) + h))

    out_blk = pl.BlockSpec((CHUNK, W), lambda h, c: (nc - 1 - c, h))

    def body(q_ref, f_ref, i_ref, g_ref, l_ref, ga_ref, s_ref, dy_ref, dq_ref, df_ref, di_ref, dg_ref, dl_ref, dga_ref, dst):
        first = pl.program_id(1) == 0

        @pl.when(first)
        def _():
            dst[...] = jnp.zeros((HB, HEAD, HEAD), f32)
            dl_ref[...] = jnp.zeros((2, W), f32)
            dga_ref[...] = jnp.zeros((1, W), f32)

        cols = [slice(j * HEAD, (j + 1) * HEAD) for j in range(HB)]
        heads = lambda ref: [ref[:, cs] for cs in cols]
        _, vjp = jax.vjp(_hgrn_heads, heads(q_ref), heads(f_ref), heads(i_ref), heads(g_ref), heads(l_ref), heads(ga_ref),
                         [s_ref[j] for j in range(HB)])
        dq, df, di, dg, dl, dga, ds = vjp((heads(dy_ref), [dst[j] for j in range(HB)]))
        for j, cs in enumerate(cols):
            dq_ref[:, cs] = dq[j].astype(bf16)
            df_ref[:, cs] = df[j].astype(bf16)
            di_ref[:, cs] = di[j].astype(bf16)
            dg_ref[:, cs] = dg[j].astype(bf16)
            dst[j] = ds[j]
            dl_ref[:, cs] += dl[j]
            dga_ref[:, cs] += dga[j]

    act = jax.ShapeDtypeStruct((T, H * HEAD), bf16)
    return pl.pallas_call(
        body, name=name, grid=(H // HB, nc),
        in_specs=[col(0), col(1), col(2), col(3), pl.BlockSpec((2, W), lambda h, c: (0, h)), pl.BlockSpec((1, W), lambda h, c: (0, h)),
                  pl.BlockSpec((HB, None, HEAD, HEAD), lambda h, c: (h, nc - 1 - c, 0, 0)), out_blk],
        out_specs=(out_blk, out_blk, out_blk, out_blk, pl.BlockSpec((2, W), lambda h, c: (0, h)), pl.BlockSpec((1, W), lambda h, c: (0, h))),
        out_shape=(act, act, act, act, jax.ShapeDtypeStruct((2, H * HEAD), f32), jax.ShapeDtypeStruct((1, H * HEAD), f32)),
        scratch_shapes=[pltpu.VMEM((HB, HEAD, HEAD), f32)],
        compiler_params=_params(("parallel", "arbitrary")),
    )(proj, proj, proj, proj, logits, gain, states, dy)


def _rel_index():
    t = np.arange(CHUNK)[:, None]
    sp = np.arange(BAND * CHUNK)[None, :]
    dist = (N_PAST - sp // CHUNK) * CHUNK + t - sp % CHUNK
    return (np.clip(dist, -REL_FUTURE, REL_PAST) + REL_FUTURE).reshape(1, -1).astype(np.int32)


def _bias_table(rel_bias_pad, idx, name):
    H = rel_bias_pad.shape[0]
    n = idx.shape[1]
    tc = _tile(n, 4096)

    def body(rb_ref, idx_ref, o_ref):
        onehot = lax.broadcasted_iota(jnp.int32, (N_REL_PAD, tc), 0) == idx_ref[...]
        o_ref[...] = _dot3(rb_ref[...], onehot, ((1,), (0,)), True)

    return pl.pallas_call(
        body, name=name, grid=(n // tc,),
        in_specs=[pl.BlockSpec((H, N_REL_PAD), lambda j: (0, 0)), pl.BlockSpec((1, tc), lambda j: (0, j))],
        out_specs=pl.BlockSpec((H, tc), lambda j: (0, j)), out_shape=jax.ShapeDtypeStruct((H, n), f32),
        compiler_params=_params(("parallel",)),
    )(rel_bias_pad, idx)


def _bias_table_bwd(dbias, idx, name):
    H, n = dbias.shape
    tc = _tile(n, 4096)

    def body(d_ref, idx_ref, o_ref):
        onehot = lax.broadcasted_iota(jnp.int32, (N_REL_PAD, tc), 0) == idx_ref[...]
        part = _dot3(d_ref[...], onehot, ((1,), (1,)), True)

        @pl.when(pl.program_id(0) == 0)
        def _():
            o_ref[...] = part

        @pl.when(pl.program_id(0) > 0)
        def _():
            o_ref[...] += part

    return pl.pallas_call(
        body, name=name, grid=(n // tc,),
        in_specs=[pl.BlockSpec((H, tc), lambda j: (0, j)), pl.BlockSpec((1, tc), lambda j: (0, j))],
        out_specs=pl.BlockSpec((H, N_REL_PAD), lambda j: (0, 0)), out_shape=jax.ShapeDtypeStruct((H, N_REL_PAD), f32),
        compiler_params=_params(("arbitrary",)),
    )(dbias, idx)


def _head_norm(t, gain):
    return t * lax.rsqrt(jnp.mean(t * t, axis=-1, keepdims=True) + EPS) * gain


def _attn_chunks(qs, kbs, vbs, qg, bias, ns):
    r = range(len(qs))
    qh = [_head_norm(qs[j], qg) for j in r]
    s = [_dotb(qh[j], kbs[j], ((1,), (1,))) * (HEAD ** -0.5) + bias for j in r]
    col = lax.broadcasted_iota(jnp.int32, (1, BAND * CHUNK), 1)
    s = [jnp.where(ns[j] * CHUNK - PAD + col >= 0, s[j], NEG) for j in r]
    e = [jnp.exp(s[j] - jnp.max(s[j], axis=-1, keepdims=True)) for j in r]
    p = [e[j] / jnp.sum(e[j], axis=-1, keepdims=True) for j in r]
    return [_dotb(p[j], vbs[j], ((1,), (0,))) for j in r]


def _attn_fwd(proj, q_gain, k_gain, bias, off, n_heads, name):
    T = proj.shape[0]
    nc = T // CHUNK
    H = n_heads
    CB = _group(nc, ATTN_CHUNKS)
    o0 = off // HEAD
    full = lambda k: pl.BlockSpec((T, HEAD), lambda h, c: (0, o0 + k * H + h))
    vec = pl.BlockSpec((1, HEAD), lambda h, c: (0, 0))

    def body(q_ref, k_ref, v_ref, qg_ref, kg_ref, b_ref, y_ref, kp, vp):
        c = pl.program_id(1)

        @pl.when(c == 0)
        def _():
            kp[pl.ds(0, PAD), :] = jnp.zeros((PAD, HEAD), f32)
            vp[pl.ds(0, PAD), :] = jnp.zeros((PAD, HEAD), f32)
            kp[pl.ds(PAD, T), :] = _head_norm(k_ref[...], kg_ref[...])
            vp[pl.ds(PAD, T), :] = v_ref[...]

        ns = [c * CB + j for j in range(CB)]
        rows = [pl.ds(j * CHUNK, CHUNK) for j in range(CB)]
        bands = [pl.ds(pl.multiple_of(n * CHUNK, CHUNK), BAND * CHUNK) for n in ns]
        outs = _attn_chunks([q_ref[r, :] for r in rows], [kp[b, :] for b in bands], [vp[b, :] for b in bands], qg_ref[...], b_ref[...], ns)
        for r, o in zip(rows, outs):
            y_ref[r, :] = o.astype(bf16)

    return pl.pallas_call(
        body, name=name, grid=(H, nc // CB),
        in_specs=[pl.BlockSpec((CB * CHUNK, HEAD), lambda h, c: (c, o0 + h)), full(1), full(2), vec, vec,
                  pl.BlockSpec((None, CHUNK, BAND * CHUNK), lambda h, c: (h, 0, 0))],
        out_specs=pl.BlockSpec((CB * CHUNK, HEAD), lambda h, c: (c, h)), out_shape=jax.ShapeDtypeStruct((T, H * HEAD), bf16),
        scratch_shapes=[pltpu.VMEM((T + PAD, HEAD), f32), pltpu.VMEM((T + PAD, HEAD), f32)],
        compiler_params=_params(("parallel", "arbitrary")),
    )(proj, proj, proj, q_gain, k_gain, bias)


def _attn_bwd(proj, q_gain, k_gain, bias, dy, off, n_heads, name):
    T = proj.shape[0]
    nc = T // CHUNK
    H = n_heads
    CB = _group(nc, ATTN_CHUNKS)
    o0 = off // HEAD
    full = lambda k: pl.BlockSpec((T, HEAD), lambda h, c: (0, o0 + k * H + h))
    full_out = pl.BlockSpec((T, HEAD), lambda h, c: (0, h))
    vec = pl.BlockSpec((1, HEAD), lambda h, c: (0, 0))
    chunk_out = pl.BlockSpec((CB * CHUNK, HEAD), lambda h, c: (c, h))
    bias_blk = pl.BlockSpec((None, CHUNK, BAND * CHUNK), lambda h, c: (h, 0, 0))

    def body(q_ref, k_ref, v_ref, qg_ref, kg_ref, b_ref, dy_ref, dq_ref, dk_ref, dv_ref, db_ref, dqg_ref, dkg_ref, kp, vp, dkp, dvp):
        h = pl.program_id(0)
        c = pl.program_id(1)

        @pl.when(c == 0)
        def _():
            kp[pl.ds(0, PAD), :] = jnp.zeros((PAD, HEAD), f32)
            vp[pl.ds(0, PAD), :] = jnp.zeros((PAD, HEAD), f32)
            kp[pl.ds(PAD, T), :] = _head_norm(k_ref[...], kg_ref[...])
            vp[pl.ds(PAD, T), :] = v_ref[...]
            dkp[...] = jnp.zeros((T + PAD, HEAD), f32)
            dvp[...] = jnp.zeros((T + PAD, HEAD), f32)
            db_ref[...] = jnp.zeros((CHUNK, BAND * CHUNK), f32)

        @pl.when(jnp.logical_and(h == 0, c == 0))
        def _():
            dqg_ref[...] = jnp.zeros((1, HEAD), f32)
            dkg_ref[...] = jnp.zeros((1, HEAD), f32)

        ns = [c * CB + j for j in range(CB)]
        rows = [pl.ds(j * CHUNK, CHUNK) for j in range(CB)]
        bands = [pl.ds(pl.multiple_of(n * CHUNK, CHUNK), BAND * CHUNK) for n in ns]
        _, vjp = jax.vjp(functools.partial(_attn_chunks, ns=ns), [q_ref[r, :] for r in rows], [kp[b, :] for b in bands],
                         [vp[b, :] for b in bands], qg_ref[...], b_ref[...])
        dqs, dkbs, dvbs, dqg, db = vjp([dy_ref[r, :] for r in rows])
        db_ref[...] += db
        dqg_ref[...] += dqg
        for r, b, dq, dkb, dvb in zip(rows, bands, dqs, dkbs, dvbs):
            dq_ref[r, :] = dq.astype(bf16)
            dkp[b, :] += dkb
            dvp[b, :] += dvb

        @pl.when(c == nc // CB - 1)
        def _():
            _, nvjp = jax.vjp(_head_norm, k_ref[...], kg_ref[...])
            dk, dkg = nvjp(dkp[pl.ds(PAD, T), :])
            dk_ref[...] = dk.astype(bf16)
            dv_ref[...] = dvp[pl.ds(PAD, T), :].astype(bf16)
            dkg_ref[...] += dkg

    act = jax.ShapeDtypeStruct((T, H * HEAD), bf16)
    gvec = jax.ShapeDtypeStruct((1, HEAD), f32)
    pad_buf = pltpu.VMEM((T + PAD, HEAD), f32)
    return pl.pallas_call(
        body, name=name, grid=(H, nc // CB),
        in_specs=[pl.BlockSpec((CB * CHUNK, HEAD), lambda h, c: (c, o0 + h)), full(1), full(2), vec, vec, bias_blk, chunk_out],
        out_specs=(chunk_out, full_out, full_out, bias_blk, vec, vec),
        out_shape=(act, act, act, jax.ShapeDtypeStruct((H, CHUNK, BAND * CHUNK), f32), gvec, gvec),
        scratch_shapes=[pad_buf, pad_buf, pad_buf, pad_buf],
        compiler_params=_params(("arbitrary", "arbitrary")),
    )(proj, proj, proj, q_gain, k_gain, bias, dy)


def _position():
    x, y, c = lax.axis_index("x"), lax.axis_index("y"), lax.axis_index("c")
    return x, y, c, 4 * x + 2 * y + c


def _flip(v, bit):
    return 1 - v if bit else v


def _chips(x, y):
    return [(1 - x, y), (x, 1 - y), (1 - x, 1 - y)]


def _seq_gather(shards, name, collective_id):
    n = len(shards)

    def body(*refs):
        ins, outs = refs[:n], refs[n:2 * n]
        send, recv, loc = refs[2 * n:]
        x, y, c, me = _position()
        sib = (x, y, 1 - c)
        sel = lambda a, b: c * a + (1 - c) * b
        n1 = (sel(1 - x, x), sel(y, 1 - y))
        n2 = (sel(x, 1 - x), sel(1 - y, y))
        far = (1 - x, 1 - y)
        idx = lambda chip, core: 4 * chip[0] + 2 * chip[1] + core
        barrier = pltpu.get_barrier_semaphore()
        for peer in [sib, (*n1, c), (*n2, c)]:
            pl.semaphore_signal(barrier, inc=1, device_id=peer, device_id_type=MESH)
        pl.semaphore_wait(barrier, 3)

        def copy(w, k, src, blk, to):
            return pltpu.make_async_remote_copy(src_ref=src, dst_ref=outs[w].at[blk], send_sem=send.at[7 * w + k], recv_sem=recv.at[7 * w + k],
                                                device_id=to, device_id_type=MESH)

        mine = [pltpu.make_async_copy(ins[w], outs[w].at[me], loc.at[w]) for w in range(n)]
        for cp in mine:
            cp.start()
        sent = [copy(w, 1, ins[w], me, (*n1, c)) for w in range(n)] + [copy(w, 2, ins[w], me, (*n2, c)) for w in range(n)]
        sent += [copy(w, 0, ins[w], me, sib) for w in range(n)]
        for cp in sent:
            cp.start()
        for k, chip in ((1, n1), (2, n2), (3, far)):
            blk = idx(chip, c)
            for w in range(n):
                copy(w, k, ins[w], blk, sib).wait_recv()
                if k == 1:
                    sent.append(copy(w, 3, outs[w].at[blk], blk, (*n2, c)))
                    sent[-1].start()
                sent.append(copy(w, 3 + k, outs[w].at[blk], blk, sib))
                sent[-1].start()
        for w in range(n):
            copy(w, 0, ins[w], idx((x, y), 1 - c), sib).wait_recv()
        for k, chip in ((4, n2), (5, n1), (6, far)):
            for w in range(n):
                copy(w, k, ins[w], idx(chip, 1 - c), sib).wait_recv()
        for cp in sent:
            cp.wait_send()
        for cp in mine:
            cp.wait()

    return pl.kernel(
        body, out_type=tuple(jax.ShapeDtypeStruct((NDEV,) + s.shape, s.dtype) for s in shards),
        mesh=plsc.ScalarSubcoreMesh(axis_name="sequencer", num_cores=1), name=name,
        scratch_types=(pltpu.SemaphoreType.DMA((7 * n,)), pltpu.SemaphoreType.DMA((7 * n,)), pltpu.SemaphoreType.DMA((n,))),
        compiler_params=pltpu.CompilerParams(collective_id=collective_id),
    )(*shards)


NCHIP = 4


def _seq_pair_exchange(grads, name, collective_id, after=()):
    n, na = len(grads), len(after)

    def body(*refs):
        ins, outs = refs[:n], refs[n + na:2 * n + na]
        send, recv = refs[2 * n + na:]
        x, y, c, me = _position()
        sib = (x, y, 1 - c)
        barrier = pltpu.get_barrier_semaphore()
        pl.semaphore_signal(barrier, inc=1, device_id=sib, device_id_type=MESH)
        pl.semaphore_wait(barrier, 1)
        copies = [pltpu.make_async_remote_copy(src_ref=ins[w].at[2 * k + (1 - c)], dst_ref=outs[w].at[k], send_sem=send.at[NCHIP * w + k],
                                               recv_sem=recv.at[NCHIP * w + k], device_id=sib, device_id_type=MESH)
                  for w in range(n) for k in range(NCHIP)]
        for cp in copies:
            cp.start()
        for cp in copies:
            cp.wait_recv()
        for cp in copies:
            cp.wait_send()

    return pl.kernel(
        body, out_type=tuple(jax.ShapeDtypeStruct((NCHIP,) + g.shape[1:], g.dtype) for g in grads),
        mesh=plsc.ScalarSubcoreMesh(axis_name="sequencer", num_cores=1), name=name,
        scratch_types=(pltpu.SemaphoreType.DMA((NCHIP * n,)), pltpu.SemaphoreType.DMA((NCHIP * n,))),
        compiler_params=pltpu.CompilerParams(collective_id=collective_id),
    )(*grads, *after)


def _pair_add(grad, sib_part, name, after=()):
    _, R, C = grad.shape
    tr = _tile(R, 1024, 16)
    core = jnp.reshape(lax.axis_index("c"), (1,)).astype(jnp.int32)

    def body(c_ref, g_ref, s_ref, *rest):
        rest[-1][...] = (g_ref[...].astype(f32) + s_ref[...].astype(f32)).astype(bf16)

    blk = pl.BlockSpec((None, tr, C), lambda k, i, c_ref: (k, i, 0))
    return pl.pallas_call(
        body, name=name,
        grid_spec=pltpu.PrefetchScalarGridSpec(
            num_scalar_prefetch=1, grid=(NCHIP, R // tr),
            in_specs=[pl.BlockSpec((None, tr, C), lambda k, i, c_ref: (2 * k + c_ref[0], i, 0)), blk]
            + [pl.BlockSpec(memory_space=pl.ANY)] * len(after), out_specs=blk),
        out_shape=jax.ShapeDtypeStruct((NCHIP, R, C), bf16), compiler_params=_params(("parallel", "parallel")),
    )(core, grad, sib_part, *after)


def _seq_chip_exchange(sums, name, collective_id, after=()):
    n, na = len(sums), len(after)

    def body(*refs):
        ins, outs = refs[:n], refs[n + na:2 * n + na]
        send, recv, loc = refs[2 * n + na:]
        x, y, c, me = _position()
        chips = _chips(x, y)
        mine = 2 * x + y
        barrier = pltpu.get_barrier_semaphore()
        for px, py in chips:
            pl.semaphore_signal(barrier, inc=1, device_id=(px, py, c), device_id_type=MESH)
        pl.semaphore_wait(barrier, 3)
        local = [pltpu.make_async_copy(ins[w].at[mine], outs[w].at[mine], loc.at[w]) for w in range(n)]
        for cp in local:
            cp.start()
        sends, waits = [], []
        for j, (px, py) in enumerate(chips):
            for w in range(n):
                sems = dict(send_sem=send.at[3 * w + j], recv_sem=recv.at[3 * w + j], device_id=(px, py, c), device_id_type=MESH)
                sends.append(pltpu.make_async_remote_copy(src_ref=ins[w].at[2 * px + py], dst_ref=outs[w].at[mine], **sems))
                waits.append(pltpu.make_async_remote_copy(src_ref=ins[w].at[2 * px + py], dst_ref=outs[w].at[2 * px + py], **sems))
        for cp in sends:
            cp.start()
        for cp in waits:
            cp.wait_recv()
        for cp in sends:
            cp.wait_send()
        for cp in local:
            cp.wait()

    return pl.kernel(
        body, out_type=tuple(jax.ShapeDtypeStruct(s.shape, s.dtype) for s in sums),
        mesh=plsc.ScalarSubcoreMesh(axis_name="sequencer", num_cores=1), name=name,
        scratch_types=(pltpu.SemaphoreType.DMA((3 * n,)), pltpu.SemaphoreType.DMA((3 * n,)), pltpu.SemaphoreType.DMA((n,))),
        compiler_params=pltpu.CompilerParams(collective_id=collective_id),
    )(*sums, *after)


def _reduce_scatter(grads, tag, ids, after=(), add_after=()):
    sib_parts = _seq_pair_exchange(grads, "pair_exchange_" + tag, ids[0], after=after)
    sums = [_pair_add(g, s, "pair_add_%s%d" % (tag, i), after=add_after) for i, (g, s) in enumerate(zip(grads, sib_parts))]
    return _seq_chip_exchange(sums, "chip_exchange_" + tag, ids[1]), sums


def _small_all_reduce(v, name):
    R, C = v.shape

    def body(v_ref, o_ref, buf, send, recv):
        x, y, c, me = _position()
        buf[me] = v_ref[...]
        sends, waits = [], []
        for r in range(1, NDEV):
            px, py, pc = _flip(x, r & 4), _flip(y, r & 2), _flip(c, r & 1)
            peer = 4 * px + 2 * py + pc
            sends.append(pltpu.make_async_remote_copy(src_ref=v_ref, dst_ref=buf.at[me], send_sem=send.at[r - 1], recv_sem=recv.at[r - 1],
                                                      device_id=(px, py, pc), device_id_type=MESH))
            waits.append(pltpu.make_async_remote_copy(src_ref=v_ref, dst_ref=buf.at[peer], send_sem=send.at[r - 1], recv_sem=recv.at[r - 1],
                                                      device_id=(px, py, pc), device_id_type=MESH))
        for cp in sends:
            cp.start()
        for cp in waits:
            cp.wait_recv()
        for cp in sends:
            cp.wait_send()
        acc = buf[0]
        for i in range(1, NDEV):
            acc = acc + buf[i]
        o_ref[...] = acc

    vm = pl.BlockSpec(memory_space=pltpu.VMEM)
    return pl.pallas_call(
        body, name=name, in_specs=[vm], out_specs=vm, out_shape=jax.ShapeDtypeStruct((R, C), f32),
        scratch_shapes=[pltpu.VMEM((NDEV, R, C), f32), pltpu.SemaphoreType.DMA((7,)), pltpu.SemaphoreType.DMA((7,))],
    )(v)


def _adamw_math(w, g, m, v):
    m = ADAM_B1 * m + (1.0 - ADAM_B1) * g
    v = ADAM_B2 * v + (1.0 - ADAM_B2) * (g * g)
    m_hat = m / (1.0 - ADAM_B1 ** ADAM_STEP)
    v_hat = v / (1.0 - ADAM_B2 ** ADAM_STEP)
    delta = -ADAM_LR * (m_hat / (jnp.sqrt(v_hat) + ADAM_EPS) + ADAM_WD * w)
    return delta, m, v


def _adamw_parts(w, m, v, parts, name, after=()):
    R, C = w.shape
    tr = _tile(R, 128, 16)
    blk = pl.BlockSpec((tr, C), lambda i: (i, 0))

    def body(w_ref, m_ref, v_ref, p_ref, *rest):
        g_ref, d_ref, mo_ref, vo_ref = rest[len(after):]
        g = p_ref[0].astype(f32)
        for i in range(1, NCHIP):
            g = g + p_ref[i].astype(f32)
        d, mn, vn = _adamw_math(w_ref[...], g, m_ref[...], v_ref[...])
        g_ref[...] = g
        d_ref[...] = d
        mo_ref[...] = mn
        vo_ref[...] = vn

    shp = jax.ShapeDtypeStruct((R, C), f32)
    return pl.pallas_call(
        body, name=name, grid=(R // tr,),
        in_specs=[blk, blk, blk, pl.BlockSpec((NCHIP, tr, C), lambda i: (0, i, 0))] + [pl.BlockSpec(a.shape, lambda i: (0, 0)) for a in after],
        out_specs=(blk, blk, blk, blk), out_shape=(shp, shp, shp, shp), compiler_params=_params(("parallel",)),
    )(w, m, v, parts, *after)


def _adamw_small(w, g, m, v, name):
    def body(w_ref, g_ref, m_ref, v_ref, d_ref, mo_ref, vo_ref):
        d, mn, vn = _adamw_math(w_ref[...], g_ref[...], m_ref[...], v_ref[...])
        d_ref[...] = d
        mo_ref[...] = mn
        vo_ref[...] = vn

    shp = jax.ShapeDtypeStruct(w.shape, f32)
    return pl.pallas_call(body, name=name, out_shape=(shp, shp, shp))(w, g, m, v)


SMALL_COLS = 1024


def _pack(arrs):
    flat = jnp.concatenate([a.reshape(-1) for a in arrs])
    rows = -(-flat.shape[0] // (8 * SMALL_COLS)) * 8
    return jnp.pad(flat, (0, rows * SMALL_COLS - flat.shape[0])).reshape(rows, SMALL_COLS)


def _unpack(packed, like):
    flat = packed.reshape(-1)
    out, pos = [], 0
    for a in like:
        out.append(flat[pos:pos + a.size].reshape(a.shape))
        pos += a.size
    return out


def kernel(x, w_in, b_gate, norm_mix, norm_ffn, hgrn_lb_logits, hgrn_out_gain, q_gain, k_gain, rel_bias, w_proj_a, w_proj_b, w_out, w_ffn_in, w_ffn_out, loss_target, m_w_in, m_b_gate, m_norm_mix, m_norm_ffn, m_hgrn_lb_logits, m_hgrn_out_gain, m_q_gain, m_k_gain, m_rel_bias, m_w_proj_a, m_w_proj_b, m_w_out, m_w_ffn_in, m_w_ffn_out, v_w_in, v_b_gate, v_norm_mix, v_norm_ffn, v_hgrn_lb_logits, v_hgrn_out_gain, v_q_gain, v_k_gain, v_rel_bias, v_w_proj_a, v_w_proj_b, v_w_out, v_w_ffn_in, v_w_ffn_out):
    xs = x[0]
    target = loss_target[0]
    T, D = xs.shape
    d_a = hgrn_out_gain.shape[-1]
    H = d_a // HEAD
    d_b = d_a
    off_b = 4 * d_a
    off_g = off_b + 3 * d_b
    assert rel_bias.shape[1] == H and T % CHUNK == 0 and T // CHUNK > N_PAST

    big_w = [w_in[0], w_proj_a[0], w_proj_b[0], w_out[0], w_ffn_in[0], w_ffn_out[0]]
    big_m = [m_w_in[0], m_w_proj_a[0], m_w_proj_b[0], m_w_out[0], m_w_ffn_in[0], m_w_ffn_out[0]]
    big_v = [v_w_in[0], v_w_proj_a[0], v_w_proj_b[0], v_w_out[0], v_w_ffn_in[0], v_w_ffn_out[0]]

    sh = [w.astype(bf16) for w in big_w]
    (g_in,) = _seq_gather(sh[0:1], "gather_a", 1)
    g_pa, g_pb, g_out = _seq_gather(sh[1:4], "gather_b", 2)
    (g_fin,) = _seq_gather(sh[4:5], "gather_c", 3)
    (g_fout,) = _seq_gather(sh[5:6], "gather_d", 4)

    h = _rms_fwd(xs, norm_mix, "rms_mix")
    proj = _mm(h, g_in, mode="nn", b_blocked=True, name="mm_proj")
    y_a, states = _hgrn_fwd(proj, hgrn_lb_logits, hgrn_out_gain, H, "hgrn_fwd")
    idx = jnp.asarray(_rel_index())
    rb_pad = jnp.pad(rel_bias[0], ((0, 0), (0, N_REL_PAD - N_REL)))
    bias = _bias_table(rb_pad, idx, "bias_table").reshape(H, CHUNK, BAND * CHUNK)
    y_b = _attn_fwd(proj, q_gain, k_gain, bias, off_b, H, "attn_fwd")
    wg_out = g_out.reshape(-1, g_out.shape[-1])
    wg_fout = g_fout.reshape(-1, g_fout.shape[-1])
    pa, pb, merged = _proj_merge(y_a, y_b, g_pa, g_pb, proj, b_gate, off_g, "mm_proj_ab")
    x1, h2 = _out_rms(merged, wg_out, xs, norm_ffn, "mm_out")
    gu, act = _ffn_in_swiglu(h2, g_fin, "mm_ffn_in")
    dy, loss_acc = _ffn_out_loss(act, wg_fout, x1, target, "mm_ffn_out")
    loss_part = loss_acc[0:1, 0:1] * (0.5 / D)

    gw_fout = _mm(act, dy, mode="tn", out_dtype=bf16, tm=1408, name="mm_gw_ffn_out")
    dgu = _d_act_swiglu(dy, wg_fout, gu, "mm_d_act")
    gw_fin = _mm(h2, dgu, mode="tn", b_stacked=True, out_blocked=True, out_dtype=bf16, tn=g_fin.shape[-1], name="mm_gw_ffn_in")
    dh2 = _mm_nt_blocked(dgu, g_fin, "mm_d_h2", a_stacked=True)
    (p_fout, p_fin), sums_a = _reduce_scatter([gw_fout.reshape(NDEV, -1, D), gw_fin], "a", (5, 6), add_after=(dh2,))
    dx1, g_norm_ffn = _rms_bwd(x1, norm_ffn, dh2, dy, "rms_ffn_bwd", after=sums_a)

    gw_out = _mm(merged, dx1, mode="tn", out_dtype=bf16, name="mm_gw_out")
    dpa, dpb, dgl, g_b_gate = _d_merged_branches(dx1, wg_out, pa, pb, proj, b_gate, off_g, "mm_d_merged")
    dy_a = _mm(dpa, g_pa, mode="nt", b_blocked=True, tm=2048, name="mm_d_ya")
    dy_b = _mm(dpb, g_pb, mode="nt", b_blocked=True, tm=2048, name="mm_d_yb")
    gw_pa = _mm(y_a, dpa, mode="tn", out_blocked=True, out_dtype=bf16, tn=g_pa.shape[-1], name="mm_gw_proj_a")
    gw_pb = _mm(y_b, dpb, mode="tn", out_blocked=True, out_dtype=bf16, tn=g_pb.shape[-1], name="mm_gw_proj_b")

    dq_a, df_a, di_a, dg_a, g_logits, g_gain = _hgrn_bwd(proj, hgrn_lb_logits, hgrn_out_gain, states, dy_a, H, "hgrn_bwd")
    dq_b, dk_b, dv_b, dbias, g_qg, g_kg = _attn_bwd(proj, q_gain, k_gain, bias, dy_b, off_b, H, "attn_bwd")
    g_rel_pad = _bias_table_bwd(dbias.reshape(H, -1), idx, "bias_table_bwd")
    g_rel = g_rel_pad[:, :N_REL]
    dproj = jnp.concatenate([dq_a, df_a, di_a, dg_a, dq_b, dk_b, dv_b, dgl[0], dgl[1]], axis=1)
    (p_out, p_pa, p_pb), sums_b = _reduce_scatter([gw_out.reshape(NDEV, -1, D), gw_pa, gw_pb], "b", (7, 8), after=(g_gain, p_fout, p_fin), add_after=(dy_b,))
    gw_in = _mm(h, dproj, mode="tn", out_blocked=True, out_dtype=bf16, tn=g_in.shape[-1], name="mm_gw_in", after=sums_b)
    (p_in,), sums_c = _reduce_scatter([gw_in], "c", (9, 10), after=(p_out, p_pa, p_pb), add_after=(g_rel_pad,))
    dh = _mm_nt_blocked(dproj, g_in, "mm_d_h", after=sums_c)
    grad_x, g_norm_mix = _rms_bwd(xs, norm_mix, dh, dx1, "rms_mix_bwd")

    parts = [p_in, p_pa, p_pb, p_out, p_fin, p_fout]
    names = ["w_in", "w_proj_a", "w_proj_b", "w_out", "w_ffn_in", "w_ffn_out"]
    big = {}
    for nm, w, m, v, p in zip(names, big_w, big_m, big_v, parts):
        big[nm] = [o[None] for o in _adamw_parts(w, m, v, p, "adamw_" + nm, after=() if nm == "w_in" else (g_norm_mix,))]

    small_names = ["b_gate", "norm_mix", "norm_ffn", "hgrn_lb_logits", "hgrn_out_gain", "q_gain", "k_gain", "rel_bias"]
    small_w = [b_gate, norm_mix, norm_ffn, hgrn_lb_logits, hgrn_out_gain, q_gain, k_gain, rel_bias]
    small_m = [m_b_gate, m_norm_mix, m_norm_ffn, m_hgrn_lb_logits, m_hgrn_out_gain, m_q_gain, m_k_gain, m_rel_bias]
    small_v = [v_b_gate, v_norm_mix, v_norm_ffn, v_hgrn_lb_logits, v_hgrn_out_gain, v_q_gain, v_k_gain, v_rel_bias]
    small_g = [g_b_gate.reshape(1, -1), g_norm_mix, g_norm_ffn, g_logits, g_gain, g_qg, g_kg, g_rel[None], loss_part]
    g_sum = _small_all_reduce(_pack(small_g), "reduce_small")
    loss = _unpack(g_sum, small_g)[-1].reshape(())
    d_s, m_s, v_s = _adamw_small(_pack(small_w), g_sum, _pack(small_m), _pack(small_v), "adamw_small")
    small = {}
    for nm, g, d, m, v in zip(small_names, _unpack(g_sum, small_w), _unpack(d_s, small_w), _unpack(m_s, small_w), _unpack(v_s, small_w)):
        small[nm] = [g, d, m, v]

    order = ["w_in", "b_gate", "norm_mix", "norm_ffn", "hgrn_lb_logits", "hgrn_out_gain", "q_gain", "k_gain", "rel_bias",
             "w_proj_a", "w_proj_b", "w_out", "w_ffn_in", "w_ffn_out"]
    res = {**big, **small}
    outs = [loss, grad_x[None]]
    for k in range(4):
        outs += [res[nm][k] for nm in order]
    return tuple(outs)
```

```python
import functools

import numpy as np
import jax
import jax.numpy as jnp
from jax import lax
from jax.experimental import pallas as pl
from jax.experimental.pallas import tpu as pltpu
from jax.experimental.pallas import tpu_sc as plsc

f32 = jnp.float32
bf16 = jnp.bfloat16
MESH = pl.DeviceIdType.MESH
AXES = ("x", "y", "c")
NDEV = 8

CHUNK = 64
HEAD = 128
SUB = 8
HGRN_BWD_HEADS = 8
ATTN_CHUNKS = 8
N_PAST = 8
BAND = N_PAST + 1
PAD = N_PAST * CHUNK
REL_FUTURE = CHUNK - 1
REL_PAST = 2 * CHUNK - 1
N_REL = REL_FUTURE + REL_PAST + 1
N_REL_PAD = 256
EPS = 1e-6
NEG = -1e30

ADAM_LR = 0.001
ADAM_B1 = 0.9
ADAM_B2 = 0.999
ADAM_EPS = 1e-08
ADAM_WD = 0.01
ADAM_STEP = 10

VMEM_LIMIT = 56 * 1024 * 1024


def _params(sem=None):
    return pltpu.CompilerParams(dimension_semantics=sem, vmem_limit_bytes=VMEM_LIMIT)


def _tile(n, pref, unit=128):
    if n <= pref:
        return n
    t = (pref // unit) * unit
    while t >= unit:
        if n % t == 0:
            return t
        t -= unit
    return n


_sigmoid = jax.nn.sigmoid


def _mm(a, b, *, mode, name, b_blocked=False, out_blocked=False, out_dtype=f32, tm=1024, tn=1024, tk=2048, after=(),
        b_stacked=False):
    if mode == "tn":
        K, M = a.shape
    else:
        M, K = a.shape
    if b_blocked:
        nb, mid, cb = b.shape
        if mode == "nn":
            assert mid == K
            N, tn = nb * cb, cb
        else:
            assert mode == "nt" and nb * cb == K
            N, tk = mid, cb
    elif b_stacked:
        assert mode == "tn"
        N = 2 * b.shape[2]
    else:
        N = b.shape[1] if mode in ("nn", "tn") else b.shape[0]
    tm = _tile(M, tm)
    tn = tn if (b_blocked and mode == "nn") or out_blocked else _tile(N, tn)
    tk = tk if b_blocked and mode == "nt" else _tile(K, tk)
    assert M % tm == 0 and N % tn == 0 and K % tk == 0
    nk = K // tk
    grid = (M // tm, N // tn, nk)
    if mode == "tn":
        a_spec = pl.BlockSpec((tk, tm), lambda i, j, k: (k, i))
    else:
        a_spec = pl.BlockSpec((tm, tk), lambda i, j, k: (i, k))
    if mode == "nn":
        b_spec = pl.BlockSpec((None, tk, cb), lambda i, j, k: (j, k, 0)) if b_blocked else pl.BlockSpec((tk, tn), lambda i, j, k: (k, j))
    elif mode == "nt":
        b_spec = pl.BlockSpec((None, tn, cb), lambda i, j, k: (k, j, 0)) if b_blocked else pl.BlockSpec((tn, tk), lambda i, j, k: (j, k))
    elif b_stacked:
        nh = N // 2 // tn
        b_spec = pl.BlockSpec((None, tk, tn), lambda i, j, k: (j // nh, k, j % nh))
    else:
        b_spec = pl.BlockSpec((tk, tn), lambda i, j, k: (k, j))
    if out_blocked:
        out_shape = jax.ShapeDtypeStruct((N // tn, M, tn), out_dtype)
        o_spec = pl.BlockSpec((None, tm, tn), lambda i, j, k: (j, i, 0))
    else:
        out_shape = jax.ShapeDtypeStruct((M, N), out_dtype)
        o_spec = pl.BlockSpec((tm, tn), lambda i, j, k: (i, j))
    dims = {"nn": ((1,), (0,)), "nt": ((1,), (1,)), "tn": ((0,), (0,))}[mode]

    def body(a_ref, b_ref, *rest):
        o_ref, acc = rest[len(after)], rest[len(after) + 1:]
        p = lax.dot_general(a_ref[...].astype(bf16), b_ref[...].astype(bf16), (dims, ((), ())), preferred_element_type=f32)
        if nk == 1:
            o_ref[...] = p.astype(out_dtype)
        else:
            acc_ref = acc[0]
            k = pl.program_id(2)

            @pl.when(k == 0)
            def _():
                acc_ref[...] = p

            @pl.when(k > 0)
            def _():
                acc_ref[...] += p

            @pl.when(k == nk - 1)
            def _():
                o_ref[...] = acc_ref[...].astype(out_dtype)

    return pl.pallas_call(
        body, name=name, grid=grid, in_specs=[a_spec, b_spec] + [pl.BlockSpec(memory_space=pl.ANY)] * len(after), out_specs=o_spec,
        out_shape=out_shape, scratch_shapes=[pltpu.VMEM((tm, tn), f32)] if nk > 1 else [],
        compiler_params=_params(("parallel", "parallel", "arbitrary")),
    )(a, b, *after)


def _mm_nt_blocked(a, b, name, *, a_stacked=False, after=(), tm=256, tn=1024):
    nb, N, cb = b.shape
    M = a.shape[1] if a_stacked else a.shape[0]
    tm, tn = _tile(M, tm, 8), _tile(N, tn)
    half = nb // 2

    def body(a_ref, b_ref, *rest):
        dn = (((1,), (1,)), ((), ()))
        p = None
        for k in range(nb):
            a_k = a_ref[k // half, :, (k % half) * cb:(k % half + 1) * cb] if a_stacked else a_ref[:, k * cb:(k + 1) * cb]
            d = lax.dot_general(a_k, b_ref[k], dn, preferred_element_type=f32)
            p = d if p is None else p + d
        rest[len(after)][...] = p

    a_spec = pl.BlockSpec((2, tm, half * cb), lambda j, i: (0, i, 0)) if a_stacked else pl.BlockSpec((tm, nb * cb), lambda j, i: (i, 0))
    return pl.pallas_call(
        body, name=name, grid=(N // tn, M // tm),
        in_specs=[a_spec, pl.BlockSpec((nb, tn, cb), lambda j, i: (0, j, 0), pipeline_mode=pl.Buffered(1))]
        + [pl.BlockSpec(memory_space=pl.ANY)] * len(after),
        out_specs=pl.BlockSpec((tm, tn), lambda j, i: (i, j)), out_shape=jax.ShapeDtypeStruct((M, N), f32),
        compiler_params=_params(("parallel", "parallel")),
    )(a, b, *after)


def _rms_fwd(x, gain, name):
    T, D = x.shape
    tr = _tile(T, 256, 8)
    row = pl.BlockSpec((tr, D), lambda i: (i, 0))

    def body(x_ref, g_ref, h_ref):
        xs = x_ref[...]
        r = lax.rsqrt(jnp.mean(xs * xs, axis=-1, keepdims=True) + EPS)
        h_ref[...] = (xs * r * g_ref[...]).astype(bf16)

    return pl.pallas_call(body, name=name, grid=(T // tr,), in_specs=[row, pl.BlockSpec((1, D), lambda i: (0, 0))], out_specs=row,
                          out_shape=jax.ShapeDtypeStruct((T, D), bf16), compiler_params=_params(("parallel",)))(x, gain)


def _rms_bwd(xs, gain, dh, extra, name, after=()):
    T, D = xs.shape
    tr = _tile(T, 256, 8)
    row = pl.BlockSpec((tr, D), lambda i: (i, 0))
    vec = pl.BlockSpec((1, D), lambda i: (0, 0))

    def body(x_ref, g_ref, dh_ref, e_ref, *rest):
        dx_ref, dg_ref = rest[len(after):]
        x = x_ref[...]
        r = lax.rsqrt(jnp.mean(x * x, axis=-1, keepdims=True) + EPS)
        xhat = x * r
        dh_v = dh_ref[...]
        gd = dh_v * g_ref[...]
        dx_ref[...] = e_ref[...] + r * (gd - xhat * jnp.mean(gd * xhat, axis=-1, keepdims=True))
        part = jnp.sum(dh_v * xhat, axis=0, keepdims=True)

        @pl.when(pl.program_id(0) == 0)
        def _():
            dg_ref[...] = part

        @pl.when(pl.program_id(0) > 0)
        def _():
            dg_ref[...] += part

    return pl.pallas_call(body, name=name, grid=(T // tr,),
                          in_specs=[row, vec, row, row] + [pl.BlockSpec(memory_space=pl.ANY)] * len(after), out_specs=(row, vec),
                          out_shape=(jax.ShapeDtypeStruct((T, D), f32), jax.ShapeDtypeStruct((1, D), f32)),
                          compiler_params=_params(("arbitrary",)))(xs, gain, dh, extra, *after)


def _proj_merge(y_a, y_b, w_a, w_b, proj, b_gate, off, name):
    T, K = y_a.shape
    nb, _, cb = w_a.shape
    D = nb * cb
    tm = _tile(T, 1024, 8)
    oa, ob = off // cb, (off + D) // cb
    blk = pl.BlockSpec((tm, cb), lambda j, i: (i, j))
    row = pl.BlockSpec((tm, K), lambda j, i: (i, 0))
    wsp = pl.BlockSpec((None, K, cb), lambda j, i: (j, 0, 0))

    def body(ya_ref, yb_ref, wa_ref, wb_ref, ga_ref, gb_ref, ba_ref, bb_ref, pa_ref, pb_ref, m_ref):
        dn = (((1,), (0,)), ((), ()))
        pa = lax.dot_general(ya_ref[...], wa_ref[...], dn, preferred_element_type=f32)
        pb = lax.dot_general(yb_ref[...], wb_ref[...], dn, preferred_element_type=f32)
        pa_ref[...] = pa
        pb_ref[...] = pb
        m_ref[...] = (_sigmoid(ga_ref[...] + ba_ref[...]) * pa + _sigmoid(gb_ref[...] + bb_ref[...]) * pb).astype(bf16)

    return pl.pallas_call(
        body, name=name, grid=(nb, T // tm),
        in_specs=[row, row, wsp, wsp, pl.BlockSpec((tm, cb), lambda j, i: (i, oa + j)), pl.BlockSpec((tm, cb), lambda j, i: (i, ob + j)),
                  pl.BlockSpec((1, cb), lambda j, i: (0, j)), pl.BlockSpec((1, cb), lambda j, i: (0, nb + j))],
        out_specs=(blk, blk, blk),
        out_shape=(jax.ShapeDtypeStruct((T, D), f32), jax.ShapeDtypeStruct((T, D), f32), jax.ShapeDtypeStruct((T, D), bf16)),
        compiler_params=_params(("parallel", "parallel")),
    )(y_a, y_b, w_a, w_b, proj, proj, b_gate, b_gate)


def _out_rms(merged, w_out, x, gain, name):
    T, K = merged.shape
    D = w_out.shape[1]
    tm = _tile(T, 256, 8)
    row = pl.BlockSpec((tm, D), lambda i: (i, 0))

    def body(m_ref, w_ref, x_ref, g_ref, x1_ref, h_ref):
        x1 = x_ref[...] + lax.dot_general(m_ref[...], w_ref[...], (((1,), (0,)), ((), ())), preferred_element_type=f32)
        x1_ref[...] = x1
        r = lax.rsqrt(jnp.mean(x1 * x1, axis=-1, keepdims=True) + EPS)
        h_ref[...] = (x1 * r * g_ref[...]).astype(bf16)

    return pl.pallas_call(
        body, name=name, grid=(T // tm,),
        in_specs=[pl.BlockSpec((tm, K), lambda i: (i, 0)), pl.BlockSpec((K, D), lambda i: (0, 0)), row, pl.BlockSpec((1, D), lambda i: (0, 0))],
        out_specs=(row, row), out_shape=(jax.ShapeDtypeStruct((T, D), f32), jax.ShapeDtypeStruct((T, D), bf16)),
        compiler_params=_params(("parallel",)),
    )(merged, w_out, x, gain)


def _d_merged_branches(dx, w_out, pa, pb, proj, b_gate, off, name):
    T, D = pa.shape
    tm, tn = _tile(T, 512, 8), _tile(D, 1024)
    oa, ob, nb = off // tn, (off + D) // tn, D // tn
    blk = pl.BlockSpec((tm, tn), lambda j, i: (i, j))

    def body(dx_ref, w_ref, pa_ref, pb_ref, ga_ref, gb_ref, ba_ref, bb_ref, dpa_ref, dpb_ref, dgl_ref, db_ref):
        dm = lax.dot_general(dx_ref[...].astype(bf16), w_ref[...], (((1,), (1,)), ((), ())), preferred_element_type=f32)
        sums = []
        for p_ref, gl_ref, b_ref, dp_ref, k in ((pa_ref, ga_ref, ba_ref, dpa_ref, 0), (pb_ref, gb_ref, bb_ref, dpb_ref, 1)):
            g = _sigmoid(gl_ref[...] + b_ref[...])
            dp_ref[...] = (dm * g).astype(bf16)
            dgl = dm * p_ref[...] * g * (1.0 - g)
            dgl_ref[k] = dgl.astype(bf16)
            sums.append(jnp.sum(dgl, axis=0, keepdims=True))

        @pl.when(pl.program_id(1) == 0)
        def _():
            db_ref[...] = jnp.zeros((2, 1, tn), f32)

        db_ref[0] += sums[0]
        db_ref[1] += sums[1]

    return pl.pallas_call(
        body, name=name, grid=(nb, T // tm),
        in_specs=[pl.BlockSpec((tm, D), lambda j, i: (i, 0)), pl.BlockSpec((tn, D), lambda j, i: (j, 0)), blk, blk,
                  pl.BlockSpec((tm, tn), lambda j, i: (i, oa + j)), pl.BlockSpec((tm, tn), lambda j, i: (i, ob + j)),
                  pl.BlockSpec((1, tn), lambda j, i: (0, j)), pl.BlockSpec((1, tn), lambda j, i: (0, nb + j))],
        out_specs=(blk, blk, pl.BlockSpec((2, tm, tn), lambda j, i: (0, i, j)), pl.BlockSpec((2, 1, tn), lambda j, i: (0, 0, j))),
        out_shape=(jax.ShapeDtypeStruct((T, D), bf16), jax.ShapeDtypeStruct((T, D), bf16), jax.ShapeDtypeStruct((2, T, D), bf16),
                   jax.ShapeDtypeStruct((2, 1, D), f32)),
        compiler_params=_params(("parallel", "arbitrary")),
    )(dx, w_out, pa, pb, proj, proj, b_gate, b_gate)


def _ffn_in_swiglu(h, w, name):
    T, K = h.shape
    nb, _, cb = w.shape
    half = nb // 2
    F = half * cb
    tm = _tile(T, 512, 8)

    def body(h_ref, wg_ref, wu_ref, gu_ref, act_ref):
        dn = (((1,), (0,)), ((), ()))
        hv = h_ref[...]
        g = lax.dot_general(hv, wg_ref[...], dn, preferred_element_type=f32)
        u = lax.dot_general(hv, wu_ref[...], dn, preferred_element_type=f32)
        gu_ref[0] = g
        gu_ref[1] = u
        act_ref[...] = (g * _sigmoid(g) * u).astype(bf16)

    return pl.pallas_call(
        body, name=name, grid=(half, T // tm),
        in_specs=[pl.BlockSpec((tm, K), lambda j, i: (i, 0)), pl.BlockSpec((None, K, cb), lambda j, i: (j, 0, 0)),
                  pl.BlockSpec((None, K, cb), lambda j, i: (j + half, 0, 0))],
        out_specs=(pl.BlockSpec((2, tm, cb), lambda j, i: (0, i, j)), pl.BlockSpec((tm, cb), lambda j, i: (i, j))),
        out_shape=(jax.ShapeDtypeStruct((2, T, F), f32), jax.ShapeDtypeStruct((T, F), bf16)),
        compiler_params=_params(("parallel", "parallel")),
    )(h, w, w)


def _d_act_swiglu(dy, w_out, gu, name):
    T, D = dy.shape
    F = w_out.shape[0]
    tm, tn = _tile(T, 512, 8), _tile(F, 1408)

    def body(dy_ref, w_ref, gu_ref, o_ref):
        halves = [pl.ds(r * (tm // 2), tm // 2) for r in range(2)]
        wv = w_ref[...]
        d = [lax.dot_general(dy_ref[r, :].astype(bf16), wv, (((1,), (1,)), ((), ())), preferred_element_type=f32) for r in halves]
        for r, dr in zip(halves, d):
            g = gu_ref[0, r, :]
            s = _sigmoid(g)
            o_ref[0, r, :] = (dr * gu_ref[1, r, :] * s * (1.0 + g * (1.0 - s))).astype(bf16)
            o_ref[1, r, :] = (dr * g * s).astype(bf16)

    blk = pl.BlockSpec((2, tm, tn), lambda j, i: (0, i, j))
    return pl.pallas_call(
        body, name=name, grid=(F // tn, T // tm),
        in_specs=[pl.BlockSpec((tm, D), lambda j, i: (i, 0)), pl.BlockSpec((tn, D), lambda j, i: (j, 0)), blk],
        out_specs=blk, out_shape=jax.ShapeDtypeStruct((2, T, F), bf16), compiler_params=_params(("parallel", "parallel")),
    )(dy, w_out, gu)


def _ffn_out_loss(act, w, x1, target, name):
    T, F = act.shape
    D = w.shape[1]
    tm, tn, tk = _tile(T, 512, 8), _tile(D, 1024), _tile(F, 2816)
    nk = F // tk
    blk = pl.BlockSpec((tm, tn), lambda i, j, k: (i, j))

    def body(a_ref, w_ref, x_ref, t_ref, dy_ref, l_ref, acc_ref):
        i, j, k = pl.program_id(0), pl.program_id(1), pl.program_id(2)
        p = lax.dot_general(a_ref[...], w_ref[...], (((1,), (0,)), ((), ())), preferred_element_type=f32)

        @pl.when(jnp.logical_and(jnp.logical_and(i == 0, j == 0), k == 0))
        def _():
            l_ref[...] = jnp.zeros((8, 128), f32)

        @pl.when(k == 0)
        def _():
            acc_ref[...] = p

        @pl.when(k > 0)
        def _():
            acc_ref[...] += p

        @pl.when(k == nk - 1)
        def _():
            d = acc_ref[...] + x_ref[...] - t_ref[...]
            dy_ref[...] = d * (1.0 / D)
            l_ref[...] += jnp.sum(jnp.sum(d * d, axis=1, keepdims=True), axis=0, keepdims=True)

    return pl.pallas_call(
        body, name=name, grid=(T // tm, D // tn, nk),
        in_specs=[pl.BlockSpec((tm, tk), lambda i, j, k: (i, k)), pl.BlockSpec((tk, tn), lambda i, j, k: (k, j)), blk, blk],
        out_specs=(blk, pl.BlockSpec((8, 128), lambda i, j, k: (0, 0))),
        out_shape=(jax.ShapeDtypeStruct((T, D), f32), jax.ShapeDtypeStruct((8, 128), f32)),
        scratch_shapes=[pltpu.VMEM((tm, tn), f32)],
        compiler_params=_params(("arbitrary", "arbitrary", "arbitrary")),
    )(act, w, x1, target)


_DIMS = {"nn": ((1,), (0,)), "nt": ((1,), (1,)), "tn": ((0,), (0,))}
_MODE = {v: k for k, v in _DIMS.items()}


def _dot_bf16(a, b, mode):
    return lax.dot_general(a.astype(bf16), b.astype(bf16), (_DIMS[mode], ((), ())), preferred_element_type=f32)


@functools.partial(jax.custom_vjp, nondiff_argnums=(2,))
def _dotm(a, b, mode):
    return _dot_bf16(a, b, mode)


def _dotm_fwd(a, b, mode):
    return _dot_bf16(a, b, mode), (a, b)


def _dotm_bwd(mode, res, g):
    a, b = res
    if mode == "nn":
        return _dot_bf16(g, b, "nt"), _dot_bf16(a, g, "tn")
    if mode == "nt":
        return _dot_bf16(g, b, "nn"), _dot_bf16(g, a, "tn")
    return _dot_bf16(b, g, "nt"), _dot_bf16(a, g, "nn")


_dotm.defvjp(_dotm_fwd, _dotm_bwd)


def _dotb(a, b, dims):
    return _dotm(a, b, _MODE[dims])


def _split3(v):
    def top(t):
        return lax.bitcast_convert_type(lax.bitcast_convert_type(t, jnp.uint32) & jnp.uint32(0xFFFF0000), f32)

    hi = top(v)
    mid = top(v - hi)
    low = (v - hi) - mid
    return hi.astype(bf16), mid.astype(bf16), low.astype(bf16)


def _dot3(v, m, dims, v_first):
    m = m.astype(bf16)
    dn = (dims, ((), ()))
    parts = [lax.dot_general(p, m, dn, preferred_element_type=f32) if v_first else lax.dot_general(m, p, dn, preferred_element_type=f32)
             for p in _split3(v)]
    return parts[0] + parts[1] + parts[2]


def _triangle_sum(v, lower):
    row = lax.broadcasted_iota(jnp.int32, (CHUNK, CHUNK), 0)
    col = lax.broadcasted_iota(jnp.int32, (CHUNK, CHUNK), 1)
    return _dot3(v, (col <= row) if lower else (col >= row), ((1,), (0,)), False)


@jax.custom_vjp
def _cumsum_rows(v):
    return _triangle_sum(v, True)


_cumsum_rows.defvjp(lambda v: (_triangle_sum(v, True), None), lambda _, g: (_triangle_sum(g, False),))


def _hgrn_heads(q, fl, iv, g, logits, gain, st):
    r = range(len(q))
    lb = [jax.nn.softmax(logits[j], axis=0)[0:1] for j in r]
    f = [lb[j] + (1.0 - lb[j]) * _sigmoid(fl[j]) for j in r]
    lf = [jnp.log(f[j]) for j in r]
    kk = [1.0 - f[j] for j in r]
    qs = [q[j] * _sigmoid(q[j]) for j in r]
    b = [_cumsum_rows(lf[j]) for j in r]
    b_last = [jnp.sum(lf[j], axis=0, keepdims=True) for j in r]
    o = [_dotb(qs[j] * jnp.exp(b[j]), st[j], ((1,), (1,))) for j in r]
    r3 = lax.broadcasted_iota(jnp.int32, (SUB, SUB, HEAD), 0)
    c3 = lax.broadcasted_iota(jnp.int32, (SUB, SUB, HEAD), 1)
    parts = [[] for _ in r]
    for i in range(CHUNK // SUB):
        lo, hi = i * SUB, (i + 1) * SUB
        bi = [b[j][lo:hi] for j in r]
        dec = [jnp.exp(jnp.where(c3 <= r3, bi[j][:, None, :] - bi[j][None, :, :], -jnp.inf)) for j in r]
        s = [jnp.sum(qs[j][lo:hi][:, None, :] * kk[j][lo:hi][None, :, :] * dec[j], axis=-1) for j in r]
        if i > 0:
            anchor = [jnp.max(bi[j], axis=0, keepdims=True) for j in r]
            qa = [qs[j][lo:hi] * jnp.exp(bi[j] - anchor[j]) for j in r]
            kd = [kk[j][:lo] * jnp.exp(anchor[j] - b[j][:lo]) for j in r]
            s = [jnp.concatenate([_dotb(qa[j], kd[j], ((1,), (1,))), s[j]], axis=1) for j in r]
        for j in r:
            parts[j].append(_dotb(s[j], iv[j][:hi], ((1,), (0,))))
    o = [o[j] + jnp.concatenate(parts[j], axis=0) for j in r]
    st_new = [st[j] * jnp.exp(b_last[j]) + _dotb(iv[j], kk[j] * jnp.exp(b_last[j] - b[j]), ((0,), (0,))) for j in r]
    o = [o[j] * lax.rsqrt(jnp.mean(o[j] * o[j], axis=-1, keepdims=True) + EPS) for j in r]
    o = [o[j] * gain[j] * (g[j] * _sigmoid(g[j])) for j in r]
    return o, st_new


def _group(n, pref):
    while n % pref:
        pref //= 2
    return pref


def _hgrn_fwd(proj, logits, gain, n_heads, name):
    T = proj.shape[0]
    nc = T // CHUNK
    H = n_heads
    HB = _group(H, 8)
    W = HB * HEAD

    def col(k):
        return pl.BlockSpec((CHUNK, W), lambda h, c: (c, k * (H // HB) + h))

    def body(q_ref, f_ref, i_ref, g_ref, l_ref, ga_ref, y_ref, s_ref, st):
        @pl.when(pl.program_id(1) == 0)
        def _():
            st[...] = jnp.zeros((HB, HEAD, HEAD), f32)

        cols = [slice(j * HEAD, (j + 1) * HEAD) for j in range(HB)]
        heads = lambda ref: [ref[:, cs] for cs in cols]
        s_ref[...] = st[...]
        o, st_new = _hgrn_heads(heads(q_ref), heads(f_ref), heads(i_ref), heads(g_ref), heads(l_ref), heads(ga_ref), [st[j] for j in range(HB)])
        for j, cs in enumerate(cols):
            y_ref[:, cs] = o[j].astype(bf16)
            st[j] = st_new[j]

    return pl.pallas_call(
        body, name=name, grid=(H // HB, nc),
        in_specs=[col(0), col(1), col(2), col(3), pl.BlockSpec((2, W), lambda h, c: (0, h)), pl.BlockSpec((1, W), lambda h, c: (0, h))],
        out_specs=(pl.BlockSpec((CHUNK, W), lambda h, c: (c, h)), pl.BlockSpec((HB, None, HEAD, HEAD), lambda h, c: (h, c, 0, 0))),
        out_shape=(jax.ShapeDtypeStruct((T, H * HEAD), bf16), jax.ShapeDtypeStruct((H, nc, HEAD, HEAD), f32)),
        scratch_shapes=[pltpu.VMEM((HB, HEAD, HEAD), f32)],
        compiler_params=_params(("parallel", "arbitrary")),
    )(proj, proj, proj, proj, logits, gain)


def _hgrn_bwd(proj, logits, gain, states, dy, n_heads, name):
    T = proj.shape[0]
    nc = T // CHUNK
    H = n_heads
    HB = _group(H, HGRN_BWD_HEADS)
    W = HB * HEAD

    def col(k):
        return pl.BlockSpec((CHUNK, W), lambda h, c: (nc - 1 - c, k * (H // HB) + h))

    out_blk = pl.BlockSpec((CHUNK, W), lambda h, c: (nc - 1 - c, h))

    def body(q_ref, f_ref, i_ref, g_ref, l_ref, ga_ref, s_ref, dy_ref, dq_ref, df_ref, di_ref, dg_ref, dl_ref, dga_ref, dst):
        first = pl.program_id(1) == 0

        @pl.when(first)
        def _():
            dst[...] = jnp.zeros((HB, HEAD, HEAD), f32)
            dl_ref[...] = jnp.zeros((2, W), f32)
            dga_ref[...] = jnp.zeros((1, W), f32)

        cols = [slice(j * HEAD, (j + 1) * HEAD) for j in range(HB)]
        heads = lambda ref: [ref[:, cs] for cs in cols]
        _, vjp = jax.vjp(_hgrn_heads, heads(q_ref), heads(f_ref), heads(i_ref), heads(g_ref), heads(l_ref), heads(ga_ref),
                         [s_ref[j] for j in range(HB)])
        dq, df, di, dg, dl, dga, ds = vjp((heads(dy_ref), [dst[j] for j in range(HB)]))
        for j, cs in enumerate(cols):
            dq_ref[:, cs] = dq[j].astype(bf16)
            df_ref[:, cs] = df[j].astype(bf16)
            di_ref[:, cs] = di[j].astype(bf16)
            dg_ref[:, cs] = dg[j].astype(bf16)
            dst[j] = ds[j]
            dl_ref[:, cs] += dl[j]
            dga_ref[:, cs] += dga[j]

    act = jax.ShapeDtypeStruct((T, H * HEAD), bf16)
    return pl.pallas_call(
        body, name=name, grid=(H // HB, nc),
        in_specs=[col(0), col(1), col(2), col(3), pl.BlockSpec((2, W), lambda h, c: (0, h)), pl.BlockSpec((1, W), lambda h, c: (0, h)),
                  pl.BlockSpec((HB, None, HEAD, HEAD), lambda h, c: (h, nc - 1 - c, 0, 0)), out_blk],
        out_specs=(out_blk, out_blk, out_blk, out_blk, pl.BlockSpec((2, W), lambda h, c: (0, h)), pl.BlockSpec((1, W), lambda h, c: (0, h))),
        out_shape=(act, act, act, act, jax.ShapeDtypeStruct((2, H * HEAD), f32), jax.ShapeDtypeStruct((1, H * HEAD), f32)),
        scratch_shapes=[pltpu.VMEM((HB, HEAD, HEAD), f32)],
        compiler_params=_params(("parallel", "arbitrary")),
    )(proj, proj, proj, proj, logits, gain, states, dy)


def _rel_index():
    t = np.arange(CHUNK)[:, None]
    sp = np.arange(BAND * CHUNK)[None, :]
    dist = (N_PAST - sp // CHUNK) * CHUNK + t - sp % CHUNK
    return (np.clip(dist, -REL_FUTURE, REL_PAST) + REL_FUTURE).reshape(1, -1).astype(np.int32)


def _bias_table(rel_bias_pad, idx, name):
    H = rel_bias_pad.shape[0]
    n = idx.shape[1]
    tc = _tile(n, 4096)

    def body(rb_ref, idx_ref, o_ref):
        onehot = lax.broadcasted_iota(jnp.int32, (N_REL_PAD, tc), 0) == idx_ref[...]
        o_ref[...] = _dot3(rb_ref[...], onehot, ((1,), (0,)), True)

    return pl.pallas_call(
        body, name=name, grid=(n // tc,),
        in_specs=[pl.BlockSpec((H, N_REL_PAD), lambda j: (0, 0)), pl.BlockSpec((1, tc), lambda j: (0, j))],
        out_specs=pl.BlockSpec((H, tc), lambda j: (0, j)), out_shape=jax.ShapeDtypeStruct((H, n), f32),
        compiler_params=_params(("parallel",)),
    )(rel_bias_pad, idx)


def _bias_table_bwd(dbias, idx, name):
    H, n = dbias.shape
    tc = _tile(n, 4096)

    def body(d_ref, idx_ref, o_ref):
        onehot = lax.broadcasted_iota(jnp.int32, (N_REL_PAD, tc), 0) == idx_ref[...]
        part = _dot3(d_ref[...], onehot, ((1,), (1,)), True)

        @pl.when(pl.program_id(0) == 0)
        def _():
            o_ref[...] = part

        @pl.when(pl.program_id(0) > 0)
        def _():
            o_ref[...] += part

    return pl.pallas_call(
        body, name=name, grid=(n // tc,),
        in_specs=[pl.BlockSpec((H, tc), lambda j: (0, j)), pl.BlockSpec((1, tc), lambda j: (0, j))],
        out_specs=pl.BlockSpec((H, N_REL_PAD), lambda j: (0, 0)), out_shape=jax.ShapeDtypeStruct((H, N_REL_PAD), f32),
        compiler_params=_params(("arbitrary",)),
    )(dbias, idx)


def _head_norm(t, gain):
    return t * lax.rsqrt(jnp.mean(t * t, axis=-1, keepdims=True) + EPS) * gain


def _attn_chunks(qs, kbs, vbs, qg, bias, ns):
    r = range(len(qs))
    qh = [_head_norm(qs[j], qg) for j in r]
    s = [_dotb(qh[j], kbs[j], ((1,), (1,))) * (HEAD ** -0.5) + bias for j in r]
    col = lax.broadcasted_iota(jnp.int32, (1, BAND * CHUNK), 1)
    s = [jnp.where(ns[j] * CHUNK - PAD + col >= 0, s[j], NEG) for j in r]
    e = [jnp.exp(s[j] - jnp.max(s[j], axis=-1, keepdims=True)) for j in r]
    p = [e[j] / jnp.sum(e[j], axis=-1, keepdims=True) for j in r]
    return [_dotb(p[j], vbs[j], ((1,), (0,))) for j in r]


def _attn_fwd(proj, q_gain, k_gain, bias, off, n_heads, name):
    T = proj.shape[0]
    nc = T // CHUNK
    H = n_heads
    CB = _group(nc, ATTN_CHUNKS)
    o0 = off // HEAD
    full = lambda k: pl.BlockSpec((T, HEAD), lambda h, c: (0, o0 + k * H + h))
    vec = pl.BlockSpec((1, HEAD), lambda h, c: (0, 0))

    def body(q_ref, k_ref, v_ref, qg_ref, kg_ref, b_ref, y_ref, kp, vp):
        c = pl.program_id(1)

        @pl.when(c == 0)
        def _():
            kp[pl.ds(0, PAD), :] = jnp.zeros((PAD, HEAD), f32)
            vp[pl.ds(0, PAD), :] = jnp.zeros((PAD, HEAD), f32)
            kp[pl.ds(PAD, T), :] = _head_norm(k_ref[...], kg_ref[...])
            vp[pl.ds(PAD, T), :] = v_ref[...]

        ns = [c * CB + j for j in range(CB)]
        rows = [pl.ds(j * CHUNK, CHUNK) for j in range(CB)]
        bands = [pl.ds(pl.multiple_of(n * CHUNK, CHUNK), BAND * CHUNK) for n in ns]
        outs = _attn_chunks([q_ref[r, :] for r in rows], [kp[b, :] for b in bands], [vp[b, :] for b in bands], qg_ref[...], b_ref[...], ns)
        for r, o in zip(rows, outs):
            y_ref[r, :] = o.astype(bf16)

    return pl.pallas_call(
        body, name=name, grid=(H, nc // CB),
        in_specs=[pl.BlockSpec((CB * CHUNK, HEAD), lambda h, c: (c, o0 + h)), full(1), full(2), vec, vec,
                  pl.BlockSpec((None, CHUNK, BAND * CHUNK), lambda h, c: (h, 0, 0))],
        out_specs=pl.BlockSpec((CB * CHUNK, HEAD), lambda h, c: (c, h)), out_shape=jax.ShapeDtypeStruct((T, H * HEAD), bf16),
        scratch_shapes=[pltpu.VMEM((T + PAD, HEAD), f32), pltpu.VMEM((T + PAD, HEAD), f32)],
        compiler_params=_params(("parallel", "arbitrary")),
    )(proj, proj, proj, q_gain, k_gain, bias)


def _attn_bwd(proj, q_gain, k_gain, bias, dy, off, n_heads, name):
    T = proj.shape[0]
    nc = T // CHUNK
    H = n_heads
    CB = _group(nc, ATTN_CHUNKS)
    o0 = off // HEAD
    full = lambda k: pl.BlockSpec((T, HEAD), lambda h, c: (0, o0 + k * H + h))
    full_out = pl.BlockSpec((T, HEAD), lambda h, c: (0, h))
    vec = pl.BlockSpec((1, HEAD), lambda h, c: (0, 0))
    chunk_out = pl.BlockSpec((CB * CHUNK, HEAD), lambda h, c: (c, h))
    bias_blk = pl.BlockSpec((None, CHUNK, BAND * CHUNK), lambda h, c: (h, 0, 0))

    def body(q_ref, k_ref, v_ref, qg_ref, kg_ref, b_ref, dy_ref, dq_ref, dk_ref, dv_ref, db_ref, dqg_ref, dkg_ref, kp, vp, dkp, dvp):
        h = pl.program_id(0)
        c = pl.program_id(1)

        @pl.when(c == 0)
        def _():
            kp[pl.ds(0, PAD), :] = jnp.zeros((PAD, HEAD), f32)
            vp[pl.ds(0, PAD), :] = jnp.zeros((PAD, HEAD), f32)
            kp[pl.ds(PAD, T), :] = _head_norm(k_ref[...], kg_ref[...])
            vp[pl.ds(PAD, T), :] = v_ref[...]
            dkp[...] = jnp.zeros((T + PAD, HEAD), f32)
            dvp[...] = jnp.zeros((T + PAD, HEAD), f32)
            db_ref[...] = jnp.zeros((CHUNK, BAND * CHUNK), f32)

        @pl.when(jnp.logical_and(h == 0, c == 0))
        def _():
            dqg_ref[...] = jnp.zeros((1, HEAD), f32)
            dkg_ref[...] = jnp.zeros((1, HEAD), f32)

        ns = [c * CB + j for j in range(CB)]
        rows = [pl.ds(j * CHUNK, CHUNK) for j in range(CB)]
        bands = [pl.ds(pl.multiple_of(n * CHUNK, CHUNK), BAND * CHUNK) for n in ns]
        _, vjp = jax.vjp(functools.partial(_attn_chunks, ns=ns), [q_ref[r, :] for r in rows], [kp[b, :] for b in bands],
                         [vp[b, :] for b in bands], qg_ref[...], b_ref[...])
        dqs, dkbs, dvbs, dqg, db = vjp([dy_ref[r, :] for r in rows])
        db_ref[...] += db
        dqg_ref[...] += dqg
        for r, b, dq, dkb, dvb in zip(rows, bands, dqs, dkbs, dvbs):
            dq_ref[r, :] = dq.astype(bf16)
            dkp[b, :] += dkb
            dvp[b, :] += dvb

        @pl.when(c == nc // CB - 1)
        def _():
            _, nvjp = jax.vjp(_head_norm, k_ref[...], kg_ref[...])
            dk, dkg = nvjp(dkp[pl.ds(PAD, T), :])
            dk_ref[...] = dk.astype(bf16)
            dv_ref[...] = dvp[pl.ds(PAD, T), :].astype(bf16)
            dkg_ref[...] += dkg

    act = jax.ShapeDtypeStruct((T, H * HEAD), bf16)
    gvec = jax.ShapeDtypeStruct((1, HEAD), f32)
    pad_buf = pltpu.VMEM((T + PAD, HEAD), f32)
    return pl.pallas_call(
        body, name=name, grid=(H, nc // CB),
        in_specs=[pl.BlockSpec((CB * CHUNK, HEAD), lambda h, c: (c, o0 + h)), full(1), full(2), vec, vec, bias_blk, chunk_out],
        out_specs=(chunk_out, full_out, full_out, bias_blk, vec, vec),
        out_shape=(act, act, act, jax.ShapeDtypeStruct((H, CHUNK, BAND * CHUNK), f32), gvec, gvec),
        scratch_shapes=[pad_buf, pad_buf, pad_buf, pad_buf],
        compiler_params=_params(("arbitrary", "arbitrary")),
    )(proj, proj, proj, q_gain, k_gain, bias, dy)


def _position():
    x, y, c = lax.axis_index("x"), lax.axis_index("y"), lax.axis_index("c")
    return x, y, c, 4 * x + 2 * y + c


def _flip(v, bit):
    return 1 - v if bit else v


def _chips(x, y):
    return [(1 - x, y), (x, 1 - y), (1 - x, 1 - y)]


def _seq_gather(shards, name, collective_id):
    n = len(shards)

    def body(*refs):
        ins, outs = refs[:n], refs[n:2 * n]
        send, recv, loc = refs[2 * n:]
        x, y, c, me = _position()
        sib = (x, y, 1 - c)
        sel = lambda a, b: c * a + (1 - c) * b
        n1 = (sel(1 - x, x), sel(y, 1 - y))
        n2 = (sel(x, 1 - x), sel(1 - y, y))
        far = (1 - x, 1 - y)
        idx = lambda chip, core: 4 * chip[0] + 2 * chip[1] + core
        barrier = pltpu.get_barrier_semaphore()
        for peer in [sib, (*n1, c), (*n2, c)]:
            pl.semaphore_signal(barrier, inc=1, device_id=peer, device_id_type=MESH)
        pl.semaphore_wait(barrier, 3)

        def copy(w, k, src, blk, to):
            return pltpu.make_async_remote_copy(src_ref=src, dst_ref=outs[w].at[blk], send_sem=send.at[7 * w + k], recv_sem=recv.at[7 * w + k],
                                                device_id=to, device_id_type=MESH)

        mine = [pltpu.make_async_copy(ins[w], outs[w].at[me], loc.at[w]) for w in range(n)]
        for cp in mine:
            cp.start()
        sent = [copy(w, 1, ins[w], me, (*n1, c)) for w in range(n)] + [copy(w, 2, ins[w], me, (*n2, c)) for w in range(n)]
        sent += [copy(w, 0, ins[w], me, sib) for w in range(n)]
        for cp in sent:
            cp.start()
        for k, chip in ((1, n1), (2, n2), (3, far)):
            blk = idx(chip, c)
            for w in range(n):
                copy(w, k, ins[w], blk, sib).wait_recv()
                if k == 1:
                    sent.append(copy(w, 3, outs[w].at[blk], blk, (*n2, c)))
                    sent[-1].start()
                sent.append(copy(w, 3 + k, outs[w].at[blk], blk, sib))
                sent[-1].start()
        for w in range(n):
            copy(w, 0, ins[w], idx((x, y), 1 - c), sib).wait_recv()
        for k, chip in ((4, n2), (5, n1), (6, far)):
            for w in range(n):
                copy(w, k, ins[w], idx(chip, 1 - c), sib).wait_recv()
        for cp in sent:
            cp.wait_send()
        for cp in mine:
            cp.wait()

    return pl.kernel(
        body, out_type=tuple(jax.ShapeDtypeStruct((NDEV,) + s.shape, s.dtype) for s in shards),
        mesh=plsc.ScalarSubcoreMesh(axis_name="sequencer", num_cores=1), name=name,
        scratch_types=(pltpu.SemaphoreType.DMA((7 * n,)), pltpu.SemaphoreType.DMA((7 * n,)), pltpu.SemaphoreType.DMA((n,))),
        compiler_params=pltpu.CompilerParams(collective_id=collective_id),
    )(*shards)


NCHIP = 4


def _seq_pair_exchange(grads, name, collective_id, after=()):
    n, na = len(grads), len(after)

    def body(*refs):
        ins, outs = refs[:n], refs[n + na:2 * n + na]
        send, recv = refs[2 * n + na:]
        x, y, c, me = _position()
        sib = (x, y, 1 - c)
        barrier = pltpu.get_barrier_semaphore()
        pl.semaphore_signal(barrier, inc=1, device_id=sib, device_id_type=MESH)
        pl.semaphore_wait(barrier, 1)
        copies = [pltpu.make_async_remote_copy(src_ref=ins[w].at[2 * k + (1 - c)], dst_ref=outs[w].at[k], send_sem=send.at[NCHIP * w + k],
                                               recv_sem=recv.at[NCHIP * w + k], device_id=sib, device_id_type=MESH)
                  for w in range(n) for k in range(NCHIP)]
        for cp in copies:
            cp.start()
        for cp in copies:
            cp.wait_recv()
        for cp in copies:
            cp.wait_send()

    return pl.kernel(
        body, out_type=tuple(jax.ShapeDtypeStruct((NCHIP,) + g.shape[1:], g.dtype) for g in grads),
        mesh=plsc.ScalarSubcoreMesh(axis_name="sequencer", num_cores=1), name=name,
        scratch_types=(pltpu.SemaphoreType.DMA((NCHIP * n,)), pltpu.SemaphoreType.DMA((NCHIP * n,))),
        compiler_params=pltpu.CompilerParams(collective_id=collective_id),
    )(*grads, *after)


def _pair_add(grad, sib_part, name, after=()):
    _, R, C = grad.shape
    tr = _tile(R, 1024, 16)
    core = jnp.reshape(lax.axis_index("c"), (1,)).astype(jnp.int32)

    def body(c_ref, g_ref, s_ref, *rest):
        rest[-1][...] = (g_ref[...].astype(f32) + s_ref[...].astype(f32)).astype(bf16)

    blk = pl.BlockSpec((None, tr, C), lambda k, i, c_ref: (k, i, 0))
    return pl.pallas_call(
        body, name=name,
        grid_spec=pltpu.PrefetchScalarGridSpec(
            num_scalar_prefetch=1, grid=(NCHIP, R // tr),
            in_specs=[pl.BlockSpec((None, tr, C), lambda k, i, c_ref: (2 * k + c_ref[0], i, 0)), blk]
            + [pl.BlockSpec(memory_space=pl.ANY)] * len(after), out_specs=blk),
        out_shape=jax.ShapeDtypeStruct((NCHIP, R, C), bf16), compiler_params=_params(("parallel", "parallel")),
    )(core, grad, sib_part, *after)


def _seq_chip_exchange(sums, name, collective_id, after=()):
    n, na = len(sums), len(after)

    def body(*refs):
        ins, outs = refs[:n], refs[n + na:2 * n + na]
        send, recv, loc = refs[2 * n + na:]
        x, y, c, me = _position()
        chips = _chips(x, y)
        mine = 2 * x + y
        barrier = pltpu.get_barrier_semaphore()
        for px, py in chips:
            pl.semaphore_signal(barrier, inc=1, device_id=(px, py, c), device_id_type=MESH)
        pl.semaphore_wait(barrier, 3)
        local = [pltpu.make_async_copy(ins[w].at[mine], outs[w].at[mine], loc.at[w]) for w in range(n)]
        for cp in local:
            cp.start()
        sends, waits = [], []
        for j, (px, py) in enumerate(chips):
            for w in range(n):
                sems = dict(send_sem=send.at[3 * w + j], recv_sem=recv.at[3 * w + j], device_id=(px, py, c), device_id_type=MESH)
                sends.append(pltpu.make_async_remote_copy(src_ref=ins[w].at[2 * px + py], dst_ref=outs[w].at[mine], **sems))
                waits.append(pltpu.make_async_remote_copy(src_ref=ins[w].at[2 * px + py], dst_ref=outs[w].at[2 * px + py], **sems))
        for cp in sends:
            cp.start()
        for cp in waits:
            cp.wait_recv()
        for cp in sends:
            cp.wait_send()
        for cp in local:
            cp.wait()

    return pl.kernel(
        body, out_type=tuple(jax.ShapeDtypeStruct(s.shape, s.dtype) for s in sums),
        mesh=plsc.ScalarSubcoreMesh(axis_name="sequencer", num_cores=1), name=name,
        scratch_types=(pltpu.SemaphoreType.DMA((3 * n,)), pltpu.SemaphoreType.DMA((3 * n,)), pltpu.SemaphoreType.DMA((n,))),
        compiler_params=pltpu.CompilerParams(collective_id=collective_id),
    )(*sums, *after)


def _reduce_scatter(grads, tag, ids, after=(), add_after=()):
    sib_parts = _seq_pair_exchange(grads, "pair_exchange_" + tag, ids[0], after=after)
    sums = [_pair_add(g, s, "pair_add_%s%d" % (tag, i), after=add_after) for i, (g, s) in enumerate(zip(grads, sib_parts))]
    return _seq_chip_exchange(sums, "chip_exchange_" + tag, ids[1]), sums


def _small_all_reduce(v, name):
    R, C = v.shape

    def body(v_ref, o_ref, buf, send, recv):
        x, y, c, me = _position()
        buf[me] = v_ref[...]
        sends, waits = [], []
        for r in range(1, NDEV):
            px, py, pc = _flip(x, r & 4), _flip(y, r & 2), _flip(c, r & 1)
            peer = 4 * px + 2 * py + pc
            sends.append(pltpu.make_async_remote_copy(src_ref=v_ref, dst_ref=buf.at[me], send_sem=send.at[r - 1], recv_sem=recv.at[r - 1],
                                                      device_id=(px, py, pc), device_id_type=MESH))
            waits.append(pltpu.make_async_remote_copy(src_ref=v_ref, dst_ref=buf.at[peer], send_sem=send.at[r - 1], recv_sem=recv.at[r - 1],
                                                      device_id=(px, py, pc), device_id_type=MESH))
        for cp in sends:
            cp.start()
        for cp in waits:
            cp.wait_recv()
        for cp in sends:
            cp.wait_send()
        acc = buf[0]
        for i in range(1, NDEV):
            acc = acc + buf[i]
        o_ref[...] = acc

    vm = pl.BlockSpec(memory_space=pltpu.VMEM)
    return pl.pallas_call(
        body, name=name, in_specs=[vm], out_specs=vm, out_shape=jax.ShapeDtypeStruct((R, C), f32),
        scratch_shapes=[pltpu.VMEM((NDEV, R, C), f32), pltpu.SemaphoreType.DMA((7,)), pltpu.SemaphoreType.DMA((7,))],
    )(v)


def _adamw_math(w, g, m, v):
    m = ADAM_B1 * m + (1.0 - ADAM_B1) * g
    v = ADAM_B2 * v + (1.0 - ADAM_B2) * (g * g)
    m_hat = m / (1.0 - ADAM_B1 ** ADAM_STEP)
    v_hat = v / (1.0 - ADAM_B2 ** ADAM_STEP)
    delta = -ADAM_LR * (m_hat / (jnp.sqrt(v_hat) + ADAM_EPS) + ADAM_WD * w)
    return delta, m, v


def _adamw_parts(w, m, v, parts, name, after=()):
    R, C = w.shape
    tr = _tile(R, 128, 16)
    blk = pl.BlockSpec((tr, C), lambda i: (i, 0))

    def body(w_ref, m_ref, v_ref, p_ref, *rest):
        g_ref, d_ref, mo_ref, vo_ref = rest[len(after):]
        g = p_ref[0].astype(f32)
        for i in range(1, NCHIP):
            g = g + p_ref[i].astype(f32)
        d, mn, vn = _adamw_math(w_ref[...], g, m_ref[...], v_ref[...])
        g_ref[...] = g
        d_ref[...] = d
        mo_ref[...] = mn
        vo_ref[...] = vn

    shp = jax.ShapeDtypeStruct((R, C), f32)
    return pl.pallas_call(
        body, name=name, grid=(R // tr,),
        in_specs=[blk, blk, blk, pl.BlockSpec((NCHIP, tr, C), lambda i: (0, i, 0))] + [pl.BlockSpec(memory_space=pl.ANY)] * len(after),
        out_specs=(blk, blk, blk, blk), out_shape=(shp, shp, shp, shp), compiler_params=_params(("parallel",)),
    )(w, m, v, parts, *after)


def _adamw_small(w, g, m, v, name):
    def body(w_ref, g_ref, m_ref, v_ref, d_ref, mo_ref, vo_ref):
        d, mn, vn = _adamw_math(w_ref[...], g_ref[...], m_ref[...], v_ref[...])
        d_ref[...] = d
        mo_ref[...] = mn
        vo_ref[...] = vn

    shp = jax.ShapeDtypeStruct(w.shape, f32)
    return pl.pallas_call(body, name=name, out_shape=(shp, shp, shp))(w, g, m, v)


SMALL_COLS = 1024


def _pack(arrs):
    flat = jnp.concatenate([a.reshape(-1) for a in arrs])
    rows = -(-flat.shape[0] // (8 * SMALL_COLS)) * 8
    return jnp.pad(flat, (0, rows * SMALL_COLS - flat.shape[0])).reshape(rows, SMALL_COLS)


def _unpack(packed, like):
    flat = packed.reshape(-1)
    out, pos = [], 0
    for a in like:
        out.append(flat[pos:pos + a.size].reshape(a.shape))
        pos += a.size
    return out


def kernel(x, w_in, b_gate, norm_mix, norm_ffn, hgrn_lb_logits, hgrn_out_gain, q_gain, k_gain, rel_bias, w_proj_a, w_proj_b, w_out, w_ffn_in, w_ffn_out, loss_target, m_w_in, m_b_gate, m_norm_mix, m_norm_ffn, m_hgrn_lb_logits, m_hgrn_out_gain, m_q_gain, m_k_gain, m_rel_bias, m_w_proj_a, m_w_proj_b, m_w_out, m_w_ffn_in, m_w_ffn_out, v_w_in, v_b_gate, v_norm_mix, v_norm_ffn, v_hgrn_lb_logits, v_hgrn_out_gain, v_q_gain, v_k_gain, v_rel_bias, v_w_proj_a, v_w_proj_b, v_w_out, v_w_ffn_in, v_w_ffn_out):
    xs = x[0]
    target = loss_target[0]
    T, D = xs.shape
    d_a = hgrn_out_gain.shape[-1]
    H = d_a // HEAD
    d_b = d_a
    off_b = 4 * d_a
    off_g = off_b + 3 * d_b
    assert rel_bias.shape[1] == H and T % CHUNK == 0 and T // CHUNK > N_PAST

    big_w = [w_in[0], w_proj_a[0], w_proj_b[0], w_out[0], w_ffn_in[0], w_ffn_out[0]]
    big_m = [m_w_in[0], m_w_proj_a[0], m_w_proj_b[0], m_w_out[0], m_w_ffn_in[0], m_w_ffn_out[0]]
    big_v = [v_w_in[0], v_w_proj_a[0], v_w_proj_b[0], v_w_out[0], v_w_ffn_in[0], v_w_ffn_out[0]]

    sh = [w.astype(bf16) for w in big_w]
    (g_in,) = _seq_gather(sh[0:1], "gather_a", 1)
    g_pa, g_pb, g_out = _seq_gather(sh[1:4], "gather_b", 2)
    (g_fin,) = _seq_gather(sh[4:5], "gather_c", 3)
    (g_fout,) = _seq_gather(sh[5:6], "gather_d", 4)

    h = _rms_fwd(xs, norm_mix, "rms_mix")
    proj = _mm(h, g_in, mode="nn", b_blocked=True, name="mm_proj")
    y_a, states = _hgrn_fwd(proj, hgrn_lb_logits, hgrn_out_gain, H, "hgrn_fwd")
    idx = jnp.asarray(_rel_index())
    rb_pad = jnp.pad(rel_bias[0], ((0, 0), (0, N_REL_PAD - N_REL)))
    bias = _bias_table(rb_pad, idx, "bias_table").reshape(H, CHUNK, BAND * CHUNK)
    y_b = _attn_fwd(proj, q_gain, k_gain, bias, off_b, H, "attn_fwd")
    wg_out = g_out.reshape(-1, g_out.shape[-1])
    wg_fout = g_fout.reshape(-1, g_fout.shape[-1])
    pa, pb, merged = _proj_merge(y_a, y_b, g_pa, g_pb, proj, b_gate, off_g, "mm_proj_ab")
    x1, h2 = _out_rms(merged, wg_out, xs, norm_ffn, "mm_out")
    gu, act = _ffn_in_swiglu(h2, g_fin, "mm_ffn_in")
    dy, loss_acc = _ffn_out_loss(act, wg_fout, x1, target, "mm_ffn_out")
    loss_part = loss_acc[0:1, 0:1] * (0.5 / D)

    gw_fout = _mm(act, dy, mode="tn", out_dtype=bf16, tm=1408, name="mm_gw_ffn_out")
    dgu = _d_act_swiglu(dy, wg_fout, gu, "mm_d_act")
    gw_fin = _mm(h2, dgu, mode="tn", b_stacked=True, out_blocked=True, out_dtype=bf16, tn=g_fin.shape[-1], name="mm_gw_ffn_in")
    dh2 = _mm_nt_blocked(dgu, g_fin, "mm_d_h2", a_stacked=True)
    (p_fout, p_fin), sums_a = _reduce_scatter([gw_fout.reshape(NDEV, -1, D), gw_fin], "a", (5, 6), add_after=(dh2,))
    dx1, g_norm_ffn = _rms_bwd(x1, norm_ffn, dh2, dy, "rms_ffn_bwd", after=sums_a)

    gw_out = _mm(merged, dx1, mode="tn", out_dtype=bf16, name="mm_gw_out")
    dpa, dpb, dgl, g_b_gate = _d_merged_branches(dx1, wg_out, pa, pb, proj, b_gate, off_g, "mm_d_merged")
    dy_a = _mm(dpa, g_pa, mode="nt", b_blocked=True, tm=2048, name="mm_d_ya")
    dy_b = _mm(dpb, g_pb, mode="nt", b_blocked=True, tm=2048, name="mm_d_yb")
    gw_pa = _mm(y_a, dpa, mode="tn", out_blocked=True, out_dtype=bf16, tn=g_pa.shape[-1], name="mm_gw_proj_a")
    gw_pb = _mm(y_b, dpb, mode="tn", out_blocked=True, out_dtype=bf16, tn=g_pb.shape[-1], name="mm_gw_proj_b")

    dq_a, df_a, di_a, dg_a, g_logits, g_gain = _hgrn_bwd(proj, hgrn_lb_logits, hgrn_out_gain, states, dy_a, H, "hgrn_bwd")
    dq_b, dk_b, dv_b, dbias, g_qg, g_kg = _attn_bwd(proj, q_gain, k_gain, bias, dy_b, off_b, H, "attn_bwd")
    g_rel_pad = _bias_table_bwd(dbias.reshape(H, -1), idx, "bias_table_bwd")
    g_rel = g_rel_pad[:, :N_REL]
    dproj = jnp.concatenate([dq_a, df_a, di_a, dg_a, dq_b, dk_b, dv_b, dgl[0], dgl[1]], axis=1)
    (p_out, p_pa, p_pb), sums_b = _reduce_scatter([gw_out.reshape(NDEV, -1, D), gw_pa, gw_pb], "b", (7, 8), after=(g_gain, p_fout, p_fin), add_after=(dy_b,))
    gw_in = _mm(h, dproj, mode="tn", out_blocked=True, out_dtype=bf16, tn=g_in.shape[-1], name="mm_gw_in", after=sums_b)
    upd_fout = _adamw_parts(big_w[5], big_m[5], big_v[5], p_fout, "adamw_w_ffn_out", after=(gw_in,))
    (p_in,), sums_c = _reduce_scatter([gw_in], "c", (9, 10), after=(p_out, p_pa, p_pb), add_after=(g_rel_pad, upd_fout[0]))
    dh = _mm_nt_blocked(dproj, g_in, "mm_d_h", after=sums_c)
    grad_x, g_norm_mix = _rms_bwd(xs, norm_mix, dh, dx1, "rms_mix_bwd")

    parts = [p_in, p_pa, p_pb, p_out, p_fin, p_fout]
    names = ["w_in", "w_proj_a", "w_proj_b", "w_out", "w_ffn_in", "w_ffn_out"]
    big = {}
    for nm, w, m, v, p in zip(names, big_w, big_m, big_v, parts):
        upd = upd_fout if nm == "w_ffn_out" else _adamw_parts(w, m, v, p, "adamw_" + nm, after=() if nm == "w_in" else (g_norm_mix,))
        big[nm] = [o[None] for o in upd]

    small_names = ["b_gate", "norm_mix", "norm_ffn", "hgrn_lb_logits", "hgrn_out_gain", "q_gain", "k_gain", "rel_bias"]
    small_w = [b_gate, norm_mix, norm_ffn, hgrn_lb_logits, hgrn_out_gain, q_gain, k_gain, rel_bias]
    small_m = [m_b_gate, m_norm_mix, m_norm_ffn, m_hgrn_lb_logits, m_hgrn_out_gain, m_q_gain, m_k_gain, m_rel_bias]
    small_v = [v_b_gate, v_norm_mix, v_norm_ffn, v_hgrn_lb_logits, v_hgrn_out_gain, v_q_gain, v_k_gain, v_rel_bias]
    small_g = [g_b_gate.reshape(1, -1), g_norm_mix, g_norm_ffn, g_logits, g_gain, g_qg, g_kg, g_rel[None], loss_part]
    g_sum = _small_all_reduce(_pack(small_g), "reduce_small")
    loss = _unpack(g_sum, small_g)[-1].reshape(())
    d_s, m_s, v_s = _adamw_small(_pack(small_w), g_sum, _pack(small_m), _pack(small_v), "adamw_small")
    small = {}
    for nm, g, d, m, v in zip(small_names, _unpack(g_sum, small_w), _unpack(d_s, small_w), _unpack(m_s, small_w), _unpack(v_s, small_w)):
        small[nm] = [g, d, m, v]

    order = ["w_in", "b_gate", "norm_mix", "norm_ffn", "hgrn_lb_logits", "hgrn_out_gain", "q_gain", "k_gain", "rel_bias",
             "w_proj_a", "w_proj_b", "w_out", "w_ffn_in", "w_ffn_out"]
    res = {**big, **small}
    outs = [loss, grad_x[None]]
    for k in range(4):
        outs += [res[nm][k] for nm in order]
    return tuple(outs)
```

```python
import functools

import numpy as np
import jax
import jax.numpy as jnp
from jax import lax
from jax.experimental import pallas as pl
from jax.experimental.pallas import tpu as pltpu
from jax.experimental.pallas import tpu_sc as plsc

f32 = jnp.float32
bf16 = jnp.bfloat16
MESH = pl.DeviceIdType.MESH
AXES = ("x", "y", "c")
NDEV = 8

CHUNK = 64
HEAD = 128
SUB = 8
HGRN_BWD_HEADS = 8
ATTN_CHUNKS = 16
N_PAST = 8
BAND = N_PAST + 1
PAD = N_PAST * CHUNK
REL_FUTURE = CHUNK - 1
REL_PAST = 2 * CHUNK - 1
N_REL = REL_FUTURE + REL_PAST + 1
N_REL_PAD = 256
EPS = 1e-6
NEG = -1e30

ADAM_LR = 0.001
ADAM_B1 = 0.9
ADAM_B2 = 0.999
ADAM_EPS = 1e-08
ADAM_WD = 0.01
ADAM_STEP = 10

VMEM_LIMIT = 56 * 1024 * 1024


def _params(sem=None):
    return pltpu.CompilerParams(dimension_semantics=sem, vmem_limit_bytes=VMEM_LIMIT)


def _tile(n, pref, unit=128):
    if n <= pref:
        return n
    t = (pref // unit) * unit
    while t >= unit:
        if n % t == 0:
            return t
        t -= unit
    return n


_sigmoid = jax.nn.sigmoid


def _mm(a, b, *, mode, name, b_blocked=False, out_blocked=False, out_dtype=f32, tm=1024, tn=1024, tk=2048, after=(),
        b_stacked=False):
    if mode == "tn":
        K, M = a.shape
    else:
        M, K = a.shape
    if b_blocked:
        nb, mid, cb = b.shape
        if mode == "nn":
            assert mid == K
            N, tn = nb * cb, cb
        else:
            assert mode == "nt" and nb * cb == K
            N, tk = mid, cb
    elif b_stacked:
        assert mode == "tn"
        N = 2 * b.shape[2]
    else:
        N = b.shape[1] if mode in ("nn", "tn") else b.shape[0]
    tm = _tile(M, tm)
    tn = tn if (b_blocked and mode == "nn") or out_blocked else _tile(N, tn)
    tk = tk if b_blocked and mode == "nt" else _tile(K, tk)
    assert M % tm == 0 and N % tn == 0 and K % tk == 0
    nk = K // tk
    grid = (M // tm, N // tn, nk)
    if mode == "tn":
        a_spec = pl.BlockSpec((tk, tm), lambda i, j, k: (k, i))
    else:
        a_spec = pl.BlockSpec((tm, tk), lambda i, j, k: (i, k))
    if mode == "nn":
        b_spec = pl.BlockSpec((None, tk, cb), lambda i, j, k: (j, k, 0)) if b_blocked else pl.BlockSpec((tk, tn), lambda i, j, k: (k, j))
    elif mode == "nt":
        b_spec = pl.BlockSpec((None, tn, cb), lambda i, j, k: (k, j, 0)) if b_blocked else pl.BlockSpec((tn, tk), lambda i, j, k: (j, k))
    elif b_stacked:
        nh = N // 2 // tn
        b_spec = pl.BlockSpec((None, tk, tn), lambda i, j, k: (j // nh, k, j % nh))
    else:
        b_spec = pl.BlockSpec((tk, tn), lambda i, j, k: (k, j))
    if out_blocked:
        out_shape = jax.ShapeDtypeStruct((N // tn, M, tn), out_dtype)
        o_spec = pl.BlockSpec((None, tm, tn), lambda i, j, k: (j, i, 0))
    else:
        out_shape = jax.ShapeDtypeStruct((M, N), out_dtype)
        o_spec = pl.BlockSpec((tm, tn), lambda i, j, k: (i, j))
    dims = {"nn": ((1,), (0,)), "nt": ((1,), (1,)), "tn": ((0,), (0,))}[mode]

    def body(a_ref, b_ref, *rest):
        o_ref, acc = rest[len(after)], rest[len(after) + 1:]
        p = lax.dot_general(a_ref[...].astype(bf16), b_ref[...].astype(bf16), (dims, ((), ())), preferred_element_type=f32)
        if nk == 1:
            o_ref[...] = p.astype(out_dtype)
        else:
            acc_ref = acc[0]
            k = pl.program_id(2)

            @pl.when(k == 0)
            def _():
                acc_ref[...] = p

            @pl.when(k > 0)
            def _():
                acc_ref[...] += p

            @pl.when(k == nk - 1)
            def _():
                o_ref[...] = acc_ref[...].astype(out_dtype)

    return pl.pallas_call(
        body, name=name, grid=grid, in_specs=[a_spec, b_spec] + [pl.BlockSpec(memory_space=pl.ANY)] * len(after), out_specs=o_spec,
        out_shape=out_shape, scratch_shapes=[pltpu.VMEM((tm, tn), f32)] if nk > 1 else [],
        compiler_params=_params(("parallel", "parallel", "arbitrary")),
    )(a, b, *after)


def _mm_nt_blocked(a, b, name, *, a_stacked=False, after=(), tm=1024, tn=1024, kb=2):
    nb, N, cb = b.shape
    M = a.shape[1] if a_stacked else a.shape[0]
    tm, tn = _tile(M, tm, 8), _tile(N, tn)
    nk = nb // kb
    per_half = nk // 2

    def body(a_ref, b_ref, *rest):
        o_ref, acc_ref = rest[len(after)], rest[len(after) + 1]
        dn = (((1,), (1,)), ((), ()))
        k = pl.program_id(2)
        p = None
        for q in range(kb):
            d = lax.dot_general(a_ref[:, q * cb:(q + 1) * cb], b_ref[q], dn, preferred_element_type=f32)
            p = d if p is None else p + d

        @pl.when(k == 0)
        def _():
            acc_ref[...] = p

        @pl.when(jnp.logical_and(k > 0, k < nk - 1))
        def _():
            acc_ref[...] += p

        @pl.when(k == nk - 1)
        def _():
            o_ref[...] = acc_ref[...] + p

    if a_stacked:
        a_spec = pl.BlockSpec((None, tm, kb * cb), lambda i, j, k: (k // per_half, i, k % per_half))
    else:
        a_spec = pl.BlockSpec((tm, kb * cb), lambda i, j, k: (i, k))
    return pl.pallas_call(
        body, name=name, grid=(M // tm, N // tn, nk),
        in_specs=[a_spec, pl.BlockSpec((kb, tn, cb), lambda i, j, k: (k, j, 0))] + [pl.BlockSpec(memory_space=pl.ANY)] * len(after),
        out_specs=pl.BlockSpec((tm, tn), lambda i, j, k: (i, j)), out_shape=jax.ShapeDtypeStruct((M, N), f32),
        scratch_shapes=[pltpu.VMEM((tm, tn), f32)],
        compiler_params=_params(("parallel", "parallel", "arbitrary")),
    )(a, b, *after)


def _rms_fwd(x, gain, name):
    T, D = x.shape
    tr = _tile(T, 256, 8)
    row = pl.BlockSpec((tr, D), lambda i: (i, 0))

    def body(x_ref, g_ref, h_ref):
        xs = x_ref[...]
        r = lax.rsqrt(jnp.mean(xs * xs, axis=-1, keepdims=True) + EPS)
        h_ref[...] = (xs * r * g_ref[...]).astype(bf16)

    return pl.pallas_call(body, name=name, grid=(T // tr,), in_specs=[row, pl.BlockSpec((1, D), lambda i: (0, 0))], out_specs=row,
                          out_shape=jax.ShapeDtypeStruct((T, D), bf16), compiler_params=_params(("parallel",)))(x, gain)


def _rms_bwd(xs, gain, dh, extra, name, after=()):
    T, D = xs.shape
    tr = _tile(T, 256, 8)
    row = pl.BlockSpec((tr, D), lambda i: (i, 0))
    vec = pl.BlockSpec((1, D), lambda i: (0, 0))

    def body(x_ref, g_ref, dh_ref, e_ref, *rest):
        dx_ref, dg_ref = rest[len(after):]
        x = x_ref[...]
        r = lax.rsqrt(jnp.mean(x * x, axis=-1, keepdims=True) + EPS)
        xhat = x * r
        dh_v = dh_ref[...]
        gd = dh_v * g_ref[...]
        dx_ref[...] = e_ref[...] + r * (gd - xhat * jnp.mean(gd * xhat, axis=-1, keepdims=True))
        part = jnp.sum(dh_v * xhat, axis=0, keepdims=True)

        @pl.when(pl.program_id(0) == 0)
        def _():
            dg_ref[...] = part

        @pl.when(pl.program_id(0) > 0)
        def _():
            dg_ref[...] += part

    return pl.pallas_call(body, name=name, grid=(T // tr,),
                          in_specs=[row, vec, row, row] + [pl.BlockSpec(memory_space=pl.ANY)] * len(after), out_specs=(row, vec),
                          out_shape=(jax.ShapeDtypeStruct((T, D), f32), jax.ShapeDtypeStruct((1, D), f32)),
                          compiler_params=_params(("arbitrary",)))(xs, gain, dh, extra, *after)


def _proj_merge(y_a, y_b, w_a, w_b, proj, b_gate, off, name):
    T, K = y_a.shape
    nb, _, cb = w_a.shape
    D = nb * cb
    tm = _tile(T, 1024, 8)
    oa, ob = off // cb, (off + D) // cb
    blk = pl.BlockSpec((tm, cb), lambda j, i: (i, j))
    row = pl.BlockSpec((tm, K), lambda j, i: (i, 0))
    wsp = pl.BlockSpec((None, K, cb), lambda j, i: (j, 0, 0))

    def body(ya_ref, yb_ref, wa_ref, wb_ref, ga_ref, gb_ref, ba_ref, bb_ref, pa_ref, pb_ref, m_ref):
        dn = (((1,), (0,)), ((), ()))
        pa = lax.dot_general(ya_ref[...], wa_ref[...], dn, preferred_element_type=f32)
        pb = lax.dot_general(yb_ref[...], wb_ref[...], dn, preferred_element_type=f32)
        pa_ref[...] = pa
        pb_ref[...] = pb
        m_ref[...] = (_sigmoid(ga_ref[...] + ba_ref[...]) * pa + _sigmoid(gb_ref[...] + bb_ref[...]) * pb).astype(bf16)

    return pl.pallas_call(
        body, name=name, grid=(nb, T // tm),
        in_specs=[row, row, wsp, wsp, pl.BlockSpec((tm, cb), lambda j, i: (i, oa + j)), pl.BlockSpec((tm, cb), lambda j, i: (i, ob + j)),
                  pl.BlockSpec((1, cb), lambda j, i: (0, j)), pl.BlockSpec((1, cb), lambda j, i: (0, nb + j))],
        out_specs=(blk, blk, blk),
        out_shape=(jax.ShapeDtypeStruct((T, D), f32), jax.ShapeDtypeStruct((T, D), f32), jax.ShapeDtypeStruct((T, D), bf16)),
        compiler_params=_params(("parallel", "parallel")),
    )(y_a, y_b, w_a, w_b, proj, proj, b_gate, b_gate)


def _out_rms(merged, w_out, x, gain, name):
    T, K = merged.shape
    D = w_out.shape[1]
    tm = _tile(T, 256, 8)
    row = pl.BlockSpec((tm, D), lambda i: (i, 0))

    def body(m_ref, w_ref, x_ref, g_ref, x1_ref, h_ref):
        x1 = x_ref[...] + lax.dot_general(m_ref[...], w_ref[...], (((1,), (0,)), ((), ())), preferred_element_type=f32)
        x1_ref[...] = x1
        r = lax.rsqrt(jnp.mean(x1 * x1, axis=-1, keepdims=True) + EPS)
        h_ref[...] = (x1 * r * g_ref[...]).astype(bf16)

    return pl.pallas_call(
        body, name=name, grid=(T // tm,),
        in_specs=[pl.BlockSpec((tm, K), lambda i: (i, 0)), pl.BlockSpec((K, D), lambda i: (0, 0)), row, pl.BlockSpec((1, D), lambda i: (0, 0))],
        out_specs=(row, row), out_shape=(jax.ShapeDtypeStruct((T, D), f32), jax.ShapeDtypeStruct((T, D), bf16)),
        compiler_params=_params(("parallel",)),
    )(merged, w_out, x, gain)


def _d_merged_branches(dx, w_out, pa, pb, proj, b_gate, off, name):
    T, D = pa.shape
    tm, tn = _tile(T, 512, 8), _tile(D, 1024)
    oa, ob, nb = off // tn, (off + D) // tn, D // tn
    blk = pl.BlockSpec((tm, tn), lambda j, i: (i, j))

    def body(dx_ref, w_ref, pa_ref, pb_ref, ga_ref, gb_ref, ba_ref, bb_ref, dpa_ref, dpb_ref, dgl_ref, db_ref):
        dm = lax.dot_general(dx_ref[...].astype(bf16), w_ref[...], (((1,), (1,)), ((), ())), preferred_element_type=f32)
        sums = []
        for p_ref, gl_ref, b_ref, dp_ref, k in ((pa_ref, ga_ref, ba_ref, dpa_ref, 0), (pb_ref, gb_ref, bb_ref, dpb_ref, 1)):
            g = _sigmoid(gl_ref[...] + b_ref[...])
            dp_ref[...] = (dm * g).astype(bf16)
            dgl = dm * p_ref[...] * g * (1.0 - g)
            dgl_ref[k] = dgl.astype(bf16)
            sums.append(jnp.sum(dgl, axis=0, keepdims=True))

        @pl.when(pl.program_id(1) == 0)
        def _():
            db_ref[...] = jnp.zeros((2, 1, tn), f32)

        db_ref[0] += sums[0]
        db_ref[1] += sums[1]

    return pl.pallas_call(
        body, name=name, grid=(nb, T // tm),
        in_specs=[pl.BlockSpec((tm, D), lambda j, i: (i, 0)), pl.BlockSpec((tn, D), lambda j, i: (j, 0)), blk, blk,
                  pl.BlockSpec((tm, tn), lambda j, i: (i, oa + j)), pl.BlockSpec((tm, tn), lambda j, i: (i, ob + j)),
                  pl.BlockSpec((1, tn), lambda j, i: (0, j)), pl.BlockSpec((1, tn), lambda j, i: (0, nb + j))],
        out_specs=(blk, blk, pl.BlockSpec((2, tm, tn), lambda j, i: (0, i, j)), pl.BlockSpec((2, 1, tn), lambda j, i: (0, 0, j))),
        out_shape=(jax.ShapeDtypeStruct((T, D), bf16), jax.ShapeDtypeStruct((T, D), bf16), jax.ShapeDtypeStruct((2, T, D), bf16),
                   jax.ShapeDtypeStruct((2, 1, D), f32)),
        compiler_params=_params(("parallel", "arbitrary")),
    )(dx, w_out, pa, pb, proj, proj, b_gate, b_gate)


def _ffn_in_swiglu(h, w, name):
    T, K = h.shape
    nb, _, cb = w.shape
    half = nb // 2
    F = half * cb
    tm = _tile(T, 512, 8)

    def body(h_ref, wg_ref, wu_ref, gu_ref, act_ref):
        dn = (((1,), (0,)), ((), ()))
        hv = h_ref[...]
        g = lax.dot_general(hv, wg_ref[...], dn, preferred_element_type=f32)
        u = lax.dot_general(hv, wu_ref[...], dn, preferred_element_type=f32)
        gu_ref[0] = g
        gu_ref[1] = u
        act_ref[...] = (g * _sigmoid(g) * u).astype(bf16)

    return pl.pallas_call(
        body, name=name, grid=(half, T // tm),
        in_specs=[pl.BlockSpec((tm, K), lambda j, i: (i, 0)), pl.BlockSpec((None, K, cb), lambda j, i: (j, 0, 0)),
                  pl.BlockSpec((None, K, cb), lambda j, i: (j + half, 0, 0))],
        out_specs=(pl.BlockSpec((2, tm, cb), lambda j, i: (0, i, j)), pl.BlockSpec((tm, cb), lambda j, i: (i, j))),
        out_shape=(jax.ShapeDtypeStruct((2, T, F), f32), jax.ShapeDtypeStruct((T, F), bf16)),
        compiler_params=_params(("parallel", "parallel")),
    )(h, w, w)


def _d_act_swiglu(dy, w_out, gu, name):
    T, D = dy.shape
    F = w_out.shape[0]
    tm, tn = _tile(T, 512, 8), _tile(F, 1408)

    def body(dy_ref, w_ref, gu_ref, o_ref):
        halves = [pl.ds(r * (tm // 2), tm // 2) for r in range(2)]
        wv = w_ref[...]
        d = [lax.dot_general(dy_ref[r, :].astype(bf16), wv, (((1,), (1,)), ((), ())), preferred_element_type=f32) for r in halves]
        for r, dr in zip(halves, d):
            g = gu_ref[0, r, :]
            s = _sigmoid(g)
            o_ref[0, r, :] = (dr * gu_ref[1, r, :] * s * (1.0 + g * (1.0 - s))).astype(bf16)
            o_ref[1, r, :] = (dr * g * s).astype(bf16)

    blk = pl.BlockSpec((2, tm, tn), lambda j, i: (0, i, j))
    return pl.pallas_call(
        body, name=name, grid=(F // tn, T // tm),
        in_specs=[pl.BlockSpec((tm, D), lambda j, i: (i, 0)), pl.BlockSpec((tn, D), lambda j, i: (j, 0)), blk],
        out_specs=blk, out_shape=jax.ShapeDtypeStruct((2, T, F), bf16), compiler_params=_params(("parallel", "parallel")),
    )(dy, w_out, gu)


def _ffn_out_loss(act, w, x1, target, name):
    T, F = act.shape
    D = w.shape[1]
    tm, tn, tk = _tile(T, 512, 8), _tile(D, 1024), _tile(F, 2816)
    nk = F // tk
    blk = pl.BlockSpec((tm, tn), lambda i, j, k: (i, j))

    def body(a_ref, w_ref, x_ref, t_ref, dy_ref, l_ref, acc_ref):
        i, j, k = pl.program_id(0), pl.program_id(1), pl.program_id(2)
        p = lax.dot_general(a_ref[...], w_ref[...], (((1,), (0,)), ((), ())), preferred_element_type=f32)

        @pl.when(jnp.logical_and(jnp.logical_and(i == 0, j == 0), k == 0))
        def _():
            l_ref[...] = jnp.zeros((8, 128), f32)

        @pl.when(k == 0)
        def _():
            acc_ref[...] = p

        @pl.when(k > 0)
        def _():
            acc_ref[...] += p

        @pl.when(k == nk - 1)
        def _():
            d = acc_ref[...] + x_ref[...] - t_ref[...]
            dy_ref[...] = d * (1.0 / D)
            l_ref[...] += jnp.sum(jnp.sum(d * d, axis=1, keepdims=True), axis=0, keepdims=True)

    return pl.pallas_call(
        body, name=name, grid=(T // tm, D // tn, nk),
        in_specs=[pl.BlockSpec((tm, tk), lambda i, j, k: (i, k)), pl.BlockSpec((tk, tn), lambda i, j, k: (k, j)), blk, blk],
        out_specs=(blk, pl.BlockSpec((8, 128), lambda i, j, k: (0, 0))),
        out_shape=(jax.ShapeDtypeStruct((T, D), f32), jax.ShapeDtypeStruct((8, 128), f32)),
        scratch_shapes=[pltpu.VMEM((tm, tn), f32)],
        compiler_params=_params(("arbitrary", "arbitrary", "arbitrary")),
    )(act, w, x1, target)


_DIMS = {"nn": ((1,), (0,)), "nt": ((1,), (1,)), "tn": ((0,), (0,))}
_MODE = {v: k for k, v in _DIMS.items()}


def _dot_bf16(a, b, mode):
    return lax.dot_general(a.astype(bf16), b.astype(bf16), (_DIMS[mode], ((), ())), preferred_element_type=f32)


@functools.partial(jax.custom_vjp, nondiff_argnums=(2,))
def _dotm(a, b, mode):
    return _dot_bf16(a, b, mode)


def _dotm_fwd(a, b, mode):
    return _dot_bf16(a, b, mode), (a, b)


def _dotm_bwd(mode, res, g):
    a, b = res
    if mode == "nn":
        return _dot_bf16(g, b, "nt"), _dot_bf16(a, g, "tn")
    if mode == "nt":
        return _dot_bf16(g, b, "nn"), _dot_bf16(g, a, "tn")
    return _dot_bf16(b, g, "nt"), _dot_bf16(a, g, "nn")


_dotm.defvjp(_dotm_fwd, _dotm_bwd)


def _dotb(a, b, dims):
    return _dotm(a, b, _MODE[dims])


def _split3(v):
    def top(t):
        return lax.bitcast_convert_type(lax.bitcast_convert_type(t, jnp.uint32) & jnp.uint32(0xFFFF0000), f32)

    hi = top(v)
    mid = top(v - hi)
    low = (v - hi) - mid
    return hi.astype(bf16), mid.astype(bf16), low.astype(bf16)


def _dot3(v, m, dims, v_first):
    m = m.astype(bf16)
    dn = (dims, ((), ()))
    parts = [lax.dot_general(p, m, dn, preferred_element_type=f32) if v_first else lax.dot_general(m, p, dn, preferred_element_type=f32)
             for p in _split3(v)]
    return parts[0] + parts[1] + parts[2]


def _triangle_sum(v, lower):
    row = lax.broadcasted_iota(jnp.int32, (CHUNK, CHUNK), 0)
    col = lax.broadcasted_iota(jnp.int32, (CHUNK, CHUNK), 1)
    return _dot3(v, (col <= row) if lower else (col >= row), ((1,), (0,)), False)


@jax.custom_vjp
def _cumsum_rows(v):
    return _triangle_sum(v, True)


_cumsum_rows.defvjp(lambda v: (_triangle_sum(v, True), None), lambda _, g: (_triangle_sum(g, False),))


def _hgrn_heads(q, fl, iv, g, logits, gain, st):
    r = range(len(q))
    lb = [jax.nn.softmax(logits[j], axis=0)[0:1] for j in r]
    f = [lb[j] + (1.0 - lb[j]) * _sigmoid(fl[j]) for j in r]
    lf = [jnp.log(f[j]) for j in r]
    kk = [1.0 - f[j] for j in r]
    qs = [q[j] * _sigmoid(q[j]) for j in r]
    b = [_cumsum_rows(lf[j]) for j in r]
    b_last = [jnp.sum(lf[j], axis=0, keepdims=True) for j in r]
    o = [_dotb(qs[j] * jnp.exp(b[j]), st[j], ((1,), (1,))) for j in r]
    r3 = lax.broadcasted_iota(jnp.int32, (SUB, SUB, HEAD), 0)
    c3 = lax.broadcasted_iota(jnp.int32, (SUB, SUB, HEAD), 1)
    parts = [[] for _ in r]
    for i in range(CHUNK // SUB):
        lo, hi = i * SUB, (i + 1) * SUB
        bi = [b[j][lo:hi] for j in r]
        dec = [jnp.exp(jnp.where(c3 <= r3, bi[j][:, None, :] - bi[j][None, :, :], -jnp.inf)) for j in r]
        s = [jnp.sum(qs[j][lo:hi][:, None, :] * kk[j][lo:hi][None, :, :] * dec[j], axis=-1) for j in r]
        if i > 0:
            anchor = [jnp.max(bi[j], axis=0, keepdims=True) for j in r]
            qa = [qs[j][lo:hi] * jnp.exp(bi[j] - anchor[j]) for j in r]
            kd = [kk[j][:lo] * jnp.exp(anchor[j] - b[j][:lo]) for j in r]
            s = [jnp.concatenate([_dotb(qa[j], kd[j], ((1,), (1,))), s[j]], axis=1) for j in r]
        for j in r:
            parts[j].append(_dotb(s[j], iv[j][:hi], ((1,), (0,))))
    o = [o[j] + jnp.concatenate(parts[j], axis=0) for j in r]
    st_new = [st[j] * jnp.exp(b_last[j]) + _dotb(iv[j], kk[j] * jnp.exp(b_last[j] - b[j]), ((0,), (0,))) for j in r]
    o = [o[j] * lax.rsqrt(jnp.mean(o[j] * o[j], axis=-1, keepdims=True) + EPS) for j in r]
    o = [o[j] * gain[j] * (g[j] * _sigmoid(g[j])) for j in r]
    return o, st_new


def _group(n, pref):
    while n % pref:
        pref //= 2
    return pref


def _hgrn_fwd(proj, logits, gain, n_heads, name):
    T = proj.shape[0]
    nc = T // CHUNK
    H = n_heads
    HB = _group(H, 8)
    W = HB * HEAD

    def col(k):
        return pl.BlockSpec((CHUNK, W), lambda h, c: (c, k * (H // HB) + h))

    def body(q_ref, f_ref, i_ref, g_ref, l_ref, ga_ref, y_ref, s_ref, st):
        @pl.when(pl.program_id(1) == 0)
        def _():
            st[...] = jnp.zeros((HB, HEAD, HEAD), f32)

        cols = [slice(j * HEAD, (j + 1) * HEAD) for j in range(HB)]
        heads = lambda ref: [ref[:, cs] for cs in cols]
        s_ref[...] = st[...]
        o, st_new = _hgrn_heads(heads(q_ref), heads(f_ref), heads(i_ref), heads(g_ref), heads(l_ref), heads(ga_ref), [st[j] for j in range(HB)])
        for j, cs in enumerate(cols):
            y_ref[:, cs] = o[j].astype(bf16)
            st[j] = st_new[j]

    return pl.pallas_call(
        body, name=name, grid=(H // HB, nc),
        in_specs=[col(0), col(1), col(2), col(3), pl.BlockSpec((2, W), lambda h, c: (0, h)), pl.BlockSpec((1, W), lambda h, c: (0, h))],
        out_specs=(pl.BlockSpec((CHUNK, W), lambda h, c: (c, h)), pl.BlockSpec((HB, None, HEAD, HEAD), lambda h, c: (h, c, 0, 0))),
        out_shape=(jax.ShapeDtypeStruct((T, H * HEAD), bf16), jax.ShapeDtypeStruct((H, nc, HEAD, HEAD), f32)),
        scratch_shapes=[pltpu.VMEM((HB, HEAD, HEAD), f32)],
        compiler_params=_params(("parallel", "arbitrary")),
    )(proj, proj, proj, proj, logits, gain)


def _hgrn_bwd(proj, logits, gain, states, dy, n_heads, name):
    T = proj.shape[0]
    nc = T // CHUNK
    H = n_heads
    HB = _group(H, HGRN_BWD_HEADS)
    W = HB * HEAD

    def col(k):
        return pl.BlockSpec((CHUNK, W), lambda h, c: (nc - 1 - c, k * (H // HB) + h))

    out_blk = pl.BlockSpec((CHUNK, W), lambda h, c: (nc - 1 - c, h))

    def body(q_ref, f_ref, i_ref, g_ref, l_ref, ga_ref, s_ref, dy_ref, dq_ref, df_ref, di_ref, dg_ref, dl_ref, dga_ref, dst):
        first = pl.program_id(1) == 0

        @pl.when(first)
        def _():
            dst[...] = jnp.zeros((HB, HEAD, HEAD), f32)
            dl_ref[...] = jnp.zeros((2, W), f32)
            dga_ref[...] = jnp.zeros((1, W), f32)

        cols = [slice(j * HEAD, (j + 1) * HEAD) for j in range(HB)]
        heads = lambda ref: [ref[:, cs] for cs in cols]
        _, vjp = jax.vjp(_hgrn_heads, heads(q_ref), heads(f_ref), heads(i_ref), heads(g_ref), heads(l_ref), heads(ga_ref),
                         [s_ref[j] for j in range(HB)])
        dq, df, di, dg, dl, dga, ds = vjp((heads(dy_ref), [dst[j] for j in range(HB)]))
        for j, cs in enumerate(cols):
            dq_ref[:, cs] = dq[j].astype(bf16)
            df_ref[:, cs] = df[j].astype(bf16)
            di_ref[:, cs] = di[j].astype(bf16)
            dg_ref[:, cs] = dg[j].astype(bf16)
            dst[j] = ds[j]
            dl_ref[:, cs] += dl[j]
            dga_ref[:, cs] += dga[j]

    act = jax.ShapeDtypeStruct((T, H * HEAD), bf16)
    return pl.pallas_call(
        body, name=name, grid=(H // HB, nc),
        in_specs=[col(0), col(1), col(2), col(3), pl.BlockSpec((2, W), lambda h, c: (0, h)), pl.BlockSpec((1, W), lambda h, c: (0, h)),
                  pl.BlockSpec((HB, None, HEAD, HEAD), lambda h, c: (h, nc - 1 - c, 0, 0)), out_blk],
        out_specs=(out_blk, out_blk, out_blk, out_blk, pl.BlockSpec((2, W), lambda h, c: (0, h)), pl.BlockSpec((1, W), lambda h, c: (0, h))),
        out_shape=(act, act, act, act, jax.ShapeDtypeStruct((2, H * HEAD), f32), jax.ShapeDtypeStruct((1, H * HEAD), f32)),
        scratch_shapes=[pltpu.VMEM((HB, HEAD, HEAD), f32)],
        compiler_params=_params(("parallel", "arbitrary")),
    )(proj, proj, proj, proj, logits, gain, states, dy)


def _rel_index():
    t = np.arange(CHUNK)[:, None]
    sp = np.arange(BAND * CHUNK)[None, :]
    dist = (N_PAST - sp // CHUNK) * CHUNK + t - sp % CHUNK
    return (np.clip(dist, -REL_FUTURE, REL_PAST) + REL_FUTURE).reshape(1, -1).astype(np.int32)


def _bias_table(rel_bias_pad, idx, name):
    H = rel_bias_pad.shape[0]
    n = idx.shape[1]
    tc = _tile(n, 4096)

    def body(rb_ref, idx_ref, o_ref):
        onehot = lax.broadcasted_iota(jnp.int32, (N_REL_PAD, tc), 0) == idx_ref[...]
        o_ref[...] = _dot3(rb_ref[...], onehot, ((1,), (0,)), True)

    return pl.pallas_call(
        body, name=name, grid=(n // tc,),
        in_specs=[pl.BlockSpec((H, N_REL_PAD), lambda j: (0, 0)), pl.BlockSpec((1, tc), lambda j: (0, j))],
        out_specs=pl.BlockSpec((H, tc), lambda j: (0, j)), out_shape=jax.ShapeDtypeStruct((H, n), f32),
        compiler_params=_params(("parallel",)),
    )(rel_bias_pad, idx)


def _bias_table_bwd(dbias, idx, name):
    H, n = dbias.shape
    tc = _tile(n, 4096)

    def body(d_ref, idx_ref, o_ref):
        onehot = lax.broadcasted_iota(jnp.int32, (N_REL_PAD, tc), 0) == idx_ref[...]
        part = _dot3(d_ref[...], onehot, ((1,), (1,)), True)

        @pl.when(pl.program_id(0) == 0)
        def _():
            o_ref[...] = part

        @pl.when(pl.program_id(0) > 0)
        def _():
            o_ref[...] += part

    return pl.pallas_call(
        body, name=name, grid=(n // tc,),
        in_specs=[pl.BlockSpec((H, tc), lambda j: (0, j)), pl.BlockSpec((1, tc), lambda j: (0, j))],
        out_specs=pl.BlockSpec((H, N_REL_PAD), lambda j: (0, 0)), out_shape=jax.ShapeDtypeStruct((H, N_REL_PAD), f32),
        compiler_params=_params(("arbitrary",)),
    )(dbias, idx)


def _head_norm(t, gain):
    return t * lax.rsqrt(jnp.mean(t * t, axis=-1, keepdims=True) + EPS) * gain


def _attn_chunks(qs, kbs, vbs, qg, bias, ns):
    r = range(len(qs))
    qh = [_head_norm(qs[j], qg) for j in r]
    s = [_dotb(qh[j], kbs[j], ((1,), (1,))) * (HEAD ** -0.5) + bias for j in r]
    col = lax.broadcasted_iota(jnp.int32, (1, BAND * CHUNK), 1)
    s = [jnp.where(ns[j] * CHUNK - PAD + col >= 0, s[j], NEG) for j in r]
    e = [jnp.exp(s[j] - jnp.max(s[j], axis=-1, keepdims=True)) for j in r]
    p = [e[j] / jnp.sum(e[j], axis=-1, keepdims=True) for j in r]
    return [_dotb(p[j], vbs[j], ((1,), (0,))) for j in r]


def _attn_fwd(proj, q_gain, k_gain, bias, off, n_heads, name):
    T = proj.shape[0]
    nc = T // CHUNK
    H = n_heads
    CB = _group(nc, ATTN_CHUNKS)
    o0 = off // HEAD
    full = lambda k: pl.BlockSpec((T, HEAD), lambda h, c: (0, o0 + k * H + h))
    vec = pl.BlockSpec((1, HEAD), lambda h, c: (0, 0))

    def body(q_ref, k_ref, v_ref, qg_ref, kg_ref, b_ref, y_ref, kp, vp):
        c = pl.program_id(1)

        @pl.when(c == 0)
        def _():
            kp[pl.ds(0, PAD), :] = jnp.zeros((PAD, HEAD), f32)
            vp[pl.ds(0, PAD), :] = jnp.zeros((PAD, HEAD), f32)
            kp[pl.ds(PAD, T), :] = _head_norm(k_ref[...], kg_ref[...])
            vp[pl.ds(PAD, T), :] = v_ref[...]

        ns = [c * CB + j for j in range(CB)]
        rows = [pl.ds(j * CHUNK, CHUNK) for j in range(CB)]
        bands = [pl.ds(pl.multiple_of(n * CHUNK, CHUNK), BAND * CHUNK) for n in ns]
        outs = _attn_chunks([q_ref[r, :] for r in rows], [kp[b, :] for b in bands], [vp[b, :] for b in bands], qg_ref[...], b_ref[...], ns)
        for r, o in zip(rows, outs):
            y_ref[r, :] = o.astype(bf16)

    return pl.pallas_call(
        body, name=name, grid=(H, nc // CB),
        in_specs=[pl.BlockSpec((CB * CHUNK, HEAD), lambda h, c: (c, o0 + h)), full(1), full(2), vec, vec,
                  pl.BlockSpec((None, CHUNK, BAND * CHUNK), lambda h, c: (h, 0, 0))],
        out_specs=pl.BlockSpec((CB * CHUNK, HEAD), lambda h, c: (c, h)), out_shape=jax.ShapeDtypeStruct((T, H * HEAD), bf16),
        scratch_shapes=[pltpu.VMEM((T + PAD, HEAD), f32), pltpu.VMEM((T + PAD, HEAD), f32)],
        compiler_params=_params(("parallel", "arbitrary")),
    )(proj, proj, proj, q_gain, k_gain, bias)


def _attn_bwd(proj, q_gain, k_gain, bias, dy, off, n_heads, name):
    T = proj.shape[0]
    nc = T // CHUNK
    H = n_heads
    CB = _group(nc, ATTN_CHUNKS)
    o0 = off // HEAD
    full = lambda k: pl.BlockSpec((T, HEAD), lambda h, c: (0, o0 + k * H + h))
    full_out = pl.BlockSpec((T, HEAD), lambda h, c: (0, h))
    vec = pl.BlockSpec((1, HEAD), lambda h, c: (0, 0))
    chunk_out = pl.BlockSpec((CB * CHUNK, HEAD), lambda h, c: (c, h))
    bias_blk = pl.BlockSpec((None, CHUNK, BAND * CHUNK), lambda h, c: (h, 0, 0))

    def body(q_ref, k_ref, v_ref, qg_ref, kg_ref, b_ref, dy_ref, dq_ref, dk_ref, dv_ref, db_ref, dqg_ref, dkg_ref, kp, vp, dkp, dvp):
        h = pl.program_id(0)
        c = pl.program_id(1)

        @pl.when(c == 0)
        def _():
            kp[pl.ds(0, PAD), :] = jnp.zeros((PAD, HEAD), f32)
            vp[pl.ds(0, PAD), :] = jnp.zeros((PAD, HEAD), f32)
            kp[pl.ds(PAD, T), :] = _head_norm(k_ref[...], kg_ref[...])
            vp[pl.ds(PAD, T), :] = v_ref[...]
            dkp[...] = jnp.zeros((T + PAD, HEAD), f32)
            dvp[...] = jnp.zeros((T + PAD, HEAD), f32)
            db_ref[...] = jnp.zeros((CHUNK, BAND * CHUNK), f32)

        @pl.when(jnp.logical_and(h == 0, c == 0))
        def _():
            dqg_ref[...] = jnp.zeros((1, HEAD), f32)
            dkg_ref[...] = jnp.zeros((1, HEAD), f32)

        ns = [c * CB + j for j in range(CB)]
        rows = [pl.ds(j * CHUNK, CHUNK) for j in range(CB)]
        bands = [pl.ds(pl.multiple_of(n * CHUNK, CHUNK), BAND * CHUNK) for n in ns]
        _, vjp = jax.vjp(functools.partial(_attn_chunks, ns=ns), [q_ref[r, :] for r in rows], [kp[b, :] for b in bands],
                         [vp[b, :] for b in bands], qg_ref[...], b_ref[...])
        dqs, dkbs, dvbs, dqg, db = vjp([dy_ref[r, :] for r in rows])
        db_ref[...] += db
        dqg_ref[...] += dqg
        for r, b, dq, dkb, dvb in zip(rows, bands, dqs, dkbs, dvbs):
            dq_ref[r, :] = dq.astype(bf16)
            dkp[b, :] += dkb
            dvp[b, :] += dvb

        @pl.when(c == nc // CB - 1)
        def _():
            _, nvjp = jax.vjp(_head_norm, k_ref[...], kg_ref[...])
            dk, dkg = nvjp(dkp[pl.ds(PAD, T), :])
            dk_ref[...] = dk.astype(bf16)
            dv_ref[...] = dvp[pl.ds(PAD, T), :].astype(bf16)
            dkg_ref[...] += dkg

    act = jax.ShapeDtypeStruct((T, H * HEAD), bf16)
    gvec = jax.ShapeDtypeStruct((1, HEAD), f32)
    pad_buf = pltpu.VMEM((T + PAD, HEAD), f32)
    return pl.pallas_call(
        body, name=name, grid=(H, nc // CB),
        in_specs=[pl.BlockSpec((CB * CHUNK, HEAD), lambda h, c: (c, o0 + h)), full(1), full(2), vec, vec, bias_blk, chunk_out],
        out_specs=(chunk_out, full_out, full_out, bias_blk, vec, vec),
        out_shape=(act, act, act, jax.ShapeDtypeStruct((H, CHUNK, BAND * CHUNK), f32), gvec, gvec),
        scratch_shapes=[pad_buf, pad_buf, pad_buf, pad_buf],
        compiler_params=_params(("arbitrary", "arbitrary")),
    )(proj, proj, proj, q_gain, k_gain, bias, dy)


def _position():
    x, y, c = lax.axis_index("x"), lax.axis_index("y"), lax.axis_index("c")
    return x, y, c, 4 * x + 2 * y + c


def _flip(v, bit):
    return 1 - v if bit else v


def _chips(x, y):
    return [(1 - x, y), (x, 1 - y), (1 - x, 1 - y)]


def _seq_gather(shards, name, collective_id):
    n = len(shards)

    def body(*refs):
        ins, outs = refs[:n], refs[n:2 * n]
        send, recv, loc = refs[2 * n:]
        x, y, c, me = _position()
        sib = (x, y, 1 - c)
        sel = lambda a, b: c * a + (1 - c) * b
        n1 = (sel(1 - x, x), sel(y, 1 - y))
        n2 = (sel(x, 1 - x), sel(1 - y, y))
        far = (1 - x, 1 - y)
        idx = lambda chip, core: 4 * chip[0] + 2 * chip[1] + core
        barrier = pltpu.get_barrier_semaphore()
        for peer in [sib, (*n1, c), (*n2, c)]:
            pl.semaphore_signal(barrier, inc=1, device_id=peer, device_id_type=MESH)
        pl.semaphore_wait(barrier, 3)

        def copy(w, k, src, blk, to):
            return pltpu.make_async_remote_copy(src_ref=src, dst_ref=outs[w].at[blk], send_sem=send.at[7 * w + k], recv_sem=recv.at[7 * w + k],
                                                device_id=to, device_id_type=MESH)

        mine = [pltpu.make_async_copy(ins[w], outs[w].at[me], loc.at[w]) for w in range(n)]
        for cp in mine:
            cp.start()
        sent = [copy(w, 1, ins[w], me, (*n1, c)) for w in range(n)] + [copy(w, 2, ins[w], me, (*n2, c)) for w in range(n)]
        sent += [copy(w, 0, ins[w], me, sib) for w in range(n)]
        for cp in sent:
            cp.start()
        for k, chip in ((1, n1), (2, n2), (3, far)):
            blk = idx(chip, c)
            for w in range(n):
                copy(w, k, ins[w], blk, sib).wait_recv()
                if k == 1:
                    sent.append(copy(w, 3, outs[w].at[blk], blk, (*n2, c)))
                    sent[-1].start()
                sent.append(copy(w, 3 + k, outs[w].at[blk], blk, sib))
                sent[-1].start()
        for w in range(n):
            copy(w, 0, ins[w], idx((x, y), 1 - c), sib).wait_recv()
        for k, chip in ((4, n2), (5, n1), (6, far)):
            for w in range(n):
                copy(w, k, ins[w], idx(chip, 1 - c), sib).wait_recv()
        for cp in sent:
            cp.wait_send()
        for cp in mine:
            cp.wait()

    return pl.kernel(
        body, out_type=tuple(jax.ShapeDtypeStruct((NDEV,) + s.shape, s.dtype) for s in shards),
        mesh=plsc.ScalarSubcoreMesh(axis_name="sequencer", num_cores=1), name=name,
        scratch_types=(pltpu.SemaphoreType.DMA((7 * n,)), pltpu.SemaphoreType.DMA((7 * n,)), pltpu.SemaphoreType.DMA((n,))),
        compiler_params=pltpu.CompilerParams(collective_id=collective_id),
    )(*shards)


NCHIP = 4


def _seq_pair_exchange(grads, name, collective_id, after=()):
    n, na = len(grads), len(after)

    def body(*refs):
        ins, outs = refs[:n], refs[n + na:2 * n + na]
        send, recv = refs[2 * n + na:]
        x, y, c, me = _position()
        sib = (x, y, 1 - c)
        barrier = pltpu.get_barrier_semaphore()
        pl.semaphore_signal(barrier, inc=1, device_id=sib, device_id_type=MESH)
        pl.semaphore_wait(barrier, 1)
        copies = [pltpu.make_async_remote_copy(src_ref=ins[w].at[2 * k + (1 - c)], dst_ref=outs[w].at[k], send_sem=send.at[NCHIP * w + k],
                                               recv_sem=recv.at[NCHIP * w + k], device_id=sib, device_id_type=MESH)
                  for w in range(n) for k in range(NCHIP)]
        for cp in copies:
            cp.start()
        for cp in copies:
            cp.wait_recv()
        for cp in copies:
            cp.wait_send()

    return pl.kernel(
        body, out_type=tuple(jax.ShapeDtypeStruct((NCHIP,) + g.shape[1:], g.dtype) for g in grads),
        mesh=plsc.ScalarSubcoreMesh(axis_name="sequencer", num_cores=1), name=name,
        scratch_types=(pltpu.SemaphoreType.DMA((NCHIP * n,)), pltpu.SemaphoreType.DMA((NCHIP * n,))),
        compiler_params=pltpu.CompilerParams(collective_id=collective_id),
    )(*grads, *after)


def _pair_add(grad, sib_part, name, after=()):
    _, R, C = grad.shape
    tr = _tile(R, 1024, 16)
    core = jnp.reshape(lax.axis_index("c"), (1,)).astype(jnp.int32)

    def body(c_ref, g_ref, s_ref, *rest):
        rest[-1][...] = (g_ref[...].astype(f32) + s_ref[...].astype(f32)).astype(bf16)

    blk = pl.BlockSpec((None, tr, C), lambda k, i, c_ref: (k, i, 0))
    return pl.pallas_call(
        body, name=name,
        grid_spec=pltpu.PrefetchScalarGridSpec(
            num_scalar_prefetch=1, grid=(NCHIP, R // tr),
            in_specs=[pl.BlockSpec((None, tr, C), lambda k, i, c_ref: (2 * k + c_ref[0], i, 0)), blk]
            + [pl.BlockSpec(memory_space=pl.ANY)] * len(after), out_specs=blk),
        out_shape=jax.ShapeDtypeStruct((NCHIP, R, C), bf16), compiler_params=_params(("parallel", "parallel")),
    )(core, grad, sib_part, *after)


def _seq_chip_exchange(sums, name, collective_id, after=()):
    n, na = len(sums), len(after)

    def body(*refs):
        ins, outs = refs[:n], refs[n + na:2 * n + na]
        send, recv, loc = refs[2 * n + na:]
        x, y, c, me = _position()
        chips = _chips(x, y)
        mine = 2 * x + y
        barrier = pltpu.get_barrier_semaphore()
        for px, py in chips:
            pl.semaphore_signal(barrier, inc=1, device_id=(px, py, c), device_id_type=MESH)
        pl.semaphore_wait(barrier, 3)
        local = [pltpu.make_async_copy(ins[w].at[mine], outs[w].at[mine], loc.at[w]) for w in range(n)]
        for cp in local:
            cp.start()
        sends, waits = [], []
        for j, (px, py) in enumerate(chips):
            for w in range(n):
                sems = dict(send_sem=send.at[3 * w + j], recv_sem=recv.at[3 * w + j], device_id=(px, py, c), device_id_type=MESH)
                sends.append(pltpu.make_async_remote_copy(src_ref=ins[w].at[2 * px + py], dst_ref=outs[w].at[mine], **sems))
                waits.append(pltpu.make_async_remote_copy(src_ref=ins[w].at[2 * px + py], dst_ref=outs[w].at[2 * px + py], **sems))
        for cp in sends:
            cp.start()
        for cp in waits:
            cp.wait_recv()
        for cp in sends:
            cp.wait_send()
        for cp in local:
            cp.wait()

    return pl.kernel(
        body, out_type=tuple(jax.ShapeDtypeStruct(s.shape, s.dtype) for s in sums),
        mesh=plsc.ScalarSubcoreMesh(axis_name="sequencer", num_cores=1), name=name,
        scratch_types=(pltpu.SemaphoreType.DMA((3 * n,)), pltpu.SemaphoreType.DMA((3 * n,)), pltpu.SemaphoreType.DMA((n,))),
        compiler_params=pltpu.CompilerParams(collective_id=collective_id),
    )(*sums, *after)


def _reduce_scatter(grads, tag, ids, after=(), add_after=()):
    sib_parts = _seq_pair_exchange(grads, "pair_exchange_" + tag, ids[0], after=after)
    sums = [_pair_add(g, s, "pair_add_%s%d" % (tag, i), after=add_after) for i, (g, s) in enumerate(zip(grads, sib_parts))]
    return _seq_chip_exchange(sums, "chip_exchange_" + tag, ids[1]), sums


def _small_all_reduce(v, name):
    R, C = v.shape

    def body(v_ref, o_ref, buf, send, recv):
        x, y, c, me = _position()
        buf[me] = v_ref[...]
        sends, waits = [], []
        for r in range(1, NDEV):
            px, py, pc = _flip(x, r & 4), _flip(y, r & 2), _flip(c, r & 1)
            peer = 4 * px + 2 * py + pc
            sends.append(pltpu.make_async_remote_copy(src_ref=v_ref, dst_ref=buf.at[me], send_sem=send.at[r - 1], recv_sem=recv.at[r - 1],
                                                      device_id=(px, py, pc), device_id_type=MESH))
            waits.append(pltpu.make_async_remote_copy(src_ref=v_ref, dst_ref=buf.at[peer], send_sem=send.at[r - 1], recv_sem=recv.at[r - 1],
                                                      device_id=(px, py, pc), device_id_type=MESH))
        for cp in sends:
            cp.start()
        for cp in waits:
            cp.wait_recv()
        for cp in sends:
            cp.wait_send()
        acc = buf[0]
        for i in range(1, NDEV):
            acc = acc + buf[i]
        o_ref[...] = acc

    vm = pl.BlockSpec(memory_space=pltpu.VMEM)
    return pl.pallas_call(
        body, name=name, in_specs=[vm], out_specs=vm, out_shape=jax.ShapeDtypeStruct((R, C), f32),
        scratch_shapes=[pltpu.VMEM((NDEV, R, C), f32), pltpu.SemaphoreType.DMA((7,)), pltpu.SemaphoreType.DMA((7,))],
    )(v)


def _adamw_math(w, g, m, v):
    m = ADAM_B1 * m + (1.0 - ADAM_B1) * g
    v = ADAM_B2 * v + (1.0 - ADAM_B2) * (g * g)
    m_hat = m / (1.0 - ADAM_B1 ** ADAM_STEP)
    v_hat = v / (1.0 - ADAM_B2 ** ADAM_STEP)
    delta = -ADAM_LR * (m_hat / (jnp.sqrt(v_hat) + ADAM_EPS) + ADAM_WD * w)
    return delta, m, v


def _adamw_parts(w, m, v, parts, name, after=()):
    R, C = w.shape
    tr = _tile(R, 128, 16)
    blk = pl.BlockSpec((tr, C), lambda i: (i, 0))

    def body(w_ref, m_ref, v_ref, p_ref, *rest):
        g_ref, d_ref, mo_ref, vo_ref = rest[len(after):]
        g = p_ref[0].astype(f32)
        for i in range(1, NCHIP):
            g = g + p_ref[i].astype(f32)
        d, mn, vn = _adamw_math(w_ref[...], g, m_ref[...], v_ref[...])
        g_ref[...] = g
        d_ref[...] = d
        mo_ref[...] = mn
        vo_ref[...] = vn

    shp = jax.ShapeDtypeStruct((R, C), f32)
    return pl.pallas_call(
        body, name=name, grid=(R // tr,),
        in_specs=[blk, blk, blk, pl.BlockSpec((NCHIP, tr, C), lambda i: (0, i, 0))] + [pl.BlockSpec(memory_space=pl.ANY)] * len(after),
        out_specs=(blk, blk, blk, blk), out_shape=(shp, shp, shp, shp), compiler_params=_params(("parallel",)),
    )(w, m, v, parts, *after)


def _adamw_small(w, g, m, v, name):
    def body(w_ref, g_ref, m_ref, v_ref, d_ref, mo_ref, vo_ref):
        d, mn, vn = _adamw_math(w_ref[...], g_ref[...], m_ref[...], v_ref[...])
        d_ref[...] = d
        mo_ref[...] = mn
        vo_ref[...] = vn

    shp = jax.ShapeDtypeStruct(w.shape, f32)
    return pl.pallas_call(body, name=name, out_shape=(shp, shp, shp))(w, g, m, v)


SMALL_COLS = 1024


def _pack(arrs):
    flat = jnp.concatenate([a.reshape(-1) for a in arrs])
    rows = -(-flat.shape[0] // (8 * SMALL_COLS)) * 8
    return jnp.pad(flat, (0, rows * SMALL_COLS - flat.shape[0])).reshape(rows, SMALL_COLS)


def _unpack(packed, like):
    flat = packed.reshape(-1)
    out, pos = [], 0
    for a in like:
        out.append(flat[pos:pos + a.size].reshape(a.shape))
        pos += a.size
    return out


def kernel(x, w_in, b_gate, norm_mix, norm_ffn, hgrn_lb_logits, hgrn_out_gain, q_gain, k_gain, rel_bias, w_proj_a, w_proj_b, w_out, w_ffn_in, w_ffn_out, loss_target, m_w_in, m_b_gate, m_norm_mix, m_norm_ffn, m_hgrn_lb_logits, m_hgrn_out_gain, m_q_gain, m_k_gain, m_rel_bias, m_w_proj_a, m_w_proj_b, m_w_out, m_w_ffn_in, m_w_ffn_out, v_w_in, v_b_gate, v_norm_mix, v_norm_ffn, v_hgrn_lb_logits, v_hgrn_out_gain, v_q_gain, v_k_gain, v_rel_bias, v_w_proj_a, v_w_proj_b, v_w_out, v_w_ffn_in, v_w_ffn_out):
    xs = x[0]
    target = loss_target[0]
    T, D = xs.shape
    d_a = hgrn_out_gain.shape[-1]
    H = d_a // HEAD
    d_b = d_a
    off_b = 4 * d_a
    off_g = off_b + 3 * d_b
    assert rel_bias.shape[1] == H and T % CHUNK == 0 and T // CHUNK > N_PAST

    big_w = [w_in[0], w_proj_a[0], w_proj_b[0], w_out[0], w_ffn_in[0], w_ffn_out[0]]
    big_m = [m_w_in[0], m_w_proj_a[0], m_w_proj_b[0], m_w_out[0], m_w_ffn_in[0], m_w_ffn_out[0]]
    big_v = [v_w_in[0], v_w_proj_a[0], v_w_proj_b[0], v_w_out[0], v_w_ffn_in[0], v_w_ffn_out[0]]

    sh = [w.astype(bf16) for w in big_w]
    (g_in,) = _seq_gather(sh[0:1], "gather_a", 1)
    g_pa, g_pb, g_out = _seq_gather(sh[1:4], "gather_b", 2)
    (g_fin,) = _seq_gather(sh[4:5], "gather_c", 3)
    (g_fout,) = _seq_gather(sh[5:6], "gather_d", 4)

    h = _rms_fwd(xs, norm_mix, "rms_mix")
    proj = _mm(h, g_in, mode="nn", b_blocked=True, name="mm_proj")
    y_a, states = _hgrn_fwd(proj, hgrn_lb_logits, hgrn_out_gain, H, "hgrn_fwd")
    idx = jnp.asarray(_rel_index())
    rb_pad = jnp.pad(rel_bias[0], ((0, 0), (0, N_REL_PAD - N_REL)))
    bias = _bias_table(rb_pad, idx, "bias_table").reshape(H, CHUNK, BAND * CHUNK)
    y_b = _attn_fwd(proj, q_gain, k_gain, bias, off_b, H, "attn_fwd")
    wg_out = g_out.reshape(-1, g_out.shape[-1])
    wg_fout = g_fout.reshape(-1, g_fout.shape[-1])
    pa, pb, merged = _proj_merge(y_a, y_b, g_pa, g_pb, proj, b_gate, off_g, "mm_proj_ab")
    x1, h2 = _out_rms(merged, wg_out, xs, norm_ffn, "mm_out")
    gu, act = _ffn_in_swiglu(h2, g_fin, "mm_ffn_in")
    dy, loss_acc = _ffn_out_loss(act, wg_fout, x1, target, "mm_ffn_out")
    loss_part = loss_acc[0:1, 0:1] * (0.5 / D)

    gw_fout = _mm(act, dy, mode="tn", out_dtype=bf16, tm=1408, name="mm_gw_ffn_out")
    dgu = _d_act_swiglu(dy, wg_fout, gu, "mm_d_act")
    gw_fin = _mm(h2, dgu, mode="tn", b_stacked=True, out_blocked=True, out_dtype=bf16, tn=g_fin.shape[-1], name="mm_gw_ffn_in")
    dh2 = _mm_nt_blocked(dgu, g_fin, "mm_d_h2", a_stacked=True)
    (p_fout, p_fin), sums_a = _reduce_scatter([gw_fout.reshape(NDEV, -1, D), gw_fin], "a", (5, 6), add_after=(dh2,))
    dx1, g_norm_ffn = _rms_bwd(x1, norm_ffn, dh2, dy, "rms_ffn_bwd", after=sums_a)

    gw_out = _mm(merged, dx1, mode="tn", out_dtype=bf16, name="mm_gw_out")
    dpa, dpb, dgl, g_b_gate = _d_merged_branches(dx1, wg_out, pa, pb, proj, b_gate, off_g, "mm_d_merged")
    dy_a = _mm(dpa, g_pa, mode="nt", b_blocked=True, tm=2048, name="mm_d_ya")
    dy_b = _mm(dpb, g_pb, mode="nt", b_blocked=True, tm=2048, name="mm_d_yb")
    gw_pa = _mm(y_a, dpa, mode="tn", out_blocked=True, out_dtype=bf16, tn=g_pa.shape[-1], name="mm_gw_proj_a")
    gw_pb = _mm(y_b, dpb, mode="tn", out_blocked=True, out_dtype=bf16, tn=g_pb.shape[-1], name="mm_gw_proj_b")

    dq_a, df_a, di_a, dg_a, g_logits, g_gain = _hgrn_bwd(proj, hgrn_lb_logits, hgrn_out_gain, states, dy_a, H, "hgrn_bwd")
    dq_b, dk_b, dv_b, dbias, g_qg, g_kg = _attn_bwd(proj, q_gain, k_gain, bias, dy_b, off_b, H, "attn_bwd")
    g_rel_pad = _bias_table_bwd(dbias.reshape(H, -1), idx, "bias_table_bwd")
    g_rel = g_rel_pad[:, :N_REL]
    dproj = jnp.concatenate([dq_a, df_a, di_a, dg_a, dq_b, dk_b, dv_b, dgl[0], dgl[1]], axis=1)
    (p_out, p_pa, p_pb), sums_b = _reduce_scatter([gw_out.reshape(NDEV, -1, D), gw_pa, gw_pb], "b", (7, 8), after=(g_gain, p_fout, p_fin), add_after=(dy_b,))
    gw_in = _mm(h, dproj, mode="tn", out_blocked=True, out_dtype=bf16, tn=g_in.shape[-1], name="mm_gw_in", after=sums_b)
    upd_fout = _adamw_parts(big_w[5], big_m[5], big_v[5], p_fout, "adamw_w_ffn_out", after=(gw_in,))
    (p_in,), sums_c = _reduce_scatter([gw_in], "c", (9, 10), after=(p_out, p_pa, p_pb), add_after=(g_rel_pad, upd_fout[0]))
    dh = _mm_nt_blocked(dproj, g_in, "mm_d_h", after=sums_c)
    grad_x, g_norm_mix = _rms_bwd(xs, norm_mix, dh, dx1, "rms_mix_bwd")

    parts = [p_in, p_pa, p_pb, p_out, p_fin, p_fout]
    names = ["w_in", "w_proj_a", "w_proj_b", "w_out", "w_ffn_in", "w_ffn_out"]
    big = {}
    for nm, w, m, v, p in zip(names, big_w, big_m, big_v, parts):
        upd = upd_fout if nm == "w_ffn_out" else _adamw_parts(w, m, v, p, "adamw_" + nm, after=() if nm == "w_in" else (g_norm_mix,))
        big[nm] = [o[None] for o in upd]

    small_names = ["b_gate", "norm_mix", "norm_ffn", "hgrn_lb_logits", "hgrn_out_gain", "q_gain", "k_gain", "rel_bias"]
    small_w = [b_gate, norm_mix, norm_ffn, hgrn_lb_logits, hgrn_out_gain, q_gain, k_gain, rel_bias]
    small_m = [m_b_gate, m_norm_mix, m_norm_ffn, m_hgrn_lb_logits, m_hgrn_out_gain, m_q_gain, m_k_gain, m_rel_bias]
    small_v = [v_b_gate, v_norm_mix, v_norm_ffn, v_hgrn_lb_logits, v_hgrn_out_gain, v_q_gain, v_k_gain, v_rel_bias]
    small_g = [g_b_gate.reshape(1, -1), g_norm_mix, g_norm_ffn, g_logits, g_gain, g_qg, g_kg, g_rel[None], loss_part]
    g_sum = _small_all_reduce(_pack(small_g), "reduce_small")
    loss = _unpack(g_sum, small_g)[-1].reshape(())
    d_s, m_s, v_s = _adamw_small(_pack(small_w), g_sum, _pack(small_m), _pack(small_v), "adamw_small")
    small = {}
    for nm, g, d, m, v in zip(small_names, _unpack(g_sum, small_w), _unpack(d_s, small_w), _unpack(m_s, small_w), _unpack(v_s, small_w)):
        small[nm] = [g, d, m, v]

    order = ["w_in", "b_gate", "norm_mix", "norm_ffn", "hgrn_lb_logits", "hgrn_out_gain", "q_gain", "k_gain", "rel_bias",
             "w_proj_a", "w_proj_b", "w_out", "w_ffn_in", "w_ffn_out"]
    res = {**big, **small}
    outs = [loss, grad_x[None]]
    for k in range(4):
        outs += [res[nm][k] for nm in order]
    return tuple(outs)
```

```python
import functools

import numpy as np
import jax
import jax.numpy as jnp
from jax import lax
from jax.experimental import pallas as pl
from jax.experimental.pallas import tpu as pltpu
from jax.experimental.pallas import tpu_sc as plsc

f32 = jnp.float32
bf16 = jnp.bfloat16
MESH = pl.DeviceIdType.MESH
AXES = ("x", "y", "c")
NDEV = 8

CHUNK = 64
HEAD = 128
SUB = 8
HGRN_BWD_HEADS = 8
ATTN_CHUNKS = 16
N_PAST = 8
BAND = N_PAST + 1
PAD = N_PAST * CHUNK
REL_FUTURE = CHUNK - 1
REL_PAST = 2 * CHUNK - 1
N_REL = REL_FUTURE + REL_PAST + 1
N_REL_PAD = 256
EPS = 1e-6
NEG = -1e30

ADAM_LR = 0.001
ADAM_B1 = 0.9
ADAM_B2 = 0.999
ADAM_EPS = 1e-08
ADAM_WD = 0.01
ADAM_STEP = 10

VMEM_LIMIT = 56 * 1024 * 1024


def _params(sem=None):
    return pltpu.CompilerParams(dimension_semantics=sem, vmem_limit_bytes=VMEM_LIMIT)


def _tile(n, pref, unit=128):
    if n <= pref:
        return n
    t = (pref // unit) * unit
    while t >= unit:
        if n % t == 0:
            return t
        t -= unit
    return n


_sigmoid = jax.nn.sigmoid


def _mm(a, b, *, mode, name, b_blocked=False, out_blocked=False, out_dtype=f32, tm=1024, tn=1024, tk=2048, after=(),
        b_stacked=False):
    if mode == "tn":
        K, M = a.shape
    else:
        M, K = a.shape
    if b_blocked:
        nb, mid, cb = b.shape
        if mode == "nn":
            assert mid == K
            N, tn = nb * cb, cb
        else:
            assert mode == "nt" and nb * cb == K
            N, tk = mid, cb
    elif b_stacked:
        assert mode == "tn"
        N = 2 * b.shape[2]
    else:
        N = b.shape[1] if mode in ("nn", "tn") else b.shape[0]
    tm = _tile(M, tm)
    tn = tn if (b_blocked and mode == "nn") or out_blocked else _tile(N, tn)
    tk = tk if b_blocked and mode == "nt" else _tile(K, tk)
    assert M % tm == 0 and N % tn == 0 and K % tk == 0
    nk = K // tk
    grid = (M // tm, N // tn, nk)
    if mode == "tn":
        a_spec = pl.BlockSpec((tk, tm), lambda i, j, k: (k, i))
    else:
        a_spec = pl.BlockSpec((tm, tk), lambda i, j, k: (i, k))
    if mode == "nn":
        b_spec = pl.BlockSpec((None, tk, cb), lambda i, j, k: (j, k, 0)) if b_blocked else pl.BlockSpec((tk, tn), lambda i, j, k: (k, j))
    elif mode == "nt":
        b_spec = pl.BlockSpec((None, tn, cb), lambda i, j, k: (k, j, 0)) if b_blocked else pl.BlockSpec((tn, tk), lambda i, j, k: (j, k))
    elif b_stacked:
        nh = N // 2 // tn
        b_spec = pl.BlockSpec((None, tk, tn), lambda i, j, k: (j // nh, k, j % nh))
    else:
        b_spec = pl.BlockSpec((tk, tn), lambda i, j, k: (k, j))
    if out_blocked:
        out_shape = jax.ShapeDtypeStruct((N // tn, M, tn), out_dtype)
        o_spec = pl.BlockSpec((None, tm, tn), lambda i, j, k: (j, i, 0))
    else:
        out_shape = jax.ShapeDtypeStruct((M, N), out_dtype)
        o_spec = pl.BlockSpec((tm, tn), lambda i, j, k: (i, j))
    dims = {"nn": ((1,), (0,)), "nt": ((1,), (1,)), "tn": ((0,), (0,))}[mode]

    def body(a_ref, b_ref, *rest):
        o_ref, acc = rest[len(after)], rest[len(after) + 1:]
        p = lax.dot_general(a_ref[...].astype(bf16), b_ref[...].astype(bf16), (dims, ((), ())), preferred_element_type=f32)
        if nk == 1:
            o_ref[...] = p.astype(out_dtype)
        else:
            acc_ref = acc[0]
            k = pl.program_id(2)

            @pl.when(k == 0)
            def _():
                acc_ref[...] = p

            @pl.when(k > 0)
            def _():
                acc_ref[...] += p

            @pl.when(k == nk - 1)
            def _():
                o_ref[...] = acc_ref[...].astype(out_dtype)

    return pl.pallas_call(
        body, name=name, grid=grid, in_specs=[a_spec, b_spec] + [pl.BlockSpec(memory_space=pl.ANY)] * len(after), out_specs=o_spec,
        out_shape=out_shape, scratch_shapes=[pltpu.VMEM((tm, tn), f32)] if nk > 1 else [],
        compiler_params=_params(("parallel", "parallel", "arbitrary")),
    )(a, b, *after)


def _mm_nt_blocked(a, b, name, *, a_stacked=False, after=(), tm=1024, tn=1024, kb=2):
    nb, N, cb = b.shape
    M = a.shape[1] if a_stacked else a.shape[0]
    tm, tn = _tile(M, tm, 8), _tile(N, tn)
    nk = nb // kb
    per_half = nk // 2

    def body(a_ref, b_ref, *rest):
        o_ref, acc_ref = rest[len(after)], rest[len(after) + 1]
        dn = (((1,), (1,)), ((), ()))
        k = pl.program_id(2)
        p = None
        for q in range(kb):
            d = lax.dot_general(a_ref[:, q * cb:(q + 1) * cb], b_ref[q], dn, preferred_element_type=f32)
            p = d if p is None else p + d

        @pl.when(k == 0)
        def _():
            acc_ref[...] = p

        @pl.when(jnp.logical_and(k > 0, k < nk - 1))
        def _():
            acc_ref[...] += p

        @pl.when(k == nk - 1)
        def _():
            o_ref[...] = acc_ref[...] + p

    if a_stacked:
        a_spec = pl.BlockSpec((None, tm, kb * cb), lambda i, j, k: (k // per_half, i, k % per_half))
    else:
        a_spec = pl.BlockSpec((tm, kb * cb), lambda i, j, k: (i, k))
    return pl.pallas_call(
        body, name=name, grid=(M // tm, N // tn, nk),
        in_specs=[a_spec, pl.BlockSpec((kb, tn, cb), lambda i, j, k: (k, j, 0))] + [pl.BlockSpec(memory_space=pl.ANY)] * len(after),
        out_specs=pl.BlockSpec((tm, tn), lambda i, j, k: (i, j)), out_shape=jax.ShapeDtypeStruct((M, N), f32),
        scratch_shapes=[pltpu.VMEM((tm, tn), f32)],
        compiler_params=_params(("parallel", "parallel", "arbitrary")),
    )(a, b, *after)


def _rms_fwd(x, gain, name):
    T, D = x.shape
    tr = _tile(T, 256, 8)
    row = pl.BlockSpec((tr, D), lambda i: (i, 0))

    def body(x_ref, g_ref, h_ref):
        xs = x_ref[...]
        r = lax.rsqrt(jnp.mean(xs * xs, axis=-1, keepdims=True) + EPS)
        h_ref[...] = (xs * r * g_ref[...]).astype(bf16)

    return pl.pallas_call(body, name=name, grid=(T // tr,), in_specs=[row, pl.BlockSpec((1, D), lambda i: (0, 0))], out_specs=row,
                          out_shape=jax.ShapeDtypeStruct((T, D), bf16), compiler_params=_params(("parallel",)))(x, gain)


def _rms_bwd(xs, gain, dh, extra, name, after=()):
    T, D = xs.shape
    tr = _tile(T, 256, 8)
    row = pl.BlockSpec((tr, D), lambda i: (i, 0))
    vec = pl.BlockSpec((1, D), lambda i: (0, 0))

    def body(x_ref, g_ref, dh_ref, e_ref, *rest):
        dx_ref, dg_ref = rest[len(after):]
        x = x_ref[...]
        r = lax.rsqrt(jnp.mean(x * x, axis=-1, keepdims=True) + EPS)
        xhat = x * r
        dh_v = dh_ref[...]
        gd = dh_v * g_ref[...]
        dx_ref[...] = e_ref[...] + r * (gd - xhat * jnp.mean(gd * xhat, axis=-1, keepdims=True))
        part = jnp.sum(dh_v * xhat, axis=0, keepdims=True)

        @pl.when(pl.program_id(0) == 0)
        def _():
            dg_ref[...] = part

        @pl.when(pl.program_id(0) > 0)
        def _():
            dg_ref[...] += part

    return pl.pallas_call(body, name=name, grid=(T // tr,),
                          in_specs=[row, vec, row, row] + [pl.BlockSpec(memory_space=pl.ANY)] * len(after), out_specs=(row, vec),
                          out_shape=(jax.ShapeDtypeStruct((T, D), f32), jax.ShapeDtypeStruct((1, D), f32)),
                          compiler_params=_params(("arbitrary",)))(xs, gain, dh, extra, *after)


def _proj_merge(y_a, y_b, w_a, w_b, proj, b_gate, off, name):
    T, K = y_a.shape
    nb, _, cb = w_a.shape
    D = nb * cb
    tm = _tile(T, 1024, 8)
    oa, ob = off // cb, (off + D) // cb
    blk = pl.BlockSpec((tm, cb), lambda j, i: (i, j))
    row = pl.BlockSpec((tm, K), lambda j, i: (i, 0))
    wsp = pl.BlockSpec((None, K, cb), lambda j, i: (j, 0, 0))

    def body(ya_ref, yb_ref, wa_ref, wb_ref, ga_ref, gb_ref, ba_ref, bb_ref, pa_ref, pb_ref, m_ref):
        dn = (((1,), (0,)), ((), ()))
        pa = lax.dot_general(ya_ref[...], wa_ref[...], dn, preferred_element_type=f32)
        pb = lax.dot_general(yb_ref[...], wb_ref[...], dn, preferred_element_type=f32)
        pa_ref[...] = pa
        pb_ref[...] = pb
        m_ref[...] = (_sigmoid(ga_ref[...] + ba_ref[...]) * pa + _sigmoid(gb_ref[...] + bb_ref[...]) * pb).astype(bf16)

    return pl.pallas_call(
        body, name=name, grid=(nb, T // tm),
        in_specs=[row, row, wsp, wsp, pl.BlockSpec((tm, cb), lambda j, i: (i, oa + j)), pl.BlockSpec((tm, cb), lambda j, i: (i, ob + j)),
                  pl.BlockSpec((1, cb), lambda j, i: (0, j)), pl.BlockSpec((1, cb), lambda j, i: (0, nb + j))],
        out_specs=(blk, blk, blk),
        out_shape=(jax.ShapeDtypeStruct((T, D), f32), jax.ShapeDtypeStruct((T, D), f32), jax.ShapeDtypeStruct((T, D), bf16)),
        compiler_params=_params(("parallel", "parallel")),
    )(y_a, y_b, w_a, w_b, proj, proj, b_gate, b_gate)


def _out_rms(merged, w_out, x, gain, name):
    T, K = merged.shape
    D = w_out.shape[1]
    tm = _tile(T, 256, 8)
    row = pl.BlockSpec((tm, D), lambda i: (i, 0))

    def body(m_ref, w_ref, x_ref, g_ref, x1_ref, h_ref):
        x1 = x_ref[...] + lax.dot_general(m_ref[...], w_ref[...], (((1,), (0,)), ((), ())), preferred_element_type=f32)
        x1_ref[...] = x1
        r = lax.rsqrt(jnp.mean(x1 * x1, axis=-1, keepdims=True) + EPS)
        h_ref[...] = (x1 * r * g_ref[...]).astype(bf16)

    return pl.pallas_call(
        body, name=name, grid=(T // tm,),
        in_specs=[pl.BlockSpec((tm, K), lambda i: (i, 0)), pl.BlockSpec((K, D), lambda i: (0, 0)), row, pl.BlockSpec((1, D), lambda i: (0, 0))],
        out_specs=(row, row), out_shape=(jax.ShapeDtypeStruct((T, D), f32), jax.ShapeDtypeStruct((T, D), bf16)),
        compiler_params=_params(("parallel",)),
    )(merged, w_out, x, gain)


def _d_merged_branches(dx, w_out, pa, pb, proj, b_gate, off, name):
    T, D = pa.shape
    tm, tn = _tile(T, 512, 8), _tile(D, 1024)
    oa, ob, nb = off // tn, (off + D) // tn, D // tn
    blk = pl.BlockSpec((tm, tn), lambda j, i: (i, j))

    def body(dx_ref, w_ref, pa_ref, pb_ref, ga_ref, gb_ref, ba_ref, bb_ref, dpa_ref, dpb_ref, dgl_ref, db_ref):
        dm = lax.dot_general(dx_ref[...].astype(bf16), w_ref[...], (((1,), (1,)), ((), ())), preferred_element_type=f32)
        sums = []
        for p_ref, gl_ref, b_ref, dp_ref, k in ((pa_ref, ga_ref, ba_ref, dpa_ref, 0), (pb_ref, gb_ref, bb_ref, dpb_ref, 1)):
            g = _sigmoid(gl_ref[...] + b_ref[...])
            dp_ref[...] = (dm * g).astype(bf16)
            dgl = dm * p_ref[...] * g * (1.0 - g)
            dgl_ref[k] = dgl.astype(bf16)
            sums.append(jnp.sum(dgl, axis=0, keepdims=True))

        @pl.when(pl.program_id(1) == 0)
        def _():
            db_ref[...] = jnp.zeros((2, 1, tn), f32)

        db_ref[0] += sums[0]
        db_ref[1] += sums[1]

    return pl.pallas_call(
        body, name=name, grid=(nb, T // tm),
        in_specs=[pl.BlockSpec((tm, D), lambda j, i: (i, 0)), pl.BlockSpec((tn, D), lambda j, i: (j, 0)), blk, blk,
                  pl.BlockSpec((tm, tn), lambda j, i: (i, oa + j)), pl.BlockSpec((tm, tn), lambda j, i: (i, ob + j)),
                  pl.BlockSpec((1, tn), lambda j, i: (0, j)), pl.BlockSpec((1, tn), lambda j, i: (0, nb + j))],
        out_specs=(blk, blk, pl.BlockSpec((2, tm, tn), lambda j, i: (0, i, j)), pl.BlockSpec((2, 1, tn), lambda j, i: (0, 0, j))),
        out_shape=(jax.ShapeDtypeStruct((T, D), bf16), jax.ShapeDtypeStruct((T, D), bf16), jax.ShapeDtypeStruct((2, T, D), bf16),
                   jax.ShapeDtypeStruct((2, 1, D), f32)),
        compiler_params=_params(("parallel", "arbitrary")),
    )(dx, w_out, pa, pb, proj, proj, b_gate, b_gate)


def _ffn_in_swiglu(h, w, name):
    T, K = h.shape
    nb, _, cb = w.shape
    half = nb // 2
    F = half * cb
    tm = _tile(T, 512, 8)

    def body(h_ref, wg_ref, wu_ref, gu_ref, act_ref):
        dn = (((1,), (0,)), ((), ()))
        hv = h_ref[...]
        g = lax.dot_general(hv, wg_ref[...], dn, preferred_element_type=f32)
        u = lax.dot_general(hv, wu_ref[...], dn, preferred_element_type=f32)
        gu_ref[0] = g
        gu_ref[1] = u
        act_ref[...] = (g * _sigmoid(g) * u).astype(bf16)

    return pl.pallas_call(
        body, name=name, grid=(half, T // tm),
        in_specs=[pl.BlockSpec((tm, K), lambda j, i: (i, 0)), pl.BlockSpec((None, K, cb), lambda j, i: (j, 0, 0)),
                  pl.BlockSpec((None, K, cb), lambda j, i: (j + half, 0, 0))],
        out_specs=(pl.BlockSpec((2, tm, cb), lambda j, i: (0, i, j)), pl.BlockSpec((tm, cb), lambda j, i: (i, j))),
        out_shape=(jax.ShapeDtypeStruct((2, T, F), f32), jax.ShapeDtypeStruct((T, F), bf16)),
        compiler_params=_params(("parallel", "parallel")),
    )(h, w, w)


def _d_act_swiglu(dy, w_out, gu, name):
    T, D = dy.shape
    F = w_out.shape[0]
    tm, tn = _tile(T, 512, 8), _tile(F, 1408)

    def body(dy_ref, w_ref, gu_ref, o_ref):
        halves = [pl.ds(r * (tm // 2), tm // 2) for r in range(2)]
        wv = w_ref[...]
        d = [lax.dot_general(dy_ref[r, :].astype(bf16), wv, (((1,), (1,)), ((), ())), preferred_element_type=f32) for r in halves]
        for r, dr in zip(halves, d):
            g = gu_ref[0, r, :]
            s = _sigmoid(g)
            o_ref[0, r, :] = (dr * gu_ref[1, r, :] * s * (1.0 + g * (1.0 - s))).astype(bf16)
            o_ref[1, r, :] = (dr * g * s).astype(bf16)

    blk = pl.BlockSpec((2, tm, tn), lambda j, i: (0, i, j))
    return pl.pallas_call(
        body, name=name, grid=(F // tn, T // tm),
        in_specs=[pl.BlockSpec((tm, D), lambda j, i: (i, 0)), pl.BlockSpec((tn, D), lambda j, i: (j, 0)), blk],
        out_specs=blk, out_shape=jax.ShapeDtypeStruct((2, T, F), bf16), compiler_params=_params(("parallel", "parallel")),
    )(dy, w_out, gu)


def _ffn_out_loss(act, w, x1, target, name):
    T, F = act.shape
    D = w.shape[1]
    tm, tn, tk = _tile(T, 512, 8), _tile(D, 1024), _tile(F, 2816)
    nk = F // tk
    blk = pl.BlockSpec((tm, tn), lambda i, j, k: (i, j))

    def body(a_ref, w_ref, x_ref, t_ref, dy_ref, l_ref, acc_ref):
        i, j, k = pl.program_id(0), pl.program_id(1), pl.program_id(2)
        p = lax.dot_general(a_ref[...], w_ref[...], (((1,), (0,)), ((), ())), preferred_element_type=f32)

        @pl.when(jnp.logical_and(jnp.logical_and(i == 0, j == 0), k == 0))
        def _():
            l_ref[...] = jnp.zeros((8, 128), f32)

        @pl.when(k == 0)
        def _():
            acc_ref[...] = p

        @pl.when(k > 0)
        def _():
            acc_ref[...] += p

        @pl.when(k == nk - 1)
        def _():
            d = acc_ref[...] + x_ref[...] - t_ref[...]
            dy_ref[...] = d * (1.0 / D)
            l_ref[...] += jnp.sum(jnp.sum(d * d, axis=1, keepdims=True), axis=0, keepdims=True)

    return pl.pallas_call(
        body, name=name, grid=(T // tm, D // tn, nk),
        in_specs=[pl.BlockSpec((tm, tk), lambda i, j, k: (i, k)), pl.BlockSpec((tk, tn), lambda i, j, k: (k, j)), blk, blk],
        out_specs=(blk, pl.BlockSpec((8, 128), lambda i, j, k: (0, 0))),
        out_shape=(jax.ShapeDtypeStruct((T, D), f32), jax.ShapeDtypeStruct((8, 128), f32)),
        scratch_shapes=[pltpu.VMEM((tm, tn), f32)],
        compiler_params=_params(("arbitrary", "arbitrary", "arbitrary")),
    )(act, w, x1, target)


_DIMS = {"nn": ((1,), (0,)), "nt": ((1,), (1,)), "tn": ((0,), (0,))}
_MODE = {v: k for k, v in _DIMS.items()}


def _dot_bf16(a, b, mode):
    return lax.dot_general(a.astype(bf16), b.astype(bf16), (_DIMS[mode], ((), ())), preferred_element_type=f32)


@functools.partial(jax.custom_vjp, nondiff_argnums=(2,))
def _dotm(a, b, mode):
    return _dot_bf16(a, b, mode)


def _dotm_fwd(a, b, mode):
    return _dot_bf16(a, b, mode), (a, b)


def _dotm_bwd(mode, res, g):
    a, b = res
    if mode == "nn":
        return _dot_bf16(g, b, "nt"), _dot_bf16(a, g, "tn")
    if mode == "nt":
        return _dot_bf16(g, b, "nn"), _dot_bf16(g, a, "tn")
    return _dot_bf16(b, g, "nt"), _dot_bf16(a, g, "nn")


_dotm.defvjp(_dotm_fwd, _dotm_bwd)


def _dotb(a, b, dims):
    return _dotm(a, b, _MODE[dims])


def _split3(v):
    def top(t):
        return lax.bitcast_convert_type(lax.bitcast_convert_type(t, jnp.uint32) & jnp.uint32(0xFFFF0000), f32)

    hi = top(v)
    mid = top(v - hi)
    low = (v - hi) - mid
    return hi.astype(bf16), mid.astype(bf16), low.astype(bf16)


def _dot3(v, m, dims, v_first):
    m = m.astype(bf16)
    dn = (dims, ((), ()))
    parts = [lax.dot_general(p, m, dn, preferred_element_type=f32) if v_first else lax.dot_general(m, p, dn, preferred_element_type=f32)
             for p in _split3(v)]
    return parts[0] + parts[1] + parts[2]


def _triangle_sum(v, lower):
    row = lax.broadcasted_iota(jnp.int32, (CHUNK, CHUNK), 0)
    col = lax.broadcasted_iota(jnp.int32, (CHUNK, CHUNK), 1)
    return _dot3(v, (col <= row) if lower else (col >= row), ((1,), (0,)), False)


@jax.custom_vjp
def _cumsum_rows(v):
    return _triangle_sum(v, True)


_cumsum_rows.defvjp(lambda v: (_triangle_sum(v, True), None), lambda _, g: (_triangle_sum(g, False),))


def _hgrn_heads(q, fl, iv, g, logits, gain, st):
    r = range(len(q))
    lb = [jax.nn.softmax(logits[j], axis=0)[0:1] for j in r]
    f = [lb[j] + (1.0 - lb[j]) * _sigmoid(fl[j]) for j in r]
    lf = [jnp.log(f[j]) for j in r]
    kk = [1.0 - f[j] for j in r]
    qs = [q[j] * _sigmoid(q[j]) for j in r]
    b = [_cumsum_rows(lf[j]) for j in r]
    b_last = [jnp.sum(lf[j], axis=0, keepdims=True) for j in r]
    o = [_dotb(qs[j] * jnp.exp(b[j]), st[j], ((1,), (1,))) for j in r]
    r3 = lax.broadcasted_iota(jnp.int32, (SUB, SUB, HEAD), 0)
    c3 = lax.broadcasted_iota(jnp.int32, (SUB, SUB, HEAD), 1)
    parts = [[] for _ in r]
    for i in range(CHUNK // SUB):
        lo, hi = i * SUB, (i + 1) * SUB
        bi = [b[j][lo:hi] for j in r]
        dec = [jnp.exp(jnp.where(c3 <= r3, bi[j][:, None, :] - bi[j][None, :, :], -jnp.inf)) for j in r]
        s = [jnp.sum(qs[j][lo:hi][:, None, :] * kk[j][lo:hi][None, :, :] * dec[j], axis=-1) for j in r]
        if i > 0:
            anchor = [jnp.max(bi[j], axis=0, keepdims=True) for j in r]
            qa = [qs[j][lo:hi] * jnp.exp(bi[j] - anchor[j]) for j in r]
            kd = [kk[j][:lo] * jnp.exp(anchor[j] - b[j][:lo]) for j in r]
            s = [jnp.concatenate([_dotb(qa[j], kd[j], ((1,), (1,))), s[j]], axis=1) for j in r]
        for j in r:
            parts[j].append(_dotb(s[j], iv[j][:hi], ((1,), (0,))))
    o = [o[j] + jnp.concatenate(parts[j], axis=0) for j in r]
    st_new = [st[j] * jnp.exp(b_last[j]) + _dotb(iv[j], kk[j] * jnp.exp(b_last[j] - b[j]), ((0,), (0,))) for j in r]
    o = [o[j] * lax.rsqrt(jnp.mean(o[j] * o[j], axis=-1, keepdims=True) + EPS) for j in r]
    o = [o[j] * gain[j] * (g[j] * _sigmoid(g[j])) for j in r]
    return o, st_new


def _group(n, pref):
    while n % pref:
        pref //= 2
    return pref


def _hgrn_fwd(proj, logits, gain, n_heads, name):
    T = proj.shape[0]
    nc = T // CHUNK
    H = n_heads
    HB = _group(H, 8)
    W = HB * HEAD

    def col(k):
        return pl.BlockSpec((CHUNK, W), lambda h, c: (c, k * (H // HB) + h))

    def body(q_ref, f_ref, i_ref, g_ref, l_ref, ga_ref, y_ref, s_ref, st):
        @pl.when(pl.program_id(1) == 0)
        def _():
            st[...] = jnp.zeros((HB, HEAD, HEAD), f32)

        cols = [slice(j * HEAD, (j + 1) * HEAD) for j in range(HB)]
        heads = lambda ref: [ref[:, cs] for cs in cols]
        s_ref[...] = st[...]
        o, st_new = _hgrn_heads(heads(q_ref), heads(f_ref), heads(i_ref), heads(g_ref), heads(l_ref), heads(ga_ref), [st[j] for j in range(HB)])
        for j, cs in enumerate(cols):
            y_ref[:, cs] = o[j].astype(bf16)
            st[j] = st_new[j]

    return pl.pallas_call(
        body, name=name, grid=(H // HB, nc),
        in_specs=[col(0), col(1), col(2), col(3), pl.BlockSpec((2, W), lambda h, c: (0, h)), pl.BlockSpec((1, W), lambda h, c: (0, h))],
        out_specs=(pl.BlockSpec((CHUNK, W), lambda h, c: (c, h)), pl.BlockSpec((HB, None, HEAD, HEAD), lambda h, c: (h, c, 0, 0))),
        out_shape=(jax.ShapeDtypeStruct((T, H * HEAD), bf16), jax.ShapeDtypeStruct((H, nc, HEAD, HEAD), f32)),
        scratch_shapes=[pltpu.VMEM((HB, HEAD, HEAD), f32)],
        compiler_params=_params(("parallel", "arbitrary")),
    )(proj, proj, proj, proj, logits, gain)


def _hgrn_bwd(proj, logits, gain, states, dy, n_heads, name):
    T = proj.shape[0]
    nc = T // CHUNK
    H = n_heads
    HB = _group(H, HGRN_BWD_HEADS)
    W = HB * HEAD

    def col(k):
        return pl.BlockSpec((CHUNK, W), lambda h, c: (nc - 1 - c, k * (H // HB) + h))

    out_blk = pl.BlockSpec((CHUNK, W), lambda h, c: (nc - 1 - c, h))

    def body(q_ref, f_ref, i_ref, g_ref, l_ref, ga_ref, s_ref, dy_ref, dq_ref, df_ref, di_ref, dg_ref, dl_ref, dga_ref, dst):
        first = pl.program_id(1) == 0

        @pl.when(first)
        def _():
            dst[...] = jnp.zeros((HB, HEAD, HEAD), f32)
            dl_ref[...] = jnp.zeros((2, W), f32)
            dga_ref[...] = jnp.zeros((1, W), f32)

        cols = [slice(j * HEAD, (j + 1) * HEAD) for j in range(HB)]
        heads = lambda ref: [ref[:, cs] for cs in cols]
        _, vjp = jax.vjp(_hgrn_heads, heads(q_ref), heads(f_ref), heads(i_ref), heads(g_ref), heads(l_ref), heads(ga_ref),
                         [s_ref[j] for j in range(HB)])
        dq, df, di, dg, dl, dga, ds = vjp((heads(dy_ref), [dst[j] for j in range(HB)]))
        for j, cs in enumerate(cols):
            dq_ref[:, cs] = dq[j].astype(bf16)
            df_ref[:, cs] = df[j].astype(bf16)
            di_ref[:, cs] = di[j].astype(bf16)
            dg_ref[:, cs] = dg[j].astype(bf16)
            dst[j] = ds[j]
            dl_ref[:, cs] += dl[j]
            dga_ref[:, cs] += dga[j]

    act = jax.ShapeDtypeStruct((T, H * HEAD), bf16)
    return pl.pallas_call(
        body, name=name, grid=(H // HB, nc),
        in_specs=[col(0), col(1), col(2), col(3), pl.BlockSpec((2, W), lambda h, c: (0, h)), pl.BlockSpec((1, W), lambda h, c: (0, h)),
                  pl.BlockSpec((HB, None, HEAD, HEAD), lambda h, c: (h, nc - 1 - c, 0, 0)), out_blk],
        out_specs=(out_blk, out_blk, out_blk, out_blk, pl.BlockSpec((2, W), lambda h, c: (0, h)), pl.BlockSpec((1, W), lambda h, c: (0, h))),
        out_shape=(act, act, act, act, jax.ShapeDtypeStruct((2, H * HEAD), f32), jax.ShapeDtypeStruct((1, H * HEAD), f32)),
        scratch_shapes=[pltpu.VMEM((HB, HEAD, HEAD), f32)],
        compiler_params=_params(("parallel", "arbitrary")),
    )(proj, proj, proj, proj, logits, gain, states, dy)


def _rel_index():
    t = np.arange(CHUNK)[:, None]
    sp = np.arange(BAND * CHUNK)[None, :]
    dist = (N_PAST - sp // CHUNK) * CHUNK + t - sp % CHUNK
    return (np.clip(dist, -REL_FUTURE, REL_PAST) + REL_FUTURE).reshape(1, -1).astype(np.int32)


def _bias_table(rel_bias_pad, idx, name):
    H = rel_bias_pad.shape[0]
    n = idx.shape[1]
    tc = _tile(n, 4096)

    def body(rb_ref, idx_ref, o_ref):
        onehot = lax.broadcasted_iota(jnp.int32, (N_REL_PAD, tc), 0) == idx_ref[...]
        o_ref[...] = _dot3(rb_ref[...], onehot, ((1,), (0,)), True)

    return pl.pallas_call(
        body, name=name, grid=(n // tc,),
        in_specs=[pl.BlockSpec((H, N_REL_PAD), lambda j: (0, 0)), pl.BlockSpec((1, tc), lambda j: (0, j))],
        out_specs=pl.BlockSpec((H, tc), lambda j: (0, j)), out_shape=jax.ShapeDtypeStruct((H, n), f32),
        compiler_params=_params(("parallel",)),
    )(rel_bias_pad, idx)


def _bias_table_bwd(dbias, idx, name):
    H, n = dbias.shape
    tc = _tile(n, 4096)

    def body(d_ref, idx_ref, o_ref):
        onehot = lax.broadcasted_iota(jnp.int32, (N_REL_PAD, tc), 0) == idx_ref[...]
        part = _dot3(d_ref[...], onehot, ((1,), (1,)), True)

        @pl.when(pl.program_id(0) == 0)
        def _():
            o_ref[...] = part

        @pl.when(pl.program_id(0) > 0)
        def _():
            o_ref[...] += part

    return pl.pallas_call(
        body, name=name, grid=(n // tc,),
        in_specs=[pl.BlockSpec((H, tc), lambda j: (0, j)), pl.BlockSpec((1, tc), lambda j: (0, j))],
        out_specs=pl.BlockSpec((H, N_REL_PAD), lambda j: (0, 0)), out_shape=jax.ShapeDtypeStruct((H, N_REL_PAD), f32),
        compiler_params=_params(("arbitrary",)),
    )(dbias, idx)


def _head_norm(t, gain):
    return t * lax.rsqrt(jnp.mean(t * t, axis=-1, keepdims=True) + EPS) * gain


def _attn_chunks(qs, kbs, vbs, qg, bias, ns):
    r = range(len(qs))
    qh = [_head_norm(qs[j], qg) for j in r]
    s = [_dotb(qh[j], kbs[j], ((1,), (1,))) * (HEAD ** -0.5) + bias for j in r]
    col = lax.broadcasted_iota(jnp.int32, (1, BAND * CHUNK), 1)
    s = [jnp.where(ns[j] * CHUNK - PAD + col >= 0, s[j], NEG) for j in r]
    e = [jnp.exp(s[j] - jnp.max(s[j], axis=-1, keepdims=True)) for j in r]
    p = [e[j] / jnp.sum(e[j], axis=-1, keepdims=True) for j in r]
    return [_dotb(p[j], vbs[j], ((1,), (0,))) for j in r]


def _attn_fwd(proj, q_gain, k_gain, bias, off, n_heads, name):
    T = proj.shape[0]
    nc = T // CHUNK
    H = n_heads
    CB = _group(nc, ATTN_CHUNKS)
    o0 = off // HEAD
    full = lambda k: pl.BlockSpec((T, HEAD), lambda h, c: (0, o0 + k * H + h))
    vec = pl.BlockSpec((1, HEAD), lambda h, c: (0, 0))

    def body(q_ref, k_ref, v_ref, qg_ref, kg_ref, b_ref, y_ref, kp, vp):
        c = pl.program_id(1)

        @pl.when(c == 0)
        def _():
            kp[pl.ds(0, PAD), :] = jnp.zeros((PAD, HEAD), f32)
            vp[pl.ds(0, PAD), :] = jnp.zeros((PAD, HEAD), f32)
            kp[pl.ds(PAD, T), :] = _head_norm(k_ref[...], kg_ref[...])
            vp[pl.ds(PAD, T), :] = v_ref[...]

        ns = [c * CB + j for j in range(CB)]
        rows = [pl.ds(j * CHUNK, CHUNK) for j in range(CB)]
        bands = [pl.ds(pl.multiple_of(n * CHUNK, CHUNK), BAND * CHUNK) for n in ns]
        outs = _attn_chunks([q_ref[r, :] for r in rows], [kp[b, :] for b in bands], [vp[b, :] for b in bands], qg_ref[...], b_ref[...], ns)
        for r, o in zip(rows, outs):
            y_ref[r, :] = o.astype(bf16)

    return pl.pallas_call(
        body, name=name, grid=(H, nc // CB),
        in_specs=[pl.BlockSpec((CB * CHUNK, HEAD), lambda h, c: (c, o0 + h)), full(1), full(2), vec, vec,
                  pl.BlockSpec((None, CHUNK, BAND * CHUNK), lambda h, c: (h, 0, 0))],
        out_specs=pl.BlockSpec((CB * CHUNK, HEAD), lambda h, c: (c, h)), out_shape=jax.ShapeDtypeStruct((T, H * HEAD), bf16),
        scratch_shapes=[pltpu.VMEM((T + PAD, HEAD), f32), pltpu.VMEM((T + PAD, HEAD), f32)],
        compiler_params=_params(("parallel", "arbitrary")),
    )(proj, proj, proj, q_gain, k_gain, bias)


def _attn_bwd(proj, q_gain, k_gain, bias, dy, off, n_heads, name):
    T = proj.shape[0]
    nc = T // CHUNK
    H = n_heads
    CB = _group(nc, ATTN_CHUNKS)
    o0 = off // HEAD
    full = lambda k: pl.BlockSpec((T, HEAD), lambda h, c: (0, o0 + k * H + h))
    full_out = pl.BlockSpec((T, HEAD), lambda h, c: (0, h))
    vec = pl.BlockSpec((1, HEAD), lambda h, c: (0, 0))
    chunk_out = pl.BlockSpec((CB * CHUNK, HEAD), lambda h, c: (c, h))
    bias_blk = pl.BlockSpec((None, CHUNK, BAND * CHUNK), lambda h, c: (h, 0, 0))

    def body(q_ref, k_ref, v_ref, qg_ref, kg_ref, b_ref, dy_ref, dq_ref, dk_ref, dv_ref, db_ref, dqg_ref, dkg_ref, kp, vp, dkp, dvp):
        h = pl.program_id(0)
        c = pl.program_id(1)

        @pl.when(c == 0)
        def _():
            kp[pl.ds(0, PAD), :] = jnp.zeros((PAD, HEAD), f32)
            vp[pl.ds(0, PAD), :] = jnp.zeros((PAD, HEAD), f32)
            kp[pl.ds(PAD, T), :] = _head_norm(k_ref[...], kg_ref[...])
            vp[pl.ds(PAD, T), :] = v_ref[...]
            dkp[...] = jnp.zeros((T + PAD, HEAD), f32)
            dvp[...] = jnp.zeros((T + PAD, HEAD), f32)
            db_ref[...] = jnp.zeros((CHUNK, BAND * CHUNK), f32)

        @pl.when(jnp.logical_and(h == 0, c == 0))
        def _():
            dqg_ref[...] = jnp.zeros((1, HEAD), f32)
            dkg_ref[...] = jnp.zeros((1, HEAD), f32)

        ns = [c * CB + j for j in range(CB)]
        rows = [pl.ds(j * CHUNK, CHUNK) for j in range(CB)]
        bands = [pl.ds(pl.multiple_of(n * CHUNK, CHUNK), BAND * CHUNK) for n in ns]
        _, vjp = jax.vjp(functools.partial(_attn_chunks, ns=ns), [q_ref[r, :] for r in rows], [kp[b, :] for b in bands],
                         [vp[b, :] for b in bands], qg_ref[...], b_ref[...])
        dqs, dkbs, dvbs, dqg, db = vjp([dy_ref[r, :] for r in rows])
        db_ref[...] += db
        dqg_ref[...] += dqg
        for r, b, dq, dkb, dvb in zip(rows, bands, dqs, dkbs, dvbs):
            dq_ref[r, :] = dq.astype(bf16)
            dkp[b, :] += dkb
            dvp[b, :] += dvb

        @pl.when(c == nc // CB - 1)
        def _():
            _, nvjp = jax.vjp(_head_norm, k_ref[...], kg_ref[...])
            dk, dkg = nvjp(dkp[pl.ds(PAD, T), :])
            dk_ref[...] = dk.astype(bf16)
            dv_ref[...] = dvp[pl.ds(PAD, T), :].astype(bf16)
            dkg_ref[...] += dkg

    act = jax.ShapeDtypeStruct((T, H * HEAD), bf16)
    gvec = jax.ShapeDtypeStruct((1, HEAD), f32)
    pad_buf = pltpu.VMEM((T + PAD, HEAD), f32)
    return pl.pallas_call(
        body, name=name, grid=(H, nc // CB),
        in_specs=[pl.BlockSpec((CB * CHUNK, HEAD), lambda h, c: (c, o0 + h)), full(1), full(2), vec, vec, bias_blk, chunk_out],
        out_specs=(chunk_out, full_out, full_out, bias_blk, vec, vec),
        out_shape=(act, act, act, jax.ShapeDtypeStruct((H, CHUNK, BAND * CHUNK), f32), gvec, gvec),
        scratch_shapes=[pad_buf, pad_buf, pad_buf, pad_buf],
        compiler_params=_params(("arbitrary", "arbitrary")),
    )(proj, proj, proj, q_gain, k_gain, bias, dy)


def _position():
    x, y, c = lax.axis_index("x"), lax.axis_index("y"), lax.axis_index("c")
    return x, y, c, 4 * x + 2 * y + c


def _flip(v, bit):
    return 1 - v if bit else v


def _chips(x, y):
    return [(1 - x, y), (x, 1 - y), (1 - x, 1 - y)]


def _seq_gather(shards, name, collective_id):
    n = len(shards)

    def body(*refs):
        ins, outs = refs[:n], refs[n:2 * n]
        send, recv, loc = refs[2 * n:]
        x, y, c, me = _position()
        sib = (x, y, 1 - c)
        sel = lambda a, b: c * a + (1 - c) * b
        n1 = (sel(1 - x, x), sel(y, 1 - y))
        n2 = (sel(x, 1 - x), sel(1 - y, y))
        far = (1 - x, 1 - y)
        idx = lambda chip, core: 4 * chip[0] + 2 * chip[1] + core
        barrier = pltpu.get_barrier_semaphore()
        for peer in [sib, (*n1, c), (*n2, c)]:
            pl.semaphore_signal(barrier, inc=1, device_id=peer, device_id_type=MESH)
        pl.semaphore_wait(barrier, 3)

        def copy(w, k, src, blk, to):
            return pltpu.make_async_remote_copy(src_ref=src, dst_ref=outs[w].at[blk], send_sem=send.at[7 * w + k], recv_sem=recv.at[7 * w + k],
                                                device_id=to, device_id_type=MESH)

        mine = [pltpu.make_async_copy(ins[w], outs[w].at[me], loc.at[w]) for w in range(n)]
        for cp in mine:
            cp.start()
        sent = [copy(w, 1, ins[w], me, (*n1, c)) for w in range(n)] + [copy(w, 2, ins[w], me, (*n2, c)) for w in range(n)]
        sent += [copy(w, 0, ins[w], me, sib) for w in range(n)]
        for cp in sent:
            cp.start()
        for k, chip in ((1, n1), (2, n2), (3, far)):
            blk = idx(chip, c)
            for w in range(n):
                copy(w, k, ins[w], blk, sib).wait_recv()
                if k == 1:
                    sent.append(copy(w, 3, outs[w].at[blk], blk, (*n2, c)))
                    sent[-1].start()
                sent.append(copy(w, 3 + k, outs[w].at[blk], blk, sib))
                sent[-1].start()
        for w in range(n):
            copy(w, 0, ins[w], idx((x, y), 1 - c), sib).wait_recv()
        for k, chip in ((4, n2), (5, n1), (6, far)):
            for w in range(n):
                copy(w, k, ins[w], idx(chip, 1 - c), sib).wait_recv()
        for cp in sent:
            cp.wait_send()
        for cp in mine:
            cp.wait()

    return pl.kernel(
        body, out_type=tuple(jax.ShapeDtypeStruct((NDEV,) + s.shape, s.dtype) for s in shards),
        mesh=plsc.ScalarSubcoreMesh(axis_name="sequencer", num_cores=1), name=name,
        scratch_types=(pltpu.SemaphoreType.DMA((7 * n,)), pltpu.SemaphoreType.DMA((7 * n,)), pltpu.SemaphoreType.DMA((n,))),
        compiler_params=pltpu.CompilerParams(collective_id=collective_id),
    )(*shards)


NCHIP = 4


def _seq_pair_exchange(grads, name, collective_id, after=()):
    n, na = len(grads), len(after)

    def body(*refs):
        ins, outs = refs[:n], refs[n + na:2 * n + na]
        send, recv = refs[2 * n + na:]
        x, y, c, me = _position()
        sib = (x, y, 1 - c)
        barrier = pltpu.get_barrier_semaphore()
        pl.semaphore_signal(barrier, inc=1, device_id=sib, device_id_type=MESH)
        pl.semaphore_wait(barrier, 1)
        copies = [pltpu.make_async_remote_copy(src_ref=ins[w].at[2 * k + (1 - c)], dst_ref=outs[w].at[k], send_sem=send.at[NCHIP * w + k],
                                               recv_sem=recv.at[NCHIP * w + k], device_id=sib, device_id_type=MESH)
                  for w in range(n) for k in range(NCHIP)]
        for cp in copies:
            cp.start()
        for cp in copies:
            cp.wait_recv()
        for cp in copies:
            cp.wait_send()

    return pl.kernel(
        body, out_type=tuple(jax.ShapeDtypeStruct((NCHIP,) + g.shape[1:], g.dtype) for g in grads),
        mesh=plsc.ScalarSubcoreMesh(axis_name="sequencer", num_cores=1), name=name,
        scratch_types=(pltpu.SemaphoreType.DMA((NCHIP * n,)), pltpu.SemaphoreType.DMA((NCHIP * n,))),
        compiler_params=pltpu.CompilerParams(collective_id=collective_id),
    )(*grads, *after)


def _pair_add(grad, sib_part, name, after=()):
    _, R, C = grad.shape
    tr = _tile(R, 1024, 16)
    core = jnp.reshape(lax.axis_index("c"), (1,)).astype(jnp.int32)

    def body(c_ref, g_ref, s_ref, *rest):
        rest[-1][...] = (g_ref[...].astype(f32) + s_ref[...].astype(f32)).astype(bf16)

    blk = pl.BlockSpec((None, tr, C), lambda k, i, c_ref: (k, i, 0))
    return pl.pallas_call(
        body, name=name,
        grid_spec=pltpu.PrefetchScalarGridSpec(
            num_scalar_prefetch=1, grid=(NCHIP, R // tr),
            in_specs=[pl.BlockSpec((None, tr, C), lambda k, i, c_ref: (2 * k + c_ref[0], i, 0)), blk]
            + [pl.BlockSpec(memory_space=pl.ANY)] * len(after), out_specs=blk),
        out_shape=jax.ShapeDtypeStruct((NCHIP, R, C), bf16), compiler_params=_params(("parallel", "parallel")),
    )(core, grad, sib_part, *after)


def _seq_chip_exchange(sums, name, collective_id, after=()):
    n, na = len(sums), len(after)

    def body(*refs):
        ins, outs = refs[:n], refs[n + na:2 * n + na]
        send, recv, loc = refs[2 * n + na:]
        x, y, c, me = _position()
        chips = _chips(x, y)
        mine = 2 * x + y
        barrier = pltpu.get_barrier_semaphore()
        for px, py in chips:
            pl.semaphore_signal(barrier, inc=1, device_id=(px, py, c), device_id_type=MESH)
        pl.semaphore_wait(barrier, 3)
        local = [pltpu.make_async_copy(ins[w].at[mine], outs[w].at[mine], loc.at[w]) for w in range(n)]
        for cp in local:
            cp.start()
        sends, waits = [], []
        for j, (px, py) in enumerate(chips):
            for w in range(n):
                sems = dict(send_sem=send.at[3 * w + j], recv_sem=recv.at[3 * w + j], device_id=(px, py, c), device_id_type=MESH)
                sends.append(pltpu.make_async_remote_copy(src_ref=ins[w].at[2 * px + py], dst_ref=outs[w].at[mine], **sems))
                waits.append(pltpu.make_async_remote_copy(src_ref=ins[w].at[2 * px + py], dst_ref=outs[w].at[2 * px + py], **sems))
        for cp in sends:
            cp.start()
        for cp in waits:
            cp.wait_recv()
        for cp in sends:
            cp.wait_send()
        for cp in local:
            cp.wait()

    return pl.kernel(
        body, out_type=tuple(jax.ShapeDtypeStruct(s.shape, s.dtype) for s in sums),
        mesh=plsc.ScalarSubcoreMesh(axis_name="sequencer", num_cores=1), name=name,
        scratch_types=(pltpu.SemaphoreType.DMA((3 * n,)), pltpu.SemaphoreType.DMA((3 * n,)), pltpu.SemaphoreType.DMA((n,))),
        compiler_params=pltpu.CompilerParams(collective_id=collective_id),
    )(*sums, *after)


def _reduce_scatter(grads, tag, ids, after=(), add_after=()):
    sib_parts = _seq_pair_exchange(grads, "pair_exchange_" + tag, ids[0], after=after)
    sums = [_pair_add(g, s, "pair_add_%s%d" % (tag, i), after=add_after) for i, (g, s) in enumerate(zip(grads, sib_parts))]
    return _seq_chip_exchange(sums, "chip_exchange_" + tag, ids[1]), sums


def _seq_small_gather(v, name, collective_id):
    R, C = v.shape

    def body(v_ref, o_ref, send, recv, loc):
        x, y, c, me = _position()
        peers = [(_flip(x, r & 4), _flip(y, r & 2), _flip(c, r & 1)) for r in range(1, NDEV)]
        barrier = pltpu.get_barrier_semaphore()
        for peer in peers:
            pl.semaphore_signal(barrier, inc=1, device_id=peer, device_id_type=MESH)
        pl.semaphore_wait(barrier, NDEV - 1)
        mine = pltpu.make_async_copy(v_ref, o_ref.at[me], loc)
        mine.start()
        sends, waits = [], []
        for k, (px, py, pc) in enumerate(peers):
            sems = dict(send_sem=send.at[k], recv_sem=recv.at[k], device_id=(px, py, pc), device_id_type=MESH)
            sends.append(pltpu.make_async_remote_copy(src_ref=v_ref, dst_ref=o_ref.at[me], **sems))
            waits.append(pltpu.make_async_remote_copy(src_ref=v_ref, dst_ref=o_ref.at[4 * px + 2 * py + pc], **sems))
        for cp in sends:
            cp.start()
        for cp in waits:
            cp.wait_recv()
        for cp in sends:
            cp.wait_send()
        mine.wait()

    return pl.kernel(
        body, out_type=jax.ShapeDtypeStruct((NDEV, R, C), f32),
        mesh=plsc.ScalarSubcoreMesh(axis_name="sequencer", num_cores=1), name=name,
        scratch_types=(pltpu.SemaphoreType.DMA((NDEV - 1,)), pltpu.SemaphoreType.DMA((NDEV - 1,)), pltpu.SemaphoreType.DMA),
        compiler_params=pltpu.CompilerParams(collective_id=collective_id),
    )(v)


def _adamw_math(w, g, m, v):
    m = ADAM_B1 * m + (1.0 - ADAM_B1) * g
    v = ADAM_B2 * v + (1.0 - ADAM_B2) * (g * g)
    m_hat = m / (1.0 - ADAM_B1 ** ADAM_STEP)
    v_hat = v / (1.0 - ADAM_B2 ** ADAM_STEP)
    delta = -ADAM_LR * (m_hat / (jnp.sqrt(v_hat) + ADAM_EPS) + ADAM_WD * w)
    return delta, m, v


def _adamw_parts(w, m, v, parts, name, after=()):
    R, C = w.shape
    tr = _tile(R, 128, 16)
    blk = pl.BlockSpec((tr, C), lambda i: (i, 0))

    def body(w_ref, m_ref, v_ref, p_ref, *rest):
        g_ref, d_ref, mo_ref, vo_ref = rest[len(after):]
        g = p_ref[0].astype(f32)
        for i in range(1, NCHIP):
            g = g + p_ref[i].astype(f32)
        d, mn, vn = _adamw_math(w_ref[...], g, m_ref[...], v_ref[...])
        g_ref[...] = g
        d_ref[...] = d
        mo_ref[...] = mn
        vo_ref[...] = vn

    shp = jax.ShapeDtypeStruct((R, C), f32)
    return pl.pallas_call(
        body, name=name, grid=(R // tr,),
        in_specs=[blk, blk, blk, pl.BlockSpec((NCHIP, tr, C), lambda i: (0, i, 0))] + [pl.BlockSpec(memory_space=pl.ANY)] * len(after),
        out_specs=(blk, blk, blk, blk), out_shape=(shp, shp, shp, shp), compiler_params=_params(("parallel",)),
    )(w, m, v, parts, *after)


def _adamw_small(w, parts, m, v, name, after=()):
    def body(w_ref, p_ref, m_ref, v_ref, *rest):
        g_ref, d_ref, mo_ref, vo_ref = rest[len(after):]
        g = p_ref[0]
        for i in range(1, NDEV):
            g = g + p_ref[i]
        d, mn, vn = _adamw_math(w_ref[...], g, m_ref[...], v_ref[...])
        g_ref[...] = g
        d_ref[...] = d
        mo_ref[...] = mn
        vo_ref[...] = vn

    shp = jax.ShapeDtypeStruct(w.shape, f32)
    vm = pl.BlockSpec(memory_space=pltpu.VMEM)
    return pl.pallas_call(body, name=name, in_specs=[vm] * 4 + [pl.BlockSpec(memory_space=pl.ANY)] * len(after), out_specs=(vm,) * 4,
                          out_shape=(shp, shp, shp, shp))(w, parts, m, v, *after)


SMALL_COLS = 1024


def _pack(arrs):
    flat = jnp.concatenate([a.reshape(-1) for a in arrs])
    rows = -(-flat.shape[0] // (8 * SMALL_COLS)) * 8
    return jnp.pad(flat, (0, rows * SMALL_COLS - flat.shape[0])).reshape(rows, SMALL_COLS)


def _unpack(packed, like):
    flat = packed.reshape(-1)
    out, pos = [], 0
    for a in like:
        out.append(flat[pos:pos + a.size].reshape(a.shape))
        pos += a.size
    return out


def kernel(x, w_in, b_gate, norm_mix, norm_ffn, hgrn_lb_logits, hgrn_out_gain, q_gain, k_gain, rel_bias, w_proj_a, w_proj_b, w_out, w_ffn_in, w_ffn_out, loss_target, m_w_in, m_b_gate, m_norm_mix, m_norm_ffn, m_hgrn_lb_logits, m_hgrn_out_gain, m_q_gain, m_k_gain, m_rel_bias, m_w_proj_a, m_w_proj_b, m_w_out, m_w_ffn_in, m_w_ffn_out, v_w_in, v_b_gate, v_norm_mix, v_norm_ffn, v_hgrn_lb_logits, v_hgrn_out_gain, v_q_gain, v_k_gain, v_rel_bias, v_w_proj_a, v_w_proj_b, v_w_out, v_w_ffn_in, v_w_ffn_out):
    xs = x[0]
    target = loss_target[0]
    T, D = xs.shape
    d_a = hgrn_out_gain.shape[-1]
    H = d_a // HEAD
    d_b = d_a
    off_b = 4 * d_a
    off_g = off_b + 3 * d_b
    assert rel_bias.shape[1] == H and T % CHUNK == 0 and T // CHUNK > N_PAST

    big_w = [w_in[0], w_proj_a[0], w_proj_b[0], w_out[0], w_ffn_in[0], w_ffn_out[0]]
    big_m = [m_w_in[0], m_w_proj_a[0], m_w_proj_b[0], m_w_out[0], m_w_ffn_in[0], m_w_ffn_out[0]]
    big_v = [v_w_in[0], v_w_proj_a[0], v_w_proj_b[0], v_w_out[0], v_w_ffn_in[0], v_w_ffn_out[0]]

    sh = [w.astype(bf16) for w in big_w]
    (g_in,) = _seq_gather(sh[0:1], "gather_a", 1)
    g_pa, g_pb, g_out = _seq_gather(sh[1:4], "gather_b", 2)
    (g_fin,) = _seq_gather(sh[4:5], "gather_c", 3)
    (g_fout,) = _seq_gather(sh[5:6], "gather_d", 4)

    h = _rms_fwd(xs, norm_mix, "rms_mix")
    proj = _mm(h, g_in, mode="nn", b_blocked=True, name="mm_proj")
    y_a, states = _hgrn_fwd(proj, hgrn_lb_logits, hgrn_out_gain, H, "hgrn_fwd")
    idx = jnp.asarray(_rel_index())
    rb_pad = jnp.pad(rel_bias[0], ((0, 0), (0, N_REL_PAD - N_REL)))
    bias = _bias_table(rb_pad, idx, "bias_table").reshape(H, CHUNK, BAND * CHUNK)
    y_b = _attn_fwd(proj, q_gain, k_gain, bias, off_b, H, "attn_fwd")
    wg_out = g_out.reshape(-1, g_out.shape[-1])
    wg_fout = g_fout.reshape(-1, g_fout.shape[-1])
    pa, pb, merged = _proj_merge(y_a, y_b, g_pa, g_pb, proj, b_gate, off_g, "mm_proj_ab")
    x1, h2 = _out_rms(merged, wg_out, xs, norm_ffn, "mm_out")
    gu, act = _ffn_in_swiglu(h2, g_fin, "mm_ffn_in")
    dy, loss_acc = _ffn_out_loss(act, wg_fout, x1, target, "mm_ffn_out")
    loss_part = loss_acc[0:1, 0:1] * (0.5 / D)

    gw_fout = _mm(act, dy, mode="tn", out_dtype=bf16, tm=1408, name="mm_gw_ffn_out")
    dgu = _d_act_swiglu(dy, wg_fout, gu, "mm_d_act")
    gw_fin = _mm(h2, dgu, mode="tn", b_stacked=True, out_blocked=True, out_dtype=bf16, tn=g_fin.shape[-1], name="mm_gw_ffn_in")
    dh2 = _mm_nt_blocked(dgu, g_fin, "mm_d_h2", a_stacked=True)
    (p_fout, p_fin), sums_a = _reduce_scatter([gw_fout.reshape(NDEV, -1, D), gw_fin], "a", (5, 6), add_after=(dh2,))
    dx1, g_norm_ffn = _rms_bwd(x1, norm_ffn, dh2, dy, "rms_ffn_bwd", after=sums_a)

    gw_out = _mm(merged, dx1, mode="tn", out_dtype=bf16, name="mm_gw_out")
    dpa, dpb, dgl, g_b_gate = _d_merged_branches(dx1, wg_out, pa, pb, proj, b_gate, off_g, "mm_d_merged")
    dy_a = _mm(dpa, g_pa, mode="nt", b_blocked=True, tm=2048, name="mm_d_ya")
    dy_b = _mm(dpb, g_pb, mode="nt", b_blocked=True, tm=2048, name="mm_d_yb")
    gw_pa = _mm(y_a, dpa, mode="tn", out_blocked=True, out_dtype=bf16, tn=g_pa.shape[-1], name="mm_gw_proj_a")
    gw_pb = _mm(y_b, dpb, mode="tn", out_blocked=True, out_dtype=bf16, tn=g_pb.shape[-1], name="mm_gw_proj_b")

    dq_a, df_a, di_a, dg_a, g_logits, g_gain = _hgrn_bwd(proj, hgrn_lb_logits, hgrn_out_gain, states, dy_a, H, "hgrn_bwd")
    dq_b, dk_b, dv_b, dbias, g_qg, g_kg = _attn_bwd(proj, q_gain, k_gain, bias, dy_b, off_b, H, "attn_bwd")
    g_rel_pad = _bias_table_bwd(dbias.reshape(H, -1), idx, "bias_table_bwd")
    g_rel = g_rel_pad[:, :N_REL]
    dproj = jnp.concatenate([dq_a, df_a, di_a, dg_a, dq_b, dk_b, dv_b, dgl[0], dgl[1]], axis=1)
    (p_out, p_pa, p_pb), sums_b = _reduce_scatter([gw_out.reshape(NDEV, -1, D), gw_pa, gw_pb], "b", (7, 8), after=(g_gain, p_fout, p_fin), add_after=(dy_b,))
    gw_in = _mm(h, dproj, mode="tn", out_blocked=True, out_dtype=bf16, tn=g_in.shape[-1], name="mm_gw_in", after=sums_b)
    upd_fout = _adamw_parts(big_w[5], big_m[5], big_v[5], p_fout, "adamw_w_ffn_out", after=(gw_in,))
    (p_in,), sums_c = _reduce_scatter([gw_in], "c", (9, 10), after=(p_out, p_pa, p_pb), add_after=(g_rel_pad, upd_fout[0]))
    dh = _mm_nt_blocked(dproj, g_in, "mm_d_h", after=sums_c)
    grad_x, g_norm_mix = _rms_bwd(xs, norm_mix, dh, dx1, "rms_mix_bwd")

    parts = [p_in, p_pa, p_pb, p_out, p_fin, p_fout]
    names = ["w_in", "w_proj_a", "w_proj_b", "w_out", "w_ffn_in", "w_ffn_out"]
    big = {}
    for nm, w, m, v, p in zip(names, big_w, big_m, big_v, parts):
        upd = upd_fout if nm == "w_ffn_out" else _adamw_parts(w, m, v, p, "adamw_" + nm, after=() if nm == "w_in" else (g_norm_mix,))
        big[nm] = [o[None] for o in upd]

    small_names = ["b_gate", "norm_mix", "norm_ffn", "hgrn_lb_logits", "hgrn_out_gain", "q_gain", "k_gain", "rel_bias"]
    small_w = [b_gate, norm_mix, norm_ffn, hgrn_lb_logits, hgrn_out_gain, q_gain, k_gain, rel_bias]
    small_m = [m_b_gate, m_norm_mix, m_norm_ffn, m_hgrn_lb_logits, m_hgrn_out_gain, m_q_gain, m_k_gain, m_rel_bias]
    small_v = [v_b_gate, v_norm_mix, v_norm_ffn, v_hgrn_lb_logits, v_hgrn_out_gain, v_q_gain, v_k_gain, v_rel_bias]
    small_g = [g_b_gate.reshape(1, -1), g_norm_mix, g_norm_ffn, g_logits, g_gain, g_qg, g_kg, g_rel[None], loss_part]
    small_parts = _seq_small_gather(_pack(small_g), "gather_small", 11)
    g_sum, d_s, m_s, v_s = _adamw_small(_pack(small_w), small_parts, _pack(small_m), _pack(small_v), "adamw_small", after=(big["w_in"][3],))
    loss = _unpack(g_sum, small_g)[-1].reshape(())
    small = {}
    for nm, g, d, m, v in zip(small_names, _unpack(g_sum, small_w), _unpack(d_s, small_w), _unpack(m_s, small_w), _unpack(v_s, small_w)):
        small[nm] = [g, d, m, v]

    order = ["w_in", "b_gate", "norm_mix", "norm_ffn", "hgrn_lb_logits", "hgrn_out_gain", "q_gain", "k_gain", "rel_bias",
             "w_proj_a", "w_proj_b", "w_out", "w_ffn_in", "w_ffn_out"]
    res = {**big, **small}
    outs = [loss, grad_x[None]]
    for k in range(4):
        outs += [res[nm][k] for nm in order]
    return tuple(outs)
```

```python
import functools

import numpy as np
import jax
import jax.numpy as jnp
from jax import lax
from jax.experimental import pallas as pl
from jax.experimental.pallas import tpu as pltpu
from jax.experimental.pallas import tpu_sc as plsc

f32 = jnp.float32
bf16 = jnp.bfloat16
MESH = pl.DeviceIdType.MESH
AXES = ("x", "y", "c")
NDEV = 8

CHUNK = 64
HEAD = 128
SUB = 8
HGRN_BWD_HEADS = 8
ATTN_CHUNKS = 32
N_PAST = 8
BAND = N_PAST + 1
PAD = N_PAST * CHUNK
REL_FUTURE = CHUNK - 1
REL_PAST = 2 * CHUNK - 1
N_REL = REL_FUTURE + REL_PAST + 1
N_REL_PAD = 256
EPS = 1e-6
NEG = -1e30

ADAM_LR = 0.001
ADAM_B1 = 0.9
ADAM_B2 = 0.999
ADAM_EPS = 1e-08
ADAM_WD = 0.01
ADAM_STEP = 10

VMEM_LIMIT = 56 * 1024 * 1024


def _params(sem=None):
    return pltpu.CompilerParams(dimension_semantics=sem, vmem_limit_bytes=VMEM_LIMIT)


def _tile(n, pref, unit=128):
    if n <= pref:
        return n
    t = (pref // unit) * unit
    while t >= unit:
        if n % t == 0:
            return t
        t -= unit
    return n


_sigmoid = jax.nn.sigmoid


def _mm(a, b, *, mode, name, b_blocked=False, out_blocked=False, out_dtype=f32, tm=1024, tn=1024, tk=2048, after=(),
        b_stacked=False):
    if mode == "tn":
        K, M = a.shape
    else:
        M, K = a.shape
    if b_blocked:
        nb, mid, cb = b.shape
        if mode == "nn":
            assert mid == K
            N, tn = nb * cb, cb
        else:
            assert mode == "nt" and nb * cb == K
            N, tk = mid, cb
    elif b_stacked:
        assert mode == "tn"
        N = 2 * b.shape[2]
    else:
        N = b.shape[1] if mode in ("nn", "tn") else b.shape[0]
    tm = _tile(M, tm)
    tn = tn if (b_blocked and mode == "nn") or out_blocked else _tile(N, tn)
    tk = tk if b_blocked and mode == "nt" else _tile(K, tk)
    assert M % tm == 0 and N % tn == 0 and K % tk == 0
    nk = K // tk
    grid = (M // tm, N // tn, nk)
    if mode == "tn":
        a_spec = pl.BlockSpec((tk, tm), lambda i, j, k: (k, i))
    else:
        a_spec = pl.BlockSpec((tm, tk), lambda i, j, k: (i, k))
    if mode == "nn":
        b_spec = pl.BlockSpec((None, tk, cb), lambda i, j, k: (j, k, 0)) if b_blocked else pl.BlockSpec((tk, tn), lambda i, j, k: (k, j))
    elif mode == "nt":
        b_spec = pl.BlockSpec((None, tn, cb), lambda i, j, k: (k, j, 0)) if b_blocked else pl.BlockSpec((tn, tk), lambda i, j, k: (j, k))
    elif b_stacked:
        nh = N // 2 // tn
        b_spec = pl.BlockSpec((None, tk, tn), lambda i, j, k: (j // nh, k, j % nh))
    else:
        b_spec = pl.BlockSpec((tk, tn), lambda i, j, k: (k, j))
    if out_blocked:
        out_shape = jax.ShapeDtypeStruct((N // tn, M, tn), out_dtype)
        o_spec = pl.BlockSpec((None, tm, tn), lambda i, j, k: (j, i, 0))
    else:
        out_shape = jax.ShapeDtypeStruct((M, N), out_dtype)
        o_spec = pl.BlockSpec((tm, tn), lambda i, j, k: (i, j))
    dims = {"nn": ((1,), (0,)), "nt": ((1,), (1,)), "tn": ((0,), (0,))}[mode]

    def body(a_ref, b_ref, *rest):
        o_ref, acc = rest[len(after)], rest[len(after) + 1:]
        p = lax.dot_general(a_ref[...].astype(bf16), b_ref[...].astype(bf16), (dims, ((), ())), preferred_element_type=f32)
        if nk == 1:
            o_ref[...] = p.astype(out_dtype)
        else:
            acc_ref = acc[0]
            k = pl.program_id(2)

            @pl.when(k == 0)
            def _():
                acc_ref[...] = p

            @pl.when(k > 0)
            def _():
                acc_ref[...] += p

            @pl.when(k == nk - 1)
            def _():
                o_ref[...] = acc_ref[...].astype(out_dtype)

    return pl.pallas_call(
        body, name=name, grid=grid, in_specs=[a_spec, b_spec] + [pl.BlockSpec(memory_space=pl.ANY)] * len(after), out_specs=o_spec,
        out_shape=out_shape, scratch_shapes=[pltpu.VMEM((tm, tn), f32)] if nk > 1 else [],
        compiler_params=_params(("parallel", "parallel", "arbitrary")),
    )(a, b, *after)


def _mm_nt_blocked(a, b, name, *, a_stacked=False, after=(), tm=1024, tn=1024, kb=2):
    nb, N, cb = b.shape
    M = a.shape[1] if a_stacked else a.shape[0]
    tm, tn = _tile(M, tm, 8), _tile(N, tn)
    nk = nb // kb
    per_half = nk // 2

    def body(a_ref, b_ref, *rest):
        o_ref, acc_ref = rest[len(after)], rest[len(after) + 1]
        dn = (((1,), (1,)), ((), ()))
        k = pl.program_id(2)
        p = None
        for q in range(kb):
            d = lax.dot_general(a_ref[:, q * cb:(q + 1) * cb], b_ref[q], dn, preferred_element_type=f32)
            p = d if p is None else p + d

        @pl.when(k == 0)
        def _():
            acc_ref[...] = p

        @pl.when(jnp.logical_and(k > 0, k < nk - 1))
        def _():
            acc_ref[...] += p

        @pl.when(k == nk - 1)
        def _():
            o_ref[...] = acc_ref[...] + p

    if a_stacked:
        a_spec = pl.BlockSpec((None, tm, kb * cb), lambda i, j, k: (k // per_half, i, k % per_half))
    else:
        a_spec = pl.BlockSpec((tm, kb * cb), lambda i, j, k: (i, k))
    return pl.pallas_call(
        body, name=name, grid=(M // tm, N // tn, nk),
        in_specs=[a_spec, pl.BlockSpec((kb, tn, cb), lambda i, j, k: (k, j, 0))] + [pl.BlockSpec(memory_space=pl.ANY)] * len(after),
        out_specs=pl.BlockSpec((tm, tn), lambda i, j, k: (i, j)), out_shape=jax.ShapeDtypeStruct((M, N), f32),
        scratch_shapes=[pltpu.VMEM((tm, tn), f32)],
        compiler_params=_params(("parallel", "parallel", "arbitrary")),
    )(a, b, *after)


def _rms_fwd(x, gain, name):
    T, D = x.shape
    tr = _tile(T, 256, 8)
    row = pl.BlockSpec((tr, D), lambda i: (i, 0))

    def body(x_ref, g_ref, h_ref):
        xs = x_ref[...]
        r = lax.rsqrt(jnp.mean(xs * xs, axis=-1, keepdims=True) + EPS)
        h_ref[...] = (xs * r * g_ref[...]).astype(bf16)

    return pl.pallas_call(body, name=name, grid=(T // tr,), in_specs=[row, pl.BlockSpec((1, D), lambda i: (0, 0))], out_specs=row,
                          out_shape=jax.ShapeDtypeStruct((T, D), bf16), compiler_params=_params(("parallel",)))(x, gain)


def _rms_bwd(xs, gain, dh, extra, name, after=()):
    T, D = xs.shape
    tr = _tile(T, 256, 8)
    row = pl.BlockSpec((tr, D), lambda i: (i, 0))
    vec = pl.BlockSpec((1, D), lambda i: (0, 0))

    def body(x_ref, g_ref, dh_ref, e_ref, *rest):
        dx_ref, dg_ref = rest[len(after):]
        x = x_ref[...]
        r = lax.rsqrt(jnp.mean(x * x, axis=-1, keepdims=True) + EPS)
        xhat = x * r
        dh_v = dh_ref[...]
        gd = dh_v * g_ref[...]
        dx_ref[...] = e_ref[...] + r * (gd - xhat * jnp.mean(gd * xhat, axis=-1, keepdims=True))
        part = jnp.sum(dh_v * xhat, axis=0, keepdims=True)

        @pl.when(pl.program_id(0) == 0)
        def _():
            dg_ref[...] = part

        @pl.when(pl.program_id(0) > 0)
        def _():
            dg_ref[...] += part

    return pl.pallas_call(body, name=name, grid=(T // tr,),
                          in_specs=[row, vec, row, row] + [pl.BlockSpec(memory_space=pl.ANY)] * len(after), out_specs=(row, vec),
                          out_shape=(jax.ShapeDtypeStruct((T, D), f32), jax.ShapeDtypeStruct((1, D), f32)),
                          compiler_params=_params(("arbitrary",)))(xs, gain, dh, extra, *after)


def _proj_merge(y_a, y_b, w_a, w_b, proj, b_gate, off, name):
    T, K = y_a.shape
    nb, _, cb = w_a.shape
    D = nb * cb
    tm = _tile(T, 1024, 8)
    oa, ob = off // cb, (off + D) // cb
    blk = pl.BlockSpec((tm, cb), lambda j, i: (i, j))
    row = pl.BlockSpec((tm, K), lambda j, i: (i, 0))
    wsp = pl.BlockSpec((None, K, cb), lambda j, i: (j, 0, 0))

    def body(ya_ref, yb_ref, wa_ref, wb_ref, ga_ref, gb_ref, ba_ref, bb_ref, pa_ref, pb_ref, m_ref):
        dn = (((1,), (0,)), ((), ()))
        pa = lax.dot_general(ya_ref[...], wa_ref[...], dn, preferred_element_type=f32)
        pb = lax.dot_general(yb_ref[...], wb_ref[...], dn, preferred_element_type=f32)
        pa_ref[...] = pa
        pb_ref[...] = pb
        m_ref[...] = (_sigmoid(ga_ref[...] + ba_ref[...]) * pa + _sigmoid(gb_ref[...] + bb_ref[...]) * pb).astype(bf16)

    return pl.pallas_call(
        body, name=name, grid=(nb, T // tm),
        in_specs=[row, row, wsp, wsp, pl.BlockSpec((tm, cb), lambda j, i: (i, oa + j)), pl.BlockSpec((tm, cb), lambda j, i: (i, ob + j)),
                  pl.BlockSpec((1, cb), lambda j, i: (0, j)), pl.BlockSpec((1, cb), lambda j, i: (0, nb + j))],
        out_specs=(blk, blk, blk),
        out_shape=(jax.ShapeDtypeStruct((T, D), f32), jax.ShapeDtypeStruct((T, D), f32), jax.ShapeDtypeStruct((T, D), bf16)),
        compiler_params=_params(("parallel", "parallel")),
    )(y_a, y_b, w_a, w_b, proj, proj, b_gate, b_gate)


def _out_rms(merged, w_out, x, gain, name):
    T, K = merged.shape
    D = w_out.shape[1]
    tm = _tile(T, 256, 8)
    row = pl.BlockSpec((tm, D), lambda i: (i, 0))

    def body(m_ref, w_ref, x_ref, g_ref, x1_ref, h_ref):
        x1 = x_ref[...] + lax.dot_general(m_ref[...], w_ref[...], (((1,), (0,)), ((), ())), preferred_element_type=f32)
        x1_ref[...] = x1
        r = lax.rsqrt(jnp.mean(x1 * x1, axis=-1, keepdims=True) + EPS)
        h_ref[...] = (x1 * r * g_ref[...]).astype(bf16)

    return pl.pallas_call(
        body, name=name, grid=(T // tm,),
        in_specs=[pl.BlockSpec((tm, K), lambda i: (i, 0)), pl.BlockSpec((K, D), lambda i: (0, 0)), row, pl.BlockSpec((1, D), lambda i: (0, 0))],
        out_specs=(row, row), out_shape=(jax.ShapeDtypeStruct((T, D), f32), jax.ShapeDtypeStruct((T, D), bf16)),
        compiler_params=_params(("parallel",)),
    )(merged, w_out, x, gain)


def _d_merged_branches(dx, w_out, pa, pb, proj, b_gate, off, name):
    T, D = pa.shape
    tm, tn = _tile(T, 512, 8), _tile(D, 1024)
    oa, ob, nb = off // tn, (off + D) // tn, D // tn
    blk = pl.BlockSpec((tm, tn), lambda j, i: (i, j))

    def body(dx_ref, w_ref, pa_ref, pb_ref, ga_ref, gb_ref, ba_ref, bb_ref, dpa_ref, dpb_ref, dgl_ref, db_ref):
        dm = lax.dot_general(dx_ref[...].astype(bf16), w_ref[...], (((1,), (1,)), ((), ())), preferred_element_type=f32)
        sums = []
        for p_ref, gl_ref, b_ref, dp_ref, k in ((pa_ref, ga_ref, ba_ref, dpa_ref, 0), (pb_ref, gb_ref, bb_ref, dpb_ref, 1)):
            g = _sigmoid(gl_ref[...] + b_ref[...])
            dp_ref[...] = (dm * g).astype(bf16)
            dgl = dm * p_ref[...] * g * (1.0 - g)
            dgl_ref[k] = dgl.astype(bf16)
            sums.append(jnp.sum(dgl, axis=0, keepdims=True))

        @pl.when(pl.program_id(1) == 0)
        def _():
            db_ref[...] = jnp.zeros((2, 1, tn), f32)

        db_ref[0] += sums[0]
        db_ref[1] += sums[1]

    return pl.pallas_call(
        body, name=name, grid=(nb, T // tm),
        in_specs=[pl.BlockSpec((tm, D), lambda j, i: (i, 0)), pl.BlockSpec((tn, D), lambda j, i: (j, 0)), blk, blk,
                  pl.BlockSpec((tm, tn), lambda j, i: (i, oa + j)), pl.BlockSpec((tm, tn), lambda j, i: (i, ob + j)),
                  pl.BlockSpec((1, tn), lambda j, i: (0, j)), pl.BlockSpec((1, tn), lambda j, i: (0, nb + j))],
        out_specs=(blk, blk, pl.BlockSpec((2, tm, tn), lambda j, i: (0, i, j)), pl.BlockSpec((2, 1, tn), lambda j, i: (0, 0, j))),
        out_shape=(jax.ShapeDtypeStruct((T, D), bf16), jax.ShapeDtypeStruct((T, D), bf16), jax.ShapeDtypeStruct((2, T, D), bf16),
                   jax.ShapeDtypeStruct((2, 1, D), f32)),
        compiler_params=_params(("parallel", "arbitrary")),
    )(dx, w_out, pa, pb, proj, proj, b_gate, b_gate)


def _ffn_in_swiglu(h, w, name):
    T, K = h.shape
    nb, _, cb = w.shape
    half = nb // 2
    F = half * cb
    tm = _tile(T, 512, 8)

    def body(h_ref, wg_ref, wu_ref, gu_ref, act_ref):
        dn = (((1,), (0,)), ((), ()))
        hv = h_ref[...]
        g = lax.dot_general(hv, wg_ref[...], dn, preferred_element_type=f32)
        u = lax.dot_general(hv, wu_ref[...], dn, preferred_element_type=f32)
        gu_ref[0] = g
        gu_ref[1] = u
        act_ref[...] = (g * _sigmoid(g) * u).astype(bf16)

    return pl.pallas_call(
        body, name=name, grid=(half, T // tm),
        in_specs=[pl.BlockSpec((tm, K), lambda j, i: (i, 0)), pl.BlockSpec((None, K, cb), lambda j, i: (j, 0, 0)),
                  pl.BlockSpec((None, K, cb), lambda j, i: (j + half, 0, 0))],
        out_specs=(pl.BlockSpec((2, tm, cb), lambda j, i: (0, i, j)), pl.BlockSpec((tm, cb), lambda j, i: (i, j))),
        out_shape=(jax.ShapeDtypeStruct((2, T, F), f32), jax.ShapeDtypeStruct((T, F), bf16)),
        compiler_params=_params(("parallel", "parallel")),
    )(h, w, w)


def _d_act_swiglu(dy, w_out, gu, name):
    T, D = dy.shape
    F = w_out.shape[0]
    tm, tn = _tile(T, 512, 8), _tile(F, 1408)

    def body(dy_ref, w_ref, gu_ref, o_ref):
        halves = [pl.ds(r * (tm // 2), tm // 2) for r in range(2)]
        wv = w_ref[...]
        d = [lax.dot_general(dy_ref[r, :].astype(bf16), wv, (((1,), (1,)), ((), ())), preferred_element_type=f32) for r in halves]
        for r, dr in zip(halves, d):
            g = gu_ref[0, r, :]
            s = _sigmoid(g)
            o_ref[0, r, :] = (dr * gu_ref[1, r, :] * s * (1.0 + g * (1.0 - s))).astype(bf16)
            o_ref[1, r, :] = (dr * g * s).astype(bf16)

    blk = pl.BlockSpec((2, tm, tn), lambda j, i: (0, i, j))
    return pl.pallas_call(
        body, name=name, grid=(F // tn, T // tm),
        in_specs=[pl.BlockSpec((tm, D), lambda j, i: (i, 0)), pl.BlockSpec((tn, D), lambda j, i: (j, 0)), blk],
        out_specs=blk, out_shape=jax.ShapeDtypeStruct((2, T, F), bf16), compiler_params=_params(("parallel", "parallel")),
    )(dy, w_out, gu)


def _ffn_out_loss(act, w, x1, target, name):
    T, F = act.shape
    D = w.shape[1]
    tm, tn, tk = _tile(T, 512, 8), _tile(D, 1024), _tile(F, 2816)
    nk = F // tk
    blk = pl.BlockSpec((tm, tn), lambda i, j, k: (i, j))

    def body(a_ref, w_ref, x_ref, t_ref, dy_ref, l_ref, acc_ref):
        i, j, k = pl.program_id(0), pl.program_id(1), pl.program_id(2)
        p = lax.dot_general(a_ref[...], w_ref[...], (((1,), (0,)), ((), ())), preferred_element_type=f32)

        @pl.when(jnp.logical_and(jnp.logical_and(i == 0, j == 0), k == 0))
        def _():
            l_ref[...] = jnp.zeros((8, 128), f32)

        @pl.when(k == 0)
        def _():
            acc_ref[...] = p

        @pl.when(k > 0)
        def _():
            acc_ref[...] += p

        @pl.when(k == nk - 1)
        def _():
            d = acc_ref[...] + x_ref[...] - t_ref[...]
            dy_ref[...] = d * (1.0 / D)
            l_ref[...] += jnp.sum(jnp.sum(d * d, axis=1, keepdims=True), axis=0, keepdims=True)

    return pl.pallas_call(
        body, name=name, grid=(T // tm, D // tn, nk),
        in_specs=[pl.BlockSpec((tm, tk), lambda i, j, k: (i, k)), pl.BlockSpec((tk, tn), lambda i, j, k: (k, j)), blk, blk],
        out_specs=(blk, pl.BlockSpec((8, 128), lambda i, j, k: (0, 0))),
        out_shape=(jax.ShapeDtypeStruct((T, D), f32), jax.ShapeDtypeStruct((8, 128), f32)),
        scratch_shapes=[pltpu.VMEM((tm, tn), f32)],
        compiler_params=_params(("arbitrary", "arbitrary", "arbitrary")),
    )(act, w, x1, target)


_DIMS = {"nn": ((1,), (0,)), "nt": ((1,), (1,)), "tn": ((0,), (0,))}
_MODE = {v: k for k, v in _DIMS.items()}


def _dot_bf16(a, b, mode):
    return lax.dot_general(a.astype(bf16), b.astype(bf16), (_DIMS[mode], ((), ())), preferred_element_type=f32)


@functools.partial(jax.custom_vjp, nondiff_argnums=(2,))
def _dotm(a, b, mode):
    return _dot_bf16(a, b, mode)


def _dotm_fwd(a, b, mode):
    return _dot_bf16(a, b, mode), (a, b)


def _dotm_bwd(mode, res, g):
    a, b = res
    if mode == "nn":
        return _dot_bf16(g, b, "nt"), _dot_bf16(a, g, "tn")
    if mode == "nt":
        return _dot_bf16(g, b, "nn"), _dot_bf16(g, a, "tn")
    return _dot_bf16(b, g, "nt"), _dot_bf16(a, g, "nn")


_dotm.defvjp(_dotm_fwd, _dotm_bwd)


def _dotb(a, b, dims):
    return _dotm(a, b, _MODE[dims])


def _split3(v):
    def top(t):
        return lax.bitcast_convert_type(lax.bitcast_convert_type(t, jnp.uint32) & jnp.uint32(0xFFFF0000), f32)

    hi = top(v)
    mid = top(v - hi)
    low = (v - hi) - mid
    return hi.astype(bf16), mid.astype(bf16), low.astype(bf16)


def _dot3(v, m, dims, v_first):
    m = m.astype(bf16)
    dn = (dims, ((), ()))
    parts = [lax.dot_general(p, m, dn, preferred_element_type=f32) if v_first else lax.dot_general(m, p, dn, preferred_element_type=f32)
             for p in _split3(v)]
    return parts[0] + parts[1] + parts[2]


def _triangle_sum(v, lower):
    row = lax.broadcasted_iota(jnp.int32, (CHUNK, CHUNK), 0)
    col = lax.broadcasted_iota(jnp.int32, (CHUNK, CHUNK), 1)
    return _dot3(v, (col <= row) if lower else (col >= row), ((1,), (0,)), False)


@jax.custom_vjp
def _cumsum_rows(v):
    return _triangle_sum(v, True)


_cumsum_rows.defvjp(lambda v: (_triangle_sum(v, True), None), lambda _, g: (_triangle_sum(g, False),))


def _hgrn_heads(q, fl, iv, g, logits, gain, st):
    r = range(len(q))
    lb = [jax.nn.softmax(logits[j], axis=0)[0:1] for j in r]
    f = [lb[j] + (1.0 - lb[j]) * _sigmoid(fl[j]) for j in r]
    lf = [jnp.log(f[j]) for j in r]
    kk = [1.0 - f[j] for j in r]
    qs = [q[j] * _sigmoid(q[j]) for j in r]
    b = [_cumsum_rows(lf[j]) for j in r]
    b_last = [jnp.sum(lf[j], axis=0, keepdims=True) for j in r]
    o = [_dotb(qs[j] * jnp.exp(b[j]), st[j], ((1,), (1,))) for j in r]
    r3 = lax.broadcasted_iota(jnp.int32, (SUB, SUB, HEAD), 0)
    c3 = lax.broadcasted_iota(jnp.int32, (SUB, SUB, HEAD), 1)
    parts = [[] for _ in r]
    for i in range(CHUNK // SUB):
        lo, hi = i * SUB, (i + 1) * SUB
        bi = [b[j][lo:hi] for j in r]
        dec = [jnp.exp(jnp.where(c3 <= r3, bi[j][:, None, :] - bi[j][None, :, :], -jnp.inf)) for j in r]
        s = [jnp.sum(qs[j][lo:hi][:, None, :] * kk[j][lo:hi][None, :, :] * dec[j], axis=-1) for j in r]
        if i > 0:
            anchor = [jnp.max(bi[j], axis=0, keepdims=True) for j in r]
            qa = [qs[j][lo:hi] * jnp.exp(bi[j] - anchor[j]) for j in r]
            kd = [kk[j][:lo] * jnp.exp(anchor[j] - b[j][:lo]) for j in r]
            s = [jnp.concatenate([_dotb(qa[j], kd[j], ((1,), (1,))), s[j]], axis=1) for j in r]
        for j in r:
            parts[j].append(_dotb(s[j], iv[j][:hi], ((1,), (0,))))
    o = [o[j] + jnp.concatenate(parts[j], axis=0) for j in r]
    st_new = [st[j] * jnp.exp(b_last[j]) + _dotb(iv[j], kk[j] * jnp.exp(b_last[j] - b[j]), ((0,), (0,))) for j in r]
    o = [o[j] * lax.rsqrt(jnp.mean(o[j] * o[j], axis=-1, keepdims=True) + EPS) for j in r]
    o = [o[j] * gain[j] * (g[j] * _sigmoid(g[j])) for j in r]
    return o, st_new


def _group(n, pref):
    while n % pref:
        pref //= 2
    return pref


def _hgrn_fwd(proj, logits, gain, n_heads, name):
    T = proj.shape[0]
    nc = T // CHUNK
    H = n_heads
    HB = _group(H, 8)
    W = HB * HEAD

    def col(k):
        return pl.BlockSpec((CHUNK, W), lambda h, c: (c, k * (H // HB) + h))

    def body(q_ref, f_ref, i_ref, g_ref, l_ref, ga_ref, y_ref, s_ref, st):
        @pl.when(pl.program_id(1) == 0)
        def _():
            st[...] = jnp.zeros((HB, HEAD, HEAD), f32)

        cols = [slice(j * HEAD, (j + 1) * HEAD) for j in range(HB)]
        heads = lambda ref: [ref[:, cs] for cs in cols]
        s_ref[...] = st[...]
        o, st_new = _hgrn_heads(heads(q_ref), heads(f_ref), heads(i_ref), heads(g_ref), heads(l_ref), heads(ga_ref), [st[j] for j in range(HB)])
        for j, cs in enumerate(cols):
            y_ref[:, cs] = o[j].astype(bf16)
            st[j] = st_new[j]

    return pl.pallas_call(
        body, name=name, grid=(H // HB, nc),
        in_specs=[col(0), col(1), col(2), col(3), pl.BlockSpec((2, W), lambda h, c: (0, h)), pl.BlockSpec((1, W), lambda h, c: (0, h))],
        out_specs=(pl.BlockSpec((CHUNK, W), lambda h, c: (c, h)), pl.BlockSpec((HB, None, HEAD, HEAD), lambda h, c: (h, c, 0, 0))),
        out_shape=(jax.ShapeDtypeStruct((T, H * HEAD), bf16), jax.ShapeDtypeStruct((H, nc, HEAD, HEAD), f32)),
        scratch_shapes=[pltpu.VMEM((HB, HEAD, HEAD), f32)],
        compiler_params=_params(("parallel", "arbitrary")),
    )(proj, proj, proj, proj, logits, gain)


def _hgrn_bwd(proj, logits, gain, states, dy, n_heads, name):
    T = proj.shape[0]
    nc = T // CHUNK
    H = n_heads
    HB = _group(H, HGRN_BWD_HEADS)
    W = HB * HEAD

    def col(k):
        return pl.BlockSpec((CHUNK, W), lambda h, c: (nc - 1 - c, k * (H // HB) + h))

    out_blk = pl.BlockSpec((CHUNK, W), lambda h, c: (nc - 1 - c, h))

    def body(q_ref, f_ref, i_ref, g_ref, l_ref, ga_ref, s_ref, dy_ref, dq_ref, df_ref, di_ref, dg_ref, dl_ref, dga_ref, dst):
        first = pl.program_id(1) == 0

        @pl.when(first)
        def _():
            dst[...] = jnp.zeros((HB, HEAD, HEAD), f32)
            dl_ref[...] = jnp.zeros((2, W), f32)
            dga_ref[...] = jnp.zeros((1, W), f32)

        cols = [slice(j * HEAD, (j + 1) * HEAD) for j in range(HB)]
        heads = lambda ref: [ref[:, cs] for cs in cols]
        _, vjp = jax.vjp(_hgrn_heads, heads(q_ref), heads(f_ref), heads(i_ref), heads(g_ref), heads(l_ref), heads(ga_ref),
                         [s_ref[j] for j in range(HB)])
        dq, df, di, dg, dl, dga, ds = vjp((heads(dy_ref), [dst[j] for j in range(HB)]))
        for j, cs in enumerate(cols):
            dq_ref[:, cs] = dq[j].astype(bf16)
            df_ref[:, cs] = df[j].astype(bf16)
            di_ref[:, cs] = di[j].astype(bf16)
            dg_ref[:, cs] = dg[j].astype(bf16)
            dst[j] = ds[j]
            dl_ref[:, cs] += dl[j]
            dga_ref[:, cs] += dga[j]

    act = jax.ShapeDtypeStruct((T, H * HEAD), bf16)
    return pl.pallas_call(
        body, name=name, grid=(H // HB, nc),
        in_specs=[col(0), col(1), col(2), col(3), pl.BlockSpec((2, W), lambda h, c: (0, h)), pl.BlockSpec((1, W), lambda h, c: (0, h)),
                  pl.BlockSpec((HB, None, HEAD, HEAD), lambda h, c: (h, nc - 1 - c, 0, 0)), out_blk],
        out_specs=(out_blk, out_blk, out_blk, out_blk, pl.BlockSpec((2, W), lambda h, c: (0, h)), pl.BlockSpec((1, W), lambda h, c: (0, h))),
        out_shape=(act, act, act, act, jax.ShapeDtypeStruct((2, H * HEAD), f32), jax.ShapeDtypeStruct((1, H * HEAD), f32)),
        scratch_shapes=[pltpu.VMEM((HB, HEAD, HEAD), f32)],
        compiler_params=_params(("parallel", "arbitrary")),
    )(proj, proj, proj, proj, logits, gain, states, dy)


def _rel_index():
    t = np.arange(CHUNK)[:, None]
    sp = np.arange(BAND * CHUNK)[None, :]
    dist = (N_PAST - sp // CHUNK) * CHUNK + t - sp % CHUNK
    return (np.clip(dist, -REL_FUTURE, REL_PAST) + REL_FUTURE).reshape(1, -1).astype(np.int32)


def _bias_table(rel_bias_pad, idx, name):
    H = rel_bias_pad.shape[0]
    n = idx.shape[1]
    tc = _tile(n, 4096)

    def body(rb_ref, idx_ref, o_ref):
        onehot = lax.broadcasted_iota(jnp.int32, (N_REL_PAD, tc), 0) == idx_ref[...]
        o_ref[...] = _dot3(rb_ref[...], onehot, ((1,), (0,)), True)

    return pl.pallas_call(
        body, name=name, grid=(n // tc,),
        in_specs=[pl.BlockSpec((H, N_REL_PAD), lambda j: (0, 0)), pl.BlockSpec((1, tc), lambda j: (0, j))],
        out_specs=pl.BlockSpec((H, tc), lambda j: (0, j)), out_shape=jax.ShapeDtypeStruct((H, n), f32),
        compiler_params=_params(("parallel",)),
    )(rel_bias_pad, idx)


def _bias_table_bwd(dbias, idx, name):
    H, n = dbias.shape
    tc = _tile(n, 4096)

    def body(d_ref, idx_ref, o_ref):
        onehot = lax.broadcasted_iota(jnp.int32, (N_REL_PAD, tc), 0) == idx_ref[...]
        part = _dot3(d_ref[...], onehot, ((1,), (1,)), True)

        @pl.when(pl.program_id(0) == 0)
        def _():
            o_ref[...] = part

        @pl.when(pl.program_id(0) > 0)
        def _():
            o_ref[...] += part

    return pl.pallas_call(
        body, name=name, grid=(n // tc,),
        in_specs=[pl.BlockSpec((H, tc), lambda j: (0, j)), pl.BlockSpec((1, tc), lambda j: (0, j))],
        out_specs=pl.BlockSpec((H, N_REL_PAD), lambda j: (0, 0)), out_shape=jax.ShapeDtypeStruct((H, N_REL_PAD), f32),
        compiler_params=_params(("arbitrary",)),
    )(dbias, idx)


def _head_norm(t, gain):
    return t * lax.rsqrt(jnp.mean(t * t, axis=-1, keepdims=True) + EPS) * gain


def _attn_chunks(qs, kbs, vbs, qg, bias, ns):
    r = range(len(qs))
    qh = [_head_norm(qs[j], qg) for j in r]
    s = [_dotb(qh[j], kbs[j], ((1,), (1,))) * (HEAD ** -0.5) + bias for j in r]
    col = lax.broadcasted_iota(jnp.int32, (1, BAND * CHUNK), 1)
    s = [jnp.where(ns[j] * CHUNK - PAD + col >= 0, s[j], NEG) for j in r]
    e = [jnp.exp(s[j] - jnp.max(s[j], axis=-1, keepdims=True)) for j in r]
    p = [e[j] / jnp.sum(e[j], axis=-1, keepdims=True) for j in r]
    return [_dotb(p[j], vbs[j], ((1,), (0,))) for j in r]


def _attn_fwd(proj, q_gain, k_gain, bias, off, n_heads, name):
    T = proj.shape[0]
    nc = T // CHUNK
    H = n_heads
    CB = _group(nc, ATTN_CHUNKS)
    o0 = off // HEAD
    full = lambda k: pl.BlockSpec((T, HEAD), lambda h, c: (0, o0 + k * H + h))
    vec = pl.BlockSpec((1, HEAD), lambda h, c: (0, 0))

    def body(q_ref, k_ref, v_ref, qg_ref, kg_ref, b_ref, y_ref, kp, vp):
        c = pl.program_id(1)

        @pl.when(c == 0)
        def _():
            kp[pl.ds(0, PAD), :] = jnp.zeros((PAD, HEAD), f32)
            vp[pl.ds(0, PAD), :] = jnp.zeros((PAD, HEAD), f32)
            kp[pl.ds(PAD, T), :] = _head_norm(k_ref[...], kg_ref[...])
            vp[pl.ds(PAD, T), :] = v_ref[...]

        ns = [c * CB + j for j in range(CB)]
        rows = [pl.ds(j * CHUNK, CHUNK) for j in range(CB)]
        bands = [pl.ds(pl.multiple_of(n * CHUNK, CHUNK), BAND * CHUNK) for n in ns]
        outs = _attn_chunks([q_ref[r, :] for r in rows], [kp[b, :] for b in bands], [vp[b, :] for b in bands], qg_ref[...], b_ref[...], ns)
        for r, o in zip(rows, outs):
            y_ref[r, :] = o.astype(bf16)

    return pl.pallas_call(
        body, name=name, grid=(H, nc // CB),
        in_specs=[pl.BlockSpec((CB * CHUNK, HEAD), lambda h, c: (c, o0 + h)), full(1), full(2), vec, vec,
                  pl.BlockSpec((None, CHUNK, BAND * CHUNK), lambda h, c: (h, 0, 0))],
        out_specs=pl.BlockSpec((CB * CHUNK, HEAD), lambda h, c: (c, h)), out_shape=jax.ShapeDtypeStruct((T, H * HEAD), bf16),
        scratch_shapes=[pltpu.VMEM((T + PAD, HEAD), f32), pltpu.VMEM((T + PAD, HEAD), f32)],
        compiler_params=_params(("parallel", "arbitrary")),
    )(proj, proj, proj, q_gain, k_gain, bias)


def _attn_bwd(proj, q_gain, k_gain, bias, dy, off, n_heads, name):
    T = proj.shape[0]
    nc = T // CHUNK
    H = n_heads
    CB = _group(nc, ATTN_CHUNKS)
    o0 = off // HEAD
    full = lambda k: pl.BlockSpec((T, HEAD), lambda h, c: (0, o0 + k * H + h))
    full_out = pl.BlockSpec((T, HEAD), lambda h, c: (0, h))
    vec = pl.BlockSpec((1, HEAD), lambda h, c: (0, 0))
    chunk_out = pl.BlockSpec((CB * CHUNK, HEAD), lambda h, c: (c, h))
    bias_blk = pl.BlockSpec((None, CHUNK, BAND * CHUNK), lambda h, c: (h, 0, 0))

    def body(q_ref, k_ref, v_ref, qg_ref, kg_ref, b_ref, dy_ref, dq_ref, dk_ref, dv_ref, db_ref, dqg_ref, dkg_ref, kp, vp, dkp, dvp):
        h = pl.program_id(0)
        c = pl.program_id(1)

        @pl.when(c == 0)
        def _():
            kp[pl.ds(0, PAD), :] = jnp.zeros((PAD, HEAD), f32)
            vp[pl.ds(0, PAD), :] = jnp.zeros((PAD, HEAD), f32)
            kp[pl.ds(PAD, T), :] = _head_norm(k_ref[...], kg_ref[...])
            vp[pl.ds(PAD, T), :] = v_ref[...]
            dkp[...] = jnp.zeros((T + PAD, HEAD), f32)
            dvp[...] = jnp.zeros((T + PAD, HEAD), f32)
            db_ref[...] = jnp.zeros((CHUNK, BAND * CHUNK), f32)

        @pl.when(jnp.logical_and(h == 0, c == 0))
        def _():
            dqg_ref[...] = jnp.zeros((1, HEAD), f32)
            dkg_ref[...] = jnp.zeros((1, HEAD), f32)

        ns = [c * CB + j for j in range(CB)]
        rows = [pl.ds(j * CHUNK, CHUNK) for j in range(CB)]
        bands = [pl.ds(pl.multiple_of(n * CHUNK, CHUNK), BAND * CHUNK) for n in ns]
        _, vjp = jax.vjp(functools.partial(_attn_chunks, ns=ns), [q_ref[r, :] for r in rows], [kp[b, :] for b in bands],
                         [vp[b, :] for b in bands], qg_ref[...], b_ref[...])
        dqs, dkbs, dvbs, dqg, db = vjp([dy_ref[r, :] for r in rows])
        db_ref[...] += db
        dqg_ref[...] += dqg
        for r, b, dq, dkb, dvb in zip(rows, bands, dqs, dkbs, dvbs):
            dq_ref[r, :] = dq.astype(bf16)
            dkp[b, :] += dkb
            dvp[b, :] += dvb

        @pl.when(c == nc // CB - 1)
        def _():
            _, nvjp = jax.vjp(_head_norm, k_ref[...], kg_ref[...])
            dk, dkg = nvjp(dkp[pl.ds(PAD, T), :])
            dk_ref[...] = dk.astype(bf16)
            dv_ref[...] = dvp[pl.ds(PAD, T), :].astype(bf16)
            dkg_ref[...] += dkg

    act = jax.ShapeDtypeStruct((T, H * HEAD), bf16)
    gvec = jax.ShapeDtypeStruct((1, HEAD), f32)
    pad_buf = pltpu.VMEM((T + PAD, HEAD), f32)
    return pl.pallas_call(
        body, name=name, grid=(H, nc // CB),
        in_specs=[pl.BlockSpec((CB * CHUNK, HEAD), lambda h, c: (c, o0 + h)), full(1), full(2), vec, vec, bias_blk, chunk_out],
        out_specs=(chunk_out, full_out, full_out, bias_blk, vec, vec),
        out_shape=(act, act, act, jax.ShapeDtypeStruct((H, CHUNK, BAND * CHUNK), f32), gvec, gvec),
        scratch_shapes=[pad_buf, pad_buf, pad_buf, pad_buf],
        compiler_params=_params(("arbitrary", "arbitrary")),
    )(proj, proj, proj, q_gain, k_gain, bias, dy)


def _position():
    x, y, c = lax.axis_index("x"), lax.axis_index("y"), lax.axis_index("c")
    return x, y, c, 4 * x + 2 * y + c


def _flip(v, bit):
    return 1 - v if bit else v


def _chips(x, y):
    return [(1 - x, y), (x, 1 - y), (1 - x, 1 - y)]


def _seq_gather(shards, name, collective_id):
    n = len(shards)

    def body(*refs):
        ins, outs = refs[:n], refs[n:2 * n]
        send, recv, loc = refs[2 * n:]
        x, y, c, me = _position()
        sib = (x, y, 1 - c)
        sel = lambda a, b: c * a + (1 - c) * b
        n1 = (sel(1 - x, x), sel(y, 1 - y))
        n2 = (sel(x, 1 - x), sel(1 - y, y))
        far = (1 - x, 1 - y)
        idx = lambda chip, core: 4 * chip[0] + 2 * chip[1] + core
        barrier = pltpu.get_barrier_semaphore()
        for peer in [sib, (*n1, c), (*n2, c)]:
            pl.semaphore_signal(barrier, inc=1, device_id=peer, device_id_type=MESH)
        pl.semaphore_wait(barrier, 3)

        def copy(w, k, src, blk, to):
            return pltpu.make_async_remote_copy(src_ref=src, dst_ref=outs[w].at[blk], send_sem=send.at[7 * w + k], recv_sem=recv.at[7 * w + k],
                                                device_id=to, device_id_type=MESH)

        mine = [pltpu.make_async_copy(ins[w], outs[w].at[me], loc.at[w]) for w in range(n)]
        for cp in mine:
            cp.start()
        sent = [copy(w, 1, ins[w], me, (*n1, c)) for w in range(n)] + [copy(w, 2, ins[w], me, (*n2, c)) for w in range(n)]
        sent += [copy(w, 0, ins[w], me, sib) for w in range(n)]
        for cp in sent:
            cp.start()
        for k, chip in ((1, n1), (2, n2), (3, far)):
            blk = idx(chip, c)
            for w in range(n):
                copy(w, k, ins[w], blk, sib).wait_recv()
                if k == 1:
                    sent.append(copy(w, 3, outs[w].at[blk], blk, (*n2, c)))
                    sent[-1].start()
                sent.append(copy(w, 3 + k, outs[w].at[blk], blk, sib))
                sent[-1].start()
        for w in range(n):
            copy(w, 0, ins[w], idx((x, y), 1 - c), sib).wait_recv()
        for k, chip in ((4, n2), (5, n1), (6, far)):
            for w in range(n):
                copy(w, k, ins[w], idx(chip, 1 - c), sib).wait_recv()
        for cp in sent:
            cp.wait_send()
        for cp in mine:
            cp.wait()

    return pl.kernel(
        body, out_type=tuple(jax.ShapeDtypeStruct((NDEV,) + s.shape, s.dtype) for s in shards),
        mesh=plsc.ScalarSubcoreMesh(axis_name="sequencer", num_cores=1), name=name,
        scratch_types=(pltpu.SemaphoreType.DMA((7 * n,)), pltpu.SemaphoreType.DMA((7 * n,)), pltpu.SemaphoreType.DMA((n,))),
        compiler_params=pltpu.CompilerParams(collective_id=collective_id),
    )(*shards)


NCHIP = 4


def _seq_pair_exchange(grads, name, collective_id, after=()):
    n, na = len(grads), len(after)

    def body(*refs):
        ins, outs = refs[:n], refs[n + na:2 * n + na]
        send, recv = refs[2 * n + na:]
        x, y, c, me = _position()
        sib = (x, y, 1 - c)
        barrier = pltpu.get_barrier_semaphore()
        pl.semaphore_signal(barrier, inc=1, device_id=sib, device_id_type=MESH)
        pl.semaphore_wait(barrier, 1)
        copies = [pltpu.make_async_remote_copy(src_ref=ins[w].at[2 * k + (1 - c)], dst_ref=outs[w].at[k], send_sem=send.at[NCHIP * w + k],
                                               recv_sem=recv.at[NCHIP * w + k], device_id=sib, device_id_type=MESH)
                  for w in range(n) for k in range(NCHIP)]
        for cp in copies:
            cp.start()
        for cp in copies:
            cp.wait_recv()
        for cp in copies:
            cp.wait_send()

    return pl.kernel(
        body, out_type=tuple(jax.ShapeDtypeStruct((NCHIP,) + g.shape[1:], g.dtype) for g in grads),
        mesh=plsc.ScalarSubcoreMesh(axis_name="sequencer", num_cores=1), name=name,
        scratch_types=(pltpu.SemaphoreType.DMA((NCHIP * n,)), pltpu.SemaphoreType.DMA((NCHIP * n,))),
        compiler_params=pltpu.CompilerParams(collective_id=collective_id),
    )(*grads, *after)


def _pair_add(grad, sib_part, name, after=()):
    _, R, C = grad.shape
    tr = _tile(R, 1024, 16)
    core = jnp.reshape(lax.axis_index("c"), (1,)).astype(jnp.int32)

    def body(c_ref, g_ref, s_ref, *rest):
        rest[-1][...] = (g_ref[...].astype(f32) + s_ref[...].astype(f32)).astype(bf16)

    blk = pl.BlockSpec((None, tr, C), lambda k, i, c_ref: (k, i, 0))
    return pl.pallas_call(
        body, name=name,
        grid_spec=pltpu.PrefetchScalarGridSpec(
            num_scalar_prefetch=1, grid=(NCHIP, R // tr),
            in_specs=[pl.BlockSpec((None, tr, C), lambda k, i, c_ref: (2 * k + c_ref[0], i, 0)), blk]
            + [pl.BlockSpec(memory_space=pl.ANY)] * len(after), out_specs=blk),
        out_shape=jax.ShapeDtypeStruct((NCHIP, R, C), bf16), compiler_params=_params(("parallel", "parallel")),
    )(core, grad, sib_part, *after)


def _seq_chip_exchange(sums, name, collective_id, after=()):
    n, na = len(sums), len(after)

    def body(*refs):
        ins, outs = refs[:n], refs[n + na:2 * n + na]
        send, recv, loc = refs[2 * n + na:]
        x, y, c, me = _position()
        chips = _chips(x, y)
        mine = 2 * x + y
        barrier = pltpu.get_barrier_semaphore()
        for px, py in chips:
            pl.semaphore_signal(barrier, inc=1, device_id=(px, py, c), device_id_type=MESH)
        pl.semaphore_wait(barrier, 3)
        local = [pltpu.make_async_copy(ins[w].at[mine], outs[w].at[mine], loc.at[w]) for w in range(n)]
        for cp in local:
            cp.start()
        sends, waits = [], []
        for j, (px, py) in enumerate(chips):
            for w in range(n):
                sems = dict(send_sem=send.at[3 * w + j], recv_sem=recv.at[3 * w + j], device_id=(px, py, c), device_id_type=MESH)
                sends.append(pltpu.make_async_remote_copy(src_ref=ins[w].at[2 * px + py], dst_ref=outs[w].at[mine], **sems))
                waits.append(pltpu.make_async_remote_copy(src_ref=ins[w].at[2 * px + py], dst_ref=outs[w].at[2 * px + py], **sems))
        for cp in sends:
            cp.start()
        for cp in waits:
            cp.wait_recv()
        for cp in sends:
            cp.wait_send()
        for cp in local:
            cp.wait()

    return pl.kernel(
        body, out_type=tuple(jax.ShapeDtypeStruct(s.shape, s.dtype) for s in sums),
        mesh=plsc.ScalarSubcoreMesh(axis_name="sequencer", num_cores=1), name=name,
        scratch_types=(pltpu.SemaphoreType.DMA((3 * n,)), pltpu.SemaphoreType.DMA((3 * n,)), pltpu.SemaphoreType.DMA((n,))),
        compiler_params=pltpu.CompilerParams(collective_id=collective_id),
    )(*sums, *after)


def _reduce_scatter(grads, tag, ids, after=(), add_after=()):
    sib_parts = _seq_pair_exchange(grads, "pair_exchange_" + tag, ids[0], after=after)
    sums = [_pair_add(g, s, "pair_add_%s%d" % (tag, i), after=add_after) for i, (g, s) in enumerate(zip(grads, sib_parts))]
    return _seq_chip_exchange(sums, "chip_exchange_" + tag, ids[1]), sums


def _small_all_reduce(v, name):
    R, C = v.shape

    def body(v_ref, o_ref, buf, send, recv):
        x, y, c, me = _position()
        buf[me] = v_ref[...]
        sends, waits = [], []
        for r in range(1, NDEV):
            px, py, pc = _flip(x, r & 4), _flip(y, r & 2), _flip(c, r & 1)
            peer = 4 * px + 2 * py + pc
            sends.append(pltpu.make_async_remote_copy(src_ref=v_ref, dst_ref=buf.at[me], send_sem=send.at[r - 1], recv_sem=recv.at[r - 1],
                                                      device_id=(px, py, pc), device_id_type=MESH))
            waits.append(pltpu.make_async_remote_copy(src_ref=v_ref, dst_ref=buf.at[peer], send_sem=send.at[r - 1], recv_sem=recv.at[r - 1],
                                                      device_id=(px, py, pc), device_id_type=MESH))
        for cp in sends:
            cp.start()
        for cp in waits:
            cp.wait_recv()
        for cp in sends:
            cp.wait_send()
        acc = buf[0]
        for i in range(1, NDEV):
            acc = acc + buf[i]
        o_ref[...] = acc

    vm = pl.BlockSpec(memory_space=pltpu.VMEM)
    return pl.pallas_call(
        body, name=name, in_specs=[vm], out_specs=vm, out_shape=jax.ShapeDtypeStruct((R, C), f32),
        scratch_shapes=[pltpu.VMEM((NDEV, R, C), f32), pltpu.SemaphoreType.DMA((7,)), pltpu.SemaphoreType.DMA((7,))],
    )(v)


def _adamw_math(w, g, m, v):
    m = ADAM_B1 * m + (1.0 - ADAM_B1) * g
    v = ADAM_B2 * v + (1.0 - ADAM_B2) * (g * g)
    m_hat = m / (1.0 - ADAM_B1 ** ADAM_STEP)
    v_hat = v / (1.0 - ADAM_B2 ** ADAM_STEP)
    delta = -ADAM_LR * (m_hat / (jnp.sqrt(v_hat) + ADAM_EPS) + ADAM_WD * w)
    return delta, m, v


def _adamw_parts(w, m, v, parts, name, after=()):
    R, C = w.shape
    tr = _tile(R, 128, 16)
    blk = pl.BlockSpec((tr, C), lambda i: (i, 0))

    def body(w_ref, m_ref, v_ref, p_ref, *rest):
        g_ref, d_ref, mo_ref, vo_ref = rest[len(after):]
        g = p_ref[0].astype(f32)
        for i in range(1, NCHIP):
            g = g + p_ref[i].astype(f32)
        d, mn, vn = _adamw_math(w_ref[...], g, m_ref[...], v_ref[...])
        g_ref[...] = g
        d_ref[...] = d
        mo_ref[...] = mn
        vo_ref[...] = vn

    shp = jax.ShapeDtypeStruct((R, C), f32)
    return pl.pallas_call(
        body, name=name, grid=(R // tr,),
        in_specs=[blk, blk, blk, pl.BlockSpec((NCHIP, tr, C), lambda i: (0, i, 0))] + [pl.BlockSpec(memory_space=pl.ANY)] * len(after),
        out_specs=(blk, blk, blk, blk), out_shape=(shp, shp, shp, shp), compiler_params=_params(("parallel",)),
    )(w, m, v, parts, *after)


def _adamw_small(w, g, m, v, name):
    def body(w_ref, g_ref, m_ref, v_ref, d_ref, mo_ref, vo_ref):
        d, mn, vn = _adamw_math(w_ref[...], g_ref[...], m_ref[...], v_ref[...])
        d_ref[...] = d
        mo_ref[...] = mn
        vo_ref[...] = vn

    shp = jax.ShapeDtypeStruct(w.shape, f32)
    return pl.pallas_call(body, name=name, out_shape=(shp, shp, shp))(w, g, m, v)


SMALL_COLS = 1024


def _pack(arrs):
    flat = jnp.concatenate([a.reshape(-1) for a in arrs])
    rows = -(-flat.shape[0] // (8 * SMALL_COLS)) * 8
    return jnp.pad(flat, (0, rows * SMALL_COLS - flat.shape[0])).reshape(rows, SMALL_COLS)


def _unpack(packed, like):
    flat = packed.reshape(-1)
    out, pos = [], 0
    for a in like:
        out.append(flat[pos:pos + a.size].reshape(a.shape))
        pos += a.size
    return out


def kernel(x, w_in, b_gate, norm_mix, norm_ffn, hgrn_lb_logits, hgrn_out_gain, q_gain, k_gain, rel_bias, w_proj_a, w_proj_b, w_out, w_ffn_in, w_ffn_out, loss_target, m_w_in, m_b_gate, m_norm_mix, m_norm_ffn, m_hgrn_lb_logits, m_hgrn_out_gain, m_q_gain, m_k_gain, m_rel_bias, m_w_proj_a, m_w_proj_b, m_w_out, m_w_ffn_in, m_w_ffn_out, v_w_in, v_b_gate, v_norm_mix, v_norm_ffn, v_hgrn_lb_logits, v_hgrn_out_gain, v_q_gain, v_k_gain, v_rel_bias, v_w_proj_a, v_w_proj_b, v_w_out, v_w_ffn_in, v_w_ffn_out):
    xs = x[0]
    target = loss_target[0]
    T, D = xs.shape
    d_a = hgrn_out_gain.shape[-1]
    H = d_a // HEAD
    d_b = d_a
    off_b = 4 * d_a
    off_g = off_b + 3 * d_b
    assert rel_bias.shape[1] == H and T % CHUNK == 0 and T // CHUNK > N_PAST

    big_w = [w_in[0], w_proj_a[0], w_proj_b[0], w_out[0], w_ffn_in[0], w_ffn_out[0]]
    big_m = [m_w_in[0], m_w_proj_a[0], m_w_proj_b[0], m_w_out[0], m_w_ffn_in[0], m_w_ffn_out[0]]
    big_v = [v_w_in[0], v_w_proj_a[0], v_w_proj_b[0], v_w_out[0], v_w_ffn_in[0], v_w_ffn_out[0]]

    sh = [w.astype(bf16) for w in big_w]
    (g_in,) = _seq_gather(sh[0:1], "gather_a", 1)
    g_pa, g_pb, g_out = _seq_gather(sh[1:4], "gather_b", 2)
    (g_fin,) = _seq_gather(sh[4:5], "gather_c", 3)
    (g_fout,) = _seq_gather(sh[5:6], "gather_d", 4)

    h = _rms_fwd(xs, norm_mix, "rms_mix")
    proj = _mm(h, g_in, mode="nn", b_blocked=True, name="mm_proj")
    y_a, states = _hgrn_fwd(proj, hgrn_lb_logits, hgrn_out_gain, H, "hgrn_fwd")
    idx = jnp.asarray(_rel_index())
    rb_pad = jnp.pad(rel_bias[0], ((0, 0), (0, N_REL_PAD - N_REL)))
    bias = _bias_table(rb_pad, idx, "bias_table").reshape(H, CHUNK, BAND * CHUNK)
    y_b = _attn_fwd(proj, q_gain, k_gain, bias, off_b, H, "attn_fwd")
    wg_out = g_out.reshape(-1, g_out.shape[-1])
    wg_fout = g_fout.reshape(-1, g_fout.shape[-1])
    pa, pb, merged = _proj_merge(y_a, y_b, g_pa, g_pb, proj, b_gate, off_g, "mm_proj_ab")
    x1, h2 = _out_rms(merged, wg_out, xs, norm_ffn, "mm_out")
    gu, act = _ffn_in_swiglu(h2, g_fin, "mm_ffn_in")
    dy, loss_acc = _ffn_out_loss(act, wg_fout, x1, target, "mm_ffn_out")
    loss_part = loss_acc[0:1, 0:1] * (0.5 / D)

    gw_fout = _mm(act, dy, mode="tn", out_dtype=bf16, tm=1408, name="mm_gw_ffn_out")
    dgu = _d_act_swiglu(dy, wg_fout, gu, "mm_d_act")
    gw_fin = _mm(h2, dgu, mode="tn", b_stacked=True, out_blocked=True, out_dtype=bf16, tn=g_fin.shape[-1], name="mm_gw_ffn_in")
    dh2 = _mm_nt_blocked(dgu, g_fin, "mm_d_h2", a_stacked=True)
    (p_fout, p_fin), sums_a = _reduce_scatter([gw_fout.reshape(NDEV, -1, D), gw_fin], "a", (5, 6), add_after=(dh2,))
    dx1, g_norm_ffn = _rms_bwd(x1, norm_ffn, dh2, dy, "rms_ffn_bwd", after=sums_a)

    gw_out = _mm(merged, dx1, mode="tn", out_dtype=bf16, name="mm_gw_out")
    dpa, dpb, dgl, g_b_gate = _d_merged_branches(dx1, wg_out, pa, pb, proj, b_gate, off_g, "mm_d_merged")
    dy_a = _mm(dpa, g_pa, mode="nt", b_blocked=True, tm=2048, name="mm_d_ya")
    dy_b = _mm(dpb, g_pb, mode="nt", b_blocked=True, tm=2048, name="mm_d_yb")
    gw_pa = _mm(y_a, dpa, mode="tn", out_blocked=True, out_dtype=bf16, tn=g_pa.shape[-1], name="mm_gw_proj_a")
    gw_pb = _mm(y_b, dpb, mode="tn", out_blocked=True, out_dtype=bf16, tn=g_pb.shape[-1], name="mm_gw_proj_b")

    dq_a, df_a, di_a, dg_a, g_logits, g_gain = _hgrn_bwd(proj, hgrn_lb_logits, hgrn_out_gain, states, dy_a, H, "hgrn_bwd")
    dq_b, dk_b, dv_b, dbias, g_qg, g_kg = _attn_bwd(proj, q_gain, k_gain, bias, dy_b, off_b, H, "attn_bwd")
    g_rel_pad = _bias_table_bwd(dbias.reshape(H, -1), idx, "bias_table_bwd")
    g_rel = g_rel_pad[:, :N_REL]
    dproj = jnp.concatenate([dq_a, df_a, di_a, dg_a, dq_b, dk_b, dv_b, dgl[0], dgl[1]], axis=1)
    (p_out, p_pa, p_pb), sums_b = _reduce_scatter([gw_out.reshape(NDEV, -1, D), gw_pa, gw_pb], "b", (7, 8), after=(g_gain, p_fout, p_fin), add_after=(dy_b,))
    gw_in = _mm(h, dproj, mode="tn", out_blocked=True, out_dtype=bf16, tn=g_in.shape[-1], name="mm_gw_in", after=sums_b)
    upd_fout = _adamw_parts(big_w[5], big_m[5], big_v[5], p_fout, "adamw_w_ffn_out", after=(gw_in,))
    (p_in,), sums_c = _reduce_scatter([gw_in], "c", (9, 10), after=(p_out, p_pa, p_pb), add_after=(g_rel_pad, upd_fout[0]))
    dh = _mm_nt_blocked(dproj, g_in, "mm_d_h", after=sums_c)
    grad_x, g_norm_mix = _rms_bwd(xs, norm_mix, dh, dx1, "rms_mix_bwd")

    parts = [p_in, p_pa, p_pb, p_out, p_fin, p_fout]
    names = ["w_in", "w_proj_a", "w_proj_b", "w_out", "w_ffn_in", "w_ffn_out"]
    big = {}
    for nm, w, m, v, p in zip(names, big_w, big_m, big_v, parts):
        upd = upd_fout if nm == "w_ffn_out" else _adamw_parts(w, m, v, p, "adamw_" + nm, after=() if nm == "w_in" else (g_norm_mix,))
        big[nm] = [o[None] for o in upd]

    small_names = ["b_gate", "norm_mix", "norm_ffn", "hgrn_lb_logits", "hgrn_out_gain", "q_gain", "k_gain", "rel_bias"]
    small_w = [b_gate, norm_mix, norm_ffn, hgrn_lb_logits, hgrn_out_gain, q_gain, k_gain, rel_bias]
    small_m = [m_b_gate, m_norm_mix, m_norm_ffn, m_hgrn_lb_logits, m_hgrn_out_gain, m_q_gain, m_k_gain, m_rel_bias]
    small_v = [v_b_gate, v_norm_mix, v_norm_ffn, v_hgrn_lb_logits, v_hgrn_out_gain, v_q_gain, v_k_gain, v_rel_bias]
    small_g = [g_b_gate.reshape(1, -1), g_norm_mix, g_norm_ffn, g_logits, g_gain, g_qg, g_kg, g_rel[None], loss_part]
    g_sum = _small_all_reduce(_pack(small_g), "reduce_small")
    loss = _unpack(g_sum, small_g)[-1].reshape(())
    d_s, m_s, v_s = _adamw_small(_pack(small_w), g_sum, _pack(small_m), _pack(small_v), "adamw_small")
    small = {}
    for nm, g, d, m, v in zip(small_names, _unpack(g_sum, small_w), _unpack(d_s, small_w), _unpack(m_s, small_w), _unpack(v_s, small_w)):
        small[nm] = [g, d, m, v]

    order = ["w_in", "b_gate", "norm_mix", "norm_ffn", "hgrn_lb_logits", "hgrn_out_gain", "q_gain", "k_gain", "rel_bias",
             "w_proj_a", "w_proj_b", "w_out", "w_ffn_in", "w_ffn_out"]
    res = {**big, **small}
    outs = [loss, grad_x[None]]
    for k in range(4):
        outs += [res[nm][k] for nm in order]
    return tuple(outs)
```

```python
import functools

import numpy as np
import jax
import jax.numpy as jnp
from jax import lax
from jax.experimental import pallas as pl
from jax.experimental.pallas import tpu as pltpu
from jax.experimental.pallas import tpu_sc as plsc

f32 = jnp.float32
bf16 = jnp.bfloat16
MESH = pl.DeviceIdType.MESH
AXES = ("x", "y", "c")
NDEV = 8

CHUNK = 64
HEAD = 128
SUB = 8
HGRN_BWD_HEADS = 8
ATTN_CHUNKS = 32
N_PAST = 8
BAND = N_PAST + 1
PAD = N_PAST * CHUNK
REL_FUTURE = CHUNK - 1
REL_PAST = 2 * CHUNK - 1
N_REL = REL_FUTURE + REL_PAST + 1
N_REL_PAD = 256
EPS = 1e-6
NEG = -1e30

ADAM_LR = 0.001
ADAM_B1 = 0.9
ADAM_B2 = 0.999
ADAM_EPS = 1e-08
ADAM_WD = 0.01
ADAM_STEP = 10

VMEM_LIMIT = 56 * 1024 * 1024


def _params(sem=None):
    return pltpu.CompilerParams(dimension_semantics=sem, vmem_limit_bytes=VMEM_LIMIT)


def _tile(n, pref, unit=128):
    if n <= pref:
        return n
    t = (pref // unit) * unit
    while t >= unit:
        if n % t == 0:
            return t
        t -= unit
    return n


_sigmoid = jax.nn.sigmoid


def _mm(a, b, *, mode, name, b_blocked=False, out_blocked=False, out_dtype=f32, tm=1024, tn=1024, tk=2048, after=(),
        b_stacked=False):
    if mode == "tn":
        K, M = a.shape
    else:
        M, K = a.shape
    if b_blocked:
        nb, mid, cb = b.shape
        if mode == "nn":
            assert mid == K
            N, tn = nb * cb, cb
        else:
            assert mode == "nt" and nb * cb == K
            N, tk = mid, cb
    elif b_stacked:
        assert mode == "tn"
        N = 2 * b.shape[2]
    else:
        N = b.shape[1] if mode in ("nn", "tn") else b.shape[0]
    tm = _tile(M, tm)
    tn = tn if (b_blocked and mode == "nn") or out_blocked else _tile(N, tn)
    tk = tk if b_blocked and mode == "nt" else _tile(K, tk)
    assert M % tm == 0 and N % tn == 0 and K % tk == 0
    nk = K // tk
    grid = (M // tm, N // tn, nk)
    if mode == "tn":
        a_spec = pl.BlockSpec((tk, tm), lambda i, j, k: (k, i))
    else:
        a_spec = pl.BlockSpec((tm, tk), lambda i, j, k: (i, k))
    if mode == "nn":
        b_spec = pl.BlockSpec((None, tk, cb), lambda i, j, k: (j, k, 0)) if b_blocked else pl.BlockSpec((tk, tn), lambda i, j, k: (k, j))
    elif mode == "nt":
        b_spec = pl.BlockSpec((None, tn, cb), lambda i, j, k: (k, j, 0)) if b_blocked else pl.BlockSpec((tn, tk), lambda i, j, k: (j, k))
    elif b_stacked:
        nh = N // 2 // tn
        b_spec = pl.BlockSpec((None, tk, tn), lambda i, j, k: (j // nh, k, j % nh))
    else:
        b_spec = pl.BlockSpec((tk, tn), lambda i, j, k: (k, j))
    if out_blocked:
        out_shape = jax.ShapeDtypeStruct((N // tn, M, tn), out_dtype)
        o_spec = pl.BlockSpec((None, tm, tn), lambda i, j, k: (j, i, 0))
    else:
        out_shape = jax.ShapeDtypeStruct((M, N), out_dtype)
        o_spec = pl.BlockSpec((tm, tn), lambda i, j, k: (i, j))
    dims = {"nn": ((1,), (0,)), "nt": ((1,), (1,)), "tn": ((0,), (0,))}[mode]

    def body(a_ref, b_ref, *rest):
        o_ref, acc = rest[len(after)], rest[len(after) + 1:]
        p = lax.dot_general(a_ref[...].astype(bf16), b_ref[...].astype(bf16), (dims, ((), ())), preferred_element_type=f32)
        if nk == 1:
            o_ref[...] = p.astype(out_dtype)
        else:
            acc_ref = acc[0]
            k = pl.program_id(2)

            @pl.when(k == 0)
            def _():
                acc_ref[...] = p

            @pl.when(k > 0)
            def _():
                acc_ref[...] += p

            @pl.when(k == nk - 1)
            def _():
                o_ref[...] = acc_ref[...].astype(out_dtype)

    return pl.pallas_call(
        body, name=name, grid=grid, in_specs=[a_spec, b_spec] + [pl.BlockSpec(memory_space=pl.ANY)] * len(after), out_specs=o_spec,
        out_shape=out_shape, scratch_shapes=[pltpu.VMEM((tm, tn), f32)] if nk > 1 else [],
        compiler_params=_params(("parallel", "parallel", "arbitrary")),
    )(a, b, *after)


def _mm_nt_blocked(a, b, name, *, a_stacked=False, after=(), tm=1024, tn=1024, kb=2):
    nb, N, cb = b.shape
    M = a.shape[1] if a_stacked else a.shape[0]
    tm, tn = _tile(M, tm, 8), _tile(N, tn)
    nk = nb // kb
    per_half = nk // 2

    def body(a_ref, b_ref, *rest):
        o_ref, acc_ref = rest[len(after)], rest[len(after) + 1]
        dn = (((1,), (1,)), ((), ()))
        k = pl.program_id(2)
        p = None
        for q in range(kb):
            d = lax.dot_general(a_ref[:, q * cb:(q + 1) * cb], b_ref[q], dn, preferred_element_type=f32)
            p = d if p is None else p + d
        if nk == 1:
            o_ref[...] = p
            return

        @pl.when(k == 0)
        def _():
            acc_ref[...] = p

        @pl.when(jnp.logical_and(k > 0, k < nk - 1))
        def _():
            acc_ref[...] += p

        @pl.when(k == nk - 1)
        def _():
            o_ref[...] = acc_ref[...] + p

    if a_stacked:
        a_spec = pl.BlockSpec((None, tm, kb * cb), lambda i, j, k: (k // per_half, i, k % per_half))
    else:
        a_spec = pl.BlockSpec((tm, kb * cb), lambda i, j, k: (i, k))
    return pl.pallas_call(
        body, name=name, grid=(M // tm, N // tn, nk),
        in_specs=[a_spec, pl.BlockSpec((kb, tn, cb), lambda i, j, k: (k, j, 0))] + [pl.BlockSpec(memory_space=pl.ANY)] * len(after),
        out_specs=pl.BlockSpec((tm, tn), lambda i, j, k: (i, j)), out_shape=jax.ShapeDtypeStruct((M, N), f32),
        scratch_shapes=[pltpu.VMEM((tm, tn), f32)],
        compiler_params=_params(("parallel", "parallel", "arbitrary")),
    )(a, b, *after)


def _rms_fwd(x, gain, name):
    T, D = x.shape
    tr = _tile(T, 256, 8)
    row = pl.BlockSpec((tr, D), lambda i: (i, 0))

    def body(x_ref, g_ref, h_ref):
        xs = x_ref[...]
        r = lax.rsqrt(jnp.mean(xs * xs, axis=-1, keepdims=True) + EPS)
        h_ref[...] = (xs * r * g_ref[...]).astype(bf16)

    return pl.pallas_call(body, name=name, grid=(T // tr,), in_specs=[row, pl.BlockSpec((1, D), lambda i: (0, 0))], out_specs=row,
                          out_shape=jax.ShapeDtypeStruct((T, D), bf16), compiler_params=_params(("parallel",)))(x, gain)


def _rms_bwd(xs, gain, dh, extra, name, after=()):
    T, D = xs.shape
    tr = _tile(T, 256, 8)
    row = pl.BlockSpec((tr, D), lambda i: (i, 0))
    vec = pl.BlockSpec((1, D), lambda i: (0, 0))

    def body(x_ref, g_ref, dh_ref, e_ref, *rest):
        dx_ref, dg_ref = rest[len(after):]
        x = x_ref[...]
        r = lax.rsqrt(jnp.mean(x * x, axis=-1, keepdims=True) + EPS)
        xhat = x * r
        dh_v = dh_ref[...]
        gd = dh_v * g_ref[...]
        dx_ref[...] = e_ref[...] + r * (gd - xhat * jnp.mean(gd * xhat, axis=-1, keepdims=True))
        part = jnp.sum(dh_v * xhat, axis=0, keepdims=True)

        @pl.when(pl.program_id(0) == 0)
        def _():
            dg_ref[...] = part

        @pl.when(pl.program_id(0) > 0)
        def _():
            dg_ref[...] += part

    return pl.pallas_call(body, name=name, grid=(T // tr,),
                          in_specs=[row, vec, row, row] + [pl.BlockSpec(memory_space=pl.ANY)] * len(after), out_specs=(row, vec),
                          out_shape=(jax.ShapeDtypeStruct((T, D), f32), jax.ShapeDtypeStruct((1, D), f32)),
                          compiler_params=_params(("arbitrary",)))(xs, gain, dh, extra, *after)


def _proj_merge(y_a, y_b, w_a, w_b, proj, b_gate, off, name):
    T, K = y_a.shape
    nb, _, cb = w_a.shape
    D = nb * cb
    tm = _tile(T, 1024, 8)
    oa, ob = off // cb, (off + D) // cb
    blk = pl.BlockSpec((tm, cb), lambda j, i: (i, j))
    row = pl.BlockSpec((tm, K), lambda j, i: (i, 0))
    wsp = pl.BlockSpec((None, K, cb), lambda j, i: (j, 0, 0))

    def body(ya_ref, yb_ref, wa_ref, wb_ref, ga_ref, gb_ref, ba_ref, bb_ref, pa_ref, pb_ref, m_ref):
        dn = (((1,), (0,)), ((), ()))
        pa = lax.dot_general(ya_ref[...], wa_ref[...], dn, preferred_element_type=f32)
        pb = lax.dot_general(yb_ref[...], wb_ref[...], dn, preferred_element_type=f32)
        pa_ref[...] = pa
        pb_ref[...] = pb
        m_ref[...] = (_sigmoid(ga_ref[...] + ba_ref[...]) * pa + _sigmoid(gb_ref[...] + bb_ref[...]) * pb).astype(bf16)

    return pl.pallas_call(
        body, name=name, grid=(nb, T // tm),
        in_specs=[row, row, wsp, wsp, pl.BlockSpec((tm, cb), lambda j, i: (i, oa + j)), pl.BlockSpec((tm, cb), lambda j, i: (i, ob + j)),
                  pl.BlockSpec((1, cb), lambda j, i: (0, j)), pl.BlockSpec((1, cb), lambda j, i: (0, nb + j))],
        out_specs=(blk, blk, blk),
        out_shape=(jax.ShapeDtypeStruct((T, D), f32), jax.ShapeDtypeStruct((T, D), f32), jax.ShapeDtypeStruct((T, D), bf16)),
        compiler_params=_params(("parallel", "parallel")),
    )(y_a, y_b, w_a, w_b, proj, proj, b_gate, b_gate)


def _out_rms(merged, w_out, x, gain, name):
    T, K = merged.shape
    D = w_out.shape[1]
    tm = _tile(T, 256, 8)
    row = pl.BlockSpec((tm, D), lambda i: (i, 0))

    def body(m_ref, w_ref, x_ref, g_ref, x1_ref, h_ref):
        x1 = x_ref[...] + lax.dot_general(m_ref[...], w_ref[...], (((1,), (0,)), ((), ())), preferred_element_type=f32)
        x1_ref[...] = x1
        r = lax.rsqrt(jnp.mean(x1 * x1, axis=-1, keepdims=True) + EPS)
        h_ref[...] = (x1 * r * g_ref[...]).astype(bf16)

    return pl.pallas_call(
        body, name=name, grid=(T // tm,),
        in_specs=[pl.BlockSpec((tm, K), lambda i: (i, 0)), pl.BlockSpec((K, D), lambda i: (0, 0)), row, pl.BlockSpec((1, D), lambda i: (0, 0))],
        out_specs=(row, row), out_shape=(jax.ShapeDtypeStruct((T, D), f32), jax.ShapeDtypeStruct((T, D), bf16)),
        compiler_params=_params(("parallel",)),
    )(merged, w_out, x, gain)


def _d_merged_branches(dx, w_out, pa, pb, proj, b_gate, off, name):
    T, D = pa.shape
    tm, tn = _tile(T, 512, 8), _tile(D, 1024)
    oa, ob, nb = off // tn, (off + D) // tn, D // tn
    blk = pl.BlockSpec((tm, tn), lambda j, i: (i, j))

    def body(dx_ref, w_ref, pa_ref, pb_ref, ga_ref, gb_ref, ba_ref, bb_ref, dpa_ref, dpb_ref, dgl_ref, db_ref):
        dm = lax.dot_general(dx_ref[...].astype(bf16), w_ref[...], (((1,), (1,)), ((), ())), preferred_element_type=f32)
        sums = []
        for p_ref, gl_ref, b_ref, dp_ref, k in ((pa_ref, ga_ref, ba_ref, dpa_ref, 0), (pb_ref, gb_ref, bb_ref, dpb_ref, 1)):
            g = _sigmoid(gl_ref[...] + b_ref[...])
            dp_ref[...] = (dm * g).astype(bf16)
            dgl = dm * p_ref[...] * g * (1.0 - g)
            dgl_ref[k] = dgl.astype(bf16)
            sums.append(jnp.sum(dgl, axis=0, keepdims=True))

        @pl.when(pl.program_id(1) == 0)
        def _():
            db_ref[...] = jnp.zeros((2, 1, tn), f32)

        db_ref[0] += sums[0]
        db_ref[1] += sums[1]

    return pl.pallas_call(
        body, name=name, grid=(nb, T // tm),
        in_specs=[pl.BlockSpec((tm, D), lambda j, i: (i, 0)), pl.BlockSpec((tn, D), lambda j, i: (j, 0)), blk, blk,
                  pl.BlockSpec((tm, tn), lambda j, i: (i, oa + j)), pl.BlockSpec((tm, tn), lambda j, i: (i, ob + j)),
                  pl.BlockSpec((1, tn), lambda j, i: (0, j)), pl.BlockSpec((1, tn), lambda j, i: (0, nb + j))],
        out_specs=(blk, blk, pl.BlockSpec((2, tm, tn), lambda j, i: (0, i, j)), pl.BlockSpec((2, 1, tn), lambda j, i: (0, 0, j))),
        out_shape=(jax.ShapeDtypeStruct((T, D), bf16), jax.ShapeDtypeStruct((T, D), bf16), jax.ShapeDtypeStruct((2, T, D), bf16),
                   jax.ShapeDtypeStruct((2, 1, D), f32)),
        compiler_params=_params(("parallel", "arbitrary")),
    )(dx, w_out, pa, pb, proj, proj, b_gate, b_gate)


def _ffn_in_swiglu(h, w, name):
    T, K = h.shape
    nb, _, cb = w.shape
    half = nb // 2
    F = half * cb
    tm = _tile(T, 512, 8)

    def body(h_ref, wg_ref, wu_ref, gu_ref, act_ref):
        dn = (((1,), (0,)), ((), ()))
        hv = h_ref[...]
        g = lax.dot_general(hv, wg_ref[...], dn, preferred_element_type=f32)
        u = lax.dot_general(hv, wu_ref[...], dn, preferred_element_type=f32)
        gu_ref[0] = g
        gu_ref[1] = u
        act_ref[...] = (g * _sigmoid(g) * u).astype(bf16)

    return pl.pallas_call(
        body, name=name, grid=(half, T // tm),
        in_specs=[pl.BlockSpec((tm, K), lambda j, i: (i, 0)), pl.BlockSpec((None, K, cb), lambda j, i: (j, 0, 0)),
                  pl.BlockSpec((None, K, cb), lambda j, i: (j + half, 0, 0))],
        out_specs=(pl.BlockSpec((2, tm, cb), lambda j, i: (0, i, j)), pl.BlockSpec((tm, cb), lambda j, i: (i, j))),
        out_shape=(jax.ShapeDtypeStruct((2, T, F), f32), jax.ShapeDtypeStruct((T, F), bf16)),
        compiler_params=_params(("parallel", "parallel")),
    )(h, w, w)


def _d_act_swiglu(dy, w_out, gu, name):
    T, D = dy.shape
    F = w_out.shape[0]
    tm, tn = _tile(T, 512, 8), _tile(F, 1408)

    def body(dy_ref, w_ref, gu_ref, o_ref):
        halves = [pl.ds(r * (tm // 2), tm // 2) for r in range(2)]
        wv = w_ref[...]
        d = [lax.dot_general(dy_ref[r, :].astype(bf16), wv, (((1,), (1,)), ((), ())), preferred_element_type=f32) for r in halves]
        for r, dr in zip(halves, d):
            g = gu_ref[0, r, :]
            s = _sigmoid(g)
            o_ref[0, r, :] = (dr * gu_ref[1, r, :] * s * (1.0 + g * (1.0 - s))).astype(bf16)
            o_ref[1, r, :] = (dr * g * s).astype(bf16)

    blk = pl.BlockSpec((2, tm, tn), lambda j, i: (0, i, j))
    return pl.pallas_call(
        body, name=name, grid=(F // tn, T // tm),
        in_specs=[pl.BlockSpec((tm, D), lambda j, i: (i, 0)), pl.BlockSpec((tn, D), lambda j, i: (j, 0)), blk],
        out_specs=blk, out_shape=jax.ShapeDtypeStruct((2, T, F), bf16), compiler_params=_params(("parallel", "parallel")),
    )(dy, w_out, gu)


def _ffn_out_loss(act, w, x1, target, name):
    T, F = act.shape
    D = w.shape[1]
    tm, tn, tk = _tile(T, 512, 8), _tile(D, 1024), _tile(F, 2816)
    nk = F // tk
    blk = pl.BlockSpec((tm, tn), lambda i, j, k: (i, j))

    def body(a_ref, w_ref, x_ref, t_ref, dy_ref, l_ref, acc_ref):
        i, j, k = pl.program_id(0), pl.program_id(1), pl.program_id(2)
        p = lax.dot_general(a_ref[...], w_ref[...], (((1,), (0,)), ((), ())), preferred_element_type=f32)

        @pl.when(jnp.logical_and(jnp.logical_and(i == 0, j == 0), k == 0))
        def _():
            l_ref[...] = jnp.zeros((8, 128), f32)

        @pl.when(k == 0)
        def _():
            acc_ref[...] = p

        @pl.when(k > 0)
        def _():
            acc_ref[...] += p

        @pl.when(k == nk - 1)
        def _():
            d = acc_ref[...] + x_ref[...] - t_ref[...]
            dy_ref[...] = d * (1.0 / D)
            l_ref[...] += jnp.sum(jnp.sum(d * d, axis=1, keepdims=True), axis=0, keepdims=True)

    return pl.pallas_call(
        body, name=name, grid=(T // tm, D // tn, nk),
        in_specs=[pl.BlockSpec((tm, tk), lambda i, j, k: (i, k)), pl.BlockSpec((tk, tn), lambda i, j, k: (k, j)), blk, blk],
        out_specs=(blk, pl.BlockSpec((8, 128), lambda i, j, k: (0, 0))),
        out_shape=(jax.ShapeDtypeStruct((T, D), f32), jax.ShapeDtypeStruct((8, 128), f32)),
        scratch_shapes=[pltpu.VMEM((tm, tn), f32)],
        compiler_params=_params(("arbitrary", "arbitrary", "arbitrary")),
    )(act, w, x1, target)


_DIMS = {"nn": ((1,), (0,)), "nt": ((1,), (1,)), "tn": ((0,), (0,))}
_MODE = {v: k for k, v in _DIMS.items()}


def _dot_bf16(a, b, mode):
    return lax.dot_general(a.astype(bf16), b.astype(bf16), (_DIMS[mode], ((), ())), preferred_element_type=f32)


@functools.partial(jax.custom_vjp, nondiff_argnums=(2,))
def _dotm(a, b, mode):
    return _dot_bf16(a, b, mode)


def _dotm_fwd(a, b, mode):
    return _dot_bf16(a, b, mode), (a, b)


def _dotm_bwd(mode, res, g):
    a, b = res
    if mode == "nn":
        return _dot_bf16(g, b, "nt"), _dot_bf16(a, g, "tn")
    if mode == "nt":
        return _dot_bf16(g, b, "nn"), _dot_bf16(g, a, "tn")
    return _dot_bf16(b, g, "nt"), _dot_bf16(a, g, "nn")


_dotm.defvjp(_dotm_fwd, _dotm_bwd)


def _dotb(a, b, dims):
    return _dotm(a, b, _MODE[dims])


def _split3(v):
    def top(t):
        return lax.bitcast_convert_type(lax.bitcast_convert_type(t, jnp.uint32) & jnp.uint32(0xFFFF0000), f32)

    hi = top(v)
    mid = top(v - hi)
    low = (v - hi) - mid
    return hi.astype(bf16), mid.astype(bf16), low.astype(bf16)


def _dot3(v, m, dims, v_first):
    m = m.astype(bf16)
    dn = (dims, ((), ()))
    parts = [lax.dot_general(p, m, dn, preferred_element_type=f32) if v_first else lax.dot_general(m, p, dn, preferred_element_type=f32)
             for p in _split3(v)]
    return parts[0] + parts[1] + parts[2]


def _triangle_sum(v, lower):
    row = lax.broadcasted_iota(jnp.int32, (CHUNK, CHUNK), 0)
    col = lax.broadcasted_iota(jnp.int32, (CHUNK, CHUNK), 1)
    return _dot3(v, (col <= row) if lower else (col >= row), ((1,), (0,)), False)


@jax.custom_vjp
def _cumsum_rows(v):
    return _triangle_sum(v, True)


_cumsum_rows.defvjp(lambda v: (_triangle_sum(v, True), None), lambda _, g: (_triangle_sum(g, False),))


def _hgrn_heads(q, fl, iv, g, logits, gain, st):
    r = range(len(q))
    lb = [jax.nn.softmax(logits[j], axis=0)[0:1] for j in r]
    f = [lb[j] + (1.0 - lb[j]) * _sigmoid(fl[j]) for j in r]
    lf = [jnp.log(f[j]) for j in r]
    kk = [1.0 - f[j] for j in r]
    qs = [q[j] * _sigmoid(q[j]) for j in r]
    b = [_cumsum_rows(lf[j]) for j in r]
    b_last = [jnp.sum(lf[j], axis=0, keepdims=True) for j in r]
    o = [_dotb(qs[j] * jnp.exp(b[j]), st[j], ((1,), (1,))) for j in r]
    r3 = lax.broadcasted_iota(jnp.int32, (SUB, SUB, HEAD), 0)
    c3 = lax.broadcasted_iota(jnp.int32, (SUB, SUB, HEAD), 1)
    parts = [[] for _ in r]
    for i in range(CHUNK // SUB):
        lo, hi = i * SUB, (i + 1) * SUB
        bi = [b[j][lo:hi] for j in r]
        dec = [jnp.exp(jnp.where(c3 <= r3, bi[j][:, None, :] - bi[j][None, :, :], -jnp.inf)) for j in r]
        s = [jnp.sum(qs[j][lo:hi][:, None, :] * kk[j][lo:hi][None, :, :] * dec[j], axis=-1) for j in r]
        if i > 0:
            anchor = [jnp.max(bi[j], axis=0, keepdims=True) for j in r]
            qa = [qs[j][lo:hi] * jnp.exp(bi[j] - anchor[j]) for j in r]
            kd = [kk[j][:lo] * jnp.exp(anchor[j] - b[j][:lo]) for j in r]
            s = [jnp.concatenate([_dotb(qa[j], kd[j], ((1,), (1,))), s[j]], axis=1) for j in r]
        for j in r:
            parts[j].append(_dotb(s[j], iv[j][:hi], ((1,), (0,))))
    o = [o[j] + jnp.concatenate(parts[j], axis=0) for j in r]
    st_new = [st[j] * jnp.exp(b_last[j]) + _dotb(iv[j], kk[j] * jnp.exp(b_last[j] - b[j]), ((0,), (0,))) for j in r]
    o = [o[j] * lax.rsqrt(jnp.mean(o[j] * o[j], axis=-1, keepdims=True) + EPS) for j in r]
    o = [o[j] * gain[j] * (g[j] * _sigmoid(g[j])) for j in r]
    return o, st_new


def _group(n, pref):
    while n % pref:
        pref //= 2
    return pref


def _hgrn_fwd(proj, logits, gain, n_heads, name):
    T = proj.shape[0]
    nc = T // CHUNK
    H = n_heads
    HB = _group(H, 8)
    W = HB * HEAD

    def col(k):
        return pl.BlockSpec((CHUNK, W), lambda h, c: (c, k * (H // HB) + h))

    def body(q_ref, f_ref, i_ref, g_ref, l_ref, ga_ref, y_ref, s_ref, st):
        @pl.when(pl.program_id(1) == 0)
        def _():
            st[...] = jnp.zeros((HB, HEAD, HEAD), f32)

        cols = [slice(j * HEAD, (j + 1) * HEAD) for j in range(HB)]
        heads = lambda ref: [ref[:, cs] for cs in cols]
        s_ref[...] = st[...]
        o, st_new = _hgrn_heads(heads(q_ref), heads(f_ref), heads(i_ref), heads(g_ref), heads(l_ref), heads(ga_ref), [st[j] for j in range(HB)])
        for j, cs in enumerate(cols):
            y_ref[:, cs] = o[j].astype(bf16)
            st[j] = st_new[j]

    return pl.pallas_call(
        body, name=name, grid=(H // HB, nc),
        in_specs=[col(0), col(1), col(2), col(3), pl.BlockSpec((2, W), lambda h, c: (0, h)), pl.BlockSpec((1, W), lambda h, c: (0, h))],
        out_specs=(pl.BlockSpec((CHUNK, W), lambda h, c: (c, h)), pl.BlockSpec((HB, None, HEAD, HEAD), lambda h, c: (h, c, 0, 0))),
        out_shape=(jax.ShapeDtypeStruct((T, H * HEAD), bf16), jax.ShapeDtypeStruct((H, nc, HEAD, HEAD), f32)),
        scratch_shapes=[pltpu.VMEM((HB, HEAD, HEAD), f32)],
        compiler_params=_params(("parallel", "arbitrary")),
    )(proj, proj, proj, proj, logits, gain)


def _hgrn_bwd(proj, logits, gain, states, dy, n_heads, name):
    T = proj.shape[0]
    nc = T // CHUNK
    H = n_heads
    HB = _group(H, HGRN_BWD_HEADS)
    W = HB * HEAD

    def col(k):
        return pl.BlockSpec((CHUNK, W), lambda h, c: (nc - 1 - c, k * (H // HB) + h))

    out_blk = pl.BlockSpec((CHUNK, W), lambda h, c: (nc - 1 - c, h))

    def body(q_ref, f_ref, i_ref, g_ref, l_ref, ga_ref, s_ref, dy_ref, dq_ref, df_ref, di_ref, dg_ref, dl_ref, dga_ref, dst):
        first = pl.program_id(1) == 0

        @pl.when(first)
        def _():
            dst[...] = jnp.zeros((HB, HEAD, HEAD), f32)
            dl_ref[...] = jnp.zeros((2, W), f32)
            dga_ref[...] = jnp.zeros((1, W), f32)

        cols = [slice(j * HEAD, (j + 1) * HEAD) for j in range(HB)]
        heads = lambda ref: [ref[:, cs] for cs in cols]
        _, vjp = jax.vjp(_hgrn_heads, heads(q_ref), heads(f_ref), heads(i_ref), heads(g_ref), heads(l_ref), heads(ga_ref),
                         [s_ref[j] for j in range(HB)])
        dq, df, di, dg, dl, dga, ds = vjp((heads(dy_ref), [dst[j] for j in range(HB)]))
        for j, cs in enumerate(cols):
            dq_ref[:, cs] = dq[j].astype(bf16)
            df_ref[:, cs] = df[j].astype(bf16)
            di_ref[:, cs] = di[j].astype(bf16)
            dg_ref[:, cs] = dg[j].astype(bf16)
            dst[j] = ds[j]
            dl_ref[:, cs] += dl[j]
            dga_ref[:, cs] += dga[j]

    act = jax.ShapeDtypeStruct((T, H * HEAD), bf16)
    return pl.pallas_call(
        body, name=name, grid=(H // HB, nc),
        in_specs=[col(0), col(1), col(2), col(3), pl.BlockSpec((2, W), lambda h, c: (0, h)), pl.BlockSpec((1, W), lambda h, c: (0, h)),
                  pl.BlockSpec((HB, None, HEAD, HEAD), lambda h, c: (h, nc - 1 - c, 0, 0)), out_blk],
        out_specs=(out_blk, out_blk, out_blk, out_blk, pl.BlockSpec((2, W), lambda h, c: (0, h)), pl.BlockSpec((1, W), lambda h, c: (0, h))),
        out_shape=(act, act, act, act, jax.ShapeDtypeStruct((2, H * HEAD), f32), jax.ShapeDtypeStruct((1, H * HEAD), f32)),
        scratch_shapes=[pltpu.VMEM((HB, HEAD, HEAD), f32)],
        compiler_params=_params(("parallel", "arbitrary")),
    )(proj, proj, proj, proj, logits, gain, states, dy)


def _rel_index():
    t = np.arange(CHUNK)[:, None]
    sp = np.arange(BAND * CHUNK)[None, :]
    dist = (N_PAST - sp // CHUNK) * CHUNK + t - sp % CHUNK
    return (np.clip(dist, -REL_FUTURE, REL_PAST) + REL_FUTURE).reshape(1, -1).astype(np.int32)


def _bias_table(rel_bias_pad, idx, name):
    H = rel_bias_pad.shape[0]
    n = idx.shape[1]
    tc = _tile(n, 4096)

    def body(rb_ref, idx_ref, o_ref):
        onehot = lax.broadcasted_iota(jnp.int32, (N_REL_PAD, tc), 0) == idx_ref[...]
        o_ref[...] = _dot3(rb_ref[...], onehot, ((1,), (0,)), True)

    return pl.pallas_call(
        body, name=name, grid=(n // tc,),
        in_specs=[pl.BlockSpec((H, N_REL_PAD), lambda j: (0, 0)), pl.BlockSpec((1, tc), lambda j: (0, j))],
        out_specs=pl.BlockSpec((H, tc), lambda j: (0, j)), out_shape=jax.ShapeDtypeStruct((H, n), f32),
        compiler_params=_params(("parallel",)),
    )(rel_bias_pad, idx)


def _bias_table_bwd(dbias, idx, name):
    H, n = dbias.shape
    tc = _tile(n, 4096)

    def body(d_ref, idx_ref, o_ref):
        onehot = lax.broadcasted_iota(jnp.int32, (N_REL_PAD, tc), 0) == idx_ref[...]
        part = _dot3(d_ref[...], onehot, ((1,), (1,)), True)

        @pl.when(pl.program_id(0) == 0)
        def _():
            o_ref[...] = part

        @pl.when(pl.program_id(0) > 0)
        def _():
            o_ref[...] += part

    return pl.pallas_call(
        body, name=name, grid=(n // tc,),
        in_specs=[pl.BlockSpec((H, tc), lambda j: (0, j)), pl.BlockSpec((1, tc), lambda j: (0, j))],
        out_specs=pl.BlockSpec((H, N_REL_PAD), lambda j: (0, 0)), out_shape=jax.ShapeDtypeStruct((H, N_REL_PAD), f32),
        compiler_params=_params(("arbitrary",)),
    )(dbias, idx)


def _head_norm(t, gain):
    return t * lax.rsqrt(jnp.mean(t * t, axis=-1, keepdims=True) + EPS) * gain


def _attn_chunks(qs, kbs, vbs, qg, bias, ns):
    r = range(len(qs))
    qh = [_head_norm(qs[j], qg) for j in r]
    s = [_dotb(qh[j], kbs[j], ((1,), (1,))) * (HEAD ** -0.5) + bias for j in r]
    col = lax.broadcasted_iota(jnp.int32, (1, BAND * CHUNK), 1)
    s = [jnp.where(ns[j] * CHUNK - PAD + col >= 0, s[j], NEG) for j in r]
    e = [jnp.exp(s[j] - jnp.max(s[j], axis=-1, keepdims=True)) for j in r]
    p = [e[j] / jnp.sum(e[j], axis=-1, keepdims=True) for j in r]
    return [_dotb(p[j], vbs[j], ((1,), (0,))) for j in r]


def _attn_fwd(proj, q_gain, k_gain, bias, off, n_heads, name):
    T = proj.shape[0]
    nc = T // CHUNK
    H = n_heads
    CB = _group(nc, ATTN_CHUNKS)
    o0 = off // HEAD
    full = lambda k: pl.BlockSpec((T, HEAD), lambda h, c: (0, o0 + k * H + h))
    vec = pl.BlockSpec((1, HEAD), lambda h, c: (0, 0))

    def body(q_ref, k_ref, v_ref, qg_ref, kg_ref, b_ref, y_ref, kp, vp):
        c = pl.program_id(1)

        @pl.when(c == 0)
        def _():
            kp[pl.ds(0, PAD), :] = jnp.zeros((PAD, HEAD), f32)
            vp[pl.ds(0, PAD), :] = jnp.zeros((PAD, HEAD), f32)
            kp[pl.ds(PAD, T), :] = _head_norm(k_ref[...], kg_ref[...])
            vp[pl.ds(PAD, T), :] = v_ref[...]

        ns = [c * CB + j for j in range(CB)]
        rows = [pl.ds(j * CHUNK, CHUNK) for j in range(CB)]
        bands = [pl.ds(pl.multiple_of(n * CHUNK, CHUNK), BAND * CHUNK) for n in ns]
        outs = _attn_chunks([q_ref[r, :] for r in rows], [kp[b, :] for b in bands], [vp[b, :] for b in bands], qg_ref[...], b_ref[...], ns)
        for r, o in zip(rows, outs):
            y_ref[r, :] = o.astype(bf16)

    return pl.pallas_call(
        body, name=name, grid=(H, nc // CB),
        in_specs=[pl.BlockSpec((CB * CHUNK, HEAD), lambda h, c: (c, o0 + h)), full(1), full(2), vec, vec,
                  pl.BlockSpec((None, CHUNK, BAND * CHUNK), lambda h, c: (h, 0, 0))],
        out_specs=pl.BlockSpec((CB * CHUNK, HEAD), lambda h, c: (c, h)), out_shape=jax.ShapeDtypeStruct((T, H * HEAD), bf16),
        scratch_shapes=[pltpu.VMEM((T + PAD, HEAD), f32), pltpu.VMEM((T + PAD, HEAD), f32)],
        compiler_params=_params(("parallel", "arbitrary")),
    )(proj, proj, proj, q_gain, k_gain, bias)


def _attn_bwd(proj, q_gain, k_gain, bias, dy, off, n_heads, name):
    T = proj.shape[0]
    nc = T // CHUNK
    H = n_heads
    CB = _group(nc, ATTN_CHUNKS)
    o0 = off // HEAD
    full = lambda k: pl.BlockSpec((T, HEAD), lambda h, c: (0, o0 + k * H + h))
    full_out = pl.BlockSpec((T, HEAD), lambda h, c: (0, h))
    vec = pl.BlockSpec((1, HEAD), lambda h, c: (0, 0))
    chunk_out = pl.BlockSpec((CB * CHUNK, HEAD), lambda h, c: (c, h))
    bias_blk = pl.BlockSpec((None, CHUNK, BAND * CHUNK), lambda h, c: (h, 0, 0))

    def body(q_ref, k_ref, v_ref, qg_ref, kg_ref, b_ref, dy_ref, dq_ref, dk_ref, dv_ref, db_ref, dqg_ref, dkg_ref, kp, vp, dkp, dvp):
        h = pl.program_id(0)
        c = pl.program_id(1)

        @pl.when(c == 0)
        def _():
            kp[pl.ds(0, PAD), :] = jnp.zeros((PAD, HEAD), f32)
            vp[pl.ds(0, PAD), :] = jnp.zeros((PAD, HEAD), f32)
            kp[pl.ds(PAD, T), :] = _head_norm(k_ref[...], kg_ref[...])
            vp[pl.ds(PAD, T), :] = v_ref[...]
            dkp[...] = jnp.zeros((T + PAD, HEAD), f32)
            dvp[...] = jnp.zeros((T + PAD, HEAD), f32)
            db_ref[...] = jnp.zeros((CHUNK, BAND * CHUNK), f32)

        @pl.when(jnp.logical_and(h == 0, c == 0))
        def _():
            dqg_ref[...] = jnp.zeros((1, HEAD), f32)
            dkg_ref[...] = jnp.zeros((1, HEAD), f32)

        ns = [c * CB + j for j in range(CB)]
        rows = [pl.ds(j * CHUNK, CHUNK) for j in range(CB)]
        bands = [pl.ds(pl.multiple_of(n * CHUNK, CHUNK), BAND * CHUNK) for n in ns]
        _, vjp = jax.vjp(functools.partial(_attn_chunks, ns=ns), [q_ref[r, :] for r in rows], [kp[b, :] for b in bands],
                         [vp[b, :] for b in bands], qg_ref[...], b_ref[...])
        dqs, dkbs, dvbs, dqg, db = vjp([dy_ref[r, :] for r in rows])
        db_ref[...] += db
        dqg_ref[...] += dqg
        for r, b, dq, dkb, dvb in zip(rows, bands, dqs, dkbs, dvbs):
            dq_ref[r, :] = dq.astype(bf16)
            dkp[b, :] += dkb
            dvp[b, :] += dvb

        @pl.when(c == nc // CB - 1)
        def _():
            _, nvjp = jax.vjp(_head_norm, k_ref[...], kg_ref[...])
            dk, dkg = nvjp(dkp[pl.ds(PAD, T), :])
            dk_ref[...] = dk.astype(bf16)
            dv_ref[...] = dvp[pl.ds(PAD, T), :].astype(bf16)
            dkg_ref[...] += dkg

    act = jax.ShapeDtypeStruct((T, H * HEAD), bf16)
    gvec = jax.ShapeDtypeStruct((1, HEAD), f32)
    pad_buf = pltpu.VMEM((T + PAD, HEAD), f32)
    return pl.pallas_call(
        body, name=name, grid=(H, nc // CB),
        in_specs=[pl.BlockSpec((CB * CHUNK, HEAD), lambda h, c: (c, o0 + h)), full(1), full(2), vec, vec, bias_blk, chunk_out],
        out_specs=(chunk_out, full_out, full_out, bias_blk, vec, vec),
        out_shape=(act, act, act, jax.ShapeDtypeStruct((H, CHUNK, BAND * CHUNK), f32), gvec, gvec),
        scratch_shapes=[pad_buf, pad_buf, pad_buf, pad_buf],
        compiler_params=_params(("arbitrary", "arbitrary")),
    )(proj, proj, proj, q_gain, k_gain, bias, dy)


def _position():
    x, y, c = lax.axis_index("x"), lax.axis_index("y"), lax.axis_index("c")
    return x, y, c, 4 * x + 2 * y + c


def _flip(v, bit):
    return 1 - v if bit else v


def _chips(x, y):
    return [(1 - x, y), (x, 1 - y), (1 - x, 1 - y)]


def _seq_gather(shards, name, collective_id):
    n = len(shards)

    def body(*refs):
        ins, outs = refs[:n], refs[n:2 * n]
        send, recv, loc = refs[2 * n:]
        x, y, c, me = _position()
        sib = (x, y, 1 - c)
        sel = lambda a, b: c * a + (1 - c) * b
        n1 = (sel(1 - x, x), sel(y, 1 - y))
        n2 = (sel(x, 1 - x), sel(1 - y, y))
        far = (1 - x, 1 - y)
        idx = lambda chip, core: 4 * chip[0] + 2 * chip[1] + core
        barrier = pltpu.get_barrier_semaphore()
        for peer in [sib, (*n1, c), (*n2, c)]:
            pl.semaphore_signal(barrier, inc=1, device_id=peer, device_id_type=MESH)
        pl.semaphore_wait(barrier, 3)

        def copy(w, k, src, blk, to):
            return pltpu.make_async_remote_copy(src_ref=src, dst_ref=outs[w].at[blk], send_sem=send.at[7 * w + k], recv_sem=recv.at[7 * w + k],
                                                device_id=to, device_id_type=MESH)

        mine = [pltpu.make_async_copy(ins[w], outs[w].at[me], loc.at[w]) for w in range(n)]
        for cp in mine:
            cp.start()
        sent = [copy(w, 1, ins[w], me, (*n1, c)) for w in range(n)] + [copy(w, 2, ins[w], me, (*n2, c)) for w in range(n)]
        sent += [copy(w, 0, ins[w], me, sib) for w in range(n)]
        for cp in sent:
            cp.start()
        for k, chip in ((1, n1), (2, n2), (3, far)):
            blk = idx(chip, c)
            for w in range(n):
                copy(w, k, ins[w], blk, sib).wait_recv()
                if k == 1:
                    sent.append(copy(w, 3, outs[w].at[blk], blk, (*n2, c)))
                    sent[-1].start()
                sent.append(copy(w, 3 + k, outs[w].at[blk], blk, sib))
                sent[-1].start()
        for w in range(n):
            copy(w, 0, ins[w], idx((x, y), 1 - c), sib).wait_recv()
        for k, chip in ((4, n2), (5, n1), (6, far)):
            for w in range(n):
                copy(w, k, ins[w], idx(chip, 1 - c), sib).wait_recv()
        for cp in sent:
            cp.wait_send()
        for cp in mine:
            cp.wait()

    return pl.kernel(
        body, out_type=tuple(jax.ShapeDtypeStruct((NDEV,) + s.shape, s.dtype) for s in shards),
        mesh=plsc.ScalarSubcoreMesh(axis_name="sequencer", num_cores=1), name=name,
        scratch_types=(pltpu.SemaphoreType.DMA((7 * n,)), pltpu.SemaphoreType.DMA((7 * n,)), pltpu.SemaphoreType.DMA((n,))),
        compiler_params=pltpu.CompilerParams(collective_id=collective_id),
    )(*shards)


NCHIP = 4


def _seq_pair_exchange(grads, name, collective_id, after=()):
    n, na = len(grads), len(after)

    def body(*refs):
        ins, outs = refs[:n], refs[n + na:2 * n + na]
        send, recv = refs[2 * n + na:]
        x, y, c, me = _position()
        sib = (x, y, 1 - c)
        barrier = pltpu.get_barrier_semaphore()
        pl.semaphore_signal(barrier, inc=1, device_id=sib, device_id_type=MESH)
        pl.semaphore_wait(barrier, 1)
        copies = [pltpu.make_async_remote_copy(src_ref=ins[w].at[2 * k + (1 - c)], dst_ref=outs[w].at[k], send_sem=send.at[NCHIP * w + k],
                                               recv_sem=recv.at[NCHIP * w + k], device_id=sib, device_id_type=MESH)
                  for w in range(n) for k in range(NCHIP)]
        for cp in copies:
            cp.start()
        for cp in copies:
            cp.wait_recv()
        for cp in copies:
            cp.wait_send()

    return pl.kernel(
        body, out_type=tuple(jax.ShapeDtypeStruct((NCHIP,) + g.shape[1:], g.dtype) for g in grads),
        mesh=plsc.ScalarSubcoreMesh(axis_name="sequencer", num_cores=1), name=name,
        scratch_types=(pltpu.SemaphoreType.DMA((NCHIP * n,)), pltpu.SemaphoreType.DMA((NCHIP * n,))),
        compiler_params=pltpu.CompilerParams(collective_id=collective_id),
    )(*grads, *after)


def _pair_add(grad, sib_part, name, after=()):
    _, R, C = grad.shape
    tr = _tile(R, 1024, 16)
    core = jnp.reshape(lax.axis_index("c"), (1,)).astype(jnp.int32)

    def body(c_ref, g_ref, s_ref, *rest):
        rest[-1][...] = (g_ref[...].astype(f32) + s_ref[...].astype(f32)).astype(bf16)

    blk = pl.BlockSpec((None, tr, C), lambda k, i, c_ref: (k, i, 0))
    return pl.pallas_call(
        body, name=name,
        grid_spec=pltpu.PrefetchScalarGridSpec(
            num_scalar_prefetch=1, grid=(NCHIP, R // tr),
            in_specs=[pl.BlockSpec((None, tr, C), lambda k, i, c_ref: (2 * k + c_ref[0], i, 0)), blk]
            + [pl.BlockSpec(memory_space=pl.ANY)] * len(after), out_specs=blk),
        out_shape=jax.ShapeDtypeStruct((NCHIP, R, C), bf16), compiler_params=_params(("parallel", "parallel")),
    )(core, grad, sib_part, *after)


def _seq_chip_exchange(sums, name, collective_id, after=()):
    n, na = len(sums), len(after)

    def body(*refs):
        ins, outs = refs[:n], refs[n + na:2 * n + na]
        send, recv, loc = refs[2 * n + na:]
        x, y, c, me = _position()
        chips = _chips(x, y)
        mine = 2 * x + y
        barrier = pltpu.get_barrier_semaphore()
        for px, py in chips:
            pl.semaphore_signal(barrier, inc=1, device_id=(px, py, c), device_id_type=MESH)
        pl.semaphore_wait(barrier, 3)
        local = [pltpu.make_async_copy(ins[w].at[mine], outs[w].at[mine], loc.at[w]) for w in range(n)]
        for cp in local:
            cp.start()
        sends, waits = [], []
        for j, (px, py) in enumerate(chips):
            for w in range(n):
                sems = dict(send_sem=send.at[3 * w + j], recv_sem=recv.at[3 * w + j], device_id=(px, py, c), device_id_type=MESH)
                sends.append(pltpu.make_async_remote_copy(src_ref=ins[w].at[2 * px + py], dst_ref=outs[w].at[mine], **sems))
                waits.append(pltpu.make_async_remote_copy(src_ref=ins[w].at[2 * px + py], dst_ref=outs[w].at[2 * px + py], **sems))
        for cp in sends:
            cp.start()
        for cp in waits:
            cp.wait_recv()
        for cp in sends:
            cp.wait_send()
        for cp in local:
            cp.wait()

    return pl.kernel(
        body, out_type=tuple(jax.ShapeDtypeStruct(s.shape, s.dtype) for s in sums),
        mesh=plsc.ScalarSubcoreMesh(axis_name="sequencer", num_cores=1), name=name,
        scratch_types=(pltpu.SemaphoreType.DMA((3 * n,)), pltpu.SemaphoreType.DMA((3 * n,)), pltpu.SemaphoreType.DMA((n,))),
        compiler_params=pltpu.CompilerParams(collective_id=collective_id),
    )(*sums, *after)


def _reduce_scatter(grads, tag, ids, after=(), add_after=()):
    sib_parts = _seq_pair_exchange(grads, "pair_exchange_" + tag, ids[0], after=after)
    sums = [_pair_add(g, s, "pair_add_%s%d" % (tag, i), after=add_after) for i, (g, s) in enumerate(zip(grads, sib_parts))]
    return _seq_chip_exchange(sums, "chip_exchange_" + tag, ids[1]), sums


def _small_all_reduce(v, name):
    R, C = v.shape

    def body(v_ref, o_ref, buf, send, recv):
        x, y, c, me = _position()
        buf[me] = v_ref[...]
        sends, waits = [], []
        for r in range(1, NDEV):
            px, py, pc = _flip(x, r & 4), _flip(y, r & 2), _flip(c, r & 1)
            peer = 4 * px + 2 * py + pc
            sends.append(pltpu.make_async_remote_copy(src_ref=v_ref, dst_ref=buf.at[me], send_sem=send.at[r - 1], recv_sem=recv.at[r - 1],
                                                      device_id=(px, py, pc), device_id_type=MESH))
            waits.append(pltpu.make_async_remote_copy(src_ref=v_ref, dst_ref=buf.at[peer], send_sem=send.at[r - 1], recv_sem=recv.at[r - 1],
                                                      device_id=(px, py, pc), device_id_type=MESH))
        for cp in sends:
            cp.start()
        for cp in waits:
            cp.wait_recv()
        for cp in sends:
            cp.wait_send()
        acc = buf[0]
        for i in range(1, NDEV):
            acc = acc + buf[i]
        o_ref[...] = acc

    vm = pl.BlockSpec(memory_space=pltpu.VMEM)
    return pl.pallas_call(
        body, name=name, in_specs=[vm], out_specs=vm, out_shape=jax.ShapeDtypeStruct((R, C), f32),
        scratch_shapes=[pltpu.VMEM((NDEV, R, C), f32), pltpu.SemaphoreType.DMA((7,)), pltpu.SemaphoreType.DMA((7,))],
    )(v)


def _adamw_math(w, g, m, v):
    m = ADAM_B1 * m + (1.0 - ADAM_B1) * g
    v = ADAM_B2 * v + (1.0 - ADAM_B2) * (g * g)
    m_hat = m / (1.0 - ADAM_B1 ** ADAM_STEP)
    v_hat = v / (1.0 - ADAM_B2 ** ADAM_STEP)
    delta = -ADAM_LR * (m_hat / (jnp.sqrt(v_hat) + ADAM_EPS) + ADAM_WD * w)
    return delta, m, v


def _adamw_parts(w, m, v, parts, name, after=()):
    R, C = w.shape
    tr = _tile(R, 128, 16)
    blk = pl.BlockSpec((tr, C), lambda i: (i, 0))

    def body(w_ref, m_ref, v_ref, p_ref, *rest):
        g_ref, d_ref, mo_ref, vo_ref = rest[len(after):]
        g = p_ref[0].astype(f32)
        for i in range(1, NCHIP):
            g = g + p_ref[i].astype(f32)
        d, mn, vn = _adamw_math(w_ref[...], g, m_ref[...], v_ref[...])
        g_ref[...] = g
        d_ref[...] = d
        mo_ref[...] = mn
        vo_ref[...] = vn

    shp = jax.ShapeDtypeStruct((R, C), f32)
    return pl.pallas_call(
        body, name=name, grid=(R // tr,),
        in_specs=[blk, blk, blk, pl.BlockSpec((NCHIP, tr, C), lambda i: (0, i, 0))] + [pl.BlockSpec(memory_space=pl.ANY)] * len(after),
        out_specs=(blk, blk, blk, blk), out_shape=(shp, shp, shp, shp), compiler_params=_params(("parallel",)),
    )(w, m, v, parts, *after)


def _adamw_small(w, g, m, v, name):
    def body(w_ref, g_ref, m_ref, v_ref, d_ref, mo_ref, vo_ref):
        d, mn, vn = _adamw_math(w_ref[...], g_ref[...], m_ref[...], v_ref[...])
        d_ref[...] = d
        mo_ref[...] = mn
        vo_ref[...] = vn

    shp = jax.ShapeDtypeStruct(w.shape, f32)
    return pl.pallas_call(body, name=name, out_shape=(shp, shp, shp))(w, g, m, v)


SMALL_COLS = 1024


def _pack(arrs):
    flat = jnp.concatenate([a.reshape(-1) for a in arrs])
    rows = -(-flat.shape[0] // (8 * SMALL_COLS)) * 8
    return jnp.pad(flat, (0, rows * SMALL_COLS - flat.shape[0])).reshape(rows, SMALL_COLS)


def _unpack(packed, like):
    flat = packed.reshape(-1)
    out, pos = [], 0
    for a in like:
        out.append(flat[pos:pos + a.size].reshape(a.shape))
        pos += a.size
    return out


def kernel(x, w_in, b_gate, norm_mix, norm_ffn, hgrn_lb_logits, hgrn_out_gain, q_gain, k_gain, rel_bias, w_proj_a, w_proj_b, w_out, w_ffn_in, w_ffn_out, loss_target, m_w_in, m_b_gate, m_norm_mix, m_norm_ffn, m_hgrn_lb_logits, m_hgrn_out_gain, m_q_gain, m_k_gain, m_rel_bias, m_w_proj_a, m_w_proj_b, m_w_out, m_w_ffn_in, m_w_ffn_out, v_w_in, v_b_gate, v_norm_mix, v_norm_ffn, v_hgrn_lb_logits, v_hgrn_out_gain, v_q_gain, v_k_gain, v_rel_bias, v_w_proj_a, v_w_proj_b, v_w_out, v_w_ffn_in, v_w_ffn_out):
    xs = x[0]
    target = loss_target[0]
    T, D = xs.shape
    d_a = hgrn_out_gain.shape[-1]
    H = d_a // HEAD
    d_b = d_a
    off_b = 4 * d_a
    off_g = off_b + 3 * d_b
    assert rel_bias.shape[1] == H and T % CHUNK == 0 and T // CHUNK > N_PAST

    big_w = [w_in[0], w_proj_a[0], w_proj_b[0], w_out[0], w_ffn_in[0], w_ffn_out[0]]
    big_m = [m_w_in[0], m_w_proj_a[0], m_w_proj_b[0], m_w_out[0], m_w_ffn_in[0], m_w_ffn_out[0]]
    big_v = [v_w_in[0], v_w_proj_a[0], v_w_proj_b[0], v_w_out[0], v_w_ffn_in[0], v_w_ffn_out[0]]

    sh = [w.astype(bf16) for w in big_w]
    (g_in,) = _seq_gather(sh[0:1], "gather_a", 1)
    g_pa, g_pb, g_out = _seq_gather(sh[1:4], "gather_b", 2)
    (g_fin,) = _seq_gather(sh[4:5], "gather_c", 3)
    (g_fout,) = _seq_gather(sh[5:6], "gather_d", 4)

    h = _rms_fwd(xs, norm_mix, "rms_mix")
    proj = _mm(h, g_in, mode="nn", b_blocked=True, name="mm_proj")
    y_a, states = _hgrn_fwd(proj, hgrn_lb_logits, hgrn_out_gain, H, "hgrn_fwd")
    idx = jnp.asarray(_rel_index())
    rb_pad = jnp.pad(rel_bias[0], ((0, 0), (0, N_REL_PAD - N_REL)))
    bias = _bias_table(rb_pad, idx, "bias_table").reshape(H, CHUNK, BAND * CHUNK)
    y_b = _attn_fwd(proj, q_gain, k_gain, bias, off_b, H, "attn_fwd")
    wg_out = g_out.reshape(-1, g_out.shape[-1])
    wg_fout = g_fout.reshape(-1, g_fout.shape[-1])
    pa, pb, merged = _proj_merge(y_a, y_b, g_pa, g_pb, proj, b_gate, off_g, "mm_proj_ab")
    x1, h2 = _out_rms(merged, wg_out, xs, norm_ffn, "mm_out")
    gu, act = _ffn_in_swiglu(h2, g_fin, "mm_ffn_in")
    dy, loss_acc = _ffn_out_loss(act, wg_fout, x1, target, "mm_ffn_out")
    loss_part = loss_acc[0:1, 0:1] * (0.5 / D)

    gw_fout = _mm(act, dy, mode="tn", out_dtype=bf16, tm=1408, name="mm_gw_ffn_out")
    dgu = _d_act_swiglu(dy, wg_fout, gu, "mm_d_act")
    gw_fin = _mm(h2, dgu, mode="tn", b_stacked=True, out_blocked=True, out_dtype=bf16, tn=g_fin.shape[-1], name="mm_gw_ffn_in")
    dh2 = _mm_nt_blocked(dgu, g_fin, "mm_d_h2", a_stacked=True)
    (p_fout, p_fin), sums_a = _reduce_scatter([gw_fout.reshape(NDEV, -1, D), gw_fin], "a", (5, 6), add_after=(dh2,))
    dx1, g_norm_ffn = _rms_bwd(x1, norm_ffn, dh2, dy, "rms_ffn_bwd", after=sums_a)

    gw_out = _mm(merged, dx1, mode="tn", out_dtype=bf16, name="mm_gw_out")
    dpa, dpb, dgl, g_b_gate = _d_merged_branches(dx1, wg_out, pa, pb, proj, b_gate, off_g, "mm_d_merged")
    dy_a = _mm_nt_blocked(dpa, g_pa, "mm_d_ya", kb=g_pa.shape[0])
    dy_b = _mm_nt_blocked(dpb, g_pb, "mm_d_yb", kb=g_pb.shape[0])
    gw_pa = _mm(y_a, dpa, mode="tn", out_blocked=True, out_dtype=bf16, tn=g_pa.shape[-1], name="mm_gw_proj_a")
    gw_pb = _mm(y_b, dpb, mode="tn", out_blocked=True, out_dtype=bf16, tn=g_pb.shape[-1], name="mm_gw_proj_b")

    dq_a, df_a, di_a, dg_a, g_logits, g_gain = _hgrn_bwd(proj, hgrn_lb_logits, hgrn_out_gain, states, dy_a, H, "hgrn_bwd")
    dq_b, dk_b, dv_b, dbias, g_qg, g_kg = _attn_bwd(proj, q_gain, k_gain, bias, dy_b, off_b, H, "attn_bwd")
    g_rel_pad = _bias_table_bwd(dbias.reshape(H, -1), idx, "bias_table_bwd")
    g_rel = g_rel_pad[:, :N_REL]
    dproj = jnp.concatenate([dq_a, df_a, di_a, dg_a, dq_b, dk_b, dv_b, dgl[0], dgl[1]], axis=1)
    (p_out, p_pa, p_pb), sums_b = _reduce_scatter([gw_out.reshape(NDEV, -1, D), gw_pa, gw_pb], "b", (7, 8), after=(g_gain, p_fout, p_fin), add_after=(dy_b,))
    gw_in = _mm(h, dproj, mode="tn", out_blocked=True, out_dtype=bf16, tn=g_in.shape[-1], name="mm_gw_in", after=sums_b)
    upd_fout = _adamw_parts(big_w[5], big_m[5], big_v[5], p_fout, "adamw_w_ffn_out", after=(gw_in,))
    (p_in,), sums_c = _reduce_scatter([gw_in], "c", (9, 10), after=(p_out, p_pa, p_pb), add_after=(g_rel_pad, upd_fout[0]))
    dh = _mm_nt_blocked(dproj, g_in, "mm_d_h", after=sums_c)
    grad_x, g_norm_mix = _rms_bwd(xs, norm_mix, dh, dx1, "rms_mix_bwd")

    parts = [p_in, p_pa, p_pb, p_out, p_fin, p_fout]
    names = ["w_in", "w_proj_a", "w_proj_b", "w_out", "w_ffn_in", "w_ffn_out"]
    big = {}
    for nm, w, m, v, p in zip(names, big_w, big_m, big_v, parts):
        upd = upd_fout if nm == "w_ffn_out" else _adamw_parts(w, m, v, p, "adamw_" + nm, after=() if nm == "w_in" else (g_norm_mix,))
        big[nm] = [o[None] for o in upd]

    small_names = ["b_gate", "norm_mix", "norm_ffn", "hgrn_lb_logits", "hgrn_out_gain", "q_gain", "k_gain", "rel_bias"]
    small_w = [b_gate, norm_mix, norm_ffn, hgrn_lb_logits, hgrn_out_gain, q_gain, k_gain, rel_bias]
    small_m = [m_b_gate, m_norm_mix, m_norm_ffn, m_hgrn_lb_logits, m_hgrn_out_gain, m_q_gain, m_k_gain, m_rel_bias]
    small_v = [v_b_gate, v_norm_mix, v_norm_ffn, v_hgrn_lb_logits, v_hgrn_out_gain, v_q_gain, v_k_gain, v_rel_bias]
    small_g = [g_b_gate.reshape(1, -1), g_norm_mix, g_norm_ffn, g_logits, g_gain, g_qg, g_kg, g_rel[None], loss_part]
    g_sum = _small_all_reduce(_pack(small_g), "reduce_small")
    loss = _unpack(g_sum, small_g)[-1].reshape(())
    d_s, m_s, v_s = _adamw_small(_pack(small_w), g_sum, _pack(small_m), _pack(small_v), "adamw_small")
    small = {}
    for nm, g, d, m, v in zip(small_names, _unpack(g_sum, small_w), _unpack(d_s, small_w), _unpack(m_s, small_w), _unpack(v_s, small_w)):
        small[nm] = [g, d, m, v]

    order = ["w_in", "b_gate", "norm_mix", "norm_ffn", "hgrn_lb_logits", "hgrn_out_gain", "q_gain", "k_gain", "rel_bias",
             "w_proj_a", "w_proj_b", "w_out", "w_ffn_in", "w_ffn_out"]
    res = {**big, **small}
    outs = [loss, grad_x[None]]
    for k in range(4):
        outs += [res[nm][k] for nm in order]
    return tuple(outs)
```

```python
import functools

import numpy as np
import jax
import jax.numpy as jnp
from jax import lax
from jax.experimental import pallas as pl
from jax.experimental.pallas import tpu as pltpu
from jax.experimental.pallas import tpu_sc as plsc

f32 = jnp.float32
bf16 = jnp.bfloat16
MESH = pl.DeviceIdType.MESH
AXES = ("x", "y", "c")
NDEV = 8

CHUNK = 64
HEAD = 128
SUB = 8
HGRN_BWD_HEADS = 8
ATTN_CHUNKS = 32
N_PAST = 8
BAND = N_PAST + 1
PAD = N_PAST * CHUNK
REL_FUTURE = CHUNK - 1
REL_PAST = 2 * CHUNK - 1
N_REL = REL_FUTURE + REL_PAST + 1
N_REL_PAD = 256
EPS = 1e-6
NEG = -1e30

ADAM_LR = 0.001
ADAM_B1 = 0.9
ADAM_B2 = 0.999
ADAM_EPS = 1e-08
ADAM_WD = 0.01
ADAM_STEP = 10

VMEM_LIMIT = 56 * 1024 * 1024


def _params(sem=None):
    return pltpu.CompilerParams(dimension_semantics=sem, vmem_limit_bytes=VMEM_LIMIT)


def _tile(n, pref, unit=128):
    if n <= pref:
        return n
    t = (pref // unit) * unit
    while t >= unit:
        if n % t == 0:
            return t
        t -= unit
    return n


_sigmoid = jax.nn.sigmoid


def _mm(a, b, *, mode, name, b_blocked=False, out_blocked=False, out_dtype=f32, tm=1024, tn=1024, tk=2048, after=(),
        b_stacked=False):
    if mode == "tn":
        K, M = a.shape
    else:
        M, K = a.shape
    if b_blocked:
        nb, mid, cb = b.shape
        if mode == "nn":
            assert mid == K
            N, tn = nb * cb, cb
        else:
            assert mode == "nt" and nb * cb == K
            N, tk = mid, cb
    elif b_stacked:
        assert mode == "tn"
        N = 2 * b.shape[2]
    else:
        N = b.shape[1] if mode in ("nn", "tn") else b.shape[0]
    tm = _tile(M, tm)
    tn = tn if (b_blocked and mode == "nn") or out_blocked else _tile(N, tn)
    tk = tk if b_blocked and mode == "nt" else _tile(K, tk)
    assert M % tm == 0 and N % tn == 0 and K % tk == 0
    nk = K // tk
    grid = (M // tm, N // tn, nk)
    if mode == "tn":
        a_spec = pl.BlockSpec((tk, tm), lambda i, j, k: (k, i))
    else:
        a_spec = pl.BlockSpec((tm, tk), lambda i, j, k: (i, k))
    if mode == "nn":
        b_spec = pl.BlockSpec((None, tk, cb), lambda i, j, k: (j, k, 0)) if b_blocked else pl.BlockSpec((tk, tn), lambda i, j, k: (k, j))
    elif mode == "nt":
        b_spec = pl.BlockSpec((None, tn, cb), lambda i, j, k: (k, j, 0)) if b_blocked else pl.BlockSpec((tn, tk), lambda i, j, k: (j, k))
    elif b_stacked:
        nh = N // 2 // tn
        b_spec = pl.BlockSpec((None, tk, tn), lambda i, j, k: (j // nh, k, j % nh))
    else:
        b_spec = pl.BlockSpec((tk, tn), lambda i, j, k: (k, j))
    if out_blocked:
        out_shape = jax.ShapeDtypeStruct((N // tn, M, tn), out_dtype)
        o_spec = pl.BlockSpec((None, tm, tn), lambda i, j, k: (j, i, 0))
    else:
        out_shape = jax.ShapeDtypeStruct((M, N), out_dtype)
        o_spec = pl.BlockSpec((tm, tn), lambda i, j, k: (i, j))
    dims = {"nn": ((1,), (0,)), "nt": ((1,), (1,)), "tn": ((0,), (0,))}[mode]

    def body(a_ref, b_ref, *rest):
        o_ref, acc = rest[len(after)], rest[len(after) + 1:]
        p = lax.dot_general(a_ref[...].astype(bf16), b_ref[...].astype(bf16), (dims, ((), ())), preferred_element_type=f32)
        if nk == 1:
            o_ref[...] = p.astype(out_dtype)
        else:
            acc_ref = acc[0]
            k = pl.program_id(2)

            @pl.when(k == 0)
            def _():
                acc_ref[...] = p

            @pl.when(k > 0)
            def _():
                acc_ref[...] += p

            @pl.when(k == nk - 1)
            def _():
                o_ref[...] = acc_ref[...].astype(out_dtype)

    return pl.pallas_call(
        body, name=name, grid=grid, in_specs=[a_spec, b_spec] + [pl.BlockSpec(memory_space=pl.ANY)] * len(after), out_specs=o_spec,
        out_shape=out_shape, scratch_shapes=[pltpu.VMEM((tm, tn), f32)] if nk > 1 else [],
        compiler_params=_params(("parallel", "parallel", "arbitrary")),
    )(a, b, *after)


def _mm_nt_blocked(a, b, name, *, a_stacked=False, after=(), tm=1024, tn=1024, kb=2):
    nb, N, cb = b.shape
    M = a.shape[1] if a_stacked else a.shape[0]
    tm, tn = _tile(M, tm, 8), _tile(N, tn)
    nk = nb // kb
    per_half = nk // 2

    def body(a_ref, b_ref, *rest):
        o_ref, acc_ref = rest[len(after)], rest[len(after) + 1]
        dn = (((1,), (1,)), ((), ()))
        k = pl.program_id(2)
        p = None
        for q in range(kb):
            d = lax.dot_general(a_ref[:, q * cb:(q + 1) * cb], b_ref[q], dn, preferred_element_type=f32)
            p = d if p is None else p + d
        if nk == 1:
            o_ref[...] = p
            return

        @pl.when(k == 0)
        def _():
            acc_ref[...] = p

        @pl.when(jnp.logical_and(k > 0, k < nk - 1))
        def _():
            acc_ref[...] += p

        @pl.when(k == nk - 1)
        def _():
            o_ref[...] = acc_ref[...] + p

    if a_stacked:
        a_spec = pl.BlockSpec((None, tm, kb * cb), lambda i, j, k: (k // per_half, i, k % per_half))
    else:
        a_spec = pl.BlockSpec((tm, kb * cb), lambda i, j, k: (i, k))
    return pl.pallas_call(
        body, name=name, grid=(M // tm, N // tn, nk),
        in_specs=[a_spec, pl.BlockSpec((kb, tn, cb), lambda i, j, k: (k, j, 0))] + [pl.BlockSpec(memory_space=pl.ANY)] * len(after),
        out_specs=pl.BlockSpec((tm, tn), lambda i, j, k: (i, j)), out_shape=jax.ShapeDtypeStruct((M, N), f32),
        scratch_shapes=[pltpu.VMEM((tm, tn), f32)],
        compiler_params=_params(("parallel", "parallel", "arbitrary")),
    )(a, b, *after)


def _rms_fwd(x, gain, name):
    T, D = x.shape
    tr = _tile(T, 256, 8)
    row = pl.BlockSpec((tr, D), lambda i: (i, 0))

    def body(x_ref, g_ref, h_ref):
        xs = x_ref[...]
        r = lax.rsqrt(jnp.mean(xs * xs, axis=-1, keepdims=True) + EPS)
        h_ref[...] = (xs * r * g_ref[...]).astype(bf16)

    return pl.pallas_call(body, name=name, grid=(T // tr,), in_specs=[row, pl.BlockSpec((1, D), lambda i: (0, 0))], out_specs=row,
                          out_shape=jax.ShapeDtypeStruct((T, D), bf16), compiler_params=_params(("parallel",)))(x, gain)


def _rms_bwd(xs, gain, dh, extra, name, after=()):
    T, D = xs.shape
    tr = _tile(T, 256, 8)
    row = pl.BlockSpec((tr, D), lambda i: (i, 0))
    vec = pl.BlockSpec((1, D), lambda i: (0, 0))

    def body(x_ref, g_ref, dh_ref, e_ref, *rest):
        dx_ref, dg_ref = rest[len(after):]
        x = x_ref[...]
        r = lax.rsqrt(jnp.mean(x * x, axis=-1, keepdims=True) + EPS)
        xhat = x * r
        dh_v = dh_ref[...]
        gd = dh_v * g_ref[...]
        dx_ref[...] = e_ref[...] + r * (gd - xhat * jnp.mean(gd * xhat, axis=-1, keepdims=True))
        part = jnp.sum(dh_v * xhat, axis=0, keepdims=True)

        @pl.when(pl.program_id(0) == 0)
        def _():
            dg_ref[...] = part

        @pl.when(pl.program_id(0) > 0)
        def _():
            dg_ref[...] += part

    return pl.pallas_call(body, name=name, grid=(T // tr,),
                          in_specs=[row, vec, row, row] + [pl.BlockSpec(memory_space=pl.ANY)] * len(after), out_specs=(row, vec),
                          out_shape=(jax.ShapeDtypeStruct((T, D), f32), jax.ShapeDtypeStruct((1, D), f32)),
                          compiler_params=_params(("arbitrary",)))(xs, gain, dh, extra, *after)


def _proj_merge(y_a, y_b, w_a, w_b, proj, b_gate, off, name):
    T, K = y_a.shape
    nb, _, cb = w_a.shape
    D = nb * cb
    tm = _tile(T, 1024, 8)
    cw, nbp = 2 * cb, nb // 2
    oa, ob = off // cw, (off + D) // cw
    blk = pl.BlockSpec((tm, cw), lambda j, i: (i, j))
    row = pl.BlockSpec((tm, K), lambda j, i: (i, 0))
    wsp = pl.BlockSpec((2, K, cb), lambda j, i: (j, 0, 0))

    def body(ya_ref, yb_ref, wa_ref, wb_ref, ga_ref, gb_ref, ba_ref, bb_ref, pa_ref, pb_ref, m_ref):
        dn = (((1,), (0,)), ((), ()))
        ya, yb = ya_ref[...], yb_ref[...]
        pa = jnp.concatenate([lax.dot_general(ya, wa_ref[q], dn, preferred_element_type=f32) for q in range(2)], axis=1)
        pb = jnp.concatenate([lax.dot_general(yb, wb_ref[q], dn, preferred_element_type=f32) for q in range(2)], axis=1)
        pa_ref[...] = pa
        pb_ref[...] = pb
        m_ref[...] = (_sigmoid(ga_ref[...] + ba_ref[...]) * pa + _sigmoid(gb_ref[...] + bb_ref[...]) * pb).astype(bf16)

    return pl.pallas_call(
        body, name=name, grid=(nbp, T // tm),
        in_specs=[row, row, wsp, wsp, pl.BlockSpec((tm, cw), lambda j, i: (i, oa + j)), pl.BlockSpec((tm, cw), lambda j, i: (i, ob + j)),
                  pl.BlockSpec((1, cw), lambda j, i: (0, j)), pl.BlockSpec((1, cw), lambda j, i: (0, nbp + j))],
        out_specs=(blk, blk, blk),
        out_shape=(jax.ShapeDtypeStruct((T, D), f32), jax.ShapeDtypeStruct((T, D), f32), jax.ShapeDtypeStruct((T, D), bf16)),
        compiler_params=_params(("parallel", "parallel")),
    )(y_a, y_b, w_a, w_b, proj, proj, b_gate, b_gate)


def _out_rms(merged, w_out, x, gain, name):
    T, K = merged.shape
    D = w_out.shape[1]
    tm = _tile(T, 256, 8)
    row = pl.BlockSpec((tm, D), lambda i: (i, 0))

    def body(m_ref, w_ref, x_ref, g_ref, x1_ref, h_ref):
        x1 = x_ref[...] + lax.dot_general(m_ref[...], w_ref[...], (((1,), (0,)), ((), ())), preferred_element_type=f32)
        x1_ref[...] = x1
        r = lax.rsqrt(jnp.mean(x1 * x1, axis=-1, keepdims=True) + EPS)
        h_ref[...] = (x1 * r * g_ref[...]).astype(bf16)

    return pl.pallas_call(
        body, name=name, grid=(T // tm,),
        in_specs=[pl.BlockSpec((tm, K), lambda i: (i, 0)), pl.BlockSpec((K, D), lambda i: (0, 0)), row, pl.BlockSpec((1, D), lambda i: (0, 0))],
        out_specs=(row, row), out_shape=(jax.ShapeDtypeStruct((T, D), f32), jax.ShapeDtypeStruct((T, D), bf16)),
        compiler_params=_params(("parallel",)),
    )(merged, w_out, x, gain)


def _d_merged_branches(dx, w_out, pa, pb, proj, b_gate, off, name):
    T, D = pa.shape
    tm, tn = _tile(T, 512, 8), _tile(D, 1024)
    oa, ob, nb = off // tn, (off + D) // tn, D // tn
    blk = pl.BlockSpec((tm, tn), lambda j, i: (i, j))

    def body(dx_ref, w_ref, pa_ref, pb_ref, ga_ref, gb_ref, ba_ref, bb_ref, dpa_ref, dpb_ref, dgl_ref, db_ref):
        dm = lax.dot_general(dx_ref[...].astype(bf16), w_ref[...], (((1,), (1,)), ((), ())), preferred_element_type=f32)
        sums = []
        for p_ref, gl_ref, b_ref, dp_ref, k in ((pa_ref, ga_ref, ba_ref, dpa_ref, 0), (pb_ref, gb_ref, bb_ref, dpb_ref, 1)):
            g = _sigmoid(gl_ref[...] + b_ref[...])
            dp_ref[...] = (dm * g).astype(bf16)
            dgl = dm * p_ref[...] * g * (1.0 - g)
            dgl_ref[k] = dgl.astype(bf16)
            sums.append(jnp.sum(dgl, axis=0, keepdims=True))

        @pl.when(pl.program_id(1) == 0)
        def _():
            db_ref[...] = jnp.zeros((2, 1, tn), f32)

        db_ref[0] += sums[0]
        db_ref[1] += sums[1]

    return pl.pallas_call(
        body, name=name, grid=(nb, T // tm),
        in_specs=[pl.BlockSpec((tm, D), lambda j, i: (i, 0)), pl.BlockSpec((tn, D), lambda j, i: (j, 0)), blk, blk,
                  pl.BlockSpec((tm, tn), lambda j, i: (i, oa + j)), pl.BlockSpec((tm, tn), lambda j, i: (i, ob + j)),
                  pl.BlockSpec((1, tn), lambda j, i: (0, j)), pl.BlockSpec((1, tn), lambda j, i: (0, nb + j))],
        out_specs=(blk, blk, pl.BlockSpec((2, tm, tn), lambda j, i: (0, i, j)), pl.BlockSpec((2, 1, tn), lambda j, i: (0, 0, j))),
        out_shape=(jax.ShapeDtypeStruct((T, D), bf16), jax.ShapeDtypeStruct((T, D), bf16), jax.ShapeDtypeStruct((2, T, D), bf16),
                   jax.ShapeDtypeStruct((2, 1, D), f32)),
        compiler_params=_params(("parallel", "arbitrary")),
    )(dx, w_out, pa, pb, proj, proj, b_gate, b_gate)


def _ffn_in_swiglu(h, w, name):
    T, K = h.shape
    nb, _, cb = w.shape
    half = nb // 2
    F = half * cb
    tm = _tile(T, 512, 8)

    def body(h_ref, wg_ref, wu_ref, gu_ref, act_ref):
        dn = (((1,), (0,)), ((), ()))
        hv = h_ref[...]
        g = lax.dot_general(hv, wg_ref[...], dn, preferred_element_type=f32)
        u = lax.dot_general(hv, wu_ref[...], dn, preferred_element_type=f32)
        gu_ref[0] = g
        gu_ref[1] = u
        act_ref[...] = (g * _sigmoid(g) * u).astype(bf16)

    return pl.pallas_call(
        body, name=name, grid=(half, T // tm),
        in_specs=[pl.BlockSpec((tm, K), lambda j, i: (i, 0)), pl.BlockSpec((None, K, cb), lambda j, i: (j, 0, 0)),
                  pl.BlockSpec((None, K, cb), lambda j, i: (j + half, 0, 0))],
        out_specs=(pl.BlockSpec((2, tm, cb), lambda j, i: (0, i, j)), pl.BlockSpec((tm, cb), lambda j, i: (i, j))),
        out_shape=(jax.ShapeDtypeStruct((2, T, F), f32), jax.ShapeDtypeStruct((T, F), bf16)),
        compiler_params=_params(("parallel", "parallel")),
    )(h, w, w)


def _d_act_swiglu(dy, w_out, gu, name):
    T, D = dy.shape
    F = w_out.shape[0]
    tm, tn = _tile(T, 512, 8), _tile(F, 1408)

    def body(dy_ref, w_ref, gu_ref, o_ref):
        halves = [pl.ds(r * (tm // 2), tm // 2) for r in range(2)]
        wv = w_ref[...]
        d = [lax.dot_general(dy_ref[r, :].astype(bf16), wv, (((1,), (1,)), ((), ())), preferred_element_type=f32) for r in halves]
        for r, dr in zip(halves, d):
            g = gu_ref[0, r, :]
            s = _sigmoid(g)
            o_ref[0, r, :] = (dr * gu_ref[1, r, :] * s * (1.0 + g * (1.0 - s))).astype(bf16)
            o_ref[1, r, :] = (dr * g * s).astype(bf16)

    blk = pl.BlockSpec((2, tm, tn), lambda j, i: (0, i, j))
    return pl.pallas_call(
        body, name=name, grid=(F // tn, T // tm),
        in_specs=[pl.BlockSpec((tm, D), lambda j, i: (i, 0)), pl.BlockSpec((tn, D), lambda j, i: (j, 0)), blk],
        out_specs=blk, out_shape=jax.ShapeDtypeStruct((2, T, F), bf16), compiler_params=_params(("parallel", "parallel")),
    )(dy, w_out, gu)


def _ffn_out_loss(act, w, x1, target, name):
    T, F = act.shape
    D = w.shape[1]
    tm, tn, tk = _tile(T, 512, 8), _tile(D, 1024), _tile(F, 2816)
    nk = F // tk
    blk = pl.BlockSpec((tm, tn), lambda i, j, k: (i, j))

    def body(a_ref, w_ref, x_ref, t_ref, dy_ref, l_ref, acc_ref):
        i, j, k = pl.program_id(0), pl.program_id(1), pl.program_id(2)
        p = lax.dot_general(a_ref[...], w_ref[...], (((1,), (0,)), ((), ())), preferred_element_type=f32)

        @pl.when(jnp.logical_and(jnp.logical_and(i == 0, j == 0), k == 0))
        def _():
            l_ref[...] = jnp.zeros((8, 128), f32)

        @pl.when(k == 0)
        def _():
            acc_ref[...] = p

        @pl.when(k > 0)
        def _():
            acc_ref[...] += p

        @pl.when(k == nk - 1)
        def _():
            d = acc_ref[...] + x_ref[...] - t_ref[...]
            dy_ref[...] = d * (1.0 / D)
            l_ref[...] += jnp.sum(jnp.sum(d * d, axis=1, keepdims=True), axis=0, keepdims=True)

    return pl.pallas_call(
        body, name=name, grid=(T // tm, D // tn, nk),
        in_specs=[pl.BlockSpec((tm, tk), lambda i, j, k: (i, k)), pl.BlockSpec((tk, tn), lambda i, j, k: (k, j)), blk, blk],
        out_specs=(blk, pl.BlockSpec((8, 128), lambda i, j, k: (0, 0))),
        out_shape=(jax.ShapeDtypeStruct((T, D), f32), jax.ShapeDtypeStruct((8, 128), f32)),
        scratch_shapes=[pltpu.VMEM((tm, tn), f32)],
        compiler_params=_params(("arbitrary", "arbitrary", "arbitrary")),
    )(act, w, x1, target)


_DIMS = {"nn": ((1,), (0,)), "nt": ((1,), (1,)), "tn": ((0,), (0,))}
_MODE = {v: k for k, v in _DIMS.items()}


def _dot_bf16(a, b, mode):
    return lax.dot_general(a.astype(bf16), b.astype(bf16), (_DIMS[mode], ((), ())), preferred_element_type=f32)


@functools.partial(jax.custom_vjp, nondiff_argnums=(2,))
def _dotm(a, b, mode):
    return _dot_bf16(a, b, mode)


def _dotm_fwd(a, b, mode):
    return _dot_bf16(a, b, mode), (a, b)


def _dotm_bwd(mode, res, g):
    a, b = res
    if mode == "nn":
        return _dot_bf16(g, b, "nt"), _dot_bf16(a, g, "tn")
    if mode == "nt":
        return _dot_bf16(g, b, "nn"), _dot_bf16(g, a, "tn")
    return _dot_bf16(b, g, "nt"), _dot_bf16(a, g, "nn")


_dotm.defvjp(_dotm_fwd, _dotm_bwd)


def _dotb(a, b, dims):
    return _dotm(a, b, _MODE[dims])


def _split3(v):
    def top(t):
        return lax.bitcast_convert_type(lax.bitcast_convert_type(t, jnp.uint32) & jnp.uint32(0xFFFF0000), f32)

    hi = top(v)
    mid = top(v - hi)
    low = (v - hi) - mid
    return hi.astype(bf16), mid.astype(bf16), low.astype(bf16)


def _dot3(v, m, dims, v_first):
    m = m.astype(bf16)
    dn = (dims, ((), ()))
    parts = [lax.dot_general(p, m, dn, preferred_element_type=f32) if v_first else lax.dot_general(m, p, dn, preferred_element_type=f32)
             for p in _split3(v)]
    return parts[0] + parts[1] + parts[2]


def _triangle_sum(v, lower):
    row = lax.broadcasted_iota(jnp.int32, (CHUNK, CHUNK), 0)
    col = lax.broadcasted_iota(jnp.int32, (CHUNK, CHUNK), 1)
    return _dot3(v, (col <= row) if lower else (col >= row), ((1,), (0,)), False)


@jax.custom_vjp
def _cumsum_rows(v):
    return _triangle_sum(v, True)


_cumsum_rows.defvjp(lambda v: (_triangle_sum(v, True), None), lambda _, g: (_triangle_sum(g, False),))


def _hgrn_heads(q, fl, iv, g, logits, gain, st):
    r = range(len(q))
    lb = [jax.nn.softmax(logits[j], axis=0)[0:1] for j in r]
    f = [lb[j] + (1.0 - lb[j]) * _sigmoid(fl[j]) for j in r]
    lf = [jnp.log(f[j]) for j in r]
    kk = [1.0 - f[j] for j in r]
    qs = [q[j] * _sigmoid(q[j]) for j in r]
    b = [_cumsum_rows(lf[j]) for j in r]
    b_last = [jnp.sum(lf[j], axis=0, keepdims=True) for j in r]
    o = [_dotb(qs[j] * jnp.exp(b[j]), st[j], ((1,), (1,))) for j in r]
    r3 = lax.broadcasted_iota(jnp.int32, (SUB, SUB, HEAD), 0)
    c3 = lax.broadcasted_iota(jnp.int32, (SUB, SUB, HEAD), 1)
    parts = [[] for _ in r]
    for i in range(CHUNK // SUB):
        lo, hi = i * SUB, (i + 1) * SUB
        bi = [b[j][lo:hi] for j in r]
        dec = [jnp.exp(jnp.where(c3 <= r3, bi[j][:, None, :] - bi[j][None, :, :], -jnp.inf)) for j in r]
        s = [jnp.sum(qs[j][lo:hi][:, None, :] * kk[j][lo:hi][None, :, :] * dec[j], axis=-1) for j in r]
        if i > 0:
            anchor = [jnp.max(bi[j], axis=0, keepdims=True) for j in r]
            qa = [qs[j][lo:hi] * jnp.exp(bi[j] - anchor[j]) for j in r]
            kd = [kk[j][:lo] * jnp.exp(anchor[j] - b[j][:lo]) for j in r]
            s = [jnp.concatenate([_dotb(qa[j], kd[j], ((1,), (1,))), s[j]], axis=1) for j in r]
        for j in r:
            parts[j].append(_dotb(s[j], iv[j][:hi], ((1,), (0,))))
    o = [o[j] + jnp.concatenate(parts[j], axis=0) for j in r]
    st_new = [st[j] * jnp.exp(b_last[j]) + _dotb(iv[j], kk[j] * jnp.exp(b_last[j] - b[j]), ((0,), (0,))) for j in r]
    o = [o[j] * lax.rsqrt(jnp.mean(o[j] * o[j], axis=-1, keepdims=True) + EPS) for j in r]
    o = [o[j] * gain[j] * (g[j] * _sigmoid(g[j])) for j in r]
    return o, st_new


def _group(n, pref):
    while n % pref:
        pref //= 2
    return pref


def _hgrn_fwd(proj, logits, gain, n_heads, name):
    T = proj.shape[0]
    nc = T // CHUNK
    H = n_heads
    HB = _group(H, 8)
    W = HB * HEAD

    def col(k):
        return pl.BlockSpec((CHUNK, W), lambda h, c: (c, k * (H // HB) + h))

    def body(q_ref, f_ref, i_ref, g_ref, l_ref, ga_ref, y_ref, s_ref, st):
        @pl.when(pl.program_id(1) == 0)
        def _():
            st[...] = jnp.zeros((HB, HEAD, HEAD), f32)

        cols = [slice(j * HEAD, (j + 1) * HEAD) for j in range(HB)]
        heads = lambda ref: [ref[:, cs] for cs in cols]
        s_ref[...] = st[...]
        o, st_new = _hgrn_heads(heads(q_ref), heads(f_ref), heads(i_ref), heads(g_ref), heads(l_ref), heads(ga_ref), [st[j] for j in range(HB)])
        for j, cs in enumerate(cols):
            y_ref[:, cs] = o[j].astype(bf16)
            st[j] = st_new[j]

    return pl.pallas_call(
        body, name=name, grid=(H // HB, nc),
        in_specs=[col(0), col(1), col(2), col(3), pl.BlockSpec((2, W), lambda h, c: (0, h)), pl.BlockSpec((1, W), lambda h, c: (0, h))],
        out_specs=(pl.BlockSpec((CHUNK, W), lambda h, c: (c, h)), pl.BlockSpec((HB, None, HEAD, HEAD), lambda h, c: (h, c, 0, 0))),
        out_shape=(jax.ShapeDtypeStruct((T, H * HEAD), bf16), jax.ShapeDtypeStruct((H, nc, HEAD, HEAD), f32)),
        scratch_shapes=[pltpu.VMEM((HB, HEAD, HEAD), f32)],
        compiler_params=_params(("parallel", "arbitrary")),
    )(proj, proj, proj, proj, logits, gain)


def _hgrn_bwd(proj, logits, gain, states, dy, n_heads, name):
    T = proj.shape[0]
    nc = T // CHUNK
    H = n_heads
    HB = _group(H, HGRN_BWD_HEADS)
    W = HB * HEAD

    def col(k):
        return pl.BlockSpec((CHUNK, W), lambda h, c: (nc - 1 - c, k * (H // HB) + h))

    out_blk = pl.BlockSpec((CHUNK, W), lambda h, c: (nc - 1 - c, h))

    def body(q_ref, f_ref, i_ref, g_ref, l_ref, ga_ref, s_ref, dy_ref, dq_ref, df_ref, di_ref, dg_ref, dl_ref, dga_ref, dst):
        first = pl.program_id(1) == 0

        @pl.when(first)
        def _():
            dst[...] = jnp.zeros((HB, HEAD, HEAD), f32)
            dl_ref[...] = jnp.zeros((2, W), f32)
            dga_ref[...] = jnp.zeros((1, W), f32)

        cols = [slice(j * HEAD, (j + 1) * HEAD) for j in range(HB)]
        heads = lambda ref: [ref[:, cs] for cs in cols]
        _, vjp = jax.vjp(_hgrn_heads, heads(q_ref), heads(f_ref), heads(i_ref), heads(g_ref), heads(l_ref), heads(ga_ref),
                         [s_ref[j] for j in range(HB)])
        dq, df, di, dg, dl, dga, ds = vjp((heads(dy_ref), [dst[j] for j in range(HB)]))
        for j, cs in enumerate(cols):
            dq_ref[:, cs] = dq[j].astype(bf16)
            df_ref[:, cs] = df[j].astype(bf16)
            di_ref[:, cs] = di[j].astype(bf16)
            dg_ref[:, cs] = dg[j].astype(bf16)
            dst[j] = ds[j]
            dl_ref[:, cs] += dl[j]
            dga_ref[:, cs] += dga[j]

    act = jax.ShapeDtypeStruct((T, H * HEAD), bf16)
    return pl.pallas_call(
        body, name=name, grid=(H // HB, nc),
        in_specs=[col(0), col(1), col(2), col(3), pl.BlockSpec((2, W), lambda h, c: (0, h)), pl.BlockSpec((1, W), lambda h, c: (0, h)),
                  pl.BlockSpec((HB, None, HEAD, HEAD), lambda h, c: (h, nc - 1 - c, 0, 0)), out_blk],
        out_specs=(out_blk, out_blk, out_blk, out_blk, pl.BlockSpec((2, W), lambda h, c: (0, h)), pl.BlockSpec((1, W), lambda h, c: (0, h))),
        out_shape=(act, act, act, act, jax.ShapeDtypeStruct((2, H * HEAD), f32), jax.ShapeDtypeStruct((1, H * HEAD), f32)),
        scratch_shapes=[pltpu.VMEM((HB, HEAD, HEAD), f32)],
        compiler_params=_params(("parallel", "arbitrary")),
    )(proj, proj, proj, proj, logits, gain, states, dy)


def _rel_index():
    t = np.arange(CHUNK)[:, None]
    sp = np.arange(BAND * CHUNK)[None, :]
    dist = (N_PAST - sp // CHUNK) * CHUNK + t - sp % CHUNK
    return (np.clip(dist, -REL_FUTURE, REL_PAST) + REL_FUTURE).reshape(1, -1).astype(np.int32)


def _bias_table(rel_bias_pad, idx, name):
    H = rel_bias_pad.shape[0]
    n = idx.shape[1]
    tc = _tile(n, 4096)

    def body(rb_ref, idx_ref, o_ref):
        onehot = lax.broadcasted_iota(jnp.int32, (N_REL_PAD, tc), 0) == idx_ref[...]
        o_ref[...] = _dot3(rb_ref[...], onehot, ((1,), (0,)), True)

    return pl.pallas_call(
        body, name=name, grid=(n // tc,),
        in_specs=[pl.BlockSpec((H, N_REL_PAD), lambda j: (0, 0)), pl.BlockSpec((1, tc), lambda j: (0, j))],
        out_specs=pl.BlockSpec((H, tc), lambda j: (0, j)), out_shape=jax.ShapeDtypeStruct((H, n), f32),
        compiler_params=_params(("parallel",)),
    )(rel_bias_pad, idx)


def _bias_table_bwd(dbias, idx, name):
    H, n = dbias.shape
    tc = _tile(n, 4096)

    def body(d_ref, idx_ref, o_ref):
        onehot = lax.broadcasted_iota(jnp.int32, (N_REL_PAD, tc), 0) == idx_ref[...]
        part = _dot3(d_ref[...], onehot, ((1,), (1,)), True)

        @pl.when(pl.program_id(0) == 0)
        def _():
            o_ref[...] = part

        @pl.when(pl.program_id(0) > 0)
        def _():
            o_ref[...] += part

    return pl.pallas_call(
        body, name=name, grid=(n // tc,),
        in_specs=[pl.BlockSpec((H, tc), lambda j: (0, j)), pl.BlockSpec((1, tc), lambda j: (0, j))],
        out_specs=pl.BlockSpec((H, N_REL_PAD), lambda j: (0, 0)), out_shape=jax.ShapeDtypeStruct((H, N_REL_PAD), f32),
        compiler_params=_params(("arbitrary",)),
    )(dbias, idx)


def _head_norm(t, gain):
    return t * lax.rsqrt(jnp.mean(t * t, axis=-1, keepdims=True) + EPS) * gain


def _attn_chunks(qs, kbs, vbs, qg, bias, ns):
    r = range(len(qs))
    qh = [_head_norm(qs[j], qg) for j in r]
    s = [_dotb(qh[j], kbs[j], ((1,), (1,))) * (HEAD ** -0.5) + bias for j in r]
    col = lax.broadcasted_iota(jnp.int32, (1, BAND * CHUNK), 1)
    s = [jnp.where(ns[j] * CHUNK - PAD + col >= 0, s[j], NEG) for j in r]
    e = [jnp.exp(s[j] - jnp.max(s[j], axis=-1, keepdims=True)) for j in r]
    p = [e[j] / jnp.sum(e[j], axis=-1, keepdims=True) for j in r]
    return [_dotb(p[j], vbs[j], ((1,), (0,))) for j in r]


def _attn_fwd(proj, q_gain, k_gain, bias, off, n_heads, name):
    T = proj.shape[0]
    nc = T // CHUNK
    H = n_heads
    CB = _group(nc, ATTN_CHUNKS)
    o0 = off // HEAD
    full = lambda k: pl.BlockSpec((T, HEAD), lambda h, c: (0, o0 + k * H + h))
    vec = pl.BlockSpec((1, HEAD), lambda h, c: (0, 0))

    def body(q_ref, k_ref, v_ref, qg_ref, kg_ref, b_ref, y_ref, kp, vp):
        c = pl.program_id(1)

        @pl.when(c == 0)
        def _():
            kp[pl.ds(0, PAD), :] = jnp.zeros((PAD, HEAD), f32)
            vp[pl.ds(0, PAD), :] = jnp.zeros((PAD, HEAD), f32)
            kp[pl.ds(PAD, T), :] = _head_norm(k_ref[...], kg_ref[...])
            vp[pl.ds(PAD, T), :] = v_ref[...]

        ns = [c * CB + j for j in range(CB)]
        rows = [pl.ds(j * CHUNK, CHUNK) for j in range(CB)]
        bands = [pl.ds(pl.multiple_of(n * CHUNK, CHUNK), BAND * CHUNK) for n in ns]
        outs = _attn_chunks([q_ref[r, :] for r in rows], [kp[b, :] for b in bands], [vp[b, :] for b in bands], qg_ref[...], b_ref[...], ns)
        for r, o in zip(rows, outs):
            y_ref[r, :] = o.astype(bf16)

    return pl.pallas_call(
        body, name=name, grid=(H, nc // CB),
        in_specs=[pl.BlockSpec((CB * CHUNK, HEAD), lambda h, c: (c, o0 + h)), full(1), full(2), vec, vec,
                  pl.BlockSpec((None, CHUNK, BAND * CHUNK), lambda h, c: (h, 0, 0))],
        out_specs=pl.BlockSpec((CB * CHUNK, HEAD), lambda h, c: (c, h)), out_shape=jax.ShapeDtypeStruct((T, H * HEAD), bf16),
        scratch_shapes=[pltpu.VMEM((T + PAD, HEAD), f32), pltpu.VMEM((T + PAD, HEAD), f32)],
        compiler_params=_params(("parallel", "arbitrary")),
    )(proj, proj, proj, q_gain, k_gain, bias)


def _attn_bwd(proj, q_gain, k_gain, bias, dy, off, n_heads, name):
    T = proj.shape[0]
    nc = T // CHUNK
    H = n_heads
    CB = _group(nc, ATTN_CHUNKS)
    o0 = off // HEAD
    full = lambda k: pl.BlockSpec((T, HEAD), lambda h, c: (0, o0 + k * H + h))
    full_out = pl.BlockSpec((T, HEAD), lambda h, c: (0, h))
    vec = pl.BlockSpec((1, HEAD), lambda h, c: (0, 0))
    chunk_out = pl.BlockSpec((CB * CHUNK, HEAD), lambda h, c: (c, h))
    bias_blk = pl.BlockSpec((None, CHUNK, BAND * CHUNK), lambda h, c: (h, 0, 0))

    def body(q_ref, k_ref, v_ref, qg_ref, kg_ref, b_ref, dy_ref, dq_ref, dk_ref, dv_ref, db_ref, dqg_ref, dkg_ref, kp, vp, dkp, dvp):
        h = pl.program_id(0)
        c = pl.program_id(1)

        @pl.when(c == 0)
        def _():
            kp[pl.ds(0, PAD), :] = jnp.zeros((PAD, HEAD), f32)
            vp[pl.ds(0, PAD), :] = jnp.zeros((PAD, HEAD), f32)
            kp[pl.ds(PAD, T), :] = _head_norm(k_ref[...], kg_ref[...])
            vp[pl.ds(PAD, T), :] = v_ref[...]
            dkp[...] = jnp.zeros((T + PAD, HEAD), f32)
            dvp[...] = jnp.zeros((T + PAD, HEAD), f32)
            db_ref[...] = jnp.zeros((CHUNK, BAND * CHUNK), f32)

        @pl.when(jnp.logical_and(h == 0, c == 0))
        def _():
            dqg_ref[...] = jnp.zeros((1, HEAD), f32)
            dkg_ref[...] = jnp.zeros((1, HEAD), f32)

        ns = [c * CB + j for j in range(CB)]
        rows = [pl.ds(j * CHUNK, CHUNK) for j in range(CB)]
        bands = [pl.ds(pl.multiple_of(n * CHUNK, CHUNK), BAND * CHUNK) for n in ns]
        _, vjp = jax.vjp(functools.partial(_attn_chunks, ns=ns), [q_ref[r, :] for r in rows], [kp[b, :] for b in bands],
                         [vp[b, :] for b in bands], qg_ref[...], b_ref[...])
        dqs, dkbs, dvbs, dqg, db = vjp([dy_ref[r, :] for r in rows])
        db_ref[...] += db
        dqg_ref[...] += dqg
        for r, b, dq, dkb, dvb in zip(rows, bands, dqs, dkbs, dvbs):
            dq_ref[r, :] = dq.astype(bf16)
            dkp[b, :] += dkb
            dvp[b, :] += dvb

        @pl.when(c == nc // CB - 1)
        def _():
            _, nvjp = jax.vjp(_head_norm, k_ref[...], kg_ref[...])
            dk, dkg = nvjp(dkp[pl.ds(PAD, T), :])
            dk_ref[...] = dk.astype(bf16)
            dv_ref[...] = dvp[pl.ds(PAD, T), :].astype(bf16)
            dkg_ref[...] += dkg

    act = jax.ShapeDtypeStruct((T, H * HEAD), bf16)
    gvec = jax.ShapeDtypeStruct((1, HEAD), f32)
    pad_buf = pltpu.VMEM((T + PAD, HEAD), f32)
    return pl.pallas_call(
        body, name=name, grid=(H, nc // CB),
        in_specs=[pl.BlockSpec((CB * CHUNK, HEAD), lambda h, c: (c, o0 + h)), full(1), full(2), vec, vec, bias_blk, chunk_out],
        out_specs=(chunk_out, full_out, full_out, bias_blk, vec, vec),
        out_shape=(act, act, act, jax.ShapeDtypeStruct((H, CHUNK, BAND * CHUNK), f32), gvec, gvec),
        scratch_shapes=[pad_buf, pad_buf, pad_buf, pad_buf],
        compiler_params=_params(("arbitrary", "arbitrary")),
    )(proj, proj, proj, q_gain, k_gain, bias, dy)


def _position():
    x, y, c = lax.axis_index("x"), lax.axis_index("y"), lax.axis_index("c")
    return x, y, c, 4 * x + 2 * y + c


def _flip(v, bit):
    return 1 - v if bit else v


def _chips(x, y):
    return [(1 - x, y), (x, 1 - y), (1 - x, 1 - y)]


def _seq_gather(shards, name, collective_id):
    n = len(shards)

    def body(*refs):
        ins, outs = refs[:n], refs[n:2 * n]
        send, recv, loc = refs[2 * n:]
        x, y, c, me = _position()
        sib = (x, y, 1 - c)
        sel = lambda a, b: c * a + (1 - c) * b
        n1 = (sel(1 - x, x), sel(y, 1 - y))
        n2 = (sel(x, 1 - x), sel(1 - y, y))
        far = (1 - x, 1 - y)
        idx = lambda chip, core: 4 * chip[0] + 2 * chip[1] + core
        barrier = pltpu.get_barrier_semaphore()
        for peer in [sib, (*n1, c), (*n2, c)]:
            pl.semaphore_signal(barrier, inc=1, device_id=peer, device_id_type=MESH)
        pl.semaphore_wait(barrier, 3)

        def copy(w, k, src, blk, to):
            return pltpu.make_async_remote_copy(src_ref=src, dst_ref=outs[w].at[blk], send_sem=send.at[7 * w + k], recv_sem=recv.at[7 * w + k],
                                                device_id=to, device_id_type=MESH)

        mine = [pltpu.make_async_copy(ins[w], outs[w].at[me], loc.at[w]) for w in range(n)]
        for cp in mine:
            cp.start()
        sent = [copy(w, 1, ins[w], me, (*n1, c)) for w in range(n)] + [copy(w, 2, ins[w], me, (*n2, c)) for w in range(n)]
        sent += [copy(w, 0, ins[w], me, sib) for w in range(n)]
        for cp in sent:
            cp.start()
        for k, chip in ((1, n1), (2, n2), (3, far)):
            blk = idx(chip, c)
            for w in range(n):
                copy(w, k, ins[w], blk, sib).wait_recv()
                if k == 1:
                    sent.append(copy(w, 3, outs[w].at[blk], blk, (*n2, c)))
                    sent[-1].start()
                sent.append(copy(w, 3 + k, outs[w].at[blk], blk, sib))
                sent[-1].start()
        for w in range(n):
            copy(w, 0, ins[w], idx((x, y), 1 - c), sib).wait_recv()
        for k, chip in ((4, n2), (5, n1), (6, far)):
            for w in range(n):
                copy(w, k, ins[w], idx(chip, 1 - c), sib).wait_recv()
        for cp in sent:
            cp.wait_send()
        for cp in mine:
            cp.wait()

    return pl.kernel(
        body, out_type=tuple(jax.ShapeDtypeStruct((NDEV,) + s.shape, s.dtype) for s in shards),
        mesh=plsc.ScalarSubcoreMesh(axis_name="sequencer", num_cores=1), name=name,
        scratch_types=(pltpu.SemaphoreType.DMA((7 * n,)), pltpu.SemaphoreType.DMA((7 * n,)), pltpu.SemaphoreType.DMA((n,))),
        compiler_params=pltpu.CompilerParams(collective_id=collective_id),
    )(*shards)


NCHIP = 4


def _seq_pair_exchange(grads, name, collective_id, after=()):
    n, na = len(grads), len(after)

    def body(*refs):
        ins, outs = refs[:n], refs[n + na:2 * n + na]
        send, recv = refs[2 * n + na:]
        x, y, c, me = _position()
        sib = (x, y, 1 - c)
        barrier = pltpu.get_barrier_semaphore()
        pl.semaphore_signal(barrier, inc=1, device_id=sib, device_id_type=MESH)
        pl.semaphore_wait(barrier, 1)
        copies = [pltpu.make_async_remote_copy(src_ref=ins[w].at[2 * k + (1 - c)], dst_ref=outs[w].at[k], send_sem=send.at[NCHIP * w + k],
                                               recv_sem=recv.at[NCHIP * w + k], device_id=sib, device_id_type=MESH)
                  for w in range(n) for k in range(NCHIP)]
        for cp in copies:
            cp.start()
        for cp in copies:
            cp.wait_recv()
        for cp in copies:
            cp.wait_send()

    return pl.kernel(
        body, out_type=tuple(jax.ShapeDtypeStruct((NCHIP,) + g.shape[1:], g.dtype) for g in grads),
        mesh=plsc.ScalarSubcoreMesh(axis_name="sequencer", num_cores=1), name=name,
        scratch_types=(pltpu.SemaphoreType.DMA((NCHIP * n,)), pltpu.SemaphoreType.DMA((NCHIP * n,))),
        compiler_params=pltpu.CompilerParams(collective_id=collective_id),
    )(*grads, *after)


def _pair_add(grad, sib_part, name, after=()):
    _, R, C = grad.shape
    tr = _tile(R, 1024, 16)
    core = jnp.reshape(lax.axis_index("c"), (1,)).astype(jnp.int32)

    def body(c_ref, g_ref, s_ref, *rest):
        rest[-1][...] = (g_ref[...].astype(f32) + s_ref[...].astype(f32)).astype(bf16)

    blk = pl.BlockSpec((None, tr, C), lambda k, i, c_ref: (k, i, 0))
    return pl.pallas_call(
        body, name=name,
        grid_spec=pltpu.PrefetchScalarGridSpec(
            num_scalar_prefetch=1, grid=(NCHIP, R // tr),
            in_specs=[pl.BlockSpec((None, tr, C), lambda k, i, c_ref: (2 * k + c_ref[0], i, 0)), blk]
            + [pl.BlockSpec(memory_space=pl.ANY)] * len(after), out_specs=blk),
        out_shape=jax.ShapeDtypeStruct((NCHIP, R, C), bf16), compiler_params=_params(("parallel", "parallel")),
    )(core, grad, sib_part, *after)


def _seq_chip_exchange(sums, name, collective_id, after=()):
    n, na = len(sums), len(after)

    def body(*refs):
        ins, outs = refs[:n], refs[n + na:2 * n + na]
        send, recv, loc = refs[2 * n + na:]
        x, y, c, me = _position()
        chips = _chips(x, y)
        mine = 2 * x + y
        barrier = pltpu.get_barrier_semaphore()
        for px, py in chips:
            pl.semaphore_signal(barrier, inc=1, device_id=(px, py, c), device_id_type=MESH)
        pl.semaphore_wait(barrier, 3)
        local = [pltpu.make_async_copy(ins[w].at[mine], outs[w].at[mine], loc.at[w]) for w in range(n)]
        for cp in local:
            cp.start()
        sends, waits = [], []
        for j, (px, py) in enumerate(chips):
            for w in range(n):
                sems = dict(send_sem=send.at[3 * w + j], recv_sem=recv.at[3 * w + j], device_id=(px, py, c), device_id_type=MESH)
                sends.append(pltpu.make_async_remote_copy(src_ref=ins[w].at[2 * px + py], dst_ref=outs[w].at[mine], **sems))
                waits.append(pltpu.make_async_remote_copy(src_ref=ins[w].at[2 * px + py], dst_ref=outs[w].at[2 * px + py], **sems))
        for cp in sends:
            cp.start()
        for cp in waits:
            cp.wait_recv()
        for cp in sends:
            cp.wait_send()
        for cp in local:
            cp.wait()

    return pl.kernel(
        body, out_type=tuple(jax.ShapeDtypeStruct(s.shape, s.dtype) for s in sums),
        mesh=plsc.ScalarSubcoreMesh(axis_name="sequencer", num_cores=1), name=name,
        scratch_types=(pltpu.SemaphoreType.DMA((3 * n,)), pltpu.SemaphoreType.DMA((3 * n,)), pltpu.SemaphoreType.DMA((n,))),
        compiler_params=pltpu.CompilerParams(collective_id=collective_id),
    )(*sums, *after)


def _reduce_scatter(grads, tag, ids, after=(), add_after=()):
    sib_parts = _seq_pair_exchange(grads, "pair_exchange_" + tag, ids[0], after=after)
    sums = [_pair_add(g, s, "pair_add_%s%d" % (tag, i), after=add_after) for i, (g, s) in enumerate(zip(grads, sib_parts))]
    return _seq_chip_exchange(sums, "chip_exchange_" + tag, ids[1]), sums


def _small_all_reduce(v, name):
    R, C = v.shape

    def body(v_ref, o_ref, buf, send, recv):
        x, y, c, me = _position()
        buf[me] = v_ref[...]
        sends, waits = [], []
        for r in range(1, NDEV):
            px, py, pc = _flip(x, r & 4), _flip(y, r & 2), _flip(c, r & 1)
            peer = 4 * px + 2 * py + pc
            sends.append(pltpu.make_async_remote_copy(src_ref=v_ref, dst_ref=buf.at[me], send_sem=send.at[r - 1], recv_sem=recv.at[r - 1],
                                                      device_id=(px, py, pc), device_id_type=MESH))
            waits.append(pltpu.make_async_remote_copy(src_ref=v_ref, dst_ref=buf.at[peer], send_sem=send.at[r - 1], recv_sem=recv.at[r - 1],
                                                      device_id=(px, py, pc), device_id_type=MESH))
        for cp in sends:
            cp.start()
        for cp in waits:
            cp.wait_recv()
        for cp in sends:
            cp.wait_send()
        acc = buf[0]
        for i in range(1, NDEV):
            acc = acc + buf[i]
        o_ref[...] = acc

    vm = pl.BlockSpec(memory_space=pltpu.VMEM)
    return pl.pallas_call(
        body, name=name, in_specs=[vm], out_specs=vm, out_shape=jax.ShapeDtypeStruct((R, C), f32),
        scratch_shapes=[pltpu.VMEM((NDEV, R, C), f32), pltpu.SemaphoreType.DMA((7,)), pltpu.SemaphoreType.DMA((7,))],
    )(v)


def _adamw_math(w, g, m, v):
    m = ADAM_B1 * m + (1.0 - ADAM_B1) * g
    v = ADAM_B2 * v + (1.0 - ADAM_B2) * (g * g)
    m_hat = m / (1.0 - ADAM_B1 ** ADAM_STEP)
    v_hat = v / (1.0 - ADAM_B2 ** ADAM_STEP)
    delta = -ADAM_LR * (m_hat / (jnp.sqrt(v_hat) + ADAM_EPS) + ADAM_WD * w)
    return delta, m, v


def _adamw_parts(w, m, v, parts, name, after=()):
    R, C = w.shape
    tr = _tile(R, 128, 16)
    blk = pl.BlockSpec((tr, C), lambda i: (i, 0))

    def body(w_ref, m_ref, v_ref, p_ref, *rest):
        g_ref, d_ref, mo_ref, vo_ref = rest[len(after):]
        g = p_ref[0].astype(f32)
        for i in range(1, NCHIP):
            g = g + p_ref[i].astype(f32)
        d, mn, vn = _adamw_math(w_ref[...], g, m_ref[...], v_ref[...])
        g_ref[...] = g
        d_ref[...] = d
        mo_ref[...] = mn
        vo_ref[...] = vn

    shp = jax.ShapeDtypeStruct((R, C), f32)
    return pl.pallas_call(
        body, name=name, grid=(R // tr,),
        in_specs=[blk, blk, blk, pl.BlockSpec((NCHIP, tr, C), lambda i: (0, i, 0))] + [pl.BlockSpec(memory_space=pl.ANY)] * len(after),
        out_specs=(blk, blk, blk, blk), out_shape=(shp, shp, shp, shp), compiler_params=_params(("parallel",)),
    )(w, m, v, parts, *after)


def _adamw_small(w, g, m, v, name):
    def body(w_ref, g_ref, m_ref, v_ref, d_ref, mo_ref, vo_ref):
        d, mn, vn = _adamw_math(w_ref[...], g_ref[...], m_ref[...], v_ref[...])
        d_ref[...] = d
        mo_ref[...] = mn
        vo_ref[...] = vn

    shp = jax.ShapeDtypeStruct(w.shape, f32)
    return pl.pallas_call(body, name=name, out_shape=(shp, shp, shp))(w, g, m, v)


SMALL_COLS = 1024


def _pack(arrs):
    flat = jnp.concatenate([a.reshape(-1) for a in arrs])
    rows = -(-flat.shape[0] // (8 * SMALL_COLS)) * 8
    return jnp.pad(flat, (0, rows * SMALL_COLS - flat.shape[0])).reshape(rows, SMALL_COLS)


def _unpack(packed, like):
    flat = packed.reshape(-1)
    out, pos = [], 0
    for a in like:
        out.append(flat[pos:pos + a.size].reshape(a.shape))
        pos += a.size
    return out


def kernel(x, w_in, b_gate, norm_mix, norm_ffn, hgrn_lb_logits, hgrn_out_gain, q_gain, k_gain, rel_bias, w_proj_a, w_proj_b, w_out, w_ffn_in, w_ffn_out, loss_target, m_w_in, m_b_gate, m_norm_mix, m_norm_ffn, m_hgrn_lb_logits, m_hgrn_out_gain, m_q_gain, m_k_gain, m_rel_bias, m_w_proj_a, m_w_proj_b, m_w_out, m_w_ffn_in, m_w_ffn_out, v_w_in, v_b_gate, v_norm_mix, v_norm_ffn, v_hgrn_lb_logits, v_hgrn_out_gain, v_q_gain, v_k_gain, v_rel_bias, v_w_proj_a, v_w_proj_b, v_w_out, v_w_ffn_in, v_w_ffn_out):
    xs = x[0]
    target = loss_target[0]
    T, D = xs.shape
    d_a = hgrn_out_gain.shape[-1]
    H = d_a // HEAD
    d_b = d_a
    off_b = 4 * d_a
    off_g = off_b + 3 * d_b
    assert rel_bias.shape[1] == H and T % CHUNK == 0 and T // CHUNK > N_PAST

    big_w = [w_in[0], w_proj_a[0], w_proj_b[0], w_out[0], w_ffn_in[0], w_ffn_out[0]]
    big_m = [m_w_in[0], m_w_proj_a[0], m_w_proj_b[0], m_w_out[0], m_w_ffn_in[0], m_w_ffn_out[0]]
    big_v = [v_w_in[0], v_w_proj_a[0], v_w_proj_b[0], v_w_out[0], v_w_ffn_in[0], v_w_ffn_out[0]]

    sh = [w.astype(bf16) for w in big_w]
    (g_in,) = _seq_gather(sh[0:1], "gather_a", 1)
    g_pa, g_pb, g_out = _seq_gather(sh[1:4], "gather_b", 2)
    (g_fin,) = _seq_gather(sh[4:5], "gather_c", 3)
    (g_fout,) = _seq_gather(sh[5:6], "gather_d", 4)

    h = _rms_fwd(xs, norm_mix, "rms_mix")
    proj = _mm(h, g_in, mode="nn", b_blocked=True, name="mm_proj")
    y_a, states = _hgrn_fwd(proj, hgrn_lb_logits, hgrn_out_gain, H, "hgrn_fwd")
    idx = jnp.asarray(_rel_index())
    rb_pad = jnp.pad(rel_bias[0], ((0, 0), (0, N_REL_PAD - N_REL)))
    bias = _bias_table(rb_pad, idx, "bias_table").reshape(H, CHUNK, BAND * CHUNK)
    y_b = _attn_fwd(proj, q_gain, k_gain, bias, off_b, H, "attn_fwd")
    wg_out = g_out.reshape(-1, g_out.shape[-1])
    wg_fout = g_fout.reshape(-1, g_fout.shape[-1])
    pa, pb, merged = _proj_merge(y_a, y_b, g_pa, g_pb, proj, b_gate, off_g, "mm_proj_ab")
    x1, h2 = _out_rms(merged, wg_out, xs, norm_ffn, "mm_out")
    gu, act = _ffn_in_swiglu(h2, g_fin, "mm_ffn_in")
    dy, loss_acc = _ffn_out_loss(act, wg_fout, x1, target, "mm_ffn_out")
    loss_part = loss_acc[0:1, 0:1] * (0.5 / D)

    gw_fout = _mm(act, dy, mode="tn", out_dtype=bf16, tm=1408, name="mm_gw_ffn_out")
    dgu = _d_act_swiglu(dy, wg_fout, gu, "mm_d_act")
    gw_fin = _mm(h2, dgu, mode="tn", b_stacked=True, out_blocked=True, out_dtype=bf16, tn=g_fin.shape[-1], name="mm_gw_ffn_in")
    dh2 = _mm_nt_blocked(dgu, g_fin, "mm_d_h2", a_stacked=True)
    (p_fout, p_fin), sums_a = _reduce_scatter([gw_fout.reshape(NDEV, -1, D), gw_fin], "a", (5, 6), add_after=(dh2,))
    dx1, g_norm_ffn = _rms_bwd(x1, norm_ffn, dh2, dy, "rms_ffn_bwd", after=sums_a)

    gw_out = _mm(merged, dx1, mode="tn", out_dtype=bf16, name="mm_gw_out")
    dpa, dpb, dgl, g_b_gate = _d_merged_branches(dx1, wg_out, pa, pb, proj, b_gate, off_g, "mm_d_merged")
    dy_a = _mm_nt_blocked(dpa, g_pa, "mm_d_ya", kb=g_pa.shape[0])
    dy_b = _mm_nt_blocked(dpb, g_pb, "mm_d_yb", kb=g_pb.shape[0])
    gw_pa = _mm(y_a, dpa, mode="tn", out_blocked=True, out_dtype=bf16, tn=g_pa.shape[-1], name="mm_gw_proj_a")
    gw_pb = _mm(y_b, dpb, mode="tn", out_blocked=True, out_dtype=bf16, tn=g_pb.shape[-1], name="mm_gw_proj_b")

    dq_a, df_a, di_a, dg_a, g_logits, g_gain = _hgrn_bwd(proj, hgrn_lb_logits, hgrn_out_gain, states, dy_a, H, "hgrn_bwd")
    dq_b, dk_b, dv_b, dbias, g_qg, g_kg = _attn_bwd(proj, q_gain, k_gain, bias, dy_b, off_b, H, "attn_bwd")
    g_rel_pad = _bias_table_bwd(dbias.reshape(H, -1), idx, "bias_table_bwd")
    g_rel = g_rel_pad[:, :N_REL]
    dproj = jnp.concatenate([dq_a, df_a, di_a, dg_a, dq_b, dk_b, dv_b, dgl[0], dgl[1]], axis=1)
    (p_out, p_pa, p_pb), sums_b = _reduce_scatter([gw_out.reshape(NDEV, -1, D), gw_pa, gw_pb], "b", (7, 8), after=(g_gain, p_fout, p_fin), add_after=(dy_b,))
    gw_in = _mm(h, dproj, mode="tn", out_blocked=True, out_dtype=bf16, tn=g_in.shape[-1], name="mm_gw_in", after=sums_b)
    upd_fout = _adamw_parts(big_w[5], big_m[5], big_v[5], p_fout, "adamw_w_ffn_out", after=(gw_in,))
    (p_in,), sums_c = _reduce_scatter([gw_in], "c", (9, 10), after=(p_out, p_pa, p_pb), add_after=(g_rel_pad, upd_fout[0]))
    dh = _mm_nt_blocked(dproj, g_in, "mm_d_h", after=sums_c)
    grad_x, g_norm_mix = _rms_bwd(xs, norm_mix, dh, dx1, "rms_mix_bwd")

    parts = [p_in, p_pa, p_pb, p_out, p_fin, p_fout]
    names = ["w_in", "w_proj_a", "w_proj_b", "w_out", "w_ffn_in", "w_ffn_out"]
    big = {}
    for nm, w, m, v, p in zip(names, big_w, big_m, big_v, parts):
        upd = upd_fout if nm == "w_ffn_out" else _adamw_parts(w, m, v, p, "adamw_" + nm, after=() if nm == "w_in" else (g_norm_mix,))
        big[nm] = [o[None] for o in upd]

    small_names = ["b_gate", "norm_mix", "norm_ffn", "hgrn_lb_logits", "hgrn_out_gain", "q_gain", "k_gain", "rel_bias"]
    small_w = [b_gate, norm_mix, norm_ffn, hgrn_lb_logits, hgrn_out_gain, q_gain, k_gain, rel_bias]
    small_m = [m_b_gate, m_norm_mix, m_norm_ffn, m_hgrn_lb_logits, m_hgrn_out_gain, m_q_gain, m_k_gain, m_rel_bias]
    small_v = [v_b_gate, v_norm_mix, v_norm_ffn, v_hgrn_lb_logits, v_hgrn_out_gain, v_q_gain, v_k_gain, v_rel_bias]
    small_g = [g_b_gate.reshape(1, -1), g_norm_mix, g_norm_ffn, g_logits, g_gain, g_qg, g_kg, g_rel[None], loss_part]
    g_sum = _small_all_reduce(_pack(small_g), "reduce_small")
    loss = _unpack(g_sum, small_g)[-1].reshape(())
    d_s, m_s, v_s = _adamw_small(_pack(small_w), g_sum, _pack(small_m), _pack(small_v), "adamw_small")
    small = {}
    for nm, g, d, m, v in zip(small_names, _unpack(g_sum, small_w), _unpack(d_s, small_w), _unpack(m_s, small_w), _unpack(v_s, small_w)):
        small[nm] = [g, d, m, v]

    order = ["w_in", "b_gate", "norm_mix", "norm_ffn", "hgrn_lb_logits", "hgrn_out_gain", "q_gain", "k_gain", "rel_bias",
             "w_proj_a", "w_proj_b", "w_out", "w_ffn_in", "w_ffn_out"]
    res = {**big, **small}
    outs = [loss, grad_x[None]]
    for k in range(4):
        outs += [res[nm][k] for nm in order]
    return tuple(outs)
```
